```python
import jax, jax.numpy as jnp
from jax import lax
import numpy as np

D_MODEL = 1024
BATCH = 2
SEQ = 8192
DEPTH = 2

CHUNK = 64
N_MEM = 256
EPS = 1e-6

A_HEADS = 8
A_HEAD_DIM = 64
A_WIDTH = A_HEADS * A_HEAD_DIM
A_LEFT_CHUNKS = 8
A_BAND = (A_LEFT_CHUNKS + 1) * CHUNK
A_MAX_REL = 256

B_HEADS = 4
B_KEY_DIM = 64
B_VAL_DIM = 128
B_KEY_WIDTH = B_HEADS * B_KEY_DIM
B_VAL_WIDTH = B_HEADS * B_VAL_DIM
B_GATE_RANK = 16
B_GATE_TAU = 16.0

N_BRANCH = 2
BRANCH_WIDTH = 512
IN_SPLITS = (A_WIDTH, A_WIDTH, A_WIDTH, B_KEY_WIDTH, B_KEY_WIDTH, B_VAL_WIDTH, B_GATE_RANK, B_VAL_WIDTH, N_BRANCH * D_MODEL)
IN_WIDTH = sum(IN_SPLITS)

X_HEADS = 4
X_HEAD_DIM = D_MODEL // X_HEADS

N_GROUPS = 4
EXPERTS_PER_GROUP = 8
EXPERT_FF = 256
EXPERT_TOP_K = 2

kernel_name = "hybrid_chunk_band_gla_hier_moe_trunk"


def rmsnorm(x, g):
    xf = x.astype(jnp.float32)
    y = xf * lax.rsqrt(jnp.mean(xf * xf, axis=-1, keepdims=True) + EPS)
    return (y * g.astype(jnp.float32)).astype(x.dtype)


def chunk_band_attention(q, k, v, rel_table):
    b, L, h, dh = q.shape
    nc = L // CHUNK
    q = q.reshape(b, nc, CHUNK, h, dh)
    pad = ((0, 0), (A_LEFT_CHUNKS, 0), (0, 0), (0, 0), (0, 0))
    kp = jnp.pad(k.reshape(b, nc, CHUNK, h, dh), pad)
    vp = jnp.pad(v.reshape(b, nc, CHUNK, h, dh), pad)
    kb = jnp.concatenate([kp[:, j:j + nc] for j in range(A_LEFT_CHUNKS + 1)], axis=2)
    vb = jnp.concatenate([vp[:, j:j + nc] for j in range(A_LEFT_CHUNKS + 1)], axis=2)
    q_in_band = A_LEFT_CHUNKS * CHUNK + jnp.arange(CHUNK)[:, None]
    rel = q_in_band - jnp.arange(A_BAND)[None, :]
    idx = jnp.clip(rel, -A_MAX_REL, A_MAX_REL) + A_MAX_REL
    bias = rel_table.astype(jnp.float32)[:, idx]
    key_chunk = jnp.arange(nc)[:, None] + (jnp.arange(A_BAND) // CHUNK)[None, :] - A_LEFT_CHUNKS
    valid = key_chunk >= 0
    s = jnp.einsum('bnqhd,bnkhd->bhnqk', q, kb).astype(jnp.float32) * (dh ** -0.5) + bias[:, None]
    s = jnp.where(valid[None, None, :, None, :], s, -jnp.inf)
    p = jax.nn.softmax(s, axis=-1).astype(v.dtype)
    o = jnp.einsum('bhnqk,bnkhd->bnqhd', p, vb)
    return o.reshape(b, L, h * dh)


def gated_linear_attention(q, k, v, log_a):
    b, L, h, dk = q.shape
    dv = v.shape[-1]
    nc = L // CHUNK

    def to_chunks(t):
        return t.reshape(b, nc, CHUNK, h, t.shape[-1]).transpose(1, 0, 3, 2, 4)

    tri = jnp.tril(jnp.ones((CHUNK, CHUNK), dtype=bool))

    def step(S, inp):
        qc, kc, vc, gc = inp
        cum = jnp.cumsum(gc, axis=2)
        inter = jnp.einsum('bhtk,bhkv->bhtv', qc * jnp.exp(cum), S)
        diff = cum[:, :, :, None, :] - cum[:, :, None, :, :]
        decay = jnp.exp(jnp.where(tri[:, :, None], diff, -jnp.inf))
        attn = jnp.einsum('bhtk,bhsk,bhtsk->bhts', qc, kc, decay)
        intra = jnp.einsum('bhts,bhsv->bhtv', attn, vc)
        last = cum[:, :, -1:, :]
        S = jnp.exp(last[:, :, 0, :])[..., None] * S + jnp.einsum('bhsk,bhsv->bhkv', kc * jnp.exp(last - cum), vc)
        return S, inter + intra

    S0 = jnp.zeros((b, h, dk, dv), jnp.float32)
    _, o = lax.scan(step, S0, (to_chunks(q), to_chunks(k), to_chunks(v), to_chunks(log_a)))
    return o.transpose(1, 0, 3, 2, 4).reshape(b, L, h, dv)


def hybrid_mixer(hn, w_in, rel_bias, gla_w_alpha, gla_b_alpha, gla_norm_g, w_branch, w_mix_out):
    b, L, d = hn.shape
    offsets = [int(o) for o in np.cumsum(IN_SPLITS)[:-1]]
    proj = hn @ w_in
    aq, ak, av, bq, bk, bv, b_alpha, b_r, gates = jnp.split(proj, offsets, axis=-1)
    o_a = chunk_band_attention(aq.reshape(b, L, A_HEADS, A_HEAD_DIM), ak.reshape(b, L, A_HEADS, A_HEAD_DIM),
                               av.reshape(b, L, A_HEADS, A_HEAD_DIM), rel_bias)
    log_a = jax.nn.log_sigmoid((b_alpha @ gla_w_alpha + gla_b_alpha).astype(jnp.float32)) / B_GATE_TAU
    o_b = gated_linear_attention(
        bq.reshape(b, L, B_HEADS, B_KEY_DIM).astype(jnp.float32) * (B_KEY_DIM ** -0.5),
        bk.reshape(b, L, B_HEADS, B_KEY_DIM).astype(jnp.float32),
        bv.reshape(b, L, B_HEADS, B_VAL_DIM).astype(jnp.float32),
        log_a.reshape(b, L, B_HEADS, B_KEY_DIM))
    o_b = rmsnorm(o_b, gla_norm_g) * jax.nn.silu(b_r.reshape(b, L, B_HEADS, B_VAL_DIM).astype(jnp.float32))
    o_b = o_b.reshape(b, L, B_VAL_WIDTH).astype(hn.dtype)
    g_a, g_b = jnp.split(jax.nn.sigmoid(gates), N_BRANCH, axis=-1)
    merged = g_a * (o_a @ w_branch[0]) + g_b * (o_b @ w_branch[1])
    return merged @ w_mix_out


def cross_attention(hn, mem_n, w_q, w_kv, w_o):
    b, L, d = hn.shape
    m = mem_n.shape[1]
    q = (hn @ w_q).reshape(b, L, X_HEADS, X_HEAD_DIM)
    k, v = jnp.split(mem_n @ w_kv, 2, axis=-1)
    k = k.reshape(b, m, X_HEADS, X_HEAD_DIM)
    v = v.reshape(b, m, X_HEADS, X_HEAD_DIM)
    s = jnp.einsum('blhd,bmhd->bhlm', q, k).astype(jnp.float32) * (X_HEAD_DIM ** -0.5)
    p = jax.nn.softmax(s, axis=-1).astype(v.dtype)
    o = jnp.einsum('bhlm,bmhd->blhd', p, v).reshape(b, L, d)
    return o @ w_o


def hierarchical_moe(hn, w_gr, b_gr, w_er, b_er, w_gate, w_up, w_down):
    b, L, d = hn.shape
    t = hn.reshape(-1, d)
    g_logits = (t @ w_gr).astype(jnp.float32) + b_gr.astype(jnp.float32)
    g_idx = jnp.argmax(g_logits, axis=-1)
    g_w = jnp.take_along_axis(jax.nn.softmax(g_logits, axis=-1), g_idx[:, None], axis=-1)
    e_logits = ((t @ w_er).astype(jnp.float32) + b_er.astype(jnp.float32)).reshape(-1, N_GROUPS, EXPERTS_PER_GROUP)
    e_sel = jnp.take_along_axis(e_logits, g_idx[:, None, None], axis=1)[:, 0]
    top_v, top_i = lax.top_k(e_sel, EXPERT_TOP_K)
    top_w = jax.nn.softmax(top_v, axis=-1) * g_w
    in_group = jnp.sum(jax.nn.one_hot(top_i, EXPERTS_PER_GROUP, dtype=jnp.float32) * top_w[..., None], axis=1)
    gate = (jax.nn.one_hot(g_idx, N_GROUPS, dtype=jnp.float32)[:, :, None] * in_group[:, None, :]).astype(t.dtype)
    y = jnp.zeros_like(t)
    for g in range(N_GROUPS):
        hid = jax.nn.silu(jnp.einsum('td,edf->tef', t, w_gate[g])) * jnp.einsum('td,edf->tef', t, w_up[g])
        y = y + jnp.einsum('tef,efd->td', hid * gate[:, g, :, None], w_down[g])
    return y.reshape(b, L, d)


def setup_inputs(seed: int = 0) -> dict:
    key = jax.random.key(seed)
    ks = iter(jax.random.split(key, 32))

    def nrm(shape, scale):
        return jax.random.normal(next(ks), shape, jnp.float32) * scale

    def gain(shape):
        return 1.0 + nrm(shape, 0.02)

    D, G, E, F = D_MODEL, N_GROUPS, EXPERTS_PER_GROUP, EXPERT_FF
    return {
        "x": nrm((BATCH, SEQ, D), 1.0),
        "mem": nrm((BATCH, N_MEM, D), 1.0),
        "norm_mix_g": gain((DEPTH, D)),
        "w_in": nrm((DEPTH, D, IN_WIDTH), D ** -0.5),
        "rel_bias": nrm((DEPTH, A_HEADS, 2 * A_MAX_REL + 1), 0.1),
        "gla_w_alpha": nrm((DEPTH, B_GATE_RANK, B_KEY_WIDTH), B_GATE_RANK ** -0.5),
        "gla_b_alpha": nrm((DEPTH, B_KEY_WIDTH), 0.1),
        "gla_norm_g": gain((DEPTH, B_VAL_DIM)),
        "w_branch": nrm((DEPTH, N_BRANCH, BRANCH_WIDTH, D), BRANCH_WIDTH ** -0.5),
        "w_mix_out": nrm((DEPTH, D, D), D ** -0.5),
        "norm_x_g": gain((DEPTH, D)),
        "mem_norm_g": gain((D,)),
        "w_xq": nrm((DEPTH, D, D), D ** -0.5),
        "w_xkv": nrm((DEPTH, D, 2 * D), D ** -0.5),
        "w_xo": nrm((DEPTH, D, D), D ** -0.5),
        "norm_ffn_g": gain((DEPTH, D)),
        "w_group_router": nrm((DEPTH, D, G), D ** -0.5),
        "b_group_router": nrm((DEPTH, G), 0.01),
        "w_expert_router": nrm((DEPTH, D, G * E), D ** -0.5),
        "b_expert_router": nrm((DEPTH, G * E), 0.01),
        "w_exp_gate": nrm((DEPTH, G, E, D, F), D ** -0.5),
        "w_exp_up": nrm((DEPTH, G, E, D, F), D ** -0.5),
        "w_exp_down": nrm((DEPTH, G, E, F, D), F ** -0.5),
        "final_norm_g": gain((D,)),
    }


def reference(x, mem, norm_mix_g, w_in, rel_bias, gla_w_alpha, gla_b_alpha, gla_norm_g, w_branch, w_mix_out,
              norm_x_g, mem_norm_g, w_xq, w_xkv, w_xo, norm_ffn_g, w_group_router, b_group_router,
              w_expert_router, b_expert_router, w_exp_gate, w_exp_up, w_exp_down, final_norm_g):
    mem_n = rmsnorm(mem, mem_norm_g)
    for l in range(DEPTH):
        h = rmsnorm(x, norm_mix_g[l])
        x = x + hybrid_mixer(h, w_in[l], rel_bias[l], gla_w_alpha[l], gla_b_alpha[l], gla_norm_g[l],
                             w_branch[l], w_mix_out[l])
        h = rmsnorm(x, norm_x_g[l])
        x = x + cross_attention(h, mem_n, w_xq[l], w_xkv[l], w_xo[l])
        h = rmsnorm(x, norm_ffn_g[l])
        x = x + hierarchical_moe(h, w_group_router[l], b_group_router[l], w_expert_router[l], b_expert_router[l],
                                 w_exp_gate[l], w_exp_up[l], w_exp_down[l])
    return rmsnorm(x, final_norm_g)
```

```python
import functools

import numpy as np
import jax
import jax.numpy as jnp
from jax import lax
from jax.experimental import pallas as pl
from jax.experimental.pallas import tpu as pltpu

F32 = jnp.float32
BF16 = jnp.bfloat16
I32 = jnp.int32

D_MODEL = 1024
CHUNK = 64
EPS = 1e-6
A_HEADS = 8
A_HEAD_DIM = 64
A_WIDTH = 512
A_LEFT_CHUNKS = 8
A_MAX_REL = 256
B_HEADS = 4
B_KEY_DIM = 64
B_VAL_DIM = 128
B_KEY_WIDTH = 256
B_VAL_WIDTH = 512
B_GATE_RANK = 16
B_GATE_TAU = 16.0
X_HEADS = 4
X_HEAD_DIM = 256
N_GROUPS = 4
EXPERTS_PER_GROUP = 8
N_EXPERTS = N_GROUPS * EXPERTS_PER_GROUP
EXPERT_FF = 256

LANES = 128
ROW_TILE = 256
CHUNKS_PER_TILE = ROW_TILE // CHUNK
BAND_TILES = A_LEFT_CHUNKS // CHUNKS_PER_TILE + 1
BAND_KEYS = BAND_TILES * ROW_TILE
LOG_CHUNK = 6
N_LEVELS = LOG_CHUNK
EXP_ROWS = (2 + N_LEVELS) * CHUNK
NEG = -1e30
VMEM_LIMIT = 56 * 1024 * 1024


def _params(*sem):
    return pltpu.CompilerParams(dimension_semantics=sem, vmem_limit_bytes=VMEM_LIMIT)


def _rms(x, g):
    return x * lax.rsqrt(jnp.mean(x * x, axis=-1, keepdims=True) + EPS) * g


def _dot(a, b):
    return jnp.dot(a, b, preferred_element_type=F32)


def _dot_nt(a, b):
    return lax.dot_general(a, b, (((1,), (1,)), ((), ())), preferred_element_type=F32)


def _dot_tn(a, b):
    return lax.dot_general(a, b, (((0,), (0,)), ((), ())), preferred_element_type=F32)


def _memkv_kernel(mem_ref, g_ref, w_ref, k_ref, v_ref):
    mn = _rms(mem_ref[0], g_ref[...]).astype(BF16)
    kv = _dot(mn, w_ref[0])
    k_ref[0, 0] = kv[:, :D_MODEL].astype(BF16)
    v_ref[0, 0] = kv[:, D_MODEL:].astype(BF16)


def _memkv(mem, g, w_xkv):
    depth = w_xkv.shape[0]
    b, m, d = mem.shape
    out = jax.ShapeDtypeStruct((depth, b, m, d), BF16)
    return pl.pallas_call(
        _memkv_kernel,
        grid=(depth, b),
        in_specs=[pl.BlockSpec((1, m, d), lambda l, i: (i, 0, 0)),
                  pl.BlockSpec((1, d), lambda l, i: (0, 0)),
                  pl.BlockSpec((1, d, 2 * d), lambda l, i: (l, 0, 0))],
        out_specs=[pl.BlockSpec((1, 1, m, d), lambda l, i: (l, i, 0, 0)),
                   pl.BlockSpec((1, 1, m, d), lambda l, i: (l, i, 0, 0))],
        out_shape=[out, out],
        compiler_params=_params("arbitrary", "arbitrary"),
        name="memkv",
    )(mem, g.reshape(1, d), w_xkv)


_C_AQ, _C_AK, _C_AV = 0, 512, 1024
_C_BQ, _C_BK, _C_BV = 1536, 1792, 2048
_C_BR, _C_GATE, _C_END = 2560, 3072, 5120


def _inproj_kernel(x_ref, g_ref, w_ref, wal_ref, wal2_ref, bal_ref,
                   aq_ref, ak_ref, av_ref, bq_ref, bk_ref, bv_ref, lga_ref, br_ref, gate_ref):
    h = _rms(x_ref[...], g_ref[...]).astype(BF16)

    def mm(lo, hi):
        return _dot(h, w_ref[:, lo:hi])

    aq_ref[...] = (mm(_C_AQ, _C_AK) * (A_HEAD_DIM ** -0.5)).astype(BF16)
    ak_ref[...] = mm(_C_AK, _C_AV).astype(BF16)
    av_ref[...] = mm(_C_AV, _C_BQ).astype(BF16)
    bq_ref[...] = (mm(_C_BQ, _C_BK) * (B_KEY_DIM ** -0.5)).astype(BF16)
    bk_ref[...] = mm(_C_BK, _C_BV).astype(BF16)
    bv_ref[...] = mm(_C_BV, _C_BR).astype(BF16)
    r = mm(_C_BR, _C_GATE)
    br_ref[...] = (r * jax.nn.sigmoid(r)).astype(BF16)
    for c in range(_C_GATE, _C_END, 512):
        gate_ref[:, c - _C_GATE:c - _C_GATE + 512] = jax.nn.sigmoid(mm(c, c + 512)).astype(BF16)
    z = _dot(_dot(h, wal_ref[...]).astype(BF16), wal2_ref[...]) + bal_ref[...]
    lga_ref[...] = (jnp.minimum(z, 0.0) - jnp.log(1.0 + jnp.exp(-jnp.abs(z)))) * (1.0 / B_GATE_TAU)


def _inproj(x, g, w_main, w_al, w_al2, b_al):
    t, d = x.shape
    row = lambda w: pl.BlockSpec((ROW_TILE, w), lambda i: (i, 0))
    full = lambda a: pl.BlockSpec(a.shape, lambda i: (0,) * a.ndim)
    sds = lambda w, dt: jax.ShapeDtypeStruct((t, w), dt)
    widths = [(512, BF16), (512, BF16), (512, BF16), (256, BF16), (256, BF16), (512, BF16),
              (256, F32), (512, BF16), (2048, BF16)]
    return pl.pallas_call(
        _inproj_kernel,
        grid=(t // ROW_TILE,),
        in_specs=[row(d), full(g), full(w_main), full(w_al), full(w_al2), full(b_al)],
        out_specs=[row(w) for w, _ in widths],
        out_shape=[sds(w, dt) for w, dt in widths],
        compiler_params=_params("arbitrary"),
        name="inproj",
    )(x, g, w_main, w_al, w_al2, b_al)


def _band_kernel(q_ref, k0_ref, k1_ref, k2_ref, v0_ref, v1_ref, v2_ref, bias_ref, o_ref):
    i = pl.program_id(1)
    col = lax.broadcasted_iota(I32, (1, BAND_KEYS), 1)
    n_missing = jnp.maximum(BAND_TILES - 1 - i, 0)
    pen = jnp.where(col < n_missing * ROW_TILE, NEG, 0.0).astype(F32)
    lane = lax.broadcasted_iota(I32, (1, LANES), 1)
    low = lane < A_HEAD_DIM
    for p in range(A_HEADS // 2):
        sl = slice(p * LANES, (p + 1) * LANES)
        qp = q_ref[:, sl]
        kp = jnp.concatenate([k0_ref[:, sl], k1_ref[:, sl], k2_ref[:, sl]], axis=0)
        vp = jnp.concatenate([v0_ref[:, sl], v1_ref[:, sl], v2_ref[:, sl]], axis=0)
        outs = []
        for e in range(2):
            qm = jnp.where(low if e == 0 else jnp.logical_not(low), qp, jnp.zeros_like(qp))
            s = _dot_nt(qm, kp) + bias_ref[2 * p + e] + pen
            m = jnp.max(s, axis=-1, keepdims=True)
            pe = jnp.exp(s - m)
            l = jnp.sum(pe, axis=-1, keepdims=True)
            outs.append(_dot(pe.astype(BF16), vp) * (1.0 / l))
        o_ref[:, sl] = jnp.where(low, outs[0], outs[1]).astype(BF16)


def _band_bias(rel_table):
    q = np.arange(ROW_TILE)[:, None]
    k = np.arange(BAND_KEYS)[None, :]
    rel = q - k + A_LEFT_CHUNKS * CHUNK
    idx = np.clip(rel, -A_MAX_REL, A_MAX_REL) + A_MAX_REL
    cq, ck = q // CHUNK, k // CHUNK
    valid = (ck >= cq) & (ck <= cq + A_LEFT_CHUNKS)
    return jnp.where(valid[None], rel_table.astype(F32)[:, idx], NEG)


def _band_attention(q, k, v, bias, batch):
    t, w = q.shape
    nb = t // batch // ROW_TILE
    cur = lambda b, i: (b * nb + i, 0)
    back = lambda n: (lambda b, i: (b * nb + jnp.maximum(i - n, 0), 0))
    blk = lambda f: pl.BlockSpec((ROW_TILE, w), f)
    return pl.pallas_call(
        _band_kernel,
        grid=(batch, nb),
        in_specs=[blk(cur), blk(back(2)), blk(back(1)), blk(cur), blk(back(2)), blk(back(1)), blk(cur),
                  pl.BlockSpec(bias.shape, lambda b, i: (0, 0, 0))],
        out_specs=blk(cur),
        out_shape=jax.ShapeDtypeStruct((t, w), BF16),
        compiler_params=_params("arbitrary", "arbitrary"),
        name="band_attn",
    )(q, k, k, k, v, v, v, bias)


def _gla_constants():
    c = CHUNK
    t = np.arange(c)[:, None]
    r = np.arange(c)[None, :]
    mats = [(r <= t), (r > t)]
    lvl = np.full((c, c), -1, np.int32)
    lvl[np.arange(c), np.arange(c)] = N_LEVELS
    for l in range(N_LEVELS):
        m = (c // 2) >> l
        mid = (t // (2 * m)) * (2 * m) + m
        upper = t >= mid
        mats.append(np.where(upper, (r >= mid) & (r <= t), (r > t) & (r < mid)))
        s = r
        same = (s // (2 * m)) == (t // (2 * m))
        lvl[np.asarray(same & upper & (s < mid))] = l
    mexp = np.concatenate(mats, axis=0).astype(np.float32)
    lvl = np.tile(lvl, (1, B_HEADS))
    return jnp.asarray(mexp, BF16), jnp.asarray(lvl, I32)


def _gla_kernel(q_ref, k_ref, v_ref, g_ref, r_ref, gn_ref, mexp_ref, lvl_ref, o_ref, s_ref):
    @pl.when(pl.program_id(1) == 0)
    def _():
        s_ref[...] = jnp.zeros_like(s_ref)

    kw = B_KEY_WIDTH
    ri = lax.broadcasted_iota(I32, (kw, kw), 0) >> LOG_CHUNK
    ci = lax.broadcasted_iota(I32, (kw, kw), 1) >> LOG_CHUNK
    bd = ri == ci
    head_ind = jnp.where(bd, 1.0, 0.0).astype(BF16)
    ri2 = lax.broadcasted_iota(I32, (kw, 2 * kw), 0) >> LOG_CHUNK
    ci2 = (lax.broadcasted_iota(I32, (kw, 2 * kw), 1) & (kw - 1)) >> LOG_CHUNK
    bd2 = ri2 == ci2
    lvl = lvl_ref[...]
    mexp = mexp_ref[...]
    row8 = lax.broadcasted_iota(I32, (16, kw), 0)
    ones = jnp.ones((16, LANES), BF16)
    zero_b = jnp.zeros((kw, kw), BF16)

    for c in range(CHUNKS_PER_TILE):
        rows = slice(c * CHUNK, (c + 1) * CHUNK)
        q = q_ref[rows, :].astype(F32)
        k = k_ref[rows, :].astype(F32)
        v = v_ref[rows, :]
        g = g_ref[rows, :]
        g1 = g.astype(BF16)
        g2 = (g - g1.astype(F32)).astype(BF16)
        ex = _dot(mexp, jnp.concatenate([g1, g2], axis=1))
        w = jnp.exp(ex[:, :kw] + ex[:, kw:])
        w_cum = w[0:CHUNK]
        w_rev = w[CHUNK:2 * CHUNK]
        qt = (q * w_cum).astype(BF16)
        kb = (k * w_rev).astype(BF16)

        a = jnp.zeros((CHUNK, kw), F32)
        for l in range(N_LEVELS):
            wl = w[(2 + l) * CHUNK:(3 + l) * CHUNK]
            qh = (q * wl).astype(BF16)
            kh = (k * wl).astype(BF16)
            x = jnp.where(bd, jnp.concatenate([kh] * B_HEADS, axis=0), zero_b)
            a = jnp.where(lvl == l, _dot_nt(qh, x), a)
        a = jnp.where(lvl == N_LEVELS, _dot((q * k).astype(BF16), head_ind), a)

        s_old = s_ref[...]
        lhs = jnp.concatenate([a.astype(BF16), qt], axis=1)
        lhs = jnp.where(bd2, jnp.concatenate([lhs] * B_HEADS, axis=0), jnp.zeros((kw, 2 * kw), BF16))
        vstack = jnp.concatenate([v[:, j * LANES:(j + 1) * LANES] for j in range(B_HEADS)], axis=0)
        rhs = jnp.concatenate([vstack, s_old.astype(BF16)], axis=0)
        o = _dot(lhs, rhs)

        xk = jnp.where(bd, jnp.concatenate([kb] * B_HEADS, axis=0), zero_b)
        kv = _dot_tn(xk, vstack)
        d = w_cum[CHUNK - 1:CHUNK]
        d1 = d.astype(BF16).astype(F32)
        dp = jnp.where(row8 == 0, d1, jnp.where(row8 == 1, d - d1, 0.0)).astype(BF16)
        dcol = _dot_tn(dp, ones)
        s_ref[...] = dcol * s_old + kv

        for j in range(B_HEADS):
            oj = o[j * CHUNK:(j + 1) * CHUNK]
            sl = slice(j * LANES, (j + 1) * LANES)
            y = oj * lax.rsqrt(jnp.mean(oj * oj, axis=-1, keepdims=True) + EPS) * gn_ref[...]
            o_ref[rows, sl] = (y * r_ref[rows, sl].astype(F32)).astype(BF16)


def _gla(q, k, v, g, r, gn, batch):
    t = q.shape[0]
    nb = t // batch // ROW_TILE
    mexp, lvl = _gla_constants()
    cur = lambda b, i: (b * nb + i, 0)
    blk = lambda w: pl.BlockSpec((ROW_TILE, w), cur)
    full = lambda a: pl.BlockSpec(a.shape, lambda b, i: (0,) * a.ndim)
    return pl.pallas_call(
        _gla_kernel,
        grid=(batch, nb),
        in_specs=[blk(B_KEY_WIDTH), blk(B_KEY_WIDTH), blk(B_VAL_WIDTH), blk(B_KEY_WIDTH), blk(B_VAL_WIDTH),
                  full(gn), full(mexp), full(lvl)],
        out_specs=blk(B_VAL_WIDTH),
        out_shape=jax.ShapeDtypeStruct((t, B_VAL_WIDTH), BF16),
        scratch_shapes=[pltpu.VMEM((B_KEY_WIDTH, B_VAL_DIM), F32)],
        compiler_params=_params("arbitrary", "arbitrary"),
        name="gla",
    )(q, k, v, g, r, gn, mexp, lvl)


def _token_kernel(x_ref, oa_ref, ob_ref, gate_ref, wb0_ref, wb1_ref, wmix_ref, gx_ref, wq_ref,
                  km_ref, vm_ref, wo_ref, gf_ref, wr_ref, br_ref,
                  x2_ref, h3_ref, route_ref):
    ma = _dot(oa_ref[...], wb0_ref[...])
    mb = _dot(ob_ref[...], wb1_ref[...])
    merged = (gate_ref[:, :D_MODEL].astype(F32) * ma + gate_ref[:, D_MODEL:].astype(F32) * mb).astype(BF16)
    x1 = x_ref[...] + _dot(merged, wmix_ref[...])

    h2 = _rms(x1, gx_ref[...]).astype(BF16)
    qx = (_dot(h2, wq_ref[...]) * (X_HEAD_DIM ** -0.5)).astype(BF16)
    heads = []
    for h in range(X_HEADS):
        sl = slice(h * X_HEAD_DIM, (h + 1) * X_HEAD_DIM)
        s = _dot_nt(qx[:, sl], km_ref[0, :, sl])
        m = jnp.max(s, axis=-1, keepdims=True)
        pe = jnp.exp(s - m)
        l = jnp.sum(pe, axis=-1, keepdims=True)
        heads.append((_dot(pe.astype(BF16), vm_ref[0, :, sl]) * (1.0 / l)).astype(BF16))
    x2 = x1 + _dot(jnp.concatenate(heads, axis=1), wo_ref[...])
    x2_ref[...] = x2

    h3 = _rms(x2, gf_ref[...])
    h3_ref[...] = h3

    logits = jnp.dot(h3, wr_ref[...], preferred_element_type=F32,
                     precision=lax.Precision.HIGHEST) + br_ref[...]
    lane = lax.broadcasted_iota(I32, logits.shape, 1).astype(F32)
    big = jnp.float32(LANES)
    gl = jnp.where(lane < N_GROUPS, logits, NEG)
    gmax = jnp.max(gl, axis=-1, keepdims=True)
    gidx = jnp.min(jnp.where(gl == gmax, lane, big), axis=-1, keepdims=True)
    g_w = 1.0 / jnp.sum(jnp.exp(gl - gmax), axis=-1, keepdims=True)
    lo = N_GROUPS + EXPERTS_PER_GROUP * gidx
    el = jnp.where((lane >= lo) & (lane < lo + EXPERTS_PER_GROUP), logits, NEG)
    v1 = jnp.max(el, axis=-1, keepdims=True)
    i1 = jnp.min(jnp.where(el == v1, lane, big), axis=-1, keepdims=True)
    el2 = jnp.where(lane == i1, NEG, el)
    v2 = jnp.max(el2, axis=-1, keepdims=True)
    i2 = jnp.min(jnp.where(el2 == v2, lane, big), axis=-1, keepdims=True)
    e21 = jnp.exp(v2 - v1)
    w1 = g_w / (1.0 + e21)
    w2 = w1 * e21
    route = jnp.where(lane == 0, i1 - N_GROUPS,
                      jnp.where(lane == 1, i2 - N_GROUPS,
                                jnp.where(lane == 2, w1, jnp.where(lane == 3, w2, 0.0))))
    route_ref[...] = route


def _token(x, oa, ob, gates, wb0, wb1, wmix, gx, wq, km, vm, wo, gf, wr, br, batch):
    t, d = x.shape
    nb = t // batch // ROW_TILE
    cur = lambda b, i: (b * nb + i, 0)
    blk = lambda w: pl.BlockSpec((ROW_TILE, w), cur)
    full = lambda a: pl.BlockSpec(a.shape, lambda b, i: (0,) * a.ndim)
    mem = pl.BlockSpec((1,) + km.shape[1:], lambda b, i: (b, 0, 0))
    return pl.pallas_call(
        _token_kernel,
        grid=(batch, nb),
        in_specs=[blk(d), blk(A_WIDTH), blk(B_VAL_WIDTH), blk(2 * d), full(wb0), full(wb1), full(wmix),
                  full(gx), full(wq), mem, mem, full(wo), full(gf), full(wr), full(br)],
        out_specs=[blk(d), blk(d), blk(LANES)],
        out_shape=[jax.ShapeDtypeStruct((t, d), F32), jax.ShapeDtypeStruct((t, d), F32),
                   jax.ShapeDtypeStruct((t, LANES), F32)],
        compiler_params=_params("arbitrary", "arbitrary"),
        name="token",
    )(x, oa, ob, gates, wb0, wb1, wmix, gx, wq, km, vm, wo, gf, wr, br)


def _row_copy(src_hbm, row, dst_vmem, slot, sem):
    return pltpu.make_async_copy(src_hbm.at[pl.ds(row, 1)], dst_vmem.at[pl.ds(slot, 1)], sem)


def _expert_kernel(te_ref, nu_ref, src_ref, h_hbm, wg_ref, wu_ref, wd_ref, y_ref, xbuf, sem):
    i = pl.program_id(0)

    @pl.when(i < nu_ref[0])
    def _():
        def issue(r, carry):
            _row_copy(h_hbm, src_ref[0, 0, r], xbuf, r, sem).start()
            return carry

        lax.fori_loop(0, ROW_TILE, issue, 0)

        def drain(r, carry):
            _row_copy(h_hbm, 0, xbuf, r, sem).wait()
            return carry

        lax.fori_loop(0, ROW_TILE, drain, 0)

        x = xbuf[...].astype(BF16)
        hg = _dot(x, wg_ref[0])
        hu = _dot(x, wu_ref[0])
        hid = (hg * jax.nn.sigmoid(hg) * hu).astype(BF16)
        y_ref[...] = _dot(hid, wd_ref[0])

    @pl.when(i >= nu_ref[0])
    def _():
        y_ref[...] = jnp.zeros_like(y_ref)


def _experts(h3p, src, tile_expert, n_used, wg, wu, wd):
    n_tiles = tile_expert.shape[0]
    last = lambda i, te, nu: jnp.minimum(i, nu[0] - 1)
    wmap = lambda i, te, nu: (te[last(i, te, nu)], 0, 0)
    grid_spec = pltpu.PrefetchScalarGridSpec(
        num_scalar_prefetch=2,
        grid=(n_tiles,),
        in_specs=[pl.BlockSpec((1, 1, ROW_TILE), lambda i, te, nu: (last(i, te, nu), 0, 0),
                               memory_space=pltpu.SMEM),
                  pl.BlockSpec(memory_space=pl.ANY),
                  pl.BlockSpec((1, D_MODEL, EXPERT_FF), wmap),
                  pl.BlockSpec((1, D_MODEL, EXPERT_FF), wmap),
                  pl.BlockSpec((1, EXPERT_FF, D_MODEL), wmap)],
        out_specs=pl.BlockSpec((ROW_TILE, D_MODEL), lambda i, te, nu: (i, 0)),
        scratch_shapes=[pltpu.VMEM((ROW_TILE, D_MODEL), F32), pltpu.SemaphoreType.DMA(())],
    )
    return pl.pallas_call(
        _expert_kernel,
        grid_spec=grid_spec,
        out_shape=jax.ShapeDtypeStruct((n_tiles * ROW_TILE, D_MODEL), F32),
        compiler_params=_params("arbitrary"),
        name="experts",
    )(tile_expert, n_used, src.reshape(n_tiles, 1, ROW_TILE), h3p, wg, wu, wd)


def _combine_kernel(final_norm, dest_ref, x_ref, route_ref, gfin_ref, y_hbm, o_ref, ybuf, sem):
    def issue(r, carry):
        _row_copy(y_hbm, dest_ref[0, 0, r], ybuf, r, sem).start()
        return carry

    lax.fori_loop(0, 2 * ROW_TILE, issue, 0)

    def drain(r, carry):
        _row_copy(y_hbm, 0, ybuf, r, sem).wait()
        return carry

    lax.fori_loop(0, 2 * ROW_TILE, drain, 0)

    w1 = route_ref[:, 2:3]
    w2 = route_ref[:, 3:4]
    x3 = x_ref[...] + w1 * ybuf[0:ROW_TILE, :] + w2 * ybuf[ROW_TILE:2 * ROW_TILE, :]
    if final_norm:
        x3 = _rms(x3, gfin_ref[...])
    o_ref[...] = x3


def _combine(x2, route, dest, ys, gfin, final_norm):
    t, d = x2.shape
    nt = t // ROW_TILE
    dest_t = dest.reshape(nt, ROW_TILE, 2).transpose(0, 2, 1).reshape(nt, 1, 2 * ROW_TILE)
    return pl.pallas_call(
        functools.partial(_combine_kernel, final_norm),
        grid=(nt,),
        in_specs=[pl.BlockSpec((1, 1, 2 * ROW_TILE), lambda i: (i, 0, 0), memory_space=pltpu.SMEM),
                  pl.BlockSpec((ROW_TILE, d), lambda i: (i, 0)),
                  pl.BlockSpec((ROW_TILE, LANES), lambda i: (i, 0)),
                  pl.BlockSpec((1, d), lambda i: (0, 0)),
                  pl.BlockSpec(memory_space=pl.ANY)],
        out_specs=pl.BlockSpec((ROW_TILE, d), lambda i: (i, 0)),
        out_shape=jax.ShapeDtypeStruct((t, d), F32),
        scratch_shapes=[pltpu.VMEM((2 * ROW_TILE, D_MODEL), F32), pltpu.SemaphoreType.DMA(())],
        compiler_params=_params("arbitrary"),
        name="combine",
    )(dest_t, x2, route, gfin, ys)


def _routing_plan(route, n_tiles):
    t = route.shape[0]
    ef = route[:, :2].astype(I32).reshape(-1)
    onehot = (ef[:, None] == jnp.arange(N_EXPERTS, dtype=I32)[None, :]).astype(I32)
    csum = jnp.cumsum(onehot, axis=0)
    rank = jnp.take_along_axis(csum, ef[:, None], axis=1)[:, 0] - 1
    counts = csum[-1]
    tiles = (counts + ROW_TILE - 1) // ROW_TILE
    tile_end = jnp.cumsum(tiles)
    dest = (tile_end - tiles)[ef] * ROW_TILE + rank
    n_used = tile_end[-1:].astype(I32)
    tile_ids = jnp.arange(n_tiles, dtype=I32)
    tile_expert = jnp.minimum(jnp.sum((tile_end[None, :] <= tile_ids[:, None]).astype(I32), axis=1),
                              N_EXPERTS - 1)
    src = jnp.zeros((n_tiles * ROW_TILE,), I32).at[dest].set(jnp.arange(2 * t, dtype=I32) // 2)
    return dest, src, tile_expert, n_used


def kernel(x, mem, norm_mix_g, w_in, rel_bias, gla_w_alpha, gla_b_alpha, gla_norm_g, w_branch, w_mix_out, norm_x_g, mem_norm_g, w_xq, w_xkv, w_xo, norm_ffn_g, w_group_router, b_group_router, w_expert_router, b_expert_router, w_exp_gate, w_exp_up, w_exp_down, final_norm_g):
    batch, seq, d = x.shape
    depth = w_in.shape[0]
    t = batch * seq
    assert d == D_MODEL and seq % ROW_TILE == 0
    n_tiles = (2 * t) // ROW_TILE + N_EXPERTS

    xf = x.reshape(t, d)
    km_all, vm_all = _memkv(mem, mem_norm_g, w_xkv.astype(BF16))
    row = lambda a: a.reshape(1, -1).astype(F32)

    for l in range(depth):
        wi = w_in[l]
        c0 = A_WIDTH * 3 + B_KEY_WIDTH * 2 + B_VAL_WIDTH
        w_main = jnp.concatenate([wi[:, :c0], wi[:, c0 + B_GATE_RANK:]], axis=1).astype(BF16)
        w_al = jnp.pad(wi[:, c0:c0 + B_GATE_RANK], ((0, 0), (0, LANES - B_GATE_RANK))).astype(BF16)
        w_al2 = jnp.pad(gla_w_alpha[l], ((0, LANES - B_GATE_RANK), (0, 0))).astype(BF16)
        aq, ak, av, bq, bk, bv, lga, br, gates = _inproj(
            xf, row(norm_mix_g[l]), w_main, w_al, w_al2, row(gla_b_alpha[l]))

        oa = _band_attention(aq, ak, av, _band_bias(rel_bias[l]), batch)
        ob = _gla(bq, bk, bv, lga, br, row(gla_norm_g[l]), batch)

        wr = jnp.pad(jnp.concatenate([w_group_router[l], w_expert_router[l]], axis=1).astype(F32),
                     ((0, 0), (0, LANES - N_GROUPS - N_EXPERTS)))
        brt = jnp.pad(jnp.concatenate([b_group_router[l], b_expert_router[l]]).astype(F32),
                      (0, LANES - N_GROUPS - N_EXPERTS)).reshape(1, LANES)
        x2, h3p, route = _token(
            xf, oa, ob, gates, w_branch[l, 0].astype(BF16), w_branch[l, 1].astype(BF16),
            w_mix_out[l].astype(BF16), row(norm_x_g[l]), w_xq[l].astype(BF16), km_all[l], vm_all[l],
            w_xo[l].astype(BF16), row(norm_ffn_g[l]), wr, brt, batch)

        dest, src, tile_expert, n_used = _routing_plan(route, n_tiles)
        e3 = lambda w: w[l].reshape((N_EXPERTS,) + w.shape[3:]).astype(BF16)
        ys = _experts(h3p, src, tile_expert, n_used, e3(w_exp_gate), e3(w_exp_up), e3(w_exp_down))
        xf = _combine(x2, route, dest, ys, row(final_norm_g), l == depth - 1)

    return xf.reshape(batch, seq, d)
```

```python
import functools

import numpy as np
import jax
import jax.numpy as jnp
from jax import lax
from jax.experimental import pallas as pl
from jax.experimental.pallas import tpu as pltpu

F32 = jnp.float32
BF16 = jnp.bfloat16
I32 = jnp.int32

D_MODEL = 1024
CHUNK = 64
EPS = 1e-6
A_HEADS = 8
A_HEAD_DIM = 64
A_WIDTH = 512
A_LEFT_CHUNKS = 8
A_MAX_REL = 256
B_HEADS = 4
B_KEY_DIM = 64
B_VAL_DIM = 128
B_KEY_WIDTH = 256
B_VAL_WIDTH = 512
B_GATE_RANK = 16
B_GATE_TAU = 16.0
X_HEADS = 4
X_HEAD_DIM = 256
N_GROUPS = 4
EXPERTS_PER_GROUP = 8
N_EXPERTS = N_GROUPS * EXPERTS_PER_GROUP
EXPERT_FF = 256

LANES = 128
ROW_TILE = 256
CHUNKS_PER_TILE = ROW_TILE // CHUNK
BAND_TILES = A_LEFT_CHUNKS // CHUNKS_PER_TILE + 1
BAND_KEYS = BAND_TILES * ROW_TILE
LOG_CHUNK = 6
N_LEVELS = LOG_CHUNK
EXP_ROWS = (2 + N_LEVELS) * CHUNK
CHUNK_ROWS = 16
TILE_CHUNKS = ROW_TILE // CHUNK_ROWS
LOCAL_CHUNKS = 2 * TILE_CHUNKS + N_EXPERTS
LOCAL_ROWS = LOCAL_CHUNKS * CHUNK_ROWS
NEG = -1e30
VMEM_LIMIT = 56 * 1024 * 1024


def _params(*sem):
    return pltpu.CompilerParams(dimension_semantics=sem, vmem_limit_bytes=VMEM_LIMIT)


def _rms(x, g):
    return x * lax.rsqrt(jnp.mean(x * x, axis=-1, keepdims=True) + EPS) * g


def _dot(a, b):
    return jnp.dot(a, b, preferred_element_type=F32)


def _dot_nt(a, b):
    return lax.dot_general(a, b, (((1,), (1,)), ((), ())), preferred_element_type=F32)


def _dot_tn(a, b):
    return lax.dot_general(a, b, (((0,), (0,)), ((), ())), preferred_element_type=F32)


def _memkv_kernel(mem_ref, g_ref, w_ref, k_ref, v_ref):
    mn = _rms(mem_ref[0], g_ref[...]).astype(BF16)
    kv = _dot(mn, w_ref[0])
    k_ref[0, 0] = kv[:, :D_MODEL].astype(BF16)
    v_ref[0, 0] = kv[:, D_MODEL:].astype(BF16)


def _memkv(mem, g, w_xkv):
    depth = w_xkv.shape[0]
    b, m, d = mem.shape
    out = jax.ShapeDtypeStruct((depth, b, m, d), BF16)
    return pl.pallas_call(
        _memkv_kernel,
        grid=(depth, b),
        in_specs=[pl.BlockSpec((1, m, d), lambda l, i: (i, 0, 0)),
                  pl.BlockSpec((1, d), lambda l, i: (0, 0)),
                  pl.BlockSpec((1, d, 2 * d), lambda l, i: (l, 0, 0))],
        out_specs=[pl.BlockSpec((1, 1, m, d), lambda l, i: (l, i, 0, 0)),
                   pl.BlockSpec((1, 1, m, d), lambda l, i: (l, i, 0, 0))],
        out_shape=[out, out],
        compiler_params=_params("arbitrary", "arbitrary"),
        name="memkv",
    )(mem, g.reshape(1, d), w_xkv)


_C_AQ, _C_AK, _C_AV = 0, 512, 1024
_C_BQ, _C_BK, _C_BV = 1536, 1792, 2048
_C_BR, _C_GATE, _C_END = 2560, 3072, 5120


def _inproj_kernel(x_ref, g_ref, w_ref, wal_ref, wal2_ref, bal_ref,
                   aq_ref, ak_ref, av_ref, bq_ref, bk_ref, bv_ref, lga_ref, br_ref, gate_ref):
    h = _rms(x_ref[...], g_ref[...]).astype(BF16)

    def mm(lo, hi):
        return _dot(h, w_ref[:, lo:hi])

    aq_ref[...] = (mm(_C_AQ, _C_AK) * (A_HEAD_DIM ** -0.5)).astype(BF16)
    ak_ref[...] = mm(_C_AK, _C_AV).astype(BF16)
    av_ref[...] = mm(_C_AV, _C_BQ).astype(BF16)
    bq_ref[...] = (mm(_C_BQ, _C_BK) * (B_KEY_DIM ** -0.5)).astype(BF16)
    bk_ref[...] = mm(_C_BK, _C_BV).astype(BF16)
    bv_ref[...] = mm(_C_BV, _C_BR).astype(BF16)
    r = mm(_C_BR, _C_GATE)
    br_ref[...] = (r * jax.nn.sigmoid(r)).astype(BF16)
    for c in range(_C_GATE, _C_END, 512):
        gate_ref[:, c - _C_GATE:c - _C_GATE + 512] = jax.nn.sigmoid(mm(c, c + 512)).astype(BF16)
    z = _dot(_dot(h, wal_ref[...]).astype(BF16), wal2_ref[...]) + bal_ref[...]
    lga_ref[...] = (jnp.minimum(z, 0.0) - jnp.log(1.0 + jnp.exp(-jnp.abs(z)))) * (1.0 / B_GATE_TAU)


def _inproj(x, g, w_main, w_al, w_al2, b_al):
    t, d = x.shape
    row = lambda w: pl.BlockSpec((ROW_TILE, w), lambda i: (i, 0))
    full = lambda a: pl.BlockSpec(a.shape, lambda i: (0,) * a.ndim)
    sds = lambda w, dt: jax.ShapeDtypeStruct((t, w), dt)
    widths = [(512, BF16), (512, BF16), (512, BF16), (256, BF16), (256, BF16), (512, BF16),
              (256, F32), (512, BF16), (2048, BF16)]
    return pl.pallas_call(
        _inproj_kernel,
        grid=(t // ROW_TILE,),
        in_specs=[row(d), full(g), full(w_main), full(w_al), full(w_al2), full(b_al)],
        out_specs=[row(w) for w, _ in widths],
        out_shape=[sds(w, dt) for w, dt in widths],
        compiler_params=_params("arbitrary"),
        name="inproj",
    )(x, g, w_main, w_al, w_al2, b_al)


def _band_kernel(q_ref, k0_ref, k1_ref, k2_ref, v0_ref, v1_ref, v2_ref, bias_ref, o_ref):
    i = pl.program_id(1)
    col = lax.broadcasted_iota(I32, (1, BAND_KEYS), 1)
    n_missing = jnp.maximum(BAND_TILES - 1 - i, 0)
    pen = jnp.where(col < n_missing * ROW_TILE, NEG, 0.0).astype(F32)
    lane = lax.broadcasted_iota(I32, (1, LANES), 1)
    low = lane < A_HEAD_DIM
    for p in range(A_HEADS // 2):
        sl = slice(p * LANES, (p + 1) * LANES)
        qp = q_ref[:, sl]
        kp = jnp.concatenate([k0_ref[:, sl], k1_ref[:, sl], k2_ref[:, sl]], axis=0)
        vp = jnp.concatenate([v0_ref[:, sl], v1_ref[:, sl], v2_ref[:, sl]], axis=0)
        outs = []
        for e in range(2):
            qm = jnp.where(low if e == 0 else jnp.logical_not(low), qp, jnp.zeros_like(qp))
            s = _dot_nt(qm, kp) + bias_ref[2 * p + e] + pen
            m = jnp.max(s, axis=-1, keepdims=True)
            pe = jnp.exp(s - m)
            l = jnp.sum(pe, axis=-1, keepdims=True)
            outs.append(_dot(pe.astype(BF16), vp) * (1.0 / l))
        o_ref[:, sl] = jnp.where(low, outs[0], outs[1]).astype(BF16)


def _band_bias(rel_table):
    q = np.arange(ROW_TILE)[:, None]
    k = np.arange(BAND_KEYS)[None, :]
    rel = q - k + A_LEFT_CHUNKS * CHUNK
    idx = np.clip(rel, -A_MAX_REL, A_MAX_REL) + A_MAX_REL
    cq, ck = q // CHUNK, k // CHUNK
    valid = (ck >= cq) & (ck <= cq + A_LEFT_CHUNKS)
    return jnp.where(valid[None], rel_table.astype(F32)[:, idx], NEG)


def _band_attention(q, k, v, bias, batch):
    t, w = q.shape
    nb = t // batch // ROW_TILE
    cur = lambda b, i: (b * nb + i, 0)
    back = lambda n: (lambda b, i: (b * nb + jnp.maximum(i - n, 0), 0))
    blk = lambda f: pl.BlockSpec((ROW_TILE, w), f)
    return pl.pallas_call(
        _band_kernel,
        grid=(batch, nb),
        in_specs=[blk(cur), blk(back(2)), blk(back(1)), blk(cur), blk(back(2)), blk(back(1)), blk(cur),
                  pl.BlockSpec(bias.shape, lambda b, i: (0, 0, 0))],
        out_specs=blk(cur),
        out_shape=jax.ShapeDtypeStruct((t, w), BF16),
        compiler_params=_params("arbitrary", "arbitrary"),
        name="band_attn",
    )(q, k, k, k, v, v, v, bias)


def _gla_constants():
    c = CHUNK
    t = np.arange(c)[:, None]
    r = np.arange(c)[None, :]
    mats = [(r <= t), (r > t)]
    lvl = np.full((c, c), -1, np.int32)
    lvl[np.arange(c), np.arange(c)] = N_LEVELS
    for l in range(N_LEVELS):
        m = (c // 2) >> l
        mid = (t // (2 * m)) * (2 * m) + m
        upper = t >= mid
        mats.append(np.where(upper, (r >= mid) & (r <= t), (r > t) & (r < mid)))
        s = r
        same = (s // (2 * m)) == (t // (2 * m))
        lvl[np.asarray(same & upper & (s < mid))] = l
    mexp = np.concatenate(mats, axis=0).astype(np.float32)
    lvl = np.tile(lvl, (1, B_HEADS))
    return jnp.asarray(mexp, BF16), jnp.asarray(lvl, I32)


def _gla_kernel(q_ref, k_ref, v_ref, g_ref, r_ref, gn_ref, mexp_ref, lvl_ref, o_ref, s_ref):
    @pl.when(pl.program_id(1) == 0)
    def _():
        s_ref[...] = jnp.zeros_like(s_ref)

    kw = B_KEY_WIDTH
    ri = lax.broadcasted_iota(I32, (kw, kw), 0) >> LOG_CHUNK
    ci = lax.broadcasted_iota(I32, (kw, kw), 1) >> LOG_CHUNK
    bd = ri == ci
    head_ind = jnp.where(bd, 1.0, 0.0).astype(BF16)
    ri2 = lax.broadcasted_iota(I32, (kw, 2 * kw), 0) >> LOG_CHUNK
    ci2 = (lax.broadcasted_iota(I32, (kw, 2 * kw), 1) & (kw - 1)) >> LOG_CHUNK
    bd2 = ri2 == ci2
    lvl = lvl_ref[...]
    mexp = mexp_ref[...]
    row8 = lax.broadcasted_iota(I32, (16, kw), 0)
    ones = jnp.ones((16, LANES), BF16)
    zero_b = jnp.zeros((kw, kw), BF16)

    for c in range(CHUNKS_PER_TILE):
        rows = slice(c * CHUNK, (c + 1) * CHUNK)
        q = q_ref[rows, :].astype(F32)
        k = k_ref[rows, :].astype(F32)
        v = v_ref[rows, :]
        g = g_ref[rows, :]
        g1 = g.astype(BF16)
        g2 = (g - g1.astype(F32)).astype(BF16)
        ex = _dot(mexp, jnp.concatenate([g1, g2], axis=1))
        w = jnp.exp(ex[:, :kw] + ex[:, kw:])
        w_cum = w[0:CHUNK]
        w_rev = w[CHUNK:2 * CHUNK]
        qt = (q * w_cum).astype(BF16)
        kb = (k * w_rev).astype(BF16)

        a = jnp.zeros((CHUNK, kw), F32)
        for l in range(N_LEVELS):
            wl = w[(2 + l) * CHUNK:(3 + l) * CHUNK]
            qh = (q * wl).astype(BF16)
            kh = (k * wl).astype(BF16)
            x = jnp.where(bd, jnp.concatenate([kh] * B_HEADS, axis=0), zero_b)
            a = jnp.where(lvl == l, _dot_nt(qh, x), a)
        a = jnp.where(lvl == N_LEVELS, _dot((q * k).astype(BF16), head_ind), a)

        s_old = s_ref[...]
        lhs = jnp.concatenate([a.astype(BF16), qt], axis=1)
        lhs = jnp.where(bd2, jnp.concatenate([lhs] * B_HEADS, axis=0), jnp.zeros((kw, 2 * kw), BF16))
        vstack = jnp.concatenate([v[:, j * LANES:(j + 1) * LANES] for j in range(B_HEADS)], axis=0)
        rhs = jnp.concatenate([vstack, s_old.astype(BF16)], axis=0)
        o = _dot(lhs, rhs)

        xk = jnp.where(bd, jnp.concatenate([kb] * B_HEADS, axis=0), zero_b)
        kv = _dot_tn(xk, vstack)
        d = w_cum[CHUNK - 1:CHUNK]
        d1 = d.astype(BF16).astype(F32)
        dp = jnp.where(row8 == 0, d1, jnp.where(row8 == 1, d - d1, 0.0)).astype(BF16)
        dcol = _dot_tn(dp, ones)
        s_ref[...] = dcol * s_old + kv

        for j in range(B_HEADS):
            oj = o[j * CHUNK:(j + 1) * CHUNK]
            sl = slice(j * LANES, (j + 1) * LANES)
            y = oj * lax.rsqrt(jnp.mean(oj * oj, axis=-1, keepdims=True) + EPS) * gn_ref[...]
            o_ref[rows, sl] = (y * r_ref[rows, sl].astype(F32)).astype(BF16)


def _gla(q, k, v, g, r, gn, batch):
    t = q.shape[0]
    nb = t // batch // ROW_TILE
    mexp, lvl = _gla_constants()
    cur = lambda b, i: (b * nb + i, 0)
    blk = lambda w: pl.BlockSpec((ROW_TILE, w), cur)
    full = lambda a: pl.BlockSpec(a.shape, lambda b, i: (0,) * a.ndim)
    return pl.pallas_call(
        _gla_kernel,
        grid=(batch, nb),
        in_specs=[blk(B_KEY_WIDTH), blk(B_KEY_WIDTH), blk(B_VAL_WIDTH), blk(B_KEY_WIDTH), blk(B_VAL_WIDTH),
                  full(gn), full(mexp), full(lvl)],
        out_specs=blk(B_VAL_WIDTH),
        out_shape=jax.ShapeDtypeStruct((t, B_VAL_WIDTH), BF16),
        scratch_shapes=[pltpu.VMEM((B_KEY_WIDTH, B_VAL_DIM), F32)],
        compiler_params=_params("arbitrary", "arbitrary"),
        name="gla",
    )(q, k, v, g, r, gn, mexp, lvl)


def _token_kernel(x_ref, oa_ref, ob_ref, gate_ref, wb0_ref, wb1_ref, wmix_ref, gx_ref, wq_ref,
                  km_ref, vm_ref, wo_ref, gf_ref, wr_ref, br_ref, ltri_ref, utri_ref,
                  x2_ref, hs_ref, ws_ref, route_ref, cnt_ref):
    ma = _dot(oa_ref[...], wb0_ref[...])
    mb = _dot(ob_ref[...], wb1_ref[...])
    merged = (gate_ref[:, :D_MODEL].astype(F32) * ma + gate_ref[:, D_MODEL:].astype(F32) * mb).astype(BF16)
    x1 = x_ref[...] + _dot(merged, wmix_ref[...])

    h2 = _rms(x1, gx_ref[...]).astype(BF16)
    qx = (_dot(h2, wq_ref[...]) * (X_HEAD_DIM ** -0.5)).astype(BF16)
    heads = []
    for h in range(X_HEADS):
        sl = slice(h * X_HEAD_DIM, (h + 1) * X_HEAD_DIM)
        s = _dot_nt(qx[:, sl], km_ref[0, :, sl])
        m = jnp.max(s, axis=-1, keepdims=True)
        pe = jnp.exp(s - m)
        l = jnp.sum(pe, axis=-1, keepdims=True)
        heads.append((_dot(pe.astype(BF16), vm_ref[0, :, sl]) * (1.0 / l)).astype(BF16))
    x2 = x1 + _dot(jnp.concatenate(heads, axis=1), wo_ref[...])
    x2_ref[...] = x2

    h3 = _rms(x2, gf_ref[...])

    logits = jnp.dot(h3, wr_ref[...], preferred_element_type=F32,
                     precision=lax.Precision.HIGHEST) + br_ref[...]
    lane = lax.broadcasted_iota(I32, logits.shape, 1).astype(F32)
    big = jnp.float32(LANES)
    gl = jnp.where(lane < N_GROUPS, logits, NEG)
    gmax = jnp.max(gl, axis=-1, keepdims=True)
    gidx = jnp.min(jnp.where(gl == gmax, lane, big), axis=-1, keepdims=True)
    g_w = 1.0 / jnp.sum(jnp.exp(gl - gmax), axis=-1, keepdims=True)
    lo = N_GROUPS + EXPERTS_PER_GROUP * gidx
    el = jnp.where((lane >= lo) & (lane < lo + EXPERTS_PER_GROUP), logits, NEG)
    v1 = jnp.max(el, axis=-1, keepdims=True)
    i1 = jnp.min(jnp.where(el == v1, lane, big), axis=-1, keepdims=True)
    el2 = jnp.where(lane == i1, NEG, el)
    v2 = jnp.max(el2, axis=-1, keepdims=True)
    i2 = jnp.min(jnp.where(el2 == v2, lane, big), axis=-1, keepdims=True)
    e21 = jnp.exp(v2 - v1)
    w1 = g_w / (1.0 + e21)
    w2 = w1 * e21
    oh0 = jnp.where(lane == i1 - N_GROUPS, 1.0, 0.0)
    oh1 = jnp.where(lane == i2 - N_GROUPS, 1.0, 0.0)
    oh = oh0 + oh1
    nch = jnp.floor((jnp.sum(oh, axis=0, keepdims=True) + (CHUNK_ROWS - 1)) * (1.0 / CHUNK_ROWS))
    nch8 = jnp.broadcast_to(nch, (8, LANES))
    start = _dot(nch8.astype(BF16), utri_ref[...])[0:1] * CHUNK_ROWS
    rank = _dot(ltri_ref[...], oh.astype(BF16))
    row = start + rank
    pos0 = jnp.sum(row * oh0, axis=-1, keepdims=True)
    pos1 = jnp.sum(row * oh1, axis=-1, keepdims=True)
    r = lax.broadcasted_iota(I32, (ROW_TILE, LOCAL_ROWS), 1).astype(F32)
    p0 = jnp.where(r == pos0, 1.0, 0.0).astype(BF16)
    p1 = jnp.where(r == pos1, 1.0, 0.0).astype(BF16)
    hs_ref[...] = _dot_tn(p0 + p1, h3.astype(BF16)).astype(BF16).reshape(hs_ref.shape)

    def gate_cols(w):
        hi = w.astype(BF16).astype(F32)
        return jnp.where(lane == 0, hi, jnp.where(lane == 1, w - hi, 0.0)).astype(BF16)

    ws_ref[...] = (_dot_tn(p0, gate_cols(w1)) + _dot_tn(p1, gate_cols(w2))).astype(BF16).reshape(ws_ref.shape)
    route_ref[...] = jnp.where(lane == 0, pos0, jnp.where(lane == 1, pos1, 0.0))
    cnt_ref[0] = nch8


def _token(x, oa, ob, gates, wb0, wb1, wmix, gx, wq, km, vm, wo, gf, wr, br, batch):
    t, d = x.shape
    nb = t // batch // ROW_TILE
    nt = t // ROW_TILE
    ltri = jnp.asarray(np.tril(np.ones((ROW_TILE, ROW_TILE), np.float32), -1), BF16)
    utri = jnp.asarray(np.triu(np.ones((LANES, LANES), np.float32), 1), BF16)
    cur = lambda b, i: (b * nb + i, 0)
    cur3 = lambda b, i: (b * nb + i, 0, 0)
    blk = lambda w: pl.BlockSpec((ROW_TILE, w), cur)
    full = lambda a: pl.BlockSpec(a.shape, lambda b, i: (0,) * a.ndim)
    mem = pl.BlockSpec((1,) + km.shape[1:], lambda b, i: (b, 0, 0))
    return pl.pallas_call(
        _token_kernel,
        grid=(batch, nb),
        in_specs=[blk(d), blk(A_WIDTH), blk(B_VAL_WIDTH), blk(2 * d), full(wb0), full(wb1), full(wmix),
                  full(gx), full(wq), mem, mem, full(wo), full(gf), full(wr), full(br), full(ltri), full(utri)],
        out_specs=[blk(d), pl.BlockSpec((LOCAL_CHUNKS, CHUNK_ROWS, d), cur3),
                   pl.BlockSpec((LOCAL_CHUNKS, CHUNK_ROWS, LANES), cur3),
                   blk(LANES), pl.BlockSpec((1, 8, LANES), cur3)],
        out_shape=[jax.ShapeDtypeStruct((t, d), F32),
                   jax.ShapeDtypeStruct((nt * LOCAL_CHUNKS, CHUNK_ROWS, d), BF16),
                   jax.ShapeDtypeStruct((nt * LOCAL_CHUNKS, CHUNK_ROWS, LANES), BF16),
                   jax.ShapeDtypeStruct((t, LANES), F32),
                   jax.ShapeDtypeStruct((nt, 8, LANES), F32)],
        compiler_params=_params("arbitrary", "arbitrary"),
        name="token",
    )(x, oa, ob, gates, wb0, wb1, wmix, gx, wq, km, vm, wo, gf, wr, br, ltri, utri)


def _expert_kernel(te_ref, nu_ref, nv_ref, ch_ref, hs_hbm, ws_hbm, wg_ref, wu_ref, wd_ref, yinit_hbm, ys_hbm,
                   xbuf, gbuf, ybuf, gsem, ssem):
    del yinit_hbm
    i = pl.program_id(0)
    n_used = nu_ref[0]
    slot = lax.rem(i, 2)

    def gather(tile, s, start):
        def body(c, carry):
            ch = ch_ref[tile * TILE_CHUNKS + c]
            for cp in (pltpu.make_async_copy(hs_hbm.at[ch], xbuf.at[s, c], gsem.at[s]),
                       pltpu.make_async_copy(ws_hbm.at[ch], gbuf.at[s, c], gsem.at[s])):
                cp.start() if start else cp.wait()
            return carry

        lax.fori_loop(0, nv_ref[tile], body, 0)

    def scatter(tile, s, start):
        def body(c, carry):
            ch = ch_ref[tile * TILE_CHUNKS + c]
            cp = pltpu.make_async_copy(ybuf.at[s, c], ys_hbm.at[ch], ssem.at[s])
            cp.start() if start else cp.wait()
            return carry

        lax.fori_loop(0, nv_ref[tile], body, 0)

    @pl.when(i == 0)
    def _():
        xbuf[...] = jnp.zeros_like(xbuf)
        gbuf[...] = jnp.zeros_like(gbuf)
        gather(0, 0, True)

    @pl.when(i < n_used)
    def _():
        @pl.when(i + 1 < n_used)
        def _():
            gather(i + 1, 1 - slot, True)

        gather(i, slot, False)

        @pl.when(i >= 2)
        def _():
            scatter(i - 2, slot, False)

        x = xbuf[slot].reshape(ROW_TILE, D_MODEL)
        hg = _dot(x, wg_ref[0])
        hu = _dot(x, wu_ref[0])
        hid = (hg * jax.nn.sigmoid(hg) * hu).astype(BF16)
        g = gbuf[slot].reshape(ROW_TILE, LANES).astype(F32)
        gate = g[:, 0:1] + g[:, 1:2]
        ybuf[slot] = (gate * _dot(hid, wd_ref[0])).astype(BF16).reshape(TILE_CHUNKS, CHUNK_ROWS, D_MODEL)
        scatter(i, slot, True)

        @pl.when(i == n_used - 1)
        def _():
            scatter(i, slot, False)

            @pl.when(i >= 1)
            def _():
                scatter(i - 1, 1 - slot, False)


def _experts(hs, ws, tile_expert, n_used, n_valid, chunks, wg, wu, wd):
    n_tiles = tile_expert.shape[0]
    last = lambda i, te, nu, nv, ch: jnp.minimum(i, nu[0] - 1)
    wmap = lambda i, te, nu, nv, ch: (te[last(i, te, nu, nv, ch)], 0, 0)
    anyspace = pl.BlockSpec(memory_space=pl.ANY)
    grid_spec = pltpu.PrefetchScalarGridSpec(
        num_scalar_prefetch=4,
        grid=(n_tiles,),
        in_specs=[anyspace, anyspace,
                  pl.BlockSpec((1, D_MODEL, EXPERT_FF), wmap),
                  pl.BlockSpec((1, D_MODEL, EXPERT_FF), wmap),
                  pl.BlockSpec((1, EXPERT_FF, D_MODEL), wmap),
                  anyspace],
        out_specs=anyspace,
        scratch_shapes=[pltpu.VMEM((2, TILE_CHUNKS, CHUNK_ROWS, D_MODEL), BF16),
                        pltpu.VMEM((2, TILE_CHUNKS, CHUNK_ROWS, LANES), BF16),
                        pltpu.VMEM((2, TILE_CHUNKS, CHUNK_ROWS, D_MODEL), BF16),
                        pltpu.SemaphoreType.DMA((2,)), pltpu.SemaphoreType.DMA((2,))],
    )
    return pl.pallas_call(
        _expert_kernel,
        grid_spec=grid_spec,
        out_shape=jax.ShapeDtypeStruct(hs.shape, BF16),
        input_output_aliases={9: 0},
        compiler_params=_params("arbitrary"),
        name="experts",
    )(tile_expert, n_used, n_valid, chunks, hs, ws, wg, wu, wd, jnp.zeros(hs.shape, BF16))


def _combine_kernel(final_norm, x_ref, route_ref, ys_ref, gfin_ref, o_ref):
    r = lax.broadcasted_iota(I32, (ROW_TILE, LOCAL_ROWS), 1).astype(F32)
    sel = jnp.where(r == route_ref[:, 0:1], 1.0, jnp.where(r == route_ref[:, 1:2], 1.0, 0.0)).astype(BF16)
    x3 = x_ref[...] + _dot(sel, ys_ref[...].reshape(LOCAL_ROWS, D_MODEL))
    if final_norm:
        x3 = _rms(x3, gfin_ref[...])
    o_ref[...] = x3


def _combine(x2, route, ys, gfin, final_norm):
    t, d = x2.shape
    nt = t // ROW_TILE
    return pl.pallas_call(
        functools.partial(_combine_kernel, final_norm),
        grid=(nt,),
        in_specs=[pl.BlockSpec((ROW_TILE, d), lambda i: (i, 0)),
                  pl.BlockSpec((ROW_TILE, LANES), lambda i: (i, 0)),
                  pl.BlockSpec((LOCAL_CHUNKS, CHUNK_ROWS, d), lambda i: (i, 0, 0)),
                  pl.BlockSpec((1, d), lambda i: (0, 0))],
        out_specs=pl.BlockSpec((ROW_TILE, d), lambda i: (i, 0)),
        out_shape=jax.ShapeDtypeStruct((t, d), F32),
        compiler_params=_params("arbitrary"),
        name="combine",
    )(x2, route, ys, gfin)


def _chunk_plan(nch, n_tiles):
    nt = nch.shape[0]
    local_start = jnp.cumsum(nch, axis=1) - nch
    cum = jnp.cumsum(nch, axis=0)
    total = cum[-1]
    tiles = (total + TILE_CHUNKS - 1) // TILE_CHUNKS
    tile_end = jnp.cumsum(tiles)
    n_used = tile_end[-1:]
    tile_ids = jnp.arange(n_tiles, dtype=I32)
    tile_expert = jnp.minimum(jnp.sum((tile_end[None, :] <= tile_ids[:, None]).astype(I32), axis=1),
                              N_EXPERTS - 1)
    sel = (tile_expert[:, None] == jnp.arange(N_EXPERTS, dtype=I32)[None, :]).astype(I32)
    pick = lambda table: jnp.sum(sel[:, :, None] * table.T[None, :, :], axis=1)
    first_tile = jnp.sum(sel * (tile_end - tiles)[None, :], axis=1)
    slot = (tile_ids - first_tile)[:, None] * TILE_CHUNKS + jnp.arange(TILE_CHUNKS, dtype=I32)[None, :]
    valid = (slot < jnp.sum(sel * total[None, :], axis=1)[:, None]) & (tile_ids < n_used)[:, None]
    src_tile = jnp.sum((pick(cum)[:, None, :] <= slot[:, :, None]).astype(I32), axis=2)
    src_tile = jnp.minimum(src_tile, nt - 1)
    at = (src_tile[:, :, None] == jnp.arange(nt, dtype=I32)[None, None, :]).astype(I32)
    before = jnp.sum(at * pick(cum - nch)[:, None, :], axis=2)
    start = jnp.sum(at * pick(local_start)[:, None, :], axis=2)
    chunk = jnp.where(valid, src_tile * LOCAL_CHUNKS + start + slot - before, 0)
    return tile_expert, n_used, jnp.sum(valid.astype(I32), axis=1), chunk.reshape(-1)


def kernel(x, mem, norm_mix_g, w_in, rel_bias, gla_w_alpha, gla_b_alpha, gla_norm_g, w_branch, w_mix_out, norm_x_g, mem_norm_g, w_xq, w_xkv, w_xo, norm_ffn_g, w_group_router, b_group_router, w_expert_router, b_expert_router, w_exp_gate, w_exp_up, w_exp_down, final_norm_g):
    batch, seq, d = x.shape
    depth = w_in.shape[0]
    t = batch * seq
    assert d == D_MODEL and seq % ROW_TILE == 0
    nt = t // ROW_TILE
    n_tiles = nt * LOCAL_CHUNKS // TILE_CHUNKS + N_EXPERTS

    xf = x.reshape(t, d)
    km_all, vm_all = _memkv(mem, mem_norm_g, w_xkv.astype(BF16))
    row = lambda a: a.reshape(1, -1).astype(F32)

    for l in range(depth):
        wi = w_in[l]
        c0 = A_WIDTH * 3 + B_KEY_WIDTH * 2 + B_VAL_WIDTH
        w_main = jnp.concatenate([wi[:, :c0], wi[:, c0 + B_GATE_RANK:]], axis=1).astype(BF16)
        w_al = jnp.pad(wi[:, c0:c0 + B_GATE_RANK], ((0, 0), (0, LANES - B_GATE_RANK))).astype(BF16)
        w_al2 = jnp.pad(gla_w_alpha[l], ((0, LANES - B_GATE_RANK), (0, 0))).astype(BF16)
        aq, ak, av, bq, bk, bv, lga, br, gates = _inproj(
            xf, row(norm_mix_g[l]), w_main, w_al, w_al2, row(gla_b_alpha[l]))

        oa = _band_attention(aq, ak, av, _band_bias(rel_bias[l]), batch)
        ob = _gla(bq, bk, bv, lga, br, row(gla_norm_g[l]), batch)

        wr = jnp.pad(jnp.concatenate([w_group_router[l], w_expert_router[l]], axis=1).astype(F32),
                     ((0, 0), (0, LANES - N_GROUPS - N_EXPERTS)))
        brt = jnp.pad(jnp.concatenate([b_group_router[l], b_expert_router[l]]).astype(F32),
                      (0, LANES - N_GROUPS - N_EXPERTS)).reshape(1, LANES)
        x2, hs, ws, route, cnt = _token(
            xf, oa, ob, gates, w_branch[l, 0].astype(BF16), w_branch[l, 1].astype(BF16),
            w_mix_out[l].astype(BF16), row(norm_x_g[l]), w_xq[l].astype(BF16), km_all[l], vm_all[l],
            w_xo[l].astype(BF16), row(norm_ffn_g[l]), wr, brt, batch)

        plan = _chunk_plan(cnt[:, 0, :N_EXPERTS].astype(I32), n_tiles)
        e3 = lambda w: w[l].reshape((N_EXPERTS,) + w.shape[3:]).astype(BF16)
        ys = _experts(hs, ws, *plan, e3(w_exp_gate), e3(w_exp_up), e3(w_exp_down))
        xf = _combine(x2, route, ys, row(final_norm_g), l == depth - 1)

    return xf.reshape(batch, seq, d)
```

```python
import functools

import numpy as np
import jax
import jax.numpy as jnp
from jax import lax
from jax.experimental import pallas as pl
from jax.experimental.pallas import tpu as pltpu

F32 = jnp.float32
BF16 = jnp.bfloat16
I32 = jnp.int32

D_MODEL = 1024
CHUNK = 64
EPS = 1e-6
A_HEADS = 8
A_HEAD_DIM = 64
A_WIDTH = 512
A_LEFT_CHUNKS = 8
A_MAX_REL = 256
B_HEADS = 4
B_KEY_DIM = 64
B_VAL_DIM = 128
B_KEY_WIDTH = 256
B_VAL_WIDTH = 512
B_GATE_RANK = 16
B_GATE_TAU = 16.0
X_HEADS = 4
X_HEAD_DIM = 256
N_GROUPS = 4
EXPERTS_PER_GROUP = 8
N_EXPERTS = N_GROUPS * EXPERTS_PER_GROUP
EXPERT_FF = 256

LANES = 128
ROW_TILE = 256
CHUNKS_PER_TILE = ROW_TILE // CHUNK
BAND_TILES = A_LEFT_CHUNKS // CHUNKS_PER_TILE + 1
BAND_KEYS = BAND_TILES * ROW_TILE
LOG_CHUNK = 6
N_LEVELS = LOG_CHUNK
EXP_ROWS = (2 + N_LEVELS) * CHUNK
CHUNK_ROWS = 16
EXPERT_ROWS = 512
TILE_CHUNKS = EXPERT_ROWS // CHUNK_ROWS
LOCAL_CHUNKS = 2 * ROW_TILE // CHUNK_ROWS + N_EXPERTS
LOCAL_ROWS = LOCAL_CHUNKS * CHUNK_ROWS
NEG = -1e30
VMEM_LIMIT = 56 * 1024 * 1024


def _params(*sem):
    return pltpu.CompilerParams(dimension_semantics=sem, vmem_limit_bytes=VMEM_LIMIT)


def _rms(x, g):
    return x * lax.rsqrt(jnp.mean(x * x, axis=-1, keepdims=True) + EPS) * g


def _dot(a, b):
    return jnp.dot(a, b, preferred_element_type=F32)


def _dot_nt(a, b):
    return lax.dot_general(a, b, (((1,), (1,)), ((), ())), preferred_element_type=F32)


def _dot_tn(a, b):
    return lax.dot_general(a, b, (((0,), (0,)), ((), ())), preferred_element_type=F32)


def _memkv_kernel(mem_ref, g_ref, w_ref, k_ref, v_ref):
    mn = _rms(mem_ref[0], g_ref[...]).astype(BF16)
    kv = _dot(mn, w_ref[0])
    k_ref[0, 0] = kv[:, :D_MODEL].astype(BF16)
    v_ref[0, 0] = kv[:, D_MODEL:].astype(BF16)


def _memkv(mem, g, w_xkv):
    depth = w_xkv.shape[0]
    b, m, d = mem.shape
    out = jax.ShapeDtypeStruct((depth, b, m, d), BF16)
    return pl.pallas_call(
        _memkv_kernel,
        grid=(depth, b),
        in_specs=[pl.BlockSpec((1, m, d), lambda l, i: (i, 0, 0)),
                  pl.BlockSpec((1, d), lambda l, i: (0, 0)),
                  pl.BlockSpec((1, d, 2 * d), lambda l, i: (l, 0, 0))],
        out_specs=[pl.BlockSpec((1, 1, m, d), lambda l, i: (l, i, 0, 0)),
                   pl.BlockSpec((1, 1, m, d), lambda l, i: (l, i, 0, 0))],
        out_shape=[out, out],
        compiler_params=_params("arbitrary", "arbitrary"),
        name="memkv",
    )(mem, g.reshape(1, d), w_xkv)


_C_AQ, _C_AK, _C_AV = 0, 512, 1024
_C_BQ, _C_BK, _C_BV = 1536, 1792, 2048
_C_BR, _C_GATE, _C_END = 2560, 3072, 5120


def _inproj_kernel(x_ref, g_ref, w_ref, wal_ref, wal2_ref, bal_ref,
                   aq_ref, ak_ref, av_ref, bq_ref, bk_ref, bv_ref, lga_ref, br_ref, gate_ref):
    h = _rms(x_ref[...], g_ref[...]).astype(BF16)

    def mm(lo, hi):
        return _dot(h, w_ref[:, lo:hi])

    aq_ref[...] = (mm(_C_AQ, _C_AK) * (A_HEAD_DIM ** -0.5)).astype(BF16)
    ak_ref[...] = mm(_C_AK, _C_AV).astype(BF16)
    av_ref[...] = mm(_C_AV, _C_BQ).astype(BF16)
    bq_ref[...] = (mm(_C_BQ, _C_BK) * (B_KEY_DIM ** -0.5)).astype(BF16)
    bk_ref[...] = mm(_C_BK, _C_BV).astype(BF16)
    bv_ref[...] = mm(_C_BV, _C_BR).astype(BF16)
    r = mm(_C_BR, _C_GATE)
    br_ref[...] = (r * jax.nn.sigmoid(r)).astype(BF16)
    for c in range(_C_GATE, _C_END, 512):
        gate_ref[:, c - _C_GATE:c - _C_GATE + 512] = jax.nn.sigmoid(mm(c, c + 512)).astype(BF16)
    z = _dot(_dot(h, wal_ref[...]).astype(BF16), wal2_ref[...]) + bal_ref[...]
    lga_ref[...] = (jnp.minimum(z, 0.0) - jnp.log(1.0 + jnp.exp(-jnp.abs(z)))) * (1.0 / B_GATE_TAU)


def _inproj(x, g, w_main, w_al, w_al2, b_al):
    t, d = x.shape
    row = lambda w: pl.BlockSpec((ROW_TILE, w), lambda i: (i, 0))
    full = lambda a: pl.BlockSpec(a.shape, lambda i: (0,) * a.ndim)
    sds = lambda w, dt: jax.ShapeDtypeStruct((t, w), dt)
    widths = [(512, BF16), (512, BF16), (512, BF16), (256, BF16), (256, BF16), (512, BF16),
              (256, F32), (512, BF16), (2048, BF16)]
    return pl.pallas_call(
        _inproj_kernel,
        grid=(t // ROW_TILE,),
        in_specs=[row(d), full(g), full(w_main), full(w_al), full(w_al2), full(b_al)],
        out_specs=[row(w) for w, _ in widths],
        out_shape=[sds(w, dt) for w, dt in widths],
        compiler_params=_params("arbitrary"),
        name="inproj",
    )(x, g, w_main, w_al, w_al2, b_al)


def _band_kernel(q_ref, k0_ref, k1_ref, k2_ref, v0_ref, v1_ref, v2_ref, bias_ref, o_ref):
    i = pl.program_id(1)
    col = lax.broadcasted_iota(I32, (1, BAND_KEYS), 1)
    n_missing = jnp.maximum(BAND_TILES - 1 - i, 0)
    pen = jnp.where(col < n_missing * ROW_TILE, NEG, 0.0).astype(F32)
    lane = lax.broadcasted_iota(I32, (1, LANES), 1)
    low = lane < A_HEAD_DIM
    for p in range(A_HEADS // 2):
        sl = slice(p * LANES, (p + 1) * LANES)
        qp = q_ref[:, sl]
        kp = jnp.concatenate([k0_ref[:, sl], k1_ref[:, sl], k2_ref[:, sl]], axis=0)
        vp = jnp.concatenate([v0_ref[:, sl], v1_ref[:, sl], v2_ref[:, sl]], axis=0)
        outs = []
        for e in range(2):
            qm = jnp.where(low if e == 0 else jnp.logical_not(low), qp, jnp.zeros_like(qp))
            s = _dot_nt(qm, kp) + bias_ref[2 * p + e] + pen
            m = jnp.max(s, axis=-1, keepdims=True)
            pe = jnp.exp(s - m)
            l = jnp.sum(pe, axis=-1, keepdims=True)
            outs.append(_dot(pe.astype(BF16), vp) * (1.0 / l))
        o_ref[:, sl] = jnp.where(low, outs[0], outs[1]).astype(BF16)


def _band_bias(rel_table):
    h = rel_table.shape[0]
    tab = rel_table.astype(F32)
    shift = A_LEFT_CHUNKS * CHUNK + ROW_TILE - 1
    n_far = shift - A_MAX_REL + 1
    span = ROW_TILE + BAND_KEYS - 1
    assert span - 1 - shift <= A_MAX_REL
    u = jnp.concatenate([jnp.broadcast_to(tab[:, 2 * A_MAX_REL:], (h, n_far)),
                         tab[:, 2 * A_MAX_REL - 1:2 * A_MAX_REL - 1 - (span - n_far):-1]], axis=1)
    period = span + 1
    u = jnp.pad(u, ((0, 0), (0, period - span)))
    rows = jnp.tile(u, (1, ROW_TILE + 1))[:, :ROW_TILE * (period + 1)].reshape(h, ROW_TILE, period + 1)
    bias = rows[:, ::-1, :BAND_KEYS]
    q = np.arange(ROW_TILE)[:, None]
    k = np.arange(BAND_KEYS)[None, :]
    cq, ck = q // CHUNK, k // CHUNK
    valid = (ck >= cq) & (ck <= cq + A_LEFT_CHUNKS)
    return jnp.where(valid[None], bias, NEG)


def _band_attention(q, k, v, bias, batch):
    t, w = q.shape
    nb = t // batch // ROW_TILE
    cur = lambda b, i: (b * nb + i, 0)
    back = lambda n: (lambda b, i: (b * nb + jnp.maximum(i - n, 0), 0))
    blk = lambda f: pl.BlockSpec((ROW_TILE, w), f)
    return pl.pallas_call(
        _band_kernel,
        grid=(batch, nb),
        in_specs=[blk(cur), blk(back(2)), blk(back(1)), blk(cur), blk(back(2)), blk(back(1)), blk(cur),
                  pl.BlockSpec(bias.shape, lambda b, i: (0, 0, 0))],
        out_specs=blk(cur),
        out_shape=jax.ShapeDtypeStruct((t, w), BF16),
        compiler_params=_params("arbitrary", "arbitrary"),
        name="band_attn",
    )(q, k, k, k, v, v, v, bias)


def _gla_constants():
    c = CHUNK
    t = np.arange(c)[:, None]
    r = np.arange(c)[None, :]
    mats = [(r <= t), (r > t)]
    lvl = np.full((c, c), -1, np.int32)
    lvl[np.arange(c), np.arange(c)] = N_LEVELS
    for l in range(N_LEVELS):
        m = (c // 2) >> l
        mid = (t // (2 * m)) * (2 * m) + m
        upper = t >= mid
        mats.append(np.where(upper, (r >= mid) & (r <= t), (r > t) & (r < mid)))
        s = r
        same = (s // (2 * m)) == (t // (2 * m))
        lvl[np.asarray(same & upper & (s < mid))] = l
    mexp = np.concatenate(mats, axis=0).astype(np.float32)
    lvl = np.tile(lvl, (1, B_HEADS))
    return jnp.asarray(mexp, BF16), jnp.asarray(lvl, I32)


def _gla_kernel(q_ref, k_ref, v_ref, g_ref, r_ref, gn_ref, mexp_ref, lvl_ref, o_ref, s_ref):
    @pl.when(pl.program_id(1) == 0)
    def _():
        s_ref[...] = jnp.zeros_like(s_ref)

    kw = B_KEY_WIDTH
    ri = lax.broadcasted_iota(I32, (kw, kw), 0) >> LOG_CHUNK
    ci = lax.broadcasted_iota(I32, (kw, kw), 1) >> LOG_CHUNK
    bd = ri == ci
    head_ind = jnp.where(bd, 1.0, 0.0).astype(BF16)
    ri2 = lax.broadcasted_iota(I32, (kw, 2 * kw), 0) >> LOG_CHUNK
    ci2 = (lax.broadcasted_iota(I32, (kw, 2 * kw), 1) & (kw - 1)) >> LOG_CHUNK
    bd2 = ri2 == ci2
    lvl = lvl_ref[...]
    mexp = mexp_ref[...]
    row8 = lax.broadcasted_iota(I32, (16, kw), 0)
    ones = jnp.ones((16, LANES), BF16)
    zero_b = jnp.zeros((kw, kw), BF16)

    for c in range(CHUNKS_PER_TILE):
        rows = slice(c * CHUNK, (c + 1) * CHUNK)
        q = q_ref[rows, :].astype(F32)
        k = k_ref[rows, :].astype(F32)
        v = v_ref[rows, :]
        g = g_ref[rows, :]
        g1 = g.astype(BF16)
        g2 = (g - g1.astype(F32)).astype(BF16)
        ex = _dot(mexp, jnp.concatenate([g1, g2], axis=1))
        w = jnp.exp(ex[:, :kw] + ex[:, kw:])
        w_cum = w[0:CHUNK]
        w_rev = w[CHUNK:2 * CHUNK]
        qt = (q * w_cum).astype(BF16)
        kb = (k * w_rev).astype(BF16)

        a = jnp.zeros((CHUNK, kw), F32)
        for l in range(N_LEVELS):
            wl = w[(2 + l) * CHUNK:(3 + l) * CHUNK]
            qh = (q * wl).astype(BF16)
            kh = (k * wl).astype(BF16)
            x = jnp.where(bd, jnp.concatenate([kh] * B_HEADS, axis=0), zero_b)
            a = jnp.where(lvl == l, _dot_nt(qh, x), a)
        a = jnp.where(lvl == N_LEVELS, _dot((q * k).astype(BF16), head_ind), a)

        s_old = s_ref[...]
        lhs = jnp.concatenate([a.astype(BF16), qt], axis=1)
        lhs = jnp.where(bd2, jnp.concatenate([lhs] * B_HEADS, axis=0), jnp.zeros((kw, 2 * kw), BF16))
        vstack = jnp.concatenate([v[:, j * LANES:(j + 1) * LANES] for j in range(B_HEADS)], axis=0)
        rhs = jnp.concatenate([vstack, s_old.astype(BF16)], axis=0)
        o = _dot(lhs, rhs)

        xk = jnp.where(bd, jnp.concatenate([kb] * B_HEADS, axis=0), zero_b)
        kv = _dot_tn(xk, vstack)
        d = w_cum[CHUNK - 1:CHUNK]
        d1 = d.astype(BF16).astype(F32)
        dp = jnp.where(row8 == 0, d1, jnp.where(row8 == 1, d - d1, 0.0)).astype(BF16)
        dcol = _dot_tn(dp, ones)
        s_ref[...] = dcol * s_old + kv

        for j in range(B_HEADS):
            oj = o[j * CHUNK:(j + 1) * CHUNK]
            sl = slice(j * LANES, (j + 1) * LANES)
            y = oj * lax.rsqrt(jnp.mean(oj * oj, axis=-1, keepdims=True) + EPS) * gn_ref[...]
            o_ref[rows, sl] = (y * r_ref[rows, sl].astype(F32)).astype(BF16)


def _gla(q, k, v, g, r, gn, batch):
    t = q.shape[0]
    nb = t // batch // ROW_TILE
    mexp, lvl = _gla_constants()
    cur = lambda b, i: (b * nb + i, 0)
    blk = lambda w: pl.BlockSpec((ROW_TILE, w), cur)
    full = lambda a: pl.BlockSpec(a.shape, lambda b, i: (0,) * a.ndim)
    return pl.pallas_call(
        _gla_kernel,
        grid=(batch, nb),
        in_specs=[blk(B_KEY_WIDTH), blk(B_KEY_WIDTH), blk(B_VAL_WIDTH), blk(B_KEY_WIDTH), blk(B_VAL_WIDTH),
                  full(gn), full(mexp), full(lvl)],
        out_specs=blk(B_VAL_WIDTH),
        out_shape=jax.ShapeDtypeStruct((t, B_VAL_WIDTH), BF16),
        scratch_shapes=[pltpu.VMEM((B_KEY_WIDTH, B_VAL_DIM), F32)],
        compiler_params=_params("arbitrary", "arbitrary"),
        name="gla",
    )(q, k, v, g, r, gn, mexp, lvl)


def _token_kernel(x_ref, oa_ref, ob_ref, gate_ref, wb0_ref, wb1_ref, wmix_ref, gx_ref, wq_ref,
                  km_ref, vm_ref, wo_ref, gf_ref, wr_ref, br_ref, ltri_ref, utri_ref,
                  x2_ref, hs_ref, ws_ref, route_ref, cnt_ref):
    ma = _dot(oa_ref[...], wb0_ref[...])
    mb = _dot(ob_ref[...], wb1_ref[...])
    merged = (gate_ref[:, :D_MODEL].astype(F32) * ma + gate_ref[:, D_MODEL:].astype(F32) * mb).astype(BF16)
    x1 = x_ref[...] + _dot(merged, wmix_ref[...])

    h2 = _rms(x1, gx_ref[...]).astype(BF16)
    qx = (_dot(h2, wq_ref[...]) * (X_HEAD_DIM ** -0.5)).astype(BF16)
    heads = []
    for h in range(X_HEADS):
        sl = slice(h * X_HEAD_DIM, (h + 1) * X_HEAD_DIM)
        s = _dot_nt(qx[:, sl], km_ref[0, :, sl])
        m = jnp.max(s, axis=-1, keepdims=True)
        pe = jnp.exp(s - m)
        l = jnp.sum(pe, axis=-1, keepdims=True)
        heads.append((_dot(pe.astype(BF16), vm_ref[0, :, sl]) * (1.0 / l)).astype(BF16))
    x2 = x1 + _dot(jnp.concatenate(heads, axis=1), wo_ref[...])
    x2_ref[...] = x2

    h3 = _rms(x2, gf_ref[...])

    h3_hi = h3.astype(BF16)
    h3_lo = (h3 - h3_hi.astype(F32)).astype(BF16)
    hw = _dot(h3_hi, wr_ref[...])
    logits = hw[:, :LANES] + hw[:, LANES:] + _dot(h3_lo, wr_ref[:, :LANES]) + br_ref[...]
    lane = lax.broadcasted_iota(I32, logits.shape, 1).astype(F32)
    big = jnp.float32(LANES)
    gl = jnp.where(lane < N_GROUPS, logits, NEG)
    gmax = jnp.max(gl, axis=-1, keepdims=True)
    gidx = jnp.min(jnp.where(gl == gmax, lane, big), axis=-1, keepdims=True)
    g_w = 1.0 / jnp.sum(jnp.exp(gl - gmax), axis=-1, keepdims=True)
    lo = N_GROUPS + EXPERTS_PER_GROUP * gidx
    el = jnp.where((lane >= lo) & (lane < lo + EXPERTS_PER_GROUP), logits, NEG)
    v1 = jnp.max(el, axis=-1, keepdims=True)
    i1 = jnp.min(jnp.where(el == v1, lane, big), axis=-1, keepdims=True)
    el2 = jnp.where(lane == i1, NEG, el)
    v2 = jnp.max(el2, axis=-1, keepdims=True)
    i2 = jnp.min(jnp.where(el2 == v2, lane, big), axis=-1, keepdims=True)
    e21 = jnp.exp(v2 - v1)
    w1 = g_w / (1.0 + e21)
    w2 = w1 * e21
    oh0 = jnp.where(lane == i1 - N_GROUPS, 1.0, 0.0)
    oh1 = jnp.where(lane == i2 - N_GROUPS, 1.0, 0.0)
    oh = oh0 + oh1
    nch = jnp.floor((jnp.sum(oh, axis=0, keepdims=True) + (CHUNK_ROWS - 1)) * (1.0 / CHUNK_ROWS))
    nch8 = jnp.broadcast_to(nch, (8, LANES))
    start = _dot(nch8.astype(BF16), utri_ref[...])[0:1] * CHUNK_ROWS
    rank = _dot(ltri_ref[...], oh.astype(BF16))
    row = start + rank
    pos0 = jnp.sum(row * oh0, axis=-1, keepdims=True)
    pos1 = jnp.sum(row * oh1, axis=-1, keepdims=True)
    route = jnp.where(lane == 0, pos0, jnp.where(lane == 1, pos1, 0.0))
    route_t = jnp.transpose(route)
    r = lax.broadcasted_iota(I32, (LOCAL_ROWS, ROW_TILE), 0).astype(F32)
    p0 = jnp.where(r == route_t[0:1, :], 1.0, 0.0).astype(BF16)
    p1 = jnp.where(r == route_t[1:2, :], 1.0, 0.0).astype(BF16)
    hs_ref[...] = _dot(p0 + p1, h3_hi).astype(BF16).reshape(hs_ref.shape)

    def gate_cols(w):
        hi = w.astype(BF16).astype(F32)
        return jnp.where(lane == 0, hi, jnp.where(lane == 1, w - hi, 0.0)).astype(BF16)

    ws_ref[...] = (_dot(p0, gate_cols(w1)) + _dot(p1, gate_cols(w2))).astype(BF16).reshape(ws_ref.shape)
    route_ref[...] = route
    cnt_ref[0] = nch8


def _token(x, oa, ob, gates, wb0, wb1, wmix, gx, wq, km, vm, wo, gf, wr, br, batch):
    t, d = x.shape
    nb = t // batch // ROW_TILE
    nt = t // ROW_TILE
    ltri = jnp.asarray(np.tril(np.ones((ROW_TILE, ROW_TILE), np.float32), -1), BF16)
    utri = jnp.asarray(np.triu(np.ones((LANES, LANES), np.float32), 1), BF16)
    cur = lambda b, i: (b * nb + i, 0)
    cur3 = lambda b, i: (b * nb + i, 0, 0)
    blk = lambda w: pl.BlockSpec((ROW_TILE, w), cur)
    full = lambda a: pl.BlockSpec(a.shape, lambda b, i: (0,) * a.ndim)
    mem = pl.BlockSpec((1,) + km.shape[1:], lambda b, i: (b, 0, 0))
    return pl.pallas_call(
        _token_kernel,
        grid=(batch, nb),
        in_specs=[blk(d), blk(A_WIDTH), blk(B_VAL_WIDTH), blk(2 * d), full(wb0), full(wb1), full(wmix),
                  full(gx), full(wq), mem, mem, full(wo), full(gf), full(wr), full(br), full(ltri), full(utri)],
        out_specs=[blk(d), pl.BlockSpec((LOCAL_CHUNKS, CHUNK_ROWS, d), cur3),
                   pl.BlockSpec((LOCAL_CHUNKS, CHUNK_ROWS, LANES), cur3),
                   blk(LANES), pl.BlockSpec((1, 8, LANES), cur3)],
        out_shape=[jax.ShapeDtypeStruct((t, d), F32),
                   jax.ShapeDtypeStruct((nt * LOCAL_CHUNKS, CHUNK_ROWS, d), BF16),
                   jax.ShapeDtypeStruct((nt * LOCAL_CHUNKS, CHUNK_ROWS, LANES), BF16),
                   jax.ShapeDtypeStruct((t, LANES), F32),
                   jax.ShapeDtypeStruct((nt, 8, LANES), F32)],
        compiler_params=_params("arbitrary", "arbitrary"),
        name="token",
    )(x, oa, ob, gates, wb0, wb1, wmix, gx, wq, km, vm, wo, gf, wr, br, ltri, utri)


def _expert_kernel(te_ref, nu_ref, nv_ref, ch_ref, hs_hbm, ws_hbm, wg_ref, wu_ref, wd_ref, yinit_hbm, ys_hbm,
                   xbuf, gbuf, ybuf, gsem, ssem):
    del yinit_hbm
    i = pl.program_id(0)
    n_used = nu_ref[0]
    slot = lax.rem(i, 2)

    def gather(tile, s, start):
        def body(c, carry):
            ch = ch_ref[tile * TILE_CHUNKS + c]
            for cp in (pltpu.make_async_copy(hs_hbm.at[ch], xbuf.at[s, c], gsem.at[s]),
                       pltpu.make_async_copy(ws_hbm.at[ch], gbuf.at[s, c], gsem.at[s])):
                cp.start() if start else cp.wait()
            return carry

        lax.fori_loop(0, nv_ref[tile], body, 0)

    def scatter(tile, s, start):
        def body(c, carry):
            ch = ch_ref[tile * TILE_CHUNKS + c]
            cp = pltpu.make_async_copy(ybuf.at[s, c], ys_hbm.at[ch], ssem.at[s])
            cp.start() if start else cp.wait()
            return carry

        lax.fori_loop(0, nv_ref[tile], body, 0)

    @pl.when(i == 0)
    def _():
        xbuf[...] = jnp.zeros_like(xbuf)
        gbuf[...] = jnp.zeros_like(gbuf)
        gather(0, 0, True)

    @pl.when(i < n_used)
    def _():
        @pl.when(i + 1 < n_used)
        def _():
            gather(i + 1, 1 - slot, True)

        gather(i, slot, False)

        @pl.when(i >= 2)
        def _():
            scatter(i - 2, slot, False)

        x = xbuf[slot].reshape(EXPERT_ROWS, D_MODEL)
        hg = _dot(x, wg_ref[0])
        hu = _dot(x, wu_ref[0])
        hid = (hg * jax.nn.sigmoid(hg) * hu).astype(BF16)
        g = gbuf[slot].reshape(EXPERT_ROWS, LANES).astype(F32)
        gate = g[:, 0:1] + g[:, 1:2]
        ybuf[slot] = (gate * _dot(hid, wd_ref[0])).astype(BF16).reshape(TILE_CHUNKS, CHUNK_ROWS, D_MODEL)
        scatter(i, slot, True)

        @pl.when(i == n_used - 1)
        def _():
            scatter(i, slot, False)

            @pl.when(i >= 1)
            def _():
                scatter(i - 1, 1 - slot, False)


def _experts(hs, ws, tile_expert, n_used, n_valid, chunks, wg, wu, wd):
    n_tiles = tile_expert.shape[0]
    last = lambda i, te, nu, nv, ch: jnp.minimum(i, nu[0] - 1)
    wmap = lambda i, te, nu, nv, ch: (te[last(i, te, nu, nv, ch)], 0, 0)
    anyspace = pl.BlockSpec(memory_space=pl.ANY)
    grid_spec = pltpu.PrefetchScalarGridSpec(
        num_scalar_prefetch=4,
        grid=(n_tiles,),
        in_specs=[anyspace, anyspace,
                  pl.BlockSpec((1, D_MODEL, EXPERT_FF), wmap),
                  pl.BlockSpec((1, D_MODEL, EXPERT_FF), wmap),
                  pl.BlockSpec((1, EXPERT_FF, D_MODEL), wmap),
                  anyspace],
        out_specs=anyspace,
        scratch_shapes=[pltpu.VMEM((2, TILE_CHUNKS, CHUNK_ROWS, D_MODEL), BF16),
                        pltpu.VMEM((2, TILE_CHUNKS, CHUNK_ROWS, LANES), BF16),
                        pltpu.VMEM((2, TILE_CHUNKS, CHUNK_ROWS, D_MODEL), BF16),
                        pltpu.SemaphoreType.DMA((2,)), pltpu.SemaphoreType.DMA((2,))],
    )
    return pl.pallas_call(
        _expert_kernel,
        grid_spec=grid_spec,
        out_shape=jax.ShapeDtypeStruct(hs.shape, BF16),
        input_output_aliases={9: 0},
        compiler_params=_params("arbitrary"),
        name="experts",
    )(tile_expert, n_used, n_valid, chunks, hs, ws, wg, wu, wd, jnp.zeros(hs.shape, BF16))


def _combine_kernel(final_norm, x_ref, route_ref, ys_ref, gfin_ref, o_ref):
    r = lax.broadcasted_iota(I32, (ROW_TILE, LOCAL_ROWS), 1).astype(F32)
    sel = jnp.where(r == route_ref[:, 0:1], 1.0, jnp.where(r == route_ref[:, 1:2], 1.0, 0.0)).astype(BF16)
    x3 = x_ref[...] + _dot(sel, ys_ref[...].reshape(LOCAL_ROWS, D_MODEL))
    if final_norm:
        x3 = _rms(x3, gfin_ref[...])
    o_ref[...] = x3


def _combine(x2, route, ys, gfin, final_norm):
    t, d = x2.shape
    nt = t // ROW_TILE
    return pl.pallas_call(
        functools.partial(_combine_kernel, final_norm),
        grid=(nt,),
        in_specs=[pl.BlockSpec((ROW_TILE, d), lambda i: (i, 0)),
                  pl.BlockSpec((ROW_TILE, LANES), lambda i: (i, 0)),
                  pl.BlockSpec((LOCAL_CHUNKS, CHUNK_ROWS, d), lambda i: (i, 0, 0)),
                  pl.BlockSpec((1, d), lambda i: (0, 0))],
        out_specs=pl.BlockSpec((ROW_TILE, d), lambda i: (i, 0)),
        out_shape=jax.ShapeDtypeStruct((t, d), F32),
        compiler_params=_params("arbitrary"),
        name="combine",
    )(x2, route, ys, gfin)


def _chunk_plan(nch, n_tiles):
    nt = nch.shape[0]
    local_start = jnp.cumsum(nch, axis=1) - nch
    cum = jnp.cumsum(nch, axis=0)
    total = cum[-1]
    tiles = (total + TILE_CHUNKS - 1) // TILE_CHUNKS
    tile_end = jnp.cumsum(tiles)
    n_used = tile_end[-1:]
    tile_ids = jnp.arange(n_tiles, dtype=I32)
    tile_expert = jnp.minimum(jnp.sum((tile_end[None, :] <= tile_ids[:, None]).astype(I32), axis=1),
                              N_EXPERTS - 1)
    sel = (tile_expert[:, None] == jnp.arange(N_EXPERTS, dtype=I32)[None, :]).astype(I32)
    pick = lambda table: jnp.sum(sel[:, :, None] * table.T[None, :, :], axis=1)
    first_tile = jnp.sum(sel * (tile_end - tiles)[None, :], axis=1)
    slot = (tile_ids - first_tile)[:, None] * TILE_CHUNKS + jnp.arange(TILE_CHUNKS, dtype=I32)[None, :]
    valid = (slot < jnp.sum(sel * total[None, :], axis=1)[:, None]) & (tile_ids < n_used)[:, None]
    src_tile = jnp.sum((pick(cum)[:, None, :] <= slot[:, :, None]).astype(I32), axis=2)
    src_tile = jnp.minimum(src_tile, nt - 1)
    at = (src_tile[:, :, None] == jnp.arange(nt, dtype=I32)[None, None, :]).astype(I32)
    before = jnp.sum(at * pick(cum - nch)[:, None, :], axis=2)
    start = jnp.sum(at * pick(local_start)[:, None, :], axis=2)
    chunk = jnp.where(valid, src_tile * LOCAL_CHUNKS + start + slot - before, 0)
    return tile_expert, n_used, jnp.sum(valid.astype(I32), axis=1), chunk.reshape(-1)


def kernel(x, mem, norm_mix_g, w_in, rel_bias, gla_w_alpha, gla_b_alpha, gla_norm_g, w_branch, w_mix_out, norm_x_g, mem_norm_g, w_xq, w_xkv, w_xo, norm_ffn_g, w_group_router, b_group_router, w_expert_router, b_expert_router, w_exp_gate, w_exp_up, w_exp_down, final_norm_g):
    batch, seq, d = x.shape
    depth = w_in.shape[0]
    t = batch * seq
    assert d == D_MODEL and seq % ROW_TILE == 0
    nt = t // ROW_TILE
    n_tiles = nt * LOCAL_CHUNKS // TILE_CHUNKS + N_EXPERTS

    xf = x.reshape(t, d)
    km_all, vm_all = _memkv(mem, mem_norm_g, w_xkv.astype(BF16))
    row = lambda a: a.reshape(1, -1).astype(F32)

    for l in range(depth):
        wi = w_in[l]
        c0 = A_WIDTH * 3 + B_KEY_WIDTH * 2 + B_VAL_WIDTH
        w_main = jnp.concatenate([wi[:, :c0], wi[:, c0 + B_GATE_RANK:]], axis=1).astype(BF16)
        w_al = jnp.pad(wi[:, c0:c0 + B_GATE_RANK], ((0, 0), (0, LANES - B_GATE_RANK))).astype(BF16)
        w_al2 = jnp.pad(gla_w_alpha[l], ((0, LANES - B_GATE_RANK), (0, 0))).astype(BF16)
        aq, ak, av, bq, bk, bv, lga, br, gates = _inproj(
            xf, row(norm_mix_g[l]), w_main, w_al, w_al2, row(gla_b_alpha[l]))

        oa = _band_attention(aq, ak, av, _band_bias(rel_bias[l]), batch)
        ob = _gla(bq, bk, bv, lga, br, row(gla_norm_g[l]), batch)

        wr = jnp.pad(jnp.concatenate([w_group_router[l], w_expert_router[l]], axis=1).astype(F32),
                     ((0, 0), (0, LANES - N_GROUPS - N_EXPERTS)))
        wr_hi = wr.astype(BF16)
        wr = jnp.concatenate([wr_hi, (wr - wr_hi.astype(F32)).astype(BF16)], axis=1)
        brt = jnp.pad(jnp.concatenate([b_group_router[l], b_expert_router[l]]).astype(F32),
                      (0, LANES - N_GROUPS - N_EXPERTS)).reshape(1, LANES)
        x2, hs, ws, route, cnt = _token(
            xf, oa, ob, gates, w_branch[l, 0].astype(BF16), w_branch[l, 1].astype(BF16),
            w_mix_out[l].astype(BF16), row(norm_x_g[l]), w_xq[l].astype(BF16), km_all[l], vm_all[l],
            w_xo[l].astype(BF16), row(norm_ffn_g[l]), wr, brt, batch)

        plan = _chunk_plan(cnt[:, 0, :N_EXPERTS].astype(I32), n_tiles)
        e3 = lambda w: w[l].reshape((N_EXPERTS,) + w.shape[3:]).astype(BF16)
        ys = _experts(hs, ws, *plan, e3(w_exp_gate), e3(w_exp_up), e3(w_exp_down))
        xf = _combine(x2, route, ys, row(final_norm_g), l == depth - 1)

    return xf.reshape(batch, seq, d)
```

```python
import functools

import numpy as np
import jax
import jax.numpy as jnp
from jax import lax
from jax.experimental import pallas as pl
from jax.experimental.pallas import tpu as pltpu

F32 = jnp.float32
BF16 = jnp.bfloat16
I32 = jnp.int32

D_MODEL = 1024
CHUNK = 64
EPS = 1e-6
A_HEADS = 8
A_HEAD_DIM = 64
A_WIDTH = 512
A_LEFT_CHUNKS = 8
A_MAX_REL = 256
B_HEADS = 4
B_KEY_DIM = 64
B_VAL_DIM = 128
B_KEY_WIDTH = 256
B_VAL_WIDTH = 512
B_GATE_RANK = 16
B_GATE_TAU = 16.0
X_HEADS = 4
X_HEAD_DIM = 256
N_GROUPS = 4
EXPERTS_PER_GROUP = 8
N_EXPERTS = N_GROUPS * EXPERTS_PER_GROUP
EXPERT_FF = 256

LANES = 128
ROW_TILE = 256
CHUNKS_PER_TILE = ROW_TILE // CHUNK
BAND_TILES = A_LEFT_CHUNKS // CHUNKS_PER_TILE + 1
BAND_KEYS = BAND_TILES * ROW_TILE
LOG_CHUNK = 6
N_LEVELS = LOG_CHUNK
EXP_ROWS = (2 + N_LEVELS) * CHUNK
CHUNK_ROWS = 16
TOKEN_TILES_PER_STEP = 2
INPROJ_ROWS = 512
EXPERT_ROWS = 512
TILE_CHUNKS = EXPERT_ROWS // CHUNK_ROWS
LOCAL_CHUNKS = 2 * ROW_TILE // CHUNK_ROWS + N_EXPERTS
LOCAL_ROWS = LOCAL_CHUNKS * CHUNK_ROWS
NEG = -1e30
LOG2E = 1.4426950408889634
VMEM_LIMIT = 56 * 1024 * 1024


def _params(*sem):
    return pltpu.CompilerParams(dimension_semantics=sem, vmem_limit_bytes=VMEM_LIMIT)


def _rms(x, g):
    return x * lax.rsqrt(jnp.mean(x * x, axis=-1, keepdims=True) + EPS) * g


def _dot(a, b):
    return jnp.dot(a, b, preferred_element_type=F32)


def _dot_nt(a, b):
    return lax.dot_general(a, b, (((1,), (1,)), ((), ())), preferred_element_type=F32)


def _dot_tn(a, b):
    return lax.dot_general(a, b, (((0,), (0,)), ((), ())), preferred_element_type=F32)


def _memkv_kernel(mem_ref, g_ref, w_ref, k_ref, v_ref):
    mn = _rms(mem_ref[0], g_ref[...]).astype(BF16)
    kv = _dot(mn, w_ref[0])
    k_ref[0, 0] = kv[:, :D_MODEL].astype(BF16)
    v_ref[0, 0] = kv[:, D_MODEL:].astype(BF16)


def _memkv(mem, g, w_xkv):
    depth = w_xkv.shape[0]
    b, m, d = mem.shape
    out = jax.ShapeDtypeStruct((depth, b, m, d), BF16)
    return pl.pallas_call(
        _memkv_kernel,
        grid=(depth, b),
        in_specs=[pl.BlockSpec((1, m, d), lambda l, i: (i, 0, 0)),
                  pl.BlockSpec((1, d), lambda l, i: (0, 0)),
                  pl.BlockSpec((1, d, 2 * d), lambda l, i: (l, 0, 0))],
        out_specs=[pl.BlockSpec((1, 1, m, d), lambda l, i: (l, i, 0, 0)),
                   pl.BlockSpec((1, 1, m, d), lambda l, i: (l, i, 0, 0))],
        out_shape=[out, out],
        compiler_params=_params("arbitrary", "arbitrary"),
        name="memkv",
    )(mem, g.reshape(1, d), w_xkv)


_C_AQ, _C_AK, _C_AV = 0, 512, 1024
_C_BQ, _C_BK, _C_BV = 1536, 1792, 2048
_C_BR, _C_GATE, _C_END = 2560, 3072, 5120


def _inproj_kernel(x_ref, g_ref, w_ref, wal_ref, wal2_ref, bal_ref,
                   aq_ref, ak_ref, av_ref, bq_ref, bk_ref, bv_ref, lga_ref, br_ref, gate_ref):
    h = _rms(x_ref[...], g_ref[...]).astype(BF16)

    def mm(lo, hi):
        return _dot(h, w_ref[:, lo:hi])

    aq_ref[...] = (mm(_C_AQ, _C_AK) * (A_HEAD_DIM ** -0.5 * LOG2E)).astype(BF16)
    ak_ref[...] = mm(_C_AK, _C_AV).astype(BF16)
    av_ref[...] = mm(_C_AV, _C_BQ).astype(BF16)
    bq_ref[...] = (mm(_C_BQ, _C_BK) * (B_KEY_DIM ** -0.5)).astype(BF16)
    bk_ref[...] = mm(_C_BK, _C_BV).astype(BF16)
    bv_ref[...] = mm(_C_BV, _C_BR).astype(BF16)
    r = mm(_C_BR, _C_GATE)
    br_ref[...] = (r * jax.nn.sigmoid(r)).astype(BF16)
    for c in range(_C_GATE, _C_END, 512):
        gate_ref[:, c - _C_GATE:c - _C_GATE + 512] = jax.nn.sigmoid(mm(c, c + 512)).astype(BF16)
    z = _dot(_dot(h, wal_ref[...]).astype(BF16), wal2_ref[...]) + bal_ref[...]
    lga_ref[...] = (jnp.minimum(z, 0.0) - jnp.log(1.0 + jnp.exp(-jnp.abs(z)))) * (1.0 / B_GATE_TAU)


def _inproj(x, g, w_main, w_al, w_al2, b_al):
    t, d = x.shape
    row = lambda w: pl.BlockSpec((INPROJ_ROWS, w), lambda i: (i, 0))
    full = lambda a: pl.BlockSpec(a.shape, lambda i: (0,) * a.ndim)
    sds = lambda w, dt: jax.ShapeDtypeStruct((t, w), dt)
    widths = [(512, BF16), (512, BF16), (512, BF16), (256, BF16), (256, BF16), (512, BF16),
              (256, F32), (512, BF16), (2048, BF16)]
    return pl.pallas_call(
        _inproj_kernel,
        grid=(t // INPROJ_ROWS,),
        in_specs=[row(d), full(g), full(w_main), full(w_al), full(w_al2), full(b_al)],
        out_specs=[row(w) for w, _ in widths],
        out_shape=[sds(w, dt) for w, dt in widths],
        compiler_params=_params("arbitrary"),
        name="inproj",
    )(x, g, w_main, w_al, w_al2, b_al)


def _band_kernel(q_ref, k0_ref, k1_ref, k2_ref, v0_ref, v1_ref, v2_ref, bias_ref, o_ref):
    i = pl.program_id(1)
    lane = lax.broadcasted_iota(I32, (1, LANES), 1)
    low = lane < A_HEAD_DIM
    ones = jnp.ones((BAND_KEYS, LANES), BF16)

    def attend(pen):
        for p in range(A_HEADS // 2):
            sl = slice(p * LANES, (p + 1) * LANES)
            qp = q_ref[:, sl]
            zero = jnp.zeros_like(qp)
            q2 = jnp.concatenate([jnp.where(low, qp, zero), jnp.where(low, zero, qp)], axis=0)
            kp = jnp.concatenate([k0_ref[:, sl], k1_ref[:, sl], k2_ref[:, sl]], axis=0)
            vp = jnp.concatenate([v0_ref[:, sl], v1_ref[:, sl], v2_ref[:, sl]], axis=0)
            s = _dot_nt(q2, kp) + bias_ref[p]
            if pen is not None:
                s = s + pen
            pe = jnp.exp2(s - jnp.max(s, axis=-1, keepdims=True)).astype(BF16)
            o2 = _dot(pe, jnp.concatenate([vp, ones], axis=1))
            o = o2[:, :LANES] * (1.0 / o2[:, LANES:])
            o_ref[:, sl] = jnp.where(low, o[:ROW_TILE], o[ROW_TILE:]).astype(BF16)

    @pl.when(i >= BAND_TILES - 1)
    def _():
        attend(None)

    @pl.when(i < BAND_TILES - 1)
    def _():
        col = lax.broadcasted_iota(I32, (1, BAND_KEYS), 1)
        attend(jnp.where(col < (BAND_TILES - 1 - i) * ROW_TILE, NEG, 0.0).astype(F32))


def _band_bias(rel_table):
    h = rel_table.shape[0]
    tab = rel_table.astype(F32)
    shift = A_LEFT_CHUNKS * CHUNK + ROW_TILE - 1
    n_far = shift - A_MAX_REL + 1
    span = ROW_TILE + BAND_KEYS - 1
    assert span - 1 - shift <= A_MAX_REL
    u = jnp.concatenate([jnp.broadcast_to(tab[:, 2 * A_MAX_REL:], (h, n_far)),
                         tab[:, 2 * A_MAX_REL - 1:2 * A_MAX_REL - 1 - (span - n_far):-1]], axis=1)
    period = span + 1
    u = jnp.roll(jnp.pad(u, ((0, 0), (0, period - span))), -(ROW_TILE - 1), axis=1)
    rows = jnp.tile(u, (1, ROW_TILE))[:, :ROW_TILE * (period - 1)].reshape(h, ROW_TILE, period - 1)
    bias = rows[:, :, :BAND_KEYS]
    q = np.arange(ROW_TILE)[:, None]
    k = np.arange(BAND_KEYS)[None, :]
    cq, ck = q // CHUNK, k // CHUNK
    valid = (ck >= cq) & (ck <= cq + A_LEFT_CHUNKS)
    bias = jnp.where(valid[None], bias * LOG2E, NEG)
    return bias.reshape(h // 2, 2 * ROW_TILE, BAND_KEYS)


def _band_attention(q, k, v, bias, batch):
    t, w = q.shape
    nb = t // batch // ROW_TILE
    cur = lambda b, i: (b * nb + i, 0)
    back = lambda n: (lambda b, i: (b * nb + jnp.maximum(i - n, 0), 0))
    blk = lambda f: pl.BlockSpec((ROW_TILE, w), f)
    return pl.pallas_call(
        _band_kernel,
        grid=(batch, nb),
        in_specs=[blk(cur), blk(back(2)), blk(back(1)), blk(cur), blk(back(2)), blk(back(1)), blk(cur),
                  pl.BlockSpec(bias.shape, lambda b, i: (0, 0, 0))],
        out_specs=blk(cur),
        out_shape=jax.ShapeDtypeStruct((t, w), BF16),
        compiler_params=_params("arbitrary", "arbitrary"),
        name="band_attn",
    )(q, k, k, k, v, v, v, bias)


def _gla_constants():
    c = CHUNK
    t = np.arange(c)[:, None]
    r = np.arange(c)[None, :]
    mats = [(r <= t), (r > t)]
    lvl = np.full((c, c), -1, np.int32)
    lvl[np.arange(c), np.arange(c)] = N_LEVELS
    for l in range(N_LEVELS):
        m = (c // 2) >> l
        mid = (t // (2 * m)) * (2 * m) + m
        upper = t >= mid
        mats.append(np.where(upper, (r >= mid) & (r <= t), (r > t) & (r < mid)))
        s = r
        same = (s // (2 * m)) == (t // (2 * m))
        lvl[np.asarray(same & upper & (s < mid))] = l
    mexp = np.concatenate(mats, axis=0).astype(np.float32)
    lvl = np.tile(lvl, (1, B_HEADS))
    return jnp.asarray(mexp, BF16), jnp.asarray(lvl, I32)


def _gla_kernel(q_ref, k_ref, v_ref, g_ref, r_ref, gn_ref, mexp_ref, lvl_ref, o_ref, s_ref):
    @pl.when(pl.program_id(1) == 0)
    def _():
        s_ref[...] = jnp.zeros_like(s_ref)

    kw = B_KEY_WIDTH
    ri = lax.broadcasted_iota(I32, (kw, kw), 0) >> LOG_CHUNK
    ci = lax.broadcasted_iota(I32, (kw, kw), 1) >> LOG_CHUNK
    bd = ri == ci
    head_ind = jnp.where(bd, 1.0, 0.0).astype(BF16)
    ri2 = lax.broadcasted_iota(I32, (kw, 2 * kw), 0) >> LOG_CHUNK
    ci2 = (lax.broadcasted_iota(I32, (kw, 2 * kw), 1) & (kw - 1)) >> LOG_CHUNK
    bd2 = ri2 == ci2
    lvl = lvl_ref[...]
    mexp = mexp_ref[...]
    row8 = lax.broadcasted_iota(I32, (16, kw), 0)
    ones = jnp.ones((16, LANES), BF16)
    zero_b = jnp.zeros((kw, kw), BF16)

    for c in range(CHUNKS_PER_TILE):
        rows = slice(c * CHUNK, (c + 1) * CHUNK)
        q = q_ref[rows, :].astype(F32)
        k = k_ref[rows, :].astype(F32)
        v = v_ref[rows, :]
        g = g_ref[rows, :]
        g1 = g.astype(BF16)
        g2 = (g - g1.astype(F32)).astype(BF16)
        ex = _dot(mexp, jnp.concatenate([g1, g2], axis=1))
        w = jnp.exp(ex[:, :kw] + ex[:, kw:])
        w_cum = w[0:CHUNK]
        w_rev = w[CHUNK:2 * CHUNK]
        qt = (q * w_cum).astype(BF16)
        kb = (k * w_rev).astype(BF16)

        a = jnp.zeros((CHUNK, kw), F32)
        for l in range(N_LEVELS):
            wl = w[(2 + l) * CHUNK:(3 + l) * CHUNK]
            qh = (q * wl).astype(BF16)
            kh = (k * wl).astype(BF16)
            x = jnp.where(bd, jnp.concatenate([kh] * B_HEADS, axis=0), zero_b)
            a = jnp.where(lvl == l, _dot_nt(qh, x), a)
        a = jnp.where(lvl == N_LEVELS, _dot((q * k).astype(BF16), head_ind), a)

        s_old = s_ref[...]
        lhs = jnp.concatenate([a.astype(BF16), qt], axis=1)
        lhs = jnp.where(bd2, jnp.concatenate([lhs] * B_HEADS, axis=0), jnp.zeros((kw, 2 * kw), BF16))
        vstack = jnp.concatenate([v[:, j * LANES:(j + 1) * LANES] for j in range(B_HEADS)], axis=0)
        rhs = jnp.concatenate([vstack, s_old.astype(BF16)], axis=0)
        o = _dot(lhs, rhs)

        xk = jnp.where(bd, jnp.concatenate([kb] * B_HEADS, axis=0), zero_b)
        kv = _dot_tn(xk, vstack)
        d = w_cum[CHUNK - 1:CHUNK]
        d1 = d.astype(BF16).astype(F32)
        dp = jnp.where(row8 == 0, d1, jnp.where(row8 == 1, d - d1, 0.0)).astype(BF16)
        dcol = _dot_tn(dp, ones)
        s_ref[...] = dcol * s_old + kv

        for j in range(B_HEADS):
            oj = o[j * CHUNK:(j + 1) * CHUNK]
            sl = slice(j * LANES, (j + 1) * LANES)
            y = oj * lax.rsqrt(jnp.mean(oj * oj, axis=-1, keepdims=True) + EPS) * gn_ref[...]
            o_ref[rows, sl] = (y * r_ref[rows, sl].astype(F32)).astype(BF16)


def _gla(q, k, v, g, r, gn, batch):
    t = q.shape[0]
    nb = t // batch // ROW_TILE
    mexp, lvl = _gla_constants()
    cur = lambda b, i: (b * nb + i, 0)
    blk = lambda w: pl.BlockSpec((ROW_TILE, w), cur)
    full = lambda a: pl.BlockSpec(a.shape, lambda b, i: (0,) * a.ndim)
    return pl.pallas_call(
        _gla_kernel,
        grid=(batch, nb),
        in_specs=[blk(B_KEY_WIDTH), blk(B_KEY_WIDTH), blk(B_VAL_WIDTH), blk(B_KEY_WIDTH), blk(B_VAL_WIDTH),
                  full(gn), full(mexp), full(lvl)],
        out_specs=blk(B_VAL_WIDTH),
        out_shape=jax.ShapeDtypeStruct((t, B_VAL_WIDTH), BF16),
        scratch_shapes=[pltpu.VMEM((B_KEY_WIDTH, B_VAL_DIM), F32)],
        compiler_params=_params("arbitrary", "arbitrary"),
        name="gla",
    )(q, k, v, g, r, gn, mexp, lvl)


def _token_kernel(x_ref, oa_ref, ob_ref, gate_ref, wb0_ref, wb1_ref, wmix_ref, gx_ref, wq_ref,
                  km_ref, vm_ref, wo_ref, gf_ref, wr_ref, br_ref, ltri_ref, utri_ref,
                  x2_ref, hs_ref, ws_ref, route_ref, cnt_ref):
    ma = _dot(oa_ref[...], wb0_ref[...])
    mb = _dot(ob_ref[...], wb1_ref[...])
    merged = (gate_ref[:, :D_MODEL].astype(F32) * ma + gate_ref[:, D_MODEL:].astype(F32) * mb).astype(BF16)
    x1 = x_ref[...] + _dot(merged, wmix_ref[...])

    h2 = _rms(x1, gx_ref[...]).astype(BF16)
    qx = (_dot(h2, wq_ref[...]) * (X_HEAD_DIM ** -0.5)).astype(BF16)
    heads = []
    for h in range(X_HEADS):
        sl = slice(h * X_HEAD_DIM, (h + 1) * X_HEAD_DIM)
        s = _dot_nt(qx[:, sl], km_ref[0, :, sl])
        m = jnp.max(s, axis=-1, keepdims=True)
        pe = jnp.exp(s - m)
        l = jnp.sum(pe, axis=-1, keepdims=True)
        heads.append((_dot(pe.astype(BF16), vm_ref[0, :, sl]) * (1.0 / l)).astype(BF16))
    x2 = x1 + _dot(jnp.concatenate(heads, axis=1), wo_ref[...])
    x2_ref[...] = x2

    h3 = _rms(x2, gf_ref[...])

    h3_hi = h3.astype(BF16)
    h3_lo = (h3 - h3_hi.astype(F32)).astype(BF16)
    hw = _dot(h3_hi, wr_ref[...])
    logits = hw[:, :LANES] + hw[:, LANES:] + _dot(h3_lo, wr_ref[:, :LANES]) + br_ref[...]
    for h in range(TOKEN_TILES_PER_STEP):
        rows = slice(h * ROW_TILE, (h + 1) * ROW_TILE)
        chunks = pl.ds(h * LOCAL_CHUNKS, LOCAL_CHUNKS)
        _route_and_sort(logits[rows], h3_hi[rows], ltri_ref, utri_ref, hs_ref.at[chunks], ws_ref.at[chunks],
                        route_ref.at[pl.ds(h * ROW_TILE, ROW_TILE)], cnt_ref.at[h])


def _route_and_sort(logits, h3_hi, ltri_ref, utri_ref, hs_ref, ws_ref, route_ref, cnt_ref):
    lane = lax.broadcasted_iota(I32, logits.shape, 1).astype(F32)
    big = jnp.float32(LANES)
    gl = jnp.where(lane < N_GROUPS, logits, NEG)
    gmax = jnp.max(gl, axis=-1, keepdims=True)
    gidx = jnp.min(jnp.where(gl == gmax, lane, big), axis=-1, keepdims=True)
    g_w = 1.0 / jnp.sum(jnp.exp(gl - gmax), axis=-1, keepdims=True)
    lo = N_GROUPS + EXPERTS_PER_GROUP * gidx
    el = jnp.where((lane >= lo) & (lane < lo + EXPERTS_PER_GROUP), logits, NEG)
    v1 = jnp.max(el, axis=-1, keepdims=True)
    i1 = jnp.min(jnp.where(el == v1, lane, big), axis=-1, keepdims=True)
    el2 = jnp.where(lane == i1, NEG, el)
    v2 = jnp.max(el2, axis=-1, keepdims=True)
    i2 = jnp.min(jnp.where(el2 == v2, lane, big), axis=-1, keepdims=True)
    e21 = jnp.exp(v2 - v1)
    w1 = g_w / (1.0 + e21)
    w2 = w1 * e21
    oh0 = jnp.where(lane == i1 - N_GROUPS, 1.0, 0.0)
    oh1 = jnp.where(lane == i2 - N_GROUPS, 1.0, 0.0)
    oh = oh0 + oh1
    nch = jnp.floor((jnp.sum(oh, axis=0, keepdims=True) + (CHUNK_ROWS - 1)) * (1.0 / CHUNK_ROWS))
    nch8 = jnp.broadcast_to(nch, (8, LANES))
    start = _dot(nch8.astype(BF16), utri_ref[...])[0:1] * CHUNK_ROWS
    rank = _dot(ltri_ref[...], oh.astype(BF16))
    row = start + rank
    pos0 = jnp.sum(row * oh0, axis=-1, keepdims=True)
    pos1 = jnp.sum(row * oh1, axis=-1, keepdims=True)
    route = jnp.where(lane == 0, pos0, jnp.where(lane == 1, pos1, 0.0))
    route_t = jnp.transpose(route)
    r = lax.broadcasted_iota(I32, (LOCAL_ROWS, ROW_TILE), 0).astype(F32)
    p0 = jnp.where(r == route_t[0:1, :], 1.0, 0.0).astype(BF16)
    p1 = jnp.where(r == route_t[1:2, :], 1.0, 0.0).astype(BF16)
    hs_ref[...] = _dot(p0 + p1, h3_hi).astype(BF16).reshape(hs_ref.shape)

    def gate_cols(w):
        hi = w.astype(BF16).astype(F32)
        return jnp.where(lane == 0, hi, jnp.where(lane == 1, w - hi, 0.0)).astype(BF16)

    ws_ref[...] = (_dot(p0, gate_cols(w1)) + _dot(p1, gate_cols(w2))).astype(BF16).reshape(ws_ref.shape)
    route_ref[...] = route
    cnt_ref[...] = nch8


def _token(x, oa, ob, gates, wb0, wb1, wmix, gx, wq, km, vm, wo, gf, wr, br, batch):
    t, d = x.shape
    n = TOKEN_TILES_PER_STEP
    nb = t // batch // (n * ROW_TILE)
    nt = t // ROW_TILE
    ltri = jnp.asarray(np.tril(np.ones((ROW_TILE, ROW_TILE), np.float32), -1), BF16)
    utri = jnp.asarray(np.triu(np.ones((LANES, LANES), np.float32), 1), BF16)
    cur = lambda b, i: (b * nb + i, 0)
    cur3 = lambda b, i: (b * nb + i, 0, 0)
    blk = lambda w: pl.BlockSpec((n * ROW_TILE, w), cur)
    full = lambda a: pl.BlockSpec(a.shape, lambda b, i: (0,) * a.ndim)
    mem = pl.BlockSpec((1,) + km.shape[1:], lambda b, i: (b, 0, 0))
    return pl.pallas_call(
        _token_kernel,
        grid=(batch, nb),
        in_specs=[blk(d), blk(A_WIDTH), blk(B_VAL_WIDTH), blk(2 * d), full(wb0), full(wb1), full(wmix),
                  full(gx), full(wq), mem, mem, full(wo), full(gf), full(wr), full(br), full(ltri), full(utri)],
        out_specs=[blk(d), pl.BlockSpec((n * LOCAL_CHUNKS, CHUNK_ROWS, d), cur3),
                   pl.BlockSpec((n * LOCAL_CHUNKS, CHUNK_ROWS, LANES), cur3),
                   blk(LANES), pl.BlockSpec((n, 8, LANES), cur3)],
        out_shape=[jax.ShapeDtypeStruct((t, d), F32),
                   jax.ShapeDtypeStruct((nt * LOCAL_CHUNKS, CHUNK_ROWS, d), BF16),
                   jax.ShapeDtypeStruct((nt * LOCAL_CHUNKS, CHUNK_ROWS, LANES), BF16),
                   jax.ShapeDtypeStruct((t, LANES), F32),
                   jax.ShapeDtypeStruct((nt, 8, LANES), F32)],
        compiler_params=_params("arbitrary", "arbitrary"),
        name="token",
    )(x, oa, ob, gates, wb0, wb1, wmix, gx, wq, km, vm, wo, gf, wr, br, ltri, utri)


def _expert_kernel(te_ref, nu_ref, nv_ref, ch_ref, hs_hbm, ws_hbm, wg_ref, wu_ref, wd_ref, ys_hbm,
                   xbuf, gbuf, ybuf, gsem, ssem):
    i = pl.program_id(0)
    n_used = nu_ref[0]
    slot = lax.rem(i, 2)

    def gather(tile, s, start):
        def body(c, carry):
            ch = ch_ref[tile * TILE_CHUNKS + c]
            for cp in (pltpu.make_async_copy(hs_hbm.at[ch], xbuf.at[s, c], gsem.at[s]),
                       pltpu.make_async_copy(ws_hbm.at[ch], gbuf.at[s, c], gsem.at[s])):
                cp.start() if start else cp.wait()
            return carry

        lax.fori_loop(0, nv_ref[tile], body, 0)

    def scatter(tile, s, start):
        def body(c, carry):
            ch = ch_ref[tile * TILE_CHUNKS + c]
            cp = pltpu.make_async_copy(ybuf.at[s, c], ys_hbm.at[ch], ssem.at[s])
            cp.start() if start else cp.wait()
            return carry

        lax.fori_loop(0, nv_ref[tile], body, 0)

    @pl.when(i == 0)
    def _():
        xbuf[...] = jnp.zeros_like(xbuf)
        gbuf[...] = jnp.zeros_like(gbuf)
        gather(0, 0, True)

    @pl.when(i < n_used)
    def _():
        @pl.when(i + 1 < n_used)
        def _():
            gather(i + 1, 1 - slot, True)

        gather(i, slot, False)

        @pl.when(i >= 2)
        def _():
            scatter(i - 2, slot, False)

        x = xbuf[slot].reshape(EXPERT_ROWS, D_MODEL)
        hg = _dot(x, wg_ref[0])
        hu = _dot(x, wu_ref[0])
        hid = (hg * jax.nn.sigmoid(hg) * hu).astype(BF16)
        g = gbuf[slot].reshape(EXPERT_ROWS, LANES).astype(F32)
        gate = g[:, 0:1] + g[:, 1:2]
        ybuf[slot] = (gate * _dot(hid, wd_ref[0])).astype(BF16).reshape(TILE_CHUNKS, CHUNK_ROWS, D_MODEL)
        scatter(i, slot, True)

        @pl.when(i == n_used - 1)
        def _():
            scatter(i, slot, False)

            @pl.when(i >= 1)
            def _():
                scatter(i - 1, 1 - slot, False)


def _experts(hs, ws, tile_expert, n_used, n_valid, chunks, wg, wu, wd):
    n_tiles = tile_expert.shape[0]
    last = lambda i, te, nu, nv, ch: jnp.minimum(i, nu[0] - 1)
    wmap = lambda i, te, nu, nv, ch: (te[last(i, te, nu, nv, ch)], 0, 0)
    anyspace = pl.BlockSpec(memory_space=pl.ANY)
    grid_spec = pltpu.PrefetchScalarGridSpec(
        num_scalar_prefetch=4,
        grid=(n_tiles,),
        in_specs=[anyspace, anyspace,
                  pl.BlockSpec((1, D_MODEL, EXPERT_FF), wmap),
                  pl.BlockSpec((1, D_MODEL, EXPERT_FF), wmap),
                  pl.BlockSpec((1, EXPERT_FF, D_MODEL), wmap)],
        out_specs=anyspace,
        scratch_shapes=[pltpu.VMEM((2, TILE_CHUNKS, CHUNK_ROWS, D_MODEL), BF16),
                        pltpu.VMEM((2, TILE_CHUNKS, CHUNK_ROWS, LANES), BF16),
                        pltpu.VMEM((2, TILE_CHUNKS, CHUNK_ROWS, D_MODEL), BF16),
                        pltpu.SemaphoreType.DMA((2,)), pltpu.SemaphoreType.DMA((2,))],
    )
    return pl.pallas_call(
        _expert_kernel,
        grid_spec=grid_spec,
        out_shape=jax.ShapeDtypeStruct(hs.shape, BF16),
        input_output_aliases={4: 0},
        compiler_params=_params("arbitrary"),
        name="experts",
    )(tile_expert, n_used, n_valid, chunks, hs, ws, wg, wu, wd)


def _combine_kernel(final_norm, x_ref, route_ref, ys_ref, gfin_ref, o_ref):
    r = lax.broadcasted_iota(I32, (ROW_TILE, LOCAL_ROWS), 1).astype(F32)
    sel = jnp.where(r == route_ref[:, 0:1], 1.0, jnp.where(r == route_ref[:, 1:2], 1.0, 0.0)).astype(BF16)
    x3 = x_ref[...] + _dot(sel, ys_ref[...].reshape(LOCAL_ROWS, D_MODEL))
    if final_norm:
        x3 = _rms(x3, gfin_ref[...])
    o_ref[...] = x3


def _combine(x2, route, ys, gfin, final_norm):
    t, d = x2.shape
    nt = t // ROW_TILE
    return pl.pallas_call(
        functools.partial(_combine_kernel, final_norm),
        grid=(nt,),
        in_specs=[pl.BlockSpec((ROW_TILE, d), lambda i: (i, 0)),
                  pl.BlockSpec((ROW_TILE, LANES), lambda i: (i, 0)),
                  pl.BlockSpec((LOCAL_CHUNKS, CHUNK_ROWS, d), lambda i: (i, 0, 0)),
                  pl.BlockSpec((1, d), lambda i: (0, 0))],
        out_specs=pl.BlockSpec((ROW_TILE, d), lambda i: (i, 0)),
        out_shape=jax.ShapeDtypeStruct((t, d), F32),
        compiler_params=_params("arbitrary"),
        name="combine",
    )(x2, route, ys, gfin)


def _chunk_plan(nch, n_tiles):
    nt = nch.shape[0]
    local_start = jnp.cumsum(nch, axis=1) - nch
    cum = jnp.cumsum(nch, axis=0)
    total = cum[-1]
    tiles = (total + TILE_CHUNKS - 1) // TILE_CHUNKS
    tile_end = jnp.cumsum(tiles)
    n_used = tile_end[-1:]
    tile_ids = jnp.arange(n_tiles, dtype=I32)
    tile_expert = jnp.minimum(jnp.sum((tile_end[None, :] <= tile_ids[:, None]).astype(I32), axis=1),
                              N_EXPERTS - 1)
    sel = (tile_expert[:, None] == jnp.arange(N_EXPERTS, dtype=I32)[None, :]).astype(I32)
    pick = lambda table: jnp.sum(sel[:, :, None] * table.T[None, :, :], axis=1)
    first_tile = jnp.sum(sel * (tile_end - tiles)[None, :], axis=1)
    slot = (tile_ids - first_tile)[:, None] * TILE_CHUNKS + jnp.arange(TILE_CHUNKS, dtype=I32)[None, :]
    valid = (slot < jnp.sum(sel * total[None, :], axis=1)[:, None]) & (tile_ids < n_used)[:, None]
    src_tile = jnp.sum((pick(cum)[:, None, :] <= slot[:, :, None]).astype(I32), axis=2)
    src_tile = jnp.minimum(src_tile, nt - 1)
    at = (src_tile[:, :, None] == jnp.arange(nt, dtype=I32)[None, None, :]).astype(I32)
    before = jnp.sum(at * pick(cum - nch)[:, None, :], axis=2)
    start = jnp.sum(at * pick(local_start)[:, None, :], axis=2)
    chunk = jnp.where(valid, src_tile * LOCAL_CHUNKS + start + slot - before, 0)
    return tile_expert, n_used, jnp.sum(valid.astype(I32), axis=1), chunk.reshape(-1)


def kernel(x, mem, norm_mix_g, w_in, rel_bias, gla_w_alpha, gla_b_alpha, gla_norm_g, w_branch, w_mix_out, norm_x_g, mem_norm_g, w_xq, w_xkv, w_xo, norm_ffn_g, w_group_router, b_group_router, w_expert_router, b_expert_router, w_exp_gate, w_exp_up, w_exp_down, final_norm_g):
    batch, seq, d = x.shape
    depth = w_in.shape[0]
    t = batch * seq
    assert d == D_MODEL and seq % (TOKEN_TILES_PER_STEP * ROW_TILE) == 0
    nt = t // ROW_TILE
    n_tiles = nt * LOCAL_CHUNKS // TILE_CHUNKS + N_EXPERTS

    xf = x.reshape(t, d)
    km_all, vm_all = _memkv(mem, mem_norm_g, w_xkv.astype(BF16))
    row = lambda a: a.reshape(1, -1).astype(F32)

    for l in range(depth):
        wi = w_in[l]
        c0 = A_WIDTH * 3 + B_KEY_WIDTH * 2 + B_VAL_WIDTH
        w_main = jnp.concatenate([wi[:, :c0], wi[:, c0 + B_GATE_RANK:]], axis=1).astype(BF16)
        w_al = jnp.pad(wi[:, c0:c0 + B_GATE_RANK], ((0, 0), (0, LANES - B_GATE_RANK))).astype(BF16)
        w_al2 = jnp.pad(gla_w_alpha[l], ((0, LANES - B_GATE_RANK), (0, 0))).astype(BF16)
        aq, ak, av, bq, bk, bv, lga, br, gates = _inproj(
            xf, row(norm_mix_g[l]), w_main, w_al, w_al2, row(gla_b_alpha[l]))

        oa = _band_attention(aq, ak, av, _band_bias(rel_bias[l]), batch)
        ob = _gla(bq, bk, bv, lga, br, row(gla_norm_g[l]), batch)

        wr = jnp.pad(jnp.concatenate([w_group_router[l], w_expert_router[l]], axis=1).astype(F32),
                     ((0, 0), (0, LANES - N_GROUPS - N_EXPERTS)))
        wr_hi = wr.astype(BF16)
        wr = jnp.concatenate([wr_hi, (wr - wr_hi.astype(F32)).astype(BF16)], axis=1)
        brt = jnp.pad(jnp.concatenate([b_group_router[l], b_expert_router[l]]).astype(F32),
                      (0, LANES - N_GROUPS - N_EXPERTS)).reshape(1, LANES)
        x2, hs, ws, route, cnt = _token(
            xf, oa, ob, gates, w_branch[l, 0].astype(BF16), w_branch[l, 1].astype(BF16),
            w_mix_out[l].astype(BF16), row(norm_x_g[l]), w_xq[l].astype(BF16), km_all[l], vm_all[l],
            w_xo[l].astype(BF16), row(norm_ffn_g[l]), wr, brt, batch)

        plan = _chunk_plan(cnt[:, 0, :N_EXPERTS].astype(I32), n_tiles)
        e3 = lambda w: w[l].reshape((N_EXPERTS,) + w.shape[3:]).astype(BF16)
        ys = _experts(hs, ws, *plan, e3(w_exp_gate), e3(w_exp_up), e3(w_exp_down))
        xf = _combine(x2, route, ys, row(final_norm_g), l == depth - 1)

    return xf.reshape(batch, seq, d)
```

```python
import functools

import numpy as np
import jax
import jax.numpy as jnp
from jax import lax
from jax.experimental import pallas as pl
from jax.experimental.pallas import tpu as pltpu

F32 = jnp.float32
BF16 = jnp.bfloat16
I32 = jnp.int32

D_MODEL = 1024
CHUNK = 64
EPS = 1e-6
A_HEADS = 8
A_HEAD_DIM = 64
A_WIDTH = 512
A_LEFT_CHUNKS = 8
A_MAX_REL = 256
B_HEADS = 4
B_KEY_DIM = 64
B_VAL_DIM = 128
B_KEY_WIDTH = 256
B_VAL_WIDTH = 512
B_GATE_RANK = 16
B_GATE_TAU = 16.0
X_HEADS = 4
X_HEAD_DIM = 256
N_GROUPS = 4
EXPERTS_PER_GROUP = 8
N_EXPERTS = N_GROUPS * EXPERTS_PER_GROUP
EXPERT_FF = 256

LANES = 128
ROW_TILE = 256
CHUNKS_PER_TILE = ROW_TILE // CHUNK
BAND_TILES = A_LEFT_CHUNKS // CHUNKS_PER_TILE + 1
BAND_KEYS = BAND_TILES * ROW_TILE
LOG_CHUNK = 6
N_LEVELS = LOG_CHUNK
EXP_ROWS = (2 + N_LEVELS) * CHUNK
CHUNK_ROWS = 16
TOKEN_TILES_PER_STEP = 2
INPROJ_ROWS = 512
EXPERT_ROWS = 512
TILE_CHUNKS = EXPERT_ROWS // CHUNK_ROWS
LOCAL_CHUNKS = 2 * ROW_TILE // CHUNK_ROWS + N_EXPERTS
LOCAL_ROWS = LOCAL_CHUNKS * CHUNK_ROWS
SORT_WIDTH = D_MODEL + LANES
NEG = -1e30
LOG2E = 1.4426950408889634
VMEM_LIMIT = 56 * 1024 * 1024


def _params(*sem):
    return pltpu.CompilerParams(dimension_semantics=sem, vmem_limit_bytes=VMEM_LIMIT)


def _rms(x, g):
    return x * lax.rsqrt(jnp.mean(x * x, axis=-1, keepdims=True) + EPS) * g


def _dot(a, b):
    return jnp.dot(a, b, preferred_element_type=F32)


def _dot_nt(a, b):
    return lax.dot_general(a, b, (((1,), (1,)), ((), ())), preferred_element_type=F32)


def _dot_tn(a, b):
    return lax.dot_general(a, b, (((0,), (0,)), ((), ())), preferred_element_type=F32)


def _memkv_kernel(mem_ref, g_ref, w_ref, k_ref, v_ref):
    mn = _rms(mem_ref[0], g_ref[...]).astype(BF16)
    kv = _dot(mn, w_ref[0])
    k_ref[0, 0] = kv[:, :D_MODEL].astype(BF16)
    v_ref[0, 0] = kv[:, D_MODEL:].astype(BF16)


def _memkv(mem, g, w_xkv):
    depth = w_xkv.shape[0]
    b, m, d = mem.shape
    out = jax.ShapeDtypeStruct((depth, b, m, d), BF16)
    return pl.pallas_call(
        _memkv_kernel,
        grid=(depth, b),
        in_specs=[pl.BlockSpec((1, m, d), lambda l, i: (i, 0, 0)),
                  pl.BlockSpec((1, d), lambda l, i: (0, 0)),
                  pl.BlockSpec((1, d, 2 * d), lambda l, i: (l, 0, 0))],
        out_specs=[pl.BlockSpec((1, 1, m, d), lambda l, i: (l, i, 0, 0)),
                   pl.BlockSpec((1, 1, m, d), lambda l, i: (l, i, 0, 0))],
        out_shape=[out, out],
        compiler_params=_params("arbitrary", "arbitrary"),
        name="memkv",
    )(mem, g.reshape(1, d), w_xkv)


_C_AQ, _C_AK, _C_AV = 0, 512, 1024
_C_BQ, _C_BK, _C_BV = 1536, 1792, 2048
_C_BR, _C_GATE, _C_END = 2560, 3072, 5120


def _inproj_kernel(x_ref, g_ref, w_ref, wal_ref, wal2_ref, bal_ref,
                   aq_ref, ak_ref, av_ref, bq_ref, bk_ref, bv_ref, lga_ref, br_ref, gate_ref):
    h = _rms(x_ref[...], g_ref[...]).astype(BF16)

    def mm(lo, hi):
        return _dot(h, w_ref[:, lo:hi])

    aq_ref[...] = (mm(_C_AQ, _C_AK) * (A_HEAD_DIM ** -0.5 * LOG2E)).astype(BF16)
    ak_ref[...] = mm(_C_AK, _C_AV).astype(BF16)
    av_ref[...] = mm(_C_AV, _C_BQ).astype(BF16)
    bq_ref[...] = (mm(_C_BQ, _C_BK) * (B_KEY_DIM ** -0.5)).astype(BF16)
    bk_ref[...] = mm(_C_BK, _C_BV).astype(BF16)
    bv_ref[...] = mm(_C_BV, _C_BR).astype(BF16)
    r = mm(_C_BR, _C_GATE)
    br_ref[...] = (r * jax.nn.sigmoid(r)).astype(BF16)
    for c in range(_C_GATE, _C_END, 512):
        gate_ref[:, c - _C_GATE:c - _C_GATE + 512] = jax.nn.sigmoid(mm(c, c + 512)).astype(BF16)
    z = _dot(_dot(h, wal_ref[...]).astype(BF16), wal2_ref[...]) + bal_ref[...]
    lga_ref[...] = (jnp.minimum(z, 0.0) - jnp.log(1.0 + jnp.exp(-jnp.abs(z)))) * (1.0 / B_GATE_TAU)


def _inproj(x, g, w_main, w_al, w_al2, b_al):
    t, d = x.shape
    row = lambda w: pl.BlockSpec((INPROJ_ROWS, w), lambda i: (i, 0))
    full = lambda a: pl.BlockSpec(a.shape, lambda i: (0,) * a.ndim)
    sds = lambda w, dt: jax.ShapeDtypeStruct((t, w), dt)
    widths = [(512, BF16), (512, BF16), (512, BF16), (256, BF16), (256, BF16), (512, BF16),
              (256, F32), (512, BF16), (2048, BF16)]
    return pl.pallas_call(
        _inproj_kernel,
        grid=(t // INPROJ_ROWS,),
        in_specs=[row(d), full(g), full(w_main), full(w_al), full(w_al2), full(b_al)],
        out_specs=[row(w) for w, _ in widths],
        out_shape=[sds(w, dt) for w, dt in widths],
        compiler_params=_params("arbitrary"),
        name="inproj",
    )(x, g, w_main, w_al, w_al2, b_al)


def _band_kernel(q_ref, k0_ref, k1_ref, k2_ref, v0_ref, v1_ref, v2_ref, bias_ref, o_ref):
    i = pl.program_id(1)
    lane = lax.broadcasted_iota(I32, (1, LANES), 1)
    low = lane < A_HEAD_DIM
    ones = jnp.ones((BAND_KEYS, LANES), BF16)

    def attend(pen):
        for p in range(A_HEADS // 2):
            sl = slice(p * LANES, (p + 1) * LANES)
            qp = q_ref[:, sl]
            zero = jnp.zeros_like(qp)
            q2 = jnp.concatenate([jnp.where(low, qp, zero), jnp.where(low, zero, qp)], axis=0)
            kp = jnp.concatenate([k0_ref[:, sl], k1_ref[:, sl], k2_ref[:, sl]], axis=0)
            vp = jnp.concatenate([v0_ref[:, sl], v1_ref[:, sl], v2_ref[:, sl]], axis=0)
            s = _dot_nt(q2, kp) + bias_ref[p]
            if pen is not None:
                s = s + pen
            pe = jnp.exp2(s - jnp.max(s, axis=-1, keepdims=True)).astype(BF16)
            o2 = _dot(pe, jnp.concatenate([vp, ones], axis=1))
            o = o2[:, :LANES] * (1.0 / o2[:, LANES:])
            o_ref[:, sl] = jnp.where(low, o[:ROW_TILE], o[ROW_TILE:]).astype(BF16)

    @pl.when(i >= BAND_TILES - 1)
    def _():
        attend(None)

    @pl.when(i < BAND_TILES - 1)
    def _():
        col = lax.broadcasted_iota(I32, (1, BAND_KEYS), 1)
        attend(jnp.where(col < (BAND_TILES - 1 - i) * ROW_TILE, NEG, 0.0).astype(F32))


def _band_bias(rel_table):
    h = rel_table.shape[0]
    tab = rel_table.astype(F32)
    shift = A_LEFT_CHUNKS * CHUNK + ROW_TILE - 1
    n_far = shift - A_MAX_REL + 1
    span = ROW_TILE + BAND_KEYS - 1
    assert span - 1 - shift <= A_MAX_REL
    u = jnp.concatenate([jnp.broadcast_to(tab[:, 2 * A_MAX_REL:], (h, n_far)),
                         tab[:, 2 * A_MAX_REL - 1:2 * A_MAX_REL - 1 - (span - n_far):-1]], axis=1)
    period = span + 1
    u = jnp.roll(jnp.pad(u, ((0, 0), (0, period - span))), -(ROW_TILE - 1), axis=1)
    rows = jnp.tile(u, (1, ROW_TILE))[:, :ROW_TILE * (period - 1)].reshape(h, ROW_TILE, period - 1)
    bias = rows[:, :, :BAND_KEYS]
    q = np.arange(ROW_TILE)[:, None]
    k = np.arange(BAND_KEYS)[None, :]
    cq, ck = q // CHUNK, k // CHUNK
    valid = (ck >= cq) & (ck <= cq + A_LEFT_CHUNKS)
    bias = jnp.where(valid[None], bias * LOG2E, NEG)
    return bias.reshape(h // 2, 2 * ROW_TILE, BAND_KEYS)


def _band_attention(q, k, v, bias, batch):
    t, w = q.shape
    nb = t // batch // ROW_TILE
    cur = lambda b, i: (b * nb + i, 0)
    back = lambda n: (lambda b, i: (b * nb + jnp.maximum(i - n, 0), 0))
    blk = lambda f: pl.BlockSpec((ROW_TILE, w), f)
    return pl.pallas_call(
        _band_kernel,
        grid=(batch, nb),
        in_specs=[blk(cur), blk(back(2)), blk(back(1)), blk(cur), blk(back(2)), blk(back(1)), blk(cur),
                  pl.BlockSpec(bias.shape, lambda b, i: (0, 0, 0))],
        out_specs=blk(cur),
        out_shape=jax.ShapeDtypeStruct((t, w), BF16),
        compiler_params=_params("arbitrary", "arbitrary"),
        name="band_attn",
    )(q, k, k, k, v, v, v, bias)


def _gla_constants():
    c = CHUNK
    t = np.arange(c)[:, None]
    r = np.arange(c)[None, :]
    mats = [(r <= t), (r > t)]
    lvl = np.full((c, c), -1, np.int32)
    lvl[np.arange(c), np.arange(c)] = N_LEVELS
    for l in range(N_LEVELS):
        m = (c // 2) >> l
        mid = (t // (2 * m)) * (2 * m) + m
        upper = t >= mid
        mats.append(np.where(upper, (r >= mid) & (r <= t), (r > t) & (r < mid)))
        s = r
        same = (s // (2 * m)) == (t // (2 * m))
        lvl[np.asarray(same & upper & (s < mid))] = l
    eye = np.eye(CHUNKS_PER_TILE)
    mexp = np.concatenate([np.kron(eye, m) for m in mats], axis=0).astype(np.float32)
    lvl = np.tile(lvl, (1, B_HEADS))
    return jnp.asarray(mexp, BF16), jnp.asarray(lvl, I32)


def _gla_kernel(q_ref, k_ref, v_ref, g_ref, r_ref, gn_ref, mexp_ref, lvl_ref, o_ref, s_ref):
    @pl.when(pl.program_id(1) == 0)
    def _():
        s_ref[...] = jnp.zeros_like(s_ref)

    kw = B_KEY_WIDTH
    ri = lax.broadcasted_iota(I32, (kw, kw), 0) >> LOG_CHUNK
    ci = lax.broadcasted_iota(I32, (kw, kw), 1) >> LOG_CHUNK
    bd = ri == ci
    head_ind = jnp.where(bd, 1.0, 0.0).astype(BF16)
    ri2 = lax.broadcasted_iota(I32, (kw, 2 * kw), 0) >> LOG_CHUNK
    ci2 = (lax.broadcasted_iota(I32, (kw, 2 * kw), 1) & (kw - 1)) >> LOG_CHUNK
    bd2 = ri2 == ci2
    lvl = lvl_ref[...]
    row8 = lax.broadcasted_iota(I32, (16, kw), 0)
    ones = jnp.ones((16, LANES), BF16)
    zero_b = jnp.zeros((kw, kw), BF16)
    chunks = [slice(c * CHUNK, (c + 1) * CHUNK) for c in range(CHUNKS_PER_TILE)]

    def head_blocks(x):
        return jnp.where(bd, jnp.concatenate([x] * B_HEADS, axis=0), zero_b)

    q = q_ref[...].astype(F32)
    k = k_ref[...].astype(F32)
    g = g_ref[...]
    gb = g.astype(BF16)
    half = EXP_ROWS * CHUNKS_PER_TILE // 2
    w = jnp.exp(jnp.concatenate([_dot(mexp_ref[:half, :], gb), _dot(mexp_ref[half:, :], gb)], axis=0))
    w_cum = w[0:ROW_TILE]
    qt = (q * w_cum).astype(BF16)
    kb = (k * w[ROW_TILE:2 * ROW_TILE]).astype(BF16)
    qk = (q * k).astype(BF16)

    attn = [jnp.zeros((CHUNK, kw), F32) for _ in chunks]
    for l in range(N_LEVELS):
        wl = w[(2 + l) * ROW_TILE:(3 + l) * ROW_TILE]
        qh = (q * wl).astype(BF16)
        kh = (k * wl).astype(BF16)
        for c, rows in enumerate(chunks):
            attn[c] = jnp.where(lvl == l, _dot_nt(qh[rows], head_blocks(kh[rows])), attn[c])

    vstacks, kvs, dcols = [], [], []
    for c, rows in enumerate(chunks):
        attn[c] = jnp.where(lvl == N_LEVELS, _dot(qk[rows], head_ind), attn[c])
        v = v_ref[rows, :]
        vstack = jnp.concatenate([v[:, j * LANES:(j + 1) * LANES] for j in range(B_HEADS)], axis=0)
        vstacks.append(vstack)
        kvs.append(_dot_tn(head_blocks(kb[rows]), vstack))
        d = jnp.exp(jnp.sum(g[rows], axis=0, keepdims=True))
        d1 = d.astype(BF16).astype(F32)
        dp = jnp.where(row8 == 0, d1, jnp.where(row8 == 1, d - d1, 0.0)).astype(BF16)
        dcols.append(_dot_tn(dp, ones))

    s = s_ref[...]
    for c, rows in enumerate(chunks):
        lhs = jnp.concatenate([attn[c].astype(BF16), qt[rows]], axis=1)
        lhs = jnp.where(bd2, jnp.concatenate([lhs] * B_HEADS, axis=0), jnp.zeros((kw, 2 * kw), BF16))
        rhs = jnp.concatenate([vstacks[c], s.astype(BF16)], axis=0)
        o = _dot(lhs, rhs)
        s = dcols[c] * s + kvs[c]
        for j in range(B_HEADS):
            oj = o[j * CHUNK:(j + 1) * CHUNK]
            sl = slice(j * LANES, (j + 1) * LANES)
            y = oj * lax.rsqrt(jnp.mean(oj * oj, axis=-1, keepdims=True) + EPS) * gn_ref[...]
            o_ref[rows, sl] = (y * r_ref[rows, sl].astype(F32)).astype(BF16)
    s_ref[...] = s


def _gla(q, k, v, g, r, gn, batch):
    t = q.shape[0]
    nb = t // batch // ROW_TILE
    mexp, lvl = _gla_constants()
    cur = lambda b, i: (b * nb + i, 0)
    blk = lambda w: pl.BlockSpec((ROW_TILE, w), cur)
    full = lambda a: pl.BlockSpec(a.shape, lambda b, i: (0,) * a.ndim)
    return pl.pallas_call(
        _gla_kernel,
        grid=(batch, nb),
        in_specs=[blk(B_KEY_WIDTH), blk(B_KEY_WIDTH), blk(B_VAL_WIDTH), blk(B_KEY_WIDTH), blk(B_VAL_WIDTH),
                  full(gn), full(mexp), full(lvl)],
        out_specs=blk(B_VAL_WIDTH),
        out_shape=jax.ShapeDtypeStruct((t, B_VAL_WIDTH), BF16),
        scratch_shapes=[pltpu.VMEM((B_KEY_WIDTH, B_VAL_DIM), F32)],
        compiler_params=_params("arbitrary", "arbitrary"),
        name="gla",
    )(q, k, v, g, r, gn, mexp, lvl)


def _token_kernel(x_ref, oa_ref, ob_ref, gate_ref, wb0_ref, wb1_ref, wmix_ref, gx_ref, wq_ref,
                  km_ref, vm_ref, wo_ref, gf_ref, wr_ref, br_ref, ltri_ref, utri_ref,
                  x2_ref, hs_ref, route_ref, cnt_ref):
    ma = _dot(oa_ref[...], wb0_ref[...])
    mb = _dot(ob_ref[...], wb1_ref[...])
    merged = (gate_ref[:, :D_MODEL].astype(F32) * ma + gate_ref[:, D_MODEL:].astype(F32) * mb).astype(BF16)
    x1 = x_ref[...] + _dot(merged, wmix_ref[...])

    h2 = _rms(x1, gx_ref[...]).astype(BF16)
    qx = (_dot(h2, wq_ref[...]) * (X_HEAD_DIM ** -0.5)).astype(BF16)
    heads = []
    for h in range(X_HEADS):
        sl = slice(h * X_HEAD_DIM, (h + 1) * X_HEAD_DIM)
        s = _dot_nt(qx[:, sl], km_ref[0, :, sl])
        m = jnp.max(s, axis=-1, keepdims=True)
        pe = jnp.exp(s - m)
        l = jnp.sum(pe, axis=-1, keepdims=True)
        heads.append((_dot(pe.astype(BF16), vm_ref[0, :, sl]) * (1.0 / l)).astype(BF16))
    x2 = x1 + _dot(jnp.concatenate(heads, axis=1), wo_ref[...])
    x2_ref[...] = x2

    h3 = _rms(x2, gf_ref[...])

    h3_hi = h3.astype(BF16)
    h3_lo = (h3 - h3_hi.astype(F32)).astype(BF16)
    hw = _dot(h3_hi, wr_ref[...])
    logits = hw[:, :LANES] + hw[:, LANES:] + _dot(h3_lo, wr_ref[:, :LANES]) + br_ref[...]
    for h in range(TOKEN_TILES_PER_STEP):
        rows = slice(h * ROW_TILE, (h + 1) * ROW_TILE)
        chunks = pl.ds(h * LOCAL_CHUNKS, LOCAL_CHUNKS)
        _route_and_sort(logits[rows], h3_hi[rows], ltri_ref, utri_ref, hs_ref.at[chunks],
                        route_ref.at[pl.ds(h * ROW_TILE, ROW_TILE)], cnt_ref.at[h])


def _route_and_sort(logits, h3_hi, ltri_ref, utri_ref, hs_ref, route_ref, cnt_ref):
    lane = lax.broadcasted_iota(I32, logits.shape, 1).astype(F32)
    big = jnp.float32(LANES)
    gl = jnp.where(lane < N_GROUPS, logits, NEG)
    gmax = jnp.max(gl, axis=-1, keepdims=True)
    gidx = jnp.min(jnp.where(gl == gmax, lane, big), axis=-1, keepdims=True)
    g_w = 1.0 / jnp.sum(jnp.exp(gl - gmax), axis=-1, keepdims=True)
    lo = N_GROUPS + EXPERTS_PER_GROUP * gidx
    el = jnp.where((lane >= lo) & (lane < lo + EXPERTS_PER_GROUP), logits, NEG)
    v1 = jnp.max(el, axis=-1, keepdims=True)
    i1 = jnp.min(jnp.where(el == v1, lane, big), axis=-1, keepdims=True)
    el2 = jnp.where(lane == i1, NEG, el)
    v2 = jnp.max(el2, axis=-1, keepdims=True)
    i2 = jnp.min(jnp.where(el2 == v2, lane, big), axis=-1, keepdims=True)
    e21 = jnp.exp(v2 - v1)
    w1 = g_w / (1.0 + e21)
    w2 = w1 * e21
    oh0 = jnp.where(lane == i1 - N_GROUPS, 1.0, 0.0)
    oh1 = jnp.where(lane == i2 - N_GROUPS, 1.0, 0.0)
    oh = oh0 + oh1
    nch = jnp.floor((jnp.sum(oh, axis=0, keepdims=True) + (CHUNK_ROWS - 1)) * (1.0 / CHUNK_ROWS))
    nch8 = jnp.broadcast_to(nch, (8, LANES))
    start = _dot(nch8.astype(BF16), utri_ref[...])[0:1] * CHUNK_ROWS
    rank = _dot(ltri_ref[...], oh.astype(BF16))
    row = start + rank
    pos0 = jnp.sum(row * oh0, axis=-1, keepdims=True)
    pos1 = jnp.sum(row * oh1, axis=-1, keepdims=True)
    route = jnp.where(lane == 0, pos0, jnp.where(lane == 1, pos1, 0.0))
    route_t = jnp.transpose(route)
    r = lax.broadcasted_iota(I32, (LOCAL_ROWS, ROW_TILE), 0).astype(F32)
    p0 = jnp.where(r == route_t[0:1, :], 1.0, 0.0).astype(BF16)
    p1 = jnp.where(r == route_t[1:2, :], 1.0, 0.0).astype(BF16)

    def gate_cols(w):
        hi = w.astype(BF16).astype(F32)
        return jnp.where(lane == 0, hi, jnp.where(lane == 1, w - hi, 0.0)).astype(BF16)

    sorted_rows = jnp.concatenate([_dot(p0 + p1, h3_hi), _dot(p0, gate_cols(w1)) + _dot(p1, gate_cols(w2))],
                                  axis=1)
    hs_ref[...] = sorted_rows.astype(BF16).reshape(hs_ref.shape)
    route_ref[...] = route
    cnt_ref[...] = nch8


def _token(x, oa, ob, gates, wb0, wb1, wmix, gx, wq, km, vm, wo, gf, wr, br, batch):
    t, d = x.shape
    n = TOKEN_TILES_PER_STEP
    nb = t // batch // (n * ROW_TILE)
    nt = t // ROW_TILE
    ltri = jnp.asarray(np.tril(np.ones((ROW_TILE, ROW_TILE), np.float32), -1), BF16)
    utri = jnp.asarray(np.triu(np.ones((LANES, LANES), np.float32), 1), BF16)
    cur = lambda b, i: (b * nb + i, 0)
    cur3 = lambda b, i: (b * nb + i, 0, 0)
    blk = lambda w: pl.BlockSpec((n * ROW_TILE, w), cur)
    full = lambda a: pl.BlockSpec(a.shape, lambda b, i: (0,) * a.ndim)
    mem = pl.BlockSpec((1,) + km.shape[1:], lambda b, i: (b, 0, 0))
    return pl.pallas_call(
        _token_kernel,
        grid=(batch, nb),
        in_specs=[blk(d), blk(A_WIDTH), blk(B_VAL_WIDTH), blk(2 * d), full(wb0), full(wb1), full(wmix),
                  full(gx), full(wq), mem, mem, full(wo), full(gf), full(wr), full(br), full(ltri), full(utri)],
        out_specs=[blk(d), pl.BlockSpec((n * LOCAL_CHUNKS, CHUNK_ROWS, SORT_WIDTH), cur3),
                   blk(LANES), pl.BlockSpec((n, 8, LANES), cur3)],
        out_shape=[jax.ShapeDtypeStruct((t, d), F32),
                   jax.ShapeDtypeStruct((nt * LOCAL_CHUNKS, CHUNK_ROWS, SORT_WIDTH), BF16),
                   jax.ShapeDtypeStruct((t, LANES), F32),
                   jax.ShapeDtypeStruct((nt, 8, LANES), F32)],
        compiler_params=_params("arbitrary", "arbitrary"),
        name="token",
    )(x, oa, ob, gates, wb0, wb1, wmix, gx, wq, km, vm, wo, gf, wr, br, ltri, utri)


def _expert_kernel(te_ref, nu_ref, nv_ref, ch_ref, hs_hbm, wg_ref, wu_ref, wd_ref, ys_hbm,
                   xbuf, ybuf, gsem, ssem):
    i = pl.program_id(0)
    n_used = nu_ref[0]
    slot = lax.rem(i, 2)

    def for_chunks(tile, fn):
        nv = nv_ref[tile]

        @pl.when(nv == TILE_CHUNKS)
        def _():
            for c in range(TILE_CHUNKS):
                fn(c)

        @pl.when(nv != TILE_CHUNKS)
        def _():
            def body(c, carry):
                fn(c)
                return carry

            lax.fori_loop(0, nv, body, 0)

    def gather(tile, s, start):
        def one(c):
            cp = pltpu.make_async_copy(hs_hbm.at[ch_ref[tile * TILE_CHUNKS + c]], xbuf.at[s, c], gsem.at[s])
            cp.start() if start else cp.wait()

        for_chunks(tile, one)

    def scatter(tile, s, start):
        def one(c):
            cp = pltpu.make_async_copy(ybuf.at[s, c], ys_hbm.at[ch_ref[tile * TILE_CHUNKS + c]], ssem.at[s])
            cp.start() if start else cp.wait()

        for_chunks(tile, one)

    @pl.when(i == 0)
    def _():
        xbuf[...] = jnp.zeros_like(xbuf)
        gather(0, 0, True)

    @pl.when(i < n_used)
    def _():
        @pl.when(i + 1 < n_used)
        def _():
            gather(i + 1, 1 - slot, True)

        gather(i, slot, False)

        @pl.when(i >= 2)
        def _():
            scatter(i - 2, slot, False)

        xg = xbuf[slot].reshape(EXPERT_ROWS, SORT_WIDTH)
        x = xg[:, :D_MODEL]
        hg = _dot(x, wg_ref[0])
        hu = _dot(x, wu_ref[0])
        hid = (hg * jax.nn.sigmoid(hg) * hu).astype(BF16)
        g = xg[:, D_MODEL:].astype(F32)
        y = ((g[:, 0:1] + g[:, 1:2]) * _dot(hid, wd_ref[0])).astype(BF16)
        y = jnp.concatenate([y, jnp.zeros((EXPERT_ROWS, LANES), BF16)], axis=1)
        ybuf[slot] = y.reshape(TILE_CHUNKS, CHUNK_ROWS, SORT_WIDTH)
        scatter(i, slot, True)

        @pl.when(i == n_used - 1)
        def _():
            scatter(i, slot, False)

            @pl.when(i >= 1)
            def _():
                scatter(i - 1, 1 - slot, False)


def _experts(hs, tile_expert, n_used, n_valid, chunks, wg, wu, wd):
    n_tiles = tile_expert.shape[0]
    last = lambda i, te, nu, nv, ch: jnp.minimum(i, nu[0] - 1)
    wmap = lambda i, te, nu, nv, ch: (te[last(i, te, nu, nv, ch)], 0, 0)
    anyspace = pl.BlockSpec(memory_space=pl.ANY)
    grid_spec = pltpu.PrefetchScalarGridSpec(
        num_scalar_prefetch=4,
        grid=(n_tiles,),
        in_specs=[anyspace,
                  pl.BlockSpec((1, D_MODEL, EXPERT_FF), wmap),
                  pl.BlockSpec((1, D_MODEL, EXPERT_FF), wmap),
                  pl.BlockSpec((1, EXPERT_FF, D_MODEL), wmap)],
        out_specs=anyspace,
        scratch_shapes=[pltpu.VMEM((2, TILE_CHUNKS, CHUNK_ROWS, SORT_WIDTH), BF16),
                        pltpu.VMEM((2, TILE_CHUNKS, CHUNK_ROWS, SORT_WIDTH), BF16),
                        pltpu.SemaphoreType.DMA((2,)), pltpu.SemaphoreType.DMA((2,))],
    )
    return pl.pallas_call(
        _expert_kernel,
        grid_spec=grid_spec,
        out_shape=jax.ShapeDtypeStruct(hs.shape, BF16),
        input_output_aliases={4: 0},
        compiler_params=_params("arbitrary"),
        name="experts",
    )(tile_expert, n_used, n_valid, chunks, hs, wg, wu, wd)


def _combine_kernel(final_norm, x_ref, route_ref, ys_ref, gfin_ref, o_ref):
    r = lax.broadcasted_iota(I32, (ROW_TILE, LOCAL_ROWS), 1).astype(F32)
    sel = jnp.where(r == route_ref[:, 0:1], 1.0, jnp.where(r == route_ref[:, 1:2], 1.0, 0.0)).astype(BF16)
    x3 = x_ref[...] + _dot(sel, ys_ref[...].reshape(LOCAL_ROWS, D_MODEL))
    if final_norm:
        x3 = _rms(x3, gfin_ref[...])
    o_ref[...] = x3


def _combine(x2, route, ys, gfin, final_norm):
    t, d = x2.shape
    nt = t // ROW_TILE
    return pl.pallas_call(
        functools.partial(_combine_kernel, final_norm),
        grid=(nt,),
        in_specs=[pl.BlockSpec((ROW_TILE, d), lambda i: (i, 0)),
                  pl.BlockSpec((ROW_TILE, LANES), lambda i: (i, 0)),
                  pl.BlockSpec((LOCAL_CHUNKS, CHUNK_ROWS, d), lambda i: (i, 0, 0)),
                  pl.BlockSpec((1, d), lambda i: (0, 0))],
        out_specs=pl.BlockSpec((ROW_TILE, d), lambda i: (i, 0)),
        out_shape=jax.ShapeDtypeStruct((t, d), F32),
        compiler_params=_params("arbitrary"),
        name="combine",
    )(x2, route, ys, gfin)


def _chunk_plan(nch, n_tiles):
    nt = nch.shape[0]
    local_start = jnp.cumsum(nch, axis=1) - nch
    cum = jnp.cumsum(nch, axis=0)
    total = cum[-1]
    tiles = (total + TILE_CHUNKS - 1) // TILE_CHUNKS
    tile_end = jnp.cumsum(tiles)
    n_used = tile_end[-1:]
    tile_ids = jnp.arange(n_tiles, dtype=I32)
    tile_expert = jnp.minimum(jnp.sum((tile_end[None, :] <= tile_ids[:, None]).astype(I32), axis=1),
                              N_EXPERTS - 1)
    sel = (tile_expert[:, None] == jnp.arange(N_EXPERTS, dtype=I32)[None, :]).astype(I32)
    pick = lambda table: jnp.sum(sel[:, :, None] * table.T[None, :, :], axis=1)
    first_tile = jnp.sum(sel * (tile_end - tiles)[None, :], axis=1)
    slot = (tile_ids - first_tile)[:, None] * TILE_CHUNKS + jnp.arange(TILE_CHUNKS, dtype=I32)[None, :]
    valid = (slot < jnp.sum(sel * total[None, :], axis=1)[:, None]) & (tile_ids < n_used)[:, None]
    src_tile = jnp.sum((pick(cum)[:, None, :] <= slot[:, :, None]).astype(I32), axis=2)
    src_tile = jnp.minimum(src_tile, nt - 1)
    at = (src_tile[:, :, None] == jnp.arange(nt, dtype=I32)[None, None, :]).astype(I32)
    before = jnp.sum(at * pick(cum - nch)[:, None, :], axis=2)
    start = jnp.sum(at * pick(local_start)[:, None, :], axis=2)
    chunk = jnp.where(valid, src_tile * LOCAL_CHUNKS + start + slot - before, 0)
    return tile_expert, n_used, jnp.sum(valid.astype(I32), axis=1), chunk.reshape(-1)


def kernel(x, mem, norm_mix_g, w_in, rel_bias, gla_w_alpha, gla_b_alpha, gla_norm_g, w_branch, w_mix_out, norm_x_g, mem_norm_g, w_xq, w_xkv, w_xo, norm_ffn_g, w_group_router, b_group_router, w_expert_router, b_expert_router, w_exp_gate, w_exp_up, w_exp_down, final_norm_g):
    batch, seq, d = x.shape
    depth = w_in.shape[0]
    t = batch * seq
    assert d == D_MODEL and seq % (TOKEN_TILES_PER_STEP * ROW_TILE) == 0
    nt = t // ROW_TILE
    n_tiles = nt * LOCAL_CHUNKS // TILE_CHUNKS + N_EXPERTS

    xf = x.reshape(t, d)
    km_all, vm_all = _memkv(mem, mem_norm_g, w_xkv.astype(BF16))
    row = lambda a: a.reshape(1, -1).astype(F32)

    for l in range(depth):
        wi = w_in[l]
        c0 = A_WIDTH * 3 + B_KEY_WIDTH * 2 + B_VAL_WIDTH
        w_main = jnp.concatenate([wi[:, :c0], wi[:, c0 + B_GATE_RANK:]], axis=1).astype(BF16)
        w_al = jnp.pad(wi[:, c0:c0 + B_GATE_RANK], ((0, 0), (0, LANES - B_GATE_RANK))).astype(BF16)
        w_al2 = jnp.pad(gla_w_alpha[l], ((0, LANES - B_GATE_RANK), (0, 0))).astype(BF16)
        aq, ak, av, bq, bk, bv, lga, br, gates = _inproj(
            xf, row(norm_mix_g[l]), w_main, w_al, w_al2, row(gla_b_alpha[l]))

        oa = _band_attention(aq, ak, av, _band_bias(rel_bias[l]), batch)
        ob = _gla(bq, bk, bv, lga, br, row(gla_norm_g[l]), batch)

        wr = jnp.pad(jnp.concatenate([w_group_router[l], w_expert_router[l]], axis=1).astype(F32),
                     ((0, 0), (0, LANES - N_GROUPS - N_EXPERTS)))
        wr_hi = wr.astype(BF16)
        wr = jnp.concatenate([wr_hi, (wr - wr_hi.astype(F32)).astype(BF16)], axis=1)
        brt = jnp.pad(jnp.concatenate([b_group_router[l], b_expert_router[l]]).astype(F32),
                      (0, LANES - N_GROUPS - N_EXPERTS)).reshape(1, LANES)
        x2, hs, route, cnt = _token(
            xf, oa, ob, gates, w_branch[l, 0].astype(BF16), w_branch[l, 1].astype(BF16),
            w_mix_out[l].astype(BF16), row(norm_x_g[l]), w_xq[l].astype(BF16), km_all[l], vm_all[l],
            w_xo[l].astype(BF16), row(norm_ffn_g[l]), wr, brt, batch)

        plan = _chunk_plan(cnt[:, 0, :N_EXPERTS].astype(I32), n_tiles)
        e3 = lambda w: w[l].reshape((N_EXPERTS,) + w.shape[3:]).astype(BF16)
        ys = _experts(hs, *plan, e3(w_exp_gate), e3(w_exp_up), e3(w_exp_down))
        xf = _combine(x2, route, ys, row(final_norm_g), l == depth - 1)

    return xf.reshape(batch, seq, d)
```

```python
import functools

import numpy as np
import jax
import jax.numpy as jnp
from jax import lax
from jax.experimental import pallas as pl
from jax.experimental.pallas import tpu as pltpu

F32 = jnp.float32
BF16 = jnp.bfloat16
I32 = jnp.int32

D_MODEL = 1024
CHUNK = 64
EPS = 1e-6
A_HEADS = 8
A_HEAD_DIM = 64
A_WIDTH = 512
A_LEFT_CHUNKS = 8
A_MAX_REL = 256
B_HEADS = 4
B_KEY_DIM = 64
B_VAL_DIM = 128
B_KEY_WIDTH = 256
B_VAL_WIDTH = 512
B_GATE_RANK = 16
B_GATE_TAU = 16.0
X_HEADS = 4
X_HEAD_DIM = 256
N_GROUPS = 4
EXPERTS_PER_GROUP = 8
N_EXPERTS = N_GROUPS * EXPERTS_PER_GROUP
EXPERT_FF = 256

LANES = 128
ROW_TILE = 256
CHUNKS_PER_TILE = ROW_TILE // CHUNK
BAND_TILES = A_LEFT_CHUNKS // CHUNKS_PER_TILE + 1
BAND_KEYS = BAND_TILES * ROW_TILE
LOG_CHUNK = 6
N_LEVELS = LOG_CHUNK
EXP_ROWS = (2 + N_LEVELS) * CHUNK
CHUNK_ROWS = 16
TOKEN_TILES_PER_STEP = 2
INPROJ_ROWS = 512
EXPERT_ROWS = 512
TILE_CHUNKS = EXPERT_ROWS // CHUNK_ROWS
LOCAL_CHUNKS = 2 * ROW_TILE // CHUNK_ROWS + N_EXPERTS
LOCAL_ROWS = LOCAL_CHUNKS * CHUNK_ROWS
SORT_WIDTH = D_MODEL + LANES
NEG = -1e30
LOG2E = 1.4426950408889634
VMEM_LIMIT = 56 * 1024 * 1024


def _params(*sem):
    return pltpu.CompilerParams(dimension_semantics=sem, vmem_limit_bytes=VMEM_LIMIT)


def _rms(x, g):
    return x * lax.rsqrt(jnp.mean(x * x, axis=-1, keepdims=True) + EPS) * g


def _dot(a, b):
    return jnp.dot(a, b, preferred_element_type=F32)


def _dot_nt(a, b):
    return lax.dot_general(a, b, (((1,), (1,)), ((), ())), preferred_element_type=F32)


def _dot_tn(a, b):
    return lax.dot_general(a, b, (((0,), (0,)), ((), ())), preferred_element_type=F32)


def _memkv_kernel(mem_ref, g_ref, w_ref, k_ref, v_ref):
    mn = _rms(mem_ref[0], g_ref[...]).astype(BF16)
    kv = _dot(mn, w_ref[0])
    k_ref[0, 0] = kv[:, :D_MODEL].astype(BF16)
    v_ref[0, 0] = kv[:, D_MODEL:].astype(BF16)


def _memkv(mem, g, w_xkv):
    depth = w_xkv.shape[0]
    b, m, d = mem.shape
    out = jax.ShapeDtypeStruct((depth, b, m, d), BF16)
    return pl.pallas_call(
        _memkv_kernel,
        grid=(depth, b),
        in_specs=[pl.BlockSpec((1, m, d), lambda l, i: (i, 0, 0)),
                  pl.BlockSpec((1, d), lambda l, i: (0, 0)),
                  pl.BlockSpec((1, d, 2 * d), lambda l, i: (l, 0, 0))],
        out_specs=[pl.BlockSpec((1, 1, m, d), lambda l, i: (l, i, 0, 0)),
                   pl.BlockSpec((1, 1, m, d), lambda l, i: (l, i, 0, 0))],
        out_shape=[out, out],
        compiler_params=_params("arbitrary", "arbitrary"),
        name="memkv",
    )(mem, g.reshape(1, d), w_xkv)


_C_AQ, _C_AK, _C_AV = 0, 512, 1024
_C_BQ, _C_BK, _C_BV = 1536, 1792, 2048
_C_BR, _C_GATE, _C_END = 2560, 3072, 5120
_W_HEAD = _C_BR
_W_PIECE = 512


def _inproj_kernel(x_ref, g_ref, wa_ref, wb_ref, wlast_ref, wal2_ref, bal_ref,
                   aq_ref, ak_ref, av_ref, bq_ref, bk_ref, bv_ref, lga_ref, br_ref, gate_ref,
                   w_bf, wal_bf, wal2_bf):
    @pl.when(pl.program_id(0) == 0)
    def _():
        lane = lax.broadcasted_iota(I32, (1, LANES), 1)
        for c in range(0, _W_HEAD, _W_PIECE):
            w_bf[:, c:c + _W_PIECE] = wa_ref[0, :, c:c + _W_PIECE].astype(BF16)
            hi = min(c + _W_PIECE + B_GATE_RANK, _W_HEAD)
            piece = wb_ref[0, :, c + B_GATE_RANK:hi]
            if hi == _W_HEAD:
                piece = jnp.concatenate([piece, wlast_ref[0]], axis=1)
            w_bf[:, _W_HEAD + c:_W_HEAD + c + _W_PIECE] = piece.astype(BF16)
        wal_bf[...] = jnp.where(lane < B_GATE_RANK, wb_ref[0, :, :LANES], 0.0).astype(BF16)
        wal2_bf[...] = jnp.concatenate(
            [wal2_ref[0].astype(BF16), jnp.zeros((LANES - B_GATE_RANK, B_KEY_WIDTH), BF16)], axis=0)

    h = _rms(x_ref[...], g_ref[...]).astype(BF16)

    def mm(lo, hi):
        return _dot(h, w_bf[:, lo:hi])

    aq_ref[...] = (mm(_C_AQ, _C_AK) * (A_HEAD_DIM ** -0.5 * LOG2E)).astype(BF16)
    ak_ref[...] = mm(_C_AK, _C_AV).astype(BF16)
    av_ref[...] = mm(_C_AV, _C_BQ).astype(BF16)
    bq_ref[...] = (mm(_C_BQ, _C_BK) * (B_KEY_DIM ** -0.5)).astype(BF16)
    bk_ref[...] = mm(_C_BK, _C_BV).astype(BF16)
    bv_ref[...] = mm(_C_BV, _C_BR).astype(BF16)
    r = mm(_C_BR, _C_GATE)
    br_ref[...] = (r * jax.nn.sigmoid(r)).astype(BF16)
    for c in range(_C_GATE, _C_END, 512):
        gate_ref[:, c - _C_GATE:c - _C_GATE + 512] = jax.nn.sigmoid(mm(c, c + 512)).astype(BF16)
    z = _dot(_dot(h, wal_bf[...]).astype(BF16), wal2_bf[...]) + bal_ref[...]
    lga_ref[...] = (jnp.minimum(z, 0.0) - jnp.log(1.0 + jnp.exp(-jnp.abs(z)))) * (1.0 / B_GATE_TAU)


def _inproj(x, g, w_in, w_al2, b_al, layer):
    t, d = x.shape
    assert w_in.shape[2] == 2 * _W_HEAD + B_GATE_RANK
    row = lambda w: pl.BlockSpec((INPROJ_ROWS, w), lambda i: (i, 0))
    full = lambda a: pl.BlockSpec(a.shape, lambda i: (0,) * a.ndim)
    once = pl.Buffered(1)
    w_last = w_in[:, :, 2 * _W_HEAD:]
    sds = lambda w, dt: jax.ShapeDtypeStruct((t, w), dt)
    widths = [(512, BF16), (512, BF16), (512, BF16), (256, BF16), (256, BF16), (512, BF16),
              (256, F32), (512, BF16), (2048, BF16)]
    return pl.pallas_call(
        _inproj_kernel,
        grid=(t // INPROJ_ROWS,),
        in_specs=[row(d), full(g),
                  pl.BlockSpec((1, d, _W_HEAD), lambda i: (layer, 0, 0), pipeline_mode=once),
                  pl.BlockSpec((1, d, _W_HEAD), lambda i: (layer, 0, 1), pipeline_mode=once),
                  pl.BlockSpec((1, d, B_GATE_RANK), lambda i: (layer, 0, 0), pipeline_mode=once),
                  pl.BlockSpec((1,) + w_al2.shape[1:], lambda i: (layer, 0, 0)),
                  full(b_al)],
        out_specs=[row(w) for w, _ in widths],
        out_shape=[sds(w, dt) for w, dt in widths],
        scratch_shapes=[pltpu.VMEM((d, 2 * _W_HEAD), BF16), pltpu.VMEM((d, LANES), BF16),
                        pltpu.VMEM((LANES, B_KEY_WIDTH), BF16)],
        compiler_params=_params("arbitrary"),
        name="inproj",
    )(x, g, w_in, w_in, w_last, w_al2, b_al)


def _band_kernel(q_ref, k0_ref, k1_ref, k2_ref, v0_ref, v1_ref, v2_ref, bias_ref, o_ref):
    i = pl.program_id(1)
    lane = lax.broadcasted_iota(I32, (1, LANES), 1)
    low = lane < A_HEAD_DIM
    ones = jnp.ones((BAND_KEYS, LANES), BF16)

    def attend(pen):
        for p in range(A_HEADS // 2):
            sl = slice(p * LANES, (p + 1) * LANES)
            qp = q_ref[:, sl]
            zero = jnp.zeros_like(qp)
            q2 = jnp.concatenate([jnp.where(low, qp, zero), jnp.where(low, zero, qp)], axis=0)
            kp = jnp.concatenate([k0_ref[:, sl], k1_ref[:, sl], k2_ref[:, sl]], axis=0)
            vp = jnp.concatenate([v0_ref[:, sl], v1_ref[:, sl], v2_ref[:, sl]], axis=0)
            s = _dot_nt(q2, kp) + bias_ref[p]
            if pen is not None:
                s = s + pen
            pe = jnp.exp2(s - jnp.max(s, axis=-1, keepdims=True)).astype(BF16)
            o2 = _dot(pe, jnp.concatenate([vp, ones], axis=1))
            o = o2[:, :LANES] * (1.0 / o2[:, LANES:])
            o_ref[:, sl] = jnp.where(low, o[:ROW_TILE], o[ROW_TILE:]).astype(BF16)

    @pl.when(i >= BAND_TILES - 1)
    def _():
        attend(None)

    @pl.when(i < BAND_TILES - 1)
    def _():
        col = lax.broadcasted_iota(I32, (1, BAND_KEYS), 1)
        attend(jnp.where(col < (BAND_TILES - 1 - i) * ROW_TILE, NEG, 0.0).astype(F32))


def _band_bias(rel_table):
    h = rel_table.shape[0]
    tab = rel_table.astype(F32)
    shift = A_LEFT_CHUNKS * CHUNK + ROW_TILE - 1
    n_far = shift - A_MAX_REL + 1
    span = ROW_TILE + BAND_KEYS - 1
    assert span - 1 - shift <= A_MAX_REL
    u = jnp.concatenate([jnp.broadcast_to(tab[:, 2 * A_MAX_REL:], (h, n_far)),
                         tab[:, 2 * A_MAX_REL - 1:2 * A_MAX_REL - 1 - (span - n_far):-1]], axis=1)
    period = span + 1
    u = jnp.roll(jnp.pad(u, ((0, 0), (0, period - span))), -(ROW_TILE - 1), axis=1)
    rows = jnp.tile(u, (1, ROW_TILE))[:, :ROW_TILE * (period - 1)].reshape(h, ROW_TILE, period - 1)
    bias = rows[:, :, :BAND_KEYS]
    q = np.arange(ROW_TILE)[:, None]
    k = np.arange(BAND_KEYS)[None, :]
    cq, ck = q // CHUNK, k // CHUNK
    valid = (ck >= cq) & (ck <= cq + A_LEFT_CHUNKS)
    bias = jnp.where(valid[None], bias * LOG2E, NEG)
    return bias.reshape(h // 2, 2 * ROW_TILE, BAND_KEYS)


def _band_attention(q, k, v, bias, batch):
    t, w = q.shape
    nb = t // batch // ROW_TILE
    cur = lambda b, i: (b * nb + i, 0)
    back = lambda n: (lambda b, i: (b * nb + jnp.maximum(i - n, 0), 0))
    blk = lambda f: pl.BlockSpec((ROW_TILE, w), f)
    return pl.pallas_call(
        _band_kernel,
        grid=(batch, nb),
        in_specs=[blk(cur), blk(back(2)), blk(back(1)), blk(cur), blk(back(2)), blk(back(1)), blk(cur),
                  pl.BlockSpec(bias.shape, lambda b, i: (0, 0, 0))],
        out_specs=blk(cur),
        out_shape=jax.ShapeDtypeStruct((t, w), BF16),
        compiler_params=_params("arbitrary", "arbitrary"),
        name="band_attn",
    )(q, k, k, k, v, v, v, bias)


def _gla_constants():
    c = CHUNK
    t = np.arange(c)[:, None]
    r = np.arange(c)[None, :]
    mats = [(r <= t), (r > t)]
    lvl = np.full((c, c), -1, np.int32)
    lvl[np.arange(c), np.arange(c)] = N_LEVELS
    for l in range(N_LEVELS):
        m = (c // 2) >> l
        mid = (t // (2 * m)) * (2 * m) + m
        upper = t >= mid
        mats.append(np.where(upper, (r >= mid) & (r <= t), (r > t) & (r < mid)))
        s = r
        same = (s // (2 * m)) == (t // (2 * m))
        lvl[np.asarray(same & upper & (s < mid))] = l
    eye = np.eye(CHUNKS_PER_TILE)
    mexp = np.concatenate([np.kron(eye, m) for m in mats], axis=0).astype(np.float32)
    lvl = np.tile(lvl, (1, B_HEADS))
    return jnp.asarray(mexp, BF16), jnp.asarray(lvl, I32)


def _gla_kernel(q_ref, k_ref, v_ref, g_ref, r_ref, gn_ref, mexp_ref, lvl_ref, o_ref, s_ref):
    @pl.when(pl.program_id(1) == 0)
    def _():
        s_ref[...] = jnp.zeros_like(s_ref)

    kw = B_KEY_WIDTH
    ri = lax.broadcasted_iota(I32, (kw, kw), 0) >> LOG_CHUNK
    ci = lax.broadcasted_iota(I32, (kw, kw), 1) >> LOG_CHUNK
    bd = ri == ci
    head_ind = jnp.where(bd, 1.0, 0.0).astype(BF16)
    ri2 = lax.broadcasted_iota(I32, (kw, 2 * kw), 0) >> LOG_CHUNK
    ci2 = (lax.broadcasted_iota(I32, (kw, 2 * kw), 1) & (kw - 1)) >> LOG_CHUNK
    bd2 = ri2 == ci2
    lvl = lvl_ref[...]
    row8 = lax.broadcasted_iota(I32, (16, kw), 0)
    ones = jnp.ones((16, LANES), BF16)
    zero_b = jnp.zeros((kw, kw), BF16)
    chunks = [slice(c * CHUNK, (c + 1) * CHUNK) for c in range(CHUNKS_PER_TILE)]

    def head_blocks(x):
        return jnp.where(bd, jnp.concatenate([x] * B_HEADS, axis=0), zero_b)

    q = q_ref[...].astype(F32)
    k = k_ref[...].astype(F32)
    g = g_ref[...]
    gb = g.astype(BF16)
    half = EXP_ROWS * CHUNKS_PER_TILE // 2
    w = jnp.exp(jnp.concatenate([_dot(mexp_ref[:half, :], gb), _dot(mexp_ref[half:, :], gb)], axis=0))
    w_cum = w[0:ROW_TILE]
    qt = (q * w_cum).astype(BF16)
    kb = (k * w[ROW_TILE:2 * ROW_TILE]).astype(BF16)
    qk = (q * k).astype(BF16)

    attn = [jnp.zeros((CHUNK, kw), F32) for _ in chunks]
    for l in range(N_LEVELS):
        wl = w[(2 + l) * ROW_TILE:(3 + l) * ROW_TILE]
        qh = (q * wl).astype(BF16)
        kh = (k * wl).astype(BF16)
        for c, rows in enumerate(chunks):
            attn[c] = jnp.where(lvl == l, _dot_nt(qh[rows], head_blocks(kh[rows])), attn[c])

    vstacks, kvs, dcols = [], [], []
    for c, rows in enumerate(chunks):
        attn[c] = jnp.where(lvl == N_LEVELS, _dot(qk[rows], head_ind), attn[c])
        v = v_ref[rows, :]
        vstack = jnp.concatenate([v[:, j * LANES:(j + 1) * LANES] for j in range(B_HEADS)], axis=0)
        vstacks.append(vstack)
        kvs.append(_dot_tn(head_blocks(kb[rows]), vstack))
        d = jnp.exp(jnp.sum(g[rows], axis=0, keepdims=True))
        d1 = d.astype(BF16).astype(F32)
        dp = jnp.where(row8 == 0, d1, jnp.where(row8 == 1, d - d1, 0.0)).astype(BF16)
        dcols.append(_dot_tn(dp, ones))

    s = s_ref[...]
    for c, rows in enumerate(chunks):
        lhs = jnp.concatenate([attn[c].astype(BF16), qt[rows]], axis=1)
        lhs = jnp.where(bd2, jnp.concatenate([lhs] * B_HEADS, axis=0), jnp.zeros((kw, 2 * kw), BF16))
        rhs = jnp.concatenate([vstacks[c], s.astype(BF16)], axis=0)
        o = _dot(lhs, rhs)
        s = dcols[c] * s + kvs[c]
        for j in range(B_HEADS):
            oj = o[j * CHUNK:(j + 1) * CHUNK]
            sl = slice(j * LANES, (j + 1) * LANES)
            y = oj * lax.rsqrt(jnp.mean(oj * oj, axis=-1, keepdims=True) + EPS) * gn_ref[...]
            o_ref[rows, sl] = (y * r_ref[rows, sl].astype(F32)).astype(BF16)
    s_ref[...] = s


def _gla(q, k, v, g, r, gn, batch):
    t = q.shape[0]
    nb = t // batch // ROW_TILE
    mexp, lvl = _gla_constants()
    cur = lambda b, i: (b * nb + i, 0)
    blk = lambda w: pl.BlockSpec((ROW_TILE, w), cur)
    full = lambda a: pl.BlockSpec(a.shape, lambda b, i: (0,) * a.ndim)
    return pl.pallas_call(
        _gla_kernel,
        grid=(batch, nb),
        in_specs=[blk(B_KEY_WIDTH), blk(B_KEY_WIDTH), blk(B_VAL_WIDTH), blk(B_KEY_WIDTH), blk(B_VAL_WIDTH),
                  full(gn), full(mexp), full(lvl)],
        out_specs=blk(B_VAL_WIDTH),
        out_shape=jax.ShapeDtypeStruct((t, B_VAL_WIDTH), BF16),
        scratch_shapes=[pltpu.VMEM((B_KEY_WIDTH, B_VAL_DIM), F32)],
        compiler_params=_params("arbitrary", "arbitrary"),
        name="gla",
    )(q, k, v, g, r, gn, mexp, lvl)


def _token_kernel(x_ref, oa_ref, ob_ref, gate_ref, wb0_ref, wb1_ref, wmix_ref, gx_ref, wq_ref,
                  km_ref, vm_ref, wo_ref, gf_ref, wr_ref, br_ref, ltri_ref, utri_ref,
                  x2_ref, hs_ref, route_ref, cnt_ref):
    ma = _dot(oa_ref[...], wb0_ref[...])
    mb = _dot(ob_ref[...], wb1_ref[...])
    merged = (gate_ref[:, :D_MODEL].astype(F32) * ma + gate_ref[:, D_MODEL:].astype(F32) * mb).astype(BF16)
    x1 = x_ref[...] + _dot(merged, wmix_ref[...])

    h2 = _rms(x1, gx_ref[...]).astype(BF16)
    qx = (_dot(h2, wq_ref[...]) * (X_HEAD_DIM ** -0.5)).astype(BF16)
    heads = []
    for h in range(X_HEADS):
        sl = slice(h * X_HEAD_DIM, (h + 1) * X_HEAD_DIM)
        s = _dot_nt(qx[:, sl], km_ref[0, :, sl])
        m = jnp.max(s, axis=-1, keepdims=True)
        pe = jnp.exp(s - m)
        l = jnp.sum(pe, axis=-1, keepdims=True)
        heads.append((_dot(pe.astype(BF16), vm_ref[0, :, sl]) * (1.0 / l)).astype(BF16))
    x2 = x1 + _dot(jnp.concatenate(heads, axis=1), wo_ref[...])
    x2_ref[...] = x2

    h3 = _rms(x2, gf_ref[...])

    h3_hi = h3.astype(BF16)
    h3_lo = (h3 - h3_hi.astype(F32)).astype(BF16)
    hw = _dot(h3_hi, wr_ref[...])
    logits = hw[:, :LANES] + hw[:, LANES:] + _dot(h3_lo, wr_ref[:, :LANES]) + br_ref[...]
    for h in range(TOKEN_TILES_PER_STEP):
        rows = slice(h * ROW_TILE, (h + 1) * ROW_TILE)
        chunks = pl.ds(h * LOCAL_CHUNKS, LOCAL_CHUNKS)
        _route_and_sort(logits[rows], h3_hi[rows], ltri_ref, utri_ref, hs_ref.at[chunks],
                        route_ref.at[pl.ds(h * ROW_TILE, ROW_TILE)], cnt_ref.at[h])


def _route_and_sort(logits, h3_hi, ltri_ref, utri_ref, hs_ref, route_ref, cnt_ref):
    lane = lax.broadcasted_iota(I32, logits.shape, 1).astype(F32)
    big = jnp.float32(LANES)
    gl = jnp.where(lane < N_GROUPS, logits, NEG)
    gmax = jnp.max(gl, axis=-1, keepdims=True)
    gidx = jnp.min(jnp.where(gl == gmax, lane, big), axis=-1, keepdims=True)
    g_w = 1.0 / jnp.sum(jnp.exp(gl - gmax), axis=-1, keepdims=True)
    lo = N_GROUPS + EXPERTS_PER_GROUP * gidx
    el = jnp.where((lane >= lo) & (lane < lo + EXPERTS_PER_GROUP), logits, NEG)
    v1 = jnp.max(el, axis=-1, keepdims=True)
    i1 = jnp.min(jnp.where(el == v1, lane, big), axis=-1, keepdims=True)
    el2 = jnp.where(lane == i1, NEG, el)
    v2 = jnp.max(el2, axis=-1, keepdims=True)
    i2 = jnp.min(jnp.where(el2 == v2, lane, big), axis=-1, keepdims=True)
    e21 = jnp.exp(v2 - v1)
    w1 = g_w / (1.0 + e21)
    w2 = w1 * e21
    oh0 = jnp.where(lane == i1 - N_GROUPS, 1.0, 0.0)
    oh1 = jnp.where(lane == i2 - N_GROUPS, 1.0, 0.0)
    oh = oh0 + oh1
    nch = jnp.floor((jnp.sum(oh, axis=0, keepdims=True) + (CHUNK_ROWS - 1)) * (1.0 / CHUNK_ROWS))
    nch8 = jnp.broadcast_to(nch, (8, LANES))
    start = _dot(nch8.astype(BF16), utri_ref[...])[0:1] * CHUNK_ROWS
    rank = _dot(ltri_ref[...], oh.astype(BF16))
    row = start + rank
    pos0 = jnp.sum(row * oh0, axis=-1, keepdims=True)
    pos1 = jnp.sum(row * oh1, axis=-1, keepdims=True)
    route = jnp.where(lane == 0, pos0, jnp.where(lane == 1, pos1, 0.0))
    route_t = jnp.transpose(route)
    r = lax.broadcasted_iota(I32, (LOCAL_ROWS, ROW_TILE), 0).astype(F32)
    p0 = jnp.where(r == route_t[0:1, :], 1.0, 0.0).astype(BF16)
    p1 = jnp.where(r == route_t[1:2, :], 1.0, 0.0).astype(BF16)

    def gate_cols(w):
        hi = w.astype(BF16).astype(F32)
        return jnp.where(lane == 0, hi, jnp.where(lane == 1, w - hi, 0.0)).astype(BF16)

    sorted_rows = jnp.concatenate([_dot(p0 + p1, h3_hi), _dot(p0, gate_cols(w1)) + _dot(p1, gate_cols(w2))],
                                  axis=1)
    hs_ref[...] = sorted_rows.astype(BF16).reshape(hs_ref.shape)
    route_ref[...] = route
    cnt_ref[...] = nch8


def _token(x, oa, ob, gates, wb0, wb1, wmix, gx, wq, km, vm, wo, gf, wr, br, batch):
    t, d = x.shape
    n = TOKEN_TILES_PER_STEP
    nb = t // batch // (n * ROW_TILE)
    nt = t // ROW_TILE
    ltri = jnp.asarray(np.tril(np.ones((ROW_TILE, ROW_TILE), np.float32), -1), BF16)
    utri = jnp.asarray(np.triu(np.ones((LANES, LANES), np.float32), 1), BF16)
    cur = lambda b, i: (b * nb + i, 0)
    cur3 = lambda b, i: (b * nb + i, 0, 0)
    blk = lambda w: pl.BlockSpec((n * ROW_TILE, w), cur)
    full = lambda a: pl.BlockSpec(a.shape, lambda b, i: (0,) * a.ndim)
    mem = pl.BlockSpec((1,) + km.shape[1:], lambda b, i: (b, 0, 0))
    return pl.pallas_call(
        _token_kernel,
        grid=(batch, nb),
        in_specs=[blk(d), blk(A_WIDTH), blk(B_VAL_WIDTH), blk(2 * d), full(wb0), full(wb1), full(wmix),
                  full(gx), full(wq), mem, mem, full(wo), full(gf), full(wr), full(br), full(ltri), full(utri)],
        out_specs=[blk(d), pl.BlockSpec((n * LOCAL_CHUNKS, CHUNK_ROWS, SORT_WIDTH), cur3),
                   blk(LANES), pl.BlockSpec((n, 8, LANES), cur3)],
        out_shape=[jax.ShapeDtypeStruct((t, d), F32),
                   jax.ShapeDtypeStruct((nt * LOCAL_CHUNKS, CHUNK_ROWS, SORT_WIDTH), BF16),
                   jax.ShapeDtypeStruct((t, LANES), F32),
                   jax.ShapeDtypeStruct((nt, 8, LANES), F32)],
        compiler_params=_params("arbitrary", "arbitrary"),
        name="token",
    )(x, oa, ob, gates, wb0, wb1, wmix, gx, wq, km, vm, wo, gf, wr, br, ltri, utri)


def _expert_kernel(te_ref, nu_ref, nv_ref, ch_ref, hs_hbm, wg_ref, wu_ref, wd_ref, ys_hbm,
                   xbuf, ybuf, wg_bf, wu_bf, wd_bf, gsem, ssem):
    i = pl.program_id(0)
    n_used = nu_ref[0]
    slot = lax.rem(i, 2)

    def for_chunks(tile, fn):
        nv = nv_ref[tile]

        @pl.when(nv == TILE_CHUNKS)
        def _():
            for c in range(TILE_CHUNKS):
                fn(c)

        @pl.when(nv != TILE_CHUNKS)
        def _():
            def body(c, carry):
                fn(c)
                return carry

            lax.fori_loop(0, nv, body, 0)

    def gather(tile, s, start):
        def one(c):
            cp = pltpu.make_async_copy(hs_hbm.at[ch_ref[tile * TILE_CHUNKS + c]], xbuf.at[s, c], gsem.at[s])
            cp.start() if start else cp.wait()

        for_chunks(tile, one)

    def scatter(tile, s, start):
        def one(c):
            cp = pltpu.make_async_copy(ybuf.at[s, c], ys_hbm.at[ch_ref[tile * TILE_CHUNKS + c]], ssem.at[s])
            cp.start() if start else cp.wait()

        for_chunks(tile, one)

    @pl.when(i == 0)
    def _():
        xbuf[...] = jnp.zeros_like(xbuf)
        gather(0, 0, True)

    @pl.when(i < n_used)
    def _():
        @pl.when(i + 1 < n_used)
        def _():
            gather(i + 1, 1 - slot, True)

        gather(i, slot, False)

        @pl.when(i >= 2)
        def _():
            scatter(i - 2, slot, False)

        @pl.when((i == 0) | (te_ref[i] != te_ref[jnp.maximum(i - 1, 0)]))
        def _():
            wg_bf[...] = wg_ref[0].astype(BF16)
            wu_bf[...] = wu_ref[0].astype(BF16)
            wd_bf[...] = wd_ref[0].astype(BF16)

        xg = xbuf[slot].reshape(EXPERT_ROWS, SORT_WIDTH)
        x = xg[:, :D_MODEL]
        hg = _dot(x, wg_bf[...])
        hu = _dot(x, wu_bf[...])
        hid = (hg * jax.nn.sigmoid(hg) * hu).astype(BF16)
        g = xg[:, D_MODEL:].astype(F32)
        y = ((g[:, 0:1] + g[:, 1:2]) * _dot(hid, wd_bf[...])).astype(BF16)
        y = jnp.concatenate([y, jnp.zeros((EXPERT_ROWS, LANES), BF16)], axis=1)
        ybuf[slot] = y.reshape(TILE_CHUNKS, CHUNK_ROWS, SORT_WIDTH)
        scatter(i, slot, True)

        @pl.when(i == n_used - 1)
        def _():
            scatter(i, slot, False)

            @pl.when(i >= 1)
            def _():
                scatter(i - 1, 1 - slot, False)


def _experts(hs, tile_expert, n_used, n_valid, chunks, wg, wu, wd, layer):
    n_tiles = tile_expert.shape[0]
    last = lambda i, te, nu, nv, ch: jnp.minimum(i, nu[0] - 1)
    wmap = lambda i, te, nu, nv, ch: (layer * N_EXPERTS + te[last(i, te, nu, nv, ch)], 0, 0)
    anyspace = pl.BlockSpec(memory_space=pl.ANY)
    grid_spec = pltpu.PrefetchScalarGridSpec(
        num_scalar_prefetch=4,
        grid=(n_tiles,),
        in_specs=[anyspace,
                  pl.BlockSpec((1, D_MODEL, EXPERT_FF), wmap),
                  pl.BlockSpec((1, D_MODEL, EXPERT_FF), wmap),
                  pl.BlockSpec((1, EXPERT_FF, D_MODEL), wmap)],
        out_specs=anyspace,
        scratch_shapes=[pltpu.VMEM((2, TILE_CHUNKS, CHUNK_ROWS, SORT_WIDTH), BF16),
                        pltpu.VMEM((2, TILE_CHUNKS, CHUNK_ROWS, SORT_WIDTH), BF16),
                        pltpu.VMEM((D_MODEL, EXPERT_FF), BF16), pltpu.VMEM((D_MODEL, EXPERT_FF), BF16),
                        pltpu.VMEM((EXPERT_FF, D_MODEL), BF16),
                        pltpu.SemaphoreType.DMA((2,)), pltpu.SemaphoreType.DMA((2,))],
    )
    return pl.pallas_call(
        _expert_kernel,
        grid_spec=grid_spec,
        out_shape=jax.ShapeDtypeStruct(hs.shape, BF16),
        input_output_aliases={4: 0},
        compiler_params=_params("arbitrary"),
        name="experts",
    )(tile_expert, n_used, n_valid, chunks, hs, wg, wu, wd)


def _combine_kernel(final_norm, x_ref, route_ref, ys_ref, gfin_ref, o_ref):
    r = lax.broadcasted_iota(I32, (ROW_TILE, LOCAL_ROWS), 1).astype(F32)
    sel = jnp.where(r == route_ref[:, 0:1], 1.0, jnp.where(r == route_ref[:, 1:2], 1.0, 0.0)).astype(BF16)
    x3 = x_ref[...] + _dot(sel, ys_ref[...].reshape(LOCAL_ROWS, D_MODEL))
    if final_norm:
        x3 = _rms(x3, gfin_ref[...])
    o_ref[...] = x3


def _combine(x2, route, ys, gfin, final_norm):
    t, d = x2.shape
    nt = t // ROW_TILE
    return pl.pallas_call(
        functools.partial(_combine_kernel, final_norm),
        grid=(nt,),
        in_specs=[pl.BlockSpec((ROW_TILE, d), lambda i: (i, 0)),
                  pl.BlockSpec((ROW_TILE, LANES), lambda i: (i, 0)),
                  pl.BlockSpec((LOCAL_CHUNKS, CHUNK_ROWS, d), lambda i: (i, 0, 0)),
                  pl.BlockSpec((1, d), lambda i: (0, 0))],
        out_specs=pl.BlockSpec((ROW_TILE, d), lambda i: (i, 0)),
        out_shape=jax.ShapeDtypeStruct((t, d), F32),
        compiler_params=_params("arbitrary"),
        name="combine",
    )(x2, route, ys, gfin)


def _chunk_plan(nch, n_tiles):
    nt = nch.shape[0]
    local_start = jnp.cumsum(nch, axis=1) - nch
    cum = jnp.cumsum(nch, axis=0)
    total = cum[-1]
    tiles = (total + TILE_CHUNKS - 1) // TILE_CHUNKS
    tile_end = jnp.cumsum(tiles)
    n_used = tile_end[-1:]
    tile_ids = jnp.arange(n_tiles, dtype=I32)
    tile_expert = jnp.minimum(jnp.sum((tile_end[None, :] <= tile_ids[:, None]).astype(I32), axis=1),
                              N_EXPERTS - 1)
    sel = (tile_expert[:, None] == jnp.arange(N_EXPERTS, dtype=I32)[None, :]).astype(I32)
    pick = lambda table: jnp.sum(sel[:, :, None] * table.T[None, :, :], axis=1)
    first_tile = jnp.sum(sel * (tile_end - tiles)[None, :], axis=1)
    slot = (tile_ids - first_tile)[:, None] * TILE_CHUNKS + jnp.arange(TILE_CHUNKS, dtype=I32)[None, :]
    valid = (slot < jnp.sum(sel * total[None, :], axis=1)[:, None]) & (tile_ids < n_used)[:, None]
    src_tile = jnp.sum((pick(cum)[:, None, :] <= slot[:, :, None]).astype(I32), axis=2)
    src_tile = jnp.minimum(src_tile, nt - 1)
    at = (src_tile[:, :, None] == jnp.arange(nt, dtype=I32)[None, None, :]).astype(I32)
    before = jnp.sum(at * pick(cum - nch)[:, None, :], axis=2)
    start = jnp.sum(at * pick(local_start)[:, None, :], axis=2)
    chunk = jnp.where(valid, src_tile * LOCAL_CHUNKS + start + slot - before, 0)
    return tile_expert, n_used, jnp.sum(valid.astype(I32), axis=1), chunk.reshape(-1)


def kernel(x, mem, norm_mix_g, w_in, rel_bias, gla_w_alpha, gla_b_alpha, gla_norm_g, w_branch, w_mix_out, norm_x_g, mem_norm_g, w_xq, w_xkv, w_xo, norm_ffn_g, w_group_router, b_group_router, w_expert_router, b_expert_router, w_exp_gate, w_exp_up, w_exp_down, final_norm_g):
    batch, seq, d = x.shape
    depth = w_in.shape[0]
    t = batch * seq
    assert d == D_MODEL and seq % (TOKEN_TILES_PER_STEP * ROW_TILE) == 0
    nt = t // ROW_TILE
    n_tiles = nt * LOCAL_CHUNKS // TILE_CHUNKS + N_EXPERTS

    xf = x.reshape(t, d)
    km_all, vm_all = _memkv(mem, mem_norm_g, w_xkv.astype(BF16))
    row = lambda a: a.reshape(1, -1).astype(F32)

    for l in range(depth):
        aq, ak, av, bq, bk, bv, lga, br, gates = _inproj(
            xf, row(norm_mix_g[l]), w_in, gla_w_alpha, row(gla_b_alpha[l]), l)

        oa = _band_attention(aq, ak, av, _band_bias(rel_bias[l]), batch)
        ob = _gla(bq, bk, bv, lga, br, row(gla_norm_g[l]), batch)

        wr = jnp.pad(jnp.concatenate([w_group_router[l], w_expert_router[l]], axis=1).astype(F32),
                     ((0, 0), (0, LANES - N_GROUPS - N_EXPERTS)))
        wr_hi = wr.astype(BF16)
        wr = jnp.concatenate([wr_hi, (wr - wr_hi.astype(F32)).astype(BF16)], axis=1)
        brt = jnp.pad(jnp.concatenate([b_group_router[l], b_expert_router[l]]).astype(F32),
                      (0, LANES - N_GROUPS - N_EXPERTS)).reshape(1, LANES)
        x2, hs, route, cnt = _token(
            xf, oa, ob, gates, w_branch[l, 0].astype(BF16), w_branch[l, 1].astype(BF16),
            w_mix_out[l].astype(BF16), row(norm_x_g[l]), w_xq[l].astype(BF16), km_all[l], vm_all[l],
            w_xo[l].astype(BF16), row(norm_ffn_g[l]), wr, brt, batch)

        plan = _chunk_plan(cnt[:, 0, :N_EXPERTS].astype(I32), n_tiles)
        e3 = lambda w: w.reshape((depth * N_EXPERTS,) + w.shape[3:])
        ys = _experts(hs, *plan, e3(w_exp_gate), e3(w_exp_up), e3(w_exp_down), l)
        xf = _combine(x2, route, ys, row(final_norm_g), l == depth - 1)

    return xf.reshape(batch, seq, d)
```

```python
import functools

import numpy as np
import jax
import jax.numpy as jnp
from jax import lax
from jax.experimental import pallas as pl
from jax.experimental.pallas import tpu as pltpu

F32 = jnp.float32
BF16 = jnp.bfloat16
I32 = jnp.int32

D_MODEL = 1024
CHUNK = 64
EPS = 1e-6
A_HEADS = 8
A_HEAD_DIM = 64
A_WIDTH = 512
A_LEFT_CHUNKS = 8
A_MAX_REL = 256
B_HEADS = 4
B_KEY_DIM = 64
B_VAL_DIM = 128
B_KEY_WIDTH = 256
B_VAL_WIDTH = 512
B_GATE_RANK = 16
B_GATE_TAU = 16.0
X_HEADS = 4
X_HEAD_DIM = 256
N_GROUPS = 4
EXPERTS_PER_GROUP = 8
N_EXPERTS = N_GROUPS * EXPERTS_PER_GROUP
EXPERT_FF = 256

LANES = 128
ROW_TILE = 256
CHUNKS_PER_TILE = ROW_TILE // CHUNK
BAND_TILES = A_LEFT_CHUNKS // CHUNKS_PER_TILE + 1
BAND_KEYS = BAND_TILES * ROW_TILE
BIAS_PERIOD = 1024
LOG_CHUNK = 6
N_LEVELS = LOG_CHUNK
EXP_ROWS = (2 + N_LEVELS) * CHUNK
CHUNK_ROWS = 16
TOKEN_TILES_PER_STEP = 2
INPROJ_ROWS = 512
EXPERT_ROWS = 512
TILE_CHUNKS = EXPERT_ROWS // CHUNK_ROWS
LOCAL_CHUNKS = 2 * ROW_TILE // CHUNK_ROWS + N_EXPERTS
LOCAL_ROWS = LOCAL_CHUNKS * CHUNK_ROWS
SORT_WIDTH = D_MODEL + LANES
NEG = -1e30
LOG2E = 1.4426950408889634
VMEM_LIMIT = 56 * 1024 * 1024


def _params(*sem):
    return pltpu.CompilerParams(dimension_semantics=sem, vmem_limit_bytes=VMEM_LIMIT)


def _rms(x, g):
    return x * lax.rsqrt(jnp.mean(x * x, axis=-1, keepdims=True) + EPS) * g


def _dot(a, b):
    return jnp.dot(a, b, preferred_element_type=F32)


def _dot_nt(a, b):
    return lax.dot_general(a, b, (((1,), (1,)), ((), ())), preferred_element_type=F32)


def _dot_tn(a, b):
    return lax.dot_general(a, b, (((0,), (0,)), ((), ())), preferred_element_type=F32)


def _memkv_kernel(mem_ref, g_ref, w_ref, k_ref, v_ref):
    mn = _rms(mem_ref[0], g_ref[...]).astype(BF16)
    kv = _dot(mn, w_ref[0])
    k_ref[0, 0] = kv[:, :D_MODEL].astype(BF16)
    v_ref[0, 0] = kv[:, D_MODEL:].astype(BF16)


def _memkv(mem, g, w_xkv):
    depth = w_xkv.shape[0]
    b, m, d = mem.shape
    out = jax.ShapeDtypeStruct((depth, b, m, d), BF16)
    return pl.pallas_call(
        _memkv_kernel,
        grid=(depth, b),
        in_specs=[pl.BlockSpec((1, m, d), lambda l, i: (i, 0, 0)),
                  pl.BlockSpec((1, d), lambda l, i: (0, 0)),
                  pl.BlockSpec((1, d, 2 * d), lambda l, i: (l, 0, 0))],
        out_specs=[pl.BlockSpec((1, 1, m, d), lambda l, i: (l, i, 0, 0)),
                   pl.BlockSpec((1, 1, m, d), lambda l, i: (l, i, 0, 0))],
        out_shape=[out, out],
        compiler_params=_params("arbitrary", "arbitrary"),
        name="memkv",
    )(mem, g.reshape(1, d), w_xkv)


_R_AQ, _R_AK, _R_AV = 0, 512, 1024
_R_BQ, _R_BK, _R_BV = 1536, 1792, 2048
_R_ALPHA, _R_BR, _R_GATE, _R_END = 2560, 2576, 3088, 5136
_W_PIECE = 512


def _inproj_kernel(x_ref, g_ref, wt_ref, wal2_ref, bal_ref,
                   aq_ref, ak_ref, av_ref, bq_ref, bk_ref, bv_ref, lga_ref, br_ref, gate_ref,
                   wt_bf, wal2_bf):
    @pl.when(pl.program_id(0) == 0)
    def _():
        for c in range(0, _R_END, _W_PIECE):
            n = min(_W_PIECE, _R_END - c)
            wt_bf[c:c + n, :] = wt_ref[0, c:c + n, :].astype(BF16)
        wal2_bf[...] = jnp.concatenate(
            [wal2_ref[0].astype(BF16), jnp.zeros((LANES - B_GATE_RANK, B_KEY_WIDTH), BF16)], axis=0)

    h = _rms(x_ref[...], g_ref[...]).astype(BF16)

    def mm(lo, hi):
        return _dot_nt(h, wt_bf[lo:hi, :])

    aq_ref[...] = (mm(_R_AQ, _R_AK) * (A_HEAD_DIM ** -0.5 * LOG2E)).astype(BF16)
    ak_ref[...] = mm(_R_AK, _R_AV).astype(BF16)
    av_ref[...] = mm(_R_AV, _R_BQ).astype(BF16)
    bq_ref[...] = (mm(_R_BQ, _R_BK) * (B_KEY_DIM ** -0.5)).astype(BF16)
    bk_ref[...] = mm(_R_BK, _R_BV).astype(BF16)
    bv_ref[...] = mm(_R_BV, _R_ALPHA).astype(BF16)
    r = mm(_R_BR, _R_GATE)
    br_ref[...] = (r * jax.nn.sigmoid(r)).astype(BF16)
    for c in range(_R_GATE, _R_END, 512):
        gate_ref[:, c - _R_GATE:c - _R_GATE + 512] = jax.nn.sigmoid(mm(c, c + 512)).astype(BF16)
    z = _dot(mm(_R_ALPHA, _R_ALPHA + LANES).astype(BF16), wal2_bf[...]) + bal_ref[...]
    lga_ref[...] = (jnp.minimum(z, 0.0) - jnp.log(1.0 + jnp.exp(-jnp.abs(z)))) * (1.0 / B_GATE_TAU)


def _inproj(x, g, w_in, w_al2, b_al, layer):
    t, d = x.shape
    assert w_in.shape[2] == _R_END
    row = lambda w: pl.BlockSpec((INPROJ_ROWS, w), lambda i: (i, 0))
    full = lambda a: pl.BlockSpec(a.shape, lambda i: (0,) * a.ndim)
    sds = lambda w, dt: jax.ShapeDtypeStruct((t, w), dt)
    widths = [(512, BF16), (512, BF16), (512, BF16), (256, BF16), (256, BF16), (512, BF16),
              (256, F32), (512, BF16), (2048, BF16)]
    return pl.pallas_call(
        _inproj_kernel,
        grid=(t // INPROJ_ROWS,),
        in_specs=[row(d), full(g),
                  pl.BlockSpec((1, _R_END, d), lambda i: (layer, 0, 0), pipeline_mode=pl.Buffered(1)),
                  pl.BlockSpec((1,) + w_al2.shape[1:], lambda i: (layer, 0, 0)),
                  full(b_al)],
        out_specs=[row(w) for w, _ in widths],
        out_shape=[sds(w, dt) for w, dt in widths],
        scratch_shapes=[pltpu.VMEM((_R_END, d), BF16), pltpu.VMEM((LANES, B_KEY_WIDTH), BF16)],
        compiler_params=_params("arbitrary"),
        name="inproj",
    )(x, g, jnp.swapaxes(w_in, 1, 2), w_al2, b_al)


def _band_kernel(q_ref, k0_ref, k1_ref, k2_ref, v0_ref, v1_ref, v2_ref, u_ref, o_ref, bias_ref):
    i = pl.program_id(1)
    lane = lax.broadcasted_iota(I32, (1, LANES), 1)
    low = lane < A_HEAD_DIM
    ones = jnp.ones((BAND_KEYS, LANES), BF16)

    @pl.when((pl.program_id(0) == 0) & (i == 0))
    def _():
        cq = lax.broadcasted_iota(I32, (ROW_TILE, BAND_KEYS), 0) >> LOG_CHUNK
        ck = lax.broadcasted_iota(I32, (ROW_TILE, BAND_KEYS), 1) >> LOG_CHUNK
        valid = (ck >= cq) & (ck <= cq + A_LEFT_CHUNKS)
        for h in range(A_HEADS):
            rows = jnp.broadcast_to(u_ref[h:h + 1, :], (ROW_TILE, BIAS_PERIOD))
            rows = pltpu.roll(rows, BIAS_PERIOD - (ROW_TILE - 1), 1, stride=1, stride_axis=0)
            bias_ref[h // 2, (h % 2) * ROW_TILE:(h % 2 + 1) * ROW_TILE, :] = jnp.where(
                valid, rows[:, :BAND_KEYS], NEG)

    def attend(pen):
        for p in range(A_HEADS // 2):
            sl = slice(p * LANES, (p + 1) * LANES)
            qp = q_ref[:, sl]
            zero = jnp.zeros_like(qp)
            q2 = jnp.concatenate([jnp.where(low, qp, zero), jnp.where(low, zero, qp)], axis=0)
            kp = jnp.concatenate([k0_ref[:, sl], k1_ref[:, sl], k2_ref[:, sl]], axis=0)
            vp = jnp.concatenate([v0_ref[:, sl], v1_ref[:, sl], v2_ref[:, sl]], axis=0)
            s = _dot_nt(q2, kp) + bias_ref[p]
            if pen is not None:
                s = s + pen
            pe = jnp.exp2(s - jnp.max(s, axis=-1, keepdims=True)).astype(BF16)
            o2 = _dot(pe, jnp.concatenate([vp, ones], axis=1))
            o = o2[:, :LANES] * (1.0 / o2[:, LANES:])
            o_ref[:, sl] = jnp.where(low, o[:ROW_TILE], o[ROW_TILE:]).astype(BF16)

    @pl.when(i >= BAND_TILES - 1)
    def _():
        attend(None)

    @pl.when(i < BAND_TILES - 1)
    def _():
        col = lax.broadcasted_iota(I32, (1, BAND_KEYS), 1)
        attend(jnp.where(col < (BAND_TILES - 1 - i) * ROW_TILE, NEG, 0.0).astype(F32))


def _band_bias_vector(rel_table):
    h = rel_table.shape[0]
    tab = rel_table.astype(F32) * LOG2E
    shift = A_LEFT_CHUNKS * CHUNK + ROW_TILE - 1
    n_far = shift - A_MAX_REL + 1
    span = ROW_TILE + BAND_KEYS - 1
    assert span - 1 - shift <= A_MAX_REL and span <= BIAS_PERIOD
    u = jnp.concatenate([jnp.broadcast_to(tab[:, 2 * A_MAX_REL:], (h, n_far)),
                         tab[:, 2 * A_MAX_REL - 1:2 * A_MAX_REL - 1 - (span - n_far):-1]], axis=1)
    return jnp.pad(u, ((0, 0), (0, BIAS_PERIOD - span)))


def _band_attention(q, k, v, u, batch):
    t, w = q.shape
    nb = t // batch // ROW_TILE
    cur = lambda b, i: (b * nb + i, 0)
    back = lambda n: (lambda b, i: (b * nb + jnp.maximum(i - n, 0), 0))
    blk = lambda f: pl.BlockSpec((ROW_TILE, w), f)
    return pl.pallas_call(
        _band_kernel,
        grid=(batch, nb),
        in_specs=[blk(cur), blk(back(2)), blk(back(1)), blk(cur), blk(back(2)), blk(back(1)), blk(cur),
                  pl.BlockSpec(u.shape, lambda b, i: (0, 0))],
        out_specs=blk(cur),
        out_shape=jax.ShapeDtypeStruct((t, w), BF16),
        scratch_shapes=[pltpu.VMEM((A_HEADS // 2, 2 * ROW_TILE, BAND_KEYS), F32)],
        compiler_params=_params("arbitrary", "arbitrary"),
        name="band_attn",
    )(q, k, k, k, v, v, v, u)


def _gla_constants():
    c = CHUNK
    t = np.arange(c)[:, None]
    r = np.arange(c)[None, :]
    mats = [(r <= t), (r > t)]
    lvl = np.full((c, c), -1, np.int32)
    lvl[np.arange(c), np.arange(c)] = N_LEVELS
    for l in range(N_LEVELS):
        m = (c // 2) >> l
        mid = (t // (2 * m)) * (2 * m) + m
        upper = t >= mid
        mats.append(np.where(upper, (r >= mid) & (r <= t), (r > t) & (r < mid)))
        s = r
        same = (s // (2 * m)) == (t // (2 * m))
        lvl[np.asarray(same & upper & (s < mid))] = l
    eye = np.eye(CHUNKS_PER_TILE)
    mexp = np.concatenate([np.kron(eye, m) for m in mats], axis=0).astype(np.float32)
    lvl = np.tile(lvl, (1, B_HEADS))
    return jnp.asarray(mexp, BF16), jnp.asarray(lvl, I32)


def _gla_kernel(q_ref, k_ref, v_ref, g_ref, r_ref, gn_ref, mexp_ref, lvl_ref, o_ref, s_ref):
    @pl.when(pl.program_id(1) == 0)
    def _():
        s_ref[...] = jnp.zeros_like(s_ref)

    kw = B_KEY_WIDTH
    ri = lax.broadcasted_iota(I32, (kw, kw), 0) >> LOG_CHUNK
    ci = lax.broadcasted_iota(I32, (kw, kw), 1) >> LOG_CHUNK
    bd = ri == ci
    head_ind = jnp.where(bd, 1.0, 0.0).astype(BF16)
    ri2 = lax.broadcasted_iota(I32, (kw, 2 * kw), 0) >> LOG_CHUNK
    ci2 = (lax.broadcasted_iota(I32, (kw, 2 * kw), 1) & (kw - 1)) >> LOG_CHUNK
    bd2 = ri2 == ci2
    lvl = lvl_ref[...]
    row8 = lax.broadcasted_iota(I32, (16, kw), 0)
    ones = jnp.ones((16, LANES), BF16)
    zero_b = jnp.zeros((kw, kw), BF16)
    chunks = [slice(c * CHUNK, (c + 1) * CHUNK) for c in range(CHUNKS_PER_TILE)]

    def head_blocks(x):
        return jnp.where(bd, jnp.concatenate([x] * B_HEADS, axis=0), zero_b)

    q = q_ref[...].astype(F32)
    k = k_ref[...].astype(F32)
    g = g_ref[...]
    gb = g.astype(BF16)
    half = EXP_ROWS * CHUNKS_PER_TILE // 2
    w = jnp.exp(jnp.concatenate([_dot(mexp_ref[:half, :], gb), _dot(mexp_ref[half:, :], gb)], axis=0))
    w_cum = w[0:ROW_TILE]
    qt = (q * w_cum).astype(BF16)
    kb = (k * w[ROW_TILE:2 * ROW_TILE]).astype(BF16)
    qk = (q * k).astype(BF16)

    attn = [jnp.zeros((CHUNK, kw), F32) for _ in chunks]
    for l in range(N_LEVELS):
        wl = w[(2 + l) * ROW_TILE:(3 + l) * ROW_TILE]
        qh = (q * wl).astype(BF16)
        kh = (k * wl).astype(BF16)
        for c, rows in enumerate(chunks):
            attn[c] = jnp.where(lvl == l, _dot_nt(qh[rows], head_blocks(kh[rows])), attn[c])

    vstacks, kvs, dcols = [], [], []
    for c, rows in enumerate(chunks):
        attn[c] = jnp.where(lvl == N_LEVELS, _dot(qk[rows], head_ind), attn[c])
        v = v_ref[rows, :]
        vstack = jnp.concatenate([v[:, j * LANES:(j + 1) * LANES] for j in range(B_HEADS)], axis=0)
        vstacks.append(vstack)
        kvs.append(_dot_tn(head_blocks(kb[rows]), vstack))
        d = jnp.exp(jnp.sum(g[rows], axis=0, keepdims=True))
        d1 = d.astype(BF16).astype(F32)
        dp = jnp.where(row8 == 0, d1, jnp.where(row8 == 1, d - d1, 0.0)).astype(BF16)
        dcols.append(_dot_tn(dp, ones))

    s = s_ref[...]
    for c, rows in enumerate(chunks):
        lhs = jnp.concatenate([attn[c].astype(BF16), qt[rows]], axis=1)
        lhs = jnp.where(bd2, jnp.concatenate([lhs] * B_HEADS, axis=0), jnp.zeros((kw, 2 * kw), BF16))
        rhs = jnp.concatenate([vstacks[c], s.astype(BF16)], axis=0)
        o = _dot(lhs, rhs)
        s = dcols[c] * s + kvs[c]
        for j in range(B_HEADS):
            oj = o[j * CHUNK:(j + 1) * CHUNK]
            sl = slice(j * LANES, (j + 1) * LANES)
            y = oj * lax.rsqrt(jnp.mean(oj * oj, axis=-1, keepdims=True) + EPS) * gn_ref[...]
            o_ref[rows, sl] = (y * r_ref[rows, sl].astype(F32)).astype(BF16)
    s_ref[...] = s


def _gla(q, k, v, g, r, gn, batch):
    t = q.shape[0]
    nb = t // batch // ROW_TILE
    mexp, lvl = _gla_constants()
    cur = lambda b, i: (b * nb + i, 0)
    blk = lambda w: pl.BlockSpec((ROW_TILE, w), cur)
    full = lambda a: pl.BlockSpec(a.shape, lambda b, i: (0,) * a.ndim)
    return pl.pallas_call(
        _gla_kernel,
        grid=(batch, nb),
        in_specs=[blk(B_KEY_WIDTH), blk(B_KEY_WIDTH), blk(B_VAL_WIDTH), blk(B_KEY_WIDTH), blk(B_VAL_WIDTH),
                  full(gn), full(mexp), full(lvl)],
        out_specs=blk(B_VAL_WIDTH),
        out_shape=jax.ShapeDtypeStruct((t, B_VAL_WIDTH), BF16),
        scratch_shapes=[pltpu.VMEM((B_KEY_WIDTH, B_VAL_DIM), F32)],
        compiler_params=_params("arbitrary", "arbitrary"),
        name="gla",
    )(q, k, v, g, r, gn, mexp, lvl)


def _token_kernel(x_ref, oa_ref, ob_ref, gate_ref, wb0_ref, wb1_ref, wmix_ref, gx_ref, wq_ref,
                  km_ref, vm_ref, wo_ref, gf_ref, wr_ref, br_ref, ltri_ref, utri_ref,
                  x2_ref, hs_ref, route_ref, cnt_ref):
    ma = _dot(oa_ref[...], wb0_ref[...])
    mb = _dot(ob_ref[...], wb1_ref[...])
    merged = (gate_ref[:, :D_MODEL].astype(F32) * ma + gate_ref[:, D_MODEL:].astype(F32) * mb).astype(BF16)
    x1 = x_ref[...] + _dot(merged, wmix_ref[...])

    h2 = _rms(x1, gx_ref[...]).astype(BF16)
    qx = (_dot(h2, wq_ref[...]) * (X_HEAD_DIM ** -0.5)).astype(BF16)
    heads = []
    for h in range(X_HEADS):
        sl = slice(h * X_HEAD_DIM, (h + 1) * X_HEAD_DIM)
        s = _dot_nt(qx[:, sl], km_ref[0, :, sl])
        m = jnp.max(s, axis=-1, keepdims=True)
        pe = jnp.exp(s - m)
        l = jnp.sum(pe, axis=-1, keepdims=True)
        heads.append((_dot(pe.astype(BF16), vm_ref[0, :, sl]) * (1.0 / l)).astype(BF16))
    x2 = x1 + _dot(jnp.concatenate(heads, axis=1), wo_ref[...])
    x2_ref[...] = x2

    h3 = _rms(x2, gf_ref[...])

    h3_hi = h3.astype(BF16)
    h3_lo = (h3 - h3_hi.astype(F32)).astype(BF16)
    hw = _dot(h3_hi, wr_ref[...])
    logits = hw[:, :LANES] + hw[:, LANES:] + _dot(h3_lo, wr_ref[:, :LANES]) + br_ref[...]
    for h in range(TOKEN_TILES_PER_STEP):
        rows = slice(h * ROW_TILE, (h + 1) * ROW_TILE)
        chunks = pl.ds(h * LOCAL_CHUNKS, LOCAL_CHUNKS)
        _route_and_sort(logits[rows], h3_hi[rows], ltri_ref, utri_ref, hs_ref.at[chunks],
                        route_ref.at[pl.ds(h * ROW_TILE, ROW_TILE)], cnt_ref.at[h])


def _route_and_sort(logits, h3_hi, ltri_ref, utri_ref, hs_ref, route_ref, cnt_ref):
    lane = lax.broadcasted_iota(I32, logits.shape, 1).astype(F32)
    big = jnp.float32(LANES)
    gl = jnp.where(lane < N_GROUPS, logits, NEG)
    gmax = jnp.max(gl, axis=-1, keepdims=True)
    gidx = jnp.min(jnp.where(gl == gmax, lane, big), axis=-1, keepdims=True)
    g_w = 1.0 / jnp.sum(jnp.exp(gl - gmax), axis=-1, keepdims=True)
    lo = N_GROUPS + EXPERTS_PER_GROUP * gidx
    el = jnp.where((lane >= lo) & (lane < lo + EXPERTS_PER_GROUP), logits, NEG)
    v1 = jnp.max(el, axis=-1, keepdims=True)
    i1 = jnp.min(jnp.where(el == v1, lane, big), axis=-1, keepdims=True)
    el2 = jnp.where(lane == i1, NEG, el)
    v2 = jnp.max(el2, axis=-1, keepdims=True)
    i2 = jnp.min(jnp.where(el2 == v2, lane, big), axis=-1, keepdims=True)
    e21 = jnp.exp(v2 - v1)
    w1 = g_w / (1.0 + e21)
    w2 = w1 * e21
    oh0 = jnp.where(lane == i1 - N_GROUPS, 1.0, 0.0)
    oh1 = jnp.where(lane == i2 - N_GROUPS, 1.0, 0.0)
    oh = oh0 + oh1
    nch = jnp.floor((jnp.sum(oh, axis=0, keepdims=True) + (CHUNK_ROWS - 1)) * (1.0 / CHUNK_ROWS))
    nch8 = jnp.broadcast_to(nch, (8, LANES))
    start = _dot(nch8.astype(BF16), utri_ref[...])[0:1] * CHUNK_ROWS
    rank = _dot(ltri_ref[...], oh.astype(BF16))
    row = start + rank
    pos0 = jnp.sum(row * oh0, axis=-1, keepdims=True)
    pos1 = jnp.sum(row * oh1, axis=-1, keepdims=True)
    route = jnp.where(lane == 0, pos0, jnp.where(lane == 1, pos1, 0.0))
    route_t = jnp.transpose(route)
    r = lax.broadcasted_iota(I32, (LOCAL_ROWS, ROW_TILE), 0).astype(F32)
    p0 = jnp.where(r == route_t[0:1, :], 1.0, 0.0).astype(BF16)
    p1 = jnp.where(r == route_t[1:2, :], 1.0, 0.0).astype(BF16)

    def gate_cols(w):
        hi = w.astype(BF16).astype(F32)
        return jnp.where(lane == 0, hi, jnp.where(lane == 1, w - hi, 0.0)).astype(BF16)

    sorted_rows = jnp.concatenate([_dot(p0 + p1, h3_hi), _dot(p0, gate_cols(w1)) + _dot(p1, gate_cols(w2))],
                                  axis=1)
    hs_ref[...] = sorted_rows.astype(BF16).reshape(hs_ref.shape)
    route_ref[...] = route
    cnt_ref[...] = nch8


def _token(x, oa, ob, gates, wb0, wb1, wmix, gx, wq, km, vm, wo, gf, wr, br, batch):
    t, d = x.shape
    n = TOKEN_TILES_PER_STEP
    nb = t // batch // (n * ROW_TILE)
    nt = t // ROW_TILE
    ltri = jnp.asarray(np.tril(np.ones((ROW_TILE, ROW_TILE), np.float32), -1), BF16)
    utri = jnp.asarray(np.triu(np.ones((LANES, LANES), np.float32), 1), BF16)
    cur = lambda b, i: (b * nb + i, 0)
    cur3 = lambda b, i: (b * nb + i, 0, 0)
    blk = lambda w: pl.BlockSpec((n * ROW_TILE, w), cur)
    full = lambda a: pl.BlockSpec(a.shape, lambda b, i: (0,) * a.ndim)
    mem = pl.BlockSpec((1,) + km.shape[1:], lambda b, i: (b, 0, 0))
    return pl.pallas_call(
        _token_kernel,
        grid=(batch, nb),
        in_specs=[blk(d), blk(A_WIDTH), blk(B_VAL_WIDTH), blk(2 * d), full(wb0), full(wb1), full(wmix),
                  full(gx), full(wq), mem, mem, full(wo), full(gf), full(wr), full(br), full(ltri), full(utri)],
        out_specs=[blk(d), pl.BlockSpec((n * LOCAL_CHUNKS, CHUNK_ROWS, SORT_WIDTH), cur3),
                   blk(LANES), pl.BlockSpec((n, 8, LANES), cur3)],
        out_shape=[jax.ShapeDtypeStruct((t, d), F32),
                   jax.ShapeDtypeStruct((nt * LOCAL_CHUNKS, CHUNK_ROWS, SORT_WIDTH), BF16),
                   jax.ShapeDtypeStruct((t, LANES), F32),
                   jax.ShapeDtypeStruct((nt, 8, LANES), F32)],
        compiler_params=_params("arbitrary", "arbitrary"),
        name="token",
    )(x, oa, ob, gates, wb0, wb1, wmix, gx, wq, km, vm, wo, gf, wr, br, ltri, utri)


def _expert_kernel(te_ref, nu_ref, nv_ref, ch_ref, hs_hbm, wg_ref, wu_ref, wd_ref, ys_hbm,
                   xbuf, ybuf, wg_bf, wu_bf, wd_bf, gsem, ssem):
    i = pl.program_id(0)
    n_used = nu_ref[0]
    slot = lax.rem(i, 2)

    def for_chunks(tile, fn):
        nv = nv_ref[tile]

        @pl.when(nv == TILE_CHUNKS)
        def _():
            for c in range(TILE_CHUNKS):
                fn(c)

        @pl.when(nv != TILE_CHUNKS)
        def _():
            def body(c, carry):
                fn(c)
                return carry

            lax.fori_loop(0, nv, body, 0)

    def gather(tile, s, start):
        def one(c):
            cp = pltpu.make_async_copy(hs_hbm.at[ch_ref[tile * TILE_CHUNKS + c]], xbuf.at[s, c], gsem.at[s])
            cp.start() if start else cp.wait()

        for_chunks(tile, one)

    def scatter(tile, s, start):
        def one(c):
            cp = pltpu.make_async_copy(ybuf.at[s, c], ys_hbm.at[ch_ref[tile * TILE_CHUNKS + c]], ssem.at[s])
            cp.start() if start else cp.wait()

        for_chunks(tile, one)

    @pl.when(i == 0)
    def _():
        xbuf[...] = jnp.zeros_like(xbuf)
        gather(0, 0, True)

    @pl.when(i < n_used)
    def _():
        @pl.when(i + 1 < n_used)
        def _():
            gather(i + 1, 1 - slot, True)

        gather(i, slot, False)

        @pl.when(i >= 2)
        def _():
            scatter(i - 2, slot, False)

        @pl.when((i == 0) | (te_ref[i] != te_ref[jnp.maximum(i - 1, 0)]))
        def _():
            wg_bf[...] = wg_ref[0].astype(BF16)
            wu_bf[...] = wu_ref[0].astype(BF16)
            wd_bf[...] = wd_ref[0].astype(BF16)

        xg = xbuf[slot].reshape(EXPERT_ROWS, SORT_WIDTH)
        x = xg[:, :D_MODEL]
        hg = _dot(x, wg_bf[...])
        hu = _dot(x, wu_bf[...])
        hid = (hg * jax.nn.sigmoid(hg) * hu).astype(BF16)
        g = xg[:, D_MODEL:].astype(F32)
        y = ((g[:, 0:1] + g[:, 1:2]) * _dot(hid, wd_bf[...])).astype(BF16)
        y = jnp.concatenate([y, jnp.zeros((EXPERT_ROWS, LANES), BF16)], axis=1)
        ybuf[slot] = y.reshape(TILE_CHUNKS, CHUNK_ROWS, SORT_WIDTH)
        scatter(i, slot, True)

        @pl.when(i == n_used - 1)
        def _():
            scatter(i, slot, False)

            @pl.when(i >= 1)
            def _():
                scatter(i - 1, 1 - slot, False)


def _experts(hs, tile_expert, n_used, n_valid, chunks, wg, wu, wd, layer):
    n_tiles = tile_expert.shape[0]
    last = lambda i, te, nu, nv, ch: jnp.minimum(i, nu[0] - 1)
    wmap = lambda i, te, nu, nv, ch: (layer * N_EXPERTS + te[last(i, te, nu, nv, ch)], 0, 0)
    anyspace = pl.BlockSpec(memory_space=pl.ANY)
    grid_spec = pltpu.PrefetchScalarGridSpec(
        num_scalar_prefetch=4,
        grid=(n_tiles,),
        in_specs=[anyspace,
                  pl.BlockSpec((1, D_MODEL, EXPERT_FF), wmap),
                  pl.BlockSpec((1, D_MODEL, EXPERT_FF), wmap),
                  pl.BlockSpec((1, EXPERT_FF, D_MODEL), wmap)],
        out_specs=anyspace,
        scratch_shapes=[pltpu.VMEM((2, TILE_CHUNKS, CHUNK_ROWS, SORT_WIDTH), BF16),
                        pltpu.VMEM((2, TILE_CHUNKS, CHUNK_ROWS, SORT_WIDTH), BF16),
                        pltpu.VMEM((D_MODEL, EXPERT_FF), BF16), pltpu.VMEM((D_MODEL, EXPERT_FF), BF16),
                        pltpu.VMEM((EXPERT_FF, D_MODEL), BF16),
                        pltpu.SemaphoreType.DMA((2,)), pltpu.SemaphoreType.DMA((2,))],
    )
    return pl.pallas_call(
        _expert_kernel,
        grid_spec=grid_spec,
        out_shape=jax.ShapeDtypeStruct(hs.shape, BF16),
        input_output_aliases={4: 0},
        compiler_params=_params("arbitrary"),
        name="experts",
    )(tile_expert, n_used, n_valid, chunks, hs, wg, wu, wd)


def _combine_kernel(final_norm, x_ref, route_ref, ys_ref, gfin_ref, o_ref):
    r = lax.broadcasted_iota(I32, (ROW_TILE, LOCAL_ROWS), 1).astype(F32)
    sel = jnp.where(r == route_ref[:, 0:1], 1.0, jnp.where(r == route_ref[:, 1:2], 1.0, 0.0)).astype(BF16)
    x3 = x_ref[...] + _dot(sel, ys_ref[...].reshape(LOCAL_ROWS, D_MODEL))
    if final_norm:
        x3 = _rms(x3, gfin_ref[...])
    o_ref[...] = x3


def _combine(x2, route, ys, gfin, final_norm):
    t, d = x2.shape
    nt = t // ROW_TILE
    return pl.pallas_call(
        functools.partial(_combine_kernel, final_norm),
        grid=(nt,),
        in_specs=[pl.BlockSpec((ROW_TILE, d), lambda i: (i, 0)),
                  pl.BlockSpec((ROW_TILE, LANES), lambda i: (i, 0)),
                  pl.BlockSpec((LOCAL_CHUNKS, CHUNK_ROWS, d), lambda i: (i, 0, 0)),
                  pl.BlockSpec((1, d), lambda i: (0, 0))],
        out_specs=pl.BlockSpec((ROW_TILE, d), lambda i: (i, 0)),
        out_shape=jax.ShapeDtypeStruct((t, d), F32),
        compiler_params=_params("arbitrary"),
        name="combine",
    )(x2, route, ys, gfin)


def _chunk_plan(nch, n_tiles):
    nt = nch.shape[0]
    local_start = jnp.cumsum(nch, axis=1) - nch
    cum = jnp.cumsum(nch, axis=0)
    total = cum[-1]
    tiles = (total + TILE_CHUNKS - 1) // TILE_CHUNKS
    tile_end = jnp.cumsum(tiles)
    n_used = tile_end[-1:]
    tile_ids = jnp.arange(n_tiles, dtype=I32)
    tile_expert = jnp.minimum(jnp.sum((tile_end[None, :] <= tile_ids[:, None]).astype(I32), axis=1),
                              N_EXPERTS - 1)
    sel = (tile_expert[:, None] == jnp.arange(N_EXPERTS, dtype=I32)[None, :]).astype(I32)
    pick = lambda table: jnp.sum(sel[:, :, None] * table.T[None, :, :], axis=1)
    first_tile = jnp.sum(sel * (tile_end - tiles)[None, :], axis=1)
    slot = (tile_ids - first_tile)[:, None] * TILE_CHUNKS + jnp.arange(TILE_CHUNKS, dtype=I32)[None, :]
    valid = (slot < jnp.sum(sel * total[None, :], axis=1)[:, None]) & (tile_ids < n_used)[:, None]
    src_tile = jnp.sum((pick(cum)[:, None, :] <= slot[:, :, None]).astype(I32), axis=2)
    src_tile = jnp.minimum(src_tile, nt - 1)
    at = (src_tile[:, :, None] == jnp.arange(nt, dtype=I32)[None, None, :]).astype(I32)
    before = jnp.sum(at * pick(cum - nch)[:, None, :], axis=2)
    start = jnp.sum(at * pick(local_start)[:, None, :], axis=2)
    chunk = jnp.where(valid, src_tile * LOCAL_CHUNKS + start + slot - before, 0)
    return tile_expert, n_used, jnp.sum(valid.astype(I32), axis=1), chunk.reshape(-1)


def kernel(x, mem, norm_mix_g, w_in, rel_bias, gla_w_alpha, gla_b_alpha, gla_norm_g, w_branch, w_mix_out, norm_x_g, mem_norm_g, w_xq, w_xkv, w_xo, norm_ffn_g, w_group_router, b_group_router, w_expert_router, b_expert_router, w_exp_gate, w_exp_up, w_exp_down, final_norm_g):
    batch, seq, d = x.shape
    depth = w_in.shape[0]
    t = batch * seq
    assert d == D_MODEL and seq % (TOKEN_TILES_PER_STEP * ROW_TILE) == 0
    nt = t // ROW_TILE
    n_tiles = nt * LOCAL_CHUNKS // TILE_CHUNKS + N_EXPERTS

    xf = x.reshape(t, d)
    km_all, vm_all = _memkv(mem, mem_norm_g, w_xkv.astype(BF16))
    row = lambda a: a.reshape(1, -1).astype(F32)

    for l in range(depth):
        aq, ak, av, bq, bk, bv, lga, br, gates = _inproj(
            xf, row(norm_mix_g[l]), w_in, gla_w_alpha, row(gla_b_alpha[l]), l)

        oa = _band_attention(aq, ak, av, _band_bias_vector(rel_bias[l]), batch)
        ob = _gla(bq, bk, bv, lga, br, row(gla_norm_g[l]), batch)

        wr = jnp.pad(jnp.concatenate([w_group_router[l], w_expert_router[l]], axis=1).astype(F32),
                     ((0, 0), (0, LANES - N_GROUPS - N_EXPERTS)))
        wr_hi = wr.astype(BF16)
        wr = jnp.concatenate([wr_hi, (wr - wr_hi.astype(F32)).astype(BF16)], axis=1)
        brt = jnp.pad(jnp.concatenate([b_group_router[l], b_expert_router[l]]).astype(F32),
                      (0, LANES - N_GROUPS - N_EXPERTS)).reshape(1, LANES)
        x2, hs, route, cnt = _token(
            xf, oa, ob, gates, w_branch[l, 0].astype(BF16), w_branch[l, 1].astype(BF16),
            w_mix_out[l].astype(BF16), row(norm_x_g[l]), w_xq[l].astype(BF16), km_all[l], vm_all[l],
            w_xo[l].astype(BF16), row(norm_ffn_g[l]), wr, brt, batch)

        plan = _chunk_plan(cnt[:, 0, :N_EXPERTS].astype(I32), n_tiles)
        e3 = lambda w: w.reshape((depth * N_EXPERTS,) + w.shape[3:])
        ys = _experts(hs, *plan, e3(w_exp_gate), e3(w_exp_up), e3(w_exp_down), l)
        xf = _combine(x2, route, ys, row(final_norm_g), l == depth - 1)

    return xf.reshape(batch, seq, d)
```

```python
import functools

import numpy as np
import jax
import jax.numpy as jnp
from jax import lax
from jax.experimental import pallas as pl
from jax.experimental.pallas import tpu as pltpu

F32 = jnp.float32
BF16 = jnp.bfloat16
I32 = jnp.int32

D_MODEL = 1024
CHUNK = 64
EPS = 1e-6
A_HEADS = 8
A_HEAD_DIM = 64
A_WIDTH = 512
A_LEFT_CHUNKS = 8
A_MAX_REL = 256
B_HEADS = 4
B_KEY_DIM = 64
B_VAL_DIM = 128
B_KEY_WIDTH = 256
B_VAL_WIDTH = 512
B_GATE_RANK = 16
B_GATE_TAU = 16.0
X_HEADS = 4
X_HEAD_DIM = 256
N_GROUPS = 4
EXPERTS_PER_GROUP = 8
N_EXPERTS = N_GROUPS * EXPERTS_PER_GROUP
EXPERT_FF = 256

LANES = 128
ROW_TILE = 256
CHUNKS_PER_TILE = ROW_TILE // CHUNK
BAND_TILES = A_LEFT_CHUNKS // CHUNKS_PER_TILE + 1
BAND_KEYS = BAND_TILES * ROW_TILE
BIAS_PERIOD = 1024
LOG_CHUNK = 6
N_LEVELS = LOG_CHUNK
EXP_ROWS = (2 + N_LEVELS) * CHUNK
CHUNK_ROWS = 16
TOKEN_TILES_PER_STEP = 2
INPROJ_ROWS = 512
EXPERT_ROWS = 512
TILE_CHUNKS = EXPERT_ROWS // CHUNK_ROWS
LOCAL_CHUNKS = 2 * ROW_TILE // CHUNK_ROWS + N_EXPERTS
LOCAL_ROWS = LOCAL_CHUNKS * CHUNK_ROWS
SORT_WIDTH = D_MODEL + LANES
NEG = -1e30
LOG2E = 1.4426950408889634
VMEM_LIMIT = 56 * 1024 * 1024


def _params(*sem):
    return pltpu.CompilerParams(dimension_semantics=sem, vmem_limit_bytes=VMEM_LIMIT)


def _rms(x, g):
    return x * lax.rsqrt(jnp.mean(x * x, axis=-1, keepdims=True) + EPS) * g


def _dot(a, b):
    return jnp.dot(a, b, preferred_element_type=F32)


def _dot_nt(a, b):
    return lax.dot_general(a, b, (((1,), (1,)), ((), ())), preferred_element_type=F32)


def _dot_tn(a, b):
    return lax.dot_general(a, b, (((0,), (0,)), ((), ())), preferred_element_type=F32)


def _memkv_kernel(mem_ref, g_ref, w_ref, k_ref, v_ref):
    mn = _rms(mem_ref[0], g_ref[...]).astype(BF16)
    kv = _dot(mn, w_ref[0])
    k_ref[0, 0] = kv[:, :D_MODEL].astype(BF16)
    v_ref[0, 0] = kv[:, D_MODEL:].astype(BF16)


def _memkv(mem, g, w_xkv):
    depth = w_xkv.shape[0]
    b, m, d = mem.shape
    out = jax.ShapeDtypeStruct((depth, b, m, d), BF16)
    return pl.pallas_call(
        _memkv_kernel,
        grid=(depth, b),
        in_specs=[pl.BlockSpec((1, m, d), lambda l, i: (i, 0, 0)),
                  pl.BlockSpec((1, d), lambda l, i: (0, 0)),
                  pl.BlockSpec((1, d, 2 * d), lambda l, i: (l, 0, 0))],
        out_specs=[pl.BlockSpec((1, 1, m, d), lambda l, i: (l, i, 0, 0)),
                   pl.BlockSpec((1, 1, m, d), lambda l, i: (l, i, 0, 0))],
        out_shape=[out, out],
        compiler_params=_params("arbitrary", "arbitrary"),
        name="memkv",
    )(mem, g.reshape(1, d), w_xkv)


_R_AQ, _R_AK, _R_AV = 0, 512, 1024
_R_BQ, _R_BK, _R_BV = 1536, 1792, 2048
_R_ALPHA, _R_BR, _R_GATE, _R_END = 2560, 2576, 3088, 5136
_W_PIECE = 512


def _inproj_kernel(layer, fused, *refs):
    if fused:
        x_ref, route_ref, ys_ref, g_ref, wt_hbm, wal2_ref, bal_ref = refs[:7]
        refs = refs[7:]
        x3_ref, refs = refs[0], refs[1:]
    else:
        x_ref, g_ref, wt_hbm, wal2_ref, bal_ref = refs[:5]
        refs = refs[5:]
    (aq_ref, ak_ref, av_ref, bq_ref, bk_ref, bv_ref, lga_ref, br_ref, gate_ref,
     wt_bf, wal2_bf, stage, sem) = refs

    @pl.when(pl.program_id(0) == 0)
    def _():
        pieces = [(c, min(_W_PIECE, _R_END - c)) for c in range(0, _R_END, _W_PIECE)]

        def piece_copy(p):
            c, n = pieces[p]
            return pltpu.make_async_copy(wt_hbm.at[layer, pl.ds(c, n), :], stage.at[p % 2, pl.ds(0, n), :],
                                         sem.at[p % 2])

        piece_copy(0).start()
        for p, (c, n) in enumerate(pieces):
            if p + 1 < len(pieces):
                piece_copy(p + 1).start()
            piece_copy(p).wait()
            wt_bf[c:c + n, :] = stage[p % 2, 0:n, :].astype(BF16)
        wal2_bf[...] = jnp.concatenate(
            [wal2_ref[0].astype(BF16), jnp.zeros((LANES - B_GATE_RANK, B_KEY_WIDTH), BF16)], axis=0)

    if fused:
        r = lax.broadcasted_iota(I32, (ROW_TILE, LOCAL_ROWS), 1).astype(F32)
        tiles = []
        for t in range(INPROJ_ROWS // ROW_TILE):
            rows = slice(t * ROW_TILE, (t + 1) * ROW_TILE)
            pos = route_ref[rows, :]
            sel = jnp.where(r == pos[:, 0:1], 1.0, jnp.where(r == pos[:, 1:2], 1.0, 0.0)).astype(BF16)
            ys = ys_ref[t * LOCAL_CHUNKS:(t + 1) * LOCAL_CHUNKS].reshape(LOCAL_ROWS, D_MODEL)
            tiles.append(x_ref[rows, :] + _dot(sel, ys))
        x = jnp.concatenate(tiles, axis=0)
        x3_ref[...] = x
    else:
        x = x_ref[...]
    h = _rms(x, g_ref[...]).astype(BF16)

    def mm(lo, hi):
        return _dot_nt(h, wt_bf[lo:hi, :])

    aq_ref[...] = (mm(_R_AQ, _R_AK) * (A_HEAD_DIM ** -0.5 * LOG2E)).astype(BF16)
    ak_ref[...] = mm(_R_AK, _R_AV).astype(BF16)
    av_ref[...] = mm(_R_AV, _R_BQ).astype(BF16)
    bq_ref[...] = (mm(_R_BQ, _R_BK) * (B_KEY_DIM ** -0.5)).astype(BF16)
    bk_ref[...] = mm(_R_BK, _R_BV).astype(BF16)
    bv_ref[...] = mm(_R_BV, _R_ALPHA).astype(BF16)
    r = mm(_R_BR, _R_GATE)
    br_ref[...] = (r * jax.nn.sigmoid(r)).astype(BF16)
    for c in range(_R_GATE, _R_END, 512):
        gate_ref[:, c - _R_GATE:c - _R_GATE + 512] = jax.nn.sigmoid(mm(c, c + 512)).astype(BF16)
    z = _dot(mm(_R_ALPHA, _R_ALPHA + LANES).astype(BF16), wal2_bf[...]) + bal_ref[...]
    lga_ref[...] = (jnp.minimum(z, 0.0) - jnp.log(1.0 + jnp.exp(-jnp.abs(z)))) * (1.0 / B_GATE_TAU)


def _inproj(x, g, w_in, w_al2, b_al, layer, moe=None):
    t, d = x.shape
    assert w_in.shape[2] == _R_END
    row = lambda w: pl.BlockSpec((INPROJ_ROWS, w), lambda i: (i, 0))
    full = lambda a: pl.BlockSpec(a.shape, lambda i: (0,) * a.ndim)
    sds = lambda w, dt: jax.ShapeDtypeStruct((t, w), dt)
    widths = [(512, BF16), (512, BF16), (512, BF16), (256, BF16), (256, BF16), (512, BF16),
              (256, F32), (512, BF16), (2048, BF16)]
    fused = moe is not None
    moe_specs, moe_out_specs, moe_out_shape = [], [], []
    if fused:
        n_chunks = INPROJ_ROWS // ROW_TILE * LOCAL_CHUNKS
        moe_specs = [row(LANES), pl.BlockSpec((n_chunks, CHUNK_ROWS, d), lambda i: (i, 0, 0))]
        moe_out_specs, moe_out_shape = [row(d)], [sds(d, F32)]
    return pl.pallas_call(
        functools.partial(_inproj_kernel, layer, fused),
        grid=(t // INPROJ_ROWS,),
        in_specs=[row(d)] + moe_specs + [full(g), pl.BlockSpec(memory_space=pl.ANY),
                                         pl.BlockSpec((1,) + w_al2.shape[1:], lambda i: (layer, 0, 0)),
                                         full(b_al)],
        out_specs=moe_out_specs + [row(w) for w, _ in widths],
        out_shape=moe_out_shape + [sds(w, dt) for w, dt in widths],
        scratch_shapes=[pltpu.VMEM((_R_END, d), BF16), pltpu.VMEM((LANES, B_KEY_WIDTH), BF16),
                        pltpu.VMEM((2, _W_PIECE, d), F32), pltpu.SemaphoreType.DMA((2,))],
        compiler_params=_params("arbitrary"),
        name="inproj",
    )(x, *(moe or ()), g, jnp.swapaxes(w_in, 1, 2), w_al2, b_al)


def _band_kernel(q_ref, k0_ref, k1_ref, k2_ref, v0_ref, v1_ref, v2_ref, u_ref, o_ref, bias_ref):
    i = pl.program_id(1)
    lane = lax.broadcasted_iota(I32, (1, LANES), 1)
    low = lane < A_HEAD_DIM
    ones = jnp.ones((BAND_KEYS, LANES), BF16)

    @pl.when((pl.program_id(0) == 0) & (i == 0))
    def _():
        cq = lax.broadcasted_iota(I32, (ROW_TILE, BAND_KEYS), 0) >> LOG_CHUNK
        ck = lax.broadcasted_iota(I32, (ROW_TILE, BAND_KEYS), 1) >> LOG_CHUNK
        valid = (ck >= cq) & (ck <= cq + A_LEFT_CHUNKS)
        for h in range(A_HEADS):
            rows = jnp.broadcast_to(u_ref[h:h + 1, :], (ROW_TILE, BIAS_PERIOD))
            rows = pltpu.roll(rows, BIAS_PERIOD - (ROW_TILE - 1), 1, stride=1, stride_axis=0)
            bias_ref[h // 2, (h % 2) * ROW_TILE:(h % 2 + 1) * ROW_TILE, :] = jnp.where(
                valid, rows[:, :BAND_KEYS], NEG)

    def attend(pen):
        for p in range(A_HEADS // 2):
            sl = slice(p * LANES, (p + 1) * LANES)
            qp = q_ref[:, sl]
            zero = jnp.zeros_like(qp)
            q2 = jnp.concatenate([jnp.where(low, qp, zero), jnp.where(low, zero, qp)], axis=0)
            kp = jnp.concatenate([k0_ref[:, sl], k1_ref[:, sl], k2_ref[:, sl]], axis=0)
            vp = jnp.concatenate([v0_ref[:, sl], v1_ref[:, sl], v2_ref[:, sl]], axis=0)
            s = _dot_nt(q2, kp) + bias_ref[p]
            if pen is not None:
                s = s + pen
            pe = jnp.exp2(s - jnp.max(s, axis=-1, keepdims=True)).astype(BF16)
            o2 = _dot(pe, jnp.concatenate([vp, ones], axis=1))
            o = o2[:, :LANES] * (1.0 / o2[:, LANES:])
            o_ref[:, sl] = jnp.where(low, o[:ROW_TILE], o[ROW_TILE:]).astype(BF16)

    @pl.when(i >= BAND_TILES - 1)
    def _():
        attend(None)

    @pl.when(i < BAND_TILES - 1)
    def _():
        col = lax.broadcasted_iota(I32, (1, BAND_KEYS), 1)
        attend(jnp.where(col < (BAND_TILES - 1 - i) * ROW_TILE, NEG, 0.0).astype(F32))


def _band_bias_vector(rel_table):
    h = rel_table.shape[0]
    tab = rel_table.astype(F32) * LOG2E
    shift = A_LEFT_CHUNKS * CHUNK + ROW_TILE - 1
    n_far = shift - A_MAX_REL + 1
    span = ROW_TILE + BAND_KEYS - 1
    assert span - 1 - shift <= A_MAX_REL and span <= BIAS_PERIOD
    u = jnp.concatenate([jnp.broadcast_to(tab[:, 2 * A_MAX_REL:], (h, n_far)),
                         tab[:, 2 * A_MAX_REL - 1:2 * A_MAX_REL - 1 - (span - n_far):-1]], axis=1)
    return jnp.pad(u, ((0, 0), (0, BIAS_PERIOD - span)))


def _band_attention(q, k, v, u, batch):
    t, w = q.shape
    nb = t // batch // ROW_TILE
    cur = lambda b, i: (b * nb + i, 0)
    back = lambda n: (lambda b, i: (b * nb + jnp.maximum(i - n, 0), 0))
    blk = lambda f: pl.BlockSpec((ROW_TILE, w), f)
    return pl.pallas_call(
        _band_kernel,
        grid=(batch, nb),
        in_specs=[blk(cur), blk(back(2)), blk(back(1)), blk(cur), blk(back(2)), blk(back(1)), blk(cur),
                  pl.BlockSpec(u.shape, lambda b, i: (0, 0))],
        out_specs=blk(cur),
        out_shape=jax.ShapeDtypeStruct((t, w), BF16),
        scratch_shapes=[pltpu.VMEM((A_HEADS // 2, 2 * ROW_TILE, BAND_KEYS), F32)],
        compiler_params=_params("arbitrary", "arbitrary"),
        name="band_attn",
    )(q, k, k, k, v, v, v, u)


def _gla_constants():
    c = CHUNK
    t = np.arange(c)[:, None]
    r = np.arange(c)[None, :]
    mats = [(r <= t), (r > t)]
    lvl = np.full((c, c), -1, np.int32)
    lvl[np.arange(c), np.arange(c)] = N_LEVELS
    for l in range(N_LEVELS):
        m = (c // 2) >> l
        mid = (t // (2 * m)) * (2 * m) + m
        upper = t >= mid
        mats.append(np.where(upper, (r >= mid) & (r <= t), (r > t) & (r < mid)))
        s = r
        same = (s // (2 * m)) == (t // (2 * m))
        lvl[np.asarray(same & upper & (s < mid))] = l
    eye = np.eye(CHUNKS_PER_TILE)
    mexp = np.concatenate([np.kron(eye, m) for m in mats], axis=0).astype(np.float32)
    lvl = np.tile(lvl, (1, B_HEADS))
    return jnp.asarray(mexp, BF16), jnp.asarray(lvl, I32)


def _gla_kernel(q_ref, k_ref, v_ref, g_ref, r_ref, gn_ref, mexp_ref, lvl_ref, o_ref, s_ref):
    @pl.when(pl.program_id(1) == 0)
    def _():
        s_ref[...] = jnp.zeros_like(s_ref)

    kw = B_KEY_WIDTH
    ri = lax.broadcasted_iota(I32, (kw, kw), 0) >> LOG_CHUNK
    ci = lax.broadcasted_iota(I32, (kw, kw), 1) >> LOG_CHUNK
    bd = ri == ci
    head_ind = jnp.where(bd, 1.0, 0.0).astype(BF16)
    ri2 = lax.broadcasted_iota(I32, (kw, 2 * kw), 0) >> LOG_CHUNK
    ci2 = (lax.broadcasted_iota(I32, (kw, 2 * kw), 1) & (kw - 1)) >> LOG_CHUNK
    bd2 = ri2 == ci2
    lvl = lvl_ref[...]
    row8 = lax.broadcasted_iota(I32, (16, kw), 0)
    ones = jnp.ones((16, LANES), BF16)
    zero_b = jnp.zeros((kw, kw), BF16)
    chunks = [slice(c * CHUNK, (c + 1) * CHUNK) for c in range(CHUNKS_PER_TILE)]

    def head_blocks(x):
        return jnp.where(bd, jnp.concatenate([x] * B_HEADS, axis=0), zero_b)

    q = q_ref[...].astype(F32)
    k = k_ref[...].astype(F32)
    g = g_ref[...]
    gb = g.astype(BF16)
    half = EXP_ROWS * CHUNKS_PER_TILE // 2
    w = jnp.exp(jnp.concatenate([_dot(mexp_ref[:half, :], gb), _dot(mexp_ref[half:, :], gb)], axis=0))
    w_cum = w[0:ROW_TILE]
    qt = (q * w_cum).astype(BF16)
    kb = (k * w[ROW_TILE:2 * ROW_TILE]).astype(BF16)
    qk = (q * k).astype(BF16)

    attn = [jnp.zeros((CHUNK, kw), F32) for _ in chunks]
    for l in range(N_LEVELS):
        wl = w[(2 + l) * ROW_TILE:(3 + l) * ROW_TILE]
        qh = (q * wl).astype(BF16)
        kh = (k * wl).astype(BF16)
        for c, rows in enumerate(chunks):
            attn[c] = jnp.where(lvl == l, _dot_nt(qh[rows], head_blocks(kh[rows])), attn[c])

    vstacks, kvs, dcols = [], [], []
    for c, rows in enumerate(chunks):
        attn[c] = jnp.where(lvl == N_LEVELS, _dot(qk[rows], head_ind), attn[c])
        v = v_ref[rows, :]
        vstack = jnp.concatenate([v[:, j * LANES:(j + 1) * LANES] for j in range(B_HEADS)], axis=0)
        vstacks.append(vstack)
        kvs.append(_dot_tn(head_blocks(kb[rows]), vstack))
        d = jnp.exp(jnp.sum(g[rows], axis=0, keepdims=True))
        d1 = d.astype(BF16).astype(F32)
        dp = jnp.where(row8 == 0, d1, jnp.where(row8 == 1, d - d1, 0.0)).astype(BF16)
        dcols.append(_dot_tn(dp, ones))

    s = s_ref[...]
    for c, rows in enumerate(chunks):
        lhs = jnp.concatenate([attn[c].astype(BF16), qt[rows]], axis=1)
        lhs = jnp.where(bd2, jnp.concatenate([lhs] * B_HEADS, axis=0), jnp.zeros((kw, 2 * kw), BF16))
        rhs = jnp.concatenate([vstacks[c], s.astype(BF16)], axis=0)
        o = _dot(lhs, rhs)
        s = dcols[c] * s + kvs[c]
        for j in range(B_HEADS):
            oj = o[j * CHUNK:(j + 1) * CHUNK]
            sl = slice(j * LANES, (j + 1) * LANES)
            y = oj * lax.rsqrt(jnp.mean(oj * oj, axis=-1, keepdims=True) + EPS) * gn_ref[...]
            o_ref[rows, sl] = (y * r_ref[rows, sl].astype(F32)).astype(BF16)
    s_ref[...] = s


def _gla(q, k, v, g, r, gn, batch):
    t = q.shape[0]
    nb = t // batch // ROW_TILE
    mexp, lvl = _gla_constants()
    cur = lambda b, i: (b * nb + i, 0)
    blk = lambda w: pl.BlockSpec((ROW_TILE, w), cur)
    full = lambda a: pl.BlockSpec(a.shape, lambda b, i: (0,) * a.ndim)
    return pl.pallas_call(
        _gla_kernel,
        grid=(batch, nb),
        in_specs=[blk(B_KEY_WIDTH), blk(B_KEY_WIDTH), blk(B_VAL_WIDTH), blk(B_KEY_WIDTH), blk(B_VAL_WIDTH),
                  full(gn), full(mexp), full(lvl)],
        out_specs=blk(B_VAL_WIDTH),
        out_shape=jax.ShapeDtypeStruct((t, B_VAL_WIDTH), BF16),
        scratch_shapes=[pltpu.VMEM((B_KEY_WIDTH, B_VAL_DIM), F32)],
        compiler_params=_params("arbitrary", "arbitrary"),
        name="gla",
    )(q, k, v, g, r, gn, mexp, lvl)


def _token_kernel(x_ref, oa_ref, ob_ref, gate_ref, wb0_ref, wb1_ref, wmix_ref, gx_ref, wq_ref,
                  km_ref, vm_ref, wo_ref, gf_ref, wr_ref, br_ref, ltri_ref, utri_ref,
                  x2_ref, hs_ref, route_ref, cnt_ref):
    ma = _dot(oa_ref[...], wb0_ref[...])
    mb = _dot(ob_ref[...], wb1_ref[...])
    merged = (gate_ref[:, :D_MODEL].astype(F32) * ma + gate_ref[:, D_MODEL:].astype(F32) * mb).astype(BF16)
    x1 = x_ref[...] + _dot(merged, wmix_ref[...])

    h2 = _rms(x1, gx_ref[...]).astype(BF16)
    qx = (_dot(h2, wq_ref[...]) * (X_HEAD_DIM ** -0.5)).astype(BF16)
    heads = []
    for h in range(X_HEADS):
        sl = slice(h * X_HEAD_DIM, (h + 1) * X_HEAD_DIM)
        s = _dot_nt(qx[:, sl], km_ref[0, :, sl])
        m = jnp.max(s, axis=-1, keepdims=True)
        pe = jnp.exp(s - m)
        l = jnp.sum(pe, axis=-1, keepdims=True)
        heads.append((_dot(pe.astype(BF16), vm_ref[0, :, sl]) * (1.0 / l)).astype(BF16))
    x2 = x1 + _dot(jnp.concatenate(heads, axis=1), wo_ref[...])
    x2_ref[...] = x2

    h3 = _rms(x2, gf_ref[...])

    h3_hi = h3.astype(BF16)
    h3_lo = (h3 - h3_hi.astype(F32)).astype(BF16)
    hw = _dot(h3_hi, wr_ref[...])
    logits = hw[:, :LANES] + hw[:, LANES:] + _dot(h3_lo, wr_ref[:, :LANES]) + br_ref[...]
    for h in range(TOKEN_TILES_PER_STEP):
        rows = slice(h * ROW_TILE, (h + 1) * ROW_TILE)
        chunks = pl.ds(h * LOCAL_CHUNKS, LOCAL_CHUNKS)
        _route_and_sort(logits[rows], h3_hi[rows], ltri_ref, utri_ref, hs_ref.at[chunks],
                        route_ref.at[pl.ds(h * ROW_TILE, ROW_TILE)], cnt_ref.at[h])


def _route_and_sort(logits, h3_hi, ltri_ref, utri_ref, hs_ref, route_ref, cnt_ref):
    lane = lax.broadcasted_iota(I32, logits.shape, 1).astype(F32)
    big = jnp.float32(LANES)
    gl = jnp.where(lane < N_GROUPS, logits, NEG)
    gmax = jnp.max(gl, axis=-1, keepdims=True)
    gidx = jnp.min(jnp.where(gl == gmax, lane, big), axis=-1, keepdims=True)
    g_w = 1.0 / jnp.sum(jnp.exp(gl - gmax), axis=-1, keepdims=True)
    lo = N_GROUPS + EXPERTS_PER_GROUP * gidx
    el = jnp.where((lane >= lo) & (lane < lo + EXPERTS_PER_GROUP), logits, NEG)
    v1 = jnp.max(el, axis=-1, keepdims=True)
    i1 = jnp.min(jnp.where(el == v1, lane, big), axis=-1, keepdims=True)
    el2 = jnp.where(lane == i1, NEG, el)
    v2 = jnp.max(el2, axis=-1, keepdims=True)
    i2 = jnp.min(jnp.where(el2 == v2, lane, big), axis=-1, keepdims=True)
    e21 = jnp.exp(v2 - v1)
    w1 = g_w / (1.0 + e21)
    w2 = w1 * e21
    oh0 = jnp.where(lane == i1 - N_GROUPS, 1.0, 0.0)
    oh1 = jnp.where(lane == i2 - N_GROUPS, 1.0, 0.0)
    oh = oh0 + oh1
    nch = jnp.floor((jnp.sum(oh, axis=0, keepdims=True) + (CHUNK_ROWS - 1)) * (1.0 / CHUNK_ROWS))
    nch8 = jnp.broadcast_to(nch, (8, LANES))
    start = _dot(nch8.astype(BF16), utri_ref[...])[0:1] * CHUNK_ROWS
    rank = _dot(ltri_ref[...], oh.astype(BF16))
    row = start + rank
    pos0 = jnp.sum(row * oh0, axis=-1, keepdims=True)
    pos1 = jnp.sum(row * oh1, axis=-1, keepdims=True)
    route = jnp.where(lane == 0, pos0, jnp.where(lane == 1, pos1, 0.0))
    route_t = jnp.transpose(route)
    r = lax.broadcasted_iota(I32, (LOCAL_ROWS, ROW_TILE), 0).astype(F32)
    p0 = jnp.where(r == route_t[0:1, :], 1.0, 0.0).astype(BF16)
    p1 = jnp.where(r == route_t[1:2, :], 1.0, 0.0).astype(BF16)

    def gate_cols(w):
        hi = w.astype(BF16).astype(F32)
        return jnp.where(lane == 0, hi, jnp.where(lane == 1, w - hi, 0.0)).astype(BF16)

    sorted_rows = jnp.concatenate([_dot(p0 + p1, h3_hi), _dot(p0, gate_cols(w1)) + _dot(p1, gate_cols(w2))],
                                  axis=1)
    hs_ref[...] = sorted_rows.astype(BF16).reshape(hs_ref.shape)
    route_ref[...] = route
    cnt_ref[...] = nch8


def _token(x, oa, ob, gates, wb0, wb1, wmix, gx, wq, km, vm, wo, gf, wr, br, batch):
    t, d = x.shape
    n = TOKEN_TILES_PER_STEP
    nb = t // batch // (n * ROW_TILE)
    nt = t // ROW_TILE
    ltri = jnp.asarray(np.tril(np.ones((ROW_TILE, ROW_TILE), np.float32), -1), BF16)
    utri = jnp.asarray(np.triu(np.ones((LANES, LANES), np.float32), 1), BF16)
    cur = lambda b, i: (b * nb + i, 0)
    cur3 = lambda b, i: (b * nb + i, 0, 0)
    blk = lambda w: pl.BlockSpec((n * ROW_TILE, w), cur)
    full = lambda a: pl.BlockSpec(a.shape, lambda b, i: (0,) * a.ndim)
    mem = pl.BlockSpec((1,) + km.shape[1:], lambda b, i: (b, 0, 0))
    return pl.pallas_call(
        _token_kernel,
        grid=(batch, nb),
        in_specs=[blk(d), blk(A_WIDTH), blk(B_VAL_WIDTH), blk(2 * d), full(wb0), full(wb1), full(wmix),
                  full(gx), full(wq), mem, mem, full(wo), full(gf), full(wr), full(br), full(ltri), full(utri)],
        out_specs=[blk(d), pl.BlockSpec((n * LOCAL_CHUNKS, CHUNK_ROWS, SORT_WIDTH), cur3),
                   blk(LANES), pl.BlockSpec((n, 8, LANES), cur3)],
        out_shape=[jax.ShapeDtypeStruct((t, d), F32),
                   jax.ShapeDtypeStruct((nt * LOCAL_CHUNKS, CHUNK_ROWS, SORT_WIDTH), BF16),
                   jax.ShapeDtypeStruct((t, LANES), F32),
                   jax.ShapeDtypeStruct((nt, 8, LANES), F32)],
        compiler_params=_params("arbitrary", "arbitrary"),
        name="token",
    )(x, oa, ob, gates, wb0, wb1, wmix, gx, wq, km, vm, wo, gf, wr, br, ltri, utri)


def _expert_kernel(te_ref, nu_ref, nv_ref, ch_ref, hs_hbm, wg_ref, wu_ref, wd_ref, ys_hbm,
                   xbuf, ybuf, wg_bf, wu_bf, wd_bf, gsem, ssem):
    i = pl.program_id(0)
    n_used = nu_ref[0]
    slot = lax.rem(i, 2)

    def for_chunks(tile, fn):
        nv = nv_ref[tile]

        @pl.when(nv == TILE_CHUNKS)
        def _():
            for c in range(TILE_CHUNKS):
                fn(c)

        @pl.when(nv != TILE_CHUNKS)
        def _():
            def body(c, carry):
                fn(c)
                return carry

            lax.fori_loop(0, nv, body, 0)

    def gather(tile, s, start):
        def one(c):
            cp = pltpu.make_async_copy(hs_hbm.at[ch_ref[tile * TILE_CHUNKS + c]], xbuf.at[s, c], gsem.at[s])
            cp.start() if start else cp.wait()

        for_chunks(tile, one)

    def scatter(tile, s, start):
        def one(c):
            cp = pltpu.make_async_copy(ybuf.at[s, c], ys_hbm.at[ch_ref[tile * TILE_CHUNKS + c]], ssem.at[s])
            cp.start() if start else cp.wait()

        for_chunks(tile, one)

    @pl.when(i == 0)
    def _():
        xbuf[...] = jnp.zeros_like(xbuf)
        gather(0, 0, True)

    @pl.when(i < n_used)
    def _():
        @pl.when(i + 1 < n_used)
        def _():
            gather(i + 1, 1 - slot, True)

        gather(i, slot, False)

        @pl.when(i >= 2)
        def _():
            scatter(i - 2, slot, False)

        @pl.when((i == 0) | (te_ref[i] != te_ref[jnp.maximum(i - 1, 0)]))
        def _():
            wg_bf[...] = wg_ref[0].astype(BF16)
            wu_bf[...] = wu_ref[0].astype(BF16)
            wd_bf[...] = wd_ref[0].astype(BF16)

        xg = xbuf[slot].reshape(EXPERT_ROWS, SORT_WIDTH)
        x = xg[:, :D_MODEL]
        hg = _dot(x, wg_bf[...])
        hu = _dot(x, wu_bf[...])
        hid = (hg * jax.nn.sigmoid(hg) * hu).astype(BF16)
        g = xg[:, D_MODEL:].astype(F32)
        y = ((g[:, 0:1] + g[:, 1:2]) * _dot(hid, wd_bf[...])).astype(BF16)
        y = jnp.concatenate([y, jnp.zeros((EXPERT_ROWS, LANES), BF16)], axis=1)
        ybuf[slot] = y.reshape(TILE_CHUNKS, CHUNK_ROWS, SORT_WIDTH)
        scatter(i, slot, True)

        @pl.when(i == n_used - 1)
        def _():
            scatter(i, slot, False)

            @pl.when(i >= 1)
            def _():
                scatter(i - 1, 1 - slot, False)


def _experts(hs, tile_expert, n_used, n_valid, chunks, wg, wu, wd, layer):
    n_tiles = tile_expert.shape[0]
    last = lambda i, te, nu, nv, ch: jnp.minimum(i, nu[0] - 1)
    wmap = lambda i, te, nu, nv, ch: (layer * N_EXPERTS + te[last(i, te, nu, nv, ch)], 0, 0)
    anyspace = pl.BlockSpec(memory_space=pl.ANY)
    grid_spec = pltpu.PrefetchScalarGridSpec(
        num_scalar_prefetch=4,
        grid=(n_tiles,),
        in_specs=[anyspace,
                  pl.BlockSpec((1, D_MODEL, EXPERT_FF), wmap),
                  pl.BlockSpec((1, D_MODEL, EXPERT_FF), wmap),
                  pl.BlockSpec((1, EXPERT_FF, D_MODEL), wmap)],
        out_specs=anyspace,
        scratch_shapes=[pltpu.VMEM((2, TILE_CHUNKS, CHUNK_ROWS, SORT_WIDTH), BF16),
                        pltpu.VMEM((2, TILE_CHUNKS, CHUNK_ROWS, SORT_WIDTH), BF16),
                        pltpu.VMEM((D_MODEL, EXPERT_FF), BF16), pltpu.VMEM((D_MODEL, EXPERT_FF), BF16),
                        pltpu.VMEM((EXPERT_FF, D_MODEL), BF16),
                        pltpu.SemaphoreType.DMA((2,)), pltpu.SemaphoreType.DMA((2,))],
    )
    return pl.pallas_call(
        _expert_kernel,
        grid_spec=grid_spec,
        out_shape=jax.ShapeDtypeStruct(hs.shape, BF16),
        input_output_aliases={4: 0},
        compiler_params=_params("arbitrary"),
        name="experts",
    )(tile_expert, n_used, n_valid, chunks, hs, wg, wu, wd)


def _combine_kernel(x_ref, route_ref, ys_ref, gfin_ref, o_ref):
    r = lax.broadcasted_iota(I32, (ROW_TILE, LOCAL_ROWS), 1).astype(F32)
    sel = jnp.where(r == route_ref[:, 0:1], 1.0, jnp.where(r == route_ref[:, 1:2], 1.0, 0.0)).astype(BF16)
    x3 = x_ref[...] + _dot(sel, ys_ref[...].reshape(LOCAL_ROWS, D_MODEL))
    o_ref[...] = _rms(x3, gfin_ref[...])


def _combine(x2, route, ys, gfin):
    t, d = x2.shape
    nt = t // ROW_TILE
    return pl.pallas_call(
        _combine_kernel,
        grid=(nt,),
        in_specs=[pl.BlockSpec((ROW_TILE, d), lambda i: (i, 0)),
                  pl.BlockSpec((ROW_TILE, LANES), lambda i: (i, 0)),
                  pl.BlockSpec((LOCAL_CHUNKS, CHUNK_ROWS, d), lambda i: (i, 0, 0)),
                  pl.BlockSpec((1, d), lambda i: (0, 0))],
        out_specs=pl.BlockSpec((ROW_TILE, d), lambda i: (i, 0)),
        out_shape=jax.ShapeDtypeStruct((t, d), F32),
        compiler_params=_params("arbitrary"),
        name="combine",
    )(x2, route, ys, gfin)


def _chunk_plan(nch, n_tiles):
    nt = nch.shape[0]
    local_start = jnp.cumsum(nch, axis=1) - nch
    cum = jnp.cumsum(nch, axis=0)
    total = cum[-1]
    tiles = (total + TILE_CHUNKS - 1) // TILE_CHUNKS
    tile_end = jnp.cumsum(tiles)
    n_used = tile_end[-1:]
    tile_ids = jnp.arange(n_tiles, dtype=I32)
    tile_expert = jnp.minimum(jnp.sum((tile_end[None, :] <= tile_ids[:, None]).astype(I32), axis=1),
                              N_EXPERTS - 1)
    sel = (tile_expert[:, None] == jnp.arange(N_EXPERTS, dtype=I32)[None, :]).astype(I32)
    pick = lambda table: jnp.sum(sel[:, :, None] * table.T[None, :, :], axis=1)
    first_tile = jnp.sum(sel * (tile_end - tiles)[None, :], axis=1)
    slot = (tile_ids - first_tile)[:, None] * TILE_CHUNKS + jnp.arange(TILE_CHUNKS, dtype=I32)[None, :]
    valid = (slot < jnp.sum(sel * total[None, :], axis=1)[:, None]) & (tile_ids < n_used)[:, None]
    src_tile = jnp.sum((pick(cum)[:, None, :] <= slot[:, :, None]).astype(I32), axis=2)
    src_tile = jnp.minimum(src_tile, nt - 1)
    at = (src_tile[:, :, None] == jnp.arange(nt, dtype=I32)[None, None, :]).astype(I32)
    before = jnp.sum(at * pick(cum - nch)[:, None, :], axis=2)
    start = jnp.sum(at * pick(local_start)[:, None, :], axis=2)
    chunk = jnp.where(valid, src_tile * LOCAL_CHUNKS + start + slot - before, 0)
    return tile_expert, n_used, jnp.sum(valid.astype(I32), axis=1), chunk.reshape(-1)


def kernel(x, mem, norm_mix_g, w_in, rel_bias, gla_w_alpha, gla_b_alpha, gla_norm_g, w_branch, w_mix_out, norm_x_g, mem_norm_g, w_xq, w_xkv, w_xo, norm_ffn_g, w_group_router, b_group_router, w_expert_router, b_expert_router, w_exp_gate, w_exp_up, w_exp_down, final_norm_g):
    batch, seq, d = x.shape
    depth = w_in.shape[0]
    t = batch * seq
    assert d == D_MODEL and seq % (TOKEN_TILES_PER_STEP * ROW_TILE) == 0
    nt = t // ROW_TILE
    n_tiles = nt * LOCAL_CHUNKS // TILE_CHUNKS + N_EXPERTS

    xf = x.reshape(t, d)
    km_all, vm_all = _memkv(mem, mem_norm_g, w_xkv.astype(BF16))
    row = lambda a: a.reshape(1, -1).astype(F32)

    moe = None
    for l in range(depth):
        res = _inproj(xf, row(norm_mix_g[l]), w_in, gla_w_alpha, row(gla_b_alpha[l]), l, moe)
        if moe is not None:
            xf, res = res[0], res[1:]
        aq, ak, av, bq, bk, bv, lga, br, gates = res

        oa = _band_attention(aq, ak, av, _band_bias_vector(rel_bias[l]), batch)
        ob = _gla(bq, bk, bv, lga, br, row(gla_norm_g[l]), batch)

        wr = jnp.pad(jnp.concatenate([w_group_router[l], w_expert_router[l]], axis=1).astype(F32),
                     ((0, 0), (0, LANES - N_GROUPS - N_EXPERTS)))
        wr_hi = wr.astype(BF16)
        wr = jnp.concatenate([wr_hi, (wr - wr_hi.astype(F32)).astype(BF16)], axis=1)
        brt = jnp.pad(jnp.concatenate([b_group_router[l], b_expert_router[l]]).astype(F32),
                      (0, LANES - N_GROUPS - N_EXPERTS)).reshape(1, LANES)
        x2, hs, route, cnt = _token(
            xf, oa, ob, gates, w_branch[l, 0].astype(BF16), w_branch[l, 1].astype(BF16),
            w_mix_out[l].astype(BF16), row(norm_x_g[l]), w_xq[l].astype(BF16), km_all[l], vm_all[l],
            w_xo[l].astype(BF16), row(norm_ffn_g[l]), wr, brt, batch)

        plan = _chunk_plan(cnt[:, 0, :N_EXPERTS].astype(I32), n_tiles)
        e3 = lambda w: w.reshape((depth * N_EXPERTS,) + w.shape[3:])
        ys = _experts(hs, *plan, e3(w_exp_gate), e3(w_exp_up), e3(w_exp_down), l)
        xf, moe = x2, (route, ys)

    return _combine(x2, route, ys, row(final_norm_g)).reshape(batch, seq, d)
```

```python
import functools

import numpy as np
import jax
import jax.numpy as jnp
from jax import lax
from jax.experimental import pallas as pl
from jax.experimental.pallas import tpu as pltpu

F32 = jnp.float32
BF16 = jnp.bfloat16
I32 = jnp.int32

D_MODEL = 1024
CHUNK = 64
EPS = 1e-6
A_HEADS = 8
A_HEAD_DIM = 64
A_WIDTH = 512
A_LEFT_CHUNKS = 8
A_MAX_REL = 256
B_HEADS = 4
B_KEY_DIM = 64
B_VAL_DIM = 128
B_KEY_WIDTH = 256
B_VAL_WIDTH = 512
B_GATE_RANK = 16
B_GATE_TAU = 16.0
X_HEADS = 4
X_HEAD_DIM = 256
N_GROUPS = 4
EXPERTS_PER_GROUP = 8
N_EXPERTS = N_GROUPS * EXPERTS_PER_GROUP
EXPERT_FF = 256

LANES = 128
ROW_TILE = 256
CHUNKS_PER_TILE = ROW_TILE // CHUNK
BAND_TILES = A_LEFT_CHUNKS // CHUNKS_PER_TILE + 1
BAND_KEYS = BAND_TILES * ROW_TILE
BIAS_PERIOD = 1024
LOG_CHUNK = 6
N_LEVELS = LOG_CHUNK
EXP_ROWS = (2 + N_LEVELS) * CHUNK
CHUNK_ROWS = 16
TOKEN_TILES_PER_STEP = 2
INPROJ_ROWS = 512
COMBINE_TILES_PER_STEP = 2
EXPERT_ROWS = 512
TILE_CHUNKS = EXPERT_ROWS // CHUNK_ROWS
LOCAL_CHUNKS = 2 * ROW_TILE // CHUNK_ROWS + N_EXPERTS
LOCAL_ROWS = LOCAL_CHUNKS * CHUNK_ROWS
assert (2 * ROW_TILE + N_EXPERTS * (CHUNK_ROWS - 1)) // CHUNK_ROWS < LOCAL_CHUNKS
SORT_WIDTH = D_MODEL + LANES
NEG = -1e30
LOG2E = 1.4426950408889634
VMEM_LIMIT = 56 * 1024 * 1024


def _params(*sem):
    return pltpu.CompilerParams(dimension_semantics=sem, vmem_limit_bytes=VMEM_LIMIT)


def _rms(x, g):
    return x * lax.rsqrt(jnp.mean(x * x, axis=-1, keepdims=True) + EPS) * g


def _dot(a, b):
    return jnp.dot(a, b, preferred_element_type=F32)


def _dot_nt(a, b):
    return lax.dot_general(a, b, (((1,), (1,)), ((), ())), preferred_element_type=F32)


def _dot_tn(a, b):
    return lax.dot_general(a, b, (((0,), (0,)), ((), ())), preferred_element_type=F32)


def _memkv_kernel(mem_ref, g_ref, w_ref, k_ref, v_ref):
    mn = _rms(mem_ref[0], g_ref[...]).astype(BF16)
    kv = _dot(mn, w_ref[0])
    k_ref[0, 0] = kv[:, :D_MODEL].astype(BF16)
    v_ref[0, 0] = kv[:, D_MODEL:].astype(BF16)


def _memkv(mem, g, w_xkv):
    depth = w_xkv.shape[0]
    b, m, d = mem.shape
    out = jax.ShapeDtypeStruct((depth, b, m, d), BF16)
    return pl.pallas_call(
        _memkv_kernel,
        grid=(depth, b),
        in_specs=[pl.BlockSpec((1, m, d), lambda l, i: (i, 0, 0)),
                  pl.BlockSpec((1, d), lambda l, i: (0, 0)),
                  pl.BlockSpec((1, d, 2 * d), lambda l, i: (l, 0, 0))],
        out_specs=[pl.BlockSpec((1, 1, m, d), lambda l, i: (l, i, 0, 0)),
                   pl.BlockSpec((1, 1, m, d), lambda l, i: (l, i, 0, 0))],
        out_shape=[out, out],
        compiler_params=_params("arbitrary", "arbitrary"),
        name="memkv",
    )(mem, g.reshape(1, d), w_xkv)


_R_AQ, _R_AK, _R_AV = 0, 512, 1024
_R_BQ, _R_BK, _R_BV = 1536, 1792, 2048
_R_ALPHA, _R_BR, _R_GATE, _R_END = 2560, 2576, 3088, 5136
_W_PIECE = 512


def _inproj_kernel(layer, fused, *refs):
    if fused:
        x_ref, route_ref, ys_ref, g_ref, wt_hbm, wal2_ref, bal_ref = refs[:7]
        refs = refs[7:]
        x3_ref, refs = refs[0], refs[1:]
    else:
        x_ref, g_ref, wt_hbm, wal2_ref, bal_ref = refs[:5]
        refs = refs[5:]
    (aq_ref, ak_ref, av_ref, bq_ref, bk_ref, bv_ref, lga_ref, br_ref, gate_ref,
     wt_bf, wal2_bf, stage, sem) = refs

    @pl.when(pl.program_id(0) == 0)
    def _():
        pieces = [(c, min(_W_PIECE, _R_END - c)) for c in range(0, _R_END, _W_PIECE)]

        def piece_copy(p):
            c, n = pieces[p]
            return pltpu.make_async_copy(wt_hbm.at[layer, pl.ds(c, n), :], stage.at[p % 2, pl.ds(0, n), :],
                                         sem.at[p % 2])

        piece_copy(0).start()
        for p, (c, n) in enumerate(pieces):
            if p + 1 < len(pieces):
                piece_copy(p + 1).start()
            piece_copy(p).wait()
            wt_bf[c:c + n, :] = stage[p % 2, 0:n, :].astype(BF16)
        wal2_bf[...] = jnp.concatenate(
            [wal2_ref[0].astype(BF16), jnp.zeros((LANES - B_GATE_RANK, B_KEY_WIDTH), BF16)], axis=0)

    if fused:
        r = lax.broadcasted_iota(I32, (ROW_TILE, LOCAL_ROWS), 1).astype(F32)
        tiles = []
        for t in range(INPROJ_ROWS // ROW_TILE):
            rows = slice(t * ROW_TILE, (t + 1) * ROW_TILE)
            pos = route_ref[rows, :]
            sel = jnp.where(r == pos[:, 0:1], 1.0, jnp.where(r == pos[:, 1:2], 1.0, 0.0)).astype(BF16)
            ys = ys_ref[t * LOCAL_CHUNKS:(t + 1) * LOCAL_CHUNKS].reshape(LOCAL_ROWS, D_MODEL)
            tiles.append(x_ref[rows, :] + _dot(sel, ys))
        x = jnp.concatenate(tiles, axis=0)
        x3_ref[...] = x
    else:
        x = x_ref[...]
    h = _rms(x, g_ref[...]).astype(BF16)

    def mm(lo, hi):
        return _dot_nt(h, wt_bf[lo:hi, :])

    aq_ref[...] = (mm(_R_AQ, _R_AK) * (A_HEAD_DIM ** -0.5 * LOG2E)).astype(BF16)
    ak_ref[...] = mm(_R_AK, _R_AV).astype(BF16)
    av_ref[...] = mm(_R_AV, _R_BQ).astype(BF16)
    bq_ref[...] = (mm(_R_BQ, _R_BK) * (B_KEY_DIM ** -0.5)).astype(BF16)
    bk_ref[...] = mm(_R_BK, _R_BV).astype(BF16)
    bv_ref[...] = mm(_R_BV, _R_ALPHA).astype(BF16)
    r = mm(_R_BR, _R_GATE)
    br_ref[...] = (r * jax.nn.sigmoid(r)).astype(BF16)
    for c in range(_R_GATE, _R_END, 512):
        gate_ref[:, c - _R_GATE:c - _R_GATE + 512] = jax.nn.sigmoid(mm(c, c + 512)).astype(BF16)
    z = _dot(mm(_R_ALPHA, _R_ALPHA + LANES).astype(BF16), wal2_bf[...]) + bal_ref[...]
    lga_ref[...] = (jnp.minimum(z, 0.0) - jnp.log(1.0 + jnp.exp(-jnp.abs(z)))) * (1.0 / B_GATE_TAU)


def _inproj(x, g, w_in, w_al2, b_al, layer, moe=None):
    t, d = x.shape
    assert w_in.shape[2] == _R_END
    row = lambda w: pl.BlockSpec((INPROJ_ROWS, w), lambda i: (i, 0))
    full = lambda a: pl.BlockSpec(a.shape, lambda i: (0,) * a.ndim)
    sds = lambda w, dt: jax.ShapeDtypeStruct((t, w), dt)
    widths = [(512, BF16), (512, BF16), (512, BF16), (256, BF16), (256, BF16), (512, BF16),
              (256, F32), (512, BF16), (2048, BF16)]
    fused = moe is not None
    moe_specs, moe_out_specs, moe_out_shape = [], [], []
    if fused:
        n_chunks = INPROJ_ROWS // ROW_TILE * LOCAL_CHUNKS
        moe_specs = [row(LANES), pl.BlockSpec((n_chunks, CHUNK_ROWS, d), lambda i: (i, 0, 0))]
        moe_out_specs, moe_out_shape = [row(d)], [sds(d, F32)]
    return pl.pallas_call(
        functools.partial(_inproj_kernel, layer, fused),
        grid=(t // INPROJ_ROWS,),
        in_specs=[row(d)] + moe_specs + [full(g), pl.BlockSpec(memory_space=pl.ANY),
                                         pl.BlockSpec((1,) + w_al2.shape[1:], lambda i: (layer, 0, 0)),
                                         full(b_al)],
        out_specs=moe_out_specs + [row(w) for w, _ in widths],
        out_shape=moe_out_shape + [sds(w, dt) for w, dt in widths],
        scratch_shapes=[pltpu.VMEM((_R_END, d), BF16), pltpu.VMEM((LANES, B_KEY_WIDTH), BF16),
                        pltpu.VMEM((2, _W_PIECE, d), F32), pltpu.SemaphoreType.DMA((2,))],
        compiler_params=_params("arbitrary"),
        name="inproj",
    )(x, *(moe or ()), g, jnp.swapaxes(w_in, 1, 2), w_al2, b_al)


def _band_kernel(q_ref, k0_ref, k1_ref, k2_ref, v0_ref, v1_ref, v2_ref, u_ref, o_ref, bias_ref):
    i = pl.program_id(1)
    lane = lax.broadcasted_iota(I32, (1, LANES), 1)
    low = lane < A_HEAD_DIM
    ones = jnp.ones((BAND_KEYS, LANES), BF16)

    @pl.when((pl.program_id(0) == 0) & (i == 0))
    def _():
        cq = lax.broadcasted_iota(I32, (ROW_TILE, BAND_KEYS), 0) >> LOG_CHUNK
        ck = lax.broadcasted_iota(I32, (ROW_TILE, BAND_KEYS), 1) >> LOG_CHUNK
        valid = (ck >= cq) & (ck <= cq + A_LEFT_CHUNKS)
        for h in range(A_HEADS):
            rows = jnp.broadcast_to(u_ref[h:h + 1, :], (ROW_TILE, BIAS_PERIOD))
            rows = pltpu.roll(rows, BIAS_PERIOD - (ROW_TILE - 1), 1, stride=1, stride_axis=0)
            bias_ref[h // 2, (h % 2) * ROW_TILE:(h % 2 + 1) * ROW_TILE, :] = jnp.where(
                valid, rows[:, :BAND_KEYS], NEG)

    def attend(pen):
        for p in range(A_HEADS // 2):
            sl = slice(p * LANES, (p + 1) * LANES)
            qp = q_ref[:, sl]
            zero = jnp.zeros_like(qp)
            q2 = jnp.concatenate([jnp.where(low, qp, zero), jnp.where(low, zero, qp)], axis=0)
            kp = jnp.concatenate([k0_ref[:, sl], k1_ref[:, sl], k2_ref[:, sl]], axis=0)
            vp = jnp.concatenate([v0_ref[:, sl], v1_ref[:, sl], v2_ref[:, sl]], axis=0)
            s = _dot_nt(q2, kp) + bias_ref[p]
            if pen is not None:
                s = s + pen
            pe = jnp.exp2(s - jnp.max(s, axis=-1, keepdims=True)).astype(BF16)
            o2 = _dot(pe, jnp.concatenate([vp, ones], axis=1))
            o = o2[:, :LANES] * (1.0 / o2[:, LANES:])
            o_ref[:, sl] = jnp.where(low, o[:ROW_TILE], o[ROW_TILE:]).astype(BF16)

    @pl.when(i >= BAND_TILES - 1)
    def _():
        attend(None)

    @pl.when(i < BAND_TILES - 1)
    def _():
        col = lax.broadcasted_iota(I32, (1, BAND_KEYS), 1)
        attend(jnp.where(col < (BAND_TILES - 1 - i) * ROW_TILE, NEG, 0.0).astype(F32))


def _band_bias_vector(rel_table):
    h = rel_table.shape[0]
    tab = rel_table.astype(F32) * LOG2E
    shift = A_LEFT_CHUNKS * CHUNK + ROW_TILE - 1
    n_far = shift - A_MAX_REL + 1
    span = ROW_TILE + BAND_KEYS - 1
    assert span - 1 - shift <= A_MAX_REL and span <= BIAS_PERIOD
    u = jnp.concatenate([jnp.broadcast_to(tab[:, 2 * A_MAX_REL:], (h, n_far)),
                         tab[:, 2 * A_MAX_REL - 1:2 * A_MAX_REL - 1 - (span - n_far):-1]], axis=1)
    return jnp.pad(u, ((0, 0), (0, BIAS_PERIOD - span)))


def _band_attention(q, k, v, u, batch):
    t, w = q.shape
    nb = t // batch // ROW_TILE
    cur = lambda b, i: (b * nb + i, 0)
    back = lambda n: (lambda b, i: (b * nb + jnp.maximum(i - n, 0), 0))
    blk = lambda f: pl.BlockSpec((ROW_TILE, w), f)
    return pl.pallas_call(
        _band_kernel,
        grid=(batch, nb),
        in_specs=[blk(cur), blk(back(2)), blk(back(1)), blk(cur), blk(back(2)), blk(back(1)), blk(cur),
                  pl.BlockSpec(u.shape, lambda b, i: (0, 0))],
        out_specs=blk(cur),
        out_shape=jax.ShapeDtypeStruct((t, w), BF16),
        scratch_shapes=[pltpu.VMEM((A_HEADS // 2, 2 * ROW_TILE, BAND_KEYS), F32)],
        compiler_params=_params("arbitrary", "arbitrary"),
        name="band_attn",
    )(q, k, k, k, v, v, v, u)


def _gla_constants():
    c = CHUNK
    t = np.arange(c)[:, None]
    r = np.arange(c)[None, :]
    mats = [(r <= t), (r > t)]
    lvl = np.full((c, c), -1, np.int32)
    lvl[np.arange(c), np.arange(c)] = N_LEVELS
    for l in range(N_LEVELS):
        m = (c // 2) >> l
        mid = (t // (2 * m)) * (2 * m) + m
        upper = t >= mid
        mats.append(np.where(upper, (r >= mid) & (r <= t), (r > t) & (r < mid)))
        s = r
        same = (s // (2 * m)) == (t // (2 * m))
        lvl[np.asarray(same & upper & (s < mid))] = l
    eye = np.eye(CHUNKS_PER_TILE)
    mexp = np.concatenate([np.kron(eye, m) for m in mats], axis=0).astype(np.float32)
    lvl = np.tile(lvl, (1, B_HEADS))
    return jnp.asarray(mexp, BF16), jnp.asarray(lvl, I32)


def _gla_kernel(q_ref, k_ref, v_ref, g_ref, r_ref, gn_ref, mexp_ref, lvl_ref, o_ref, s_ref):
    @pl.when(pl.program_id(1) == 0)
    def _():
        s_ref[...] = jnp.zeros_like(s_ref)

    kw = B_KEY_WIDTH
    ri = lax.broadcasted_iota(I32, (kw, kw), 0) >> LOG_CHUNK
    ci = lax.broadcasted_iota(I32, (kw, kw), 1) >> LOG_CHUNK
    bd = ri == ci
    head_ind = jnp.where(bd, 1.0, 0.0).astype(BF16)
    ri2 = lax.broadcasted_iota(I32, (kw, 2 * kw), 0) >> LOG_CHUNK
    ci2 = (lax.broadcasted_iota(I32, (kw, 2 * kw), 1) & (kw - 1)) >> LOG_CHUNK
    bd2 = ri2 == ci2
    lvl = lvl_ref[...]
    row8 = lax.broadcasted_iota(I32, (16, kw), 0)
    ones = jnp.ones((16, LANES), BF16)
    zero_b = jnp.zeros((kw, kw), BF16)
    chunks = [slice(c * CHUNK, (c + 1) * CHUNK) for c in range(CHUNKS_PER_TILE)]

    def head_blocks(x):
        return jnp.where(bd, jnp.concatenate([x] * B_HEADS, axis=0), zero_b)

    q = q_ref[...].astype(F32)
    k = k_ref[...].astype(F32)
    g = g_ref[...]
    gb = g.astype(BF16)
    half = EXP_ROWS * CHUNKS_PER_TILE // 2
    w = jnp.exp(jnp.concatenate([_dot(mexp_ref[:half, :], gb), _dot(mexp_ref[half:, :], gb)], axis=0))
    w_cum = w[0:ROW_TILE]
    qt = (q * w_cum).astype(BF16)
    kb = (k * w[ROW_TILE:2 * ROW_TILE]).astype(BF16)
    qk = (q * k).astype(BF16)

    attn = [jnp.zeros((CHUNK, kw), F32) for _ in chunks]
    for l in range(N_LEVELS):
        wl = w[(2 + l) * ROW_TILE:(3 + l) * ROW_TILE]
        qh = (q * wl).astype(BF16)
        kh = (k * wl).astype(BF16)
        for c, rows in enumerate(chunks):
            attn[c] = jnp.where(lvl == l, _dot_nt(qh[rows], head_blocks(kh[rows])), attn[c])

    vstacks, kvs, dcols = [], [], []
    for c, rows in enumerate(chunks):
        attn[c] = jnp.where(lvl == N_LEVELS, _dot(qk[rows], head_ind), attn[c])
        v = v_ref[rows, :]
        vstack = jnp.concatenate([v[:, j * LANES:(j + 1) * LANES] for j in range(B_HEADS)], axis=0)
        vstacks.append(vstack)
        kvs.append(_dot_tn(head_blocks(kb[rows]), vstack))
        d = jnp.exp(jnp.sum(g[rows], axis=0, keepdims=True))
        d1 = d.astype(BF16).astype(F32)
        dp = jnp.where(row8 == 0, d1, jnp.where(row8 == 1, d - d1, 0.0)).astype(BF16)
        dcols.append(_dot_tn(dp, ones))

    s = s_ref[...]
    for c, rows in enumerate(chunks):
        lhs = jnp.concatenate([attn[c].astype(BF16), qt[rows]], axis=1)
        lhs = jnp.where(bd2, jnp.concatenate([lhs] * B_HEADS, axis=0), jnp.zeros((kw, 2 * kw), BF16))
        rhs = jnp.concatenate([vstacks[c], s.astype(BF16)], axis=0)
        o = _dot(lhs, rhs)
        s = dcols[c] * s + kvs[c]
        for j in range(B_HEADS):
            oj = o[j * CHUNK:(j + 1) * CHUNK]
            sl = slice(j * LANES, (j + 1) * LANES)
            y = oj * lax.rsqrt(jnp.mean(oj * oj, axis=-1, keepdims=True) + EPS) * gn_ref[...]
            o_ref[rows, sl] = (y * r_ref[rows, sl].astype(F32)).astype(BF16)
    s_ref[...] = s


def _gla(q, k, v, g, r, gn, batch):
    t = q.shape[0]
    nb = t // batch // ROW_TILE
    mexp, lvl = _gla_constants()
    cur = lambda b, i: (b * nb + i, 0)
    blk = lambda w: pl.BlockSpec((ROW_TILE, w), cur)
    full = lambda a: pl.BlockSpec(a.shape, lambda b, i: (0,) * a.ndim)
    return pl.pallas_call(
        _gla_kernel,
        grid=(batch, nb),
        in_specs=[blk(B_KEY_WIDTH), blk(B_KEY_WIDTH), blk(B_VAL_WIDTH), blk(B_KEY_WIDTH), blk(B_VAL_WIDTH),
                  full(gn), full(mexp), full(lvl)],
        out_specs=blk(B_VAL_WIDTH),
        out_shape=jax.ShapeDtypeStruct((t, B_VAL_WIDTH), BF16),
        scratch_shapes=[pltpu.VMEM((B_KEY_WIDTH, B_VAL_DIM), F32)],
        compiler_params=_params("arbitrary", "arbitrary"),
        name="gla",
    )(q, k, v, g, r, gn, mexp, lvl)


def _token_kernel(x_ref, oa_ref, ob_ref, gate_ref, wb0_ref, wb1_ref, wmix_ref, gx_ref, wq_ref,
                  km_ref, vm_ref, wo_ref, gf_ref, wr_ref, br_ref, ltri_ref, utri_ref,
                  x2_ref, hs_ref, route_ref, cnt_ref):
    ma = _dot(oa_ref[...], wb0_ref[...])
    mb = _dot(ob_ref[...], wb1_ref[...])
    merged = (gate_ref[:, :D_MODEL].astype(F32) * ma + gate_ref[:, D_MODEL:].astype(F32) * mb).astype(BF16)
    x1 = x_ref[...] + _dot(merged, wmix_ref[...])

    h2 = _rms(x1, gx_ref[...]).astype(BF16)
    qx = (_dot(h2, wq_ref[...]) * (X_HEAD_DIM ** -0.5)).astype(BF16)
    heads = []
    for h in range(X_HEADS):
        sl = slice(h * X_HEAD_DIM, (h + 1) * X_HEAD_DIM)
        s = _dot_nt(qx[:, sl], km_ref[0, :, sl])
        m = jnp.max(s, axis=-1, keepdims=True)
        pe = jnp.exp(s - m)
        l = jnp.sum(pe, axis=-1, keepdims=True)
        heads.append((_dot(pe.astype(BF16), vm_ref[0, :, sl]) * (1.0 / l)).astype(BF16))
    x2 = x1 + _dot(jnp.concatenate(heads, axis=1), wo_ref[...])
    x2_ref[...] = x2

    h3 = _rms(x2, gf_ref[...])

    h3_hi = h3.astype(BF16)
    h3_lo = (h3 - h3_hi.astype(F32)).astype(BF16)
    hw = _dot(h3_hi, wr_ref[...])
    logits = hw[:, :LANES] + hw[:, LANES:] + _dot(h3_lo, wr_ref[:, :LANES]) + br_ref[...]
    for h in range(TOKEN_TILES_PER_STEP):
        rows = slice(h * ROW_TILE, (h + 1) * ROW_TILE)
        chunks = pl.ds(h * LOCAL_CHUNKS, LOCAL_CHUNKS)
        _route_and_sort(logits[rows], h3_hi[rows], ltri_ref, utri_ref, hs_ref.at[chunks],
                        route_ref.at[pl.ds(h * ROW_TILE, ROW_TILE)], cnt_ref.at[h])


def _route_and_sort(logits, h3_hi, ltri_ref, utri_ref, hs_ref, route_ref, cnt_ref):
    lane = lax.broadcasted_iota(I32, logits.shape, 1).astype(F32)
    big = jnp.float32(LANES)
    gl = jnp.where(lane < N_GROUPS, logits, NEG)
    gmax = jnp.max(gl, axis=-1, keepdims=True)
    gidx = jnp.min(jnp.where(gl == gmax, lane, big), axis=-1, keepdims=True)
    g_w = 1.0 / jnp.sum(jnp.exp(gl - gmax), axis=-1, keepdims=True)
    lo = N_GROUPS + EXPERTS_PER_GROUP * gidx
    el = jnp.where((lane >= lo) & (lane < lo + EXPERTS_PER_GROUP), logits, NEG)
    v1 = jnp.max(el, axis=-1, keepdims=True)
    i1 = jnp.min(jnp.where(el == v1, lane, big), axis=-1, keepdims=True)
    el2 = jnp.where(lane == i1, NEG, el)
    v2 = jnp.max(el2, axis=-1, keepdims=True)
    i2 = jnp.min(jnp.where(el2 == v2, lane, big), axis=-1, keepdims=True)
    e21 = jnp.exp(v2 - v1)
    w1 = g_w / (1.0 + e21)
    w2 = w1 * e21
    oh0 = jnp.where(lane == i1 - N_GROUPS, 1.0, 0.0)
    oh1 = jnp.where(lane == i2 - N_GROUPS, 1.0, 0.0)
    oh = oh0 + oh1
    nch = jnp.floor((jnp.sum(oh, axis=0, keepdims=True) + (CHUNK_ROWS - 1)) * (1.0 / CHUNK_ROWS))
    nch8 = jnp.broadcast_to(nch, (8, LANES))
    start = _dot(nch8.astype(BF16), utri_ref[...])[0:1] * CHUNK_ROWS
    rank = _dot(ltri_ref[...], oh.astype(BF16))
    row = start + rank
    pos0 = jnp.sum(row * oh0, axis=-1, keepdims=True)
    pos1 = jnp.sum(row * oh1, axis=-1, keepdims=True)
    route = jnp.where(lane == 0, pos0, jnp.where(lane == 1, pos1, 0.0))
    route_t = jnp.transpose(route)
    r = lax.broadcasted_iota(I32, (LOCAL_ROWS, ROW_TILE), 0).astype(F32)
    p0 = jnp.where(r == route_t[0:1, :], 1.0, 0.0).astype(BF16)
    p1 = jnp.where(r == route_t[1:2, :], 1.0, 0.0).astype(BF16)

    def gate_cols(w):
        hi = w.astype(BF16).astype(F32)
        return jnp.where(lane == 0, hi, jnp.where(lane == 1, w - hi, 0.0)).astype(BF16)

    sorted_rows = jnp.concatenate([_dot(p0 + p1, h3_hi), _dot(p0, gate_cols(w1)) + _dot(p1, gate_cols(w2))],
                                  axis=1)
    hs_ref[...] = sorted_rows.astype(BF16).reshape(hs_ref.shape)
    route_ref[...] = route
    cnt_ref[...] = nch8


def _token(x, oa, ob, gates, wb0, wb1, wmix, gx, wq, km, vm, wo, gf, wr, br, batch):
    t, d = x.shape
    n = TOKEN_TILES_PER_STEP
    nb = t // batch // (n * ROW_TILE)
    nt = t // ROW_TILE
    ltri = jnp.asarray(np.tril(np.ones((ROW_TILE, ROW_TILE), np.float32), -1), BF16)
    utri = jnp.asarray(np.triu(np.ones((LANES, LANES), np.float32), 1), BF16)
    cur = lambda b, i: (b * nb + i, 0)
    cur3 = lambda b, i: (b * nb + i, 0, 0)
    blk = lambda w: pl.BlockSpec((n * ROW_TILE, w), cur)
    full = lambda a: pl.BlockSpec(a.shape, lambda b, i: (0,) * a.ndim)
    mem = pl.BlockSpec((1,) + km.shape[1:], lambda b, i: (b, 0, 0))
    return pl.pallas_call(
        _token_kernel,
        grid=(batch, nb),
        in_specs=[blk(d), blk(A_WIDTH), blk(B_VAL_WIDTH), blk(2 * d), full(wb0), full(wb1), full(wmix),
                  full(gx), full(wq), mem, mem, full(wo), full(gf), full(wr), full(br), full(ltri), full(utri)],
        out_specs=[blk(d), pl.BlockSpec((n * LOCAL_CHUNKS, CHUNK_ROWS, SORT_WIDTH), cur3),
                   blk(LANES), pl.BlockSpec((n, 8, LANES), cur3)],
        out_shape=[jax.ShapeDtypeStruct((t, d), F32),
                   jax.ShapeDtypeStruct((nt * LOCAL_CHUNKS, CHUNK_ROWS, SORT_WIDTH), BF16),
                   jax.ShapeDtypeStruct((t, LANES), F32),
                   jax.ShapeDtypeStruct((nt, 8, LANES), F32)],
        compiler_params=_params("arbitrary", "arbitrary"),
        name="token",
    )(x, oa, ob, gates, wb0, wb1, wmix, gx, wq, km, vm, wo, gf, wr, br, ltri, utri)


def _expert_kernel(te_ref, nu_ref, nv_ref, ch_ref, hs_hbm, wg_ref, wu_ref, wd_ref, ys_hbm,
                   xbuf, ybuf, wg_bf, wu_bf, wd_bf, gsem, ssem):
    i = pl.program_id(0)
    n_used = nu_ref[0]
    slot = lax.rem(i, 2)

    def for_chunks(tile, fn):
        nv = nv_ref[tile]

        @pl.when(nv == TILE_CHUNKS)
        def _():
            for c in range(TILE_CHUNKS):
                fn(c)

        @pl.when(nv != TILE_CHUNKS)
        def _():
            def body(c, carry):
                fn(c)
                return carry

            lax.fori_loop(0, nv, body, 0)

    def gather(tile, s, start):
        for c in range(TILE_CHUNKS):
            cp = pltpu.make_async_copy(hs_hbm.at[ch_ref[tile * TILE_CHUNKS + c]], xbuf.at[s, c], gsem.at[s])
            cp.start() if start else cp.wait()

    def scatter(tile, s, start):
        def one(c):
            cp = pltpu.make_async_copy(ybuf.at[s, c], ys_hbm.at[ch_ref[tile * TILE_CHUNKS + c]], ssem.at[s])
            cp.start() if start else cp.wait()

        for_chunks(tile, one)

    next_tile = jnp.minimum(i + 1, pl.num_programs(0) - 1)

    @pl.when(i == 0)
    def _():
        gather(0, 0, True)

    @pl.when(i < n_used)
    def _():
        gather(next_tile, 1 - slot, True)
        gather(i, slot, False)

        @pl.when(i >= 2)
        def _():
            scatter(i - 2, slot, False)

        @pl.when((i == 0) | (te_ref[i] != te_ref[jnp.maximum(i - 1, 0)]))
        def _():
            wg_bf[...] = wg_ref[0].astype(BF16)
            wu_bf[...] = wu_ref[0].astype(BF16)
            wd_bf[...] = wd_ref[0].astype(BF16)

        xg = xbuf[slot].reshape(EXPERT_ROWS, SORT_WIDTH)
        x = xg[:, :D_MODEL]
        hg = _dot(x, wg_bf[...])
        hu = _dot(x, wu_bf[...])
        hid = (hg * jax.nn.sigmoid(hg) * hu).astype(BF16)
        g = xg[:, D_MODEL:].astype(F32)
        y = ((g[:, 0:1] + g[:, 1:2]) * _dot(hid, wd_bf[...])).astype(BF16)
        y = jnp.concatenate([y, jnp.zeros((EXPERT_ROWS, LANES), BF16)], axis=1)
        ybuf[slot] = y.reshape(TILE_CHUNKS, CHUNK_ROWS, SORT_WIDTH)
        scatter(i, slot, True)

        @pl.when(i == n_used - 1)
        def _():
            gather(next_tile, 1 - slot, False)
            scatter(i, slot, False)

            @pl.when(i >= 1)
            def _():
                scatter(i - 1, 1 - slot, False)


def _experts(hs, tile_expert, n_used, n_valid, chunks, wg, wu, wd, layer):
    n_tiles = tile_expert.shape[0]
    last = lambda i, te, nu, nv, ch: jnp.minimum(i, nu[0] - 1)
    wmap = lambda i, te, nu, nv, ch: (layer * N_EXPERTS + te[last(i, te, nu, nv, ch)], 0, 0)
    anyspace = pl.BlockSpec(memory_space=pl.ANY)
    grid_spec = pltpu.PrefetchScalarGridSpec(
        num_scalar_prefetch=4,
        grid=(n_tiles,),
        in_specs=[anyspace,
                  pl.BlockSpec((1, D_MODEL, EXPERT_FF), wmap),
                  pl.BlockSpec((1, D_MODEL, EXPERT_FF), wmap),
                  pl.BlockSpec((1, EXPERT_FF, D_MODEL), wmap)],
        out_specs=anyspace,
        scratch_shapes=[pltpu.VMEM((2, TILE_CHUNKS, CHUNK_ROWS, SORT_WIDTH), BF16),
                        pltpu.VMEM((2, TILE_CHUNKS, CHUNK_ROWS, SORT_WIDTH), BF16),
                        pltpu.VMEM((D_MODEL, EXPERT_FF), BF16), pltpu.VMEM((D_MODEL, EXPERT_FF), BF16),
                        pltpu.VMEM((EXPERT_FF, D_MODEL), BF16),
                        pltpu.SemaphoreType.DMA((2,)), pltpu.SemaphoreType.DMA((2,))],
    )
    return pl.pallas_call(
        _expert_kernel,
        grid_spec=grid_spec,
        out_shape=jax.ShapeDtypeStruct(hs.shape, BF16),
        input_output_aliases={4: 0},
        compiler_params=_params("arbitrary"),
        name="experts",
    )(tile_expert, n_used, n_valid, chunks, hs, wg, wu, wd)


def _combine_kernel(x_ref, route_ref, ys_ref, gfin_ref, o_ref):
    r = lax.broadcasted_iota(I32, (ROW_TILE, LOCAL_ROWS), 1).astype(F32)
    for t in range(COMBINE_TILES_PER_STEP):
        rows = slice(t * ROW_TILE, (t + 1) * ROW_TILE)
        pos = route_ref[rows, :]
        sel = jnp.where(r == pos[:, 0:1], 1.0, jnp.where(r == pos[:, 1:2], 1.0, 0.0)).astype(BF16)
        ys = ys_ref[t * LOCAL_CHUNKS:(t + 1) * LOCAL_CHUNKS].reshape(LOCAL_ROWS, D_MODEL)
        o_ref[rows, :] = _rms(x_ref[rows, :] + _dot(sel, ys), gfin_ref[...])


def _combine(x2, route, ys, gfin):
    t, d = x2.shape
    n = COMBINE_TILES_PER_STEP
    return pl.pallas_call(
        _combine_kernel,
        grid=(t // (n * ROW_TILE),),
        in_specs=[pl.BlockSpec((n * ROW_TILE, d), lambda i: (i, 0)),
                  pl.BlockSpec((n * ROW_TILE, LANES), lambda i: (i, 0)),
                  pl.BlockSpec((n * LOCAL_CHUNKS, CHUNK_ROWS, d), lambda i: (i, 0, 0)),
                  pl.BlockSpec((1, d), lambda i: (0, 0))],
        out_specs=pl.BlockSpec((n * ROW_TILE, d), lambda i: (i, 0)),
        out_shape=jax.ShapeDtypeStruct((t, d), F32),
        compiler_params=_params("arbitrary"),
        name="combine",
    )(x2, route, ys, gfin)


def _chunk_plan(nch, n_tiles):
    nt = nch.shape[0]
    local_start = jnp.cumsum(nch, axis=1) - nch
    cum = jnp.cumsum(nch, axis=0)
    total = cum[-1]
    tiles = (total + TILE_CHUNKS - 1) // TILE_CHUNKS
    tile_end = jnp.cumsum(tiles)
    n_used = tile_end[-1:]
    tile_ids = jnp.arange(n_tiles, dtype=I32)
    tile_expert = jnp.minimum(jnp.sum((tile_end[None, :] <= tile_ids[:, None]).astype(I32), axis=1),
                              N_EXPERTS - 1)
    sel = (tile_expert[:, None] == jnp.arange(N_EXPERTS, dtype=I32)[None, :]).astype(I32)
    pick = lambda table: jnp.sum(sel[:, :, None] * table.T[None, :, :], axis=1)
    first_tile = jnp.sum(sel * (tile_end - tiles)[None, :], axis=1)
    slot = (tile_ids - first_tile)[:, None] * TILE_CHUNKS + jnp.arange(TILE_CHUNKS, dtype=I32)[None, :]
    valid = (slot < jnp.sum(sel * total[None, :], axis=1)[:, None]) & (tile_ids < n_used)[:, None]
    src_tile = jnp.sum((pick(cum)[:, None, :] <= slot[:, :, None]).astype(I32), axis=2)
    src_tile = jnp.minimum(src_tile, nt - 1)
    at = (src_tile[:, :, None] == jnp.arange(nt, dtype=I32)[None, None, :]).astype(I32)
    before = jnp.sum(at * pick(cum - nch)[:, None, :], axis=2)
    start = jnp.sum(at * pick(local_start)[:, None, :], axis=2)
    chunk = jnp.where(valid, src_tile * LOCAL_CHUNKS + start + slot - before, LOCAL_CHUNKS - 1)
    return tile_expert, n_used, jnp.sum(valid.astype(I32), axis=1), chunk.reshape(-1)


def kernel(x, mem, norm_mix_g, w_in, rel_bias, gla_w_alpha, gla_b_alpha, gla_norm_g, w_branch, w_mix_out, norm_x_g, mem_norm_g, w_xq, w_xkv, w_xo, norm_ffn_g, w_group_router, b_group_router, w_expert_router, b_expert_router, w_exp_gate, w_exp_up, w_exp_down, final_norm_g):
    batch, seq, d = x.shape
    depth = w_in.shape[0]
    t = batch * seq
    assert d == D_MODEL and seq % (TOKEN_TILES_PER_STEP * ROW_TILE) == 0
    nt = t // ROW_TILE
    n_tiles = nt * LOCAL_CHUNKS // TILE_CHUNKS + N_EXPERTS

    xf = x.reshape(t, d)
    km_all, vm_all = _memkv(mem, mem_norm_g, w_xkv.astype(BF16))
    row = lambda a: a.reshape(1, -1).astype(F32)

    moe = None
    for l in range(depth):
        res = _inproj(xf, row(norm_mix_g[l]), w_in, gla_w_alpha, row(gla_b_alpha[l]), l, moe)
        if moe is not None:
            xf, res = res[0], res[1:]
        aq, ak, av, bq, bk, bv, lga, br, gates = res

        oa = _band_attention(aq, ak, av, _band_bias_vector(rel_bias[l]), batch)
        ob = _gla(bq, bk, bv, lga, br, row(gla_norm_g[l]), batch)

        wr = jnp.pad(jnp.concatenate([w_group_router[l], w_expert_router[l]], axis=1).astype(F32),
                     ((0, 0), (0, LANES - N_GROUPS - N_EXPERTS)))
        wr_hi = wr.astype(BF16)
        wr = jnp.concatenate([wr_hi, (wr - wr_hi.astype(F32)).astype(BF16)], axis=1)
        brt = jnp.pad(jnp.concatenate([b_group_router[l], b_expert_router[l]]).astype(F32),
                      (0, LANES - N_GROUPS - N_EXPERTS)).reshape(1, LANES)
        x2, hs, route, cnt = _token(
            xf, oa, ob, gates, w_branch[l, 0].astype(BF16), w_branch[l, 1].astype(BF16),
            w_mix_out[l].astype(BF16), row(norm_x_g[l]), w_xq[l].astype(BF16), km_all[l], vm_all[l],
            w_xo[l].astype(BF16), row(norm_ffn_g[l]), wr, brt, batch)

        plan = _chunk_plan(cnt[:, 0, :N_EXPERTS].astype(I32), n_tiles)
        e3 = lambda w: w.reshape((depth * N_EXPERTS,) + w.shape[3:])
        ys = _experts(hs, *plan, e3(w_exp_gate), e3(w_exp_up), e3(w_exp_down), l)
        xf, moe = x2, (route, ys)

    return _combine(x2, route, ys, row(final_norm_g)).reshape(batch, seq, d)
```

```python
import functools

import numpy as np
import jax
import jax.numpy as jnp
from jax import lax
from jax.experimental import pallas as pl
from jax.experimental.pallas import tpu as pltpu

F32 = jnp.float32
BF16 = jnp.bfloat16
I32 = jnp.int32

D_MODEL = 1024
CHUNK = 64
EPS = 1e-6
A_HEADS = 8
A_HEAD_DIM = 64
A_WIDTH = 512
A_LEFT_CHUNKS = 8
A_MAX_REL = 256
B_HEADS = 4
B_KEY_DIM = 64
B_VAL_DIM = 128
B_KEY_WIDTH = 256
B_VAL_WIDTH = 512
B_GATE_RANK = 16
B_GATE_TAU = 16.0
X_HEADS = 4
X_HEAD_DIM = 256
N_GROUPS = 4
EXPERTS_PER_GROUP = 8
N_EXPERTS = N_GROUPS * EXPERTS_PER_GROUP
EXPERT_FF = 256

LANES = 128
ROW_TILE = 256
CHUNKS_PER_TILE = ROW_TILE // CHUNK
BAND_TILES = A_LEFT_CHUNKS // CHUNKS_PER_TILE + 1
BAND_KEYS = BAND_TILES * ROW_TILE
BIAS_PERIOD = 1024
LOG_CHUNK = 6
N_LEVELS = LOG_CHUNK
EXP_ROWS = (2 + N_LEVELS) * CHUNK
CHUNK_ROWS = 16
TOKEN_TILES_PER_STEP = 2
INPROJ_ROWS = 512
COMBINE_TILES_PER_STEP = 2
EXPERT_ROWS = 512
TILE_CHUNKS = EXPERT_ROWS // CHUNK_ROWS
LOCAL_CHUNKS = 2 * ROW_TILE // CHUNK_ROWS + N_EXPERTS
LOCAL_ROWS = LOCAL_CHUNKS * CHUNK_ROWS
assert (2 * ROW_TILE + N_EXPERTS * (CHUNK_ROWS - 1)) // CHUNK_ROWS < LOCAL_CHUNKS
SORT_WIDTH = D_MODEL + LANES
NEG = -1e30
LOG2E = 1.4426950408889634
VMEM_LIMIT = 56 * 1024 * 1024


def _params(*sem):
    return pltpu.CompilerParams(dimension_semantics=sem, vmem_limit_bytes=VMEM_LIMIT)


def _rms(x, g):
    return x * lax.rsqrt(jnp.mean(x * x, axis=-1, keepdims=True) + EPS) * g


def _dot(a, b):
    return jnp.dot(a, b, preferred_element_type=F32)


def _dot_nt(a, b):
    return lax.dot_general(a, b, (((1,), (1,)), ((), ())), preferred_element_type=F32)


def _dot_tn(a, b):
    return lax.dot_general(a, b, (((0,), (0,)), ((), ())), preferred_element_type=F32)


def _memkv_kernel(mem_ref, g_ref, w_ref, k_ref, v_ref):
    mn = _rms(mem_ref[0], g_ref[...]).astype(BF16)
    kv = _dot(mn, w_ref[0])
    k_ref[0, 0] = kv[:, :D_MODEL].astype(BF16)
    v_ref[0, 0] = kv[:, D_MODEL:].astype(BF16)


def _memkv(mem, g, w_xkv):
    depth = w_xkv.shape[0]
    b, m, d = mem.shape
    out = jax.ShapeDtypeStruct((depth, b, m, d), BF16)
    return pl.pallas_call(
        _memkv_kernel,
        grid=(depth, b),
        in_specs=[pl.BlockSpec((1, m, d), lambda l, i: (i, 0, 0)),
                  pl.BlockSpec((1, d), lambda l, i: (0, 0)),
                  pl.BlockSpec((1, d, 2 * d), lambda l, i: (l, 0, 0))],
        out_specs=[pl.BlockSpec((1, 1, m, d), lambda l, i: (l, i, 0, 0)),
                   pl.BlockSpec((1, 1, m, d), lambda l, i: (l, i, 0, 0))],
        out_shape=[out, out],
        compiler_params=_params("arbitrary", "arbitrary"),
        name="memkv",
    )(mem, g.reshape(1, d), w_xkv)


_R_AQ, _R_AK, _R_AV = 0, 512, 1024
_R_BQ, _R_BK, _R_BV = 1536, 1792, 2048
_R_ALPHA, _R_BR, _R_GATE, _R_END = 2560, 2576, 3088, 5136
_W_PIECE = 512


def _inproj_kernel(layer, fused, *refs):
    if fused:
        x_ref, route_ref, ys_ref, g_ref, wt_hbm, wal2_ref, bal_ref = refs[:7]
        refs = refs[7:]
        x3_ref, refs = refs[0], refs[1:]
    else:
        x_ref, g_ref, wt_hbm, wal2_ref, bal_ref = refs[:5]
        refs = refs[5:]
    (aq_ref, ak_ref, av_ref, bq_ref, bk_ref, bv_ref, lga_ref, br_ref, gate_ref,
     wt_bf, wal2_bf, stage, sem) = refs

    @pl.when(pl.program_id(0) == 0)
    def _():
        pieces = [(c, min(_W_PIECE, _R_END - c)) for c in range(0, _R_END, _W_PIECE)]

        def piece_copy(p):
            c, n = pieces[p]
            return pltpu.make_async_copy(wt_hbm.at[layer, pl.ds(c, n), :], stage.at[p % 2, pl.ds(0, n), :],
                                         sem.at[p % 2])

        piece_copy(0).start()
        for p, (c, n) in enumerate(pieces):
            if p + 1 < len(pieces):
                piece_copy(p + 1).start()
            piece_copy(p).wait()
            wt_bf[c:c + n, :] = stage[p % 2, 0:n, :].astype(BF16)
        wal2_bf[...] = jnp.concatenate(
            [wal2_ref[0].astype(BF16), jnp.zeros((LANES - B_GATE_RANK, B_KEY_WIDTH), BF16)], axis=0)

    if fused:
        r = lax.broadcasted_iota(I32, (ROW_TILE, LOCAL_ROWS), 1).astype(F32)
        tiles = []
        for t in range(INPROJ_ROWS // ROW_TILE):
            rows = slice(t * ROW_TILE, (t + 1) * ROW_TILE)
            pos = route_ref[rows, :]
            sel = jnp.where(r == pos[:, 0:1], 1.0, jnp.where(r == pos[:, 1:2], 1.0, 0.0)).astype(BF16)
            ys = ys_ref[t * LOCAL_CHUNKS:(t + 1) * LOCAL_CHUNKS].reshape(LOCAL_ROWS, D_MODEL)
            tiles.append(x_ref[rows, :] + _dot(sel, ys))
        x = jnp.concatenate(tiles, axis=0)
        x3_ref[...] = x
    else:
        x = x_ref[...]
    h = _rms(x, g_ref[...]).astype(BF16)

    def mm(lo, hi):
        return _dot_nt(h, wt_bf[lo:hi, :])

    aq_ref[...] = (mm(_R_AQ, _R_AK) * (A_HEAD_DIM ** -0.5 * LOG2E)).astype(BF16)
    ak_ref[...] = mm(_R_AK, _R_AV).astype(BF16)
    av_ref[...] = mm(_R_AV, _R_BQ).astype(BF16)
    bq_ref[...] = (mm(_R_BQ, _R_BK) * (B_KEY_DIM ** -0.5)).astype(BF16)
    bk_ref[...] = mm(_R_BK, _R_BV).astype(BF16)
    bv_ref[...] = mm(_R_BV, _R_ALPHA).astype(BF16)
    r = mm(_R_BR, _R_GATE)
    br_ref[...] = (r * jax.nn.sigmoid(r)).astype(BF16)
    for c in range(_R_GATE, _R_END, 512):
        gate_ref[:, c - _R_GATE:c - _R_GATE + 512] = jax.nn.sigmoid(mm(c, c + 512)).astype(BF16)
    z = _dot(mm(_R_ALPHA, _R_ALPHA + LANES).astype(BF16), wal2_bf[...]) + bal_ref[...]
    lga_ref[...] = (jnp.minimum(z, 0.0) - jnp.log(1.0 + jnp.exp(-jnp.abs(z)))) * (1.0 / B_GATE_TAU)


def _inproj(x, g, w_in, w_al2, b_al, layer, moe=None):
    t, d = x.shape
    assert w_in.shape[2] == _R_END
    row = lambda w: pl.BlockSpec((INPROJ_ROWS, w), lambda i: (i, 0))
    full = lambda a: pl.BlockSpec(a.shape, lambda i: (0,) * a.ndim)
    sds = lambda w, dt: jax.ShapeDtypeStruct((t, w), dt)
    widths = [(512, BF16), (512, BF16), (512, BF16), (256, BF16), (256, BF16), (512, BF16),
              (256, F32), (512, BF16), (2048, BF16)]
    fused = moe is not None
    moe_specs, moe_out_specs, moe_out_shape = [], [], []
    if fused:
        n_chunks = INPROJ_ROWS // ROW_TILE * LOCAL_CHUNKS
        moe_specs = [row(LANES), pl.BlockSpec((n_chunks, CHUNK_ROWS, d), lambda i: (i, 0, 0))]
        moe_out_specs, moe_out_shape = [row(d)], [sds(d, F32)]
    return pl.pallas_call(
        functools.partial(_inproj_kernel, layer, fused),
        grid=(t // INPROJ_ROWS,),
        in_specs=[row(d)] + moe_specs + [full(g), pl.BlockSpec(memory_space=pl.ANY),
                                         pl.BlockSpec((1,) + w_al2.shape[1:], lambda i: (layer, 0, 0)),
                                         full(b_al)],
        out_specs=moe_out_specs + [row(w) for w, _ in widths],
        out_shape=moe_out_shape + [sds(w, dt) for w, dt in widths],
        scratch_shapes=[pltpu.VMEM((_R_END, d), BF16), pltpu.VMEM((LANES, B_KEY_WIDTH), BF16),
                        pltpu.VMEM((2, _W_PIECE, d), F32), pltpu.SemaphoreType.DMA((2,))],
        compiler_params=_params("arbitrary"),
        name="inproj",
    )(x, *(moe or ()), g, jnp.swapaxes(w_in, 1, 2), w_al2, b_al)


def _band_kernel(q_ref, k0_ref, k1_ref, k2_ref, v0_ref, v1_ref, v2_ref, u_ref, o_ref, bias_ref):
    i = pl.program_id(1)
    lane = lax.broadcasted_iota(I32, (1, LANES), 1)
    low = lane < A_HEAD_DIM
    ones = jnp.ones((BAND_KEYS, LANES), BF16)

    @pl.when((pl.program_id(0) == 0) & (i == 0))
    def _():
        cq = lax.broadcasted_iota(I32, (ROW_TILE, BAND_KEYS), 0) >> LOG_CHUNK
        ck = lax.broadcasted_iota(I32, (ROW_TILE, BAND_KEYS), 1) >> LOG_CHUNK
        valid = (ck >= cq) & (ck <= cq + A_LEFT_CHUNKS)
        for h in range(A_HEADS):
            rows = jnp.broadcast_to(u_ref[h:h + 1, :], (ROW_TILE, BIAS_PERIOD))
            rows = pltpu.roll(rows, BIAS_PERIOD - (ROW_TILE - 1), 1, stride=1, stride_axis=0)
            bias_ref[h // 2, (h % 2) * ROW_TILE:(h % 2 + 1) * ROW_TILE, :] = jnp.where(
                valid, rows[:, :BAND_KEYS], NEG)

    def attend(pen):
        for p in range(A_HEADS // 2):
            sl = slice(p * LANES, (p + 1) * LANES)
            qp = q_ref[:, sl]
            zero = jnp.zeros_like(qp)
            q2 = jnp.concatenate([jnp.where(low, qp, zero), jnp.where(low, zero, qp)], axis=0)
            kp = jnp.concatenate([k0_ref[:, sl], k1_ref[:, sl], k2_ref[:, sl]], axis=0)
            vp = jnp.concatenate([v0_ref[:, sl], v1_ref[:, sl], v2_ref[:, sl]], axis=0)
            s = _dot_nt(q2, kp) + bias_ref[p]
            if pen is not None:
                s = s + pen
            pe = jnp.exp2(s - jnp.max(s, axis=-1, keepdims=True)).astype(BF16)
            o2 = _dot(pe, jnp.concatenate([vp, ones], axis=1))
            o = o2[:, :LANES] * (1.0 / o2[:, LANES:])
            o_ref[:, sl] = jnp.where(low, o[:ROW_TILE], o[ROW_TILE:]).astype(BF16)

    @pl.when(i >= BAND_TILES - 1)
    def _():
        attend(None)

    @pl.when(i < BAND_TILES - 1)
    def _():
        col = lax.broadcasted_iota(I32, (1, BAND_KEYS), 1)
        attend(jnp.where(col < (BAND_TILES - 1 - i) * ROW_TILE, NEG, 0.0).astype(F32))


def _band_bias_vector(rel_table):
    h = rel_table.shape[0]
    tab = rel_table.astype(F32) * LOG2E
    shift = A_LEFT_CHUNKS * CHUNK + ROW_TILE - 1
    n_far = shift - A_MAX_REL + 1
    span = ROW_TILE + BAND_KEYS - 1
    assert span - 1 - shift <= A_MAX_REL and span <= BIAS_PERIOD
    u = jnp.concatenate([jnp.broadcast_to(tab[:, 2 * A_MAX_REL:], (h, n_far)),
                         tab[:, 2 * A_MAX_REL - 1:2 * A_MAX_REL - 1 - (span - n_far):-1]], axis=1)
    return jnp.pad(u, ((0, 0), (0, BIAS_PERIOD - span)))


def _band_attention(q, k, v, u, batch):
    t, w = q.shape
    nb = t // batch // ROW_TILE
    cur = lambda b, i: (b * nb + i, 0)
    back = lambda n: (lambda b, i: (b * nb + jnp.maximum(i - n, 0), 0))
    blk = lambda f: pl.BlockSpec((ROW_TILE, w), f)
    return pl.pallas_call(
        _band_kernel,
        grid=(batch, nb),
        in_specs=[blk(cur), blk(back(2)), blk(back(1)), blk(cur), blk(back(2)), blk(back(1)), blk(cur),
                  pl.BlockSpec(u.shape, lambda b, i: (0, 0))],
        out_specs=blk(cur),
        out_shape=jax.ShapeDtypeStruct((t, w), BF16),
        scratch_shapes=[pltpu.VMEM((A_HEADS // 2, 2 * ROW_TILE, BAND_KEYS), F32)],
        compiler_params=_params("arbitrary", "arbitrary"),
        name="band_attn",
    )(q, k, k, k, v, v, v, u)


def _gla_constants():
    c = CHUNK
    t = np.arange(c)[:, None]
    r = np.arange(c)[None, :]
    mats = [(r <= t), (r > t)]
    lvl = np.full((c, c), -1, np.int32)
    lvl[np.arange(c), np.arange(c)] = N_LEVELS
    for l in range(N_LEVELS):
        m = (c // 2) >> l
        mid = (t // (2 * m)) * (2 * m) + m
        upper = t >= mid
        mats.append(np.where(upper, (r >= mid) & (r <= t), (r > t) & (r < mid)))
        s = r
        same = (s // (2 * m)) == (t // (2 * m))
        lvl[np.asarray(same & upper & (s < mid))] = l
    eye = np.eye(CHUNKS_PER_TILE)
    mexp = np.concatenate([np.kron(eye, m) for m in mats], axis=0).astype(np.float32)
    lvl = np.tile(lvl, (1, B_HEADS))
    return jnp.asarray(mexp, BF16), jnp.asarray(lvl, I32)


def _gla_kernel(q_ref, k_ref, v_ref, g_ref, r_ref, gn_ref, mexp_ref, lvl_ref, o_ref, s_ref):
    @pl.when(pl.program_id(1) == 0)
    def _():
        s_ref[...] = jnp.zeros_like(s_ref)

    kw = B_KEY_WIDTH
    ri = lax.broadcasted_iota(I32, (kw, kw), 0) >> LOG_CHUNK
    ci = lax.broadcasted_iota(I32, (kw, kw), 1) >> LOG_CHUNK
    bd = ri == ci
    head_ind = jnp.where(bd, 1.0, 0.0).astype(BF16)
    ri2 = lax.broadcasted_iota(I32, (kw, 2 * kw), 0) >> LOG_CHUNK
    ci2 = (lax.broadcasted_iota(I32, (kw, 2 * kw), 1) & (kw - 1)) >> LOG_CHUNK
    bd2 = ri2 == ci2
    lvl = lvl_ref[...]
    row8 = lax.broadcasted_iota(I32, (16, kw), 0)
    ones = jnp.ones((16, LANES), BF16)
    zero_b = jnp.zeros((kw, kw), BF16)
    chunks = [slice(c * CHUNK, (c + 1) * CHUNK) for c in range(CHUNKS_PER_TILE)]

    def head_blocks(x):
        return jnp.where(bd, jnp.concatenate([x] * B_HEADS, axis=0), zero_b)

    q = q_ref[...].astype(F32)
    k = k_ref[...].astype(F32)
    g = g_ref[...]
    gb = g.astype(BF16)
    half = EXP_ROWS * CHUNKS_PER_TILE // 2
    w = jnp.exp(jnp.concatenate([_dot(mexp_ref[:half, :], gb), _dot(mexp_ref[half:, :], gb)], axis=0))
    w_cum = w[0:ROW_TILE]
    qt = (q * w_cum).astype(BF16)
    kb = (k * w[ROW_TILE:2 * ROW_TILE]).astype(BF16)
    qk = (q * k).astype(BF16)

    attn = [jnp.zeros((CHUNK, kw), F32) for _ in chunks]
    for l in range(N_LEVELS):
        wl = w[(2 + l) * ROW_TILE:(3 + l) * ROW_TILE]
        qh = (q * wl).astype(BF16)
        kh = (k * wl).astype(BF16)
        for c, rows in enumerate(chunks):
            attn[c] = jnp.where(lvl == l, _dot_nt(qh[rows], head_blocks(kh[rows])), attn[c])

    vstacks, kvs, dcols = [], [], []
    for c, rows in enumerate(chunks):
        attn[c] = jnp.where(lvl == N_LEVELS, _dot(qk[rows], head_ind), attn[c])
        v = v_ref[rows, :]
        vstack = jnp.concatenate([v[:, j * LANES:(j + 1) * LANES] for j in range(B_HEADS)], axis=0)
        vstacks.append(vstack)
        kvs.append(_dot_tn(head_blocks(kb[rows]), vstack))
        d = jnp.exp(jnp.sum(g[rows], axis=0, keepdims=True))
        d1 = d.astype(BF16).astype(F32)
        dp = jnp.where(row8 == 0, d1, jnp.where(row8 == 1, d - d1, 0.0)).astype(BF16)
        dcols.append(_dot_tn(dp, ones))

    s = s_ref[...]
    for c, rows in enumerate(chunks):
        lhs = jnp.concatenate([attn[c].astype(BF16), qt[rows]], axis=1)
        lhs = jnp.where(bd2, jnp.concatenate([lhs] * B_HEADS, axis=0), jnp.zeros((kw, 2 * kw), BF16))
        rhs = jnp.concatenate([vstacks[c], s.astype(BF16)], axis=0)
        o = _dot(lhs, rhs)
        s = dcols[c] * s + kvs[c]
        for j in range(B_HEADS):
            oj = o[j * CHUNK:(j + 1) * CHUNK]
            sl = slice(j * LANES, (j + 1) * LANES)
            y = oj * lax.rsqrt(jnp.mean(oj * oj, axis=-1, keepdims=True) + EPS) * gn_ref[...]
            o_ref[rows, sl] = (y * r_ref[rows, sl].astype(F32)).astype(BF16)
    s_ref[...] = s


def _gla(q, k, v, g, r, gn, batch):
    t = q.shape[0]
    nb = t // batch // ROW_TILE
    mexp, lvl = _gla_constants()
    cur = lambda b, i: (b * nb + i, 0)
    blk = lambda w: pl.BlockSpec((ROW_TILE, w), cur)
    full = lambda a: pl.BlockSpec(a.shape, lambda b, i: (0,) * a.ndim)
    return pl.pallas_call(
        _gla_kernel,
        grid=(batch, nb),
        in_specs=[blk(B_KEY_WIDTH), blk(B_KEY_WIDTH), blk(B_VAL_WIDTH), blk(B_KEY_WIDTH), blk(B_VAL_WIDTH),
                  full(gn), full(mexp), full(lvl)],
        out_specs=blk(B_VAL_WIDTH),
        out_shape=jax.ShapeDtypeStruct((t, B_VAL_WIDTH), BF16),
        scratch_shapes=[pltpu.VMEM((B_KEY_WIDTH, B_VAL_DIM), F32)],
        compiler_params=_params("arbitrary", "arbitrary"),
        name="gla",
    )(q, k, v, g, r, gn, mexp, lvl)


def _token_kernel(x_ref, oa_ref, ob_ref, gate_ref, wb0_ref, wb1_ref, wmix_ref, gx_ref, wq_ref,
                  km_ref, vm_ref, wo_ref, gf_ref, wr_ref, br_ref, ltri_ref, utri_ref,
                  x2_ref, hs_ref, route_ref, cnt_ref):
    ma = _dot(oa_ref[...], wb0_ref[...])
    mb = _dot(ob_ref[...], wb1_ref[...])
    merged = (gate_ref[:, :D_MODEL].astype(F32) * ma + gate_ref[:, D_MODEL:].astype(F32) * mb).astype(BF16)
    x1 = x_ref[...] + _dot(merged, wmix_ref[...])

    h2 = _rms(x1, gx_ref[...]).astype(BF16)
    qx = (_dot(h2, wq_ref[...]) * (X_HEAD_DIM ** -0.5)).astype(BF16)
    heads = []
    for h in range(X_HEADS):
        sl = slice(h * X_HEAD_DIM, (h + 1) * X_HEAD_DIM)
        s = _dot_nt(qx[:, sl], km_ref[0, :, sl])
        m = jnp.max(s, axis=-1, keepdims=True)
        pe = jnp.exp(s - m)
        l = jnp.sum(pe, axis=-1, keepdims=True)
        heads.append((_dot(pe.astype(BF16), vm_ref[0, :, sl]) * (1.0 / l)).astype(BF16))
    x2 = x1 + _dot(jnp.concatenate(heads, axis=1), wo_ref[...])
    x2_ref[...] = x2

    h3 = _rms(x2, gf_ref[...])

    h3_hi = h3.astype(BF16)
    h3_lo = (h3 - h3_hi.astype(F32)).astype(BF16)
    hw = _dot(h3_hi, wr_ref[...])
    logits = hw[:, :LANES] + hw[:, LANES:] + _dot(h3_lo, wr_ref[:, :LANES]) + br_ref[...]
    for h in range(TOKEN_TILES_PER_STEP):
        rows = slice(h * ROW_TILE, (h + 1) * ROW_TILE)
        chunks = pl.ds(h * LOCAL_CHUNKS, LOCAL_CHUNKS)
        _route_and_sort(logits[rows], h3_hi[rows], ltri_ref, utri_ref, hs_ref.at[chunks],
                        route_ref.at[pl.ds(h * ROW_TILE, ROW_TILE)], cnt_ref.at[h])


def _route_and_sort(logits, h3_hi, ltri_ref, utri_ref, hs_ref, route_ref, cnt_ref):
    lane = lax.broadcasted_iota(I32, logits.shape, 1).astype(F32)
    big = jnp.float32(LANES)
    gl = jnp.where(lane < N_GROUPS, logits, NEG)
    gmax = jnp.max(gl, axis=-1, keepdims=True)
    gidx = jnp.min(jnp.where(gl == gmax, lane, big), axis=-1, keepdims=True)
    g_w = 1.0 / jnp.sum(jnp.exp(gl - gmax), axis=-1, keepdims=True)
    lo = N_GROUPS + EXPERTS_PER_GROUP * gidx
    el = jnp.where((lane >= lo) & (lane < lo + EXPERTS_PER_GROUP), logits, NEG)
    v1 = jnp.max(el, axis=-1, keepdims=True)
    i1 = jnp.min(jnp.where(el == v1, lane, big), axis=-1, keepdims=True)
    el2 = jnp.where(lane == i1, NEG, el)
    v2 = jnp.max(el2, axis=-1, keepdims=True)
    i2 = jnp.min(jnp.where(el2 == v2, lane, big), axis=-1, keepdims=True)
    e21 = jnp.exp(v2 - v1)
    w1 = g_w / (1.0 + e21)
    w2 = w1 * e21
    oh0 = jnp.where(lane == i1 - N_GROUPS, 1.0, 0.0)
    oh1 = jnp.where(lane == i2 - N_GROUPS, 1.0, 0.0)
    oh = oh0 + oh1
    nch = jnp.floor((jnp.sum(oh, axis=0, keepdims=True) + (CHUNK_ROWS - 1)) * (1.0 / CHUNK_ROWS))
    nch8 = jnp.broadcast_to(nch, (8, LANES))
    start = _dot(nch8.astype(BF16), utri_ref[...])[0:1] * CHUNK_ROWS
    rank = _dot(ltri_ref[...], oh.astype(BF16))
    row = start + rank
    pos0 = jnp.sum(row * oh0, axis=-1, keepdims=True)
    pos1 = jnp.sum(row * oh1, axis=-1, keepdims=True)
    route = jnp.where(lane == 0, pos0, jnp.where(lane == 1, pos1, 0.0))
    route_t = jnp.transpose(route)
    r = lax.broadcasted_iota(I32, (LOCAL_ROWS, ROW_TILE), 0).astype(F32)
    p0 = jnp.where(r == route_t[0:1, :], 1.0, 0.0).astype(BF16)
    p1 = jnp.where(r == route_t[1:2, :], 1.0, 0.0).astype(BF16)

    def gate_cols(w):
        hi = w.astype(BF16).astype(F32)
        return jnp.where(lane == 0, hi, jnp.where(lane == 1, w - hi, 0.0)).astype(BF16)

    sorted_rows = jnp.concatenate([_dot(p0 + p1, h3_hi), _dot(p0, gate_cols(w1)) + _dot(p1, gate_cols(w2))],
                                  axis=1)
    hs_ref[...] = sorted_rows.astype(BF16).reshape(hs_ref.shape)
    route_ref[...] = route
    cnt_ref[...] = nch8


def _token(x, oa, ob, gates, wb0, wb1, wmix, gx, wq, km, vm, wo, gf, wr, br, batch):
    t, d = x.shape
    n = TOKEN_TILES_PER_STEP
    nb = t // batch // (n * ROW_TILE)
    nt = t // ROW_TILE
    ltri = jnp.asarray(np.tril(np.ones((ROW_TILE, ROW_TILE), np.float32), -1), BF16)
    utri = jnp.asarray(np.triu(np.ones((LANES, LANES), np.float32), 1), BF16)
    cur = lambda b, i: (b * nb + i, 0)
    cur3 = lambda b, i: (b * nb + i, 0, 0)
    blk = lambda w: pl.BlockSpec((n * ROW_TILE, w), cur)
    full = lambda a: pl.BlockSpec(a.shape, lambda b, i: (0,) * a.ndim)
    mem = pl.BlockSpec((1,) + km.shape[1:], lambda b, i: (b, 0, 0))
    return pl.pallas_call(
        _token_kernel,
        grid=(batch, nb),
        in_specs=[blk(d), blk(A_WIDTH), blk(B_VAL_WIDTH), blk(2 * d), full(wb0), full(wb1), full(wmix),
                  full(gx), full(wq), mem, mem, full(wo), full(gf), full(wr), full(br), full(ltri), full(utri)],
        out_specs=[blk(d), pl.BlockSpec((n * LOCAL_CHUNKS, CHUNK_ROWS, SORT_WIDTH), cur3),
                   blk(LANES), pl.BlockSpec((n, 8, LANES), cur3)],
        out_shape=[jax.ShapeDtypeStruct((t, d), F32),
                   jax.ShapeDtypeStruct((nt * LOCAL_CHUNKS, CHUNK_ROWS, SORT_WIDTH), BF16),
                   jax.ShapeDtypeStruct((t, LANES), F32),
                   jax.ShapeDtypeStruct((nt, 8, LANES), F32)],
        compiler_params=_params("arbitrary", "arbitrary"),
        name="token",
    )(x, oa, ob, gates, wb0, wb1, wmix, gx, wq, km, vm, wo, gf, wr, br, ltri, utri)


def _expert_kernel(layer, te_ref, nu_ref, nv_ref, ch_ref, first_ref, nxt_ref,
                   hs_hbm, wg_hbm, wu_hbm, wd_hbm, ys_hbm,
                   xbuf, ybuf, wg_st, wu_st, wd_st, wg_bf, wu_bf, wd_bf, run_ref, gsem, ssem, wsem):
    i = pl.program_id(0)
    n_used = nu_ref[0]
    slot = lax.rem(i, 2)

    def weights(expert, s, start):
        for src, dst in ((wg_hbm, wg_st), (wu_hbm, wu_st), (wd_hbm, wd_st)):
            cp = pltpu.make_async_copy(src.at[layer * N_EXPERTS + expert], dst.at[s], wsem.at[s])
            cp.start() if start else cp.wait()

    def for_chunks(tile, fn):
        nv = nv_ref[tile]

        @pl.when(nv == TILE_CHUNKS)
        def _():
            for c in range(TILE_CHUNKS):
                fn(c)

        @pl.when(nv != TILE_CHUNKS)
        def _():
            def body(c, carry):
                fn(c)
                return carry

            lax.fori_loop(0, nv, body, 0)

    def gather(tile, s, start):
        for c in range(TILE_CHUNKS):
            cp = pltpu.make_async_copy(hs_hbm.at[ch_ref[tile * TILE_CHUNKS + c]], xbuf.at[s, c], gsem.at[s])
            cp.start() if start else cp.wait()

    def scatter(tile, s, start):
        def one(c):
            cp = pltpu.make_async_copy(ybuf.at[s, c], ys_hbm.at[ch_ref[tile * TILE_CHUNKS + c]], ssem.at[s])
            cp.start() if start else cp.wait()

        for_chunks(tile, one)

    next_tile = jnp.minimum(i + 1, pl.num_programs(0) - 1)

    @pl.when(i == 0)
    def _():
        run_ref[0] = 0
        weights(te_ref[0], 0, True)
        gather(0, 0, True)

    @pl.when(i < n_used)
    def _():
        gather(next_tile, 1 - slot, True)
        gather(i, slot, False)

        @pl.when(i >= 2)
        def _():
            scatter(i - 2, slot, False)

        @pl.when(first_ref[i] == 1)
        def _():
            s = lax.rem(run_ref[0], 2)
            weights(te_ref[i], s, False)

            @pl.when(nxt_ref[i] >= 0)
            def _():
                weights(nxt_ref[i], 1 - s, True)

            wg_bf[...] = wg_st[s].astype(BF16)
            wu_bf[...] = wu_st[s].astype(BF16)
            wd_bf[...] = wd_st[s].astype(BF16)
            run_ref[0] = run_ref[0] + 1

        xg = xbuf[slot].reshape(EXPERT_ROWS, SORT_WIDTH)
        x = xg[:, :D_MODEL]
        hg = _dot(x, wg_bf[...])
        hu = _dot(x, wu_bf[...])
        hid = (hg * jax.nn.sigmoid(hg) * hu).astype(BF16)
        g = xg[:, D_MODEL:].astype(F32)
        y = ((g[:, 0:1] + g[:, 1:2]) * _dot(hid, wd_bf[...])).astype(BF16)
        y = jnp.concatenate([y, jnp.zeros((EXPERT_ROWS, LANES), BF16)], axis=1)
        ybuf[slot] = y.reshape(TILE_CHUNKS, CHUNK_ROWS, SORT_WIDTH)
        scatter(i, slot, True)

        @pl.when(i == n_used - 1)
        def _():
            gather(next_tile, 1 - slot, False)
            scatter(i, slot, False)

            @pl.when(i >= 1)
            def _():
                scatter(i - 1, 1 - slot, False)


def _experts(hs, tile_expert, n_used, n_valid, chunks, run_first, run_next, wg, wu, wd, layer):
    n_tiles = tile_expert.shape[0]
    anyspace = pl.BlockSpec(memory_space=pl.ANY)
    grid_spec = pltpu.PrefetchScalarGridSpec(
        num_scalar_prefetch=6,
        grid=(n_tiles,),
        in_specs=[anyspace] * 4,
        out_specs=anyspace,
        scratch_shapes=[pltpu.VMEM((2, TILE_CHUNKS, CHUNK_ROWS, SORT_WIDTH), BF16),
                        pltpu.VMEM((2, TILE_CHUNKS, CHUNK_ROWS, SORT_WIDTH), BF16),
                        pltpu.VMEM((2, D_MODEL, EXPERT_FF), F32), pltpu.VMEM((2, D_MODEL, EXPERT_FF), F32),
                        pltpu.VMEM((2, EXPERT_FF, D_MODEL), F32),
                        pltpu.VMEM((D_MODEL, EXPERT_FF), BF16), pltpu.VMEM((D_MODEL, EXPERT_FF), BF16),
                        pltpu.VMEM((EXPERT_FF, D_MODEL), BF16),
                        pltpu.SMEM((1,), I32),
                        pltpu.SemaphoreType.DMA((2,)), pltpu.SemaphoreType.DMA((2,)),
                        pltpu.SemaphoreType.DMA((2,))],
    )
    return pl.pallas_call(
        functools.partial(_expert_kernel, layer),
        grid_spec=grid_spec,
        out_shape=jax.ShapeDtypeStruct(hs.shape, BF16),
        input_output_aliases={6: 0},
        compiler_params=_params("arbitrary"),
        name="experts",
    )(tile_expert, n_used, n_valid, chunks, run_first, run_next, hs, wg, wu, wd)


def _combine_kernel(x_ref, route_ref, ys_ref, gfin_ref, o_ref):
    r = lax.broadcasted_iota(I32, (ROW_TILE, LOCAL_ROWS), 1).astype(F32)
    for t in range(COMBINE_TILES_PER_STEP):
        rows = slice(t * ROW_TILE, (t + 1) * ROW_TILE)
        pos = route_ref[rows, :]
        sel = jnp.where(r == pos[:, 0:1], 1.0, jnp.where(r == pos[:, 1:2], 1.0, 0.0)).astype(BF16)
        ys = ys_ref[t * LOCAL_CHUNKS:(t + 1) * LOCAL_CHUNKS].reshape(LOCAL_ROWS, D_MODEL)
        o_ref[rows, :] = _rms(x_ref[rows, :] + _dot(sel, ys), gfin_ref[...])


def _combine(x2, route, ys, gfin):
    t, d = x2.shape
    n = COMBINE_TILES_PER_STEP
    return pl.pallas_call(
        _combine_kernel,
        grid=(t // (n * ROW_TILE),),
        in_specs=[pl.BlockSpec((n * ROW_TILE, d), lambda i: (i, 0)),
                  pl.BlockSpec((n * ROW_TILE, LANES), lambda i: (i, 0)),
                  pl.BlockSpec((n * LOCAL_CHUNKS, CHUNK_ROWS, d), lambda i: (i, 0, 0)),
                  pl.BlockSpec((1, d), lambda i: (0, 0))],
        out_specs=pl.BlockSpec((n * ROW_TILE, d), lambda i: (i, 0)),
        out_shape=jax.ShapeDtypeStruct((t, d), F32),
        compiler_params=_params("arbitrary"),
        name="combine",
    )(x2, route, ys, gfin)


def _chunk_plan(nch, n_tiles):
    nt = nch.shape[0]
    local_start = jnp.cumsum(nch, axis=1) - nch
    cum = jnp.cumsum(nch, axis=0)
    total = cum[-1]
    tiles = (total + TILE_CHUNKS - 1) // TILE_CHUNKS
    tile_end = jnp.cumsum(tiles)
    n_used = tile_end[-1:]
    tile_ids = jnp.arange(n_tiles, dtype=I32)
    tile_expert = jnp.minimum(jnp.sum((tile_end[None, :] <= tile_ids[:, None]).astype(I32), axis=1),
                              N_EXPERTS - 1)
    sel = (tile_expert[:, None] == jnp.arange(N_EXPERTS, dtype=I32)[None, :]).astype(I32)
    pick = lambda table: jnp.sum(sel[:, :, None] * table.T[None, :, :], axis=1)
    first_tile = jnp.sum(sel * (tile_end - tiles)[None, :], axis=1)
    slot = (tile_ids - first_tile)[:, None] * TILE_CHUNKS + jnp.arange(TILE_CHUNKS, dtype=I32)[None, :]
    valid = (slot < jnp.sum(sel * total[None, :], axis=1)[:, None]) & (tile_ids < n_used)[:, None]
    src_tile = jnp.sum((pick(cum)[:, None, :] <= slot[:, :, None]).astype(I32), axis=2)
    src_tile = jnp.minimum(src_tile, nt - 1)
    at = (src_tile[:, :, None] == jnp.arange(nt, dtype=I32)[None, None, :]).astype(I32)
    before = jnp.sum(at * pick(cum - nch)[:, None, :], axis=2)
    start = jnp.sum(at * pick(local_start)[:, None, :], axis=2)
    chunk = jnp.where(valid, src_tile * LOCAL_CHUNKS + start + slot - before, LOCAL_CHUNKS - 1)
    used = tile_ids < n_used
    run_first = ((tile_ids == first_tile) & used).astype(I32)
    run_end = jnp.sum(sel * tile_end[None, :], axis=1)
    next_expert = jnp.sum((run_end[:, None] == tile_ids[None, :]).astype(I32) * tile_expert[None, :], axis=1)
    run_next = jnp.where(used & (run_end < n_used), next_expert, -1)
    return tile_expert, n_used, jnp.sum(valid.astype(I32), axis=1), chunk.reshape(-1), run_first, run_next


def kernel(x, mem, norm_mix_g, w_in, rel_bias, gla_w_alpha, gla_b_alpha, gla_norm_g, w_branch, w_mix_out, norm_x_g, mem_norm_g, w_xq, w_xkv, w_xo, norm_ffn_g, w_group_router, b_group_router, w_expert_router, b_expert_router, w_exp_gate, w_exp_up, w_exp_down, final_norm_g):
    batch, seq, d = x.shape
    depth = w_in.shape[0]
    t = batch * seq
    assert d == D_MODEL and seq % (TOKEN_TILES_PER_STEP * ROW_TILE) == 0
    nt = t // ROW_TILE
    n_tiles = nt * LOCAL_CHUNKS // TILE_CHUNKS + N_EXPERTS

    xf = x.reshape(t, d)
    km_all, vm_all = _memkv(mem, mem_norm_g, w_xkv.astype(BF16))
    row = lambda a: a.reshape(1, -1).astype(F32)

    moe = None
    for l in range(depth):
        res = _inproj(xf, row(norm_mix_g[l]), w_in, gla_w_alpha, row(gla_b_alpha[l]), l, moe)
        if moe is not None:
            xf, res = res[0], res[1:]
        aq, ak, av, bq, bk, bv, lga, br, gates = res

        oa = _band_attention(aq, ak, av, _band_bias_vector(rel_bias[l]), batch)
        ob = _gla(bq, bk, bv, lga, br, row(gla_norm_g[l]), batch)

        wr = jnp.pad(jnp.concatenate([w_group_router[l], w_expert_router[l]], axis=1).astype(F32),
                     ((0, 0), (0, LANES - N_GROUPS - N_EXPERTS)))
        wr_hi = wr.astype(BF16)
        wr = jnp.concatenate([wr_hi, (wr - wr_hi.astype(F32)).astype(BF16)], axis=1)
        brt = jnp.pad(jnp.concatenate([b_group_router[l], b_expert_router[l]]).astype(F32),
                      (0, LANES - N_GROUPS - N_EXPERTS)).reshape(1, LANES)
        x2, hs, route, cnt = _token(
            xf, oa, ob, gates, w_branch[l, 0].astype(BF16), w_branch[l, 1].astype(BF16),
            w_mix_out[l].astype(BF16), row(norm_x_g[l]), w_xq[l].astype(BF16), km_all[l], vm_all[l],
            w_xo[l].astype(BF16), row(norm_ffn_g[l]), wr, brt, batch)

        plan = _chunk_plan(cnt[:, 0, :N_EXPERTS].astype(I32), n_tiles)
        e3 = lambda w: w.reshape((depth * N_EXPERTS,) + w.shape[3:])
        ys = _experts(hs, *plan, e3(w_exp_gate), e3(w_exp_up), e3(w_exp_down), l)
        xf, moe = x2, (route, ys)

    return _combine(x2, route, ys, row(final_norm_g)).reshape(batch, seq, d)
```

```python
import functools

import numpy as np
import jax
import jax.numpy as jnp
from jax import lax
from jax.experimental import pallas as pl
from jax.experimental.pallas import tpu as pltpu

F32 = jnp.float32
BF16 = jnp.bfloat16
I32 = jnp.int32

D_MODEL = 1024
CHUNK = 64
EPS = 1e-6
A_HEADS = 8
A_HEAD_DIM = 64
A_WIDTH = 512
A_LEFT_CHUNKS = 8
A_MAX_REL = 256
B_HEADS = 4
B_KEY_DIM = 64
B_VAL_DIM = 128
B_KEY_WIDTH = 256
B_VAL_WIDTH = 512
B_GATE_RANK = 16
B_GATE_TAU = 16.0
X_HEADS = 4
X_HEAD_DIM = 256
N_GROUPS = 4
EXPERTS_PER_GROUP = 8
N_EXPERTS = N_GROUPS * EXPERTS_PER_GROUP
EXPERT_FF = 256

LANES = 128
ROW_TILE = 256
CHUNKS_PER_TILE = ROW_TILE // CHUNK
BAND_TILES = A_LEFT_CHUNKS // CHUNKS_PER_TILE + 1
BAND_KEYS = BAND_TILES * ROW_TILE
BIAS_PERIOD = 1024
LOG_CHUNK = 6
N_LEVELS = LOG_CHUNK
EXP_ROWS = (2 + N_LEVELS) * CHUNK
CHUNK_ROWS = 16
TOKEN_TILES_PER_STEP = 2
INPROJ_ROWS = 512
COMBINE_TILES_PER_STEP = 2
GLA_TILES_PER_STEP = 2
BAND_TILES_PER_STEP = 2
EXPERT_ROWS = 512
TILE_CHUNKS = EXPERT_ROWS // CHUNK_ROWS
LOCAL_CHUNKS = 2 * ROW_TILE // CHUNK_ROWS + N_EXPERTS
LOCAL_ROWS = LOCAL_CHUNKS * CHUNK_ROWS
assert (2 * ROW_TILE + N_EXPERTS * (CHUNK_ROWS - 1)) // CHUNK_ROWS < LOCAL_CHUNKS
SORT_WIDTH = D_MODEL + LANES
NEG = -1e30
LOG2E = 1.4426950408889634
VMEM_LIMIT = 56 * 1024 * 1024


def _params(*sem):
    return pltpu.CompilerParams(dimension_semantics=sem, vmem_limit_bytes=VMEM_LIMIT)


def _rms(x, g):
    return x * lax.rsqrt(jnp.mean(x * x, axis=-1, keepdims=True) + EPS) * g


def _dot(a, b):
    return jnp.dot(a, b, preferred_element_type=F32)


def _dot_nt(a, b):
    return lax.dot_general(a, b, (((1,), (1,)), ((), ())), preferred_element_type=F32)


def _dot_tn(a, b):
    return lax.dot_general(a, b, (((0,), (0,)), ((), ())), preferred_element_type=F32)


def _memkv_kernel(mem_ref, g_ref, w_ref, k_ref, v_ref):
    mn = _rms(mem_ref[0], g_ref[...]).astype(BF16)
    kv = _dot(mn, w_ref[0])
    k_ref[0, 0] = kv[:, :D_MODEL].astype(BF16)
    v_ref[0, 0] = kv[:, D_MODEL:].astype(BF16)


def _memkv(mem, g, w_xkv):
    depth = w_xkv.shape[0]
    b, m, d = mem.shape
    out = jax.ShapeDtypeStruct((depth, b, m, d), BF16)
    return pl.pallas_call(
        _memkv_kernel,
        grid=(depth, b),
        in_specs=[pl.BlockSpec((1, m, d), lambda l, i: (i, 0, 0)),
                  pl.BlockSpec((1, d), lambda l, i: (0, 0)),
                  pl.BlockSpec((1, d, 2 * d), lambda l, i: (l, 0, 0))],
        out_specs=[pl.BlockSpec((1, 1, m, d), lambda l, i: (l, i, 0, 0)),
                   pl.BlockSpec((1, 1, m, d), lambda l, i: (l, i, 0, 0))],
        out_shape=[out, out],
        compiler_params=_params("arbitrary", "arbitrary"),
        name="memkv",
    )(mem, g.reshape(1, d), w_xkv)


_R_AQ, _R_AK, _R_AV = 0, 512, 1024
_R_BQ, _R_BK, _R_BV = 1536, 1792, 2048
_R_ALPHA, _R_BR, _R_GATE, _R_END = 2560, 2576, 3088, 5136
_W_PIECE = 512


def _inproj_kernel(layer, fused, *refs):
    if fused:
        x_ref, route_ref, ys_ref, g_ref, wt_hbm, wal2_ref, bal_ref = refs[:7]
        refs = refs[7:]
        x3_ref, refs = refs[0], refs[1:]
    else:
        x_ref, g_ref, wt_hbm, wal2_ref, bal_ref = refs[:5]
        refs = refs[5:]
    (aq_ref, ak_ref, av_ref, bq_ref, bk_ref, bv_ref, lga_ref, br_ref, gate_ref,
     wt_bf, wal2_bf, stage, sem) = refs

    @pl.when(pl.program_id(0) == 0)
    def _():
        pieces = [(c, min(_W_PIECE, _R_END - c)) for c in range(0, _R_END, _W_PIECE)]

        def piece_copy(p):
            c, n = pieces[p]
            return pltpu.make_async_copy(wt_hbm.at[layer, pl.ds(c, n), :], stage.at[p % 2, pl.ds(0, n), :],
                                         sem.at[p % 2])

        piece_copy(0).start()
        for p, (c, n) in enumerate(pieces):
            if p + 1 < len(pieces):
                piece_copy(p + 1).start()
            piece_copy(p).wait()
            wt_bf[c:c + n, :] = stage[p % 2, 0:n, :].astype(BF16)
        wal2_bf[...] = jnp.concatenate(
            [wal2_ref[0].astype(BF16), jnp.zeros((LANES - B_GATE_RANK, B_KEY_WIDTH), BF16)], axis=0)

    if fused:
        r = lax.broadcasted_iota(I32, (ROW_TILE, LOCAL_ROWS), 1).astype(F32)
        tiles = []
        for t in range(INPROJ_ROWS // ROW_TILE):
            rows = slice(t * ROW_TILE, (t + 1) * ROW_TILE)
            pos = route_ref[rows, :]
            sel = jnp.where(r == pos[:, 0:1], 1.0, jnp.where(r == pos[:, 1:2], 1.0, 0.0)).astype(BF16)
            ys = ys_ref[t * LOCAL_CHUNKS:(t + 1) * LOCAL_CHUNKS].reshape(LOCAL_ROWS, D_MODEL)
            tiles.append(x_ref[rows, :] + _dot(sel, ys))
        x = jnp.concatenate(tiles, axis=0)
        x3_ref[...] = x
    else:
        x = x_ref[...]
    h = _rms(x, g_ref[...]).astype(BF16)

    def mm(lo, hi):
        return _dot_nt(h, wt_bf[lo:hi, :])

    aq_ref[...] = (mm(_R_AQ, _R_AK) * (A_HEAD_DIM ** -0.5 * LOG2E)).astype(BF16)
    ak_ref[...] = mm(_R_AK, _R_AV).astype(BF16)
    av_ref[...] = mm(_R_AV, _R_BQ).astype(BF16)
    bq_ref[...] = (mm(_R_BQ, _R_BK) * (B_KEY_DIM ** -0.5)).astype(BF16)
    bk_ref[...] = mm(_R_BK, _R_BV).astype(BF16)
    bv_ref[...] = mm(_R_BV, _R_ALPHA).astype(BF16)
    r = mm(_R_BR, _R_GATE)
    br_ref[...] = (r * jax.nn.sigmoid(r)).astype(BF16)
    for c in range(_R_GATE, _R_END, 512):
        gate_ref[:, c - _R_GATE:c - _R_GATE + 512] = jax.nn.sigmoid(mm(c, c + 512)).astype(BF16)
    z = _dot(mm(_R_ALPHA, _R_ALPHA + LANES).astype(BF16), wal2_bf[...]) + bal_ref[...]
    lga_ref[...] = (jnp.minimum(z, 0.0) - jnp.log(1.0 + jnp.exp(-jnp.abs(z)))) * (1.0 / B_GATE_TAU)


def _inproj(x, g, w_in, w_al2, b_al, layer, moe=None):
    t, d = x.shape
    assert w_in.shape[2] == _R_END
    row = lambda w: pl.BlockSpec((INPROJ_ROWS, w), lambda i: (i, 0))
    full = lambda a: pl.BlockSpec(a.shape, lambda i: (0,) * a.ndim)
    sds = lambda w, dt: jax.ShapeDtypeStruct((t, w), dt)
    widths = [(512, BF16), (512, BF16), (512, BF16), (256, BF16), (256, BF16), (512, BF16),
              (256, F32), (512, BF16), (2048, BF16)]
    fused = moe is not None
    moe_specs, moe_out_specs, moe_out_shape = [], [], []
    if fused:
        n_chunks = INPROJ_ROWS // ROW_TILE * LOCAL_CHUNKS
        moe_specs = [row(LANES), pl.BlockSpec((n_chunks, CHUNK_ROWS, d), lambda i: (i, 0, 0))]
        moe_out_specs, moe_out_shape = [row(d)], [sds(d, F32)]
    return pl.pallas_call(
        functools.partial(_inproj_kernel, layer, fused),
        grid=(t // INPROJ_ROWS,),
        in_specs=[row(d)] + moe_specs + [full(g), pl.BlockSpec(memory_space=pl.ANY),
                                         pl.BlockSpec((1,) + w_al2.shape[1:], lambda i: (layer, 0, 0)),
                                         full(b_al)],
        out_specs=moe_out_specs + [row(w) for w, _ in widths],
        out_shape=moe_out_shape + [sds(w, dt) for w, dt in widths],
        scratch_shapes=[pltpu.VMEM((_R_END, d), BF16), pltpu.VMEM((LANES, B_KEY_WIDTH), BF16),
                        pltpu.VMEM((2, _W_PIECE, d), F32), pltpu.SemaphoreType.DMA((2,))],
        compiler_params=_params("arbitrary"),
        name="inproj",
    )(x, *(moe or ()), g, jnp.swapaxes(w_in, 1, 2), w_al2, b_al)


def _band_kernel(q_ref, *refs):
    n_win = BAND_TILES_PER_STEP + BAND_TILES - 1
    k_refs, v_refs = refs[:n_win], refs[n_win:2 * n_win]
    u_ref, o_ref, bias_ref = refs[2 * n_win:]
    i = pl.program_id(1)
    lane = lax.broadcasted_iota(I32, (1, LANES), 1)
    low = lane < A_HEAD_DIM
    ones = jnp.ones((BAND_KEYS, LANES), BF16)

    @pl.when((pl.program_id(0) == 0) & (i == 0))
    def _():
        cq = lax.broadcasted_iota(I32, (ROW_TILE, BAND_KEYS), 0) >> LOG_CHUNK
        ck = lax.broadcasted_iota(I32, (ROW_TILE, BAND_KEYS), 1) >> LOG_CHUNK
        valid = (ck >= cq) & (ck <= cq + A_LEFT_CHUNKS)
        for h in range(A_HEADS):
            rows = jnp.broadcast_to(u_ref[h:h + 1, :], (ROW_TILE, BIAS_PERIOD))
            rows = pltpu.roll(rows, BIAS_PERIOD - (ROW_TILE - 1), 1, stride=1, stride_axis=0)
            bias_ref[h // 2, (h % 2) * ROW_TILE:(h % 2 + 1) * ROW_TILE, :] = jnp.where(
                valid, rows[:, :BAND_KEYS], NEG)

    def attend(tile, n_missing):
        rows = pl.ds(tile * ROW_TILE, ROW_TILE)
        window = range(tile, tile + BAND_TILES)
        for p in range(A_HEADS // 2):
            sl = slice(p * LANES, (p + 1) * LANES)
            qp = q_ref[rows, sl]
            zero = jnp.zeros_like(qp)
            q2 = jnp.concatenate([jnp.where(low, qp, zero), jnp.where(low, zero, qp)], axis=0)
            kp = jnp.concatenate([k_refs[j][:, sl] for j in window], axis=0)
            vp = jnp.concatenate([v_refs[j][:, sl] for j in window], axis=0)
            s = _dot_nt(q2, kp) + bias_ref[p]
            if n_missing:
                col = lax.broadcasted_iota(I32, (1, BAND_KEYS), 1)
                s = s + jnp.where(col < n_missing * ROW_TILE, NEG, 0.0).astype(F32)
            pe = jnp.exp2(s - jnp.max(s, axis=-1, keepdims=True)).astype(BF16)
            o2 = _dot(pe, jnp.concatenate([vp, ones], axis=1))
            o = o2[:, :LANES] * (1.0 / o2[:, LANES:])
            o_ref[rows, sl] = jnp.where(low, o[:ROW_TILE], o[ROW_TILE:]).astype(BF16)

    @pl.when(i > 0)
    def _():
        for tile in range(BAND_TILES_PER_STEP):
            attend(tile, 0)

    @pl.when(i == 0)
    def _():
        for tile in range(BAND_TILES_PER_STEP):
            attend(tile, max(BAND_TILES - 1 - tile, 0))


def _band_bias_vector(rel_table):
    h = rel_table.shape[0]
    tab = rel_table.astype(F32) * LOG2E
    shift = A_LEFT_CHUNKS * CHUNK + ROW_TILE - 1
    n_far = shift - A_MAX_REL + 1
    span = ROW_TILE + BAND_KEYS - 1
    assert span - 1 - shift <= A_MAX_REL and span <= BIAS_PERIOD
    u = jnp.concatenate([jnp.broadcast_to(tab[:, 2 * A_MAX_REL:], (h, n_far)),
                         tab[:, 2 * A_MAX_REL - 1:2 * A_MAX_REL - 1 - (span - n_far):-1]], axis=1)
    return jnp.pad(u, ((0, 0), (0, BIAS_PERIOD - span)))


def _band_attention(q, k, v, u, batch):
    t, w = q.shape
    n = BAND_TILES_PER_STEP
    nb = t // batch // (n * ROW_TILE)
    step = pl.BlockSpec((n * ROW_TILE, w), lambda b, i: (b * nb + i, 0))
    tile = lambda j: pl.BlockSpec(
        (ROW_TILE, w), lambda b, i: (n * b * nb + jnp.maximum(n * i + j - (BAND_TILES - 1), 0), 0))
    window = [tile(j) for j in range(n + BAND_TILES - 1)]
    return pl.pallas_call(
        _band_kernel,
        grid=(batch, nb),
        in_specs=[step] + window + window + [pl.BlockSpec(u.shape, lambda b, i: (0, 0))],
        out_specs=step,
        out_shape=jax.ShapeDtypeStruct((t, w), BF16),
        scratch_shapes=[pltpu.VMEM((A_HEADS // 2, 2 * ROW_TILE, BAND_KEYS), F32)],
        compiler_params=_params("arbitrary", "arbitrary"),
        name="band_attn",
    )(q, *([k] * len(window)), *([v] * len(window)), u)


def _gla_constants():
    c = CHUNK
    t = np.arange(c)[:, None]
    r = np.arange(c)[None, :]
    mats = [(r <= t), (r > t)]
    lvl = np.full((c, c), -1, np.int32)
    lvl[np.arange(c), np.arange(c)] = N_LEVELS
    for l in range(N_LEVELS):
        m = (c // 2) >> l
        mid = (t // (2 * m)) * (2 * m) + m
        upper = t >= mid
        mats.append(np.where(upper, (r >= mid) & (r <= t), (r > t) & (r < mid)))
        s = r
        same = (s // (2 * m)) == (t // (2 * m))
        lvl[np.asarray(same & upper & (s < mid))] = l
    eye = np.eye(CHUNKS_PER_TILE)
    mexp = np.concatenate([np.kron(eye, m) for m in mats], axis=0).astype(np.float32)
    lvl = np.tile(lvl, (1, B_HEADS))
    return jnp.asarray(mexp, BF16), jnp.asarray(lvl, I32)


def _gla_kernel(q_ref, k_ref, v_ref, g_ref, r_ref, gn_ref, mexp_ref, lvl_ref, o_ref, s_ref):
    @pl.when(pl.program_id(1) == 0)
    def _():
        s_ref[...] = jnp.zeros_like(s_ref)

    kw = B_KEY_WIDTH
    ri = lax.broadcasted_iota(I32, (kw, kw), 0) >> LOG_CHUNK
    ci = lax.broadcasted_iota(I32, (kw, kw), 1) >> LOG_CHUNK
    bd = ri == ci
    head_ind = jnp.where(bd, 1.0, 0.0).astype(BF16)
    ri2 = lax.broadcasted_iota(I32, (kw, 2 * kw), 0) >> LOG_CHUNK
    ci2 = (lax.broadcasted_iota(I32, (kw, 2 * kw), 1) & (kw - 1)) >> LOG_CHUNK
    bd2 = ri2 == ci2
    lvl = lvl_ref[...]
    row8 = lax.broadcasted_iota(I32, (16, kw), 0)
    ones = jnp.ones((16, LANES), BF16)
    zero_b = jnp.zeros((kw, kw), BF16)
    chunks = [slice(c * CHUNK, (c + 1) * CHUNK) for c in range(CHUNKS_PER_TILE)]

    def head_blocks(x):
        return jnp.where(bd, jnp.concatenate([x] * B_HEADS, axis=0), zero_b)

    def prepare(tile):
        trows = pl.ds(tile * ROW_TILE, ROW_TILE)
        q = q_ref[trows, :].astype(F32)
        k = k_ref[trows, :].astype(F32)
        g = g_ref[trows, :]
        gb = g.astype(BF16)
        half = EXP_ROWS * CHUNKS_PER_TILE // 2
        w = jnp.exp(jnp.concatenate([_dot(mexp_ref[:half, :], gb), _dot(mexp_ref[half:, :], gb)], axis=0))
        qt = (q * w[0:ROW_TILE]).astype(BF16)
        kb = (k * w[ROW_TILE:2 * ROW_TILE]).astype(BF16)
        qk = (q * k).astype(BF16)

        attn = [jnp.zeros((CHUNK, kw), F32) for _ in chunks]
        for l in range(N_LEVELS):
            wl = w[(2 + l) * ROW_TILE:(3 + l) * ROW_TILE]
            qh = (q * wl).astype(BF16)
            kh = (k * wl).astype(BF16)
            for c, rows in enumerate(chunks):
                attn[c] = jnp.where(lvl == l, _dot_nt(qh[rows], head_blocks(kh[rows])), attn[c])

        out = []
        for c, rows in enumerate(chunks):
            a = jnp.where(lvl == N_LEVELS, _dot(qk[rows], head_ind), attn[c])
            v = v_ref[pl.ds(tile * ROW_TILE + c * CHUNK, CHUNK), :]
            vstack = jnp.concatenate([v[:, j * LANES:(j + 1) * LANES] for j in range(B_HEADS)], axis=0)
            kv = _dot_tn(head_blocks(kb[rows]), vstack)
            d = jnp.exp(jnp.sum(g[rows], axis=0, keepdims=True))
            d1 = d.astype(BF16).astype(F32)
            dp = jnp.where(row8 == 0, d1, jnp.where(row8 == 1, d - d1, 0.0)).astype(BF16)
            dcol = _dot_tn(dp, ones)
            out.append((a.astype(BF16), qt[rows], vstack, kv, dcol))
        return out

    prepared = [p for tile in range(GLA_TILES_PER_STEP) for p in prepare(tile)]

    s = s_ref[...]
    for c, (a, qtc, vstack, kv, dcol) in enumerate(prepared):
        rows = pl.ds(c * CHUNK, CHUNK)
        lhs = jnp.concatenate([a, qtc], axis=1)
        lhs = jnp.where(bd2, jnp.concatenate([lhs] * B_HEADS, axis=0), jnp.zeros((kw, 2 * kw), BF16))
        rhs = jnp.concatenate([vstack, s.astype(BF16)], axis=0)
        o = _dot(lhs, rhs)
        s = dcol * s + kv
        for j in range(B_HEADS):
            oj = o[j * CHUNK:(j + 1) * CHUNK]
            sl = slice(j * LANES, (j + 1) * LANES)
            y = oj * lax.rsqrt(jnp.mean(oj * oj, axis=-1, keepdims=True) + EPS) * gn_ref[...]
            o_ref[rows, sl] = (y * r_ref[rows, sl].astype(F32)).astype(BF16)
    s_ref[...] = s


def _gla(q, k, v, g, r, gn, batch):
    t = q.shape[0]
    nb = t // batch // (GLA_TILES_PER_STEP * ROW_TILE)
    mexp, lvl = _gla_constants()
    cur = lambda b, i: (b * nb + i, 0)
    blk = lambda w: pl.BlockSpec((GLA_TILES_PER_STEP * ROW_TILE, w), cur)
    full = lambda a: pl.BlockSpec(a.shape, lambda b, i: (0,) * a.ndim)
    return pl.pallas_call(
        _gla_kernel,
        grid=(batch, nb),
        in_specs=[blk(B_KEY_WIDTH), blk(B_KEY_WIDTH), blk(B_VAL_WIDTH), blk(B_KEY_WIDTH), blk(B_VAL_WIDTH),
                  full(gn), full(mexp), full(lvl)],
        out_specs=blk(B_VAL_WIDTH),
        out_shape=jax.ShapeDtypeStruct((t, B_VAL_WIDTH), BF16),
        scratch_shapes=[pltpu.VMEM((B_KEY_WIDTH, B_VAL_DIM), F32)],
        compiler_params=_params("arbitrary", "arbitrary"),
        name="gla",
    )(q, k, v, g, r, gn, mexp, lvl)


def _token_kernel(x_ref, oa_ref, ob_ref, gate_ref, wb0_ref, wb1_ref, wmix_ref, gx_ref, wq_ref,
                  km_ref, vm_ref, wo_ref, gf_ref, wr_ref, br_ref, ltri_ref, utri_ref,
                  x2_ref, hs_ref, route_ref, cnt_ref):
    ma = _dot(oa_ref[...], wb0_ref[...])
    mb = _dot(ob_ref[...], wb1_ref[...])
    merged = (gate_ref[:, :D_MODEL].astype(F32) * ma + gate_ref[:, D_MODEL:].astype(F32) * mb).astype(BF16)
    x1 = x_ref[...] + _dot(merged, wmix_ref[...])

    h2 = _rms(x1, gx_ref[...]).astype(BF16)
    qx = (_dot(h2, wq_ref[...]) * (X_HEAD_DIM ** -0.5)).astype(BF16)
    heads = []
    for h in range(X_HEADS):
        sl = slice(h * X_HEAD_DIM, (h + 1) * X_HEAD_DIM)
        s = _dot_nt(qx[:, sl], km_ref[0, :, sl])
        m = jnp.max(s, axis=-1, keepdims=True)
        pe = jnp.exp(s - m)
        l = jnp.sum(pe, axis=-1, keepdims=True)
        heads.append((_dot(pe.astype(BF16), vm_ref[0, :, sl]) * (1.0 / l)).astype(BF16))
    x2 = x1 + _dot(jnp.concatenate(heads, axis=1), wo_ref[...])
    x2_ref[...] = x2

    h3 = _rms(x2, gf_ref[...])

    h3_hi = h3.astype(BF16)
    h3_lo = (h3 - h3_hi.astype(F32)).astype(BF16)
    hw = _dot(h3_hi, wr_ref[...])
    logits = hw[:, :LANES] + hw[:, LANES:] + _dot(h3_lo, wr_ref[:, :LANES]) + br_ref[...]
    for h in range(TOKEN_TILES_PER_STEP):
        rows = slice(h * ROW_TILE, (h + 1) * ROW_TILE)
        chunks = pl.ds(h * LOCAL_CHUNKS, LOCAL_CHUNKS)
        _route_and_sort(logits[rows], h3_hi[rows], ltri_ref, utri_ref, hs_ref.at[chunks],
                        route_ref.at[pl.ds(h * ROW_TILE, ROW_TILE)], cnt_ref.at[h])


def _route_and_sort(logits, h3_hi, ltri_ref, utri_ref, hs_ref, route_ref, cnt_ref):
    lane = lax.broadcasted_iota(I32, logits.shape, 1).astype(F32)
    big = jnp.float32(LANES)
    gl = jnp.where(lane < N_GROUPS, logits, NEG)
    gmax = jnp.max(gl, axis=-1, keepdims=True)
    gidx = jnp.min(jnp.where(gl == gmax, lane, big), axis=-1, keepdims=True)
    g_w = 1.0 / jnp.sum(jnp.exp(gl - gmax), axis=-1, keepdims=True)
    lo = N_GROUPS + EXPERTS_PER_GROUP * gidx
    el = jnp.where((lane >= lo) & (lane < lo + EXPERTS_PER_GROUP), logits, NEG)
    v1 = jnp.max(el, axis=-1, keepdims=True)
    i1 = jnp.min(jnp.where(el == v1, lane, big), axis=-1, keepdims=True)
    el2 = jnp.where(lane == i1, NEG, el)
    v2 = jnp.max(el2, axis=-1, keepdims=True)
    i2 = jnp.min(jnp.where(el2 == v2, lane, big), axis=-1, keepdims=True)
    e21 = jnp.exp(v2 - v1)
    w1 = g_w / (1.0 + e21)
    w2 = w1 * e21
    oh0 = jnp.where(lane == i1 - N_GROUPS, 1.0, 0.0)
    oh1 = jnp.where(lane == i2 - N_GROUPS, 1.0, 0.0)
    oh = oh0 + oh1
    nch = jnp.floor((jnp.sum(oh, axis=0, keepdims=True) + (CHUNK_ROWS - 1)) * (1.0 / CHUNK_ROWS))
    nch8 = jnp.broadcast_to(nch, (8, LANES))
    start = _dot(nch8.astype(BF16), utri_ref[...])[0:1] * CHUNK_ROWS
    rank = _dot(ltri_ref[...], oh.astype(BF16))
    row = start + rank
    pos0 = jnp.sum(row * oh0, axis=-1, keepdims=True)
    pos1 = jnp.sum(row * oh1, axis=-1, keepdims=True)
    route = jnp.where(lane == 0, pos0, jnp.where(lane == 1, pos1, 0.0))
    route_t = jnp.transpose(route)
    r = lax.broadcasted_iota(I32, (LOCAL_ROWS, ROW_TILE), 0).astype(F32)
    p0 = jnp.where(r == route_t[0:1, :], 1.0, 0.0).astype(BF16)
    p1 = jnp.where(r == route_t[1:2, :], 1.0, 0.0).astype(BF16)

    def gate_cols(w):
        hi = w.astype(BF16).astype(F32)
        return jnp.where(lane == 0, hi, jnp.where(lane == 1, w - hi, 0.0)).astype(BF16)

    sorted_rows = jnp.concatenate([_dot(p0 + p1, h3_hi), _dot(p0, gate_cols(w1)) + _dot(p1, gate_cols(w2))],
                                  axis=1)
    hs_ref[...] = sorted_rows.astype(BF16).reshape(hs_ref.shape)
    route_ref[...] = route
    cnt_ref[...] = nch8


def _token(x, oa, ob, gates, wb0, wb1, wmix, gx, wq, km, vm, wo, gf, wr, br, batch):
    t, d = x.shape
    n = TOKEN_TILES_PER_STEP
    nb = t // batch // (n * ROW_TILE)
    nt = t // ROW_TILE
    ltri = jnp.asarray(np.tril(np.ones((ROW_TILE, ROW_TILE), np.float32), -1), BF16)
    utri = jnp.asarray(np.triu(np.ones((LANES, LANES), np.float32), 1), BF16)
    cur = lambda b, i: (b * nb + i, 0)
    cur3 = lambda b, i: (b * nb + i, 0, 0)
    blk = lambda w: pl.BlockSpec((n * ROW_TILE, w), cur)
    full = lambda a: pl.BlockSpec(a.shape, lambda b, i: (0,) * a.ndim)
    mem = pl.BlockSpec((1,) + km.shape[1:], lambda b, i: (b, 0, 0))
    return pl.pallas_call(
        _token_kernel,
        grid=(batch, nb),
        in_specs=[blk(d), blk(A_WIDTH), blk(B_VAL_WIDTH), blk(2 * d), full(wb0), full(wb1), full(wmix),
                  full(gx), full(wq), mem, mem, full(wo), full(gf), full(wr), full(br), full(ltri), full(utri)],
        out_specs=[blk(d), pl.BlockSpec((n * LOCAL_CHUNKS, CHUNK_ROWS, SORT_WIDTH), cur3),
                   blk(LANES), pl.BlockSpec((n, 8, LANES), cur3)],
        out_shape=[jax.ShapeDtypeStruct((t, d), F32),
                   jax.ShapeDtypeStruct((nt * LOCAL_CHUNKS, CHUNK_ROWS, SORT_WIDTH), BF16),
                   jax.ShapeDtypeStruct((t, LANES), F32),
                   jax.ShapeDtypeStruct((nt, 8, LANES), F32)],
        compiler_params=_params("arbitrary", "arbitrary"),
        name="token",
    )(x, oa, ob, gates, wb0, wb1, wmix, gx, wq, km, vm, wo, gf, wr, br, ltri, utri)


def _expert_kernel(te_ref, nu_ref, nv_ref, ch_ref, hs_hbm, wg_ref, wu_ref, wd_ref, ys_hbm,
                   xbuf, ybuf, wg_bf, wu_bf, wd_bf, gsem, ssem):
    i = pl.program_id(0)
    n_used = nu_ref[0]
    slot = lax.rem(i, 2)

    def for_chunks(tile, fn):
        nv = nv_ref[tile]

        @pl.when(nv == TILE_CHUNKS)
        def _():
            for c in range(TILE_CHUNKS):
                fn(c)

        @pl.when(nv != TILE_CHUNKS)
        def _():
            def body(c, carry):
                fn(c)
                return carry

            lax.fori_loop(0, nv, body, 0)

    def gather(tile, s, start):
        for c in range(TILE_CHUNKS):
            cp = pltpu.make_async_copy(hs_hbm.at[ch_ref[tile * TILE_CHUNKS + c]], xbuf.at[s, c], gsem.at[s])
            cp.start() if start else cp.wait()

    def scatter(tile, s, start):
        def one(c):
            cp = pltpu.make_async_copy(ybuf.at[s, c], ys_hbm.at[ch_ref[tile * TILE_CHUNKS + c]], ssem.at[s])
            cp.start() if start else cp.wait()

        for_chunks(tile, one)

    next_tile = jnp.minimum(i + 1, pl.num_programs(0) - 1)

    @pl.when(i == 0)
    def _():
        gather(0, 0, True)

    @pl.when(i < n_used)
    def _():
        gather(next_tile, 1 - slot, True)
        gather(i, slot, False)

        @pl.when(i >= 2)
        def _():
            scatter(i - 2, slot, False)

        @pl.when((i == 0) | (te_ref[i] != te_ref[jnp.maximum(i - 1, 0)]))
        def _():
            wg_bf[...] = wg_ref[0].astype(BF16)
            wu_bf[...] = wu_ref[0].astype(BF16)
            wd_bf[...] = wd_ref[0].astype(BF16)

        xg = xbuf[slot].reshape(EXPERT_ROWS, SORT_WIDTH)
        x = xg[:, :D_MODEL]
        hg = _dot(x, wg_bf[...])
        hu = _dot(x, wu_bf[...])
        hid = (hg * jax.nn.sigmoid(hg) * hu).astype(BF16)
        g = xg[:, D_MODEL:].astype(F32)
        y = ((g[:, 0:1] + g[:, 1:2]) * _dot(hid, wd_bf[...])).astype(BF16)
        y = jnp.concatenate([y, jnp.zeros((EXPERT_ROWS, LANES), BF16)], axis=1)
        ybuf[slot] = y.reshape(TILE_CHUNKS, CHUNK_ROWS, SORT_WIDTH)
        scatter(i, slot, True)

        @pl.when(i == n_used - 1)
        def _():
            gather(next_tile, 1 - slot, False)
            scatter(i, slot, False)

            @pl.when(i >= 1)
            def _():
                scatter(i - 1, 1 - slot, False)


def _experts(hs, tile_expert, n_used, n_valid, chunks, wg, wu, wd, layer):
    n_tiles = tile_expert.shape[0]
    last = lambda i, te, nu, nv, ch: jnp.minimum(i, nu[0] - 1)
    wmap = lambda i, te, nu, nv, ch: (layer * N_EXPERTS + te[last(i, te, nu, nv, ch)], 0, 0)
    anyspace = pl.BlockSpec(memory_space=pl.ANY)
    grid_spec = pltpu.PrefetchScalarGridSpec(
        num_scalar_prefetch=4,
        grid=(n_tiles,),
        in_specs=[anyspace,
                  pl.BlockSpec((1, D_MODEL, EXPERT_FF), wmap),
                  pl.BlockSpec((1, D_MODEL, EXPERT_FF), wmap),
                  pl.BlockSpec((1, EXPERT_FF, D_MODEL), wmap)],
        out_specs=anyspace,
        scratch_shapes=[pltpu.VMEM((2, TILE_CHUNKS, CHUNK_ROWS, SORT_WIDTH), BF16),
                        pltpu.VMEM((2, TILE_CHUNKS, CHUNK_ROWS, SORT_WIDTH), BF16),
                        pltpu.VMEM((D_MODEL, EXPERT_FF), BF16), pltpu.VMEM((D_MODEL, EXPERT_FF), BF16),
                        pltpu.VMEM((EXPERT_FF, D_MODEL), BF16),
                        pltpu.SemaphoreType.DMA((2,)), pltpu.SemaphoreType.DMA((2,))],
    )
    return pl.pallas_call(
        _expert_kernel,
        grid_spec=grid_spec,
        out_shape=jax.ShapeDtypeStruct(hs.shape, BF16),
        input_output_aliases={4: 0},
        compiler_params=_params("arbitrary"),
        name="experts",
    )(tile_expert, n_used, n_valid, chunks, hs, wg, wu, wd)


def _combine_kernel(x_ref, route_ref, ys_ref, gfin_ref, o_ref):
    r = lax.broadcasted_iota(I32, (ROW_TILE, LOCAL_ROWS), 1).astype(F32)
    for t in range(COMBINE_TILES_PER_STEP):
        rows = slice(t * ROW_TILE, (t + 1) * ROW_TILE)
        pos = route_ref[rows, :]
        sel = jnp.where(r == pos[:, 0:1], 1.0, jnp.where(r == pos[:, 1:2], 1.0, 0.0)).astype(BF16)
        ys = ys_ref[t * LOCAL_CHUNKS:(t + 1) * LOCAL_CHUNKS].reshape(LOCAL_ROWS, D_MODEL)
        o_ref[rows, :] = _rms(x_ref[rows, :] + _dot(sel, ys), gfin_ref[...])


def _combine(x2, route, ys, gfin):
    t, d = x2.shape
    n = COMBINE_TILES_PER_STEP
    return pl.pallas_call(
        _combine_kernel,
        grid=(t // (n * ROW_TILE),),
        in_specs=[pl.BlockSpec((n * ROW_TILE, d), lambda i: (i, 0)),
                  pl.BlockSpec((n * ROW_TILE, LANES), lambda i: (i, 0)),
                  pl.BlockSpec((n * LOCAL_CHUNKS, CHUNK_ROWS, d), lambda i: (i, 0, 0)),
                  pl.BlockSpec((1, d), lambda i: (0, 0))],
        out_specs=pl.BlockSpec((n * ROW_TILE, d), lambda i: (i, 0)),
        out_shape=jax.ShapeDtypeStruct((t, d), F32),
        compiler_params=_params("arbitrary"),
        name="combine",
    )(x2, route, ys, gfin)


def _chunk_plan(nch, n_tiles):
    nt = nch.shape[0]
    local_start = jnp.cumsum(nch, axis=1) - nch
    cum = jnp.cumsum(nch, axis=0)
    total = cum[-1]
    tiles = (total + TILE_CHUNKS - 1) // TILE_CHUNKS
    tile_end = jnp.cumsum(tiles)
    n_used = tile_end[-1:]
    tile_ids = jnp.arange(n_tiles, dtype=I32)
    tile_expert = jnp.minimum(jnp.sum((tile_end[None, :] <= tile_ids[:, None]).astype(I32), axis=1),
                              N_EXPERTS - 1)
    sel = (tile_expert[:, None] == jnp.arange(N_EXPERTS, dtype=I32)[None, :]).astype(I32)
    pick = lambda table: jnp.sum(sel[:, :, None] * table.T[None, :, :], axis=1)
    first_tile = jnp.sum(sel * (tile_end - tiles)[None, :], axis=1)
    slot = (tile_ids - first_tile)[:, None] * TILE_CHUNKS + jnp.arange(TILE_CHUNKS, dtype=I32)[None, :]
    valid = (slot < jnp.sum(sel * total[None, :], axis=1)[:, None]) & (tile_ids < n_used)[:, None]
    src_tile = jnp.sum((pick(cum)[:, None, :] <= slot[:, :, None]).astype(I32), axis=2)
    src_tile = jnp.minimum(src_tile, nt - 1)
    at = (src_tile[:, :, None] == jnp.arange(nt, dtype=I32)[None, None, :]).astype(I32)
    before = jnp.sum(at * pick(cum - nch)[:, None, :], axis=2)
    start = jnp.sum(at * pick(local_start)[:, None, :], axis=2)
    chunk = jnp.where(valid, src_tile * LOCAL_CHUNKS + start + slot - before, LOCAL_CHUNKS - 1)
    return tile_expert, n_used, jnp.sum(valid.astype(I32), axis=1), chunk.reshape(-1)


def kernel(x, mem, norm_mix_g, w_in, rel_bias, gla_w_alpha, gla_b_alpha, gla_norm_g, w_branch, w_mix_out, norm_x_g, mem_norm_g, w_xq, w_xkv, w_xo, norm_ffn_g, w_group_router, b_group_router, w_expert_router, b_expert_router, w_exp_gate, w_exp_up, w_exp_down, final_norm_g):
    batch, seq, d = x.shape
    depth = w_in.shape[0]
    t = batch * seq
    assert d == D_MODEL and seq % (TOKEN_TILES_PER_STEP * ROW_TILE) == 0
    nt = t // ROW_TILE
    n_tiles = nt * LOCAL_CHUNKS // TILE_CHUNKS + N_EXPERTS

    xf = x.reshape(t, d)
    km_all, vm_all = _memkv(mem, mem_norm_g, w_xkv.astype(BF16))
    row = lambda a: a.reshape(1, -1).astype(F32)

    moe = None
    for l in range(depth):
        res = _inproj(xf, row(norm_mix_g[l]), w_in, gla_w_alpha, row(gla_b_alpha[l]), l, moe)
        if moe is not None:
            xf, res = res[0], res[1:]
        aq, ak, av, bq, bk, bv, lga, br, gates = res

        oa = _band_attention(aq, ak, av, _band_bias_vector(rel_bias[l]), batch)
        ob = _gla(bq, bk, bv, lga, br, row(gla_norm_g[l]), batch)

        wr = jnp.pad(jnp.concatenate([w_group_router[l], w_expert_router[l]], axis=1).astype(F32),
                     ((0, 0), (0, LANES - N_GROUPS - N_EXPERTS)))
        wr_hi = wr.astype(BF16)
        wr = jnp.concatenate([wr_hi, (wr - wr_hi.astype(F32)).astype(BF16)], axis=1)
        brt = jnp.pad(jnp.concatenate([b_group_router[l], b_expert_router[l]]).astype(F32),
                      (0, LANES - N_GROUPS - N_EXPERTS)).reshape(1, LANES)
        x2, hs, route, cnt = _token(
            xf, oa, ob, gates, w_branch[l, 0].astype(BF16), w_branch[l, 1].astype(BF16),
            w_mix_out[l].astype(BF16), row(norm_x_g[l]), w_xq[l].astype(BF16), km_all[l], vm_all[l],
            w_xo[l].astype(BF16), row(norm_ffn_g[l]), wr, brt, batch)

        plan = _chunk_plan(cnt[:, 0, :N_EXPERTS].astype(I32), n_tiles)
        e3 = lambda w: w.reshape((depth * N_EXPERTS,) + w.shape[3:])
        ys = _experts(hs, *plan, e3(w_exp_gate), e3(w_exp_up), e3(w_exp_down), l)
        xf, moe = x2, (route, ys)

    return _combine(x2, route, ys, row(final_norm_g)).reshape(batch, seq, d)
```

```python
import functools

import numpy as np
import jax
import jax.numpy as jnp
from jax import lax
from jax.experimental import pallas as pl
from jax.experimental.pallas import tpu as pltpu

F32 = jnp.float32
BF16 = jnp.bfloat16
I32 = jnp.int32

D_MODEL = 1024
CHUNK = 64
EPS = 1e-6
A_HEADS = 8
A_HEAD_DIM = 64
A_WIDTH = 512
A_LEFT_CHUNKS = 8
A_MAX_REL = 256
B_HEADS = 4
B_KEY_DIM = 64
B_VAL_DIM = 128
B_KEY_WIDTH = 256
B_VAL_WIDTH = 512
B_GATE_RANK = 16
B_GATE_TAU = 16.0
X_HEADS = 4
X_HEAD_DIM = 256
N_GROUPS = 4
EXPERTS_PER_GROUP = 8
N_EXPERTS = N_GROUPS * EXPERTS_PER_GROUP
EXPERT_FF = 256

LANES = 128
ROW_TILE = 256
CHUNKS_PER_TILE = ROW_TILE // CHUNK
BAND_TILES = A_LEFT_CHUNKS // CHUNKS_PER_TILE + 1
BAND_KEYS = BAND_TILES * ROW_TILE
BIAS_PERIOD = 1024
LOG_CHUNK = 6
N_LEVELS = LOG_CHUNK
EXP_ROWS = (2 + N_LEVELS) * CHUNK
CHUNK_ROWS = 16
TOKEN_TILES_PER_STEP = 2
INPROJ_ROWS = 512
COMBINE_TILES_PER_STEP = 2
GLA_TILES_PER_STEP = 4
BAND_TILES_PER_STEP = 4
EXPERT_ROWS = 512
TILE_CHUNKS = EXPERT_ROWS // CHUNK_ROWS
LOCAL_CHUNKS = 2 * ROW_TILE // CHUNK_ROWS + N_EXPERTS
LOCAL_ROWS = LOCAL_CHUNKS * CHUNK_ROWS
assert (2 * ROW_TILE + N_EXPERTS * (CHUNK_ROWS - 1)) // CHUNK_ROWS < LOCAL_CHUNKS
SORT_WIDTH = D_MODEL + LANES
NEG = -1e30
LOG2E = 1.4426950408889634
VMEM_LIMIT = 56 * 1024 * 1024


def _params(*sem):
    return pltpu.CompilerParams(dimension_semantics=sem, vmem_limit_bytes=VMEM_LIMIT)


def _rms(x, g):
    return x * lax.rsqrt(jnp.mean(x * x, axis=-1, keepdims=True) + EPS) * g


def _dot(a, b):
    return jnp.dot(a, b, preferred_element_type=F32)


def _dot_nt(a, b):
    return lax.dot_general(a, b, (((1,), (1,)), ((), ())), preferred_element_type=F32)


def _dot_tn(a, b):
    return lax.dot_general(a, b, (((0,), (0,)), ((), ())), preferred_element_type=F32)


def _memkv_kernel(mem_ref, g_ref, w_ref, k_ref, v_ref):
    mn = _rms(mem_ref[0], g_ref[...]).astype(BF16)
    kv = _dot(mn, w_ref[0])
    k_ref[0, 0] = kv[:, :D_MODEL].astype(BF16)
    v_ref[0, 0] = kv[:, D_MODEL:].astype(BF16)


def _memkv(mem, g, w_xkv):
    depth = w_xkv.shape[0]
    b, m, d = mem.shape
    out = jax.ShapeDtypeStruct((depth, b, m, d), BF16)
    return pl.pallas_call(
        _memkv_kernel,
        grid=(depth, b),
        in_specs=[pl.BlockSpec((1, m, d), lambda l, i: (i, 0, 0)),
                  pl.BlockSpec((1, d), lambda l, i: (0, 0)),
                  pl.BlockSpec((1, d, 2 * d), lambda l, i: (l, 0, 0))],
        out_specs=[pl.BlockSpec((1, 1, m, d), lambda l, i: (l, i, 0, 0)),
                   pl.BlockSpec((1, 1, m, d), lambda l, i: (l, i, 0, 0))],
        out_shape=[out, out],
        compiler_params=_params("arbitrary", "arbitrary"),
        name="memkv",
    )(mem, g.reshape(1, d), w_xkv)


_R_AQ, _R_AK, _R_AV = 0, 512, 1024
_R_BQ, _R_BK, _R_BV = 1536, 1792, 2048
_R_ALPHA, _R_BR, _R_GATE, _R_END = 2560, 2576, 3088, 5136
_W_PIECE = 512


def _inproj_kernel(layer, fused, *refs):
    if fused:
        x_ref, route_ref, ys_ref, g_ref, wt_hbm, wal2_ref, bal_ref = refs[:7]
        refs = refs[7:]
        x3_ref, refs = refs[0], refs[1:]
    else:
        x_ref, g_ref, wt_hbm, wal2_ref, bal_ref = refs[:5]
        refs = refs[5:]
    (aq_ref, ak_ref, av_ref, bq_ref, bk_ref, bv_ref, lga_ref, br_ref, gate_ref,
     wt_bf, wal2_bf, stage, sem) = refs

    @pl.when(pl.program_id(0) == 0)
    def _():
        pieces = [(c, min(_W_PIECE, _R_END - c)) for c in range(0, _R_END, _W_PIECE)]

        def piece_copy(p):
            c, n = pieces[p]
            return pltpu.make_async_copy(wt_hbm.at[layer, pl.ds(c, n), :], stage.at[p % 2, pl.ds(0, n), :],
                                         sem.at[p % 2])

        piece_copy(0).start()
        for p, (c, n) in enumerate(pieces):
            if p + 1 < len(pieces):
                piece_copy(p + 1).start()
            piece_copy(p).wait()
            wt_bf[c:c + n, :] = stage[p % 2, 0:n, :].astype(BF16)
        wal2_bf[...] = jnp.concatenate(
            [wal2_ref[0].astype(BF16), jnp.zeros((LANES - B_GATE_RANK, B_KEY_WIDTH), BF16)], axis=0)

    if fused:
        r = lax.broadcasted_iota(I32, (ROW_TILE, LOCAL_ROWS), 1).astype(F32)
        tiles = []
        for t in range(INPROJ_ROWS // ROW_TILE):
            rows = slice(t * ROW_TILE, (t + 1) * ROW_TILE)
            pos = route_ref[rows, :]
            sel = jnp.where(r == pos[:, 0:1], 1.0, jnp.where(r == pos[:, 1:2], 1.0, 0.0)).astype(BF16)
            ys = ys_ref[t * LOCAL_CHUNKS:(t + 1) * LOCAL_CHUNKS].reshape(LOCAL_ROWS, D_MODEL)
            tiles.append(x_ref[rows, :] + _dot(sel, ys))
        x = jnp.concatenate(tiles, axis=0)
        x3_ref[...] = x
    else:
        x = x_ref[...]
    h = _rms(x, g_ref[...]).astype(BF16)

    def mm(lo, hi):
        return _dot_nt(h, wt_bf[lo:hi, :])

    aq_ref[...] = (mm(_R_AQ, _R_AK) * (A_HEAD_DIM ** -0.5 * LOG2E)).astype(BF16)
    ak_ref[...] = mm(_R_AK, _R_AV).astype(BF16)
    av_ref[...] = mm(_R_AV, _R_BQ).astype(BF16)
    bq_ref[...] = (mm(_R_BQ, _R_BK) * (B_KEY_DIM ** -0.5)).astype(BF16)
    bk_ref[...] = mm(_R_BK, _R_BV).astype(BF16)
    bv_ref[...] = mm(_R_BV, _R_ALPHA).astype(BF16)
    r = mm(_R_BR, _R_GATE)
    br_ref[...] = (r * jax.nn.sigmoid(r)).astype(BF16)
    for c in range(_R_GATE, _R_END, 512):
        gate_ref[:, c - _R_GATE:c - _R_GATE + 512] = jax.nn.sigmoid(mm(c, c + 512)).astype(BF16)
    z = _dot(mm(_R_ALPHA, _R_ALPHA + LANES).astype(BF16), wal2_bf[...]) + bal_ref[...]
    lga_ref[...] = (jnp.minimum(z, 0.0) - jnp.log(1.0 + jnp.exp(-jnp.abs(z)))) * (1.0 / B_GATE_TAU)


def _inproj(x, g, w_in, w_al2, b_al, layer, moe=None):
    t, d = x.shape
    assert w_in.shape[2] == _R_END
    row = lambda w: pl.BlockSpec((INPROJ_ROWS, w), lambda i: (i, 0))
    full = lambda a: pl.BlockSpec(a.shape, lambda i: (0,) * a.ndim)
    sds = lambda w, dt: jax.ShapeDtypeStruct((t, w), dt)
    widths = [(512, BF16), (512, BF16), (512, BF16), (256, BF16), (256, BF16), (512, BF16),
              (256, F32), (512, BF16), (2048, BF16)]
    fused = moe is not None
    moe_specs, moe_out_specs, moe_out_shape = [], [], []
    if fused:
        n_chunks = INPROJ_ROWS // ROW_TILE * LOCAL_CHUNKS
        moe_specs = [row(LANES), pl.BlockSpec((n_chunks, CHUNK_ROWS, d), lambda i: (i, 0, 0))]
        moe_out_specs, moe_out_shape = [row(d)], [sds(d, F32)]
    return pl.pallas_call(
        functools.partial(_inproj_kernel, layer, fused),
        grid=(t // INPROJ_ROWS,),
        in_specs=[row(d)] + moe_specs + [full(g), pl.BlockSpec(memory_space=pl.ANY),
                                         pl.BlockSpec((1,) + w_al2.shape[1:], lambda i: (layer, 0, 0)),
                                         full(b_al)],
        out_specs=moe_out_specs + [row(w) for w, _ in widths],
        out_shape=moe_out_shape + [sds(w, dt) for w, dt in widths],
        scratch_shapes=[pltpu.VMEM((_R_END, d), BF16), pltpu.VMEM((LANES, B_KEY_WIDTH), BF16),
                        pltpu.VMEM((2, _W_PIECE, d), F32), pltpu.SemaphoreType.DMA((2,))],
        compiler_params=_params("arbitrary"),
        name="inproj",
    )(x, *(moe or ()), g, jnp.swapaxes(w_in, 1, 2), w_al2, b_al)


def _band_kernel(q_ref, *refs):
    n_win = BAND_TILES_PER_STEP + BAND_TILES - 1
    k_refs, v_refs = refs[:n_win], refs[n_win:2 * n_win]
    u_ref, o_ref, bias_ref = refs[2 * n_win:]
    i = pl.program_id(1)
    lane = lax.broadcasted_iota(I32, (1, LANES), 1)
    low = lane < A_HEAD_DIM
    ones = jnp.ones((BAND_KEYS, LANES), BF16)

    @pl.when((pl.program_id(0) == 0) & (i == 0))
    def _():
        cq = lax.broadcasted_iota(I32, (ROW_TILE, BAND_KEYS), 0) >> LOG_CHUNK
        ck = lax.broadcasted_iota(I32, (ROW_TILE, BAND_KEYS), 1) >> LOG_CHUNK
        valid = (ck >= cq) & (ck <= cq + A_LEFT_CHUNKS)
        for h in range(A_HEADS):
            rows = jnp.broadcast_to(u_ref[h:h + 1, :], (ROW_TILE, BIAS_PERIOD))
            rows = pltpu.roll(rows, BIAS_PERIOD - (ROW_TILE - 1), 1, stride=1, stride_axis=0)
            bias_ref[h // 2, (h % 2) * ROW_TILE:(h % 2 + 1) * ROW_TILE, :] = jnp.where(
                valid, rows[:, :BAND_KEYS], NEG)

    def attend(tile, n_missing):
        rows = pl.ds(tile * ROW_TILE, ROW_TILE)
        window = range(tile, tile + BAND_TILES)
        for p in range(A_HEADS // 2):
            sl = slice(p * LANES, (p + 1) * LANES)
            qp = q_ref[rows, sl]
            zero = jnp.zeros_like(qp)
            q2 = jnp.concatenate([jnp.where(low, qp, zero), jnp.where(low, zero, qp)], axis=0)
            kp = jnp.concatenate([k_refs[j][:, sl] for j in window], axis=0)
            vp = jnp.concatenate([v_refs[j][:, sl] for j in window], axis=0)
            s = _dot_nt(q2, kp) + bias_ref[p]
            if n_missing:
                col = lax.broadcasted_iota(I32, (1, BAND_KEYS), 1)
                s = s + jnp.where(col < n_missing * ROW_TILE, NEG, 0.0).astype(F32)
            pe = jnp.exp2(s - jnp.max(s, axis=-1, keepdims=True)).astype(BF16)
            o2 = _dot(pe, jnp.concatenate([vp, ones], axis=1))
            o = o2[:, :LANES] * (1.0 / o2[:, LANES:])
            o_ref[rows, sl] = jnp.where(low, o[:ROW_TILE], o[ROW_TILE:]).astype(BF16)

    @pl.when(i > 0)
    def _():
        for tile in range(BAND_TILES_PER_STEP):
            attend(tile, 0)

    @pl.when(i == 0)
    def _():
        for tile in range(BAND_TILES_PER_STEP):
            attend(tile, max(BAND_TILES - 1 - tile, 0))


def _band_bias_vector(rel_table):
    h = rel_table.shape[0]
    tab = rel_table.astype(F32) * LOG2E
    shift = A_LEFT_CHUNKS * CHUNK + ROW_TILE - 1
    n_far = shift - A_MAX_REL + 1
    span = ROW_TILE + BAND_KEYS - 1
    assert span - 1 - shift <= A_MAX_REL and span <= BIAS_PERIOD
    u = jnp.concatenate([jnp.broadcast_to(tab[:, 2 * A_MAX_REL:], (h, n_far)),
                         tab[:, 2 * A_MAX_REL - 1:2 * A_MAX_REL - 1 - (span - n_far):-1]], axis=1)
    return jnp.pad(u, ((0, 0), (0, BIAS_PERIOD - span)))


def _band_attention(q, k, v, u, batch):
    t, w = q.shape
    n = BAND_TILES_PER_STEP
    nb = t // batch // (n * ROW_TILE)
    step = pl.BlockSpec((n * ROW_TILE, w), lambda b, i: (b * nb + i, 0))
    tile = lambda j: pl.BlockSpec(
        (ROW_TILE, w), lambda b, i: (n * b * nb + jnp.maximum(n * i + j - (BAND_TILES - 1), 0), 0))
    window = [tile(j) for j in range(n + BAND_TILES - 1)]
    return pl.pallas_call(
        _band_kernel,
        grid=(batch, nb),
        in_specs=[step] + window + window + [pl.BlockSpec(u.shape, lambda b, i: (0, 0))],
        out_specs=step,
        out_shape=jax.ShapeDtypeStruct((t, w), BF16),
        scratch_shapes=[pltpu.VMEM((A_HEADS // 2, 2 * ROW_TILE, BAND_KEYS), F32)],
        compiler_params=_params("arbitrary", "arbitrary"),
        name="band_attn",
    )(q, *([k] * len(window)), *([v] * len(window)), u)


def _gla_constants():
    c = CHUNK
    t = np.arange(c)[:, None]
    r = np.arange(c)[None, :]
    mats = [(r <= t), (r > t)]
    lvl = np.full((c, c), -1, np.int32)
    lvl[np.arange(c), np.arange(c)] = N_LEVELS
    for l in range(N_LEVELS):
        m = (c // 2) >> l
        mid = (t // (2 * m)) * (2 * m) + m
        upper = t >= mid
        mats.append(np.where(upper, (r >= mid) & (r <= t), (r > t) & (r < mid)))
        s = r
        same = (s // (2 * m)) == (t // (2 * m))
        lvl[np.asarray(same & upper & (s < mid))] = l
    eye = np.eye(CHUNKS_PER_TILE)
    mexp = np.concatenate([np.kron(eye, m) for m in mats], axis=0).astype(np.float32)
    lvl = np.tile(lvl, (1, B_HEADS))
    return jnp.asarray(mexp, BF16), jnp.asarray(lvl, I32)


def _gla_kernel(q_ref, k_ref, v_ref, g_ref, r_ref, gn_ref, mexp_ref, lvl_ref, o_ref, s_ref):
    @pl.when(pl.program_id(1) == 0)
    def _():
        s_ref[...] = jnp.zeros_like(s_ref)

    kw = B_KEY_WIDTH
    ri = lax.broadcasted_iota(I32, (kw, kw), 0) >> LOG_CHUNK
    ci = lax.broadcasted_iota(I32, (kw, kw), 1) >> LOG_CHUNK
    bd = ri == ci
    head_ind = jnp.where(bd, 1.0, 0.0).astype(BF16)
    ri2 = lax.broadcasted_iota(I32, (kw, 2 * kw), 0) >> LOG_CHUNK
    ci2 = (lax.broadcasted_iota(I32, (kw, 2 * kw), 1) & (kw - 1)) >> LOG_CHUNK
    bd2 = ri2 == ci2
    lvl = lvl_ref[...]
    row8 = lax.broadcasted_iota(I32, (16, kw), 0)
    ones = jnp.ones((16, LANES), BF16)
    zero_b = jnp.zeros((kw, kw), BF16)
    chunks = [slice(c * CHUNK, (c + 1) * CHUNK) for c in range(CHUNKS_PER_TILE)]

    def head_blocks(x):
        return jnp.where(bd, jnp.concatenate([x] * B_HEADS, axis=0), zero_b)

    def prepare(tile):
        trows = pl.ds(tile * ROW_TILE, ROW_TILE)
        q = q_ref[trows, :].astype(F32)
        k = k_ref[trows, :].astype(F32)
        g = g_ref[trows, :]
        gb = g.astype(BF16)
        half = EXP_ROWS * CHUNKS_PER_TILE // 2
        w = jnp.exp(jnp.concatenate([_dot(mexp_ref[:half, :], gb), _dot(mexp_ref[half:, :], gb)], axis=0))
        qt = (q * w[0:ROW_TILE]).astype(BF16)
        kb = (k * w[ROW_TILE:2 * ROW_TILE]).astype(BF16)
        qk = (q * k).astype(BF16)

        attn = [jnp.zeros((CHUNK, kw), F32) for _ in chunks]
        for l in range(N_LEVELS):
            wl = w[(2 + l) * ROW_TILE:(3 + l) * ROW_TILE]
            qh = (q * wl).astype(BF16)
            kh = (k * wl).astype(BF16)
            for c, rows in enumerate(chunks):
                attn[c] = jnp.where(lvl == l, _dot_nt(qh[rows], head_blocks(kh[rows])), attn[c])

        out = []
        for c, rows in enumerate(chunks):
            a = jnp.where(lvl == N_LEVELS, _dot(qk[rows], head_ind), attn[c])
            v = v_ref[pl.ds(tile * ROW_TILE + c * CHUNK, CHUNK), :]
            vstack = jnp.concatenate([v[:, j * LANES:(j + 1) * LANES] for j in range(B_HEADS)], axis=0)
            kv = _dot_tn(head_blocks(kb[rows]), vstack)
            d = jnp.exp(jnp.sum(g[rows], axis=0, keepdims=True))
            d1 = d.astype(BF16).astype(F32)
            dp = jnp.where(row8 == 0, d1, jnp.where(row8 == 1, d - d1, 0.0)).astype(BF16)
            dcol = _dot_tn(dp, ones)
            out.append((a.astype(BF16), qt[rows], vstack, kv, dcol))
        return out

    prepared = [p for tile in range(GLA_TILES_PER_STEP) for p in prepare(tile)]

    s = s_ref[...]
    for c, (a, qtc, vstack, kv, dcol) in enumerate(prepared):
        rows = pl.ds(c * CHUNK, CHUNK)
        lhs = jnp.concatenate([a, qtc], axis=1)
        lhs = jnp.where(bd2, jnp.concatenate([lhs] * B_HEADS, axis=0), jnp.zeros((kw, 2 * kw), BF16))
        rhs = jnp.concatenate([vstack, s.astype(BF16)], axis=0)
        o = _dot(lhs, rhs)
        s = dcol * s + kv
        for j in range(B_HEADS):
            oj = o[j * CHUNK:(j + 1) * CHUNK]
            sl = slice(j * LANES, (j + 1) * LANES)
            y = oj * lax.rsqrt(jnp.mean(oj * oj, axis=-1, keepdims=True) + EPS) * gn_ref[...]
            o_ref[rows, sl] = (y * r_ref[rows, sl].astype(F32)).astype(BF16)
    s_ref[...] = s


def _gla(q, k, v, g, r, gn, batch):
    t = q.shape[0]
    nb = t // batch // (GLA_TILES_PER_STEP * ROW_TILE)
    mexp, lvl = _gla_constants()
    cur = lambda b, i: (b * nb + i, 0)
    blk = lambda w: pl.BlockSpec((GLA_TILES_PER_STEP * ROW_TILE, w), cur)
    full = lambda a: pl.BlockSpec(a.shape, lambda b, i: (0,) * a.ndim)
    return pl.pallas_call(
        _gla_kernel,
        grid=(batch, nb),
        in_specs=[blk(B_KEY_WIDTH), blk(B_KEY_WIDTH), blk(B_VAL_WIDTH), blk(B_KEY_WIDTH), blk(B_VAL_WIDTH),
                  full(gn), full(mexp), full(lvl)],
        out_specs=blk(B_VAL_WIDTH),
        out_shape=jax.ShapeDtypeStruct((t, B_VAL_WIDTH), BF16),
        scratch_shapes=[pltpu.VMEM((B_KEY_WIDTH, B_VAL_DIM), F32)],
        compiler_params=_params("arbitrary", "arbitrary"),
        name="gla",
    )(q, k, v, g, r, gn, mexp, lvl)


def _token_kernel(x_ref, oa_ref, ob_ref, gate_ref, wb0_ref, wb1_ref, wmix_ref, gx_ref, wq_ref,
                  km_ref, vm_ref, wo_ref, gf_ref, wr_ref, br_ref, ltri_ref, utri_ref,
                  x2_ref, hs_ref, route_ref, cnt_ref):
    ma = _dot(oa_ref[...], wb0_ref[...])
    mb = _dot(ob_ref[...], wb1_ref[...])
    merged = (gate_ref[:, :D_MODEL].astype(F32) * ma + gate_ref[:, D_MODEL:].astype(F32) * mb).astype(BF16)
    x1 = x_ref[...] + _dot(merged, wmix_ref[...])

    h2 = _rms(x1, gx_ref[...]).astype(BF16)
    qx = (_dot(h2, wq_ref[...]) * (X_HEAD_DIM ** -0.5)).astype(BF16)
    heads = []
    for h in range(X_HEADS):
        sl = slice(h * X_HEAD_DIM, (h + 1) * X_HEAD_DIM)
        s = _dot_nt(qx[:, sl], km_ref[0, :, sl])
        m = jnp.max(s, axis=-1, keepdims=True)
        pe = jnp.exp(s - m)
        l = jnp.sum(pe, axis=-1, keepdims=True)
        heads.append((_dot(pe.astype(BF16), vm_ref[0, :, sl]) * (1.0 / l)).astype(BF16))
    x2 = x1 + _dot(jnp.concatenate(heads, axis=1), wo_ref[...])
    x2_ref[...] = x2

    h3 = _rms(x2, gf_ref[...])

    h3_hi = h3.astype(BF16)
    h3_lo = (h3 - h3_hi.astype(F32)).astype(BF16)
    hw = _dot(h3_hi, wr_ref[...])
    logits = hw[:, :LANES] + hw[:, LANES:] + _dot(h3_lo, wr_ref[:, :LANES]) + br_ref[...]
    for h in range(TOKEN_TILES_PER_STEP):
        rows = slice(h * ROW_TILE, (h + 1) * ROW_TILE)
        chunks = pl.ds(h * LOCAL_CHUNKS, LOCAL_CHUNKS)
        _route_and_sort(logits[rows], h3_hi[rows], ltri_ref, utri_ref, hs_ref.at[chunks],
                        route_ref.at[pl.ds(h * ROW_TILE, ROW_TILE)], cnt_ref.at[h])


def _route_and_sort(logits, h3_hi, ltri_ref, utri_ref, hs_ref, route_ref, cnt_ref):
    lane = lax.broadcasted_iota(I32, logits.shape, 1).astype(F32)
    big = jnp.float32(LANES)
    gl = jnp.where(lane < N_GROUPS, logits, NEG)
    gmax = jnp.max(gl, axis=-1, keepdims=True)
    gidx = jnp.min(jnp.where(gl == gmax, lane, big), axis=-1, keepdims=True)
    g_w = 1.0 / jnp.sum(jnp.exp(gl - gmax), axis=-1, keepdims=True)
    lo = N_GROUPS + EXPERTS_PER_GROUP * gidx
    el = jnp.where((lane >= lo) & (lane < lo + EXPERTS_PER_GROUP), logits, NEG)
    v1 = jnp.max(el, axis=-1, keepdims=True)
    i1 = jnp.min(jnp.where(el == v1, lane, big), axis=-1, keepdims=True)
    el2 = jnp.where(lane == i1, NEG, el)
    v2 = jnp.max(el2, axis=-1, keepdims=True)
    i2 = jnp.min(jnp.where(el2 == v2, lane, big), axis=-1, keepdims=True)
    e21 = jnp.exp(v2 - v1)
    w1 = g_w / (1.0 + e21)
    w2 = w1 * e21
    oh0 = jnp.where(lane == i1 - N_GROUPS, 1.0, 0.0)
    oh1 = jnp.where(lane == i2 - N_GROUPS, 1.0, 0.0)
    oh = oh0 + oh1
    nch = jnp.floor((jnp.sum(oh, axis=0, keepdims=True) + (CHUNK_ROWS - 1)) * (1.0 / CHUNK_ROWS))
    nch8 = jnp.broadcast_to(nch, (8, LANES))
    start = _dot(nch8.astype(BF16), utri_ref[...])[0:1] * CHUNK_ROWS
    rank = _dot(ltri_ref[...], oh.astype(BF16))
    row = start + rank
    pos0 = jnp.sum(row * oh0, axis=-1, keepdims=True)
    pos1 = jnp.sum(row * oh1, axis=-1, keepdims=True)
    route = jnp.where(lane == 0, pos0, jnp.where(lane == 1, pos1, 0.0))
    route_t = jnp.transpose(route)
    r = lax.broadcasted_iota(I32, (LOCAL_ROWS, ROW_TILE), 0).astype(F32)
    p0 = jnp.where(r == route_t[0:1, :], 1.0, 0.0).astype(BF16)
    p1 = jnp.where(r == route_t[1:2, :], 1.0, 0.0).astype(BF16)

    def gate_cols(w):
        hi = w.astype(BF16).astype(F32)
        return jnp.where(lane == 0, hi, jnp.where(lane == 1, w - hi, 0.0)).astype(BF16)

    sorted_rows = jnp.concatenate([_dot(p0 + p1, h3_hi), _dot(p0, gate_cols(w1)) + _dot(p1, gate_cols(w2))],
                                  axis=1)
    hs_ref[...] = sorted_rows.astype(BF16).reshape(hs_ref.shape)
    route_ref[...] = route
    cnt_ref[...] = nch8


def _token(x, oa, ob, gates, wb0, wb1, wmix, gx, wq, km, vm, wo, gf, wr, br, batch):
    t, d = x.shape
    n = TOKEN_TILES_PER_STEP
    nb = t // batch // (n * ROW_TILE)
    nt = t // ROW_TILE
    ltri = jnp.asarray(np.tril(np.ones((ROW_TILE, ROW_TILE), np.float32), -1), BF16)
    utri = jnp.asarray(np.triu(np.ones((LANES, LANES), np.float32), 1), BF16)
    cur = lambda b, i: (b * nb + i, 0)
    cur3 = lambda b, i: (b * nb + i, 0, 0)
    blk = lambda w: pl.BlockSpec((n * ROW_TILE, w), cur)
    full = lambda a: pl.BlockSpec(a.shape, lambda b, i: (0,) * a.ndim)
    mem = pl.BlockSpec((1,) + km.shape[1:], lambda b, i: (b, 0, 0))
    return pl.pallas_call(
        _token_kernel,
        grid=(batch, nb),
        in_specs=[blk(d), blk(A_WIDTH), blk(B_VAL_WIDTH), blk(2 * d), full(wb0), full(wb1), full(wmix),
                  full(gx), full(wq), mem, mem, full(wo), full(gf), full(wr), full(br), full(ltri), full(utri)],
        out_specs=[blk(d), pl.BlockSpec((n * LOCAL_CHUNKS, CHUNK_ROWS, SORT_WIDTH), cur3),
                   blk(LANES), pl.BlockSpec((n, 8, LANES), cur3)],
        out_shape=[jax.ShapeDtypeStruct((t, d), F32),
                   jax.ShapeDtypeStruct((nt * LOCAL_CHUNKS, CHUNK_ROWS, SORT_WIDTH), BF16),
                   jax.ShapeDtypeStruct((t, LANES), F32),
                   jax.ShapeDtypeStruct((nt, 8, LANES), F32)],
        compiler_params=_params("arbitrary", "arbitrary"),
        name="token",
    )(x, oa, ob, gates, wb0, wb1, wmix, gx, wq, km, vm, wo, gf, wr, br, ltri, utri)


def _expert_kernel(te_ref, nu_ref, nv_ref, ch_ref, hs_hbm, wg_ref, wu_ref, wd_ref, ys_hbm,
                   xbuf, ybuf, wg_bf, wu_bf, wd_bf, gsem, ssem):
    i = pl.program_id(0)
    n_used = nu_ref[0]
    slot = lax.rem(i, 2)

    def for_chunks(tile, fn):
        nv = nv_ref[tile]

        @pl.when(nv == TILE_CHUNKS)
        def _():
            for c in range(TILE_CHUNKS):
                fn(c)

        @pl.when(nv != TILE_CHUNKS)
        def _():
            def body(c, carry):
                fn(c)
                return carry

            lax.fori_loop(0, nv, body, 0)

    def gather(tile, s, start):
        for c in range(TILE_CHUNKS):
            cp = pltpu.make_async_copy(hs_hbm.at[ch_ref[tile * TILE_CHUNKS + c]], xbuf.at[s, c], gsem.at[s])
            cp.start() if start else cp.wait()

    def scatter(tile, s, start):
        def one(c):
            cp = pltpu.make_async_copy(ybuf.at[s, c], ys_hbm.at[ch_ref[tile * TILE_CHUNKS + c]], ssem.at[s])
            cp.start() if start else cp.wait()

        for_chunks(tile, one)

    next_tile = jnp.minimum(i + 1, pl.num_programs(0) - 1)

    @pl.when(i == 0)
    def _():
        gather(0, 0, True)

    @pl.when(i < n_used)
    def _():
        gather(next_tile, 1 - slot, True)
        gather(i, slot, False)

        @pl.when(i >= 2)
        def _():
            scatter(i - 2, slot, False)

        @pl.when((i == 0) | (te_ref[i] != te_ref[jnp.maximum(i - 1, 0)]))
        def _():
            wg_bf[...] = wg_ref[0].astype(BF16)
            wu_bf[...] = wu_ref[0].astype(BF16)
            wd_bf[...] = wd_ref[0].astype(BF16)

        xg = xbuf[slot].reshape(EXPERT_ROWS, SORT_WIDTH)
        x = xg[:, :D_MODEL]
        hg = _dot(x, wg_bf[...])
        hu = _dot(x, wu_bf[...])
        hid = (hg * jax.nn.sigmoid(hg) * hu).astype(BF16)
        g = xg[:, D_MODEL:].astype(F32)
        y = ((g[:, 0:1] + g[:, 1:2]) * _dot(hid, wd_bf[...])).astype(BF16)
        y = jnp.concatenate([y, jnp.zeros((EXPERT_ROWS, LANES), BF16)], axis=1)
        ybuf[slot] = y.reshape(TILE_CHUNKS, CHUNK_ROWS, SORT_WIDTH)
        scatter(i, slot, True)

        @pl.when(i == n_used - 1)
        def _():
            gather(next_tile, 1 - slot, False)
            scatter(i, slot, False)

            @pl.when(i >= 1)
            def _():
                scatter(i - 1, 1 - slot, False)


def _experts(hs, tile_expert, n_used, n_valid, chunks, wg, wu, wd, layer):
    n_tiles = tile_expert.shape[0]
    last = lambda i, te, nu, nv, ch: jnp.minimum(i, nu[0] - 1)
    wmap = lambda i, te, nu, nv, ch: (layer * N_EXPERTS + te[last(i, te, nu, nv, ch)], 0, 0)
    anyspace = pl.BlockSpec(memory_space=pl.ANY)
    grid_spec = pltpu.PrefetchScalarGridSpec(
        num_scalar_prefetch=4,
        grid=(n_tiles,),
        in_specs=[anyspace,
                  pl.BlockSpec((1, D_MODEL, EXPERT_FF), wmap),
                  pl.BlockSpec((1, D_MODEL, EXPERT_FF), wmap),
                  pl.BlockSpec((1, EXPERT_FF, D_MODEL), wmap)],
        out_specs=anyspace,
        scratch_shapes=[pltpu.VMEM((2, TILE_CHUNKS, CHUNK_ROWS, SORT_WIDTH), BF16),
                        pltpu.VMEM((2, TILE_CHUNKS, CHUNK_ROWS, SORT_WIDTH), BF16),
                        pltpu.VMEM((D_MODEL, EXPERT_FF), BF16), pltpu.VMEM((D_MODEL, EXPERT_FF), BF16),
                        pltpu.VMEM((EXPERT_FF, D_MODEL), BF16),
                        pltpu.SemaphoreType.DMA((2,)), pltpu.SemaphoreType.DMA((2,))],
    )
    return pl.pallas_call(
        _expert_kernel,
        grid_spec=grid_spec,
        out_shape=jax.ShapeDtypeStruct(hs.shape, BF16),
        input_output_aliases={4: 0},
        compiler_params=_params("arbitrary"),
        name="experts",
    )(tile_expert, n_used, n_valid, chunks, hs, wg, wu, wd)


def _combine_kernel(x_ref, route_ref, ys_ref, gfin_ref, o_ref):
    r = lax.broadcasted_iota(I32, (ROW_TILE, LOCAL_ROWS), 1).astype(F32)
    for t in range(COMBINE_TILES_PER_STEP):
        rows = slice(t * ROW_TILE, (t + 1) * ROW_TILE)
        pos = route_ref[rows, :]
        sel = jnp.where(r == pos[:, 0:1], 1.0, jnp.where(r == pos[:, 1:2], 1.0, 0.0)).astype(BF16)
        ys = ys_ref[t * LOCAL_CHUNKS:(t + 1) * LOCAL_CHUNKS].reshape(LOCAL_ROWS, D_MODEL)
        o_ref[rows, :] = _rms(x_ref[rows, :] + _dot(sel, ys), gfin_ref[...])


def _combine(x2, route, ys, gfin):
    t, d = x2.shape
    n = COMBINE_TILES_PER_STEP
    return pl.pallas_call(
        _combine_kernel,
        grid=(t // (n * ROW_TILE),),
        in_specs=[pl.BlockSpec((n * ROW_TILE, d), lambda i: (i, 0)),
                  pl.BlockSpec((n * ROW_TILE, LANES), lambda i: (i, 0)),
                  pl.BlockSpec((n * LOCAL_CHUNKS, CHUNK_ROWS, d), lambda i: (i, 0, 0)),
                  pl.BlockSpec((1, d), lambda i: (0, 0))],
        out_specs=pl.BlockSpec((n * ROW_TILE, d), lambda i: (i, 0)),
        out_shape=jax.ShapeDtypeStruct((t, d), F32),
        compiler_params=_params("arbitrary"),
        name="combine",
    )(x2, route, ys, gfin)


def _chunk_plan(nch, n_tiles):
    nt = nch.shape[0]
    local_start = jnp.cumsum(nch, axis=1) - nch
    cum = jnp.cumsum(nch, axis=0)
    total = cum[-1]
    tiles = (total + TILE_CHUNKS - 1) // TILE_CHUNKS
    tile_end = jnp.cumsum(tiles)
    n_used = tile_end[-1:]
    tile_ids = jnp.arange(n_tiles, dtype=I32)
    tile_expert = jnp.minimum(jnp.sum((tile_end[None, :] <= tile_ids[:, None]).astype(I32), axis=1),
                              N_EXPERTS - 1)
    sel = (tile_expert[:, None] == jnp.arange(N_EXPERTS, dtype=I32)[None, :]).astype(I32)
    pick = lambda table: jnp.sum(sel[:, :, None] * table.T[None, :, :], axis=1)
    first_tile = jnp.sum(sel * (tile_end - tiles)[None, :], axis=1)
    slot = (tile_ids - first_tile)[:, None] * TILE_CHUNKS + jnp.arange(TILE_CHUNKS, dtype=I32)[None, :]
    valid = (slot < jnp.sum(sel * total[None, :], axis=1)[:, None]) & (tile_ids < n_used)[:, None]
    src_tile = jnp.sum((pick(cum)[:, None, :] <= slot[:, :, None]).astype(I32), axis=2)
    src_tile = jnp.minimum(src_tile, nt - 1)
    at = (src_tile[:, :, None] == jnp.arange(nt, dtype=I32)[None, None, :]).astype(I32)
    before = jnp.sum(at * pick(cum - nch)[:, None, :], axis=2)
    start = jnp.sum(at * pick(local_start)[:, None, :], axis=2)
    chunk = jnp.where(valid, src_tile * LOCAL_CHUNKS + start + slot - before, LOCAL_CHUNKS - 1)
    return tile_expert, n_used, jnp.sum(valid.astype(I32), axis=1), chunk.reshape(-1)


def kernel(x, mem, norm_mix_g, w_in, rel_bias, gla_w_alpha, gla_b_alpha, gla_norm_g, w_branch, w_mix_out, norm_x_g, mem_norm_g, w_xq, w_xkv, w_xo, norm_ffn_g, w_group_router, b_group_router, w_expert_router, b_expert_router, w_exp_gate, w_exp_up, w_exp_down, final_norm_g):
    batch, seq, d = x.shape
    depth = w_in.shape[0]
    t = batch * seq
    step_tiles = max(TOKEN_TILES_PER_STEP, GLA_TILES_PER_STEP, BAND_TILES_PER_STEP, COMBINE_TILES_PER_STEP)
    assert d == D_MODEL and seq % (step_tiles * ROW_TILE) == 0 and seq % INPROJ_ROWS == 0
    nt = t // ROW_TILE
    n_tiles = nt * LOCAL_CHUNKS // TILE_CHUNKS + N_EXPERTS

    xf = x.reshape(t, d)
    km_all, vm_all = _memkv(mem, mem_norm_g, w_xkv.astype(BF16))
    row = lambda a: a.reshape(1, -1).astype(F32)

    moe = None
    for l in range(depth):
        res = _inproj(xf, row(norm_mix_g[l]), w_in, gla_w_alpha, row(gla_b_alpha[l]), l, moe)
        if moe is not None:
            xf, res = res[0], res[1:]
        aq, ak, av, bq, bk, bv, lga, br, gates = res

        oa = _band_attention(aq, ak, av, _band_bias_vector(rel_bias[l]), batch)
        ob = _gla(bq, bk, bv, lga, br, row(gla_norm_g[l]), batch)

        wr = jnp.pad(jnp.concatenate([w_group_router[l], w_expert_router[l]], axis=1).astype(F32),
                     ((0, 0), (0, LANES - N_GROUPS - N_EXPERTS)))
        wr_hi = wr.astype(BF16)
        wr = jnp.concatenate([wr_hi, (wr - wr_hi.astype(F32)).astype(BF16)], axis=1)
        brt = jnp.pad(jnp.concatenate([b_group_router[l], b_expert_router[l]]).astype(F32),
                      (0, LANES - N_GROUPS - N_EXPERTS)).reshape(1, LANES)
        x2, hs, route, cnt = _token(
            xf, oa, ob, gates, w_branch[l, 0].astype(BF16), w_branch[l, 1].astype(BF16),
            w_mix_out[l].astype(BF16), row(norm_x_g[l]), w_xq[l].astype(BF16), km_all[l], vm_all[l],
            w_xo[l].astype(BF16), row(norm_ffn_g[l]), wr, brt, batch)

        plan = _chunk_plan(cnt[:, 0, :N_EXPERTS].astype(I32), n_tiles)
        e3 = lambda w: w.reshape((depth * N_EXPERTS,) + w.shape[3:])
        ys = _experts(hs, *plan, e3(w_exp_gate), e3(w_exp_up), e3(w_exp_down), l)
        xf, moe = x2, (route, ys)

    return _combine(x2, route, ys, row(final_norm_g)).reshape(batch, seq, d)
```

```python
import functools

import numpy as np
import jax
import jax.numpy as jnp
from jax import lax
from jax.experimental import pallas as pl
from jax.experimental.pallas import tpu as pltpu

F32 = jnp.float32
BF16 = jnp.bfloat16
I32 = jnp.int32

D_MODEL = 1024
CHUNK = 64
EPS = 1e-6
A_HEADS = 8
A_HEAD_DIM = 64
A_WIDTH = 512
A_LEFT_CHUNKS = 8
A_MAX_REL = 256
B_HEADS = 4
B_KEY_DIM = 64
B_VAL_DIM = 128
B_KEY_WIDTH = 256
B_VAL_WIDTH = 512
B_GATE_RANK = 16
B_GATE_TAU = 16.0
X_HEADS = 4
X_HEAD_DIM = 256
N_GROUPS = 4
EXPERTS_PER_GROUP = 8
N_EXPERTS = N_GROUPS * EXPERTS_PER_GROUP
EXPERT_FF = 256

LANES = 128
ROW_TILE = 256
CHUNKS_PER_TILE = ROW_TILE // CHUNK
BAND_TILES = A_LEFT_CHUNKS // CHUNKS_PER_TILE + 1
BAND_KEYS = BAND_TILES * ROW_TILE
BIAS_PERIOD = 1024
LOG_CHUNK = 6
N_LEVELS = LOG_CHUNK
EXP_ROWS = (2 + N_LEVELS) * CHUNK
CHUNK_ROWS = 16
TOKEN_TILES_PER_STEP = 2
INPROJ_ROWS = 512
COMBINE_TILES_PER_STEP = 2
GLA_TILES_PER_STEP = 4
BAND_TILES_PER_STEP = 4
EXPERT_ROWS = 1024
TILE_CHUNKS = EXPERT_ROWS // CHUNK_ROWS
LOCAL_CHUNKS = 2 * ROW_TILE // CHUNK_ROWS + N_EXPERTS
LOCAL_ROWS = LOCAL_CHUNKS * CHUNK_ROWS
assert (2 * ROW_TILE + N_EXPERTS * (CHUNK_ROWS - 1)) // CHUNK_ROWS < LOCAL_CHUNKS
SORT_WIDTH = D_MODEL + LANES
NEG = -1e30
LOG2E = 1.4426950408889634
VMEM_LIMIT = 56 * 1024 * 1024


def _params(*sem):
    return pltpu.CompilerParams(dimension_semantics=sem, vmem_limit_bytes=VMEM_LIMIT)


def _rms(x, g):
    return x * lax.rsqrt(jnp.mean(x * x, axis=-1, keepdims=True) + EPS) * g


def _dot(a, b):
    return jnp.dot(a, b, preferred_element_type=F32)


def _dot_nt(a, b):
    return lax.dot_general(a, b, (((1,), (1,)), ((), ())), preferred_element_type=F32)


def _dot_tn(a, b):
    return lax.dot_general(a, b, (((0,), (0,)), ((), ())), preferred_element_type=F32)


def _memkv_kernel(mem_ref, g_ref, w_ref, k_ref, v_ref):
    mn = _rms(mem_ref[0], g_ref[...]).astype(BF16)
    kv = _dot(mn, w_ref[0])
    k_ref[0, 0] = kv[:, :D_MODEL].astype(BF16)
    v_ref[0, 0] = kv[:, D_MODEL:].astype(BF16)


def _memkv(mem, g, w_xkv):
    depth = w_xkv.shape[0]
    b, m, d = mem.shape
    out = jax.ShapeDtypeStruct((depth, b, m, d), BF16)
    return pl.pallas_call(
        _memkv_kernel,
        grid=(depth, b),
        in_specs=[pl.BlockSpec((1, m, d), lambda l, i: (i, 0, 0)),
                  pl.BlockSpec((1, d), lambda l, i: (0, 0)),
                  pl.BlockSpec((1, d, 2 * d), lambda l, i: (l, 0, 0))],
        out_specs=[pl.BlockSpec((1, 1, m, d), lambda l, i: (l, i, 0, 0)),
                   pl.BlockSpec((1, 1, m, d), lambda l, i: (l, i, 0, 0))],
        out_shape=[out, out],
        compiler_params=_params("arbitrary", "arbitrary"),
        name="memkv",
    )(mem, g.reshape(1, d), w_xkv)


_R_AQ, _R_AK, _R_AV = 0, 512, 1024
_R_BQ, _R_BK, _R_BV = 1536, 1792, 2048
_R_ALPHA, _R_BR, _R_GATE, _R_END = 2560, 2576, 3088, 5136
_W_PIECE = 512


def _inproj_kernel(layer, fused, *refs):
    if fused:
        x_ref, route_ref, ys_ref, g_ref, wt_hbm, wal2_ref, bal_ref = refs[:7]
        refs = refs[7:]
        x3_ref, refs = refs[0], refs[1:]
    else:
        x_ref, g_ref, wt_hbm, wal2_ref, bal_ref = refs[:5]
        refs = refs[5:]
    (aq_ref, ak_ref, av_ref, bq_ref, bk_ref, bv_ref, lga_ref, br_ref, gate_ref,
     wt_bf, wal2_bf, stage, sem) = refs

    @pl.when(pl.program_id(0) == 0)
    def _():
        pieces = [(c, min(_W_PIECE, _R_END - c)) for c in range(0, _R_END, _W_PIECE)]

        def piece_copy(p):
            c, n = pieces[p]
            return pltpu.make_async_copy(wt_hbm.at[layer, pl.ds(c, n), :], stage.at[p % 2, pl.ds(0, n), :],
                                         sem.at[p % 2])

        piece_copy(0).start()
        for p, (c, n) in enumerate(pieces):
            if p + 1 < len(pieces):
                piece_copy(p + 1).start()
            piece_copy(p).wait()
            wt_bf[c:c + n, :] = stage[p % 2, 0:n, :].astype(BF16)
        wal2_bf[...] = jnp.concatenate(
            [wal2_ref[0].astype(BF16), jnp.zeros((LANES - B_GATE_RANK, B_KEY_WIDTH), BF16)], axis=0)

    if fused:
        r = lax.broadcasted_iota(I32, (ROW_TILE, LOCAL_ROWS), 1).astype(F32)
        tiles = []
        for t in range(INPROJ_ROWS // ROW_TILE):
            rows = slice(t * ROW_TILE, (t + 1) * ROW_TILE)
            pos = route_ref[rows, :]
            sel = jnp.where(r == pos[:, 0:1], 1.0, jnp.where(r == pos[:, 1:2], 1.0, 0.0)).astype(BF16)
            ys = ys_ref[t * LOCAL_CHUNKS:(t + 1) * LOCAL_CHUNKS].reshape(LOCAL_ROWS, D_MODEL)
            tiles.append(x_ref[rows, :] + _dot(sel, ys))
        x = jnp.concatenate(tiles, axis=0)
        x3_ref[...] = x
    else:
        x = x_ref[...]
    h = _rms(x, g_ref[...]).astype(BF16)

    def mm(lo, hi):
        return _dot_nt(h, wt_bf[lo:hi, :])

    aq_ref[...] = (mm(_R_AQ, _R_AK) * (A_HEAD_DIM ** -0.5 * LOG2E)).astype(BF16)
    ak_ref[...] = mm(_R_AK, _R_AV).astype(BF16)
    av_ref[...] = mm(_R_AV, _R_BQ).astype(BF16)
    bq_ref[...] = (mm(_R_BQ, _R_BK) * (B_KEY_DIM ** -0.5)).astype(BF16)
    bk_ref[...] = mm(_R_BK, _R_BV).astype(BF16)
    bv_ref[...] = mm(_R_BV, _R_ALPHA).astype(BF16)
    r = mm(_R_BR, _R_GATE)
    br_ref[...] = (r * jax.nn.sigmoid(r)).astype(BF16)
    for c in range(_R_GATE, _R_END, 512):
        gate_ref[:, c - _R_GATE:c - _R_GATE + 512] = jax.nn.sigmoid(mm(c, c + 512)).astype(BF16)
    z = _dot(mm(_R_ALPHA, _R_ALPHA + LANES).astype(BF16), wal2_bf[...]) + bal_ref[...]
    lga_ref[...] = (jnp.minimum(z, 0.0) - jnp.log(1.0 + jnp.exp(-jnp.abs(z)))) * (1.0 / B_GATE_TAU)


def _inproj(x, g, w_in, w_al2, b_al, layer, moe=None):
    t, d = x.shape
    assert w_in.shape[2] == _R_END
    row = lambda w: pl.BlockSpec((INPROJ_ROWS, w), lambda i: (i, 0))
    full = lambda a: pl.BlockSpec(a.shape, lambda i: (0,) * a.ndim)
    sds = lambda w, dt: jax.ShapeDtypeStruct((t, w), dt)
    widths = [(512, BF16), (512, BF16), (512, BF16), (256, BF16), (256, BF16), (512, BF16),
              (256, F32), (512, BF16), (2048, BF16)]
    fused = moe is not None
    moe_specs, moe_out_specs, moe_out_shape = [], [], []
    if fused:
        n_chunks = INPROJ_ROWS // ROW_TILE * LOCAL_CHUNKS
        moe_specs = [row(LANES), pl.BlockSpec((n_chunks, CHUNK_ROWS, d), lambda i: (i, 0, 0))]
        moe_out_specs, moe_out_shape = [row(d)], [sds(d, F32)]
    return pl.pallas_call(
        functools.partial(_inproj_kernel, layer, fused),
        grid=(t // INPROJ_ROWS,),
        in_specs=[row(d)] + moe_specs + [full(g), pl.BlockSpec(memory_space=pl.ANY),
                                         pl.BlockSpec((1,) + w_al2.shape[1:], lambda i: (layer, 0, 0)),
                                         full(b_al)],
        out_specs=moe_out_specs + [row(w) for w, _ in widths],
        out_shape=moe_out_shape + [sds(w, dt) for w, dt in widths],
        scratch_shapes=[pltpu.VMEM((_R_END, d), BF16), pltpu.VMEM((LANES, B_KEY_WIDTH), BF16),
                        pltpu.VMEM((2, _W_PIECE, d), F32), pltpu.SemaphoreType.DMA((2,))],
        compiler_params=_params("arbitrary"),
        name="inproj",
    )(x, *(moe or ()), g, jnp.swapaxes(w_in, 1, 2), w_al2, b_al)


def _band_kernel(q_ref, *refs):
    n_win = BAND_TILES_PER_STEP + BAND_TILES - 1
    k_refs, v_refs = refs[:n_win], refs[n_win:2 * n_win]
    u_ref, o_ref, bias_ref = refs[2 * n_win:]
    i = pl.program_id(1)
    lane = lax.broadcasted_iota(I32, (1, LANES), 1)
    low = lane < A_HEAD_DIM
    ones = jnp.ones((BAND_KEYS, LANES), BF16)

    @pl.when((pl.program_id(0) == 0) & (i == 0))
    def _():
        cq = lax.broadcasted_iota(I32, (ROW_TILE, BAND_KEYS), 0) >> LOG_CHUNK
        ck = lax.broadcasted_iota(I32, (ROW_TILE, BAND_KEYS), 1) >> LOG_CHUNK
        valid = (ck >= cq) & (ck <= cq + A_LEFT_CHUNKS)
        for h in range(A_HEADS):
            rows = jnp.broadcast_to(u_ref[h:h + 1, :], (ROW_TILE, BIAS_PERIOD))
            rows = pltpu.roll(rows, BIAS_PERIOD - (ROW_TILE - 1), 1, stride=1, stride_axis=0)
            bias_ref[h // 2, (h % 2) * ROW_TILE:(h % 2 + 1) * ROW_TILE, :] = jnp.where(
                valid, rows[:, :BAND_KEYS], NEG)

    def attend(tile, n_missing):
        rows = pl.ds(tile * ROW_TILE, ROW_TILE)
        window = range(tile, tile + BAND_TILES)
        for p in range(A_HEADS // 2):
            sl = slice(p * LANES, (p + 1) * LANES)
            qp = q_ref[rows, sl]
            zero = jnp.zeros_like(qp)
            q2 = jnp.concatenate([jnp.where(low, qp, zero), jnp.where(low, zero, qp)], axis=0)
            kp = jnp.concatenate([k_refs[j][:, sl] for j in window], axis=0)
            vp = jnp.concatenate([v_refs[j][:, sl] for j in window], axis=0)
            s = _dot_nt(q2, kp) + bias_ref[p]
            if n_missing:
                col = lax.broadcasted_iota(I32, (1, BAND_KEYS), 1)
                s = s + jnp.where(col < n_missing * ROW_TILE, NEG, 0.0).astype(F32)
            pe = jnp.exp2(s - jnp.max(s, axis=-1, keepdims=True)).astype(BF16)
            o2 = _dot(pe, jnp.concatenate([vp, ones], axis=1))
            o = o2[:, :LANES] * (1.0 / o2[:, LANES:])
            o_ref[rows, sl] = jnp.where(low, o[:ROW_TILE], o[ROW_TILE:]).astype(BF16)

    @pl.when(i > 0)
    def _():
        for tile in range(BAND_TILES_PER_STEP):
            attend(tile, 0)

    @pl.when(i == 0)
    def _():
        for tile in range(BAND_TILES_PER_STEP):
            attend(tile, max(BAND_TILES - 1 - tile, 0))


def _band_bias_vector(rel_table):
    h = rel_table.shape[0]
    tab = rel_table.astype(F32) * LOG2E
    shift = A_LEFT_CHUNKS * CHUNK + ROW_TILE - 1
    n_far = shift - A_MAX_REL + 1
    span = ROW_TILE + BAND_KEYS - 1
    assert span - 1 - shift <= A_MAX_REL and span <= BIAS_PERIOD
    u = jnp.concatenate([jnp.broadcast_to(tab[:, 2 * A_MAX_REL:], (h, n_far)),
                         tab[:, 2 * A_MAX_REL - 1:2 * A_MAX_REL - 1 - (span - n_far):-1]], axis=1)
    return jnp.pad(u, ((0, 0), (0, BIAS_PERIOD - span)))


def _band_attention(q, k, v, u, batch):
    t, w = q.shape
    n = BAND_TILES_PER_STEP
    nb = t // batch // (n * ROW_TILE)
    step = pl.BlockSpec((n * ROW_TILE, w), lambda b, i: (b * nb + i, 0))
    tile = lambda j: pl.BlockSpec(
        (ROW_TILE, w), lambda b, i: (n * b * nb + jnp.maximum(n * i + j - (BAND_TILES - 1), 0), 0))
    window = [tile(j) for j in range(n + BAND_TILES - 1)]
    return pl.pallas_call(
        _band_kernel,
        grid=(batch, nb),
        in_specs=[step] + window + window + [pl.BlockSpec(u.shape, lambda b, i: (0, 0))],
        out_specs=step,
        out_shape=jax.ShapeDtypeStruct((t, w), BF16),
        scratch_shapes=[pltpu.VMEM((A_HEADS // 2, 2 * ROW_TILE, BAND_KEYS), F32)],
        compiler_params=_params("arbitrary", "arbitrary"),
        name="band_attn",
    )(q, *([k] * len(window)), *([v] * len(window)), u)


def _gla_constants():
    c = CHUNK
    t = np.arange(c)[:, None]
    r = np.arange(c)[None, :]
    mats = [(r <= t), (r > t)]
    lvl = np.full((c, c), -1, np.int32)
    lvl[np.arange(c), np.arange(c)] = N_LEVELS
    for l in range(N_LEVELS):
        m = (c // 2) >> l
        mid = (t // (2 * m)) * (2 * m) + m
        upper = t >= mid
        mats.append(np.where(upper, (r >= mid) & (r <= t), (r > t) & (r < mid)))
        s = r
        same = (s // (2 * m)) == (t // (2 * m))
        lvl[np.asarray(same & upper & (s < mid))] = l
    eye = np.eye(CHUNKS_PER_TILE)
    mexp = np.concatenate([np.kron(eye, m) for m in mats], axis=0).astype(np.float32)
    lvl = np.tile(lvl, (1, B_HEADS))
    return jnp.asarray(mexp, BF16), jnp.asarray(lvl, I32)


def _gla_kernel(q_ref, k_ref, v_ref, g_ref, r_ref, gn_ref, mexp_ref, lvl_ref, o_ref, s_ref):
    @pl.when(pl.program_id(1) == 0)
    def _():
        s_ref[...] = jnp.zeros_like(s_ref)

    kw = B_KEY_WIDTH
    ri = lax.broadcasted_iota(I32, (kw, kw), 0) >> LOG_CHUNK
    ci = lax.broadcasted_iota(I32, (kw, kw), 1) >> LOG_CHUNK
    bd = ri == ci
    head_ind = jnp.where(bd, 1.0, 0.0).astype(BF16)
    ri2 = lax.broadcasted_iota(I32, (kw, 2 * kw), 0) >> LOG_CHUNK
    ci2 = (lax.broadcasted_iota(I32, (kw, 2 * kw), 1) & (kw - 1)) >> LOG_CHUNK
    bd2 = ri2 == ci2
    lvl = lvl_ref[...]
    row8 = lax.broadcasted_iota(I32, (16, kw), 0)
    ones = jnp.ones((16, LANES), BF16)
    zero_b = jnp.zeros((kw, kw), BF16)
    chunks = [slice(c * CHUNK, (c + 1) * CHUNK) for c in range(CHUNKS_PER_TILE)]

    def head_blocks(x):
        return jnp.where(bd, jnp.concatenate([x] * B_HEADS, axis=0), zero_b)

    def prepare(tile):
        trows = pl.ds(tile * ROW_TILE, ROW_TILE)
        q = q_ref[trows, :].astype(F32)
        k = k_ref[trows, :].astype(F32)
        g = g_ref[trows, :]
        gb = g.astype(BF16)
        half = EXP_ROWS * CHUNKS_PER_TILE // 2
        w = jnp.exp(jnp.concatenate([_dot(mexp_ref[:half, :], gb), _dot(mexp_ref[half:, :], gb)], axis=0))
        qt = (q * w[0:ROW_TILE]).astype(BF16)
        kb = (k * w[ROW_TILE:2 * ROW_TILE]).astype(BF16)
        qk = (q * k).astype(BF16)

        attn = [jnp.zeros((CHUNK, kw), F32) for _ in chunks]
        for l in range(N_LEVELS):
            wl = w[(2 + l) * ROW_TILE:(3 + l) * ROW_TILE]
            qh = (q * wl).astype(BF16)
            kh = (k * wl).astype(BF16)
            for c, rows in enumerate(chunks):
                attn[c] = jnp.where(lvl == l, _dot_nt(qh[rows], head_blocks(kh[rows])), attn[c])

        out = []
        for c, rows in enumerate(chunks):
            a = jnp.where(lvl == N_LEVELS, _dot(qk[rows], head_ind), attn[c])
            v = v_ref[pl.ds(tile * ROW_TILE + c * CHUNK, CHUNK), :]
            vstack = jnp.concatenate([v[:, j * LANES:(j + 1) * LANES] for j in range(B_HEADS)], axis=0)
            kv = _dot_tn(head_blocks(kb[rows]), vstack)
            d = jnp.exp(jnp.sum(g[rows], axis=0, keepdims=True))
            d1 = d.astype(BF16).astype(F32)
            dp = jnp.where(row8 == 0, d1, jnp.where(row8 == 1, d - d1, 0.0)).astype(BF16)
            dcol = _dot_tn(dp, ones)
            out.append((a.astype(BF16), qt[rows], vstack, kv, dcol))
        return out

    prepared = [p for tile in range(GLA_TILES_PER_STEP) for p in prepare(tile)]

    s = s_ref[...]
    for c, (a, qtc, vstack, kv, dcol) in enumerate(prepared):
        rows = pl.ds(c * CHUNK, CHUNK)
        lhs = jnp.concatenate([a, qtc], axis=1)
        lhs = jnp.where(bd2, jnp.concatenate([lhs] * B_HEADS, axis=0), jnp.zeros((kw, 2 * kw), BF16))
        rhs = jnp.concatenate([vstack, s.astype(BF16)], axis=0)
        o = _dot(lhs, rhs)
        s = dcol * s + kv
        for j in range(B_HEADS):
            oj = o[j * CHUNK:(j + 1) * CHUNK]
            sl = slice(j * LANES, (j + 1) * LANES)
            y = oj * lax.rsqrt(jnp.mean(oj * oj, axis=-1, keepdims=True) + EPS) * gn_ref[...]
            o_ref[rows, sl] = (y * r_ref[rows, sl].astype(F32)).astype(BF16)
    s_ref[...] = s


def _gla(q, k, v, g, r, gn, batch):
    t = q.shape[0]
    nb = t // batch // (GLA_TILES_PER_STEP * ROW_TILE)
    mexp, lvl = _gla_constants()
    cur = lambda b, i: (b * nb + i, 0)
    blk = lambda w: pl.BlockSpec((GLA_TILES_PER_STEP * ROW_TILE, w), cur)
    full = lambda a: pl.BlockSpec(a.shape, lambda b, i: (0,) * a.ndim)
    return pl.pallas_call(
        _gla_kernel,
        grid=(batch, nb),
        in_specs=[blk(B_KEY_WIDTH), blk(B_KEY_WIDTH), blk(B_VAL_WIDTH), blk(B_KEY_WIDTH), blk(B_VAL_WIDTH),
                  full(gn), full(mexp), full(lvl)],
        out_specs=blk(B_VAL_WIDTH),
        out_shape=jax.ShapeDtypeStruct((t, B_VAL_WIDTH), BF16),
        scratch_shapes=[pltpu.VMEM((B_KEY_WIDTH, B_VAL_DIM), F32)],
        compiler_params=_params("arbitrary", "arbitrary"),
        name="gla",
    )(q, k, v, g, r, gn, mexp, lvl)


def _token_kernel(x_ref, oa_ref, ob_ref, gate_ref, wb0_ref, wb1_ref, wmix_ref, gx_ref, wq_ref,
                  km_ref, vm_ref, wo_ref, gf_ref, wr_ref, br_ref, ltri_ref, utri_ref,
                  x2_ref, hs_ref, route_ref, cnt_ref):
    ma = _dot(oa_ref[...], wb0_ref[...])
    mb = _dot(ob_ref[...], wb1_ref[...])
    merged = (gate_ref[:, :D_MODEL].astype(F32) * ma + gate_ref[:, D_MODEL:].astype(F32) * mb).astype(BF16)
    x1 = x_ref[...] + _dot(merged, wmix_ref[...])

    h2 = _rms(x1, gx_ref[...]).astype(BF16)
    qx = (_dot(h2, wq_ref[...]) * (X_HEAD_DIM ** -0.5)).astype(BF16)
    heads = []
    for h in range(X_HEADS):
        sl = slice(h * X_HEAD_DIM, (h + 1) * X_HEAD_DIM)
        s = _dot_nt(qx[:, sl], km_ref[0, :, sl])
        m = jnp.max(s, axis=-1, keepdims=True)
        pe = jnp.exp(s - m)
        l = jnp.sum(pe, axis=-1, keepdims=True)
        heads.append((_dot(pe.astype(BF16), vm_ref[0, :, sl]) * (1.0 / l)).astype(BF16))
    x2 = x1 + _dot(jnp.concatenate(heads, axis=1), wo_ref[...])
    x2_ref[...] = x2

    h3 = _rms(x2, gf_ref[...])

    h3_hi = h3.astype(BF16)
    h3_lo = (h3 - h3_hi.astype(F32)).astype(BF16)
    hw = _dot(h3_hi, wr_ref[...])
    logits = hw[:, :LANES] + hw[:, LANES:] + _dot(h3_lo, wr_ref[:, :LANES]) + br_ref[...]
    oh0, oh1, g0, g1 = _route(logits)
    for h in range(TOKEN_TILES_PER_STEP):
        rows = slice(h * ROW_TILE, (h + 1) * ROW_TILE)
        chunks = pl.ds(h * LOCAL_CHUNKS, LOCAL_CHUNKS)
        _sort_tile(oh0[rows], oh1[rows], g0[rows], g1[rows], h3_hi[rows], ltri_ref, utri_ref,
                   hs_ref.at[chunks], route_ref.at[pl.ds(h * ROW_TILE, ROW_TILE)], cnt_ref.at[h])


def _route(logits):
    lane = lax.broadcasted_iota(I32, logits.shape, 1).astype(F32)
    big = jnp.float32(LANES)
    gl = jnp.where(lane < N_GROUPS, logits, NEG)
    gmax = jnp.max(gl, axis=-1, keepdims=True)
    gidx = jnp.min(jnp.where(gl == gmax, lane, big), axis=-1, keepdims=True)
    g_w = 1.0 / jnp.sum(jnp.exp(gl - gmax), axis=-1, keepdims=True)
    lo = N_GROUPS + EXPERTS_PER_GROUP * gidx
    el = jnp.where((lane >= lo) & (lane < lo + EXPERTS_PER_GROUP), logits, NEG)
    v1 = jnp.max(el, axis=-1, keepdims=True)
    i1 = jnp.min(jnp.where(el == v1, lane, big), axis=-1, keepdims=True)
    el2 = jnp.where(lane == i1, NEG, el)
    v2 = jnp.max(el2, axis=-1, keepdims=True)
    i2 = jnp.min(jnp.where(el2 == v2, lane, big), axis=-1, keepdims=True)
    e21 = jnp.exp(v2 - v1)
    w1 = g_w / (1.0 + e21)

    def gate_cols(w):
        hi = w.astype(BF16).astype(F32)
        return jnp.where(lane == 0, hi, jnp.where(lane == 1, w - hi, 0.0)).astype(BF16)

    oh0 = jnp.where(lane == i1 - N_GROUPS, 1.0, 0.0)
    oh1 = jnp.where(lane == i2 - N_GROUPS, 1.0, 0.0)
    return oh0, oh1, gate_cols(w1), gate_cols(w1 * e21)


def _sort_tile(oh0, oh1, g0, g1, h3_hi, ltri_ref, utri_ref, hs_ref, route_ref, cnt_ref):
    lane = lax.broadcasted_iota(I32, oh0.shape, 1)
    oh = oh0 + oh1
    nch = jnp.floor((jnp.sum(oh, axis=0, keepdims=True) + (CHUNK_ROWS - 1)) * (1.0 / CHUNK_ROWS))
    nch8 = jnp.broadcast_to(nch, (8, LANES))
    start = _dot(nch8.astype(BF16), utri_ref[...])[0:1] * CHUNK_ROWS
    rank = _dot(ltri_ref[...], oh.astype(BF16))
    row = start + rank
    pos0 = jnp.sum(row * oh0, axis=-1, keepdims=True)
    pos1 = jnp.sum(row * oh1, axis=-1, keepdims=True)
    route = jnp.where(lane == 0, pos0, jnp.where(lane == 1, pos1, 0.0))
    route_t = jnp.transpose(route)
    r = lax.broadcasted_iota(I32, (LOCAL_ROWS, ROW_TILE), 0).astype(F32)
    p0 = jnp.where(r == route_t[0:1, :], 1.0, 0.0).astype(BF16)
    p1 = jnp.where(r == route_t[1:2, :], 1.0, 0.0).astype(BF16)
    sorted_rows = jnp.concatenate([_dot(p0 + p1, h3_hi), _dot(p0, g0) + _dot(p1, g1)], axis=1)
    hs_ref[...] = sorted_rows.astype(BF16).reshape(hs_ref.shape)
    route_ref[...] = route
    cnt_ref[...] = nch8


def _token(x, oa, ob, gates, wb0, wb1, wmix, gx, wq, km, vm, wo, gf, wr, br, batch):
    t, d = x.shape
    n = TOKEN_TILES_PER_STEP
    nb = t // batch // (n * ROW_TILE)
    nt = t // ROW_TILE
    ltri = jnp.asarray(np.tril(np.ones((ROW_TILE, ROW_TILE), np.float32), -1), BF16)
    utri = jnp.asarray(np.triu(np.ones((LANES, LANES), np.float32), 1), BF16)
    cur = lambda b, i: (b * nb + i, 0)
    cur3 = lambda b, i: (b * nb + i, 0, 0)
    blk = lambda w: pl.BlockSpec((n * ROW_TILE, w), cur)
    full = lambda a: pl.BlockSpec(a.shape, lambda b, i: (0,) * a.ndim)
    mem = pl.BlockSpec((1,) + km.shape[1:], lambda b, i: (b, 0, 0))
    return pl.pallas_call(
        _token_kernel,
        grid=(batch, nb),
        in_specs=[blk(d), blk(A_WIDTH), blk(B_VAL_WIDTH), blk(2 * d), full(wb0), full(wb1), full(wmix),
                  full(gx), full(wq), mem, mem, full(wo), full(gf), full(wr), full(br), full(ltri), full(utri)],
        out_specs=[blk(d), pl.BlockSpec((n * LOCAL_CHUNKS, CHUNK_ROWS, SORT_WIDTH), cur3),
                   blk(LANES), pl.BlockSpec((n, 8, LANES), cur3)],
        out_shape=[jax.ShapeDtypeStruct((t, d), F32),
                   jax.ShapeDtypeStruct((nt * LOCAL_CHUNKS, CHUNK_ROWS, SORT_WIDTH), BF16),
                   jax.ShapeDtypeStruct((t, LANES), F32),
                   jax.ShapeDtypeStruct((nt, 8, LANES), F32)],
        compiler_params=_params("arbitrary", "arbitrary"),
        name="token",
    )(x, oa, ob, gates, wb0, wb1, wmix, gx, wq, km, vm, wo, gf, wr, br, ltri, utri)


def _expert_kernel(te_ref, nu_ref, nv_ref, ch_ref, hs_hbm, wg_ref, wu_ref, wd_ref, ys_hbm,
                   xbuf, ybuf, wgu_bf, wd_bf, gsem, ssem):
    i = pl.program_id(0)
    n_used = nu_ref[0]
    slot = lax.rem(i, 2)

    def for_chunks(tile, fn):
        nv = nv_ref[tile]

        @pl.when(nv == TILE_CHUNKS)
        def _():
            for c in range(TILE_CHUNKS):
                fn(c)

        @pl.when(nv != TILE_CHUNKS)
        def _():
            def body(c, carry):
                fn(c)
                return carry

            lax.fori_loop(0, nv, body, 0)

    def gather(tile, s, start):
        for c in range(TILE_CHUNKS):
            cp = pltpu.make_async_copy(hs_hbm.at[ch_ref[tile * TILE_CHUNKS + c]], xbuf.at[s, c], gsem.at[s])
            cp.start() if start else cp.wait()

    def scatter(tile, s, start):
        def one(c):
            cp = pltpu.make_async_copy(ybuf.at[s, c], ys_hbm.at[ch_ref[tile * TILE_CHUNKS + c]], ssem.at[s])
            cp.start() if start else cp.wait()

        for_chunks(tile, one)

    next_tile = jnp.minimum(i + 1, pl.num_programs(0) - 1)

    @pl.when(i == 0)
    def _():
        gather(0, 0, True)

    @pl.when(i < n_used)
    def _():
        gather(next_tile, 1 - slot, True)
        gather(i, slot, False)

        @pl.when(i >= 2)
        def _():
            scatter(i - 2, slot, False)

        @pl.when((i == 0) | (te_ref[i] != te_ref[jnp.maximum(i - 1, 0)]))
        def _():
            wgu_bf[:, :EXPERT_FF] = wg_ref[0].astype(BF16)
            wgu_bf[:, EXPERT_FF:] = wu_ref[0].astype(BF16)
            wd_bf[...] = wd_ref[0].astype(BF16)

        xg = xbuf[slot].reshape(EXPERT_ROWS, SORT_WIDTH)
        x = xg[:, :D_MODEL]
        hgu = _dot(x, wgu_bf[...])
        hg, hu = hgu[:, :EXPERT_FF], hgu[:, EXPERT_FF:]
        hid = (hg * jax.nn.sigmoid(hg) * hu).astype(BF16)
        g = xg[:, D_MODEL:].astype(F32)
        y = ((g[:, 0:1] + g[:, 1:2]) * _dot(hid, wd_bf[...])).astype(BF16)
        y = jnp.concatenate([y, jnp.zeros((EXPERT_ROWS, LANES), BF16)], axis=1)
        ybuf[slot] = y.reshape(TILE_CHUNKS, CHUNK_ROWS, SORT_WIDTH)
        scatter(i, slot, True)

        @pl.when(i == n_used - 1)
        def _():
            gather(next_tile, 1 - slot, False)
            scatter(i, slot, False)

            @pl.when(i >= 1)
            def _():
                scatter(i - 1, 1 - slot, False)


def _experts(hs, tile_expert, n_used, n_valid, chunks, wg, wu, wd, layer):
    n_tiles = tile_expert.shape[0]
    last = lambda i, te, nu, nv, ch: jnp.minimum(i, nu[0] - 1)
    wmap = lambda i, te, nu, nv, ch: (layer * N_EXPERTS + te[last(i, te, nu, nv, ch)], 0, 0)
    anyspace = pl.BlockSpec(memory_space=pl.ANY)
    grid_spec = pltpu.PrefetchScalarGridSpec(
        num_scalar_prefetch=4,
        grid=(n_tiles,),
        in_specs=[anyspace,
                  pl.BlockSpec((1, D_MODEL, EXPERT_FF), wmap),
                  pl.BlockSpec((1, D_MODEL, EXPERT_FF), wmap),
                  pl.BlockSpec((1, EXPERT_FF, D_MODEL), wmap)],
        out_specs=anyspace,
        scratch_shapes=[pltpu.VMEM((2, TILE_CHUNKS, CHUNK_ROWS, SORT_WIDTH), BF16),
                        pltpu.VMEM((2, TILE_CHUNKS, CHUNK_ROWS, SORT_WIDTH), BF16),
                        pltpu.VMEM((D_MODEL, 2 * EXPERT_FF), BF16), pltpu.VMEM((EXPERT_FF, D_MODEL), BF16),
                        pltpu.SemaphoreType.DMA((2,)), pltpu.SemaphoreType.DMA((2,))],
    )
    return pl.pallas_call(
        _expert_kernel,
        grid_spec=grid_spec,
        out_shape=jax.ShapeDtypeStruct(hs.shape, BF16),
        input_output_aliases={4: 0},
        compiler_params=_params("arbitrary"),
        name="experts",
    )(tile_expert, n_used, n_valid, chunks, hs, wg, wu, wd)


def _combine_kernel(x_ref, route_ref, ys_ref, gfin_ref, o_ref):
    r = lax.broadcasted_iota(I32, (ROW_TILE, LOCAL_ROWS), 1).astype(F32)
    for t in range(COMBINE_TILES_PER_STEP):
        rows = slice(t * ROW_TILE, (t + 1) * ROW_TILE)
        pos = route_ref[rows, :]
        sel = jnp.where(r == pos[:, 0:1], 1.0, jnp.where(r == pos[:, 1:2], 1.0, 0.0)).astype(BF16)
        ys = ys_ref[t * LOCAL_CHUNKS:(t + 1) * LOCAL_CHUNKS].reshape(LOCAL_ROWS, D_MODEL)
        o_ref[rows, :] = _rms(x_ref[rows, :] + _dot(sel, ys), gfin_ref[...])


def _combine(x2, route, ys, gfin):
    t, d = x2.shape
    n = COMBINE_TILES_PER_STEP
    return pl.pallas_call(
        _combine_kernel,
        grid=(t // (n * ROW_TILE),),
        in_specs=[pl.BlockSpec((n * ROW_TILE, d), lambda i: (i, 0)),
                  pl.BlockSpec((n * ROW_TILE, LANES), lambda i: (i, 0)),
                  pl.BlockSpec((n * LOCAL_CHUNKS, CHUNK_ROWS, d), lambda i: (i, 0, 0)),
                  pl.BlockSpec((1, d), lambda i: (0, 0))],
        out_specs=pl.BlockSpec((n * ROW_TILE, d), lambda i: (i, 0)),
        out_shape=jax.ShapeDtypeStruct((t, d), F32),
        compiler_params=_params("arbitrary"),
        name="combine",
    )(x2, route, ys, gfin)


def _chunk_plan(nch, n_tiles):
    nt = nch.shape[0]
    local_start = jnp.cumsum(nch, axis=1) - nch
    cum = jnp.cumsum(nch, axis=0)
    total = cum[-1]
    tiles = (total + TILE_CHUNKS - 1) // TILE_CHUNKS
    tile_end = jnp.cumsum(tiles)
    n_used = tile_end[-1:]
    tile_ids = jnp.arange(n_tiles, dtype=I32)
    tile_expert = jnp.minimum(jnp.sum((tile_end[None, :] <= tile_ids[:, None]).astype(I32), axis=1),
                              N_EXPERTS - 1)
    sel = (tile_expert[:, None] == jnp.arange(N_EXPERTS, dtype=I32)[None, :]).astype(I32)
    pick = lambda table: jnp.sum(sel[:, :, None] * table.T[None, :, :], axis=1)
    first_tile = jnp.sum(sel * (tile_end - tiles)[None, :], axis=1)
    slot = (tile_ids - first_tile)[:, None] * TILE_CHUNKS + jnp.arange(TILE_CHUNKS, dtype=I32)[None, :]
    valid = (slot < jnp.sum(sel * total[None, :], axis=1)[:, None]) & (tile_ids < n_used)[:, None]
    src_tile = jnp.sum((pick(cum)[:, None, :] <= slot[:, :, None]).astype(I32), axis=2)
    src_tile = jnp.minimum(src_tile, nt - 1)
    at = (src_tile[:, :, None] == jnp.arange(nt, dtype=I32)[None, None, :]).astype(I32)
    before = jnp.sum(at * pick(cum - nch)[:, None, :], axis=2)
    start = jnp.sum(at * pick(local_start)[:, None, :], axis=2)
    chunk = jnp.where(valid, src_tile * LOCAL_CHUNKS + start + slot - before, LOCAL_CHUNKS - 1)
    return tile_expert, n_used, jnp.sum(valid.astype(I32), axis=1), chunk.reshape(-1)


def kernel(x, mem, norm_mix_g, w_in, rel_bias, gla_w_alpha, gla_b_alpha, gla_norm_g, w_branch, w_mix_out, norm_x_g, mem_norm_g, w_xq, w_xkv, w_xo, norm_ffn_g, w_group_router, b_group_router, w_expert_router, b_expert_router, w_exp_gate, w_exp_up, w_exp_down, final_norm_g):
    batch, seq, d = x.shape
    depth = w_in.shape[0]
    t = batch * seq
    step_tiles = max(TOKEN_TILES_PER_STEP, GLA_TILES_PER_STEP, BAND_TILES_PER_STEP, COMBINE_TILES_PER_STEP)
    assert d == D_MODEL and seq % (step_tiles * ROW_TILE) == 0 and seq % INPROJ_ROWS == 0
    nt = t // ROW_TILE
    n_tiles = nt * LOCAL_CHUNKS // TILE_CHUNKS + N_EXPERTS

    xf = x.reshape(t, d)
    km_all, vm_all = _memkv(mem, mem_norm_g, w_xkv.astype(BF16))
    row = lambda a: a.reshape(1, -1).astype(F32)

    moe = None
    for l in range(depth):
        res = _inproj(xf, row(norm_mix_g[l]), w_in, gla_w_alpha, row(gla_b_alpha[l]), l, moe)
        if moe is not None:
            xf, res = res[0], res[1:]
        aq, ak, av, bq, bk, bv, lga, br, gates = res

        oa = _band_attention(aq, ak, av, _band_bias_vector(rel_bias[l]), batch)
        ob = _gla(bq, bk, bv, lga, br, row(gla_norm_g[l]), batch)

        wr = jnp.pad(jnp.concatenate([w_group_router[l], w_expert_router[l]], axis=1).astype(F32),
                     ((0, 0), (0, LANES - N_GROUPS - N_EXPERTS)))
        wr_hi = wr.astype(BF16)
        wr = jnp.concatenate([wr_hi, (wr - wr_hi.astype(F32)).astype(BF16)], axis=1)
        brt = jnp.pad(jnp.concatenate([b_group_router[l], b_expert_router[l]]).astype(F32),
                      (0, LANES - N_GROUPS - N_EXPERTS)).reshape(1, LANES)
        x2, hs, route, cnt = _token(
            xf, oa, ob, gates, w_branch[l, 0].astype(BF16), w_branch[l, 1].astype(BF16),
            w_mix_out[l].astype(BF16), row(norm_x_g[l]), w_xq[l].astype(BF16), km_all[l], vm_all[l],
            w_xo[l].astype(BF16), row(norm_ffn_g[l]), wr, brt, batch)

        plan = _chunk_plan(cnt[:, 0, :N_EXPERTS].astype(I32), n_tiles)
        e3 = lambda w: w.reshape((depth * N_EXPERTS,) + w.shape[3:])
        ys = _experts(hs, *plan, e3(w_exp_gate), e3(w_exp_up), e3(w_exp_down), l)
        xf, moe = x2, (route, ys)

    return _combine(x2, route, ys, row(final_norm_g)).reshape(batch, seq, d)
```

```python
import functools

import numpy as np
import jax
import jax.numpy as jnp
from jax import lax
from jax.experimental import pallas as pl
from jax.experimental.pallas import tpu as pltpu

F32 = jnp.float32
BF16 = jnp.bfloat16
I32 = jnp.int32

D_MODEL = 1024
CHUNK = 64
EPS = 1e-6
A_HEADS = 8
A_HEAD_DIM = 64
A_WIDTH = 512
A_LEFT_CHUNKS = 8
A_MAX_REL = 256
B_HEADS = 4
B_KEY_DIM = 64
B_VAL_DIM = 128
B_KEY_WIDTH = 256
B_VAL_WIDTH = 512
B_GATE_RANK = 16
B_GATE_TAU = 16.0
X_HEADS = 4
X_HEAD_DIM = 256
N_GROUPS = 4
EXPERTS_PER_GROUP = 8
N_EXPERTS = N_GROUPS * EXPERTS_PER_GROUP
EXPERT_FF = 256

LANES = 128
ROW_TILE = 256
CHUNKS_PER_TILE = ROW_TILE // CHUNK
BAND_TILES = A_LEFT_CHUNKS // CHUNKS_PER_TILE + 1
BAND_KEYS = BAND_TILES * ROW_TILE
BIAS_PERIOD = 1024
LOG_CHUNK = 6
N_LEVELS = LOG_CHUNK
EXP_ROWS = (2 + N_LEVELS) * CHUNK
CHUNK_ROWS = 16
TOKEN_TILES_PER_STEP = 2
INPROJ_ROWS = 512
COMBINE_TILES_PER_STEP = 2
GLA_TILES_PER_STEP = 4
BAND_TILES_PER_STEP = 4
EXPERT_ROWS = 512
TILE_CHUNKS = EXPERT_ROWS // CHUNK_ROWS
LOCAL_CHUNKS = 2 * ROW_TILE // CHUNK_ROWS + N_EXPERTS
LOCAL_ROWS = LOCAL_CHUNKS * CHUNK_ROWS
SORT_TAIL_CHUNKS = LOCAL_CHUNKS // 4
assert (2 * ROW_TILE + N_EXPERTS * (CHUNK_ROWS - 1)) // CHUNK_ROWS < LOCAL_CHUNKS
SORT_WIDTH = D_MODEL + LANES
NEG = -1e30
LOG2E = 1.4426950408889634
VMEM_LIMIT = 56 * 1024 * 1024


def _params(*sem):
    return pltpu.CompilerParams(dimension_semantics=sem, vmem_limit_bytes=VMEM_LIMIT)


def _rms(x, g):
    return x * lax.rsqrt(jnp.mean(x * x, axis=-1, keepdims=True) + EPS) * g


def _dot(a, b):
    return jnp.dot(a, b, preferred_element_type=F32)


def _dot_nt(a, b):
    return lax.dot_general(a, b, (((1,), (1,)), ((), ())), preferred_element_type=F32)


def _dot_tn(a, b):
    return lax.dot_general(a, b, (((0,), (0,)), ((), ())), preferred_element_type=F32)


def _memkv_kernel(mem_ref, g_ref, w_ref, k_ref, v_ref):
    mn = _rms(mem_ref[0], g_ref[...]).astype(BF16)
    kv = _dot(mn, w_ref[0])
    k_ref[0, 0] = kv[:, :D_MODEL].astype(BF16)
    v_ref[0, 0] = kv[:, D_MODEL:].astype(BF16)


def _memkv(mem, g, w_xkv):
    depth = w_xkv.shape[0]
    b, m, d = mem.shape
    out = jax.ShapeDtypeStruct((depth, b, m, d), BF16)
    return pl.pallas_call(
        _memkv_kernel,
        grid=(depth, b),
        in_specs=[pl.BlockSpec((1, m, d), lambda l, i: (i, 0, 0)),
                  pl.BlockSpec((1, d), lambda l, i: (0, 0)),
                  pl.BlockSpec((1, d, 2 * d), lambda l, i: (l, 0, 0))],
        out_specs=[pl.BlockSpec((1, 1, m, d), lambda l, i: (l, i, 0, 0)),
                   pl.BlockSpec((1, 1, m, d), lambda l, i: (l, i, 0, 0))],
        out_shape=[out, out],
        compiler_params=_params("arbitrary", "arbitrary"),
        name="memkv",
    )(mem, g.reshape(1, d), w_xkv)


_R_AQ, _R_AK, _R_AV = 0, 512, 1024
_R_BQ, _R_BK, _R_BV = 1536, 1792, 2048
_R_ALPHA, _R_BR, _R_GATE, _R_END = 2560, 2576, 3088, 5136
_W_PIECE = 512


def _inproj_kernel(layer, fused, *refs):
    if fused:
        x_ref, route_ref, ys_ref, g_ref, wt_hbm, wal2_ref, bal_ref = refs[:7]
        refs = refs[7:]
        x3_ref, refs = refs[0], refs[1:]
    else:
        x_ref, g_ref, wt_hbm, wal2_ref, bal_ref = refs[:5]
        refs = refs[5:]
    (aq_ref, ak_ref, av_ref, bq_ref, bk_ref, bv_ref, lga_ref, br_ref, gate_ref,
     wt_bf, wal2_bf, stage, sem) = refs

    @pl.when(pl.program_id(0) == 0)
    def _():
        pieces = [(c, min(_W_PIECE, _R_END - c)) for c in range(0, _R_END, _W_PIECE)]

        def piece_copy(p):
            c, n = pieces[p]
            return pltpu.make_async_copy(wt_hbm.at[layer, pl.ds(c, n), :], stage.at[p % 2, pl.ds(0, n), :],
                                         sem.at[p % 2])

        piece_copy(0).start()
        for p, (c, n) in enumerate(pieces):
            if p + 1 < len(pieces):
                piece_copy(p + 1).start()
            piece_copy(p).wait()
            wt_bf[c:c + n, :] = stage[p % 2, 0:n, :].astype(BF16)
        wal2_bf[...] = jnp.concatenate(
            [wal2_ref[0].astype(BF16), jnp.zeros((LANES - B_GATE_RANK, B_KEY_WIDTH), BF16)], axis=0)

    if fused:
        r = lax.broadcasted_iota(I32, (ROW_TILE, LOCAL_ROWS), 1).astype(F32)
        tiles = []
        for t in range(INPROJ_ROWS // ROW_TILE):
            rows = slice(t * ROW_TILE, (t + 1) * ROW_TILE)
            pos = route_ref[rows, :]
            sel = jnp.where(r == pos[:, 0:1], 1.0, jnp.where(r == pos[:, 1:2], 1.0, 0.0)).astype(BF16)
            ys = ys_ref[t * LOCAL_CHUNKS:(t + 1) * LOCAL_CHUNKS].reshape(LOCAL_ROWS, D_MODEL)
            tiles.append(x_ref[rows, :] + _dot(sel, ys))
        x = jnp.concatenate(tiles, axis=0)
        x3_ref[...] = x
    else:
        x = x_ref[...]
    h = _rms(x, g_ref[...]).astype(BF16)

    def mm(lo, hi):
        return _dot_nt(h, wt_bf[lo:hi, :])

    aq_ref[...] = (mm(_R_AQ, _R_AK) * (A_HEAD_DIM ** -0.5 * LOG2E)).astype(BF16)
    ak_ref[...] = mm(_R_AK, _R_AV).astype(BF16)
    av_ref[...] = mm(_R_AV, _R_BQ).astype(BF16)
    bq_ref[...] = (mm(_R_BQ, _R_BK) * (B_KEY_DIM ** -0.5)).astype(BF16)
    bk_ref[...] = mm(_R_BK, _R_BV).astype(BF16)
    bv_ref[...] = mm(_R_BV, _R_ALPHA).astype(BF16)
    r = mm(_R_BR, _R_GATE)
    br_ref[...] = (r * jax.nn.sigmoid(r)).astype(BF16)
    for c in range(_R_GATE, _R_END, 512):
        gate_ref[:, c - _R_GATE:c - _R_GATE + 512] = jax.nn.sigmoid(mm(c, c + 512)).astype(BF16)
    z = _dot(mm(_R_ALPHA, _R_ALPHA + LANES).astype(BF16), wal2_bf[...]) + bal_ref[...]
    lga_ref[...] = (jnp.minimum(z, 0.0) - jnp.log(1.0 + jnp.exp(-jnp.abs(z)))) * (1.0 / B_GATE_TAU)


def _inproj(x, g, w_in, w_al2, b_al, layer, moe=None):
    t, d = x.shape
    assert w_in.shape[2] == _R_END
    row = lambda w: pl.BlockSpec((INPROJ_ROWS, w), lambda i: (i, 0))
    full = lambda a: pl.BlockSpec(a.shape, lambda i: (0,) * a.ndim)
    sds = lambda w, dt: jax.ShapeDtypeStruct((t, w), dt)
    widths = [(512, BF16), (512, BF16), (512, BF16), (256, BF16), (256, BF16), (512, BF16),
              (256, F32), (512, BF16), (2048, BF16)]
    fused = moe is not None
    moe_specs, moe_out_specs, moe_out_shape = [], [], []
    if fused:
        n_chunks = INPROJ_ROWS // ROW_TILE * LOCAL_CHUNKS
        moe_specs = [row(LANES), pl.BlockSpec((n_chunks, CHUNK_ROWS, d), lambda i: (i, 0, 0))]
        moe_out_specs, moe_out_shape = [row(d)], [sds(d, F32)]
    return pl.pallas_call(
        functools.partial(_inproj_kernel, layer, fused),
        grid=(t // INPROJ_ROWS,),
        in_specs=[row(d)] + moe_specs + [full(g), pl.BlockSpec(memory_space=pl.ANY),
                                         pl.BlockSpec((1,) + w_al2.shape[1:], lambda i: (layer, 0, 0)),
                                         full(b_al)],
        out_specs=moe_out_specs + [row(w) for w, _ in widths],
        out_shape=moe_out_shape + [sds(w, dt) for w, dt in widths],
        scratch_shapes=[pltpu.VMEM((_R_END, d), BF16), pltpu.VMEM((LANES, B_KEY_WIDTH), BF16),
                        pltpu.VMEM((2, _W_PIECE, d), F32), pltpu.SemaphoreType.DMA((2,))],
        compiler_params=_params("arbitrary"),
        name="inproj",
    )(x, *(moe or ()), g, jnp.swapaxes(w_in, 1, 2), w_al2, b_al)


def _band_kernel(q_ref, *refs):
    n_win = BAND_TILES_PER_STEP + BAND_TILES - 1
    k_refs, v_refs = refs[:n_win], refs[n_win:2 * n_win]
    u_ref, o_ref, bias_ref = refs[2 * n_win:]
    i = pl.program_id(1)
    lane = lax.broadcasted_iota(I32, (1, LANES), 1)
    low = lane < A_HEAD_DIM
    ones = jnp.ones((BAND_KEYS, LANES), BF16)

    @pl.when((pl.program_id(0) == 0) & (i == 0))
    def _():
        cq = lax.broadcasted_iota(I32, (ROW_TILE, BAND_KEYS), 0) >> LOG_CHUNK
        ck = lax.broadcasted_iota(I32, (ROW_TILE, BAND_KEYS), 1) >> LOG_CHUNK
        valid = (ck >= cq) & (ck <= cq + A_LEFT_CHUNKS)
        for h in range(A_HEADS):
            rows = jnp.broadcast_to(u_ref[h:h + 1, :], (ROW_TILE, BIAS_PERIOD))
            rows = pltpu.roll(rows, BIAS_PERIOD - (ROW_TILE - 1), 1, stride=1, stride_axis=0)
            bias_ref[h // 2, (h % 2) * ROW_TILE:(h % 2 + 1) * ROW_TILE, :] = jnp.where(
                valid, rows[:, :BAND_KEYS], NEG)

    def attend(tile, n_missing):
        rows = pl.ds(tile * ROW_TILE, ROW_TILE)
        window = range(tile, tile + BAND_TILES)
        for p in range(A_HEADS // 2):
            sl = slice(p * LANES, (p + 1) * LANES)
            qp = q_ref[rows, sl]
            zero = jnp.zeros_like(qp)
            q2 = jnp.concatenate([jnp.where(low, qp, zero), jnp.where(low, zero, qp)], axis=0)
            kp = jnp.concatenate([k_refs[j][:, sl] for j in window], axis=0)
            vp = jnp.concatenate([v_refs[j][:, sl] for j in window], axis=0)
            s = _dot_nt(q2, kp) + bias_ref[p]
            if n_missing:
                col = lax.broadcasted_iota(I32, (1, BAND_KEYS), 1)
                s = s + jnp.where(col < n_missing * ROW_TILE, NEG, 0.0).astype(F32)
            pe = jnp.exp2(s - jnp.max(s, axis=-1, keepdims=True)).astype(BF16)
            o2 = _dot(pe, jnp.concatenate([vp, ones], axis=1))
            o = o2[:, :LANES] * (1.0 / o2[:, LANES:])
            o_ref[rows, sl] = jnp.where(low, o[:ROW_TILE], o[ROW_TILE:]).astype(BF16)

    @pl.when(i > 0)
    def _():
        for tile in range(BAND_TILES_PER_STEP):
            attend(tile, 0)

    @pl.when(i == 0)
    def _():
        for tile in range(BAND_TILES_PER_STEP):
            attend(tile, max(BAND_TILES - 1 - tile, 0))


def _band_bias_vector(rel_table):
    h = rel_table.shape[0]
    tab = rel_table.astype(F32) * LOG2E
    shift = A_LEFT_CHUNKS * CHUNK + ROW_TILE - 1
    n_far = shift - A_MAX_REL + 1
    span = ROW_TILE + BAND_KEYS - 1
    assert span - 1 - shift <= A_MAX_REL and span <= BIAS_PERIOD
    u = jnp.concatenate([jnp.broadcast_to(tab[:, 2 * A_MAX_REL:], (h, n_far)),
                         tab[:, 2 * A_MAX_REL - 1:2 * A_MAX_REL - 1 - (span - n_far):-1]], axis=1)
    return jnp.pad(u, ((0, 0), (0, BIAS_PERIOD - span)))


def _band_attention(q, k, v, u, batch):
    t, w = q.shape
    n = BAND_TILES_PER_STEP
    nb = t // batch // (n * ROW_TILE)
    step = pl.BlockSpec((n * ROW_TILE, w), lambda b, i: (b * nb + i, 0))
    tile = lambda j: pl.BlockSpec(
        (ROW_TILE, w), lambda b, i: (n * b * nb + jnp.maximum(n * i + j - (BAND_TILES - 1), 0), 0))
    window = [tile(j) for j in range(n + BAND_TILES - 1)]
    return pl.pallas_call(
        _band_kernel,
        grid=(batch, nb),
        in_specs=[step] + window + window + [pl.BlockSpec(u.shape, lambda b, i: (0, 0))],
        out_specs=step,
        out_shape=jax.ShapeDtypeStruct((t, w), BF16),
        scratch_shapes=[pltpu.VMEM((A_HEADS // 2, 2 * ROW_TILE, BAND_KEYS), F32)],
        compiler_params=_params("arbitrary", "arbitrary"),
        name="band_attn",
    )(q, *([k] * len(window)), *([v] * len(window)), u)


def _gla_constants():
    c = CHUNK
    t = np.arange(c)[:, None]
    r = np.arange(c)[None, :]
    mats = [(r <= t), (r > t)]
    lvl = np.full((c, c), -1, np.int32)
    lvl[np.arange(c), np.arange(c)] = N_LEVELS
    for l in range(N_LEVELS):
        m = (c // 2) >> l
        mid = (t // (2 * m)) * (2 * m) + m
        upper = t >= mid
        mats.append(np.where(upper, (r >= mid) & (r <= t), (r > t) & (r < mid)))
        s = r
        same = (s // (2 * m)) == (t // (2 * m))
        lvl[np.asarray(same & upper & (s < mid))] = l
    eye = np.eye(CHUNKS_PER_TILE)
    mexp = np.concatenate([np.kron(eye, m) for m in mats], axis=0).astype(np.float32)
    lvl = np.tile(lvl, (1, B_HEADS))
    return jnp.asarray(mexp, BF16), jnp.asarray(lvl, I32)


def _gla_kernel(q_ref, k_ref, v_ref, g_ref, r_ref, gn_ref, mexp_ref, lvl_ref, o_ref, s_ref):
    @pl.when(pl.program_id(1) == 0)
    def _():
        s_ref[...] = jnp.zeros_like(s_ref)

    kw = B_KEY_WIDTH
    ri = lax.broadcasted_iota(I32, (kw, kw), 0) >> LOG_CHUNK
    ci = lax.broadcasted_iota(I32, (kw, kw), 1) >> LOG_CHUNK
    bd = ri == ci
    head_ind = jnp.where(bd, 1.0, 0.0).astype(BF16)
    ri2 = lax.broadcasted_iota(I32, (kw, 2 * kw), 0) >> LOG_CHUNK
    ci2 = (lax.broadcasted_iota(I32, (kw, 2 * kw), 1) & (kw - 1)) >> LOG_CHUNK
    bd2 = ri2 == ci2
    lvl = lvl_ref[...]
    row8 = lax.broadcasted_iota(I32, (16, kw), 0)
    ones = jnp.ones((16, LANES), BF16)
    zero_b = jnp.zeros((kw, kw), BF16)
    chunks = [slice(c * CHUNK, (c + 1) * CHUNK) for c in range(CHUNKS_PER_TILE)]

    def head_blocks(x):
        return jnp.where(bd, jnp.concatenate([x] * B_HEADS, axis=0), zero_b)

    def prepare(tile):
        trows = pl.ds(tile * ROW_TILE, ROW_TILE)
        q = q_ref[trows, :].astype(F32)
        k = k_ref[trows, :].astype(F32)
        g = g_ref[trows, :]
        gb = g.astype(BF16)
        half = EXP_ROWS * CHUNKS_PER_TILE // 2
        w = jnp.exp(jnp.concatenate([_dot(mexp_ref[:half, :], gb), _dot(mexp_ref[half:, :], gb)], axis=0))
        qt = (q * w[0:ROW_TILE]).astype(BF16)
        kb = (k * w[ROW_TILE:2 * ROW_TILE]).astype(BF16)
        qk = (q * k).astype(BF16)

        attn = [jnp.zeros((CHUNK, kw), F32) for _ in chunks]
        for l in range(N_LEVELS):
            wl = w[(2 + l) * ROW_TILE:(3 + l) * ROW_TILE]
            qh = (q * wl).astype(BF16)
            kh = (k * wl).astype(BF16)
            for c, rows in enumerate(chunks):
                attn[c] = jnp.where(lvl == l, _dot_nt(qh[rows], head_blocks(kh[rows])), attn[c])

        out = []
        for c, rows in enumerate(chunks):
            a = jnp.where(lvl == N_LEVELS, _dot(qk[rows], head_ind), attn[c])
            v = v_ref[pl.ds(tile * ROW_TILE + c * CHUNK, CHUNK), :]
            vstack = jnp.concatenate([v[:, j * LANES:(j + 1) * LANES] for j in range(B_HEADS)], axis=0)
            kv = _dot_tn(head_blocks(kb[rows]), vstack)
            d = jnp.exp(jnp.sum(g[rows], axis=0, keepdims=True))
            d1 = d.astype(BF16).astype(F32)
            dp = jnp.where(row8 == 0, d1, jnp.where(row8 == 1, d - d1, 0.0)).astype(BF16)
            dcol = _dot_tn(dp, ones)
            out.append((a.astype(BF16), qt[rows], vstack, kv, dcol))
        return out

    prepared = [p for tile in range(GLA_TILES_PER_STEP) for p in prepare(tile)]

    s = s_ref[...]
    for c, (a, qtc, vstack, kv, dcol) in enumerate(prepared):
        rows = pl.ds(c * CHUNK, CHUNK)
        lhs = jnp.concatenate([a, qtc], axis=1)
        lhs = jnp.where(bd2, jnp.concatenate([lhs] * B_HEADS, axis=0), jnp.zeros((kw, 2 * kw), BF16))
        rhs = jnp.concatenate([vstack, s.astype(BF16)], axis=0)
        o = _dot(lhs, rhs)
        s = dcol * s + kv
        for j in range(B_HEADS):
            oj = o[j * CHUNK:(j + 1) * CHUNK]
            sl = slice(j * LANES, (j + 1) * LANES)
            y = oj * lax.rsqrt(jnp.mean(oj * oj, axis=-1, keepdims=True) + EPS) * gn_ref[...]
            o_ref[rows, sl] = (y * r_ref[rows, sl].astype(F32)).astype(BF16)
    s_ref[...] = s


def _gla(q, k, v, g, r, gn, batch):
    t = q.shape[0]
    nb = t // batch // (GLA_TILES_PER_STEP * ROW_TILE)
    mexp, lvl = _gla_constants()
    cur = lambda b, i: (b * nb + i, 0)
    blk = lambda w: pl.BlockSpec((GLA_TILES_PER_STEP * ROW_TILE, w), cur)
    full = lambda a: pl.BlockSpec(a.shape, lambda b, i: (0,) * a.ndim)
    return pl.pallas_call(
        _gla_kernel,
        grid=(batch, nb),
        in_specs=[blk(B_KEY_WIDTH), blk(B_KEY_WIDTH), blk(B_VAL_WIDTH), blk(B_KEY_WIDTH), blk(B_VAL_WIDTH),
                  full(gn), full(mexp), full(lvl)],
        out_specs=blk(B_VAL_WIDTH),
        out_shape=jax.ShapeDtypeStruct((t, B_VAL_WIDTH), BF16),
        scratch_shapes=[pltpu.VMEM((B_KEY_WIDTH, B_VAL_DIM), F32)],
        compiler_params=_params("arbitrary", "arbitrary"),
        name="gla",
    )(q, k, v, g, r, gn, mexp, lvl)


def _token_kernel(x_ref, oa_ref, ob_ref, gate_ref, wb0_ref, wb1_ref, wmix_ref, gx_ref, wq_ref,
                  km_ref, vm_ref, wo_ref, gf_ref, wr_ref, br_ref, ltri_ref, utri_ref,
                  x2_ref, hs_ref, route_ref, cnt_ref):
    ma = _dot(oa_ref[...], wb0_ref[...])
    mb = _dot(ob_ref[...], wb1_ref[...])
    merged = (gate_ref[:, :D_MODEL].astype(F32) * ma + gate_ref[:, D_MODEL:].astype(F32) * mb).astype(BF16)
    x1 = x_ref[...] + _dot(merged, wmix_ref[...])

    h2 = _rms(x1, gx_ref[...]).astype(BF16)
    qx = (_dot(h2, wq_ref[...]) * (X_HEAD_DIM ** -0.5)).astype(BF16)
    heads = []
    for h in range(X_HEADS):
        sl = slice(h * X_HEAD_DIM, (h + 1) * X_HEAD_DIM)
        s = _dot_nt(qx[:, sl], km_ref[0, :, sl])
        m = jnp.max(s, axis=-1, keepdims=True)
        pe = jnp.exp(s - m)
        l = jnp.sum(pe, axis=-1, keepdims=True)
        heads.append((_dot(pe.astype(BF16), vm_ref[0, :, sl]) * (1.0 / l)).astype(BF16))
    x2 = x1 + _dot(jnp.concatenate(heads, axis=1), wo_ref[...])
    x2_ref[...] = x2

    h3 = _rms(x2, gf_ref[...])

    h3_hi = h3.astype(BF16)
    h3_lo = (h3 - h3_hi.astype(F32)).astype(BF16)
    hw = _dot(h3_hi, wr_ref[...])
    logits = hw[:, :LANES] + hw[:, LANES:] + _dot(h3_lo, wr_ref[:, :LANES]) + br_ref[...]
    oh0, oh1, g0, g1 = _route(logits)
    for h in range(TOKEN_TILES_PER_STEP):
        rows = slice(h * ROW_TILE, (h + 1) * ROW_TILE)
        chunks = pl.ds(h * LOCAL_CHUNKS, LOCAL_CHUNKS)
        _sort_tile(oh0[rows], oh1[rows], g0[rows], g1[rows], h3_hi[rows], ltri_ref, utri_ref,
                   hs_ref.at[chunks], route_ref.at[pl.ds(h * ROW_TILE, ROW_TILE)], cnt_ref.at[h])


def _route(logits):
    lane = lax.broadcasted_iota(I32, logits.shape, 1).astype(F32)
    big = jnp.float32(LANES)
    gl = jnp.where(lane < N_GROUPS, logits, NEG)
    gmax = jnp.max(gl, axis=-1, keepdims=True)
    gidx = jnp.min(jnp.where(gl == gmax, lane, big), axis=-1, keepdims=True)
    g_w = 1.0 / jnp.sum(jnp.exp(gl - gmax), axis=-1, keepdims=True)
    lo = N_GROUPS + EXPERTS_PER_GROUP * gidx
    el = jnp.where((lane >= lo) & (lane < lo + EXPERTS_PER_GROUP), logits, NEG)
    v1 = jnp.max(el, axis=-1, keepdims=True)
    i1 = jnp.min(jnp.where(el == v1, lane, big), axis=-1, keepdims=True)
    el2 = jnp.where(lane == i1, NEG, el)
    v2 = jnp.max(el2, axis=-1, keepdims=True)
    i2 = jnp.min(jnp.where(el2 == v2, lane, big), axis=-1, keepdims=True)
    e21 = jnp.exp(v2 - v1)
    w1 = g_w / (1.0 + e21)

    def gate_cols(w):
        hi = w.astype(BF16).astype(F32)
        return jnp.where(lane == 0, hi, jnp.where(lane == 1, w - hi, 0.0)).astype(BF16)

    oh0 = jnp.where(lane == i1 - N_GROUPS, 1.0, 0.0)
    oh1 = jnp.where(lane == i2 - N_GROUPS, 1.0, 0.0)
    return oh0, oh1, gate_cols(w1), gate_cols(w1 * e21)


def _sort_tile(oh0, oh1, g0, g1, h3_hi, ltri_ref, utri_ref, hs_ref, route_ref, cnt_ref):
    lane = lax.broadcasted_iota(I32, oh0.shape, 1)
    oh = oh0 + oh1
    nch = jnp.floor((jnp.sum(oh, axis=0, keepdims=True) + (CHUNK_ROWS - 1)) * (1.0 / CHUNK_ROWS))
    nch8 = jnp.broadcast_to(nch, (8, LANES))
    start = _dot(nch8.astype(BF16), utri_ref[...])[0:1] * CHUNK_ROWS
    rank = _dot(ltri_ref[...], oh.astype(BF16))
    row = start + rank
    pos0 = jnp.sum(row * oh0, axis=-1, keepdims=True)
    pos1 = jnp.sum(row * oh1, axis=-1, keepdims=True)
    route = jnp.where(lane == 0, pos0, jnp.where(lane == 1, pos1, 0.0))
    route_t = jnp.transpose(route)

    def sort_rows(lo, n):
        r = (lax.broadcasted_iota(I32, (n, ROW_TILE), 0) + lo).astype(F32)
        p0 = jnp.where(r == route_t[0:1, :], 1.0, 0.0).astype(BF16)
        p1 = jnp.where(r == route_t[1:2, :], 1.0, 0.0).astype(BF16)
        rows = jnp.concatenate([_dot(p0 + p1, h3_hi), _dot(p0, g0) + _dot(p1, g1)], axis=1)
        return rows.astype(BF16).reshape(n // CHUNK_ROWS, CHUNK_ROWS, SORT_WIDTH)

    tail = LOCAL_CHUNKS - SORT_TAIL_CHUNKS
    hs_ref[:tail] = sort_rows(0, tail * CHUNK_ROWS)
    tail_used = jnp.sum(nch) > tail

    @pl.when(tail_used)
    def _():
        hs_ref[tail:] = sort_rows(tail * CHUNK_ROWS, SORT_TAIL_CHUNKS * CHUNK_ROWS)

    @pl.when(jnp.logical_not(tail_used))
    def _():
        hs_ref[tail:] = jnp.zeros((SORT_TAIL_CHUNKS, CHUNK_ROWS, SORT_WIDTH), BF16)

    route_ref[...] = route
    cnt_ref[...] = nch8


def _token(x, oa, ob, gates, wb0, wb1, wmix, gx, wq, km, vm, wo, gf, wr, br, batch):
    t, d = x.shape
    n = TOKEN_TILES_PER_STEP
    nb = t // batch // (n * ROW_TILE)
    nt = t // ROW_TILE
    ltri = jnp.asarray(np.tril(np.ones((ROW_TILE, ROW_TILE), np.float32), -1), BF16)
    utri = jnp.asarray(np.triu(np.ones((LANES, LANES), np.float32), 1), BF16)
    cur = lambda b, i: (b * nb + i, 0)
    cur3 = lambda b, i: (b * nb + i, 0, 0)
    blk = lambda w: pl.BlockSpec((n * ROW_TILE, w), cur)
    full = lambda a: pl.BlockSpec(a.shape, lambda b, i: (0,) * a.ndim)
    mem = pl.BlockSpec((1,) + km.shape[1:], lambda b, i: (b, 0, 0))
    return pl.pallas_call(
        _token_kernel,
        grid=(batch, nb),
        in_specs=[blk(d), blk(A_WIDTH), blk(B_VAL_WIDTH), blk(2 * d), full(wb0), full(wb1), full(wmix),
                  full(gx), full(wq), mem, mem, full(wo), full(gf), full(wr), full(br), full(ltri), full(utri)],
        out_specs=[blk(d), pl.BlockSpec((n * LOCAL_CHUNKS, CHUNK_ROWS, SORT_WIDTH), cur3),
                   blk(LANES), pl.BlockSpec((n, 8, LANES), cur3)],
        out_shape=[jax.ShapeDtypeStruct((t, d), F32),
                   jax.ShapeDtypeStruct((nt * LOCAL_CHUNKS, CHUNK_ROWS, SORT_WIDTH), BF16),
                   jax.ShapeDtypeStruct((t, LANES), F32),
                   jax.ShapeDtypeStruct((nt, 8, LANES), F32)],
        compiler_params=_params("arbitrary", "arbitrary"),
        name="token",
    )(x, oa, ob, gates, wb0, wb1, wmix, gx, wq, km, vm, wo, gf, wr, br, ltri, utri)


def _expert_kernel(te_ref, nu_ref, nv_ref, ch_ref, hs_hbm, wg_ref, wu_ref, wd_ref, ys_hbm,
                   xbuf, ybuf, wgu_bf, wd_bf, gsem, ssem):
    i = pl.program_id(0)
    n_used = nu_ref[0]
    slot = lax.rem(i, 2)

    def for_chunks(tile, fn):
        nv = nv_ref[tile]

        @pl.when(nv == TILE_CHUNKS)
        def _():
            for c in range(TILE_CHUNKS):
                fn(c)

        @pl.when(nv != TILE_CHUNKS)
        def _():
            def body(c, carry):
                fn(c)
                return carry

            lax.fori_loop(0, nv, body, 0)

    def gather(tile, s, start):
        for c in range(TILE_CHUNKS):
            cp = pltpu.make_async_copy(hs_hbm.at[ch_ref[tile * TILE_CHUNKS + c]], xbuf.at[s, c], gsem.at[s])
            cp.start() if start else cp.wait()

    def scatter(tile, s, start):
        def one(c):
            cp = pltpu.make_async_copy(ybuf.at[s, c], ys_hbm.at[ch_ref[tile * TILE_CHUNKS + c]], ssem.at[s])
            cp.start() if start else cp.wait()

        for_chunks(tile, one)

    next_tile = jnp.minimum(i + 1, pl.num_programs(0) - 1)

    @pl.when(i == 0)
    def _():
        gather(0, 0, True)

    @pl.when(i < n_used)
    def _():
        gather(next_tile, 1 - slot, True)
        gather(i, slot, False)

        @pl.when(i >= 2)
        def _():
            scatter(i - 2, slot, False)

        @pl.when((i == 0) | (te_ref[i] != te_ref[jnp.maximum(i - 1, 0)]))
        def _():
            wgu_bf[:, :EXPERT_FF] = wg_ref[0].astype(BF16)
            wgu_bf[:, EXPERT_FF:] = wu_ref[0].astype(BF16)
            wd_bf[...] = wd_ref[0].astype(BF16)

        xg = xbuf[slot].reshape(EXPERT_ROWS, SORT_WIDTH)
        hgu = _dot(xg[:, :D_MODEL], wgu_bf[...])
        hg, hu = hgu[:, :EXPERT_FF], hgu[:, EXPERT_FF:]
        hid = (hg * jax.nn.sigmoid(hg) * hu).astype(BF16)
        g = xg[:, D_MODEL:].astype(F32)
        y = ((g[:, 0:1] + g[:, 1:2]) * _dot(hid, wd_bf[...])).astype(BF16)
        y = jnp.concatenate([y, jnp.zeros((EXPERT_ROWS, LANES), BF16)], axis=1)
        ybuf[slot] = y.reshape(TILE_CHUNKS, CHUNK_ROWS, SORT_WIDTH)
        scatter(i, slot, True)

        @pl.when(i == n_used - 1)
        def _():
            gather(next_tile, 1 - slot, False)
            scatter(i, slot, False)

            @pl.when(i >= 1)
            def _():
                scatter(i - 1, 1 - slot, False)


def _experts(hs, tile_expert, n_used, n_valid, chunks, wg, wu, wd, layer):
    n_tiles = tile_expert.shape[0]
    last = lambda i, te, nu, nv, ch: jnp.minimum(i, nu[0] - 1)
    wmap = lambda i, te, nu, nv, ch: (layer * N_EXPERTS + te[last(i, te, nu, nv, ch)], 0, 0)
    anyspace = pl.BlockSpec(memory_space=pl.ANY)
    grid_spec = pltpu.PrefetchScalarGridSpec(
        num_scalar_prefetch=4,
        grid=(n_tiles,),
        in_specs=[anyspace,
                  pl.BlockSpec((1, D_MODEL, EXPERT_FF), wmap),
                  pl.BlockSpec((1, D_MODEL, EXPERT_FF), wmap),
                  pl.BlockSpec((1, EXPERT_FF, D_MODEL), wmap)],
        out_specs=anyspace,
        scratch_shapes=[pltpu.VMEM((2, TILE_CHUNKS, CHUNK_ROWS, SORT_WIDTH), BF16),
                        pltpu.VMEM((2, TILE_CHUNKS, CHUNK_ROWS, SORT_WIDTH), BF16),
                        pltpu.VMEM((D_MODEL, 2 * EXPERT_FF), BF16), pltpu.VMEM((EXPERT_FF, D_MODEL), BF16),
                        pltpu.SemaphoreType.DMA((2,)), pltpu.SemaphoreType.DMA((2,))],
    )
    return pl.pallas_call(
        _expert_kernel,
        grid_spec=grid_spec,
        out_shape=jax.ShapeDtypeStruct(hs.shape, BF16),
        input_output_aliases={4: 0},
        compiler_params=_params("arbitrary"),
        name="experts",
    )(tile_expert, n_used, n_valid, chunks, hs, wg, wu, wd)


def _combine_kernel(x_ref, route_ref, ys_ref, gfin_ref, o_ref):
    r = lax.broadcasted_iota(I32, (ROW_TILE, LOCAL_ROWS), 1).astype(F32)
    for t in range(COMBINE_TILES_PER_STEP):
        rows = slice(t * ROW_TILE, (t + 1) * ROW_TILE)
        pos = route_ref[rows, :]
        sel = jnp.where(r == pos[:, 0:1], 1.0, jnp.where(r == pos[:, 1:2], 1.0, 0.0)).astype(BF16)
        ys = ys_ref[t * LOCAL_CHUNKS:(t + 1) * LOCAL_CHUNKS].reshape(LOCAL_ROWS, D_MODEL)
        o_ref[rows, :] = _rms(x_ref[rows, :] + _dot(sel, ys), gfin_ref[...])


def _combine(x2, route, ys, gfin):
    t, d = x2.shape
    n = COMBINE_TILES_PER_STEP
    return pl.pallas_call(
        _combine_kernel,
        grid=(t // (n * ROW_TILE),),
        in_specs=[pl.BlockSpec((n * ROW_TILE, d), lambda i: (i, 0)),
                  pl.BlockSpec((n * ROW_TILE, LANES), lambda i: (i, 0)),
                  pl.BlockSpec((n * LOCAL_CHUNKS, CHUNK_ROWS, d), lambda i: (i, 0, 0)),
                  pl.BlockSpec((1, d), lambda i: (0, 0))],
        out_specs=pl.BlockSpec((n * ROW_TILE, d), lambda i: (i, 0)),
        out_shape=jax.ShapeDtypeStruct((t, d), F32),
        compiler_params=_params("arbitrary"),
        name="combine",
    )(x2, route, ys, gfin)


def _chunk_plan(nch, n_tiles):
    nt = nch.shape[0]
    local_start = jnp.cumsum(nch, axis=1) - nch
    cum = jnp.cumsum(nch, axis=0)
    total = cum[-1]
    tiles = (total + TILE_CHUNKS - 1) // TILE_CHUNKS
    tile_end = jnp.cumsum(tiles)
    n_used = tile_end[-1:]
    tile_ids = jnp.arange(n_tiles, dtype=I32)
    tile_expert = jnp.minimum(jnp.sum((tile_end[None, :] <= tile_ids[:, None]).astype(I32), axis=1),
                              N_EXPERTS - 1)
    sel = (tile_expert[:, None] == jnp.arange(N_EXPERTS, dtype=I32)[None, :]).astype(I32)
    pick = lambda table: jnp.sum(sel[:, :, None] * table.T[None, :, :], axis=1)
    first_tile = jnp.sum(sel * (tile_end - tiles)[None, :], axis=1)
    slot = (tile_ids - first_tile)[:, None] * TILE_CHUNKS + jnp.arange(TILE_CHUNKS, dtype=I32)[None, :]
    valid = (slot < jnp.sum(sel * total[None, :], axis=1)[:, None]) & (tile_ids < n_used)[:, None]
    src_tile = jnp.sum((pick(cum)[:, None, :] <= slot[:, :, None]).astype(I32), axis=2)
    src_tile = jnp.minimum(src_tile, nt - 1)
    at = (src_tile[:, :, None] == jnp.arange(nt, dtype=I32)[None, None, :]).astype(I32)
    before = jnp.sum(at * pick(cum - nch)[:, None, :], axis=2)
    start = jnp.sum(at * pick(local_start)[:, None, :], axis=2)
    chunk = jnp.where(valid, src_tile * LOCAL_CHUNKS + start + slot - before, LOCAL_CHUNKS - 1)
    return tile_expert, n_used, jnp.sum(valid.astype(I32), axis=1), chunk.reshape(-1)


def kernel(x, mem, norm_mix_g, w_in, rel_bias, gla_w_alpha, gla_b_alpha, gla_norm_g, w_branch, w_mix_out, norm_x_g, mem_norm_g, w_xq, w_xkv, w_xo, norm_ffn_g, w_group_router, b_group_router, w_expert_router, b_expert_router, w_exp_gate, w_exp_up, w_exp_down, final_norm_g):
    batch, seq, d = x.shape
    depth = w_in.shape[0]
    t = batch * seq
    step_tiles = max(TOKEN_TILES_PER_STEP, GLA_TILES_PER_STEP, BAND_TILES_PER_STEP, COMBINE_TILES_PER_STEP)
    assert d == D_MODEL and seq % (step_tiles * ROW_TILE) == 0 and seq % INPROJ_ROWS == 0
    nt = t // ROW_TILE
    n_tiles = nt * LOCAL_CHUNKS // TILE_CHUNKS + N_EXPERTS

    xf = x.reshape(t, d)
    km_all, vm_all = _memkv(mem, mem_norm_g, w_xkv.astype(BF16))
    row = lambda a: a.reshape(1, -1).astype(F32)

    moe = None
    for l in range(depth):
        res = _inproj(xf, row(norm_mix_g[l]), w_in, gla_w_alpha, row(gla_b_alpha[l]), l, moe)
        if moe is not None:
            xf, res = res[0], res[1:]
        aq, ak, av, bq, bk, bv, lga, br, gates = res

        oa = _band_attention(aq, ak, av, _band_bias_vector(rel_bias[l]), batch)
        ob = _gla(bq, bk, bv, lga, br, row(gla_norm_g[l]), batch)

        wr = jnp.pad(jnp.concatenate([w_group_router[l], w_expert_router[l]], axis=1).astype(F32),
                     ((0, 0), (0, LANES - N_GROUPS - N_EXPERTS)))
        wr_hi = wr.astype(BF16)
        wr = jnp.concatenate([wr_hi, (wr - wr_hi.astype(F32)).astype(BF16)], axis=1)
        brt = jnp.pad(jnp.concatenate([b_group_router[l], b_expert_router[l]]).astype(F32),
                      (0, LANES - N_GROUPS - N_EXPERTS)).reshape(1, LANES)
        x2, hs, route, cnt = _token(
            xf, oa, ob, gates, w_branch[l, 0].astype(BF16), w_branch[l, 1].astype(BF16),
            w_mix_out[l].astype(BF16), row(norm_x_g[l]), w_xq[l].astype(BF16), km_all[l], vm_all[l],
            w_xo[l].astype(BF16), row(norm_ffn_g[l]), wr, brt, batch)

        plan = _chunk_plan(cnt[:, 0, :N_EXPERTS].astype(I32), n_tiles)
        e3 = lambda w: w.reshape((depth * N_EXPERTS,) + w.shape[3:])
        ys = _experts(hs, *plan, e3(w_exp_gate), e3(w_exp_up), e3(w_exp_down), l)
        xf, moe = x2, (route, ys)

    return _combine(x2, route, ys, row(final_norm_g)).reshape(batch, seq, d)
```

```python
import functools

import numpy as np
import jax
import jax.numpy as jnp
from jax import lax
from jax.experimental import pallas as pl
from jax.experimental.pallas import tpu as pltpu

F32 = jnp.float32
BF16 = jnp.bfloat16
I32 = jnp.int32

D_MODEL = 1024
CHUNK = 64
EPS = 1e-6
A_HEADS = 8
A_HEAD_DIM = 64
A_WIDTH = 512
A_LEFT_CHUNKS = 8
A_MAX_REL = 256
B_HEADS = 4
B_KEY_DIM = 64
B_VAL_DIM = 128
B_KEY_WIDTH = 256
B_VAL_WIDTH = 512
B_GATE_RANK = 16
B_GATE_TAU = 16.0
X_HEADS = 4
X_HEAD_DIM = 256
N_GROUPS = 4
EXPERTS_PER_GROUP = 8
N_EXPERTS = N_GROUPS * EXPERTS_PER_GROUP
EXPERT_FF = 256

LANES = 128
ROW_TILE = 256
CHUNKS_PER_TILE = ROW_TILE // CHUNK
BAND_TILES = A_LEFT_CHUNKS // CHUNKS_PER_TILE + 1
BAND_KEYS = BAND_TILES * ROW_TILE
BIAS_PERIOD = 1024
LOG_CHUNK = 6
N_LEVELS = LOG_CHUNK
EXP_ROWS = (2 + N_LEVELS) * CHUNK
CHUNK_ROWS = 16
TOKEN_TILES_PER_STEP = 2
INPROJ_ROWS = 512
COMBINE_TILES_PER_STEP = 2
GLA_TILES_PER_STEP = 4
BAND_TILES_PER_STEP = 4
EXPERT_ROWS = 512
TILE_CHUNKS = EXPERT_ROWS // CHUNK_ROWS
LOCAL_CHUNKS = 2 * ROW_TILE // CHUNK_ROWS + N_EXPERTS
LOCAL_ROWS = LOCAL_CHUNKS * CHUNK_ROWS
assert (2 * ROW_TILE + N_EXPERTS * (CHUNK_ROWS - 1)) // CHUNK_ROWS < LOCAL_CHUNKS
SORT_WIDTH = D_MODEL + LANES
NEG = -1e30
LOG2E = 1.4426950408889634
VMEM_LIMIT = 56 * 1024 * 1024


def _params(*sem):
    return pltpu.CompilerParams(dimension_semantics=sem, vmem_limit_bytes=VMEM_LIMIT)


def _rms(x, g):
    return x * lax.rsqrt(jnp.mean(x * x, axis=-1, keepdims=True) + EPS) * g


def _dot(a, b):
    return jnp.dot(a, b, preferred_element_type=F32)


def _dot_nt(a, b):
    return lax.dot_general(a, b, (((1,), (1,)), ((), ())), preferred_element_type=F32)


def _dot_tn(a, b):
    return lax.dot_general(a, b, (((0,), (0,)), ((), ())), preferred_element_type=F32)


def _memkv_kernel(mem_ref, g_ref, w_ref, k_ref, v_ref):
    mn = _rms(mem_ref[0], g_ref[...]).astype(BF16)
    kv = _dot(mn, w_ref[0])
    k_ref[0, 0] = kv[:, :D_MODEL].astype(BF16)
    v_ref[0, 0] = kv[:, D_MODEL:].astype(BF16)


def _memkv(mem, g, w_xkv):
    depth = w_xkv.shape[0]
    b, m, d = mem.shape
    out = jax.ShapeDtypeStruct((depth, b, m, d), BF16)
    return pl.pallas_call(
        _memkv_kernel,
        grid=(depth, b),
        in_specs=[pl.BlockSpec((1, m, d), lambda l, i: (i, 0, 0)),
                  pl.BlockSpec((1, d), lambda l, i: (0, 0)),
                  pl.BlockSpec((1, d, 2 * d), lambda l, i: (l, 0, 0))],
        out_specs=[pl.BlockSpec((1, 1, m, d), lambda l, i: (l, i, 0, 0)),
                   pl.BlockSpec((1, 1, m, d), lambda l, i: (l, i, 0, 0))],
        out_shape=[out, out],
        compiler_params=_params("arbitrary", "arbitrary"),
        name="memkv",
    )(mem, g.reshape(1, d), w_xkv)


_R_AQ, _R_AK, _R_AV = 0, 512, 1024
_R_BQ, _R_BK, _R_BV = 1536, 1792, 2048
_R_ALPHA, _R_BR, _R_GATE, _R_END = 2560, 2576, 3088, 5136
_W_PIECE = 512


def _inproj_kernel(layer, fused, *refs):
    if fused:
        x_ref, route_ref, ys_ref, g_ref, wt_hbm, wal2_ref, bal_ref = refs[:7]
        refs = refs[7:]
        x3_ref, refs = refs[0], refs[1:]
    else:
        x_ref, g_ref, wt_hbm, wal2_ref, bal_ref = refs[:5]
        refs = refs[5:]
    (aq_ref, ak_ref, av_ref, bq_ref, bk_ref, bv_ref, lga_ref, br_ref, gate_ref,
     wt_bf, wal2_bf, stage, sem) = refs

    @pl.when(pl.program_id(0) == 0)
    def _():
        pieces = [(c, min(_W_PIECE, _R_END - c)) for c in range(0, _R_END, _W_PIECE)]

        def piece_copy(p):
            c, n = pieces[p]
            return pltpu.make_async_copy(wt_hbm.at[layer, pl.ds(c, n), :], stage.at[p % 2, pl.ds(0, n), :],
                                         sem.at[p % 2])

        piece_copy(0).start()
        for p, (c, n) in enumerate(pieces):
            if p + 1 < len(pieces):
                piece_copy(p + 1).start()
            piece_copy(p).wait()
            wt_bf[c:c + n, :] = stage[p % 2, 0:n, :].astype(BF16)
        wal2_bf[...] = jnp.concatenate(
            [wal2_ref[0].astype(BF16), jnp.zeros((LANES - B_GATE_RANK, B_KEY_WIDTH), BF16)], axis=0)

    if fused:
        r = lax.broadcasted_iota(I32, (ROW_TILE, LOCAL_ROWS), 1).astype(F32)
        tiles = []
        for t in range(INPROJ_ROWS // ROW_TILE):
            rows = slice(t * ROW_TILE, (t + 1) * ROW_TILE)
            pos = route_ref[rows, :]
            sel = jnp.where(r == pos[:, 0:1], 1.0, jnp.where(r == pos[:, 1:2], 1.0, 0.0)).astype(BF16)
            ys = ys_ref[t * LOCAL_CHUNKS:(t + 1) * LOCAL_CHUNKS].reshape(LOCAL_ROWS, D_MODEL)
            tiles.append(x_ref[rows, :] + _dot(sel, ys))
        x = jnp.concatenate(tiles, axis=0)
        x3_ref[...] = x
    else:
        x = x_ref[...]
    h = _rms(x, g_ref[...]).astype(BF16)

    def mm(lo, hi):
        return _dot_nt(h, wt_bf[lo:hi, :])

    aq_ref[...] = (mm(_R_AQ, _R_AK) * (A_HEAD_DIM ** -0.5 * LOG2E)).astype(BF16)
    ak_ref[...] = mm(_R_AK, _R_AV).astype(BF16)
    av_ref[...] = mm(_R_AV, _R_BQ).astype(BF16)
    bq_ref[...] = (mm(_R_BQ, _R_BK) * (B_KEY_DIM ** -0.5)).astype(BF16)
    bk_ref[...] = mm(_R_BK, _R_BV).astype(BF16)
    bv_ref[...] = mm(_R_BV, _R_ALPHA).astype(BF16)
    r = mm(_R_BR, _R_GATE)
    br_ref[...] = (r * jax.nn.sigmoid(r)).astype(BF16)
    for c in range(_R_GATE, _R_END, 512):
        gate_ref[:, c - _R_GATE:c - _R_GATE + 512] = jax.nn.sigmoid(mm(c, c + 512)).astype(BF16)
    z = _dot(mm(_R_ALPHA, _R_ALPHA + LANES).astype(BF16), wal2_bf[...]) + bal_ref[...]
    lga_ref[...] = (jnp.minimum(z, 0.0) - jnp.log(1.0 + jnp.exp(-jnp.abs(z)))) * (1.0 / B_GATE_TAU)


def _inproj(x, g, w_in, w_al2, b_al, layer, moe=None):
    t, d = x.shape
    assert w_in.shape[2] == _R_END
    row = lambda w: pl.BlockSpec((INPROJ_ROWS, w), lambda i: (i, 0))
    full = lambda a: pl.BlockSpec(a.shape, lambda i: (0,) * a.ndim)
    sds = lambda w, dt: jax.ShapeDtypeStruct((t, w), dt)
    widths = [(512, BF16), (512, BF16), (512, BF16), (256, BF16), (256, BF16), (512, BF16),
              (256, F32), (512, BF16), (2048, BF16)]
    fused = moe is not None
    moe_specs, moe_out_specs, moe_out_shape = [], [], []
    if fused:
        n_chunks = INPROJ_ROWS // ROW_TILE * LOCAL_CHUNKS
        moe_specs = [row(LANES), pl.BlockSpec((n_chunks, CHUNK_ROWS, d), lambda i: (i, 0, 0))]
        moe_out_specs, moe_out_shape = [row(d)], [sds(d, F32)]
    return pl.pallas_call(
        functools.partial(_inproj_kernel, layer, fused),
        grid=(t // INPROJ_ROWS,),
        in_specs=[row(d)] + moe_specs + [full(g), pl.BlockSpec(memory_space=pl.ANY),
                                         pl.BlockSpec((1,) + w_al2.shape[1:], lambda i: (layer, 0, 0)),
                                         full(b_al)],
        out_specs=moe_out_specs + [row(w) for w, _ in widths],
        out_shape=moe_out_shape + [sds(w, dt) for w, dt in widths],
        scratch_shapes=[pltpu.VMEM((_R_END, d), BF16), pltpu.VMEM((LANES, B_KEY_WIDTH), BF16),
                        pltpu.VMEM((2, _W_PIECE, d), F32), pltpu.SemaphoreType.DMA((2,))],
        compiler_params=_params("arbitrary"),
        name="inproj",
    )(x, *(moe or ()), g, jnp.swapaxes(w_in, 1, 2), w_al2, b_al)


def _band_kernel(q_ref, *refs):
    n_win = BAND_TILES_PER_STEP + BAND_TILES - 1
    k_refs, v_refs = refs[:n_win], refs[n_win:2 * n_win]
    u_ref, o_ref, bias_ref = refs[2 * n_win:]
    i = pl.program_id(1)
    lane = lax.broadcasted_iota(I32, (1, LANES), 1)
    low = lane < A_HEAD_DIM
    ones = jnp.ones((BAND_KEYS, LANES), BF16)

    @pl.when((pl.program_id(0) == 0) & (i == 0))
    def _():
        cq = lax.broadcasted_iota(I32, (ROW_TILE, BAND_KEYS), 0) >> LOG_CHUNK
        ck = lax.broadcasted_iota(I32, (ROW_TILE, BAND_KEYS), 1) >> LOG_CHUNK
        valid = (ck >= cq) & (ck <= cq + A_LEFT_CHUNKS)
        for h in range(A_HEADS):
            rows = jnp.broadcast_to(u_ref[h:h + 1, :], (ROW_TILE, BIAS_PERIOD))
            rows = pltpu.roll(rows, BIAS_PERIOD - (ROW_TILE - 1), 1, stride=1, stride_axis=0)
            bias_ref[h // 2, (h % 2) * ROW_TILE:(h % 2 + 1) * ROW_TILE, :] = jnp.where(
                valid, rows[:, :BAND_KEYS], NEG)

    def attend(tile, n_missing):
        rows = pl.ds(tile * ROW_TILE, ROW_TILE)
        window = range(tile, tile + BAND_TILES)
        for p in range(A_HEADS // 2):
            sl = slice(p * LANES, (p + 1) * LANES)
            qp = q_ref[rows, sl]
            zero = jnp.zeros_like(qp)
            q2 = jnp.concatenate([jnp.where(low, qp, zero), jnp.where(low, zero, qp)], axis=0)
            kp = jnp.concatenate([k_refs[j][:, sl] for j in window], axis=0)
            vp = jnp.concatenate([v_refs[j][:, sl] for j in window], axis=0)
            s = _dot_nt(q2, kp) + bias_ref[p]
            if n_missing:
                col = lax.broadcasted_iota(I32, (1, BAND_KEYS), 1)
                s = s + jnp.where(col < n_missing * ROW_TILE, NEG, 0.0).astype(F32)
            pe = jnp.exp2(s - jnp.max(s, axis=-1, keepdims=True)).astype(BF16)
            o2 = _dot(pe, jnp.concatenate([vp, ones], axis=1))
            o = o2[:, :LANES] * (1.0 / o2[:, LANES:])
            o_ref[rows, sl] = jnp.where(low, o[:ROW_TILE], o[ROW_TILE:]).astype(BF16)

    @pl.when(i > 0)
    def _():
        for tile in range(BAND_TILES_PER_STEP):
            attend(tile, 0)

    @pl.when(i == 0)
    def _():
        for tile in range(BAND_TILES_PER_STEP):
            attend(tile, max(BAND_TILES - 1 - tile, 0))


def _band_bias_vector(rel_table):
    h = rel_table.shape[0]
    tab = rel_table.astype(F32) * LOG2E
    shift = A_LEFT_CHUNKS * CHUNK + ROW_TILE - 1
    n_far = shift - A_MAX_REL + 1
    span = ROW_TILE + BAND_KEYS - 1
    assert span - 1 - shift <= A_MAX_REL and span <= BIAS_PERIOD
    u = jnp.concatenate([jnp.broadcast_to(tab[:, 2 * A_MAX_REL:], (h, n_far)),
                         tab[:, 2 * A_MAX_REL - 1:2 * A_MAX_REL - 1 - (span - n_far):-1]], axis=1)
    return jnp.pad(u, ((0, 0), (0, BIAS_PERIOD - span)))


def _band_attention(q, k, v, u, batch):
    t, w = q.shape
    n = BAND_TILES_PER_STEP
    nb = t // batch // (n * ROW_TILE)
    step = pl.BlockSpec((n * ROW_TILE, w), lambda b, i: (b * nb + i, 0))
    tile = lambda j: pl.BlockSpec(
        (ROW_TILE, w), lambda b, i: (n * b * nb + jnp.maximum(n * i + j - (BAND_TILES - 1), 0), 0))
    window = [tile(j) for j in range(n + BAND_TILES - 1)]
    return pl.pallas_call(
        _band_kernel,
        grid=(batch, nb),
        in_specs=[step] + window + window + [pl.BlockSpec(u.shape, lambda b, i: (0, 0))],
        out_specs=step,
        out_shape=jax.ShapeDtypeStruct((t, w), BF16),
        scratch_shapes=[pltpu.VMEM((A_HEADS // 2, 2 * ROW_TILE, BAND_KEYS), F32)],
        compiler_params=_params("arbitrary", "arbitrary"),
        name="band_attn",
    )(q, *([k] * len(window)), *([v] * len(window)), u)


def _gla_constants():
    c = CHUNK
    t = np.arange(c)[:, None]
    r = np.arange(c)[None, :]
    mats = [(r <= t), (r > t)]
    lvl = np.full((c, c), -1, np.int32)
    lvl[np.arange(c), np.arange(c)] = N_LEVELS
    for l in range(N_LEVELS):
        m = (c // 2) >> l
        mid = (t // (2 * m)) * (2 * m) + m
        upper = t >= mid
        mats.append(np.where(upper, (r >= mid) & (r <= t), (r > t) & (r < mid)))
        s = r
        same = (s // (2 * m)) == (t // (2 * m))
        lvl[np.asarray(same & upper & (s < mid))] = l
    eye = np.eye(CHUNKS_PER_TILE)
    mexp = np.concatenate([np.kron(eye, m) for m in mats], axis=0).astype(np.float32)
    lvl = np.tile(lvl, (1, B_HEADS))
    return jnp.asarray(mexp, BF16), jnp.asarray(lvl, I32)


def _gla_kernel(q_ref, k_ref, v_ref, g_ref, r_ref, gn_ref, mexp_ref, lvl_ref, o_ref, s_ref):
    @pl.when(pl.program_id(1) == 0)
    def _():
        s_ref[...] = jnp.zeros_like(s_ref)

    kw = B_KEY_WIDTH
    ri = lax.broadcasted_iota(I32, (kw, kw), 0) >> LOG_CHUNK
    ci = lax.broadcasted_iota(I32, (kw, kw), 1) >> LOG_CHUNK
    bd = ri == ci
    head_ind = jnp.where(bd, 1.0, 0.0).astype(BF16)
    ri2 = lax.broadcasted_iota(I32, (kw, 2 * kw), 0) >> LOG_CHUNK
    ci2 = (lax.broadcasted_iota(I32, (kw, 2 * kw), 1) & (kw - 1)) >> LOG_CHUNK
    bd2 = ri2 == ci2
    lvl = lvl_ref[...]
    row8 = lax.broadcasted_iota(I32, (16, kw), 0)
    ones = jnp.ones((16, LANES), BF16)
    zero_b = jnp.zeros((kw, kw), BF16)
    chunks = [slice(c * CHUNK, (c + 1) * CHUNK) for c in range(CHUNKS_PER_TILE)]

    def head_blocks(x):
        return jnp.where(bd, jnp.concatenate([x] * B_HEADS, axis=0), zero_b)

    def prepare(tile):
        trows = pl.ds(tile * ROW_TILE, ROW_TILE)
        q = q_ref[trows, :].astype(F32)
        k = k_ref[trows, :].astype(F32)
        g = g_ref[trows, :]
        gb = g.astype(BF16)
        half = EXP_ROWS * CHUNKS_PER_TILE // 2
        w = jnp.exp(jnp.concatenate([_dot(mexp_ref[:half, :], gb), _dot(mexp_ref[half:, :], gb)], axis=0))
        qt = (q * w[0:ROW_TILE]).astype(BF16)
        kb = (k * w[ROW_TILE:2 * ROW_TILE]).astype(BF16)
        qk = (q * k).astype(BF16)

        attn = [jnp.zeros((CHUNK, kw), F32) for _ in chunks]
        for l in range(N_LEVELS):
            wl = w[(2 + l) * ROW_TILE:(3 + l) * ROW_TILE]
            qh = (q * wl).astype(BF16)
            kh = (k * wl).astype(BF16)
            for c, rows in enumerate(chunks):
                attn[c] = jnp.where(lvl == l, _dot_nt(qh[rows], head_blocks(kh[rows])), attn[c])

        out = []
        for c, rows in enumerate(chunks):
            a = jnp.where(lvl == N_LEVELS, _dot(qk[rows], head_ind), attn[c])
            v = v_ref[pl.ds(tile * ROW_TILE + c * CHUNK, CHUNK), :]
            vstack = jnp.concatenate([v[:, j * LANES:(j + 1) * LANES] for j in range(B_HEADS)], axis=0)
            kv = _dot_tn(head_blocks(kb[rows]), vstack)
            d = jnp.exp(jnp.sum(g[rows], axis=0, keepdims=True))
            d1 = d.astype(BF16).astype(F32)
            dp = jnp.where(row8 == 0, d1, jnp.where(row8 == 1, d - d1, 0.0)).astype(BF16)
            dcol = _dot_tn(dp, ones)
            out.append((a.astype(BF16), qt[rows], vstack, kv, dcol))
        return out

    prepared = [p for tile in range(GLA_TILES_PER_STEP) for p in prepare(tile)]

    s = s_ref[...]
    for c, (a, qtc, vstack, kv, dcol) in enumerate(prepared):
        rows = pl.ds(c * CHUNK, CHUNK)
        lhs = jnp.concatenate([a, qtc], axis=1)
        lhs = jnp.where(bd2, jnp.concatenate([lhs] * B_HEADS, axis=0), jnp.zeros((kw, 2 * kw), BF16))
        rhs = jnp.concatenate([vstack, s.astype(BF16)], axis=0)
        o = _dot(lhs, rhs)
        s = dcol * s + kv
        for j in range(B_HEADS):
            oj = o[j * CHUNK:(j + 1) * CHUNK]
            sl = slice(j * LANES, (j + 1) * LANES)
            y = oj * lax.rsqrt(jnp.mean(oj * oj, axis=-1, keepdims=True) + EPS) * gn_ref[...]
            o_ref[rows, sl] = (y * r_ref[rows, sl].astype(F32)).astype(BF16)
    s_ref[...] = s


def _gla(q, k, v, g, r, gn, batch):
    t = q.shape[0]
    nb = t // batch // (GLA_TILES_PER_STEP * ROW_TILE)
    mexp, lvl = _gla_constants()
    cur = lambda b, i: (b * nb + i, 0)
    blk = lambda w: pl.BlockSpec((GLA_TILES_PER_STEP * ROW_TILE, w), cur)
    full = lambda a: pl.BlockSpec(a.shape, lambda b, i: (0,) * a.ndim)
    return pl.pallas_call(
        _gla_kernel,
        grid=(batch, nb),
        in_specs=[blk(B_KEY_WIDTH), blk(B_KEY_WIDTH), blk(B_VAL_WIDTH), blk(B_KEY_WIDTH), blk(B_VAL_WIDTH),
                  full(gn), full(mexp), full(lvl)],
        out_specs=blk(B_VAL_WIDTH),
        out_shape=jax.ShapeDtypeStruct((t, B_VAL_WIDTH), BF16),
        scratch_shapes=[pltpu.VMEM((B_KEY_WIDTH, B_VAL_DIM), F32)],
        compiler_params=_params("arbitrary", "arbitrary"),
        name="gla",
    )(q, k, v, g, r, gn, mexp, lvl)


def _token_kernel(x_ref, oa_ref, ob_ref, gate_ref, wb0_ref, wb1_ref, wmix_ref, gx_ref, wq_ref,
                  km_ref, vm_ref, wo_ref, gf_ref, wr_ref, br_ref, ltri_ref, utri_ref,
                  x2_ref, hs_ref, route_ref, cnt_ref):
    ma = _dot(oa_ref[...], wb0_ref[...])
    mb = _dot(ob_ref[...], wb1_ref[...])
    merged = (gate_ref[:, :D_MODEL].astype(F32) * ma + gate_ref[:, D_MODEL:].astype(F32) * mb).astype(BF16)
    x1 = x_ref[...] + _dot(merged, wmix_ref[...])

    h2 = _rms(x1, gx_ref[...]).astype(BF16)
    qx = (_dot(h2, wq_ref[...]) * (X_HEAD_DIM ** -0.5)).astype(BF16)
    heads = []
    for h in range(X_HEADS):
        sl = slice(h * X_HEAD_DIM, (h + 1) * X_HEAD_DIM)
        s = _dot_nt(qx[:, sl], km_ref[0, :, sl])
        m = jnp.max(s, axis=-1, keepdims=True)
        pe = jnp.exp(s - m)
        l = jnp.sum(pe, axis=-1, keepdims=True)
        heads.append((_dot(pe.astype(BF16), vm_ref[0, :, sl]) * (1.0 / l)).astype(BF16))
    x2 = x1 + _dot(jnp.concatenate(heads, axis=1), wo_ref[...])
    x2_ref[...] = x2

    h3 = _rms(x2, gf_ref[...])

    h3_hi = h3.astype(BF16)
    h3_lo = (h3 - h3_hi.astype(F32)).astype(BF16)
    hw = _dot(h3_hi, wr_ref[...])
    logits = hw[:, :LANES] + hw[:, LANES:] + _dot(h3_lo, wr_ref[:, :LANES]) + br_ref[...]
    oh0, oh1, g0, g1 = _route(logits)
    for h in range(TOKEN_TILES_PER_STEP):
        rows = slice(h * ROW_TILE, (h + 1) * ROW_TILE)
        chunks = pl.ds(h * LOCAL_CHUNKS, LOCAL_CHUNKS)
        _sort_tile(oh0[rows], oh1[rows], g0[rows], g1[rows], h3_hi[rows], ltri_ref, utri_ref,
                   hs_ref.at[chunks], route_ref.at[pl.ds(h * ROW_TILE, ROW_TILE)], cnt_ref.at[h])


def _route(logits):
    lane = lax.broadcasted_iota(I32, logits.shape, 1).astype(F32)
    big = jnp.float32(LANES)
    gl = jnp.where(lane < N_GROUPS, logits, NEG)
    gmax = jnp.max(gl, axis=-1, keepdims=True)
    gidx = jnp.min(jnp.where(gl == gmax, lane, big), axis=-1, keepdims=True)
    g_w = 1.0 / jnp.sum(jnp.exp(gl - gmax), axis=-1, keepdims=True)
    lo = N_GROUPS + EXPERTS_PER_GROUP * gidx
    el = jnp.where((lane >= lo) & (lane < lo + EXPERTS_PER_GROUP), logits, NEG)
    v1 = jnp.max(el, axis=-1, keepdims=True)
    i1 = jnp.min(jnp.where(el == v1, lane, big), axis=-1, keepdims=True)
    el2 = jnp.where(lane == i1, NEG, el)
    v2 = jnp.max(el2, axis=-1, keepdims=True)
    i2 = jnp.min(jnp.where(el2 == v2, lane, big), axis=-1, keepdims=True)
    e21 = jnp.exp(v2 - v1)
    w1 = g_w / (1.0 + e21)

    def gate_cols(w):
        hi = w.astype(BF16).astype(F32)
        return jnp.where(lane == 0, hi, jnp.where(lane == 1, w - hi, 0.0)).astype(BF16)

    oh0 = jnp.where(lane == i1 - N_GROUPS, 1.0, 0.0)
    oh1 = jnp.where(lane == i2 - N_GROUPS, 1.0, 0.0)
    return oh0, oh1, gate_cols(w1), gate_cols(w1 * e21)


def _sort_tile(oh0, oh1, g0, g1, h3_hi, ltri_ref, utri_ref, hs_ref, route_ref, cnt_ref):
    lane = lax.broadcasted_iota(I32, oh0.shape, 1)
    oh = oh0 + oh1
    nch = jnp.floor((jnp.sum(oh, axis=0, keepdims=True) + (CHUNK_ROWS - 1)) * (1.0 / CHUNK_ROWS))
    nch8 = jnp.broadcast_to(nch, (8, LANES))
    start = _dot(nch8.astype(BF16), utri_ref[...])[0:1] * CHUNK_ROWS
    rank = _dot(ltri_ref[...], oh.astype(BF16))
    row = start + rank
    pos0 = jnp.sum(row * oh0, axis=-1, keepdims=True)
    pos1 = jnp.sum(row * oh1, axis=-1, keepdims=True)
    route = jnp.where(lane == 0, pos0, jnp.where(lane == 1, pos1, 0.0))
    route_t = jnp.transpose(route)
    r = lax.broadcasted_iota(I32, (LOCAL_ROWS, ROW_TILE), 0).astype(F32)
    p0 = jnp.where(r == route_t[0:1, :], 1.0, 0.0).astype(BF16)
    p1 = jnp.where(r == route_t[1:2, :], 1.0, 0.0).astype(BF16)
    sorted_rows = jnp.concatenate([_dot(p0 + p1, h3_hi), _dot(p0, g0) + _dot(p1, g1)], axis=1)
    hs_ref[...] = sorted_rows.astype(BF16).reshape(hs_ref.shape)
    route_ref[...] = route
    cnt_ref[...] = nch8


def _token(x, oa, ob, gates, wb0, wb1, wmix, gx, wq, km, vm, wo, gf, wr, br, batch):
    t, d = x.shape
    n = TOKEN_TILES_PER_STEP
    nb = t // batch // (n * ROW_TILE)
    nt = t // ROW_TILE
    ltri = jnp.asarray(np.tril(np.ones((ROW_TILE, ROW_TILE), np.float32), -1), BF16)
    utri = jnp.asarray(np.triu(np.ones((LANES, LANES), np.float32), 1), BF16)
    cur = lambda b, i: (b * nb + i, 0)
    cur3 = lambda b, i: (b * nb + i, 0, 0)
    blk = lambda w: pl.BlockSpec((n * ROW_TILE, w), cur)
    full = lambda a: pl.BlockSpec(a.shape, lambda b, i: (0,) * a.ndim)
    mem = pl.BlockSpec((1,) + km.shape[1:], lambda b, i: (b, 0, 0))
    return pl.pallas_call(
        _token_kernel,
        grid=(batch, nb),
        in_specs=[blk(d), blk(A_WIDTH), blk(B_VAL_WIDTH), blk(2 * d), full(wb0), full(wb1), full(wmix),
                  full(gx), full(wq), mem, mem, full(wo), full(gf), full(wr), full(br), full(ltri), full(utri)],
        out_specs=[blk(d), pl.BlockSpec((n * LOCAL_CHUNKS, CHUNK_ROWS, SORT_WIDTH), cur3),
                   blk(LANES), pl.BlockSpec((n, 8, LANES), cur3)],
        out_shape=[jax.ShapeDtypeStruct((t, d), F32),
                   jax.ShapeDtypeStruct((nt * LOCAL_CHUNKS, CHUNK_ROWS, SORT_WIDTH), BF16),
                   jax.ShapeDtypeStruct((t, LANES), F32),
                   jax.ShapeDtypeStruct((nt, 8, LANES), F32)],
        compiler_params=_params("arbitrary", "arbitrary"),
        name="token",
    )(x, oa, ob, gates, wb0, wb1, wmix, gx, wq, km, vm, wo, gf, wr, br, ltri, utri)


def _expert_kernel(te_ref, nu_ref, nv_ref, ch_ref, hs_hbm, wg_ref, wu_ref, wd_ref, ys_hbm,
                   xbuf, ybuf, wgu_bf, wd_bf, gsem, ssem):
    i = pl.program_id(0)
    n_used = nu_ref[0]
    slot = lax.rem(i, 2)

    def for_chunks(tile, fn):
        nv = nv_ref[tile]

        @pl.when(nv == TILE_CHUNKS)
        def _():
            for c in range(TILE_CHUNKS):
                fn(c)

        @pl.when(nv != TILE_CHUNKS)
        def _():
            def body(c, carry):
                fn(c)
                return carry

            lax.fori_loop(0, nv, body, 0)

    def gather(tile, s, start):
        for c in range(TILE_CHUNKS):
            cp = pltpu.make_async_copy(hs_hbm.at[ch_ref[tile * TILE_CHUNKS + c]], xbuf.at[s, c], gsem.at[s])
            cp.start() if start else cp.wait()

    def scatter(tile, s, start):
        def one(c):
            cp = pltpu.make_async_copy(ybuf.at[s, c], ys_hbm.at[ch_ref[tile * TILE_CHUNKS + c]], ssem.at[s])
            cp.start(priority=1) if start else cp.wait()

        for_chunks(tile, one)

    next_tile = jnp.minimum(i + 1, pl.num_programs(0) - 1)

    @pl.when(i == 0)
    def _():
        gather(0, 0, True)

    @pl.when(i < n_used)
    def _():
        gather(next_tile, 1 - slot, True)
        gather(i, slot, False)

        @pl.when(i >= 2)
        def _():
            scatter(i - 2, slot, False)

        @pl.when((i == 0) | (te_ref[i] != te_ref[jnp.maximum(i - 1, 0)]))
        def _():
            wgu_bf[:, :EXPERT_FF] = wg_ref[0].astype(BF16)
            wgu_bf[:, EXPERT_FF:] = wu_ref[0].astype(BF16)
            wd_bf[...] = wd_ref[0].astype(BF16)

        xg = xbuf[slot].reshape(EXPERT_ROWS, SORT_WIDTH)
        hgu = _dot(xg[:, :D_MODEL], wgu_bf[...])
        hg, hu = hgu[:, :EXPERT_FF], hgu[:, EXPERT_FF:]
        hid = (hg * jax.nn.sigmoid(hg) * hu).astype(BF16)
        g = xg[:, D_MODEL:].astype(F32)
        y = ((g[:, 0:1] + g[:, 1:2]) * _dot(hid, wd_bf[...])).astype(BF16)
        y = jnp.concatenate([y, jnp.zeros((EXPERT_ROWS, LANES), BF16)], axis=1)
        ybuf[slot] = y.reshape(TILE_CHUNKS, CHUNK_ROWS, SORT_WIDTH)
        scatter(i, slot, True)

        @pl.when(i == n_used - 1)
        def _():
            gather(next_tile, 1 - slot, False)
            scatter(i, slot, False)

            @pl.when(i >= 1)
            def _():
                scatter(i - 1, 1 - slot, False)


def _experts(hs, tile_expert, n_used, n_valid, chunks, wg, wu, wd, layer):
    n_tiles = tile_expert.shape[0]
    last = lambda i, te, nu, nv, ch: jnp.minimum(i, nu[0] - 1)
    wmap = lambda i, te, nu, nv, ch: (layer * N_EXPERTS + te[last(i, te, nu, nv, ch)], 0, 0)
    anyspace = pl.BlockSpec(memory_space=pl.ANY)
    grid_spec = pltpu.PrefetchScalarGridSpec(
        num_scalar_prefetch=4,
        grid=(n_tiles,),
        in_specs=[anyspace,
                  pl.BlockSpec((1, D_MODEL, EXPERT_FF), wmap),
                  pl.BlockSpec((1, D_MODEL, EXPERT_FF), wmap),
                  pl.BlockSpec((1, EXPERT_FF, D_MODEL), wmap)],
        out_specs=anyspace,
        scratch_shapes=[pltpu.VMEM((2, TILE_CHUNKS, CHUNK_ROWS, SORT_WIDTH), BF16),
                        pltpu.VMEM((2, TILE_CHUNKS, CHUNK_ROWS, SORT_WIDTH), BF16),
                        pltpu.VMEM((D_MODEL, 2 * EXPERT_FF), BF16), pltpu.VMEM((EXPERT_FF, D_MODEL), BF16),
                        pltpu.SemaphoreType.DMA((2,)), pltpu.SemaphoreType.DMA((2,))],
    )
    return pl.pallas_call(
        _expert_kernel,
        grid_spec=grid_spec,
        out_shape=jax.ShapeDtypeStruct(hs.shape, BF16),
        input_output_aliases={4: 0},
        compiler_params=_params("arbitrary"),
        name="experts",
    )(tile_expert, n_used, n_valid, chunks, hs, wg, wu, wd)


def _combine_kernel(x_ref, route_ref, ys_ref, gfin_ref, o_ref):
    r = lax.broadcasted_iota(I32, (ROW_TILE, LOCAL_ROWS), 1).astype(F32)
    for t in range(COMBINE_TILES_PER_STEP):
        rows = slice(t * ROW_TILE, (t + 1) * ROW_TILE)
        pos = route_ref[rows, :]
        sel = jnp.where(r == pos[:, 0:1], 1.0, jnp.where(r == pos[:, 1:2], 1.0, 0.0)).astype(BF16)
        ys = ys_ref[t * LOCAL_CHUNKS:(t + 1) * LOCAL_CHUNKS].reshape(LOCAL_ROWS, D_MODEL)
        o_ref[rows, :] = _rms(x_ref[rows, :] + _dot(sel, ys), gfin_ref[...])


def _combine(x2, route, ys, gfin):
    t, d = x2.shape
    n = COMBINE_TILES_PER_STEP
    return pl.pallas_call(
        _combine_kernel,
        grid=(t // (n * ROW_TILE),),
        in_specs=[pl.BlockSpec((n * ROW_TILE, d), lambda i: (i, 0)),
                  pl.BlockSpec((n * ROW_TILE, LANES), lambda i: (i, 0)),
                  pl.BlockSpec((n * LOCAL_CHUNKS, CHUNK_ROWS, d), lambda i: (i, 0, 0)),
                  pl.BlockSpec((1, d), lambda i: (0, 0))],
        out_specs=pl.BlockSpec((n * ROW_TILE, d), lambda i: (i, 0)),
        out_shape=jax.ShapeDtypeStruct((t, d), F32),
        compiler_params=_params("arbitrary"),
        name="combine",
    )(x2, route, ys, gfin)


def _chunk_plan(nch, n_tiles):
    nt = nch.shape[0]
    local_start = jnp.cumsum(nch, axis=1) - nch
    cum = jnp.cumsum(nch, axis=0)
    total = cum[-1]
    tiles = (total + TILE_CHUNKS - 1) // TILE_CHUNKS
    tile_end = jnp.cumsum(tiles)
    n_used = tile_end[-1:]
    tile_ids = jnp.arange(n_tiles, dtype=I32)
    tile_expert = jnp.minimum(jnp.sum((tile_end[None, :] <= tile_ids[:, None]).astype(I32), axis=1),
                              N_EXPERTS - 1)
    sel = (tile_expert[:, None] == jnp.arange(N_EXPERTS, dtype=I32)[None, :]).astype(I32)
    pick = lambda table: jnp.sum(sel[:, :, None] * table.T[None, :, :], axis=1)
    first_tile = jnp.sum(sel * (tile_end - tiles)[None, :], axis=1)
    slot = (tile_ids - first_tile)[:, None] * TILE_CHUNKS + jnp.arange(TILE_CHUNKS, dtype=I32)[None, :]
    valid = (slot < jnp.sum(sel * total[None, :], axis=1)[:, None]) & (tile_ids < n_used)[:, None]
    src_tile = jnp.sum((pick(cum)[:, None, :] <= slot[:, :, None]).astype(I32), axis=2)
    src_tile = jnp.minimum(src_tile, nt - 1)
    at = (src_tile[:, :, None] == jnp.arange(nt, dtype=I32)[None, None, :]).astype(I32)
    before = jnp.sum(at * pick(cum - nch)[:, None, :], axis=2)
    start = jnp.sum(at * pick(local_start)[:, None, :], axis=2)
    chunk = jnp.where(valid, src_tile * LOCAL_CHUNKS + start + slot - before, LOCAL_CHUNKS - 1)
    return tile_expert, n_used, jnp.sum(valid.astype(I32), axis=1), chunk.reshape(-1)


def kernel(x, mem, norm_mix_g, w_in, rel_bias, gla_w_alpha, gla_b_alpha, gla_norm_g, w_branch, w_mix_out, norm_x_g, mem_norm_g, w_xq, w_xkv, w_xo, norm_ffn_g, w_group_router, b_group_router, w_expert_router, b_expert_router, w_exp_gate, w_exp_up, w_exp_down, final_norm_g):
    batch, seq, d = x.shape
    depth = w_in.shape[0]
    t = batch * seq
    step_tiles = max(TOKEN_TILES_PER_STEP, GLA_TILES_PER_STEP, BAND_TILES_PER_STEP, COMBINE_TILES_PER_STEP)
    assert d == D_MODEL and seq % (step_tiles * ROW_TILE) == 0 and seq % INPROJ_ROWS == 0
    nt = t // ROW_TILE
    n_tiles = nt * LOCAL_CHUNKS // TILE_CHUNKS + N_EXPERTS

    xf = x.reshape(t, d)
    km_all, vm_all = _memkv(mem, mem_norm_g, w_xkv.astype(BF16))
    row = lambda a: a.reshape(1, -1).astype(F32)

    moe = None
    for l in range(depth):
        res = _inproj(xf, row(norm_mix_g[l]), w_in, gla_w_alpha, row(gla_b_alpha[l]), l, moe)
        if moe is not None:
            xf, res = res[0], res[1:]
        aq, ak, av, bq, bk, bv, lga, br, gates = res

        oa = _band_attention(aq, ak, av, _band_bias_vector(rel_bias[l]), batch)
        ob = _gla(bq, bk, bv, lga, br, row(gla_norm_g[l]), batch)

        wr = jnp.pad(jnp.concatenate([w_group_router[l], w_expert_router[l]], axis=1).astype(F32),
                     ((0, 0), (0, LANES - N_GROUPS - N_EXPERTS)))
        wr_hi = wr.astype(BF16)
        wr = jnp.concatenate([wr_hi, (wr - wr_hi.astype(F32)).astype(BF16)], axis=1)
        brt = jnp.pad(jnp.concatenate([b_group_router[l], b_expert_router[l]]).astype(F32),
                      (0, LANES - N_GROUPS - N_EXPERTS)).reshape(1, LANES)
        x2, hs, route, cnt = _token(
            xf, oa, ob, gates, w_branch[l, 0].astype(BF16), w_branch[l, 1].astype(BF16),
            w_mix_out[l].astype(BF16), row(norm_x_g[l]), w_xq[l].astype(BF16), km_all[l], vm_all[l],
            w_xo[l].astype(BF16), row(norm_ffn_g[l]), wr, brt, batch)

        plan = _chunk_plan(cnt[:, 0, :N_EXPERTS].astype(I32), n_tiles)
        e3 = lambda w: w.reshape((depth * N_EXPERTS,) + w.shape[3:])
        ys = _experts(hs, *plan, e3(w_exp_gate), e3(w_exp_up), e3(w_exp_down), l)
        xf, moe = x2, (route, ys)

    return _combine(x2, route, ys, row(final_norm_g)).reshape(batch, seq, d)
```

```python
import functools

import numpy as np
import jax
import jax.numpy as jnp
from jax import lax
from jax.experimental import pallas as pl
from jax.experimental.pallas import tpu as pltpu

F32 = jnp.float32
BF16 = jnp.bfloat16
I32 = jnp.int32

D_MODEL = 1024
CHUNK = 64
EPS = 1e-6
A_HEADS = 8
A_HEAD_DIM = 64
A_WIDTH = 512
A_LEFT_CHUNKS = 8
A_MAX_REL = 256
B_HEADS = 4
B_KEY_DIM = 64
B_VAL_DIM = 128
B_KEY_WIDTH = 256
B_VAL_WIDTH = 512
B_GATE_RANK = 16
B_GATE_TAU = 16.0
X_HEADS = 4
X_HEAD_DIM = 256
N_GROUPS = 4
EXPERTS_PER_GROUP = 8
N_EXPERTS = N_GROUPS * EXPERTS_PER_GROUP
EXPERT_FF = 256

LANES = 128
ROW_TILE = 256
CHUNKS_PER_TILE = ROW_TILE // CHUNK
BAND_TILES = A_LEFT_CHUNKS // CHUNKS_PER_TILE + 1
BAND_KEYS = BAND_TILES * ROW_TILE
BIAS_PERIOD = 1024
LOG_CHUNK = 6
N_LEVELS = LOG_CHUNK
EXP_ROWS = (2 + N_LEVELS) * CHUNK
CHUNK_ROWS = 16
TOKEN_TILES_PER_STEP = 2
INPROJ_ROWS = 512
COMBINE_TILES_PER_STEP = 2
GLA_TILES_PER_STEP = 4
BAND_TILES_PER_STEP = 4
EXPERT_ROWS = 512
TILE_CHUNKS = EXPERT_ROWS // CHUNK_ROWS
LOCAL_CHUNKS = 2 * ROW_TILE // CHUNK_ROWS + N_EXPERTS
LOCAL_ROWS = LOCAL_CHUNKS * CHUNK_ROWS
assert (2 * ROW_TILE + N_EXPERTS * (CHUNK_ROWS - 1)) // CHUNK_ROWS < LOCAL_CHUNKS
SORT_WIDTH = D_MODEL + LANES
NEG = -1e30
LOG2E = 1.4426950408889634
VMEM_LIMIT = 56 * 1024 * 1024


def _params(*sem):
    return pltpu.CompilerParams(dimension_semantics=sem, vmem_limit_bytes=VMEM_LIMIT)


def _rms(x, g):
    return x * lax.rsqrt(jnp.mean(x * x, axis=-1, keepdims=True) + EPS) * g


def _dot(a, b):
    return jnp.dot(a, b, preferred_element_type=F32)


def _dot_nt(a, b):
    return lax.dot_general(a, b, (((1,), (1,)), ((), ())), preferred_element_type=F32)


def _dot_tn(a, b):
    return lax.dot_general(a, b, (((0,), (0,)), ((), ())), preferred_element_type=F32)


def _memkv_kernel(mem_ref, g_ref, w_ref, k_ref, v_ref):
    mn = _rms(mem_ref[0], g_ref[...]).astype(BF16)
    kv = _dot(mn, w_ref[0])
    k_ref[0, 0] = kv[:, :D_MODEL].astype(BF16)
    v_ref[0, 0] = kv[:, D_MODEL:].astype(BF16)


def _memkv(mem, g, w_xkv):
    depth = w_xkv.shape[0]
    b, m, d = mem.shape
    out = jax.ShapeDtypeStruct((depth, b, m, d), BF16)
    return pl.pallas_call(
        _memkv_kernel,
        grid=(depth, b),
        in_specs=[pl.BlockSpec((1, m, d), lambda l, i: (i, 0, 0)),
                  pl.BlockSpec((1, d), lambda l, i: (0, 0)),
                  pl.BlockSpec((1, d, 2 * d), lambda l, i: (l, 0, 0))],
        out_specs=[pl.BlockSpec((1, 1, m, d), lambda l, i: (l, i, 0, 0)),
                   pl.BlockSpec((1, 1, m, d), lambda l, i: (l, i, 0, 0))],
        out_shape=[out, out],
        compiler_params=_params("arbitrary", "arbitrary"),
        name="memkv",
    )(mem, g.reshape(1, d), w_xkv)


_R_AQ, _R_AK, _R_AV = 0, 512, 1024
_R_BQ, _R_BK, _R_BV = 1536, 1792, 2048
_R_ALPHA, _R_BR, _R_GATE, _R_END = 2560, 2576, 3088, 5136
_W_PIECE = 512


def _inproj_kernel(layer, fused, *refs):
    if fused:
        x_ref, route_ref, ys_ref, g_ref, wt_hbm, wal2_ref, bal_ref = refs[:7]
        refs = refs[7:]
        x3_ref, refs = refs[0], refs[1:]
    else:
        x_ref, g_ref, wt_hbm, wal2_ref, bal_ref = refs[:5]
        refs = refs[5:]
    (aq_ref, ak_ref, av_ref, bq_ref, bk_ref, bv_ref, lga_ref, br_ref, gate_ref,
     wt_bf, wal2_bf, stage, sem) = refs

    @pl.when(pl.program_id(0) == 0)
    def _():
        pieces = [(c, min(_W_PIECE, _R_END - c)) for c in range(0, _R_END, _W_PIECE)]

        def piece_copy(p):
            c, n = pieces[p]
            return pltpu.make_async_copy(wt_hbm.at[layer, pl.ds(c, n), :], stage.at[p % 2, pl.ds(0, n), :],
                                         sem.at[p % 2])

        piece_copy(0).start()
        for p, (c, n) in enumerate(pieces):
            if p + 1 < len(pieces):
                piece_copy(p + 1).start()
            piece_copy(p).wait()
            wt_bf[c:c + n, :] = stage[p % 2, 0:n, :].astype(BF16)
        wal2_bf[...] = jnp.concatenate(
            [wal2_ref[0].astype(BF16), jnp.zeros((LANES - B_GATE_RANK, B_KEY_WIDTH), BF16)], axis=0)

    if fused:
        r = lax.broadcasted_iota(I32, (ROW_TILE, LOCAL_ROWS), 1).astype(F32)
        tiles = []
        for t in range(INPROJ_ROWS // ROW_TILE):
            rows = slice(t * ROW_TILE, (t + 1) * ROW_TILE)
            pos = route_ref[rows, :]
            sel = jnp.where(r == pos[:, 0:1], 1.0, jnp.where(r == pos[:, 1:2], 1.0, 0.0)).astype(BF16)
            ys = ys_ref[t * LOCAL_CHUNKS:(t + 1) * LOCAL_CHUNKS].reshape(LOCAL_ROWS, D_MODEL)
            tiles.append(x_ref[rows, :] + _dot(sel, ys))
        x = jnp.concatenate(tiles, axis=0)
        x3_ref[...] = x
    else:
        x = x_ref[...]
    h = _rms(x, g_ref[...]).astype(BF16)

    def mm(lo, hi):
        return _dot_nt(h, wt_bf[lo:hi, :])

    aq_ref[...] = (mm(_R_AQ, _R_AK) * (A_HEAD_DIM ** -0.5 * LOG2E)).astype(BF16)
    ak_ref[...] = mm(_R_AK, _R_AV).astype(BF16)
    av_ref[...] = mm(_R_AV, _R_BQ).astype(BF16)
    bq_ref[...] = (mm(_R_BQ, _R_BK) * (B_KEY_DIM ** -0.5)).astype(BF16)
    bk_ref[...] = mm(_R_BK, _R_BV).astype(BF16)
    bv_ref[...] = mm(_R_BV, _R_ALPHA).astype(BF16)
    r = mm(_R_BR, _R_GATE)
    br_ref[...] = (r * jax.nn.sigmoid(r)).astype(BF16)
    for c in range(_R_GATE, _R_END, 512):
        gate_ref[:, c - _R_GATE:c - _R_GATE + 512] = jax.nn.sigmoid(mm(c, c + 512)).astype(BF16)
    z = _dot(mm(_R_ALPHA, _R_ALPHA + LANES).astype(BF16), wal2_bf[...]) + bal_ref[...]
    lga_ref[...] = (jnp.minimum(z, 0.0) - jnp.log(1.0 + jnp.exp(-jnp.abs(z)))) * (1.0 / B_GATE_TAU)


def _inproj(x, g, w_in, w_al2, b_al, layer, moe=None):
    t, d = x.shape
    assert w_in.shape[2] == _R_END
    row = lambda w: pl.BlockSpec((INPROJ_ROWS, w), lambda i: (i, 0))
    full = lambda a: pl.BlockSpec(a.shape, lambda i: (0,) * a.ndim)
    sds = lambda w, dt: jax.ShapeDtypeStruct((t, w), dt)
    widths = [(512, BF16), (512, BF16), (512, BF16), (256, BF16), (256, BF16), (512, BF16),
              (256, F32), (512, BF16), (2048, BF16)]
    fused = moe is not None
    moe_specs, moe_out_specs, moe_out_shape = [], [], []
    if fused:
        n_chunks = INPROJ_ROWS // ROW_TILE * LOCAL_CHUNKS
        moe_specs = [row(LANES), pl.BlockSpec((n_chunks, CHUNK_ROWS, d), lambda i: (i, 0, 0))]
        moe_out_specs, moe_out_shape = [row(d)], [sds(d, F32)]
    return pl.pallas_call(
        functools.partial(_inproj_kernel, layer, fused),
        grid=(t // INPROJ_ROWS,),
        in_specs=[row(d)] + moe_specs + [full(g), pl.BlockSpec(memory_space=pl.ANY),
                                         pl.BlockSpec((1,) + w_al2.shape[1:], lambda i: (layer, 0, 0)),
                                         full(b_al)],
        out_specs=moe_out_specs + [row(w) for w, _ in widths],
        out_shape=moe_out_shape + [sds(w, dt) for w, dt in widths],
        scratch_shapes=[pltpu.VMEM((_R_END, d), BF16), pltpu.VMEM((LANES, B_KEY_WIDTH), BF16),
                        pltpu.VMEM((2, _W_PIECE, d), F32), pltpu.SemaphoreType.DMA((2,))],
        compiler_params=_params("arbitrary"),
        name="inproj",
    )(x, *(moe or ()), g, jnp.swapaxes(w_in, 1, 2), w_al2, b_al)


def _band_kernel(q_ref, *refs):
    n_win = BAND_TILES_PER_STEP + BAND_TILES - 1
    k_refs, v_refs = refs[:n_win], refs[n_win:2 * n_win]
    u_ref, o_ref, bias_ref = refs[2 * n_win:]
    i = pl.program_id(1)
    lane = lax.broadcasted_iota(I32, (1, LANES), 1)
    low = lane < A_HEAD_DIM
    ones = jnp.ones((BAND_KEYS, LANES), BF16)

    @pl.when((pl.program_id(0) == 0) & (i == 0))
    def _():
        cq = lax.broadcasted_iota(I32, (ROW_TILE, BAND_KEYS), 0) >> LOG_CHUNK
        ck = lax.broadcasted_iota(I32, (ROW_TILE, BAND_KEYS), 1) >> LOG_CHUNK
        valid = (ck >= cq) & (ck <= cq + A_LEFT_CHUNKS)
        for h in range(A_HEADS):
            rows = jnp.broadcast_to(u_ref[h:h + 1, :], (ROW_TILE, BIAS_PERIOD))
            rows = pltpu.roll(rows, BIAS_PERIOD - (ROW_TILE - 1), 1, stride=1, stride_axis=0)
            bias_ref[h // 2, (h % 2) * ROW_TILE:(h % 2 + 1) * ROW_TILE, :] = jnp.where(
                valid, rows[:, :BAND_KEYS], NEG)

    def attend(tile, n_missing):
        rows = pl.ds(tile * ROW_TILE, ROW_TILE)
        window = range(tile, tile + BAND_TILES)
        for p in range(A_HEADS // 2):
            sl = slice(p * LANES, (p + 1) * LANES)
            qp = q_ref[rows, sl]
            zero = jnp.zeros_like(qp)
            q2 = jnp.concatenate([jnp.where(low, qp, zero), jnp.where(low, zero, qp)], axis=0)
            kp = jnp.concatenate([k_refs[j][:, sl] for j in window], axis=0)
            vp = jnp.concatenate([v_refs[j][:, sl] for j in window], axis=0)
            s = _dot_nt(q2, kp) + bias_ref[p]
            if n_missing:
                col = lax.broadcasted_iota(I32, (1, BAND_KEYS), 1)
                s = s + jnp.where(col < n_missing * ROW_TILE, NEG, 0.0).astype(F32)
            pe = jnp.exp2(s - jnp.max(s, axis=-1, keepdims=True)).astype(BF16)
            o2 = _dot(pe, jnp.concatenate([vp, ones], axis=1))
            o = o2[:, :LANES] * (1.0 / o2[:, LANES:])
            o_ref[rows, sl] = jnp.where(low, o[:ROW_TILE], o[ROW_TILE:]).astype(BF16)

    @pl.when(i > 0)
    def _():
        for tile in range(BAND_TILES_PER_STEP):
            attend(tile, 0)

    @pl.when(i == 0)
    def _():
        for tile in range(BAND_TILES_PER_STEP):
            attend(tile, max(BAND_TILES - 1 - tile, 0))


def _band_bias_vector(rel_table):
    h = rel_table.shape[0]
    tab = rel_table.astype(F32) * LOG2E
    shift = A_LEFT_CHUNKS * CHUNK + ROW_TILE - 1
    n_far = shift - A_MAX_REL + 1
    span = ROW_TILE + BAND_KEYS - 1
    assert span - 1 - shift <= A_MAX_REL and span <= BIAS_PERIOD
    u = jnp.concatenate([jnp.broadcast_to(tab[:, 2 * A_MAX_REL:], (h, n_far)),
                         tab[:, 2 * A_MAX_REL - 1:2 * A_MAX_REL - 1 - (span - n_far):-1]], axis=1)
    return jnp.pad(u, ((0, 0), (0, BIAS_PERIOD - span)))


def _band_attention(q, k, v, u, batch):
    t, w = q.shape
    n = BAND_TILES_PER_STEP
    nb = t // batch // (n * ROW_TILE)
    step = pl.BlockSpec((n * ROW_TILE, w), lambda b, i: (b * nb + i, 0))
    tile = lambda j: pl.BlockSpec(
        (ROW_TILE, w), lambda b, i: (n * b * nb + jnp.maximum(n * i + j - (BAND_TILES - 1), 0), 0))
    window = [tile(j) for j in range(n + BAND_TILES - 1)]
    return pl.pallas_call(
        _band_kernel,
        grid=(batch, nb),
        in_specs=[step] + window + window + [pl.BlockSpec(u.shape, lambda b, i: (0, 0))],
        out_specs=step,
        out_shape=jax.ShapeDtypeStruct((t, w), BF16),
        scratch_shapes=[pltpu.VMEM((A_HEADS // 2, 2 * ROW_TILE, BAND_KEYS), F32)],
        compiler_params=_params("arbitrary", "arbitrary"),
        name="band_attn",
    )(q, *([k] * len(window)), *([v] * len(window)), u)


def _gla_constants():
    c = CHUNK
    t = np.arange(c)[:, None]
    r = np.arange(c)[None, :]
    mats = [(r <= t), (r > t)]
    lvl = np.full((c, c), -1, np.int32)
    lvl[np.arange(c), np.arange(c)] = N_LEVELS
    for l in range(N_LEVELS):
        m = (c // 2) >> l
        mid = (t // (2 * m)) * (2 * m) + m
        upper = t >= mid
        mats.append(np.where(upper, (r >= mid) & (r <= t), (r > t) & (r < mid)))
        s = r
        same = (s // (2 * m)) == (t // (2 * m))
        lvl[np.asarray(same & upper & (s < mid))] = l
    eye = np.eye(CHUNKS_PER_TILE)
    mexp = np.concatenate([np.kron(eye, m) for m in mats], axis=0).astype(np.float32)
    lvl = np.tile(lvl, (1, B_HEADS))
    return jnp.asarray(mexp, BF16), jnp.asarray(lvl, I32)


def _gla_kernel(q_ref, k_ref, v_ref, g_ref, r_ref, gn_ref, mexp_ref, lvl_ref, o_ref, s_ref):
    @pl.when(pl.program_id(1) == 0)
    def _():
        s_ref[...] = jnp.zeros_like(s_ref)

    kw = B_KEY_WIDTH
    ri = lax.broadcasted_iota(I32, (kw, kw), 0) >> LOG_CHUNK
    ci = lax.broadcasted_iota(I32, (kw, kw), 1) >> LOG_CHUNK
    bd = ri == ci
    head_ind = jnp.where(bd, 1.0, 0.0).astype(BF16)
    ri2 = lax.broadcasted_iota(I32, (kw, 2 * kw), 0) >> LOG_CHUNK
    ci2 = (lax.broadcasted_iota(I32, (kw, 2 * kw), 1) & (kw - 1)) >> LOG_CHUNK
    bd2 = ri2 == ci2
    lvl = lvl_ref[...]
    row8 = lax.broadcasted_iota(I32, (16, kw), 0)
    ones = jnp.ones((16, LANES), BF16)
    zero_b = jnp.zeros((kw, kw), BF16)
    chunks = [slice(c * CHUNK, (c + 1) * CHUNK) for c in range(CHUNKS_PER_TILE)]

    def head_blocks(x):
        return jnp.where(bd, jnp.concatenate([x] * B_HEADS, axis=0), zero_b)

    def prepare(tile):
        trows = pl.ds(tile * ROW_TILE, ROW_TILE)
        q = q_ref[trows, :].astype(F32)
        k = k_ref[trows, :].astype(F32)
        g = g_ref[trows, :]
        gb = g.astype(BF16)
        half = EXP_ROWS * CHUNKS_PER_TILE // 2
        w = jnp.exp(jnp.concatenate([_dot(mexp_ref[:half, :], gb), _dot(mexp_ref[half:, :], gb)], axis=0))
        qt = (q * w[0:ROW_TILE]).astype(BF16)
        kb = (k * w[ROW_TILE:2 * ROW_TILE]).astype(BF16)
        qk = (q * k).astype(BF16)

        attn = [jnp.zeros((CHUNK, kw), F32) for _ in chunks]
        for l in range(N_LEVELS):
            wl = w[(2 + l) * ROW_TILE:(3 + l) * ROW_TILE]
            qh = (q * wl).astype(BF16)
            kh = (k * wl).astype(BF16)
            for c, rows in enumerate(chunks):
                attn[c] = jnp.where(lvl == l, _dot_nt(qh[rows], head_blocks(kh[rows])), attn[c])

        out = []
        for c, rows in enumerate(chunks):
            a = jnp.where(lvl == N_LEVELS, _dot(qk[rows], head_ind), attn[c])
            v = v_ref[pl.ds(tile * ROW_TILE + c * CHUNK, CHUNK), :]
            vstack = jnp.concatenate([v[:, j * LANES:(j + 1) * LANES] for j in range(B_HEADS)], axis=0)
            kv = _dot_tn(head_blocks(kb[rows]), vstack)
            d = jnp.exp(jnp.sum(g[rows], axis=0, keepdims=True))
            d1 = d.astype(BF16).astype(F32)
            dp = jnp.where(row8 == 0, d1, jnp.where(row8 == 1, d - d1, 0.0)).astype(BF16)
            dcol = _dot_tn(dp, ones)
            out.append((a.astype(BF16), qt[rows], vstack, kv, dcol))
        return out

    prepared = [p for tile in range(GLA_TILES_PER_STEP) for p in prepare(tile)]

    s = s_ref[...]
    for c, (a, qtc, vstack, kv, dcol) in enumerate(prepared):
        rows = pl.ds(c * CHUNK, CHUNK)
        lhs = jnp.concatenate([a, qtc], axis=1)
        lhs = jnp.where(bd2, jnp.concatenate([lhs] * B_HEADS, axis=0), jnp.zeros((kw, 2 * kw), BF16))
        rhs = jnp.concatenate([vstack, s.astype(BF16)], axis=0)
        o = _dot(lhs, rhs)
        s = dcol * s + kv
        for j in range(B_HEADS):
            oj = o[j * CHUNK:(j + 1) * CHUNK]
            sl = slice(j * LANES, (j + 1) * LANES)
            y = oj * lax.rsqrt(jnp.mean(oj * oj, axis=-1, keepdims=True) + EPS) * gn_ref[...]
            o_ref[rows, sl] = (y * r_ref[rows, sl].astype(F32)).astype(BF16)
    s_ref[...] = s


def _gla(q, k, v, g, r, gn, batch):
    t = q.shape[0]
    nb = t // batch // (GLA_TILES_PER_STEP * ROW_TILE)
    mexp, lvl = _gla_constants()
    cur = lambda b, i: (b * nb + i, 0)
    blk = lambda w: pl.BlockSpec((GLA_TILES_PER_STEP * ROW_TILE, w), cur)
    full = lambda a: pl.BlockSpec(a.shape, lambda b, i: (0,) * a.ndim)
    return pl.pallas_call(
        _gla_kernel,
        grid=(batch, nb),
        in_specs=[blk(B_KEY_WIDTH), blk(B_KEY_WIDTH), blk(B_VAL_WIDTH), blk(B_KEY_WIDTH), blk(B_VAL_WIDTH),
                  full(gn), full(mexp), full(lvl)],
        out_specs=blk(B_VAL_WIDTH),
        out_shape=jax.ShapeDtypeStruct((t, B_VAL_WIDTH), BF16),
        scratch_shapes=[pltpu.VMEM((B_KEY_WIDTH, B_VAL_DIM), F32)],
        compiler_params=_params("arbitrary", "arbitrary"),
        name="gla",
    )(q, k, v, g, r, gn, mexp, lvl)


def _token_kernel(x_ref, oa_ref, ob_ref, gate_ref, wb0_ref, wb1_ref, wmix_ref, gx_ref, wq_ref,
                  km_ref, vm_ref, wo_ref, gf_ref, wr_ref, br_ref, ltri_ref, utri_ref,
                  x2_ref, hs_ref, route_ref, cnt_ref):
    ma = _dot(oa_ref[...], wb0_ref[...])
    mb = _dot(ob_ref[...], wb1_ref[...])
    merged = (gate_ref[:, :D_MODEL].astype(F32) * ma + gate_ref[:, D_MODEL:].astype(F32) * mb).astype(BF16)
    x1 = x_ref[...] + _dot(merged, wmix_ref[...])

    h2 = _rms(x1, gx_ref[...]).astype(BF16)
    qx = (_dot(h2, wq_ref[...]) * (X_HEAD_DIM ** -0.5)).astype(BF16)
    heads = []
    for h in range(X_HEADS):
        sl = slice(h * X_HEAD_DIM, (h + 1) * X_HEAD_DIM)
        s = _dot_nt(qx[:, sl], km_ref[0, :, sl])
        m = jnp.max(s, axis=-1, keepdims=True)
        pe = jnp.exp(s - m)
        l = jnp.sum(pe, axis=-1, keepdims=True)
        heads.append((_dot(pe.astype(BF16), vm_ref[0, :, sl]) * (1.0 / l)).astype(BF16))
    x2 = x1 + _dot(jnp.concatenate(heads, axis=1), wo_ref[...])
    x2_ref[...] = x2

    h3 = _rms(x2, gf_ref[...])

    h3_hi = h3.astype(BF16)
    h3_lo = (h3 - h3_hi.astype(F32)).astype(BF16)
    hw = _dot(h3_hi, wr_ref[...])
    logits = hw[:, :LANES] + hw[:, LANES:] + _dot(h3_lo, wr_ref[:, :LANES]) + br_ref[...]
    oh0, oh1, g0, g1 = _route(logits)
    for h in range(TOKEN_TILES_PER_STEP):
        rows = slice(h * ROW_TILE, (h + 1) * ROW_TILE)
        chunks = pl.ds(h * LOCAL_CHUNKS, LOCAL_CHUNKS)
        _sort_tile(oh0[rows], oh1[rows], g0[rows], g1[rows], h3_hi[rows], ltri_ref, utri_ref,
                   hs_ref.at[chunks], route_ref.at[pl.ds(h * ROW_TILE, ROW_TILE)], cnt_ref.at[h])


def _route(logits):
    lane = lax.broadcasted_iota(I32, logits.shape, 1).astype(F32)
    big = jnp.float32(LANES)
    gl = jnp.where(lane < N_GROUPS, logits, NEG)
    gmax = jnp.max(gl, axis=-1, keepdims=True)
    gidx = jnp.min(jnp.where(gl == gmax, lane, big), axis=-1, keepdims=True)
    g_w = 1.0 / jnp.sum(jnp.exp(gl - gmax), axis=-1, keepdims=True)
    lo = N_GROUPS + EXPERTS_PER_GROUP * gidx
    el = jnp.where((lane >= lo) & (lane < lo + EXPERTS_PER_GROUP), logits, NEG)
    v1 = jnp.max(el, axis=-1, keepdims=True)
    i1 = jnp.min(jnp.where(el == v1, lane, big), axis=-1, keepdims=True)
    el2 = jnp.where(lane == i1, NEG, el)
    v2 = jnp.max(el2, axis=-1, keepdims=True)
    i2 = jnp.min(jnp.where(el2 == v2, lane, big), axis=-1, keepdims=True)
    e21 = jnp.exp(v2 - v1)
    w1 = g_w / (1.0 + e21)

    def gate_cols(w):
        hi = w.astype(BF16).astype(F32)
        return jnp.where(lane == 0, hi, jnp.where(lane == 1, w - hi, 0.0)).astype(BF16)

    oh0 = jnp.where(lane == i1 - N_GROUPS, 1.0, 0.0)
    oh1 = jnp.where(lane == i2 - N_GROUPS, 1.0, 0.0)
    return oh0, oh1, gate_cols(w1), gate_cols(w1 * e21)


def _sort_tile(oh0, oh1, g0, g1, h3_hi, ltri_ref, utri_ref, hs_ref, route_ref, cnt_ref):
    lane = lax.broadcasted_iota(I32, oh0.shape, 1)
    oh = oh0 + oh1
    nch = jnp.floor((jnp.sum(oh, axis=0, keepdims=True) + (CHUNK_ROWS - 1)) * (1.0 / CHUNK_ROWS))
    nch8 = jnp.broadcast_to(nch, (8, LANES))
    start = _dot(nch8.astype(BF16), utri_ref[...])[0:1] * CHUNK_ROWS
    rank = _dot(ltri_ref[...], oh.astype(BF16))
    row = start + rank
    pos0 = jnp.sum(row * oh0, axis=-1, keepdims=True)
    pos1 = jnp.sum(row * oh1, axis=-1, keepdims=True)
    route = jnp.where(lane == 0, pos0, jnp.where(lane == 1, pos1, 0.0))
    route_t = jnp.transpose(route)
    r = lax.broadcasted_iota(I32, (LOCAL_ROWS, ROW_TILE), 0).astype(F32)
    p0 = jnp.where(r == route_t[0:1, :], 1.0, 0.0).astype(BF16)
    p1 = jnp.where(r == route_t[1:2, :], 1.0, 0.0).astype(BF16)
    sorted_rows = jnp.concatenate([_dot(p0 + p1, h3_hi), _dot(p0, g0) + _dot(p1, g1)], axis=1)
    hs_ref[...] = sorted_rows.astype(BF16).reshape(hs_ref.shape)
    route_ref[...] = route
    cnt_ref[...] = nch8


def _token(x, oa, ob, gates, wb0, wb1, wmix, gx, wq, km, vm, wo, gf, wr, br, batch):
    t, d = x.shape
    n = TOKEN_TILES_PER_STEP
    nb = t // batch // (n * ROW_TILE)
    nt = t // ROW_TILE
    ltri = jnp.asarray(np.tril(np.ones((ROW_TILE, ROW_TILE), np.float32), -1), BF16)
    utri = jnp.asarray(np.triu(np.ones((LANES, LANES), np.float32), 1), BF16)
    cur = lambda b, i: (b * nb + i, 0)
    cur3 = lambda b, i: (b * nb + i, 0, 0)
    blk = lambda w: pl.BlockSpec((n * ROW_TILE, w), cur)
    full = lambda a: pl.BlockSpec(a.shape, lambda b, i: (0,) * a.ndim)
    mem = pl.BlockSpec((1,) + km.shape[1:], lambda b, i: (b, 0, 0))
    return pl.pallas_call(
        _token_kernel,
        grid=(batch, nb),
        in_specs=[blk(d), blk(A_WIDTH), blk(B_VAL_WIDTH), blk(2 * d), full(wb0), full(wb1), full(wmix),
                  full(gx), full(wq), mem, mem, full(wo), full(gf), full(wr), full(br), full(ltri), full(utri)],
        out_specs=[blk(d), pl.BlockSpec((n * LOCAL_CHUNKS, CHUNK_ROWS, SORT_WIDTH), cur3),
                   blk(LANES), pl.BlockSpec((n, 8, LANES), cur3)],
        out_shape=[jax.ShapeDtypeStruct((t, d), F32),
                   jax.ShapeDtypeStruct((nt * LOCAL_CHUNKS, CHUNK_ROWS, SORT_WIDTH), BF16),
                   jax.ShapeDtypeStruct((t, LANES), F32),
                   jax.ShapeDtypeStruct((nt, 8, LANES), F32)],
        compiler_params=_params("arbitrary", "arbitrary"),
        name="token",
    )(x, oa, ob, gates, wb0, wb1, wmix, gx, wq, km, vm, wo, gf, wr, br, ltri, utri)


def _expert_kernel(te_ref, nu_ref, nv_ref, ch_ref, hs_hbm, wg_ref, wu_ref, wd_ref, ys_hbm,
                   xbuf, ybuf, wgu_bf, wd_bf, gsem, ssem):
    i = pl.program_id(0)
    n_used = nu_ref[0]
    slot = lax.rem(i, 2)

    def for_chunks(tile, fn):
        nv = nv_ref[tile]

        @pl.when(nv == TILE_CHUNKS)
        def _():
            for c in range(TILE_CHUNKS):
                fn(c)

        @pl.when(nv != TILE_CHUNKS)
        def _():
            def body(c, carry):
                fn(c)
                return carry

            lax.fori_loop(0, nv, body, 0)

    def gather(tile, s, start):
        for c in range(TILE_CHUNKS):
            cp = pltpu.make_async_copy(hs_hbm.at[ch_ref[tile * TILE_CHUNKS + c]], xbuf.at[s, c], gsem.at[s])
            cp.start() if start else cp.wait()

    def scatter(tile, s, start):
        def one(c):
            cp = pltpu.make_async_copy(ybuf.at[s, c], ys_hbm.at[ch_ref[tile * TILE_CHUNKS + c]], ssem.at[s])
            cp.start(priority=1) if start else cp.wait()

        for_chunks(tile, one)

    next_tile = jnp.minimum(i + 1, pl.num_programs(0) - 1)

    @pl.when(i == 0)
    def _():
        gather(0, 0, True)

    @pl.when(i < n_used)
    def _():
        gather(i, slot, False)

        @pl.when(i >= 2)
        def _():
            scatter(i - 2, slot, False)

        @pl.when((i == 0) | (te_ref[i] != te_ref[jnp.maximum(i - 1, 0)]))
        def _():
            wgu_bf[:, :EXPERT_FF] = wg_ref[0].astype(BF16)
            wgu_bf[:, EXPERT_FF:] = wu_ref[0].astype(BF16)
            wd_bf[...] = wd_ref[0].astype(BF16)

        xg = xbuf[slot].reshape(EXPERT_ROWS, SORT_WIDTH)
        hgu = _dot(xg[:, :D_MODEL], wgu_bf[...])
        gather(next_tile, 1 - slot, True)
        hg, hu = hgu[:, :EXPERT_FF], hgu[:, EXPERT_FF:]
        hid = (hg * jax.nn.sigmoid(hg) * hu).astype(BF16)
        g = xg[:, D_MODEL:].astype(F32)
        y = ((g[:, 0:1] + g[:, 1:2]) * _dot(hid, wd_bf[...])).astype(BF16)
        y = jnp.concatenate([y, jnp.zeros((EXPERT_ROWS, LANES), BF16)], axis=1)
        ybuf[slot] = y.reshape(TILE_CHUNKS, CHUNK_ROWS, SORT_WIDTH)
        scatter(i, slot, True)

        @pl.when(i == n_used - 1)
        def _():
            gather(next_tile, 1 - slot, False)
            scatter(i, slot, False)

            @pl.when(i >= 1)
            def _():
                scatter(i - 1, 1 - slot, False)


def _experts(hs, tile_expert, n_used, n_valid, chunks, wg, wu, wd, layer):
    n_tiles = tile_expert.shape[0]
    last = lambda i, te, nu, nv, ch: jnp.minimum(i, nu[0] - 1)
    wmap = lambda i, te, nu, nv, ch: (layer * N_EXPERTS + te[last(i, te, nu, nv, ch)], 0, 0)
    anyspace = pl.BlockSpec(memory_space=pl.ANY)
    grid_spec = pltpu.PrefetchScalarGridSpec(
        num_scalar_prefetch=4,
        grid=(n_tiles,),
        in_specs=[anyspace,
                  pl.BlockSpec((1, D_MODEL, EXPERT_FF), wmap),
                  pl.BlockSpec((1, D_MODEL, EXPERT_FF), wmap),
                  pl.BlockSpec((1, EXPERT_FF, D_MODEL), wmap)],
        out_specs=anyspace,
        scratch_shapes=[pltpu.VMEM((2, TILE_CHUNKS, CHUNK_ROWS, SORT_WIDTH), BF16),
                        pltpu.VMEM((2, TILE_CHUNKS, CHUNK_ROWS, SORT_WIDTH), BF16),
                        pltpu.VMEM((D_MODEL, 2 * EXPERT_FF), BF16), pltpu.VMEM((EXPERT_FF, D_MODEL), BF16),
                        pltpu.SemaphoreType.DMA((2,)), pltpu.SemaphoreType.DMA((2,))],
    )
    return pl.pallas_call(
        _expert_kernel,
        grid_spec=grid_spec,
        out_shape=jax.ShapeDtypeStruct(hs.shape, BF16),
        input_output_aliases={4: 0},
        compiler_params=_params("arbitrary"),
        name="experts",
    )(tile_expert, n_used, n_valid, chunks, hs, wg, wu, wd)


def _combine_kernel(x_ref, route_ref, ys_ref, gfin_ref, o_ref):
    r = lax.broadcasted_iota(I32, (ROW_TILE, LOCAL_ROWS), 1).astype(F32)
    for t in range(COMBINE_TILES_PER_STEP):
        rows = slice(t * ROW_TILE, (t + 1) * ROW_TILE)
        pos = route_ref[rows, :]
        sel = jnp.where(r == pos[:, 0:1], 1.0, jnp.where(r == pos[:, 1:2], 1.0, 0.0)).astype(BF16)
        ys = ys_ref[t * LOCAL_CHUNKS:(t + 1) * LOCAL_CHUNKS].reshape(LOCAL_ROWS, D_MODEL)
        o_ref[rows, :] = _rms(x_ref[rows, :] + _dot(sel, ys), gfin_ref[...])


def _combine(x2, route, ys, gfin):
    t, d = x2.shape
    n = COMBINE_TILES_PER_STEP
    return pl.pallas_call(
        _combine_kernel,
        grid=(t // (n * ROW_TILE),),
        in_specs=[pl.BlockSpec((n * ROW_TILE, d), lambda i: (i, 0)),
                  pl.BlockSpec((n * ROW_TILE, LANES), lambda i: (i, 0)),
                  pl.BlockSpec((n * LOCAL_CHUNKS, CHUNK_ROWS, d), lambda i: (i, 0, 0)),
                  pl.BlockSpec((1, d), lambda i: (0, 0))],
        out_specs=pl.BlockSpec((n * ROW_TILE, d), lambda i: (i, 0)),
        out_shape=jax.ShapeDtypeStruct((t, d), F32),
        compiler_params=_params("arbitrary"),
        name="combine",
    )(x2, route, ys, gfin)


def _chunk_plan(nch, n_tiles):
    nt = nch.shape[0]
    local_start = jnp.cumsum(nch, axis=1) - nch
    cum = jnp.cumsum(nch, axis=0)
    total = cum[-1]
    tiles = (total + TILE_CHUNKS - 1) // TILE_CHUNKS
    tile_end = jnp.cumsum(tiles)
    n_used = tile_end[-1:]
    tile_ids = jnp.arange(n_tiles, dtype=I32)
    tile_expert = jnp.minimum(jnp.sum((tile_end[None, :] <= tile_ids[:, None]).astype(I32), axis=1),
                              N_EXPERTS - 1)
    sel = (tile_expert[:, None] == jnp.arange(N_EXPERTS, dtype=I32)[None, :]).astype(I32)
    pick = lambda table: jnp.sum(sel[:, :, None] * table.T[None, :, :], axis=1)
    first_tile = jnp.sum(sel * (tile_end - tiles)[None, :], axis=1)
    slot = (tile_ids - first_tile)[:, None] * TILE_CHUNKS + jnp.arange(TILE_CHUNKS, dtype=I32)[None, :]
    valid = (slot < jnp.sum(sel * total[None, :], axis=1)[:, None]) & (tile_ids < n_used)[:, None]
    src_tile = jnp.sum((pick(cum)[:, None, :] <= slot[:, :, None]).astype(I32), axis=2)
    src_tile = jnp.minimum(src_tile, nt - 1)
    at = (src_tile[:, :, None] == jnp.arange(nt, dtype=I32)[None, None, :]).astype(I32)
    before = jnp.sum(at * pick(cum - nch)[:, None, :], axis=2)
    start = jnp.sum(at * pick(local_start)[:, None, :], axis=2)
    chunk = jnp.where(valid, src_tile * LOCAL_CHUNKS + start + slot - before, LOCAL_CHUNKS - 1)
    return tile_expert, n_used, jnp.sum(valid.astype(I32), axis=1), chunk.reshape(-1)


def kernel(x, mem, norm_mix_g, w_in, rel_bias, gla_w_alpha, gla_b_alpha, gla_norm_g, w_branch, w_mix_out, norm_x_g, mem_norm_g, w_xq, w_xkv, w_xo, norm_ffn_g, w_group_router, b_group_router, w_expert_router, b_expert_router, w_exp_gate, w_exp_up, w_exp_down, final_norm_g):
    batch, seq, d = x.shape
    depth = w_in.shape[0]
    t = batch * seq
    step_tiles = max(TOKEN_TILES_PER_STEP, GLA_TILES_PER_STEP, BAND_TILES_PER_STEP, COMBINE_TILES_PER_STEP)
    assert d == D_MODEL and seq % (step_tiles * ROW_TILE) == 0 and seq % INPROJ_ROWS == 0
    nt = t // ROW_TILE
    n_tiles = nt * LOCAL_CHUNKS // TILE_CHUNKS + N_EXPERTS

    xf = x.reshape(t, d)
    km_all, vm_all = _memkv(mem, mem_norm_g, w_xkv.astype(BF16))
    row = lambda a: a.reshape(1, -1).astype(F32)

    moe = None
    for l in range(depth):
        res = _inproj(xf, row(norm_mix_g[l]), w_in, gla_w_alpha, row(gla_b_alpha[l]), l, moe)
        if moe is not None:
            xf, res = res[0], res[1:]
        aq, ak, av, bq, bk, bv, lga, br, gates = res

        oa = _band_attention(aq, ak, av, _band_bias_vector(rel_bias[l]), batch)
        ob = _gla(bq, bk, bv, lga, br, row(gla_norm_g[l]), batch)

        wr = jnp.pad(jnp.concatenate([w_group_router[l], w_expert_router[l]], axis=1).astype(F32),
                     ((0, 0), (0, LANES - N_GROUPS - N_EXPERTS)))
        wr_hi = wr.astype(BF16)
        wr = jnp.concatenate([wr_hi, (wr - wr_hi.astype(F32)).astype(BF16)], axis=1)
        brt = jnp.pad(jnp.concatenate([b_group_router[l], b_expert_router[l]]).astype(F32),
                      (0, LANES - N_GROUPS - N_EXPERTS)).reshape(1, LANES)
        x2, hs, route, cnt = _token(
            xf, oa, ob, gates, w_branch[l, 0].astype(BF16), w_branch[l, 1].astype(BF16),
            w_mix_out[l].astype(BF16), row(norm_x_g[l]), w_xq[l].astype(BF16), km_all[l], vm_all[l],
            w_xo[l].astype(BF16), row(norm_ffn_g[l]), wr, brt, batch)

        plan = _chunk_plan(cnt[:, 0, :N_EXPERTS].astype(I32), n_tiles)
        e3 = lambda w: w.reshape((depth * N_EXPERTS,) + w.shape[3:])
        ys = _experts(hs, *plan, e3(w_exp_gate), e3(w_exp_up), e3(w_exp_down), l)
        xf, moe = x2, (route, ys)

    return _combine(x2, route, ys, row(final_norm_g)).reshape(batch, seq, d)
```

```python
import functools

import numpy as np
import jax
import jax.numpy as jnp
from jax import lax
from jax.experimental import pallas as pl
from jax.experimental.pallas import tpu as pltpu

F32 = jnp.float32
BF16 = jnp.bfloat16
I32 = jnp.int32

D_MODEL = 1024
CHUNK = 64
EPS = 1e-6
A_HEADS = 8
A_HEAD_DIM = 64
A_WIDTH = 512
A_LEFT_CHUNKS = 8
A_MAX_REL = 256
B_HEADS = 4
B_KEY_DIM = 64
B_VAL_DIM = 128
B_KEY_WIDTH = 256
B_VAL_WIDTH = 512
B_GATE_RANK = 16
B_GATE_TAU = 16.0
X_HEADS = 4
X_HEAD_DIM = 256
N_GROUPS = 4
EXPERTS_PER_GROUP = 8
N_EXPERTS = N_GROUPS * EXPERTS_PER_GROUP
EXPERT_FF = 256

LANES = 128
ROW_TILE = 256
CHUNKS_PER_TILE = ROW_TILE // CHUNK
BAND_TILES = A_LEFT_CHUNKS // CHUNKS_PER_TILE + 1
BAND_KEYS = BAND_TILES * ROW_TILE
BIAS_PERIOD = 1024
LOG_CHUNK = 6
N_LEVELS = LOG_CHUNK
EXP_ROWS = (2 + N_LEVELS) * CHUNK
CHUNK_ROWS = 16
TOKEN_TILES_PER_STEP = 2
INPROJ_ROWS = 512
COMBINE_TILES_PER_STEP = 2
GLA_TILES_PER_STEP = 4
BAND_TILES_PER_STEP = 4
EXPERT_ROWS = 512
GATHER_AHEAD = 2
TILE_CHUNKS = EXPERT_ROWS // CHUNK_ROWS
LOCAL_CHUNKS = 2 * ROW_TILE // CHUNK_ROWS + N_EXPERTS
LOCAL_ROWS = LOCAL_CHUNKS * CHUNK_ROWS
assert (2 * ROW_TILE + N_EXPERTS * (CHUNK_ROWS - 1)) // CHUNK_ROWS < LOCAL_CHUNKS
SORT_WIDTH = D_MODEL + LANES
NEG = -1e30
LOG2E = 1.4426950408889634
VMEM_LIMIT = 56 * 1024 * 1024


def _params(*sem):
    return pltpu.CompilerParams(dimension_semantics=sem, vmem_limit_bytes=VMEM_LIMIT)


def _rms(x, g):
    return x * lax.rsqrt(jnp.mean(x * x, axis=-1, keepdims=True) + EPS) * g


def _dot(a, b):
    return jnp.dot(a, b, preferred_element_type=F32)


def _dot_nt(a, b):
    return lax.dot_general(a, b, (((1,), (1,)), ((), ())), preferred_element_type=F32)


def _dot_tn(a, b):
    return lax.dot_general(a, b, (((0,), (0,)), ((), ())), preferred_element_type=F32)


def _memkv_kernel(mem_ref, g_ref, w_ref, k_ref, v_ref):
    mn = _rms(mem_ref[0], g_ref[...]).astype(BF16)
    kv = _dot(mn, w_ref[0])
    k_ref[0, 0] = kv[:, :D_MODEL].astype(BF16)
    v_ref[0, 0] = kv[:, D_MODEL:].astype(BF16)


def _memkv(mem, g, w_xkv):
    depth = w_xkv.shape[0]
    b, m, d = mem.shape
    out = jax.ShapeDtypeStruct((depth, b, m, d), BF16)
    return pl.pallas_call(
        _memkv_kernel,
        grid=(depth, b),
        in_specs=[pl.BlockSpec((1, m, d), lambda l, i: (i, 0, 0)),
                  pl.BlockSpec((1, d), lambda l, i: (0, 0)),
                  pl.BlockSpec((1, d, 2 * d), lambda l, i: (l, 0, 0))],
        out_specs=[pl.BlockSpec((1, 1, m, d), lambda l, i: (l, i, 0, 0)),
                   pl.BlockSpec((1, 1, m, d), lambda l, i: (l, i, 0, 0))],
        out_shape=[out, out],
        compiler_params=_params("arbitrary", "arbitrary"),
        name="memkv",
    )(mem, g.reshape(1, d), w_xkv)


_R_AQ, _R_AK, _R_AV = 0, 512, 1024
_R_BQ, _R_BK, _R_BV = 1536, 1792, 2048
_R_ALPHA, _R_BR, _R_GATE, _R_END = 2560, 2576, 3088, 5136
_W_PIECE = 512


def _inproj_kernel(layer, fused, *refs):
    if fused:
        x_ref, route_ref, ys_ref, g_ref, wt_hbm, wal2_ref, bal_ref = refs[:7]
        refs = refs[7:]
        x3_ref, refs = refs[0], refs[1:]
    else:
        x_ref, g_ref, wt_hbm, wal2_ref, bal_ref = refs[:5]
        refs = refs[5:]
    (aq_ref, ak_ref, av_ref, bq_ref, bk_ref, bv_ref, lga_ref, br_ref, gate_ref,
     wt_bf, wal2_bf, stage, sem) = refs

    @pl.when(pl.program_id(0) == 0)
    def _():
        pieces = [(c, min(_W_PIECE, _R_END - c)) for c in range(0, _R_END, _W_PIECE)]

        def piece_copy(p):
            c, n = pieces[p]
            return pltpu.make_async_copy(wt_hbm.at[layer, pl.ds(c, n), :], stage.at[p % 2, pl.ds(0, n), :],
                                         sem.at[p % 2])

        piece_copy(0).start()
        for p, (c, n) in enumerate(pieces):
            if p + 1 < len(pieces):
                piece_copy(p + 1).start()
            piece_copy(p).wait()
            wt_bf[c:c + n, :] = stage[p % 2, 0:n, :].astype(BF16)
        wal2_bf[...] = jnp.concatenate(
            [wal2_ref[0].astype(BF16), jnp.zeros((LANES - B_GATE_RANK, B_KEY_WIDTH), BF16)], axis=0)

    if fused:
        r = lax.broadcasted_iota(I32, (ROW_TILE, LOCAL_ROWS), 1).astype(F32)
        tiles = []
        for t in range(INPROJ_ROWS // ROW_TILE):
            rows = slice(t * ROW_TILE, (t + 1) * ROW_TILE)
            pos = route_ref[rows, :]
            sel = jnp.where(r == pos[:, 0:1], 1.0, jnp.where(r == pos[:, 1:2], 1.0, 0.0)).astype(BF16)
            ys = ys_ref[t * LOCAL_CHUNKS:(t + 1) * LOCAL_CHUNKS].reshape(LOCAL_ROWS, D_MODEL)
            tiles.append(x_ref[rows, :] + _dot(sel, ys))
        x = jnp.concatenate(tiles, axis=0)
        x3_ref[...] = x
    else:
        x = x_ref[...]
    h = _rms(x, g_ref[...]).astype(BF16)

    def mm(lo, hi):
        return _dot_nt(h, wt_bf[lo:hi, :])

    aq_ref[...] = (mm(_R_AQ, _R_AK) * (A_HEAD_DIM ** -0.5 * LOG2E)).astype(BF16)
    ak_ref[...] = mm(_R_AK, _R_AV).astype(BF16)
    av_ref[...] = mm(_R_AV, _R_BQ).astype(BF16)
    bq_ref[...] = (mm(_R_BQ, _R_BK) * (B_KEY_DIM ** -0.5)).astype(BF16)
    bk_ref[...] = mm(_R_BK, _R_BV).astype(BF16)
    bv_ref[...] = mm(_R_BV, _R_ALPHA).astype(BF16)
    r = mm(_R_BR, _R_GATE)
    br_ref[...] = (r * jax.nn.sigmoid(r)).astype(BF16)
    for c in range(_R_GATE, _R_END, 512):
        gate_ref[:, c - _R_GATE:c - _R_GATE + 512] = jax.nn.sigmoid(mm(c, c + 512)).astype(BF16)
    z = _dot(mm(_R_ALPHA, _R_ALPHA + LANES).astype(BF16), wal2_bf[...]) + bal_ref[...]
    lga_ref[...] = (jnp.minimum(z, 0.0) - jnp.log(1.0 + jnp.exp(-jnp.abs(z)))) * (1.0 / B_GATE_TAU)


def _inproj(x, g, w_in, w_al2, b_al, layer, moe=None):
    t, d = x.shape
    assert w_in.shape[2] == _R_END
    row = lambda w: pl.BlockSpec((INPROJ_ROWS, w), lambda i: (i, 0))
    full = lambda a: pl.BlockSpec(a.shape, lambda i: (0,) * a.ndim)
    sds = lambda w, dt: jax.ShapeDtypeStruct((t, w), dt)
    widths = [(512, BF16), (512, BF16), (512, BF16), (256, BF16), (256, BF16), (512, BF16),
              (256, F32), (512, BF16), (2048, BF16)]
    fused = moe is not None
    moe_specs, moe_out_specs, moe_out_shape = [], [], []
    if fused:
        n_chunks = INPROJ_ROWS // ROW_TILE * LOCAL_CHUNKS
        moe_specs = [row(LANES), pl.BlockSpec((n_chunks, CHUNK_ROWS, d), lambda i: (i, 0, 0))]
        moe_out_specs, moe_out_shape = [row(d)], [sds(d, F32)]
    return pl.pallas_call(
        functools.partial(_inproj_kernel, layer, fused),
        grid=(t // INPROJ_ROWS,),
        in_specs=[row(d)] + moe_specs + [full(g), pl.BlockSpec(memory_space=pl.ANY),
                                         pl.BlockSpec((1,) + w_al2.shape[1:], lambda i: (layer, 0, 0)),
                                         full(b_al)],
        out_specs=moe_out_specs + [row(w) for w, _ in widths],
        out_shape=moe_out_shape + [sds(w, dt) for w, dt in widths],
        scratch_shapes=[pltpu.VMEM((_R_END, d), BF16), pltpu.VMEM((LANES, B_KEY_WIDTH), BF16),
                        pltpu.VMEM((2, _W_PIECE, d), F32), pltpu.SemaphoreType.DMA((2,))],
        compiler_params=_params("arbitrary"),
        name="inproj",
    )(x, *(moe or ()), g, jnp.swapaxes(w_in, 1, 2), w_al2, b_al)


def _band_kernel(q_ref, *refs):
    n_win = BAND_TILES_PER_STEP + BAND_TILES - 1
    k_refs, v_refs = refs[:n_win], refs[n_win:2 * n_win]
    u_ref, o_ref, bias_ref = refs[2 * n_win:]
    i = pl.program_id(1)
    lane = lax.broadcasted_iota(I32, (1, LANES), 1)
    low = lane < A_HEAD_DIM
    ones = jnp.ones((BAND_KEYS, LANES), BF16)

    @pl.when((pl.program_id(0) == 0) & (i == 0))
    def _():
        cq = lax.broadcasted_iota(I32, (ROW_TILE, BAND_KEYS), 0) >> LOG_CHUNK
        ck = lax.broadcasted_iota(I32, (ROW_TILE, BAND_KEYS), 1) >> LOG_CHUNK
        valid = (ck >= cq) & (ck <= cq + A_LEFT_CHUNKS)
        for h in range(A_HEADS):
            rows = jnp.broadcast_to(u_ref[h:h + 1, :], (ROW_TILE, BIAS_PERIOD))
            rows = pltpu.roll(rows, BIAS_PERIOD - (ROW_TILE - 1), 1, stride=1, stride_axis=0)
            bias_ref[h // 2, (h % 2) * ROW_TILE:(h % 2 + 1) * ROW_TILE, :] = jnp.where(
                valid, rows[:, :BAND_KEYS], NEG)

    def attend(tile, n_missing):
        rows = pl.ds(tile * ROW_TILE, ROW_TILE)
        window = range(tile, tile + BAND_TILES)
        for p in range(A_HEADS // 2):
            sl = slice(p * LANES, (p + 1) * LANES)
            qp = q_ref[rows, sl]
            zero = jnp.zeros_like(qp)
            q2 = jnp.concatenate([jnp.where(low, qp, zero), jnp.where(low, zero, qp)], axis=0)
            kp = jnp.concatenate([k_refs[j][:, sl] for j in window], axis=0)
            vp = jnp.concatenate([v_refs[j][:, sl] for j in window], axis=0)
            s = _dot_nt(q2, kp) + bias_ref[p]
            if n_missing:
                col = lax.broadcasted_iota(I32, (1, BAND_KEYS), 1)
                s = s + jnp.where(col < n_missing * ROW_TILE, NEG, 0.0).astype(F32)
            pe = jnp.exp2(s - jnp.max(s, axis=-1, keepdims=True)).astype(BF16)
            o2 = _dot(pe, jnp.concatenate([vp, ones], axis=1))
            o = o2[:, :LANES] * (1.0 / o2[:, LANES:])
            o_ref[rows, sl] = jnp.where(low, o[:ROW_TILE], o[ROW_TILE:]).astype(BF16)

    @pl.when(i > 0)
    def _():
        for tile in range(BAND_TILES_PER_STEP):
            attend(tile, 0)

    @pl.when(i == 0)
    def _():
        for tile in range(BAND_TILES_PER_STEP):
            attend(tile, max(BAND_TILES - 1 - tile, 0))


def _band_bias_vector(rel_table):
    h = rel_table.shape[0]
    tab = rel_table.astype(F32) * LOG2E
    shift = A_LEFT_CHUNKS * CHUNK + ROW_TILE - 1
    n_far = shift - A_MAX_REL + 1
    span = ROW_TILE + BAND_KEYS - 1
    assert span - 1 - shift <= A_MAX_REL and span <= BIAS_PERIOD
    u = jnp.concatenate([jnp.broadcast_to(tab[:, 2 * A_MAX_REL:], (h, n_far)),
                         tab[:, 2 * A_MAX_REL - 1:2 * A_MAX_REL - 1 - (span - n_far):-1]], axis=1)
    return jnp.pad(u, ((0, 0), (0, BIAS_PERIOD - span)))


def _band_attention(q, k, v, u, batch):
    t, w = q.shape
    n = BAND_TILES_PER_STEP
    nb = t // batch // (n * ROW_TILE)
    step = pl.BlockSpec((n * ROW_TILE, w), lambda b, i: (b * nb + i, 0))
    tile = lambda j: pl.BlockSpec(
        (ROW_TILE, w), lambda b, i: (n * b * nb + jnp.maximum(n * i + j - (BAND_TILES - 1), 0), 0))
    window = [tile(j) for j in range(n + BAND_TILES - 1)]
    return pl.pallas_call(
        _band_kernel,
        grid=(batch, nb),
        in_specs=[step] + window + window + [pl.BlockSpec(u.shape, lambda b, i: (0, 0))],
        out_specs=step,
        out_shape=jax.ShapeDtypeStruct((t, w), BF16),
        scratch_shapes=[pltpu.VMEM((A_HEADS // 2, 2 * ROW_TILE, BAND_KEYS), F32)],
        compiler_params=_params("arbitrary", "arbitrary"),
        name="band_attn",
    )(q, *([k] * len(window)), *([v] * len(window)), u)


def _gla_constants():
    c = CHUNK
    t = np.arange(c)[:, None]
    r = np.arange(c)[None, :]
    mats = [(r <= t), (r > t)]
    lvl = np.full((c, c), -1, np.int32)
    lvl[np.arange(c), np.arange(c)] = N_LEVELS
    for l in range(N_LEVELS):
        m = (c // 2) >> l
        mid = (t // (2 * m)) * (2 * m) + m
        upper = t >= mid
        mats.append(np.where(upper, (r >= mid) & (r <= t), (r > t) & (r < mid)))
        s = r
        same = (s // (2 * m)) == (t // (2 * m))
        lvl[np.asarray(same & upper & (s < mid))] = l
    eye = np.eye(CHUNKS_PER_TILE)
    mexp = np.concatenate([np.kron(eye, m) for m in mats], axis=0).astype(np.float32)
    lvl = np.tile(lvl, (1, B_HEADS))
    return jnp.asarray(mexp, BF16), jnp.asarray(lvl, I32)


def _gla_kernel(q_ref, k_ref, v_ref, g_ref, r_ref, gn_ref, mexp_ref, lvl_ref, o_ref, s_ref):
    @pl.when(pl.program_id(1) == 0)
    def _():
        s_ref[...] = jnp.zeros_like(s_ref)

    kw = B_KEY_WIDTH
    ri = lax.broadcasted_iota(I32, (kw, kw), 0) >> LOG_CHUNK
    ci = lax.broadcasted_iota(I32, (kw, kw), 1) >> LOG_CHUNK
    bd = ri == ci
    head_ind = jnp.where(bd, 1.0, 0.0).astype(BF16)
    ri2 = lax.broadcasted_iota(I32, (kw, 2 * kw), 0) >> LOG_CHUNK
    ci2 = (lax.broadcasted_iota(I32, (kw, 2 * kw), 1) & (kw - 1)) >> LOG_CHUNK
    bd2 = ri2 == ci2
    lvl = lvl_ref[...]
    row8 = lax.broadcasted_iota(I32, (16, kw), 0)
    ones = jnp.ones((16, LANES), BF16)
    zero_b = jnp.zeros((kw, kw), BF16)
    chunks = [slice(c * CHUNK, (c + 1) * CHUNK) for c in range(CHUNKS_PER_TILE)]

    def head_blocks(x):
        return jnp.where(bd, jnp.concatenate([x] * B_HEADS, axis=0), zero_b)

    def prepare(tile):
        trows = pl.ds(tile * ROW_TILE, ROW_TILE)
        q = q_ref[trows, :].astype(F32)
        k = k_ref[trows, :].astype(F32)
        g = g_ref[trows, :]
        gb = g.astype(BF16)
        half = EXP_ROWS * CHUNKS_PER_TILE // 2
        w = jnp.exp(jnp.concatenate([_dot(mexp_ref[:half, :], gb), _dot(mexp_ref[half:, :], gb)], axis=0))
        qt = (q * w[0:ROW_TILE]).astype(BF16)
        kb = (k * w[ROW_TILE:2 * ROW_TILE]).astype(BF16)
        qk = (q * k).astype(BF16)

        attn = [jnp.zeros((CHUNK, kw), F32) for _ in chunks]
        for l in range(N_LEVELS):
            wl = w[(2 + l) * ROW_TILE:(3 + l) * ROW_TILE]
            qh = (q * wl).astype(BF16)
            kh = (k * wl).astype(BF16)
            for c, rows in enumerate(chunks):
                attn[c] = jnp.where(lvl == l, _dot_nt(qh[rows], head_blocks(kh[rows])), attn[c])

        out = []
        for c, rows in enumerate(chunks):
            a = jnp.where(lvl == N_LEVELS, _dot(qk[rows], head_ind), attn[c])
            v = v_ref[pl.ds(tile * ROW_TILE + c * CHUNK, CHUNK), :]
            vstack = jnp.concatenate([v[:, j * LANES:(j + 1) * LANES] for j in range(B_HEADS)], axis=0)
            kv = _dot_tn(head_blocks(kb[rows]), vstack)
            d = jnp.exp(jnp.sum(g[rows], axis=0, keepdims=True))
            d1 = d.astype(BF16).astype(F32)
            dp = jnp.where(row8 == 0, d1, jnp.where(row8 == 1, d - d1, 0.0)).astype(BF16)
            dcol = _dot_tn(dp, ones)
            out.append((a.astype(BF16), qt[rows], vstack, kv, dcol))
        return out

    prepared = [p for tile in range(GLA_TILES_PER_STEP) for p in prepare(tile)]

    s = s_ref[...]
    for c, (a, qtc, vstack, kv, dcol) in enumerate(prepared):
        rows = pl.ds(c * CHUNK, CHUNK)
        lhs = jnp.concatenate([a, qtc], axis=1)
        lhs = jnp.where(bd2, jnp.concatenate([lhs] * B_HEADS, axis=0), jnp.zeros((kw, 2 * kw), BF16))
        rhs = jnp.concatenate([vstack, s.astype(BF16)], axis=0)
        o = _dot(lhs, rhs)
        s = dcol * s + kv
        for j in range(B_HEADS):
            oj = o[j * CHUNK:(j + 1) * CHUNK]
            sl = slice(j * LANES, (j + 1) * LANES)
            y = oj * lax.rsqrt(jnp.mean(oj * oj, axis=-1, keepdims=True) + EPS) * gn_ref[...]
            o_ref[rows, sl] = (y * r_ref[rows, sl].astype(F32)).astype(BF16)
    s_ref[...] = s


def _gla(q, k, v, g, r, gn, batch):
    t = q.shape[0]
    nb = t // batch // (GLA_TILES_PER_STEP * ROW_TILE)
    mexp, lvl = _gla_constants()
    cur = lambda b, i: (b * nb + i, 0)
    blk = lambda w: pl.BlockSpec((GLA_TILES_PER_STEP * ROW_TILE, w), cur)
    full = lambda a: pl.BlockSpec(a.shape, lambda b, i: (0,) * a.ndim)
    return pl.pallas_call(
        _gla_kernel,
        grid=(batch, nb),
        in_specs=[blk(B_KEY_WIDTH), blk(B_KEY_WIDTH), blk(B_VAL_WIDTH), blk(B_KEY_WIDTH), blk(B_VAL_WIDTH),
                  full(gn), full(mexp), full(lvl)],
        out_specs=blk(B_VAL_WIDTH),
        out_shape=jax.ShapeDtypeStruct((t, B_VAL_WIDTH), BF16),
        scratch_shapes=[pltpu.VMEM((B_KEY_WIDTH, B_VAL_DIM), F32)],
        compiler_params=_params("arbitrary", "arbitrary"),
        name="gla",
    )(q, k, v, g, r, gn, mexp, lvl)


def _token_kernel(x_ref, oa_ref, ob_ref, gate_ref, wb0_ref, wb1_ref, wmix_ref, gx_ref, wq_ref,
                  km_ref, vm_ref, wo_ref, gf_ref, wr_ref, br_ref, ltri_ref, utri_ref,
                  x2_ref, hs_ref, route_ref, cnt_ref):
    ma = _dot(oa_ref[...], wb0_ref[...])
    mb = _dot(ob_ref[...], wb1_ref[...])
    merged = (gate_ref[:, :D_MODEL].astype(F32) * ma + gate_ref[:, D_MODEL:].astype(F32) * mb).astype(BF16)
    x1 = x_ref[...] + _dot(merged, wmix_ref[...])

    h2 = _rms(x1, gx_ref[...]).astype(BF16)
    qx = (_dot(h2, wq_ref[...]) * (X_HEAD_DIM ** -0.5)).astype(BF16)
    heads = []
    for h in range(X_HEADS):
        sl = slice(h * X_HEAD_DIM, (h + 1) * X_HEAD_DIM)
        s = _dot_nt(qx[:, sl], km_ref[0, :, sl])
        m = jnp.max(s, axis=-1, keepdims=True)
        pe = jnp.exp(s - m)
        l = jnp.sum(pe, axis=-1, keepdims=True)
        heads.append((_dot(pe.astype(BF16), vm_ref[0, :, sl]) * (1.0 / l)).astype(BF16))
    x2 = x1 + _dot(jnp.concatenate(heads, axis=1), wo_ref[...])
    x2_ref[...] = x2

    h3 = _rms(x2, gf_ref[...])

    h3_hi = h3.astype(BF16)
    h3_lo = (h3 - h3_hi.astype(F32)).astype(BF16)
    hw = _dot(h3_hi, wr_ref[...])
    logits = hw[:, :LANES] + hw[:, LANES:] + _dot(h3_lo, wr_ref[:, :LANES]) + br_ref[...]
    oh0, oh1, g0, g1 = _route(logits)
    for h in range(TOKEN_TILES_PER_STEP):
        rows = slice(h * ROW_TILE, (h + 1) * ROW_TILE)
        chunks = pl.ds(h * LOCAL_CHUNKS, LOCAL_CHUNKS)
        _sort_tile(oh0[rows], oh1[rows], g0[rows], g1[rows], h3_hi[rows], ltri_ref, utri_ref,
                   hs_ref.at[chunks], route_ref.at[pl.ds(h * ROW_TILE, ROW_TILE)], cnt_ref.at[h])


def _route(logits):
    lane = lax.broadcasted_iota(I32, logits.shape, 1).astype(F32)
    big = jnp.float32(LANES)
    gl = jnp.where(lane < N_GROUPS, logits, NEG)
    gmax = jnp.max(gl, axis=-1, keepdims=True)
    gidx = jnp.min(jnp.where(gl == gmax, lane, big), axis=-1, keepdims=True)
    g_w = 1.0 / jnp.sum(jnp.exp(gl - gmax), axis=-1, keepdims=True)
    lo = N_GROUPS + EXPERTS_PER_GROUP * gidx
    el = jnp.where((lane >= lo) & (lane < lo + EXPERTS_PER_GROUP), logits, NEG)
    v1 = jnp.max(el, axis=-1, keepdims=True)
    i1 = jnp.min(jnp.where(el == v1, lane, big), axis=-1, keepdims=True)
    el2 = jnp.where(lane == i1, NEG, el)
    v2 = jnp.max(el2, axis=-1, keepdims=True)
    i2 = jnp.min(jnp.where(el2 == v2, lane, big), axis=-1, keepdims=True)
    e21 = jnp.exp(v2 - v1)
    w1 = g_w / (1.0 + e21)

    def gate_cols(w):
        hi = w.astype(BF16).astype(F32)
        return jnp.where(lane == 0, hi, jnp.where(lane == 1, w - hi, 0.0)).astype(BF16)

    oh0 = jnp.where(lane == i1 - N_GROUPS, 1.0, 0.0)
    oh1 = jnp.where(lane == i2 - N_GROUPS, 1.0, 0.0)
    return oh0, oh1, gate_cols(w1), gate_cols(w1 * e21)


def _sort_tile(oh0, oh1, g0, g1, h3_hi, ltri_ref, utri_ref, hs_ref, route_ref, cnt_ref):
    lane = lax.broadcasted_iota(I32, oh0.shape, 1)
    oh = oh0 + oh1
    nch = jnp.floor((jnp.sum(oh, axis=0, keepdims=True) + (CHUNK_ROWS - 1)) * (1.0 / CHUNK_ROWS))
    nch8 = jnp.broadcast_to(nch, (8, LANES))
    start = _dot(nch8.astype(BF16), utri_ref[...])[0:1] * CHUNK_ROWS
    rank = _dot(ltri_ref[...], oh.astype(BF16))
    row = start + rank
    pos0 = jnp.sum(row * oh0, axis=-1, keepdims=True)
    pos1 = jnp.sum(row * oh1, axis=-1, keepdims=True)
    route = jnp.where(lane == 0, pos0, jnp.where(lane == 1, pos1, 0.0))
    route_t = jnp.transpose(route)
    r = lax.broadcasted_iota(I32, (LOCAL_ROWS, ROW_TILE), 0).astype(F32)
    p0 = jnp.where(r == route_t[0:1, :], 1.0, 0.0).astype(BF16)
    p1 = jnp.where(r == route_t[1:2, :], 1.0, 0.0).astype(BF16)
    sorted_rows = jnp.concatenate([_dot(p0 + p1, h3_hi), _dot(p0, g0) + _dot(p1, g1)], axis=1)
    hs_ref[...] = sorted_rows.astype(BF16).reshape(hs_ref.shape)
    route_ref[...] = route
    cnt_ref[...] = nch8


def _token(x, oa, ob, gates, wb0, wb1, wmix, gx, wq, km, vm, wo, gf, wr, br, batch):
    t, d = x.shape
    n = TOKEN_TILES_PER_STEP
    nb = t // batch // (n * ROW_TILE)
    nt = t // ROW_TILE
    ltri = jnp.asarray(np.tril(np.ones((ROW_TILE, ROW_TILE), np.float32), -1), BF16)
    utri = jnp.asarray(np.triu(np.ones((LANES, LANES), np.float32), 1), BF16)
    cur = lambda b, i: (b * nb + i, 0)
    cur3 = lambda b, i: (b * nb + i, 0, 0)
    blk = lambda w: pl.BlockSpec((n * ROW_TILE, w), cur)
    full = lambda a: pl.BlockSpec(a.shape, lambda b, i: (0,) * a.ndim)
    mem = pl.BlockSpec((1,) + km.shape[1:], lambda b, i: (b, 0, 0))
    return pl.pallas_call(
        _token_kernel,
        grid=(batch, nb),
        in_specs=[blk(d), blk(A_WIDTH), blk(B_VAL_WIDTH), blk(2 * d), full(wb0), full(wb1), full(wmix),
                  full(gx), full(wq), mem, mem, full(wo), full(gf), full(wr), full(br), full(ltri), full(utri)],
        out_specs=[blk(d), pl.BlockSpec((n * LOCAL_CHUNKS, CHUNK_ROWS, SORT_WIDTH), cur3),
                   blk(LANES), pl.BlockSpec((n, 8, LANES), cur3)],
        out_shape=[jax.ShapeDtypeStruct((t, d), F32),
                   jax.ShapeDtypeStruct((nt * LOCAL_CHUNKS, CHUNK_ROWS, SORT_WIDTH), BF16),
                   jax.ShapeDtypeStruct((t, LANES), F32),
                   jax.ShapeDtypeStruct((nt, 8, LANES), F32)],
        compiler_params=_params("arbitrary", "arbitrary"),
        name="token",
    )(x, oa, ob, gates, wb0, wb1, wmix, gx, wq, km, vm, wo, gf, wr, br, ltri, utri)


def _expert_kernel(te_ref, nu_ref, nv_ref, ch_ref, hs_hbm, wg_ref, wu_ref, wd_ref, ys_hbm,
                   xbuf, ybuf, wgu_bf, wd_bf, gsem, ssem):
    i = pl.program_id(0)
    n_used = nu_ref[0]
    slot = lax.rem(i, 2)
    ring = GATHER_AHEAD + 1
    last_tile = pl.num_programs(0) - 1

    def for_chunks(tile, fn):
        nv = nv_ref[tile]

        @pl.when(nv == TILE_CHUNKS)
        def _():
            for c in range(TILE_CHUNKS):
                fn(c)

        @pl.when(nv != TILE_CHUNKS)
        def _():
            def body(c, carry):
                fn(c)
                return carry

            lax.fori_loop(0, nv, body, 0)

    def gather(step, start):
        tile = jnp.minimum(step, last_tile)
        s = lax.rem(step, ring)
        for c in range(TILE_CHUNKS):
            cp = pltpu.make_async_copy(hs_hbm.at[ch_ref[tile * TILE_CHUNKS + c]], xbuf.at[s, c], gsem.at[s])
            cp.start() if start else cp.wait()

    def scatter(tile, s, start):
        def one(c):
            cp = pltpu.make_async_copy(ybuf.at[s, c], ys_hbm.at[ch_ref[tile * TILE_CHUNKS + c]], ssem.at[s])
            cp.start(priority=1) if start else cp.wait()

        for_chunks(tile, one)

    @pl.when(i == 0)
    def _():
        for step in range(GATHER_AHEAD):
            gather(step, True)

    @pl.when(i < n_used)
    def _():
        gather(i, False)

        @pl.when(i >= 2)
        def _():
            scatter(i - 2, slot, False)

        @pl.when((i == 0) | (te_ref[i] != te_ref[jnp.maximum(i - 1, 0)]))
        def _():
            wgu_bf[:, :EXPERT_FF] = wg_ref[0].astype(BF16)
            wgu_bf[:, EXPERT_FF:] = wu_ref[0].astype(BF16)
            wd_bf[...] = wd_ref[0].astype(BF16)

        xg = xbuf[lax.rem(i, ring)].reshape(EXPERT_ROWS, SORT_WIDTH)
        hgu = _dot(xg[:, :D_MODEL], wgu_bf[...])
        gather(i + GATHER_AHEAD, True)
        hg, hu = hgu[:, :EXPERT_FF], hgu[:, EXPERT_FF:]
        hid = (hg * jax.nn.sigmoid(hg) * hu).astype(BF16)
        g = xg[:, D_MODEL:].astype(F32)
        y = ((g[:, 0:1] + g[:, 1:2]) * _dot(hid, wd_bf[...])).astype(BF16)
        y = jnp.concatenate([y, jnp.zeros((EXPERT_ROWS, LANES), BF16)], axis=1)
        ybuf[slot] = y.reshape(TILE_CHUNKS, CHUNK_ROWS, SORT_WIDTH)
        scatter(i, slot, True)

        @pl.when(i == n_used - 1)
        def _():
            for ahead in range(1, GATHER_AHEAD + 1):
                gather(i + ahead, False)
            scatter(i, slot, False)

            @pl.when(i >= 1)
            def _():
                scatter(i - 1, 1 - slot, False)


def _experts(hs, tile_expert, n_used, n_valid, chunks, wg, wu, wd, layer):
    n_tiles = tile_expert.shape[0]
    last = lambda i, te, nu, nv, ch: jnp.minimum(i, nu[0] - 1)
    wmap = lambda i, te, nu, nv, ch: (layer * N_EXPERTS + te[last(i, te, nu, nv, ch)], 0, 0)
    anyspace = pl.BlockSpec(memory_space=pl.ANY)
    grid_spec = pltpu.PrefetchScalarGridSpec(
        num_scalar_prefetch=4,
        grid=(n_tiles,),
        in_specs=[anyspace,
                  pl.BlockSpec((1, D_MODEL, EXPERT_FF), wmap),
                  pl.BlockSpec((1, D_MODEL, EXPERT_FF), wmap),
                  pl.BlockSpec((1, EXPERT_FF, D_MODEL), wmap)],
        out_specs=anyspace,
        scratch_shapes=[pltpu.VMEM((GATHER_AHEAD + 1, TILE_CHUNKS, CHUNK_ROWS, SORT_WIDTH), BF16),
                        pltpu.VMEM((2, TILE_CHUNKS, CHUNK_ROWS, SORT_WIDTH), BF16),
                        pltpu.VMEM((D_MODEL, 2 * EXPERT_FF), BF16), pltpu.VMEM((EXPERT_FF, D_MODEL), BF16),
                        pltpu.SemaphoreType.DMA((GATHER_AHEAD + 1,)), pltpu.SemaphoreType.DMA((2,))],
    )
    return pl.pallas_call(
        _expert_kernel,
        grid_spec=grid_spec,
        out_shape=jax.ShapeDtypeStruct(hs.shape, BF16),
        input_output_aliases={4: 0},
        compiler_params=_params("arbitrary"),
        name="experts",
    )(tile_expert, n_used, n_valid, chunks, hs, wg, wu, wd)


def _combine_kernel(x_ref, route_ref, ys_ref, gfin_ref, o_ref):
    r = lax.broadcasted_iota(I32, (ROW_TILE, LOCAL_ROWS), 1).astype(F32)
    for t in range(COMBINE_TILES_PER_STEP):
        rows = slice(t * ROW_TILE, (t + 1) * ROW_TILE)
        pos = route_ref[rows, :]
        sel = jnp.where(r == pos[:, 0:1], 1.0, jnp.where(r == pos[:, 1:2], 1.0, 0.0)).astype(BF16)
        ys = ys_ref[t * LOCAL_CHUNKS:(t + 1) * LOCAL_CHUNKS].reshape(LOCAL_ROWS, D_MODEL)
        o_ref[rows, :] = _rms(x_ref[rows, :] + _dot(sel, ys), gfin_ref[...])


def _combine(x2, route, ys, gfin):
    t, d = x2.shape
    n = COMBINE_TILES_PER_STEP
    return pl.pallas_call(
        _combine_kernel,
        grid=(t // (n * ROW_TILE),),
        in_specs=[pl.BlockSpec((n * ROW_TILE, d), lambda i: (i, 0)),
                  pl.BlockSpec((n * ROW_TILE, LANES), lambda i: (i, 0)),
                  pl.BlockSpec((n * LOCAL_CHUNKS, CHUNK_ROWS, d), lambda i: (i, 0, 0)),
                  pl.BlockSpec((1, d), lambda i: (0, 0))],
        out_specs=pl.BlockSpec((n * ROW_TILE, d), lambda i: (i, 0)),
        out_shape=jax.ShapeDtypeStruct((t, d), F32),
        compiler_params=_params("arbitrary"),
        name="combine",
    )(x2, route, ys, gfin)


def _chunk_plan(nch, n_tiles):
    nt = nch.shape[0]
    local_start = jnp.cumsum(nch, axis=1) - nch
    cum = jnp.cumsum(nch, axis=0)
    total = cum[-1]
    tiles = (total + TILE_CHUNKS - 1) // TILE_CHUNKS
    tile_end = jnp.cumsum(tiles)
    n_used = tile_end[-1:]
    tile_ids = jnp.arange(n_tiles, dtype=I32)
    tile_expert = jnp.minimum(jnp.sum((tile_end[None, :] <= tile_ids[:, None]).astype(I32), axis=1),
                              N_EXPERTS - 1)
    sel = (tile_expert[:, None] == jnp.arange(N_EXPERTS, dtype=I32)[None, :]).astype(I32)
    pick = lambda table: jnp.sum(sel[:, :, None] * table.T[None, :, :], axis=1)
    first_tile = jnp.sum(sel * (tile_end - tiles)[None, :], axis=1)
    slot = (tile_ids - first_tile)[:, None] * TILE_CHUNKS + jnp.arange(TILE_CHUNKS, dtype=I32)[None, :]
    valid = (slot < jnp.sum(sel * total[None, :], axis=1)[:, None]) & (tile_ids < n_used)[:, None]
    src_tile = jnp.sum((pick(cum)[:, None, :] <= slot[:, :, None]).astype(I32), axis=2)
    src_tile = jnp.minimum(src_tile, nt - 1)
    at = (src_tile[:, :, None] == jnp.arange(nt, dtype=I32)[None, None, :]).astype(I32)
    before = jnp.sum(at * pick(cum - nch)[:, None, :], axis=2)
    start = jnp.sum(at * pick(local_start)[:, None, :], axis=2)
    chunk = jnp.where(valid, src_tile * LOCAL_CHUNKS + start + slot - before, LOCAL_CHUNKS - 1)
    return tile_expert, n_used, jnp.sum(valid.astype(I32), axis=1), chunk.reshape(-1)


def kernel(x, mem, norm_mix_g, w_in, rel_bias, gla_w_alpha, gla_b_alpha, gla_norm_g, w_branch, w_mix_out, norm_x_g, mem_norm_g, w_xq, w_xkv, w_xo, norm_ffn_g, w_group_router, b_group_router, w_expert_router, b_expert_router, w_exp_gate, w_exp_up, w_exp_down, final_norm_g):
    batch, seq, d = x.shape
    depth = w_in.shape[0]
    t = batch * seq
    step_tiles = max(TOKEN_TILES_PER_STEP, GLA_TILES_PER_STEP, BAND_TILES_PER_STEP, COMBINE_TILES_PER_STEP)
    assert d == D_MODEL and seq % (step_tiles * ROW_TILE) == 0 and seq % INPROJ_ROWS == 0
    nt = t // ROW_TILE
    n_tiles = nt * LOCAL_CHUNKS // TILE_CHUNKS + N_EXPERTS

    xf = x.reshape(t, d)
    km_all, vm_all = _memkv(mem, mem_norm_g, w_xkv.astype(BF16))
    row = lambda a: a.reshape(1, -1).astype(F32)

    moe = None
    for l in range(depth):
        res = _inproj(xf, row(norm_mix_g[l]), w_in, gla_w_alpha, row(gla_b_alpha[l]), l, moe)
        if moe is not None:
            xf, res = res[0], res[1:]
        aq, ak, av, bq, bk, bv, lga, br, gates = res

        oa = _band_attention(aq, ak, av, _band_bias_vector(rel_bias[l]), batch)
        ob = _gla(bq, bk, bv, lga, br, row(gla_norm_g[l]), batch)

        wr = jnp.pad(jnp.concatenate([w_group_router[l], w_expert_router[l]], axis=1).astype(F32),
                     ((0, 0), (0, LANES - N_GROUPS - N_EXPERTS)))
        wr_hi = wr.astype(BF16)
        wr = jnp.concatenate([wr_hi, (wr - wr_hi.astype(F32)).astype(BF16)], axis=1)
        brt = jnp.pad(jnp.concatenate([b_group_router[l], b_expert_router[l]]).astype(F32),
                      (0, LANES - N_GROUPS - N_EXPERTS)).reshape(1, LANES)
        x2, hs, route, cnt = _token(
            xf, oa, ob, gates, w_branch[l, 0].astype(BF16), w_branch[l, 1].astype(BF16),
            w_mix_out[l].astype(BF16), row(norm_x_g[l]), w_xq[l].astype(BF16), km_all[l], vm_all[l],
            w_xo[l].astype(BF16), row(norm_ffn_g[l]), wr, brt, batch)

        plan = _chunk_plan(cnt[:, 0, :N_EXPERTS].astype(I32), n_tiles)
        e3 = lambda w: w.reshape((depth * N_EXPERTS,) + w.shape[3:])
        ys = _experts(hs, *plan, e3(w_exp_gate), e3(w_exp_up), e3(w_exp_down), l)
        xf, moe = x2, (route, ys)

    return _combine(x2, route, ys, row(final_norm_g)).reshape(batch, seq, d)
```

```python
import functools

import numpy as np
import jax
import jax.numpy as jnp
from jax import lax
from jax.experimental import pallas as pl
from jax.experimental.pallas import tpu as pltpu

F32 = jnp.float32
BF16 = jnp.bfloat16
I32 = jnp.int32

D_MODEL = 1024
CHUNK = 64
EPS = 1e-6
A_HEADS = 8
A_HEAD_DIM = 64
A_WIDTH = 512
A_LEFT_CHUNKS = 8
A_MAX_REL = 256
B_HEADS = 4
B_KEY_DIM = 64
B_VAL_DIM = 128
B_KEY_WIDTH = 256
B_VAL_WIDTH = 512
B_GATE_RANK = 16
B_GATE_TAU = 16.0
X_HEADS = 4
X_HEAD_DIM = 256
N_GROUPS = 4
EXPERTS_PER_GROUP = 8
N_EXPERTS = N_GROUPS * EXPERTS_PER_GROUP
EXPERT_FF = 256

LANES = 128
ROW_TILE = 256
CHUNKS_PER_TILE = ROW_TILE // CHUNK
BAND_TILES = A_LEFT_CHUNKS // CHUNKS_PER_TILE + 1
BAND_KEYS = BAND_TILES * ROW_TILE
BIAS_PERIOD = 1024
LOG_CHUNK = 6
N_LEVELS = LOG_CHUNK
EXP_ROWS = (2 + N_LEVELS) * CHUNK
CHUNK_ROWS = 16
TOKEN_TILES_PER_STEP = 2
INPROJ_ROWS = 512
COMBINE_TILES_PER_STEP = 2
GLA_TILES_PER_STEP = 4
BAND_TILES_PER_STEP = 4
EXPERT_ROWS = 512
GATHER_AHEAD = 3
TILE_CHUNKS = EXPERT_ROWS // CHUNK_ROWS
LOCAL_CHUNKS = 2 * ROW_TILE // CHUNK_ROWS + N_EXPERTS
LOCAL_ROWS = LOCAL_CHUNKS * CHUNK_ROWS
assert (2 * ROW_TILE + N_EXPERTS * (CHUNK_ROWS - 1)) // CHUNK_ROWS < LOCAL_CHUNKS
SORT_WIDTH = D_MODEL + LANES
NEG = -1e30
LOG2E = 1.4426950408889634
VMEM_LIMIT = 56 * 1024 * 1024


def _params(*sem):
    return pltpu.CompilerParams(dimension_semantics=sem, vmem_limit_bytes=VMEM_LIMIT)


def _rms(x, g):
    return x * lax.rsqrt(jnp.mean(x * x, axis=-1, keepdims=True) + EPS) * g


def _dot(a, b):
    return jnp.dot(a, b, preferred_element_type=F32)


def _dot_nt(a, b):
    return lax.dot_general(a, b, (((1,), (1,)), ((), ())), preferred_element_type=F32)


def _dot_tn(a, b):
    return lax.dot_general(a, b, (((0,), (0,)), ((), ())), preferred_element_type=F32)


def _memkv_kernel(mem_ref, g_ref, w_ref, k_ref, v_ref):
    mn = _rms(mem_ref[0], g_ref[...]).astype(BF16)
    kv = _dot(mn, w_ref[0])
    k_ref[0, 0] = kv[:, :D_MODEL].astype(BF16)
    v_ref[0, 0] = kv[:, D_MODEL:].astype(BF16)


def _memkv(mem, g, w_xkv):
    depth = w_xkv.shape[0]
    b, m, d = mem.shape
    out = jax.ShapeDtypeStruct((depth, b, m, d), BF16)
    return pl.pallas_call(
        _memkv_kernel,
        grid=(depth, b),
        in_specs=[pl.BlockSpec((1, m, d), lambda l, i: (i, 0, 0)),
                  pl.BlockSpec((1, d), lambda l, i: (0, 0)),
                  pl.BlockSpec((1, d, 2 * d), lambda l, i: (l, 0, 0))],
        out_specs=[pl.BlockSpec((1, 1, m, d), lambda l, i: (l, i, 0, 0)),
                   pl.BlockSpec((1, 1, m, d), lambda l, i: (l, i, 0, 0))],
        out_shape=[out, out],
        compiler_params=_params("arbitrary", "arbitrary"),
        name="memkv",
    )(mem, g.reshape(1, d), w_xkv)


_R_AQ, _R_AK, _R_AV = 0, 512, 1024
_R_BQ, _R_BK, _R_BV = 1536, 1792, 2048
_R_ALPHA, _R_BR, _R_GATE, _R_END = 2560, 2576, 3088, 5136
_W_PIECE = 512


def _inproj_kernel(layer, fused, *refs):
    if fused:
        x_ref, route_ref, ys_ref, g_ref, wt_hbm, wal2_ref, bal_ref = refs[:7]
        refs = refs[7:]
        x3_ref, refs = refs[0], refs[1:]
    else:
        x_ref, g_ref, wt_hbm, wal2_ref, bal_ref = refs[:5]
        refs = refs[5:]
    (aq_ref, ak_ref, av_ref, bq_ref, bk_ref, bv_ref, lga_ref, br_ref, gate_ref,
     wt_bf, wal2_bf, stage, sem) = refs

    @pl.when(pl.program_id(0) == 0)
    def _():
        pieces = [(c, min(_W_PIECE, _R_END - c)) for c in range(0, _R_END, _W_PIECE)]

        def piece_copy(p):
            c, n = pieces[p]
            return pltpu.make_async_copy(wt_hbm.at[layer, pl.ds(c, n), :], stage.at[p % 2, pl.ds(0, n), :],
                                         sem.at[p % 2])

        piece_copy(0).start()
        for p, (c, n) in enumerate(pieces):
            if p + 1 < len(pieces):
                piece_copy(p + 1).start()
            piece_copy(p).wait()
            wt_bf[c:c + n, :] = stage[p % 2, 0:n, :].astype(BF16)
        wal2_bf[...] = jnp.concatenate(
            [wal2_ref[0].astype(BF16), jnp.zeros((LANES - B_GATE_RANK, B_KEY_WIDTH), BF16)], axis=0)

    if fused:
        r = lax.broadcasted_iota(I32, (ROW_TILE, LOCAL_ROWS), 1).astype(F32)
        tiles = []
        for t in range(INPROJ_ROWS // ROW_TILE):
            rows = slice(t * ROW_TILE, (t + 1) * ROW_TILE)
            pos = route_ref[rows, :]
            sel = jnp.where(r == pos[:, 0:1], 1.0, jnp.where(r == pos[:, 1:2], 1.0, 0.0)).astype(BF16)
            ys = ys_ref[t * LOCAL_CHUNKS:(t + 1) * LOCAL_CHUNKS].reshape(LOCAL_ROWS, D_MODEL)
            tiles.append(x_ref[rows, :] + _dot(sel, ys))
        x = jnp.concatenate(tiles, axis=0)
        x3_ref[...] = x
    else:
        x = x_ref[...]
    h = _rms(x, g_ref[...]).astype(BF16)

    def mm(lo, hi):
        return _dot_nt(h, wt_bf[lo:hi, :])

    aq_ref[...] = (mm(_R_AQ, _R_AK) * (A_HEAD_DIM ** -0.5 * LOG2E)).astype(BF16)
    ak_ref[...] = mm(_R_AK, _R_AV).astype(BF16)
    av_ref[...] = mm(_R_AV, _R_BQ).astype(BF16)
    bq_ref[...] = (mm(_R_BQ, _R_BK) * (B_KEY_DIM ** -0.5)).astype(BF16)
    bk_ref[...] = mm(_R_BK, _R_BV).astype(BF16)
    bv_ref[...] = mm(_R_BV, _R_ALPHA).astype(BF16)
    r = mm(_R_BR, _R_GATE)
    br_ref[...] = (r * jax.nn.sigmoid(r)).astype(BF16)
    for c in range(_R_GATE, _R_END, 512):
        gate_ref[:, c - _R_GATE:c - _R_GATE + 512] = jax.nn.sigmoid(mm(c, c + 512)).astype(BF16)
    z = _dot(mm(_R_ALPHA, _R_ALPHA + LANES).astype(BF16), wal2_bf[...]) + bal_ref[...]
    lga_ref[...] = (jnp.minimum(z, 0.0) - jnp.log(1.0 + jnp.exp(-jnp.abs(z)))) * (1.0 / B_GATE_TAU)


def _inproj(x, g, w_in, w_al2, b_al, layer, moe=None):
    t, d = x.shape
    assert w_in.shape[2] == _R_END
    row = lambda w: pl.BlockSpec((INPROJ_ROWS, w), lambda i: (i, 0))
    full = lambda a: pl.BlockSpec(a.shape, lambda i: (0,) * a.ndim)
    sds = lambda w, dt: jax.ShapeDtypeStruct((t, w), dt)
    widths = [(512, BF16), (512, BF16), (512, BF16), (256, BF16), (256, BF16), (512, BF16),
              (256, F32), (512, BF16), (2048, BF16)]
    fused = moe is not None
    moe_specs, moe_out_specs, moe_out_shape = [], [], []
    if fused:
        n_chunks = INPROJ_ROWS // ROW_TILE * LOCAL_CHUNKS
        moe_specs = [row(LANES), pl.BlockSpec((n_chunks, CHUNK_ROWS, d), lambda i: (i, 0, 0))]
        moe_out_specs, moe_out_shape = [row(d)], [sds(d, F32)]
    return pl.pallas_call(
        functools.partial(_inproj_kernel, layer, fused),
        grid=(t // INPROJ_ROWS,),
        in_specs=[row(d)] + moe_specs + [full(g), pl.BlockSpec(memory_space=pl.ANY),
                                         pl.BlockSpec((1,) + w_al2.shape[1:], lambda i: (layer, 0, 0)),
                                         full(b_al)],
        out_specs=moe_out_specs + [row(w) for w, _ in widths],
        out_shape=moe_out_shape + [sds(w, dt) for w, dt in widths],
        scratch_shapes=[pltpu.VMEM((_R_END, d), BF16), pltpu.VMEM((LANES, B_KEY_WIDTH), BF16),
                        pltpu.VMEM((2, _W_PIECE, d), F32), pltpu.SemaphoreType.DMA((2,))],
        compiler_params=_params("arbitrary"),
        name="inproj",
    )(x, *(moe or ()), g, jnp.swapaxes(w_in, 1, 2), w_al2, b_al)


def _band_kernel(q_ref, *refs):
    n_win = BAND_TILES_PER_STEP + BAND_TILES - 1
    k_refs, v_refs = refs[:n_win], refs[n_win:2 * n_win]
    u_ref, o_ref, bias_ref = refs[2 * n_win:]
    i = pl.program_id(1)
    lane = lax.broadcasted_iota(I32, (1, LANES), 1)
    low = lane < A_HEAD_DIM
    ones = jnp.ones((BAND_KEYS, LANES), BF16)

    @pl.when((pl.program_id(0) == 0) & (i == 0))
    def _():
        cq = lax.broadcasted_iota(I32, (ROW_TILE, BAND_KEYS), 0) >> LOG_CHUNK
        ck = lax.broadcasted_iota(I32, (ROW_TILE, BAND_KEYS), 1) >> LOG_CHUNK
        valid = (ck >= cq) & (ck <= cq + A_LEFT_CHUNKS)
        for h in range(A_HEADS):
            rows = jnp.broadcast_to(u_ref[h:h + 1, :], (ROW_TILE, BIAS_PERIOD))
            rows = pltpu.roll(rows, BIAS_PERIOD - (ROW_TILE - 1), 1, stride=1, stride_axis=0)
            bias_ref[h // 2, (h % 2) * ROW_TILE:(h % 2 + 1) * ROW_TILE, :] = jnp.where(
                valid, rows[:, :BAND_KEYS], NEG)

    def attend(tile, n_missing):
        rows = pl.ds(tile * ROW_TILE, ROW_TILE)
        window = range(tile, tile + BAND_TILES)
        for p in range(A_HEADS // 2):
            sl = slice(p * LANES, (p + 1) * LANES)
            qp = q_ref[rows, sl]
            zero = jnp.zeros_like(qp)
            q2 = jnp.concatenate([jnp.where(low, qp, zero), jnp.where(low, zero, qp)], axis=0)
            kp = jnp.concatenate([k_refs[j][:, sl] for j in window], axis=0)
            vp = jnp.concatenate([v_refs[j][:, sl] for j in window], axis=0)
            s = _dot_nt(q2, kp) + bias_ref[p]
            if n_missing:
                col = lax.broadcasted_iota(I32, (1, BAND_KEYS), 1)
                s = s + jnp.where(col < n_missing * ROW_TILE, NEG, 0.0).astype(F32)
            pe = jnp.exp2(s - jnp.max(s, axis=-1, keepdims=True)).astype(BF16)
            o2 = _dot(pe, jnp.concatenate([vp, ones], axis=1))
            o = o2[:, :LANES] * (1.0 / o2[:, LANES:])
            o_ref[rows, sl] = jnp.where(low, o[:ROW_TILE], o[ROW_TILE:]).astype(BF16)

    @pl.when(i > 0)
    def _():
        for tile in range(BAND_TILES_PER_STEP):
            attend(tile, 0)

    @pl.when(i == 0)
    def _():
        for tile in range(BAND_TILES_PER_STEP):
            attend(tile, max(BAND_TILES - 1 - tile, 0))


def _band_bias_vector(rel_table):
    h = rel_table.shape[0]
    tab = rel_table.astype(F32) * LOG2E
    shift = A_LEFT_CHUNKS * CHUNK + ROW_TILE - 1
    n_far = shift - A_MAX_REL + 1
    span = ROW_TILE + BAND_KEYS - 1
    assert span - 1 - shift <= A_MAX_REL and span <= BIAS_PERIOD
    u = jnp.concatenate([jnp.broadcast_to(tab[:, 2 * A_MAX_REL:], (h, n_far)),
                         tab[:, 2 * A_MAX_REL - 1:2 * A_MAX_REL - 1 - (span - n_far):-1]], axis=1)
    return jnp.pad(u, ((0, 0), (0, BIAS_PERIOD - span)))


def _band_attention(q, k, v, u, batch):
    t, w = q.shape
    n = BAND_TILES_PER_STEP
    nb = t // batch // (n * ROW_TILE)
    step = pl.BlockSpec((n * ROW_TILE, w), lambda b, i: (b * nb + i, 0))
    tile = lambda j: pl.BlockSpec(
        (ROW_TILE, w), lambda b, i: (n * b * nb + jnp.maximum(n * i + j - (BAND_TILES - 1), 0), 0))
    window = [tile(j) for j in range(n + BAND_TILES - 1)]
    return pl.pallas_call(
        _band_kernel,
        grid=(batch, nb),
        in_specs=[step] + window + window + [pl.BlockSpec(u.shape, lambda b, i: (0, 0))],
        out_specs=step,
        out_shape=jax.ShapeDtypeStruct((t, w), BF16),
        scratch_shapes=[pltpu.VMEM((A_HEADS // 2, 2 * ROW_TILE, BAND_KEYS), F32)],
        compiler_params=_params("arbitrary", "arbitrary"),
        name="band_attn",
    )(q, *([k] * len(window)), *([v] * len(window)), u)


def _gla_constants():
    c = CHUNK
    t = np.arange(c)[:, None]
    r = np.arange(c)[None, :]
    mats = [(r <= t), (r > t)]
    lvl = np.full((c, c), -1, np.int32)
    lvl[np.arange(c), np.arange(c)] = N_LEVELS
    for l in range(N_LEVELS):
        m = (c // 2) >> l
        mid = (t // (2 * m)) * (2 * m) + m
        upper = t >= mid
        mats.append(np.where(upper, (r >= mid) & (r <= t), (r > t) & (r < mid)))
        s = r
        same = (s // (2 * m)) == (t // (2 * m))
        lvl[np.asarray(same & upper & (s < mid))] = l
    eye = np.eye(CHUNKS_PER_TILE)
    mexp = np.concatenate([np.kron(eye, m) for m in mats], axis=0).astype(np.float32)
    lvl = np.tile(lvl, (1, B_HEADS))
    return jnp.asarray(mexp, BF16), jnp.asarray(lvl, I32)


def _gla_kernel(q_ref, k_ref, v_ref, g_ref, r_ref, gn_ref, mexp_ref, lvl_ref, o_ref, s_ref):
    @pl.when(pl.program_id(1) == 0)
    def _():
        s_ref[...] = jnp.zeros_like(s_ref)

    kw = B_KEY_WIDTH
    ri = lax.broadcasted_iota(I32, (kw, kw), 0) >> LOG_CHUNK
    ci = lax.broadcasted_iota(I32, (kw, kw), 1) >> LOG_CHUNK
    bd = ri == ci
    head_ind = jnp.where(bd, 1.0, 0.0).astype(BF16)
    ri2 = lax.broadcasted_iota(I32, (kw, 2 * kw), 0) >> LOG_CHUNK
    ci2 = (lax.broadcasted_iota(I32, (kw, 2 * kw), 1) & (kw - 1)) >> LOG_CHUNK
    bd2 = ri2 == ci2
    lvl = lvl_ref[...]
    row8 = lax.broadcasted_iota(I32, (16, kw), 0)
    ones = jnp.ones((16, LANES), BF16)
    zero_b = jnp.zeros((kw, kw), BF16)
    chunks = [slice(c * CHUNK, (c + 1) * CHUNK) for c in range(CHUNKS_PER_TILE)]

    def head_blocks(x):
        return jnp.where(bd, jnp.concatenate([x] * B_HEADS, axis=0), zero_b)

    def prepare(tile):
        trows = pl.ds(tile * ROW_TILE, ROW_TILE)
        q = q_ref[trows, :].astype(F32)
        k = k_ref[trows, :].astype(F32)
        g = g_ref[trows, :]
        gb = g.astype(BF16)
        half = EXP_ROWS * CHUNKS_PER_TILE // 2
        w = jnp.exp(jnp.concatenate([_dot(mexp_ref[:half, :], gb), _dot(mexp_ref[half:, :], gb)], axis=0))
        qt = (q * w[0:ROW_TILE]).astype(BF16)
        kb = (k * w[ROW_TILE:2 * ROW_TILE]).astype(BF16)
        qk = (q * k).astype(BF16)

        attn = [jnp.zeros((CHUNK, kw), F32) for _ in chunks]
        for l in range(N_LEVELS):
            wl = w[(2 + l) * ROW_TILE:(3 + l) * ROW_TILE]
            qh = (q * wl).astype(BF16)
            kh = (k * wl).astype(BF16)
            for c, rows in enumerate(chunks):
                attn[c] = jnp.where(lvl == l, _dot_nt(qh[rows], head_blocks(kh[rows])), attn[c])

        out = []
        for c, rows in enumerate(chunks):
            a = jnp.where(lvl == N_LEVELS, _dot(qk[rows], head_ind), attn[c])
            v = v_ref[pl.ds(tile * ROW_TILE + c * CHUNK, CHUNK), :]
            vstack = jnp.concatenate([v[:, j * LANES:(j + 1) * LANES] for j in range(B_HEADS)], axis=0)
            kv = _dot_tn(head_blocks(kb[rows]), vstack)
            d = jnp.exp(jnp.sum(g[rows], axis=0, keepdims=True))
            d1 = d.astype(BF16).astype(F32)
            dp = jnp.where(row8 == 0, d1, jnp.where(row8 == 1, d - d1, 0.0)).astype(BF16)
            dcol = _dot_tn(dp, ones)
            out.append((a.astype(BF16), qt[rows], vstack, kv, dcol))
        return out

    prepared = [p for tile in range(GLA_TILES_PER_STEP) for p in prepare(tile)]

    s = s_ref[...]
    for c, (a, qtc, vstack, kv, dcol) in enumerate(prepared):
        rows = pl.ds(c * CHUNK, CHUNK)
        lhs = jnp.concatenate([a, qtc], axis=1)
        lhs = jnp.where(bd2, jnp.concatenate([lhs] * B_HEADS, axis=0), jnp.zeros((kw, 2 * kw), BF16))
        rhs = jnp.concatenate([vstack, s.astype(BF16)], axis=0)
        o = _dot(lhs, rhs)
        s = dcol * s + kv
        for j in range(B_HEADS):
            oj = o[j * CHUNK:(j + 1) * CHUNK]
            sl = slice(j * LANES, (j + 1) * LANES)
            y = oj * lax.rsqrt(jnp.mean(oj * oj, axis=-1, keepdims=True) + EPS) * gn_ref[...]
            o_ref[rows, sl] = (y * r_ref[rows, sl].astype(F32)).astype(BF16)
    s_ref[...] = s


def _gla(q, k, v, g, r, gn, batch):
    t = q.shape[0]
    nb = t // batch // (GLA_TILES_PER_STEP * ROW_TILE)
    mexp, lvl = _gla_constants()
    cur = lambda b, i: (b * nb + i, 0)
    blk = lambda w: pl.BlockSpec((GLA_TILES_PER_STEP * ROW_TILE, w), cur)
    full = lambda a: pl.BlockSpec(a.shape, lambda b, i: (0,) * a.ndim)
    return pl.pallas_call(
        _gla_kernel,
        grid=(batch, nb),
        in_specs=[blk(B_KEY_WIDTH), blk(B_KEY_WIDTH), blk(B_VAL_WIDTH), blk(B_KEY_WIDTH), blk(B_VAL_WIDTH),
                  full(gn), full(mexp), full(lvl)],
        out_specs=blk(B_VAL_WIDTH),
        out_shape=jax.ShapeDtypeStruct((t, B_VAL_WIDTH), BF16),
        scratch_shapes=[pltpu.VMEM((B_KEY_WIDTH, B_VAL_DIM), F32)],
        compiler_params=_params("arbitrary", "arbitrary"),
        name="gla",
    )(q, k, v, g, r, gn, mexp, lvl)


def _token_kernel(x_ref, oa_ref, ob_ref, gate_ref, wb0_ref, wb1_ref, wmix_ref, gx_ref, wq_ref,
                  km_ref, vm_ref, wo_ref, gf_ref, wr_ref, br_ref, ltri_ref, utri_ref,
                  x2_ref, hs_ref, route_ref, cnt_ref):
    ma = _dot(oa_ref[...], wb0_ref[...])
    mb = _dot(ob_ref[...], wb1_ref[...])
    merged = (gate_ref[:, :D_MODEL].astype(F32) * ma + gate_ref[:, D_MODEL:].astype(F32) * mb).astype(BF16)
    x1 = x_ref[...] + _dot(merged, wmix_ref[...])

    h2 = _rms(x1, gx_ref[...]).astype(BF16)
    qx = (_dot(h2, wq_ref[...]) * (X_HEAD_DIM ** -0.5)).astype(BF16)
    heads = []
    for h in range(X_HEADS):
        sl = slice(h * X_HEAD_DIM, (h + 1) * X_HEAD_DIM)
        s = _dot_nt(qx[:, sl], km_ref[0, :, sl])
        m = jnp.max(s, axis=-1, keepdims=True)
        pe = jnp.exp(s - m)
        l = jnp.sum(pe, axis=-1, keepdims=True)
        heads.append((_dot(pe.astype(BF16), vm_ref[0, :, sl]) * (1.0 / l)).astype(BF16))
    x2 = x1 + _dot(jnp.concatenate(heads, axis=1), wo_ref[...])
    x2_ref[...] = x2

    h3 = _rms(x2, gf_ref[...])

    h3_hi = h3.astype(BF16)
    h3_lo = (h3 - h3_hi.astype(F32)).astype(BF16)
    hw = _dot(h3_hi, wr_ref[...])
    logits = hw[:, :LANES] + hw[:, LANES:] + _dot(h3_lo, wr_ref[:, :LANES]) + br_ref[...]
    oh0, oh1, g0, g1 = _route(logits)
    for h in range(TOKEN_TILES_PER_STEP):
        rows = slice(h * ROW_TILE, (h + 1) * ROW_TILE)
        chunks = pl.ds(h * LOCAL_CHUNKS, LOCAL_CHUNKS)
        _sort_tile(oh0[rows], oh1[rows], g0[rows], g1[rows], h3_hi[rows], ltri_ref, utri_ref,
                   hs_ref.at[chunks], route_ref.at[pl.ds(h * ROW_TILE, ROW_TILE)], cnt_ref.at[h])


def _route(logits):
    lane = lax.broadcasted_iota(I32, logits.shape, 1).astype(F32)
    big = jnp.float32(LANES)
    gl = jnp.where(lane < N_GROUPS, logits, NEG)
    gmax = jnp.max(gl, axis=-1, keepdims=True)
    gidx = jnp.min(jnp.where(gl == gmax, lane, big), axis=-1, keepdims=True)
    g_w = 1.0 / jnp.sum(jnp.exp(gl - gmax), axis=-1, keepdims=True)
    lo = N_GROUPS + EXPERTS_PER_GROUP * gidx
    el = jnp.where((lane >= lo) & (lane < lo + EXPERTS_PER_GROUP), logits, NEG)
    v1 = jnp.max(el, axis=-1, keepdims=True)
    i1 = jnp.min(jnp.where(el == v1, lane, big), axis=-1, keepdims=True)
    el2 = jnp.where(lane == i1, NEG, el)
    v2 = jnp.max(el2, axis=-1, keepdims=True)
    i2 = jnp.min(jnp.where(el2 == v2, lane, big), axis=-1, keepdims=True)
    e21 = jnp.exp(v2 - v1)
    w1 = g_w / (1.0 + e21)

    def gate_cols(w):
        hi = w.astype(BF16).astype(F32)
        return jnp.where(lane == 0, hi, jnp.where(lane == 1, w - hi, 0.0)).astype(BF16)

    oh0 = jnp.where(lane == i1 - N_GROUPS, 1.0, 0.0)
    oh1 = jnp.where(lane == i2 - N_GROUPS, 1.0, 0.0)
    return oh0, oh1, gate_cols(w1), gate_cols(w1 * e21)


def _sort_tile(oh0, oh1, g0, g1, h3_hi, ltri_ref, utri_ref, hs_ref, route_ref, cnt_ref):
    lane = lax.broadcasted_iota(I32, oh0.shape, 1)
    oh = oh0 + oh1
    nch = jnp.floor((jnp.sum(oh, axis=0, keepdims=True) + (CHUNK_ROWS - 1)) * (1.0 / CHUNK_ROWS))
    nch8 = jnp.broadcast_to(nch, (8, LANES))
    start = _dot(nch8.astype(BF16), utri_ref[...])[0:1] * CHUNK_ROWS
    rank = _dot(ltri_ref[...], oh.astype(BF16))
    row = start + rank
    pos0 = jnp.sum(row * oh0, axis=-1, keepdims=True)
    pos1 = jnp.sum(row * oh1, axis=-1, keepdims=True)
    route = jnp.where(lane == 0, pos0, jnp.where(lane == 1, pos1, 0.0))
    route_t = jnp.transpose(route)
    r = lax.broadcasted_iota(I32, (LOCAL_ROWS, ROW_TILE), 0).astype(F32)
    p0 = jnp.where(r == route_t[0:1, :], 1.0, 0.0).astype(BF16)
    p1 = jnp.where(r == route_t[1:2, :], 1.0, 0.0).astype(BF16)
    sorted_rows = jnp.concatenate([_dot(p0 + p1, h3_hi), _dot(p0, g0) + _dot(p1, g1)], axis=1)
    hs_ref[...] = sorted_rows.astype(BF16).reshape(hs_ref.shape)
    route_ref[...] = route
    cnt_ref[...] = nch8


def _token(x, oa, ob, gates, wb0, wb1, wmix, gx, wq, km, vm, wo, gf, wr, br, batch):
    t, d = x.shape
    n = TOKEN_TILES_PER_STEP
    nb = t // batch // (n * ROW_TILE)
    nt = t // ROW_TILE
    ltri = jnp.asarray(np.tril(np.ones((ROW_TILE, ROW_TILE), np.float32), -1), BF16)
    utri = jnp.asarray(np.triu(np.ones((LANES, LANES), np.float32), 1), BF16)
    cur = lambda b, i: (b * nb + i, 0)
    cur3 = lambda b, i: (b * nb + i, 0, 0)
    blk = lambda w: pl.BlockSpec((n * ROW_TILE, w), cur)
    full = lambda a: pl.BlockSpec(a.shape, lambda b, i: (0,) * a.ndim)
    mem = pl.BlockSpec((1,) + km.shape[1:], lambda b, i: (b, 0, 0))
    return pl.pallas_call(
        _token_kernel,
        grid=(batch, nb),
        in_specs=[blk(d), blk(A_WIDTH), blk(B_VAL_WIDTH), blk(2 * d), full(wb0), full(wb1), full(wmix),
                  full(gx), full(wq), mem, mem, full(wo), full(gf), full(wr), full(br), full(ltri), full(utri)],
        out_specs=[blk(d), pl.BlockSpec((n * LOCAL_CHUNKS, CHUNK_ROWS, SORT_WIDTH), cur3),
                   blk(LANES), pl.BlockSpec((n, 8, LANES), cur3)],
        out_shape=[jax.ShapeDtypeStruct((t, d), F32),
                   jax.ShapeDtypeStruct((nt * LOCAL_CHUNKS, CHUNK_ROWS, SORT_WIDTH), BF16),
                   jax.ShapeDtypeStruct((t, LANES), F32),
                   jax.ShapeDtypeStruct((nt, 8, LANES), F32)],
        compiler_params=_params("arbitrary", "arbitrary"),
        name="token",
    )(x, oa, ob, gates, wb0, wb1, wmix, gx, wq, km, vm, wo, gf, wr, br, ltri, utri)


def _expert_kernel(te_ref, nu_ref, nv_ref, ch_ref, hs_hbm, wg_ref, wu_ref, wd_ref, ys_hbm,
                   xbuf, ybuf, wgu_bf, wd_bf, gsem, ssem):
    i = pl.program_id(0)
    n_used = nu_ref[0]
    slot = lax.rem(i, 2)
    ring = GATHER_AHEAD + 1
    last_tile = pl.num_programs(0) - 1

    def for_chunks(tile, fn):
        nv = nv_ref[tile]

        @pl.when(nv == TILE_CHUNKS)
        def _():
            for c in range(TILE_CHUNKS):
                fn(c)

        @pl.when(nv != TILE_CHUNKS)
        def _():
            def body(c, carry):
                fn(c)
                return carry

            lax.fori_loop(0, nv, body, 0)

    def gather(step, start):
        tile = jnp.minimum(step, last_tile)
        s = lax.rem(step, ring)
        for c in range(TILE_CHUNKS):
            cp = pltpu.make_async_copy(hs_hbm.at[ch_ref[tile * TILE_CHUNKS + c]], xbuf.at[s, c], gsem.at[s])
            cp.start() if start else cp.wait()

    def scatter(tile, s, start):
        def one(c):
            cp = pltpu.make_async_copy(ybuf.at[s, c], ys_hbm.at[ch_ref[tile * TILE_CHUNKS + c]], ssem.at[s])
            cp.start(priority=1) if start else cp.wait()

        for_chunks(tile, one)

    @pl.when(i == 0)
    def _():
        for step in range(GATHER_AHEAD):
            gather(step, True)

    @pl.when(i < n_used)
    def _():
        gather(i, False)

        @pl.when(i >= 2)
        def _():
            scatter(i - 2, slot, False)

        @pl.when((i == 0) | (te_ref[i] != te_ref[jnp.maximum(i - 1, 0)]))
        def _():
            wgu_bf[:, :EXPERT_FF] = wg_ref[0].astype(BF16)
            wgu_bf[:, EXPERT_FF:] = wu_ref[0].astype(BF16)
            wd_bf[...] = wd_ref[0].astype(BF16)

        xg = xbuf[lax.rem(i, ring)].reshape(EXPERT_ROWS, SORT_WIDTH)
        hgu = _dot(xg[:, :D_MODEL], wgu_bf[...])
        gather(i + GATHER_AHEAD, True)
        hg, hu = hgu[:, :EXPERT_FF], hgu[:, EXPERT_FF:]
        hid = (hg * jax.nn.sigmoid(hg) * hu).astype(BF16)
        g = xg[:, D_MODEL:].astype(F32)
        y = ((g[:, 0:1] + g[:, 1:2]) * _dot(hid, wd_bf[...])).astype(BF16)
        y = jnp.concatenate([y, jnp.zeros((EXPERT_ROWS, LANES), BF16)], axis=1)
        ybuf[slot] = y.reshape(TILE_CHUNKS, CHUNK_ROWS, SORT_WIDTH)
        scatter(i, slot, True)

        @pl.when(i == n_used - 1)
        def _():
            for ahead in range(1, GATHER_AHEAD + 1):
                gather(i + ahead, False)
            scatter(i, slot, False)

            @pl.when(i >= 1)
            def _():
                scatter(i - 1, 1 - slot, False)


def _experts(hs, tile_expert, n_used, n_valid, chunks, wg, wu, wd, layer):
    n_tiles = tile_expert.shape[0]
    last = lambda i, te, nu, nv, ch: jnp.minimum(i, nu[0] - 1)
    wmap = lambda i, te, nu, nv, ch: (layer * N_EXPERTS + te[last(i, te, nu, nv, ch)], 0, 0)
    anyspace = pl.BlockSpec(memory_space=pl.ANY)
    grid_spec = pltpu.PrefetchScalarGridSpec(
        num_scalar_prefetch=4,
        grid=(n_tiles,),
        in_specs=[anyspace,
                  pl.BlockSpec((1, D_MODEL, EXPERT_FF), wmap),
                  pl.BlockSpec((1, D_MODEL, EXPERT_FF), wmap),
                  pl.BlockSpec((1, EXPERT_FF, D_MODEL), wmap)],
        out_specs=anyspace,
        scratch_shapes=[pltpu.VMEM((GATHER_AHEAD + 1, TILE_CHUNKS, CHUNK_ROWS, SORT_WIDTH), BF16),
                        pltpu.VMEM((2, TILE_CHUNKS, CHUNK_ROWS, SORT_WIDTH), BF16),
                        pltpu.VMEM((D_MODEL, 2 * EXPERT_FF), BF16), pltpu.VMEM((EXPERT_FF, D_MODEL), BF16),
                        pltpu.SemaphoreType.DMA((GATHER_AHEAD + 1,)), pltpu.SemaphoreType.DMA((2,))],
    )
    return pl.pallas_call(
        _expert_kernel,
        grid_spec=grid_spec,
        out_shape=jax.ShapeDtypeStruct(hs.shape, BF16),
        input_output_aliases={4: 0},
        compiler_params=_params("arbitrary"),
        name="experts",
    )(tile_expert, n_used, n_valid, chunks, hs, wg, wu, wd)


def _combine_kernel(x_ref, route_ref, ys_ref, gfin_ref, o_ref):
    r = lax.broadcasted_iota(I32, (ROW_TILE, LOCAL_ROWS), 1).astype(F32)
    for t in range(COMBINE_TILES_PER_STEP):
        rows = slice(t * ROW_TILE, (t + 1) * ROW_TILE)
        pos = route_ref[rows, :]
        sel = jnp.where(r == pos[:, 0:1], 1.0, jnp.where(r == pos[:, 1:2], 1.0, 0.0)).astype(BF16)
        ys = ys_ref[t * LOCAL_CHUNKS:(t + 1) * LOCAL_CHUNKS].reshape(LOCAL_ROWS, D_MODEL)
        o_ref[rows, :] = _rms(x_ref[rows, :] + _dot(sel, ys), gfin_ref[...])


def _combine(x2, route, ys, gfin):
    t, d = x2.shape
    n = COMBINE_TILES_PER_STEP
    return pl.pallas_call(
        _combine_kernel,
        grid=(t // (n * ROW_TILE),),
        in_specs=[pl.BlockSpec((n * ROW_TILE, d), lambda i: (i, 0)),
                  pl.BlockSpec((n * ROW_TILE, LANES), lambda i: (i, 0)),
                  pl.BlockSpec((n * LOCAL_CHUNKS, CHUNK_ROWS, d), lambda i: (i, 0, 0)),
                  pl.BlockSpec((1, d), lambda i: (0, 0))],
        out_specs=pl.BlockSpec((n * ROW_TILE, d), lambda i: (i, 0)),
        out_shape=jax.ShapeDtypeStruct((t, d), F32),
        compiler_params=_params("arbitrary"),
        name="combine",
    )(x2, route, ys, gfin)


def _chunk_plan(nch, n_tiles):
    nt = nch.shape[0]
    local_start = jnp.cumsum(nch, axis=1) - nch
    cum = jnp.cumsum(nch, axis=0)
    total = cum[-1]
    tiles = (total + TILE_CHUNKS - 1) // TILE_CHUNKS
    tile_end = jnp.cumsum(tiles)
    n_used = tile_end[-1:]
    tile_ids = jnp.arange(n_tiles, dtype=I32)
    tile_expert = jnp.minimum(jnp.sum((tile_end[None, :] <= tile_ids[:, None]).astype(I32), axis=1),
                              N_EXPERTS - 1)
    sel = (tile_expert[:, None] == jnp.arange(N_EXPERTS, dtype=I32)[None, :]).astype(I32)
    pick = lambda table: jnp.sum(sel[:, :, None] * table.T[None, :, :], axis=1)
    first_tile = jnp.sum(sel * (tile_end - tiles)[None, :], axis=1)
    slot = (tile_ids - first_tile)[:, None] * TILE_CHUNKS + jnp.arange(TILE_CHUNKS, dtype=I32)[None, :]
    valid = (slot < jnp.sum(sel * total[None, :], axis=1)[:, None]) & (tile_ids < n_used)[:, None]
    src_tile = jnp.sum((pick(cum)[:, None, :] <= slot[:, :, None]).astype(I32), axis=2)
    src_tile = jnp.minimum(src_tile, nt - 1)
    at = (src_tile[:, :, None] == jnp.arange(nt, dtype=I32)[None, None, :]).astype(I32)
    before = jnp.sum(at * pick(cum - nch)[:, None, :], axis=2)
    start = jnp.sum(at * pick(local_start)[:, None, :], axis=2)
    chunk = jnp.where(valid, src_tile * LOCAL_CHUNKS + start + slot - before, LOCAL_CHUNKS - 1)
    return tile_expert, n_used, jnp.sum(valid.astype(I32), axis=1), chunk.reshape(-1)


def kernel(x, mem, norm_mix_g, w_in, rel_bias, gla_w_alpha, gla_b_alpha, gla_norm_g, w_branch, w_mix_out, norm_x_g, mem_norm_g, w_xq, w_xkv, w_xo, norm_ffn_g, w_group_router, b_group_router, w_expert_router, b_expert_router, w_exp_gate, w_exp_up, w_exp_down, final_norm_g):
    batch, seq, d = x.shape
    depth = w_in.shape[0]
    t = batch * seq
    step_tiles = max(TOKEN_TILES_PER_STEP, GLA_TILES_PER_STEP, BAND_TILES_PER_STEP, COMBINE_TILES_PER_STEP)
    assert d == D_MODEL and seq % (step_tiles * ROW_TILE) == 0 and seq % INPROJ_ROWS == 0
    nt = t // ROW_TILE
    n_tiles = nt * LOCAL_CHUNKS // TILE_CHUNKS + N_EXPERTS

    xf = x.reshape(t, d)
    km_all, vm_all = _memkv(mem, mem_norm_g, w_xkv.astype(BF16))
    row = lambda a: a.reshape(1, -1).astype(F32)

    moe = None
    for l in range(depth):
        res = _inproj(xf, row(norm_mix_g[l]), w_in, gla_w_alpha, row(gla_b_alpha[l]), l, moe)
        if moe is not None:
            xf, res = res[0], res[1:]
        aq, ak, av, bq, bk, bv, lga, br, gates = res

        oa = _band_attention(aq, ak, av, _band_bias_vector(rel_bias[l]), batch)
        ob = _gla(bq, bk, bv, lga, br, row(gla_norm_g[l]), batch)

        wr = jnp.pad(jnp.concatenate([w_group_router[l], w_expert_router[l]], axis=1).astype(F32),
                     ((0, 0), (0, LANES - N_GROUPS - N_EXPERTS)))
        wr_hi = wr.astype(BF16)
        wr = jnp.concatenate([wr_hi, (wr - wr_hi.astype(F32)).astype(BF16)], axis=1)
        brt = jnp.pad(jnp.concatenate([b_group_router[l], b_expert_router[l]]).astype(F32),
                      (0, LANES - N_GROUPS - N_EXPERTS)).reshape(1, LANES)
        x2, hs, route, cnt = _token(
            xf, oa, ob, gates, w_branch[l, 0].astype(BF16), w_branch[l, 1].astype(BF16),
            w_mix_out[l].astype(BF16), row(norm_x_g[l]), w_xq[l].astype(BF16), km_all[l], vm_all[l],
            w_xo[l].astype(BF16), row(norm_ffn_g[l]), wr, brt, batch)

        plan = _chunk_plan(cnt[:, 0, :N_EXPERTS].astype(I32), n_tiles)
        e3 = lambda w: w.reshape((depth * N_EXPERTS,) + w.shape[3:])
        ys = _experts(hs, *plan, e3(w_exp_gate), e3(w_exp_up), e3(w_exp_down), l)
        xf, moe = x2, (route, ys)

    return _combine(x2, route, ys, row(final_norm_g)).reshape(batch, seq, d)
```

```python
import functools

import numpy as np
import jax
import jax.numpy as jnp
from jax import lax
from jax.experimental import pallas as pl
from jax.experimental.pallas import tpu as pltpu

F32 = jnp.float32
BF16 = jnp.bfloat16
I32 = jnp.int32

D_MODEL = 1024
CHUNK = 64
EPS = 1e-6
A_HEADS = 8
A_HEAD_DIM = 64
A_WIDTH = 512
A_LEFT_CHUNKS = 8
A_MAX_REL = 256
B_HEADS = 4
B_KEY_DIM = 64
B_VAL_DIM = 128
B_KEY_WIDTH = 256
B_VAL_WIDTH = 512
B_GATE_RANK = 16
B_GATE_TAU = 16.0
X_HEADS = 4
X_HEAD_DIM = 256
N_GROUPS = 4
EXPERTS_PER_GROUP = 8
N_EXPERTS = N_GROUPS * EXPERTS_PER_GROUP
EXPERT_FF = 256

LANES = 128
ROW_TILE = 256
CHUNKS_PER_TILE = ROW_TILE // CHUNK
BAND_TILES = A_LEFT_CHUNKS // CHUNKS_PER_TILE + 1
BAND_KEYS = BAND_TILES * ROW_TILE
BIAS_PERIOD = 1024
LOG_CHUNK = 6
N_LEVELS = LOG_CHUNK
EXP_ROWS = (2 + N_LEVELS) * CHUNK
CHUNK_ROWS = 16
TOKEN_TILES_PER_STEP = 2
INPROJ_ROWS = 512
COMBINE_TILES_PER_STEP = 2
GLA_TILES_PER_STEP = 4
BAND_TILES_PER_STEP = 4
EXPERT_ROWS = 512
GATHER_AHEAD = 3
TILE_CHUNKS = EXPERT_ROWS // CHUNK_ROWS
LOCAL_CHUNKS = 2 * ROW_TILE // CHUNK_ROWS + N_EXPERTS
LOCAL_ROWS = LOCAL_CHUNKS * CHUNK_ROWS
assert (2 * ROW_TILE + N_EXPERTS * (CHUNK_ROWS - 1)) // CHUNK_ROWS < LOCAL_CHUNKS
SORT_WIDTH = D_MODEL + LANES
NEG = -1e30
LOG2E = 1.4426950408889634
VMEM_LIMIT = 56 * 1024 * 1024


def _params(*sem):
    return pltpu.CompilerParams(dimension_semantics=sem, vmem_limit_bytes=VMEM_LIMIT)


def _rms(x, g):
    return x * lax.rsqrt(jnp.mean(x * x, axis=-1, keepdims=True) + EPS) * g


def _dot(a, b):
    return jnp.dot(a, b, preferred_element_type=F32)


def _dot_nt(a, b):
    return lax.dot_general(a, b, (((1,), (1,)), ((), ())), preferred_element_type=F32)


def _dot_tn(a, b):
    return lax.dot_general(a, b, (((0,), (0,)), ((), ())), preferred_element_type=F32)


def _memkv_kernel(mem_ref, g_ref, w_ref, k_ref, v_ref):
    mn = _rms(mem_ref[0], g_ref[...]).astype(BF16)
    kv = _dot(mn, w_ref[0].astype(BF16))
    k_ref[0, 0] = kv[:, :D_MODEL].astype(BF16)
    v_ref[0, 0] = kv[:, D_MODEL:].astype(BF16)


def _memkv(mem, g, w_xkv):
    depth = w_xkv.shape[0]
    b, m, d = mem.shape
    out = jax.ShapeDtypeStruct((depth, b, m, d), BF16)
    return pl.pallas_call(
        _memkv_kernel,
        grid=(depth, b),
        in_specs=[pl.BlockSpec((1, m, d), lambda l, i: (i, 0, 0)),
                  pl.BlockSpec((1, d), lambda l, i: (0, 0)),
                  pl.BlockSpec((1, d, 2 * d), lambda l, i: (l, 0, 0))],
        out_specs=[pl.BlockSpec((1, 1, m, d), lambda l, i: (l, i, 0, 0)),
                   pl.BlockSpec((1, 1, m, d), lambda l, i: (l, i, 0, 0))],
        out_shape=[out, out],
        compiler_params=_params("arbitrary", "arbitrary"),
        name="memkv",
    )(mem, g.reshape(1, d), w_xkv)


_R_AQ, _R_AK, _R_AV = 0, 512, 1024
_R_BQ, _R_BK, _R_BV = 1536, 1792, 2048
_R_ALPHA, _R_BR, _R_GATE, _R_END = 2560, 2576, 3088, 5136
_W_PIECE = 512
assert (_R_END - _R_GATE) % _W_PIECE == 0


def _inproj_kernel(layer, fused, *refs):
    if fused:
        x_ref, route_ref, ys_ref, g_ref, wt_hbm, wal2_ref, bal_ref = refs[:7]
        refs = refs[7:]
        x3_ref, refs = refs[0], refs[1:]
    else:
        x_ref, g_ref, wt_hbm, wal2_ref, bal_ref = refs[:5]
        refs = refs[5:]
    (aq_ref, ak_ref, av_ref, bq_ref, bk_ref, bv_ref, lga_ref, br_ref, gate_ref,
     wt_bf, wal2_bf, stage, sem) = refs

    @pl.when(pl.program_id(0) == 0)
    def _():
        pieces = [(c, min(_W_PIECE, _R_END - c)) for c in range(0, _R_END, _W_PIECE)]

        def piece_copy(p):
            c, n = pieces[p]
            return pltpu.make_async_copy(wt_hbm.at[layer, pl.ds(c, n), :], stage.at[p % 2, pl.ds(0, n), :],
                                         sem.at[p % 2])

        piece_copy(0).start()
        for p, (c, n) in enumerate(pieces):
            if p + 1 < len(pieces):
                piece_copy(p + 1).start()
            piece_copy(p).wait()
            wt_bf[c:c + n, :] = stage[p % 2, 0:n, :].astype(BF16)
        wal2_bf[...] = jnp.concatenate(
            [wal2_ref[0].astype(BF16), jnp.zeros((LANES - B_GATE_RANK, B_KEY_WIDTH), BF16)], axis=0)

    if fused:
        r = lax.broadcasted_iota(I32, (ROW_TILE, LOCAL_ROWS), 1).astype(F32)
        tiles = []
        for t in range(INPROJ_ROWS // ROW_TILE):
            rows = slice(t * ROW_TILE, (t + 1) * ROW_TILE)
            pos = route_ref[rows, :]
            sel = jnp.where(r == pos[:, 0:1], 1.0, jnp.where(r == pos[:, 1:2], 1.0, 0.0)).astype(BF16)
            ys = ys_ref[t * LOCAL_CHUNKS:(t + 1) * LOCAL_CHUNKS].reshape(LOCAL_ROWS, D_MODEL)
            tiles.append(x_ref[rows, :] + _dot(sel, ys))
        x = jnp.concatenate(tiles, axis=0)
        x3_ref[...] = x
    else:
        x = x_ref[...]
    h = _rms(x, g_ref[...]).astype(BF16)

    def mm(lo, hi):
        return _dot_nt(h, wt_bf[lo:hi, :])

    aq_ref[...] = (mm(_R_AQ, _R_AK) * (A_HEAD_DIM ** -0.5 * LOG2E)).astype(BF16)
    ak_ref[...] = mm(_R_AK, _R_AV).astype(BF16)
    av_ref[...] = mm(_R_AV, _R_BQ).astype(BF16)
    bq_ref[...] = (mm(_R_BQ, _R_BK) * (B_KEY_DIM ** -0.5)).astype(BF16)
    bk_ref[...] = mm(_R_BK, _R_BV).astype(BF16)
    bv_ref[...] = mm(_R_BV, _R_ALPHA).astype(BF16)
    r = mm(_R_BR, _R_GATE)
    br_ref[...] = (r * jax.nn.sigmoid(r)).astype(BF16)
    for c in range(_R_GATE, _R_END, _W_PIECE):
        gate_ref[:, c - _R_GATE:c - _R_GATE + _W_PIECE] = jax.nn.sigmoid(mm(c, c + _W_PIECE)).astype(BF16)
    z = _dot(mm(_R_ALPHA, _R_ALPHA + LANES).astype(BF16), wal2_bf[...]) + bal_ref[...]
    lga_ref[...] = (jnp.minimum(z, 0.0) - jnp.log(1.0 + jnp.exp(-jnp.abs(z)))) * (1.0 / B_GATE_TAU)


def _inproj(x, g, w_in, w_al2, b_al, layer, moe=None):
    t, d = x.shape
    assert w_in.shape[2] == _R_END
    row = lambda w: pl.BlockSpec((INPROJ_ROWS, w), lambda i: (i, 0))
    full = lambda a: pl.BlockSpec(a.shape, lambda i: (0,) * a.ndim)
    sds = lambda w, dt: jax.ShapeDtypeStruct((t, w), dt)
    widths = [(A_WIDTH, BF16), (A_WIDTH, BF16), (A_WIDTH, BF16), (B_KEY_WIDTH, BF16), (B_KEY_WIDTH, BF16),
              (B_VAL_WIDTH, BF16), (B_KEY_WIDTH, F32), (B_VAL_WIDTH, BF16), (_R_END - _R_GATE, BF16)]
    fused = moe is not None
    moe_specs, moe_out_specs, moe_out_shape = [], [], []
    if fused:
        n_chunks = INPROJ_ROWS // ROW_TILE * LOCAL_CHUNKS
        moe_specs = [row(LANES), pl.BlockSpec((n_chunks, CHUNK_ROWS, d), lambda i: (i, 0, 0))]
        moe_out_specs, moe_out_shape = [row(d)], [sds(d, F32)]
    return pl.pallas_call(
        functools.partial(_inproj_kernel, layer, fused),
        grid=(t // INPROJ_ROWS,),
        in_specs=[row(d)] + moe_specs + [full(g), pl.BlockSpec(memory_space=pl.ANY),
                                         pl.BlockSpec((1,) + w_al2.shape[1:], lambda i: (layer, 0, 0)),
                                         full(b_al)],
        out_specs=moe_out_specs + [row(w) for w, _ in widths],
        out_shape=moe_out_shape + [sds(w, dt) for w, dt in widths],
        scratch_shapes=[pltpu.VMEM((_R_END, d), BF16), pltpu.VMEM((LANES, B_KEY_WIDTH), BF16),
                        pltpu.VMEM((2, _W_PIECE, d), F32), pltpu.SemaphoreType.DMA((2,))],
        compiler_params=_params("arbitrary"),
        name="inproj",
    )(x, *(moe or ()), g, jnp.swapaxes(w_in, 1, 2), w_al2, b_al)


def _band_kernel(q_ref, *refs):
    n_win = BAND_TILES_PER_STEP + BAND_TILES - 1
    k_refs, v_refs = refs[:n_win], refs[n_win:2 * n_win]
    u_ref, o_ref, bias_ref = refs[2 * n_win:]
    i = pl.program_id(1)
    lane = lax.broadcasted_iota(I32, (1, LANES), 1)
    low = lane < A_HEAD_DIM
    ones = jnp.ones((BAND_KEYS, LANES), BF16)

    @pl.when((pl.program_id(0) == 0) & (i == 0))
    def _():
        cq = lax.broadcasted_iota(I32, (ROW_TILE, BAND_KEYS), 0) >> LOG_CHUNK
        ck = lax.broadcasted_iota(I32, (ROW_TILE, BAND_KEYS), 1) >> LOG_CHUNK
        valid = (ck >= cq) & (ck <= cq + A_LEFT_CHUNKS)
        for h in range(A_HEADS):
            rows = jnp.broadcast_to(u_ref[h:h + 1, :], (ROW_TILE, BIAS_PERIOD))
            rows = pltpu.roll(rows, BIAS_PERIOD - (ROW_TILE - 1), 1, stride=1, stride_axis=0)
            bias_ref[h // 2, (h % 2) * ROW_TILE:(h % 2 + 1) * ROW_TILE, :] = jnp.where(
                valid, rows[:, :BAND_KEYS], NEG)

    def attend(tile, n_missing):
        rows = pl.ds(tile * ROW_TILE, ROW_TILE)
        window = range(tile, tile + BAND_TILES)
        for p in range(A_HEADS // 2):
            sl = slice(p * LANES, (p + 1) * LANES)
            qp = q_ref[rows, sl]
            zero = jnp.zeros_like(qp)
            q2 = jnp.concatenate([jnp.where(low, qp, zero), jnp.where(low, zero, qp)], axis=0)
            kp = jnp.concatenate([k_refs[j][:, sl] for j in window], axis=0)
            vp = jnp.concatenate([v_refs[j][:, sl] for j in window], axis=0)
            s = _dot_nt(q2, kp) + bias_ref[p]
            if n_missing:
                col = lax.broadcasted_iota(I32, (1, BAND_KEYS), 1)
                s = s + jnp.where(col < n_missing * ROW_TILE, NEG, 0.0).astype(F32)
            pe = jnp.exp2(s - jnp.max(s, axis=-1, keepdims=True)).astype(BF16)
            o2 = _dot(pe, jnp.concatenate([vp, ones], axis=1))
            o = o2[:, :LANES] * (1.0 / o2[:, LANES:])
            o_ref[rows, sl] = jnp.where(low, o[:ROW_TILE], o[ROW_TILE:]).astype(BF16)

    @pl.when(i > 0)
    def _():
        for tile in range(BAND_TILES_PER_STEP):
            attend(tile, 0)

    @pl.when(i == 0)
    def _():
        for tile in range(BAND_TILES_PER_STEP):
            attend(tile, max(BAND_TILES - 1 - tile, 0))


def _band_bias_vector(rel_table):
    h = rel_table.shape[0]
    tab = rel_table.astype(F32) * LOG2E
    shift = A_LEFT_CHUNKS * CHUNK + ROW_TILE - 1
    n_far = shift - A_MAX_REL + 1
    span = ROW_TILE + BAND_KEYS - 1
    assert span - 1 - shift <= A_MAX_REL and span <= BIAS_PERIOD
    u = jnp.concatenate([jnp.broadcast_to(tab[:, 2 * A_MAX_REL:], (h, n_far)),
                         tab[:, 2 * A_MAX_REL - 1:2 * A_MAX_REL - 1 - (span - n_far):-1]], axis=1)
    return jnp.pad(u, ((0, 0), (0, BIAS_PERIOD - span)))


def _band_attention(q, k, v, u, batch):
    t, w = q.shape
    n = BAND_TILES_PER_STEP
    nb = t // batch // (n * ROW_TILE)
    step = pl.BlockSpec((n * ROW_TILE, w), lambda b, i: (b * nb + i, 0))
    tile = lambda j: pl.BlockSpec(
        (ROW_TILE, w), lambda b, i: (n * b * nb + jnp.maximum(n * i + j - (BAND_TILES - 1), 0), 0))
    window = [tile(j) for j in range(n + BAND_TILES - 1)]
    return pl.pallas_call(
        _band_kernel,
        grid=(batch, nb),
        in_specs=[step] + window + window + [pl.BlockSpec(u.shape, lambda b, i: (0, 0))],
        out_specs=step,
        out_shape=jax.ShapeDtypeStruct((t, w), BF16),
        scratch_shapes=[pltpu.VMEM((A_HEADS // 2, 2 * ROW_TILE, BAND_KEYS), F32)],
        compiler_params=_params("arbitrary", "arbitrary"),
        name="band_attn",
    )(q, *([k] * len(window)), *([v] * len(window)), u)


def _gla_constants():
    c = CHUNK
    t = np.arange(c)[:, None]
    r = np.arange(c)[None, :]
    mats = [(r <= t), (r > t)]
    lvl = np.full((c, c), -1, np.int32)
    lvl[np.arange(c), np.arange(c)] = N_LEVELS
    for l in range(N_LEVELS):
        m = (c // 2) >> l
        mid = (t // (2 * m)) * (2 * m) + m
        upper = t >= mid
        mats.append(np.where(upper, (r >= mid) & (r <= t), (r > t) & (r < mid)))
        s = r
        same = (s // (2 * m)) == (t // (2 * m))
        lvl[np.asarray(same & upper & (s < mid))] = l
    eye = np.eye(CHUNKS_PER_TILE)
    mexp = np.concatenate([np.kron(eye, m) for m in mats], axis=0).astype(np.float32)
    lvl = np.tile(lvl, (1, B_HEADS))
    return jnp.asarray(mexp, BF16), jnp.asarray(lvl, I32)


def _gla_kernel(q_ref, k_ref, v_ref, g_ref, r_ref, gn_ref, mexp_ref, lvl_ref, o_ref, s_ref):
    @pl.when(pl.program_id(1) == 0)
    def _():
        s_ref[...] = jnp.zeros_like(s_ref)

    kw = B_KEY_WIDTH
    ri = lax.broadcasted_iota(I32, (kw, kw), 0) >> LOG_CHUNK
    ci = lax.broadcasted_iota(I32, (kw, kw), 1) >> LOG_CHUNK
    bd = ri == ci
    head_ind = jnp.where(bd, 1.0, 0.0).astype(BF16)
    ri2 = lax.broadcasted_iota(I32, (kw, 2 * kw), 0) >> LOG_CHUNK
    ci2 = (lax.broadcasted_iota(I32, (kw, 2 * kw), 1) & (kw - 1)) >> LOG_CHUNK
    bd2 = ri2 == ci2
    lvl = lvl_ref[...]
    split_row = lax.broadcasted_iota(I32, (CHUNK_ROWS, kw), 0)
    ones = jnp.ones((CHUNK_ROWS, LANES), BF16)
    zero_b = jnp.zeros((kw, kw), BF16)
    chunks = [slice(c * CHUNK, (c + 1) * CHUNK) for c in range(CHUNKS_PER_TILE)]

    def head_blocks(x):
        return jnp.where(bd, jnp.concatenate([x] * B_HEADS, axis=0), zero_b)

    def prepare(tile):
        trows = pl.ds(tile * ROW_TILE, ROW_TILE)
        q = q_ref[trows, :].astype(F32)
        k = k_ref[trows, :].astype(F32)
        g = g_ref[trows, :]
        gb = g.astype(BF16)
        half = EXP_ROWS * CHUNKS_PER_TILE // 2
        w = jnp.exp(jnp.concatenate([_dot(mexp_ref[:half, :], gb), _dot(mexp_ref[half:, :], gb)], axis=0))
        qt = (q * w[0:ROW_TILE]).astype(BF16)
        kb = (k * w[ROW_TILE:2 * ROW_TILE]).astype(BF16)
        qk = (q * k).astype(BF16)

        attn = [jnp.zeros((CHUNK, kw), F32) for _ in chunks]
        for l in range(N_LEVELS):
            wl = w[(2 + l) * ROW_TILE:(3 + l) * ROW_TILE]
            qh = (q * wl).astype(BF16)
            kh = (k * wl).astype(BF16)
            for c, rows in enumerate(chunks):
                attn[c] = jnp.where(lvl == l, _dot_nt(qh[rows], head_blocks(kh[rows])), attn[c])

        out = []
        for c, rows in enumerate(chunks):
            a = jnp.where(lvl == N_LEVELS, _dot(qk[rows], head_ind), attn[c])
            v = v_ref[pl.ds(tile * ROW_TILE + c * CHUNK, CHUNK), :]
            vstack = jnp.concatenate([v[:, j * LANES:(j + 1) * LANES] for j in range(B_HEADS)], axis=0)
            kv = _dot_tn(head_blocks(kb[rows]), vstack)
            d = jnp.exp(jnp.sum(g[rows], axis=0, keepdims=True))
            d1 = d.astype(BF16).astype(F32)
            dp = jnp.where(split_row == 0, d1, jnp.where(split_row == 1, d - d1, 0.0)).astype(BF16)
            dcol = _dot_tn(dp, ones)
            out.append((a.astype(BF16), qt[rows], vstack, kv, dcol))
        return out

    prepared = [p for tile in range(GLA_TILES_PER_STEP) for p in prepare(tile)]

    s = s_ref[...]
    for c, (a, qtc, vstack, kv, dcol) in enumerate(prepared):
        rows = pl.ds(c * CHUNK, CHUNK)
        lhs = jnp.concatenate([a, qtc], axis=1)
        lhs = jnp.where(bd2, jnp.concatenate([lhs] * B_HEADS, axis=0), jnp.zeros((kw, 2 * kw), BF16))
        rhs = jnp.concatenate([vstack, s.astype(BF16)], axis=0)
        o = _dot(lhs, rhs)
        s = dcol * s + kv
        for j in range(B_HEADS):
            oj = o[j * CHUNK:(j + 1) * CHUNK]
            sl = slice(j * LANES, (j + 1) * LANES)
            y = oj * lax.rsqrt(jnp.mean(oj * oj, axis=-1, keepdims=True) + EPS) * gn_ref[...]
            o_ref[rows, sl] = (y * r_ref[rows, sl].astype(F32)).astype(BF16)
    s_ref[...] = s


def _gla(q, k, v, g, r, gn, batch):
    t = q.shape[0]
    nb = t // batch // (GLA_TILES_PER_STEP * ROW_TILE)
    mexp, lvl = _gla_constants()
    cur = lambda b, i: (b * nb + i, 0)
    blk = lambda w: pl.BlockSpec((GLA_TILES_PER_STEP * ROW_TILE, w), cur)
    full = lambda a: pl.BlockSpec(a.shape, lambda b, i: (0,) * a.ndim)
    return pl.pallas_call(
        _gla_kernel,
        grid=(batch, nb),
        in_specs=[blk(B_KEY_WIDTH), blk(B_KEY_WIDTH), blk(B_VAL_WIDTH), blk(B_KEY_WIDTH), blk(B_VAL_WIDTH),
                  full(gn), full(mexp), full(lvl)],
        out_specs=blk(B_VAL_WIDTH),
        out_shape=jax.ShapeDtypeStruct((t, B_VAL_WIDTH), BF16),
        scratch_shapes=[pltpu.VMEM((B_KEY_WIDTH, B_VAL_DIM), F32)],
        compiler_params=_params("arbitrary", "arbitrary"),
        name="gla",
    )(q, k, v, g, r, gn, mexp, lvl)


def _token_kernel(x_ref, oa_ref, ob_ref, gate_ref, wb0_ref, wb1_ref, wmix_ref, gx_ref, wq_ref,
                  km_ref, vm_ref, wo_ref, gf_ref, wr_ref, br_ref, ltri_ref, utri_ref,
                  x2_ref, hs_ref, route_ref, cnt_ref):
    ma = _dot(oa_ref[...], wb0_ref[...])
    mb = _dot(ob_ref[...], wb1_ref[...])
    merged = (gate_ref[:, :D_MODEL].astype(F32) * ma + gate_ref[:, D_MODEL:].astype(F32) * mb).astype(BF16)
    x1 = x_ref[...] + _dot(merged, wmix_ref[...])

    h2 = _rms(x1, gx_ref[...]).astype(BF16)
    qx = (_dot(h2, wq_ref[...]) * (X_HEAD_DIM ** -0.5)).astype(BF16)
    heads = []
    for h in range(X_HEADS):
        sl = slice(h * X_HEAD_DIM, (h + 1) * X_HEAD_DIM)
        s = _dot_nt(qx[:, sl], km_ref[0, :, sl])
        m = jnp.max(s, axis=-1, keepdims=True)
        pe = jnp.exp(s - m)
        l = jnp.sum(pe, axis=-1, keepdims=True)
        heads.append((_dot(pe.astype(BF16), vm_ref[0, :, sl]) * (1.0 / l)).astype(BF16))
    x2 = x1 + _dot(jnp.concatenate(heads, axis=1), wo_ref[...])
    x2_ref[...] = x2

    h3 = _rms(x2, gf_ref[...])

    h3_hi = h3.astype(BF16)
    h3_lo = (h3 - h3_hi.astype(F32)).astype(BF16)
    hw = _dot(h3_hi, wr_ref[...])
    logits = hw[:, :LANES] + hw[:, LANES:] + _dot(h3_lo, wr_ref[:, :LANES]) + br_ref[...]
    oh0, oh1, g0, g1 = _route(logits)
    for h in range(TOKEN_TILES_PER_STEP):
        rows = slice(h * ROW_TILE, (h + 1) * ROW_TILE)
        chunks = pl.ds(h * LOCAL_CHUNKS, LOCAL_CHUNKS)
        _sort_tile(oh0[rows], oh1[rows], g0[rows], g1[rows], h3_hi[rows], ltri_ref, utri_ref,
                   hs_ref.at[chunks], route_ref.at[pl.ds(h * ROW_TILE, ROW_TILE)], cnt_ref.at[h])


def _route(logits):
    lane = lax.broadcasted_iota(I32, logits.shape, 1).astype(F32)
    big = jnp.float32(LANES)
    gl = jnp.where(lane < N_GROUPS, logits, NEG)
    gmax = jnp.max(gl, axis=-1, keepdims=True)
    gidx = jnp.min(jnp.where(gl == gmax, lane, big), axis=-1, keepdims=True)
    g_w = 1.0 / jnp.sum(jnp.exp(gl - gmax), axis=-1, keepdims=True)
    lo = N_GROUPS + EXPERTS_PER_GROUP * gidx
    el = jnp.where((lane >= lo) & (lane < lo + EXPERTS_PER_GROUP), logits, NEG)
    v1 = jnp.max(el, axis=-1, keepdims=True)
    i1 = jnp.min(jnp.where(el == v1, lane, big), axis=-1, keepdims=True)
    el2 = jnp.where(lane == i1, NEG, el)
    v2 = jnp.max(el2, axis=-1, keepdims=True)
    i2 = jnp.min(jnp.where(el2 == v2, lane, big), axis=-1, keepdims=True)
    e21 = jnp.exp(v2 - v1)
    w1 = g_w / (1.0 + e21)

    def gate_cols(w):
        hi = w.astype(BF16).astype(F32)
        return jnp.where(lane == 0, hi, jnp.where(lane == 1, w - hi, 0.0)).astype(BF16)

    oh0 = jnp.where(lane == i1 - N_GROUPS, 1.0, 0.0)
    oh1 = jnp.where(lane == i2 - N_GROUPS, 1.0, 0.0)
    return oh0, oh1, gate_cols(w1), gate_cols(w1 * e21)


def _sort_tile(oh0, oh1, g0, g1, h3_hi, ltri_ref, utri_ref, hs_ref, route_ref, cnt_ref):
    lane = lax.broadcasted_iota(I32, oh0.shape, 1)
    oh = oh0 + oh1
    nch = jnp.floor((jnp.sum(oh, axis=0, keepdims=True) + (CHUNK_ROWS - 1)) * (1.0 / CHUNK_ROWS))
    nch8 = jnp.broadcast_to(nch, (8, LANES))
    start = _dot(nch8.astype(BF16), utri_ref[...])[0:1] * CHUNK_ROWS
    rank = _dot(ltri_ref[...], oh.astype(BF16))
    row = start + rank
    pos0 = jnp.sum(row * oh0, axis=-1, keepdims=True)
    pos1 = jnp.sum(row * oh1, axis=-1, keepdims=True)
    route = jnp.where(lane == 0, pos0, jnp.where(lane == 1, pos1, 0.0))
    route_t = jnp.transpose(route)
    r = lax.broadcasted_iota(I32, (LOCAL_ROWS, ROW_TILE), 0).astype(F32)
    p0 = jnp.where(r == route_t[0:1, :], 1.0, 0.0).astype(BF16)
    p1 = jnp.where(r == route_t[1:2, :], 1.0, 0.0).astype(BF16)
    sorted_rows = jnp.concatenate([_dot(p0 + p1, h3_hi), _dot(p0, g0) + _dot(p1, g1)], axis=1)
    hs_ref[...] = sorted_rows.astype(BF16).reshape(hs_ref.shape)
    route_ref[...] = route
    cnt_ref[...] = nch8


def _token(x, oa, ob, gates, wb0, wb1, wmix, gx, wq, km, vm, wo, gf, wr, br, batch):
    t, d = x.shape
    n = TOKEN_TILES_PER_STEP
    nb = t // batch // (n * ROW_TILE)
    nt = t // ROW_TILE
    ltri = jnp.asarray(np.tril(np.ones((ROW_TILE, ROW_TILE), np.float32), -1), BF16)
    utri = jnp.asarray(np.triu(np.ones((LANES, LANES), np.float32), 1), BF16)
    cur = lambda b, i: (b * nb + i, 0)
    cur3 = lambda b, i: (b * nb + i, 0, 0)
    blk = lambda w: pl.BlockSpec((n * ROW_TILE, w), cur)
    full = lambda a: pl.BlockSpec(a.shape, lambda b, i: (0,) * a.ndim)
    mem = pl.BlockSpec((1,) + km.shape[1:], lambda b, i: (b, 0, 0))
    return pl.pallas_call(
        _token_kernel,
        grid=(batch, nb),
        in_specs=[blk(d), blk(A_WIDTH), blk(B_VAL_WIDTH), blk(2 * d), full(wb0), full(wb1), full(wmix),
                  full(gx), full(wq), mem, mem, full(wo), full(gf), full(wr), full(br), full(ltri), full(utri)],
        out_specs=[blk(d), pl.BlockSpec((n * LOCAL_CHUNKS, CHUNK_ROWS, SORT_WIDTH), cur3),
                   blk(LANES), pl.BlockSpec((n, 8, LANES), cur3)],
        out_shape=[jax.ShapeDtypeStruct((t, d), F32),
                   jax.ShapeDtypeStruct((nt * LOCAL_CHUNKS, CHUNK_ROWS, SORT_WIDTH), BF16),
                   jax.ShapeDtypeStruct((t, LANES), F32),
                   jax.ShapeDtypeStruct((nt, 8, LANES), F32)],
        compiler_params=_params("arbitrary", "arbitrary"),
        name="token",
    )(x, oa, ob, gates, wb0, wb1, wmix, gx, wq, km, vm, wo, gf, wr, br, ltri, utri)


def _expert_kernel(layer, te_ref, nu_ref, nv_ref, ch_ref, first_ref, nxt_ref,
                   hs_hbm, wg_hbm, wu_hbm, wd_hbm, ys_hbm,
                   xbuf, ybuf, wg_st, wu_st, wd_st, wgu_bf, wd_bf, gsem, ssem, wsem):
    n_used = nu_ref[0]
    ring = GATHER_AHEAD + 1
    last_tile = te_ref.shape[0] - 1

    def weights(expert, s, start):
        for src, dst in ((wg_hbm, wg_st), (wu_hbm, wu_st), (wd_hbm, wd_st)):
            cp = pltpu.make_async_copy(src.at[layer * N_EXPERTS + expert], dst.at[s], wsem.at[s])
            cp.start() if start else cp.wait()

    def for_chunks(tile, fn):
        nv = nv_ref[tile]

        @pl.when(nv == TILE_CHUNKS)
        def _():
            for c in range(TILE_CHUNKS):
                fn(c)

        @pl.when(nv != TILE_CHUNKS)
        def _():
            def body(c, carry):
                fn(c)
                return carry

            lax.fori_loop(0, nv, body, 0)

    def gather(step, start):
        tile = jnp.minimum(step, last_tile)
        s = lax.rem(step, ring)
        for c in range(TILE_CHUNKS):
            cp = pltpu.make_async_copy(hs_hbm.at[ch_ref[tile * TILE_CHUNKS + c]], xbuf.at[s, c], gsem.at[s])
            cp.start() if start else cp.wait()

    def scatter(tile, s, start):
        def one(c):
            cp = pltpu.make_async_copy(ybuf.at[s, c], ys_hbm.at[ch_ref[tile * TILE_CHUNKS + c]], ssem.at[s])
            cp.start(priority=1) if start else cp.wait()

        for_chunks(tile, one)

    weights(te_ref[0], 0, True)
    for step in range(GATHER_AHEAD):
        gather(step, True)

    def tile_body(i, run):
        slot = lax.rem(i, 2)
        gather(i, False)

        @pl.when(i >= 2)
        def _():
            scatter(i - 2, slot, False)

        @pl.when(first_ref[i] == 1)
        def _():
            s = lax.rem(run, 2)
            weights(te_ref[i], s, False)

            @pl.when(nxt_ref[i] >= 0)
            def _():
                weights(nxt_ref[i], 1 - s, True)

            wgu_bf[:, :EXPERT_FF] = wg_st[s].astype(BF16)
            wgu_bf[:, EXPERT_FF:] = wu_st[s].astype(BF16)
            wd_bf[...] = wd_st[s].astype(BF16)

        xg = xbuf[lax.rem(i, ring)].reshape(EXPERT_ROWS, SORT_WIDTH)
        hgu = _dot(xg[:, :D_MODEL], wgu_bf[...])
        gather(i + GATHER_AHEAD, True)
        hg, hu = hgu[:, :EXPERT_FF], hgu[:, EXPERT_FF:]
        hid = (hg * jax.nn.sigmoid(hg) * hu).astype(BF16)
        g = xg[:, D_MODEL:].astype(F32)
        y = ((g[:, 0:1] + g[:, 1:2]) * _dot(hid, wd_bf[...])).astype(BF16)
        y = jnp.concatenate([y, jnp.zeros((EXPERT_ROWS, LANES), BF16)], axis=1)
        ybuf[slot] = y.reshape(TILE_CHUNKS, CHUNK_ROWS, SORT_WIDTH)
        scatter(i, slot, True)
        return run + first_ref[i]

    lax.fori_loop(0, n_used, tile_body, jnp.int32(0))

    last = n_used - 1
    for ahead in range(1, GATHER_AHEAD + 1):
        gather(last + ahead, False)
    scatter(last, lax.rem(last, 2), False)

    @pl.when(last >= 1)
    def _():
        scatter(last - 1, lax.rem(last - 1, 2), False)


def _experts(hs, tile_expert, n_used, n_valid, chunks, run_first, run_next, wg, wu, wd, layer):
    anyspace = pl.BlockSpec(memory_space=pl.ANY)
    grid_spec = pltpu.PrefetchScalarGridSpec(
        num_scalar_prefetch=6,
        grid=(1,),
        in_specs=[anyspace] * 4,
        out_specs=anyspace,
        scratch_shapes=[pltpu.VMEM((GATHER_AHEAD + 1, TILE_CHUNKS, CHUNK_ROWS, SORT_WIDTH), BF16),
                        pltpu.VMEM((2, TILE_CHUNKS, CHUNK_ROWS, SORT_WIDTH), BF16),
                        pltpu.VMEM((2, D_MODEL, EXPERT_FF), F32), pltpu.VMEM((2, D_MODEL, EXPERT_FF), F32),
                        pltpu.VMEM((2, EXPERT_FF, D_MODEL), F32),
                        pltpu.VMEM((D_MODEL, 2 * EXPERT_FF), BF16), pltpu.VMEM((EXPERT_FF, D_MODEL), BF16),
                        pltpu.SemaphoreType.DMA((GATHER_AHEAD + 1,)), pltpu.SemaphoreType.DMA((2,)),
                        pltpu.SemaphoreType.DMA((2,))],
    )
    return pl.pallas_call(
        functools.partial(_expert_kernel, layer),
        grid_spec=grid_spec,
        out_shape=jax.ShapeDtypeStruct(hs.shape, BF16),
        input_output_aliases={6: 0},
        compiler_params=_params("arbitrary"),
        name="experts",
    )(tile_expert, n_used, n_valid, chunks, run_first, run_next, hs, wg, wu, wd)


def _combine_kernel(x_ref, route_ref, ys_ref, gfin_ref, o_ref):
    r = lax.broadcasted_iota(I32, (ROW_TILE, LOCAL_ROWS), 1).astype(F32)
    for t in range(COMBINE_TILES_PER_STEP):
        rows = slice(t * ROW_TILE, (t + 1) * ROW_TILE)
        pos = route_ref[rows, :]
        sel = jnp.where(r == pos[:, 0:1], 1.0, jnp.where(r == pos[:, 1:2], 1.0, 0.0)).astype(BF16)
        ys = ys_ref[t * LOCAL_CHUNKS:(t + 1) * LOCAL_CHUNKS].reshape(LOCAL_ROWS, D_MODEL)
        o_ref[rows, :] = _rms(x_ref[rows, :] + _dot(sel, ys), gfin_ref[...])


def _combine(x2, route, ys, gfin):
    t, d = x2.shape
    n = COMBINE_TILES_PER_STEP
    return pl.pallas_call(
        _combine_kernel,
        grid=(t // (n * ROW_TILE),),
        in_specs=[pl.BlockSpec((n * ROW_TILE, d), lambda i: (i, 0)),
                  pl.BlockSpec((n * ROW_TILE, LANES), lambda i: (i, 0)),
                  pl.BlockSpec((n * LOCAL_CHUNKS, CHUNK_ROWS, d), lambda i: (i, 0, 0)),
                  pl.BlockSpec((1, d), lambda i: (0, 0))],
        out_specs=pl.BlockSpec((n * ROW_TILE, d), lambda i: (i, 0)),
        out_shape=jax.ShapeDtypeStruct((t, d), F32),
        compiler_params=_params("arbitrary"),
        name="combine",
    )(x2, route, ys, gfin)


def _chunk_plan(nch, n_tiles):
    nt = nch.shape[0]
    local_start = jnp.cumsum(nch, axis=1) - nch
    cum = jnp.cumsum(nch, axis=0)
    total = cum[-1]
    tiles = (total + TILE_CHUNKS - 1) // TILE_CHUNKS
    tile_end = jnp.cumsum(tiles)
    n_used = tile_end[-1:]
    tile_ids = jnp.arange(n_tiles, dtype=I32)
    tile_expert = jnp.minimum(jnp.sum((tile_end[None, :] <= tile_ids[:, None]).astype(I32), axis=1),
                              N_EXPERTS - 1)
    sel = (tile_expert[:, None] == jnp.arange(N_EXPERTS, dtype=I32)[None, :]).astype(I32)
    pick = lambda table: jnp.sum(sel[:, :, None] * table.T[None, :, :], axis=1)
    first_tile = jnp.sum(sel * (tile_end - tiles)[None, :], axis=1)
    slot = (tile_ids - first_tile)[:, None] * TILE_CHUNKS + jnp.arange(TILE_CHUNKS, dtype=I32)[None, :]
    valid = (slot < jnp.sum(sel * total[None, :], axis=1)[:, None]) & (tile_ids < n_used)[:, None]
    src_tile = jnp.sum((pick(cum)[:, None, :] <= slot[:, :, None]).astype(I32), axis=2)
    src_tile = jnp.minimum(src_tile, nt - 1)
    at = (src_tile[:, :, None] == jnp.arange(nt, dtype=I32)[None, None, :]).astype(I32)
    before = jnp.sum(at * pick(cum - nch)[:, None, :], axis=2)
    start = jnp.sum(at * pick(local_start)[:, None, :], axis=2)
    chunk = jnp.where(valid, src_tile * LOCAL_CHUNKS + start + slot - before, LOCAL_CHUNKS - 1)
    used = tile_ids < n_used
    run_first = ((tile_ids == first_tile) & used).astype(I32)
    run_end = jnp.sum(sel * tile_end[None, :], axis=1)
    next_expert = jnp.sum((run_end[:, None] == tile_ids[None, :]).astype(I32) * tile_expert[None, :], axis=1)
    run_next = jnp.where(used & (run_end < n_used), next_expert, -1)
    return tile_expert, n_used, jnp.sum(valid.astype(I32), axis=1), chunk.reshape(-1), run_first, run_next


def kernel(x, mem, norm_mix_g, w_in, rel_bias, gla_w_alpha, gla_b_alpha, gla_norm_g, w_branch, w_mix_out, norm_x_g, mem_norm_g, w_xq, w_xkv, w_xo, norm_ffn_g, w_group_router, b_group_router, w_expert_router, b_expert_router, w_exp_gate, w_exp_up, w_exp_down, final_norm_g):
    batch, seq, d = x.shape
    depth = w_in.shape[0]
    t = batch * seq
    step_tiles = max(TOKEN_TILES_PER_STEP, GLA_TILES_PER_STEP, BAND_TILES_PER_STEP, COMBINE_TILES_PER_STEP)
    assert d == D_MODEL and seq % (step_tiles * ROW_TILE) == 0 and seq % INPROJ_ROWS == 0
    nt = t // ROW_TILE
    n_tiles = nt * LOCAL_CHUNKS // TILE_CHUNKS + N_EXPERTS

    xf = x.reshape(t, d)
    km_all, vm_all = _memkv(mem, mem_norm_g, w_xkv)
    row = lambda a: a.reshape(1, -1).astype(F32)

    moe = None
    for l in range(depth):
        res = _inproj(xf, row(norm_mix_g[l]), w_in, gla_w_alpha, row(gla_b_alpha[l]), l, moe)
        if moe is not None:
            xf, res = res[0], res[1:]
        aq, ak, av, bq, bk, bv, lga, br, gates = res

        oa = _band_attention(aq, ak, av, _band_bias_vector(rel_bias[l]), batch)
        ob = _gla(bq, bk, bv, lga, br, row(gla_norm_g[l]), batch)

        wr = jnp.pad(jnp.concatenate([w_group_router[l], w_expert_router[l]], axis=1).astype(F32),
                     ((0, 0), (0, LANES - N_GROUPS - N_EXPERTS)))
        wr_hi = wr.astype(BF16)
        wr = jnp.concatenate([wr_hi, (wr - wr_hi.astype(F32)).astype(BF16)], axis=1)
        brt = jnp.pad(jnp.concatenate([b_group_router[l], b_expert_router[l]]).astype(F32),
                      (0, LANES - N_GROUPS - N_EXPERTS)).reshape(1, LANES)
        x2, hs, route, cnt = _token(
            xf, oa, ob, gates, w_branch[l, 0].astype(BF16), w_branch[l, 1].astype(BF16),
            w_mix_out[l].astype(BF16), row(norm_x_g[l]), w_xq[l].astype(BF16), km_all[l], vm_all[l],
            w_xo[l].astype(BF16), row(norm_ffn_g[l]), wr, brt, batch)

        plan = _chunk_plan(cnt[:, 0, :N_EXPERTS].astype(I32), n_tiles)
        e3 = lambda w: w.reshape((depth * N_EXPERTS,) + w.shape[3:])
        ys = _experts(hs, *plan, e3(w_exp_gate), e3(w_exp_up), e3(w_exp_down), l)
        xf, moe = x2, (route, ys)

    return _combine(x2, route, ys, row(final_norm_g)).reshape(batch, seq, d)
```

```python
import functools

import numpy as np
import jax
import jax.numpy as jnp
from jax import lax
from jax.experimental import pallas as pl
from jax.experimental.pallas import tpu as pltpu

F32 = jnp.float32
BF16 = jnp.bfloat16
I32 = jnp.int32

D_MODEL = 1024
CHUNK = 64
EPS = 1e-6
A_HEADS = 8
A_HEAD_DIM = 64
A_WIDTH = 512
A_LEFT_CHUNKS = 8
A_MAX_REL = 256
B_HEADS = 4
B_KEY_DIM = 64
B_VAL_DIM = 128
B_KEY_WIDTH = 256
B_VAL_WIDTH = 512
B_GATE_RANK = 16
B_GATE_TAU = 16.0
X_HEADS = 4
X_HEAD_DIM = 256
N_GROUPS = 4
EXPERTS_PER_GROUP = 8
N_EXPERTS = N_GROUPS * EXPERTS_PER_GROUP
EXPERT_FF = 256

LANES = 128
ROW_TILE = 256
CHUNKS_PER_TILE = ROW_TILE // CHUNK
BAND_TILES = A_LEFT_CHUNKS // CHUNKS_PER_TILE + 1
BAND_KEYS = BAND_TILES * ROW_TILE
BIAS_PERIOD = 1024
LOG_CHUNK = 6
N_LEVELS = LOG_CHUNK
EXP_ROWS = (2 + N_LEVELS) * CHUNK
CHUNK_ROWS = 16
TOKEN_TILES_PER_STEP = 2
INPROJ_ROWS = 512
COMBINE_TILES_PER_STEP = 4
GLA_TILES_PER_STEP = 4
BAND_TILES_PER_STEP = 4
EXPERT_ROWS = 512
GATHER_AHEAD = 4
TILE_CHUNKS = EXPERT_ROWS // CHUNK_ROWS
LOCAL_CHUNKS = 2 * ROW_TILE // CHUNK_ROWS + N_EXPERTS
LOCAL_ROWS = LOCAL_CHUNKS * CHUNK_ROWS
assert (2 * ROW_TILE + N_EXPERTS * (CHUNK_ROWS - 1)) // CHUNK_ROWS < LOCAL_CHUNKS
SORT_WIDTH = D_MODEL + LANES
NEG = -1e30
LOG2E = 1.4426950408889634
VMEM_LIMIT = 56 * 1024 * 1024


def _params(*sem):
    return pltpu.CompilerParams(dimension_semantics=sem, vmem_limit_bytes=VMEM_LIMIT)


def _rms(x, g):
    return x * lax.rsqrt(jnp.mean(x * x, axis=-1, keepdims=True) + EPS) * g


def _dot(a, b):
    return jnp.dot(a, b, preferred_element_type=F32)


def _dot_nt(a, b):
    return lax.dot_general(a, b, (((1,), (1,)), ((), ())), preferred_element_type=F32)


def _dot_tn(a, b):
    return lax.dot_general(a, b, (((0,), (0,)), ((), ())), preferred_element_type=F32)


def _memkv_kernel(mem_ref, g_ref, w_ref, k_ref, v_ref):
    mn = _rms(mem_ref[0], g_ref[...]).astype(BF16)
    kv = _dot(mn, w_ref[0].astype(BF16))
    k_ref[0, 0] = kv[:, :D_MODEL].astype(BF16)
    v_ref[0, 0] = kv[:, D_MODEL:].astype(BF16)


def _memkv(mem, g, w_xkv):
    depth = w_xkv.shape[0]
    b, m, d = mem.shape
    out = jax.ShapeDtypeStruct((depth, b, m, d), BF16)
    return pl.pallas_call(
        _memkv_kernel,
        grid=(depth, b),
        in_specs=[pl.BlockSpec((1, m, d), lambda l, i: (i, 0, 0)),
                  pl.BlockSpec((1, d), lambda l, i: (0, 0)),
                  pl.BlockSpec((1, d, 2 * d), lambda l, i: (l, 0, 0))],
        out_specs=[pl.BlockSpec((1, 1, m, d), lambda l, i: (l, i, 0, 0)),
                   pl.BlockSpec((1, 1, m, d), lambda l, i: (l, i, 0, 0))],
        out_shape=[out, out],
        compiler_params=_params("arbitrary", "arbitrary"),
        name="memkv",
    )(mem, g.reshape(1, d), w_xkv)


_R_AQ, _R_AK, _R_AV = 0, 512, 1024
_R_BQ, _R_BK, _R_BV = 1536, 1792, 2048
_R_ALPHA, _R_BR, _R_GATE, _R_END = 2560, 2576, 3088, 5136
_W_PIECE = 512
assert (_R_END - _R_GATE) % _W_PIECE == 0


def _inproj_kernel(layer, fused, *refs):
    if fused:
        x_ref, route_ref, ys_ref, g_ref, wt_hbm, wal2_ref, bal_ref = refs[:7]
        refs = refs[7:]
        x3_ref, refs = refs[0], refs[1:]
    else:
        x_ref, g_ref, wt_hbm, wal2_ref, bal_ref = refs[:5]
        refs = refs[5:]
    (aq_ref, ak_ref, av_ref, bq_ref, bk_ref, bv_ref, lga_ref, br_ref, gate_ref,
     wt_bf, wal2_bf, stage, sem) = refs

    @pl.when(pl.program_id(0) == 0)
    def _():
        pieces = [(c, min(_W_PIECE, _R_END - c)) for c in range(0, _R_END, _W_PIECE)]

        def piece_copy(p):
            c, n = pieces[p]
            return pltpu.make_async_copy(wt_hbm.at[layer, pl.ds(c, n), :], stage.at[p % 2, pl.ds(0, n), :],
                                         sem.at[p % 2])

        piece_copy(0).start()
        for p, (c, n) in enumerate(pieces):
            if p + 1 < len(pieces):
                piece_copy(p + 1).start()
            piece_copy(p).wait()
            wt_bf[c:c + n, :] = stage[p % 2, 0:n, :].astype(BF16)
        wal2_bf[...] = jnp.concatenate(
            [wal2_ref[0].astype(BF16), jnp.zeros((LANES - B_GATE_RANK, B_KEY_WIDTH), BF16)], axis=0)

    if fused:
        r = lax.broadcasted_iota(I32, (ROW_TILE, LOCAL_ROWS), 1).astype(F32)
        tiles = []
        for t in range(INPROJ_ROWS // ROW_TILE):
            rows = slice(t * ROW_TILE, (t + 1) * ROW_TILE)
            pos = route_ref[rows, :]
            sel = jnp.where(r == pos[:, 0:1], 1.0, jnp.where(r == pos[:, 1:2], 1.0, 0.0)).astype(BF16)
            ys = ys_ref[t * LOCAL_CHUNKS:(t + 1) * LOCAL_CHUNKS].reshape(LOCAL_ROWS, D_MODEL)
            tiles.append(x_ref[rows, :] + _dot(sel, ys))
        x = jnp.concatenate(tiles, axis=0)
        x3_ref[...] = x
    else:
        x = x_ref[...]
    h = _rms(x, g_ref[...]).astype(BF16)

    def mm(lo, hi):
        return _dot_nt(h, wt_bf[lo:hi, :])

    aq_ref[...] = (mm(_R_AQ, _R_AK) * (A_HEAD_DIM ** -0.5 * LOG2E)).astype(BF16)
    ak_ref[...] = mm(_R_AK, _R_AV).astype(BF16)
    av_ref[...] = mm(_R_AV, _R_BQ).astype(BF16)
    bq_ref[...] = (mm(_R_BQ, _R_BK) * (B_KEY_DIM ** -0.5)).astype(BF16)
    bk_ref[...] = mm(_R_BK, _R_BV).astype(BF16)
    bv_ref[...] = mm(_R_BV, _R_ALPHA).astype(BF16)
    r = mm(_R_BR, _R_GATE)
    br_ref[...] = (r * jax.nn.sigmoid(r)).astype(BF16)
    for c in range(_R_GATE, _R_END, _W_PIECE):
        gate_ref[:, c - _R_GATE:c - _R_GATE + _W_PIECE] = jax.nn.sigmoid(mm(c, c + _W_PIECE)).astype(BF16)
    z = _dot(mm(_R_ALPHA, _R_ALPHA + LANES).astype(BF16), wal2_bf[...]) + bal_ref[...]
    lga_ref[...] = (jnp.minimum(z, 0.0) - jnp.log(1.0 + jnp.exp(-jnp.abs(z)))) * (1.0 / B_GATE_TAU)


def _inproj(x, g, w_in, w_al2, b_al, layer, moe=None):
    t, d = x.shape
    assert w_in.shape[2] == _R_END
    row = lambda w: pl.BlockSpec((INPROJ_ROWS, w), lambda i: (i, 0))
    full = lambda a: pl.BlockSpec(a.shape, lambda i: (0,) * a.ndim)
    sds = lambda w, dt: jax.ShapeDtypeStruct((t, w), dt)
    widths = [(A_WIDTH, BF16), (A_WIDTH, BF16), (A_WIDTH, BF16), (B_KEY_WIDTH, BF16), (B_KEY_WIDTH, BF16),
              (B_VAL_WIDTH, BF16), (B_KEY_WIDTH, F32), (B_VAL_WIDTH, BF16), (_R_END - _R_GATE, BF16)]
    fused = moe is not None
    moe_specs, moe_out_specs, moe_out_shape = [], [], []
    if fused:
        n_chunks = INPROJ_ROWS // ROW_TILE * LOCAL_CHUNKS
        moe_specs = [row(LANES), pl.BlockSpec((n_chunks, CHUNK_ROWS, d), lambda i: (i, 0, 0))]
        moe_out_specs, moe_out_shape = [row(d)], [sds(d, F32)]
    return pl.pallas_call(
        functools.partial(_inproj_kernel, layer, fused),
        grid=(t // INPROJ_ROWS,),
        in_specs=[row(d)] + moe_specs + [full(g), pl.BlockSpec(memory_space=pl.ANY),
                                         pl.BlockSpec((1,) + w_al2.shape[1:], lambda i: (layer, 0, 0)),
                                         full(b_al)],
        out_specs=moe_out_specs + [row(w) for w, _ in widths],
        out_shape=moe_out_shape + [sds(w, dt) for w, dt in widths],
        scratch_shapes=[pltpu.VMEM((_R_END, d), BF16), pltpu.VMEM((LANES, B_KEY_WIDTH), BF16),
                        pltpu.VMEM((2, _W_PIECE, d), F32), pltpu.SemaphoreType.DMA((2,))],
        compiler_params=_params("arbitrary"),
        name="inproj",
    )(x, *(moe or ()), g, jnp.swapaxes(w_in, 1, 2), w_al2, b_al)


def _band_kernel(q_ref, *refs):
    n_win = BAND_TILES_PER_STEP + BAND_TILES - 1
    k_refs, v_refs = refs[:n_win], refs[n_win:2 * n_win]
    u_ref, o_ref, bias_ref = refs[2 * n_win:]
    i = pl.program_id(1)
    lane = lax.broadcasted_iota(I32, (1, LANES), 1)
    low = lane < A_HEAD_DIM
    ones = jnp.ones((BAND_KEYS, LANES), BF16)

    @pl.when((pl.program_id(0) == 0) & (i == 0))
    def _():
        cq = lax.broadcasted_iota(I32, (ROW_TILE, BAND_KEYS), 0) >> LOG_CHUNK
        ck = lax.broadcasted_iota(I32, (ROW_TILE, BAND_KEYS), 1) >> LOG_CHUNK
        valid = (ck >= cq) & (ck <= cq + A_LEFT_CHUNKS)
        for h in range(A_HEADS):
            rows = jnp.broadcast_to(u_ref[h:h + 1, :], (ROW_TILE, BIAS_PERIOD))
            rows = pltpu.roll(rows, BIAS_PERIOD - (ROW_TILE - 1), 1, stride=1, stride_axis=0)
            bias_ref[h // 2, (h % 2) * ROW_TILE:(h % 2 + 1) * ROW_TILE, :] = jnp.where(
                valid, rows[:, :BAND_KEYS], NEG)

    def attend(tile, n_missing):
        rows = pl.ds(tile * ROW_TILE, ROW_TILE)
        window = range(tile, tile + BAND_TILES)
        for p in range(A_HEADS // 2):
            sl = slice(p * LANES, (p + 1) * LANES)
            qp = q_ref[rows, sl]
            zero = jnp.zeros_like(qp)
            q2 = jnp.concatenate([jnp.where(low, qp, zero), jnp.where(low, zero, qp)], axis=0)
            kp = jnp.concatenate([k_refs[j][:, sl] for j in window], axis=0)
            vp = jnp.concatenate([v_refs[j][:, sl] for j in window], axis=0)
            s = _dot_nt(q2, kp) + bias_ref[p]
            if n_missing:
                col = lax.broadcasted_iota(I32, (1, BAND_KEYS), 1)
                s = s + jnp.where(col < n_missing * ROW_TILE, NEG, 0.0).astype(F32)
            pe = jnp.exp2(s - jnp.max(s, axis=-1, keepdims=True)).astype(BF16)
            o2 = _dot(pe, jnp.concatenate([vp, ones], axis=1))
            o = o2[:, :LANES] * (1.0 / o2[:, LANES:])
            o_ref[rows, sl] = jnp.where(low, o[:ROW_TILE], o[ROW_TILE:]).astype(BF16)

    @pl.when(i > 0)
    def _():
        for tile in range(BAND_TILES_PER_STEP):
            attend(tile, 0)

    @pl.when(i == 0)
    def _():
        for tile in range(BAND_TILES_PER_STEP):
            attend(tile, max(BAND_TILES - 1 - tile, 0))


def _band_bias_vector(rel_table):
    h = rel_table.shape[0]
    tab = rel_table.astype(F32) * LOG2E
    shift = A_LEFT_CHUNKS * CHUNK + ROW_TILE - 1
    n_far = shift - A_MAX_REL + 1
    span = ROW_TILE + BAND_KEYS - 1
    assert span - 1 - shift <= A_MAX_REL and span <= BIAS_PERIOD
    u = jnp.concatenate([jnp.broadcast_to(tab[:, 2 * A_MAX_REL:], (h, n_far)),
                         tab[:, 2 * A_MAX_REL - 1:2 * A_MAX_REL - 1 - (span - n_far):-1]], axis=1)
    return jnp.pad(u, ((0, 0), (0, BIAS_PERIOD - span)))


def _band_attention(q, k, v, u, batch):
    t, w = q.shape
    n = BAND_TILES_PER_STEP
    nb = t // batch // (n * ROW_TILE)
    step = pl.BlockSpec((n * ROW_TILE, w), lambda b, i: (b * nb + i, 0))
    tile = lambda j: pl.BlockSpec(
        (ROW_TILE, w), lambda b, i: (n * b * nb + jnp.maximum(n * i + j - (BAND_TILES - 1), 0), 0))
    window = [tile(j) for j in range(n + BAND_TILES - 1)]
    return pl.pallas_call(
        _band_kernel,
        grid=(batch, nb),
        in_specs=[step] + window + window + [pl.BlockSpec(u.shape, lambda b, i: (0, 0))],
        out_specs=step,
        out_shape=jax.ShapeDtypeStruct((t, w), BF16),
        scratch_shapes=[pltpu.VMEM((A_HEADS // 2, 2 * ROW_TILE, BAND_KEYS), F32)],
        compiler_params=_params("arbitrary", "arbitrary"),
        name="band_attn",
    )(q, *([k] * len(window)), *([v] * len(window)), u)


def _gla_constants():
    c = CHUNK
    t = np.arange(c)[:, None]
    r = np.arange(c)[None, :]
    mats = [(r <= t), (r > t)]
    lvl = np.full((c, c), -1, np.int32)
    lvl[np.arange(c), np.arange(c)] = N_LEVELS
    for l in range(N_LEVELS):
        m = (c // 2) >> l
        mid = (t // (2 * m)) * (2 * m) + m
        upper = t >= mid
        mats.append(np.where(upper, (r >= mid) & (r <= t), (r > t) & (r < mid)))
        s = r
        same = (s // (2 * m)) == (t // (2 * m))
        lvl[np.asarray(same & upper & (s < mid))] = l
    eye = np.eye(CHUNKS_PER_TILE)
    mexp = np.concatenate([np.kron(eye, m) for m in mats], axis=0).astype(np.float32)
    lvl = np.tile(lvl, (1, B_HEADS))
    return jnp.asarray(mexp, BF16), jnp.asarray(lvl, I32)


def _gla_kernel(q_ref, k_ref, v_ref, g_ref, r_ref, gn_ref, mexp_ref, lvl_ref, o_ref, s_ref):
    @pl.when(pl.program_id(1) == 0)
    def _():
        s_ref[...] = jnp.zeros_like(s_ref)

    kw = B_KEY_WIDTH
    ri = lax.broadcasted_iota(I32, (kw, kw), 0) >> LOG_CHUNK
    ci = lax.broadcasted_iota(I32, (kw, kw), 1) >> LOG_CHUNK
    bd = ri == ci
    head_ind = jnp.where(bd, 1.0, 0.0).astype(BF16)
    ri2 = lax.broadcasted_iota(I32, (kw, 2 * kw), 0) >> LOG_CHUNK
    ci2 = (lax.broadcasted_iota(I32, (kw, 2 * kw), 1) & (kw - 1)) >> LOG_CHUNK
    bd2 = ri2 == ci2
    lvl = lvl_ref[...]
    split_row = lax.broadcasted_iota(I32, (CHUNK_ROWS, kw), 0)
    ones = jnp.ones((CHUNK_ROWS, LANES), BF16)
    zero_b = jnp.zeros((kw, kw), BF16)
    chunks = [slice(c * CHUNK, (c + 1) * CHUNK) for c in range(CHUNKS_PER_TILE)]

    def head_blocks(x):
        return jnp.where(bd, jnp.concatenate([x] * B_HEADS, axis=0), zero_b)

    def prepare(tile):
        trows = pl.ds(tile * ROW_TILE, ROW_TILE)
        q = q_ref[trows, :].astype(F32)
        k = k_ref[trows, :].astype(F32)
        g = g_ref[trows, :]
        gb = g.astype(BF16)
        half = EXP_ROWS * CHUNKS_PER_TILE // 2
        w = jnp.exp(jnp.concatenate([_dot(mexp_ref[:half, :], gb), _dot(mexp_ref[half:, :], gb)], axis=0))
        qt = (q * w[0:ROW_TILE]).astype(BF16)
        kb = (k * w[ROW_TILE:2 * ROW_TILE]).astype(BF16)
        qk = (q * k).astype(BF16)

        attn = [jnp.zeros((CHUNK, kw), F32) for _ in chunks]
        for l in range(N_LEVELS):
            wl = w[(2 + l) * ROW_TILE:(3 + l) * ROW_TILE]
            qh = (q * wl).astype(BF16)
            kh = (k * wl).astype(BF16)
            for c, rows in enumerate(chunks):
                attn[c] = jnp.where(lvl == l, _dot_nt(qh[rows], head_blocks(kh[rows])), attn[c])

        out = []
        for c, rows in enumerate(chunks):
            a = jnp.where(lvl == N_LEVELS, _dot(qk[rows], head_ind), attn[c])
            v = v_ref[pl.ds(tile * ROW_TILE + c * CHUNK, CHUNK), :]
            vstack = jnp.concatenate([v[:, j * LANES:(j + 1) * LANES] for j in range(B_HEADS)], axis=0)
            kv = _dot_tn(head_blocks(kb[rows]), vstack)
            d = jnp.exp(jnp.sum(g[rows], axis=0, keepdims=True))
            d1 = d.astype(BF16).astype(F32)
            dp = jnp.where(split_row == 0, d1, jnp.where(split_row == 1, d - d1, 0.0)).astype(BF16)
            dcol = _dot_tn(dp, ones)
            out.append((a.astype(BF16), qt[rows], vstack, kv, dcol))
        return out

    prepared = [p for tile in range(GLA_TILES_PER_STEP) for p in prepare(tile)]

    s = s_ref[...]
    for c, (a, qtc, vstack, kv, dcol) in enumerate(prepared):
        rows = pl.ds(c * CHUNK, CHUNK)
        lhs = jnp.concatenate([a, qtc], axis=1)
        lhs = jnp.where(bd2, jnp.concatenate([lhs] * B_HEADS, axis=0), jnp.zeros((kw, 2 * kw), BF16))
        rhs = jnp.concatenate([vstack, s.astype(BF16)], axis=0)
        o = _dot(lhs, rhs)
        s = dcol * s + kv
        for j in range(B_HEADS):
            oj = o[j * CHUNK:(j + 1) * CHUNK]
            sl = slice(j * LANES, (j + 1) * LANES)
            y = oj * lax.rsqrt(jnp.mean(oj * oj, axis=-1, keepdims=True) + EPS) * gn_ref[...]
            o_ref[rows, sl] = (y * r_ref[rows, sl].astype(F32)).astype(BF16)
    s_ref[...] = s


def _gla(q, k, v, g, r, gn, batch):
    t = q.shape[0]
    nb = t // batch // (GLA_TILES_PER_STEP * ROW_TILE)
    mexp, lvl = _gla_constants()
    cur = lambda b, i: (b * nb + i, 0)
    blk = lambda w: pl.BlockSpec((GLA_TILES_PER_STEP * ROW_TILE, w), cur)
    full = lambda a: pl.BlockSpec(a.shape, lambda b, i: (0,) * a.ndim)
    return pl.pallas_call(
        _gla_kernel,
        grid=(batch, nb),
        in_specs=[blk(B_KEY_WIDTH), blk(B_KEY_WIDTH), blk(B_VAL_WIDTH), blk(B_KEY_WIDTH), blk(B_VAL_WIDTH),
                  full(gn), full(mexp), full(lvl)],
        out_specs=blk(B_VAL_WIDTH),
        out_shape=jax.ShapeDtypeStruct((t, B_VAL_WIDTH), BF16),
        scratch_shapes=[pltpu.VMEM((B_KEY_WIDTH, B_VAL_DIM), F32)],
        compiler_params=_params("arbitrary", "arbitrary"),
        name="gla",
    )(q, k, v, g, r, gn, mexp, lvl)


def _token_kernel(x_ref, oa_ref, ob_ref, gate_ref, wb0_ref, wb1_ref, wmix_ref, gx_ref, wq_ref,
                  km_ref, vm_ref, wo_ref, gf_ref, wr_ref, br_ref, ltri_ref, utri_ref,
                  x2_ref, hs_ref, route_ref, cnt_ref):
    ma = _dot(oa_ref[...], wb0_ref[...])
    mb = _dot(ob_ref[...], wb1_ref[...])
    merged = (gate_ref[:, :D_MODEL].astype(F32) * ma + gate_ref[:, D_MODEL:].astype(F32) * mb).astype(BF16)
    x1 = x_ref[...] + _dot(merged, wmix_ref[...])

    h2 = _rms(x1, gx_ref[...]).astype(BF16)
    qx = (_dot(h2, wq_ref[...]) * (X_HEAD_DIM ** -0.5)).astype(BF16)
    heads = []
    for h in range(X_HEADS):
        sl = slice(h * X_HEAD_DIM, (h + 1) * X_HEAD_DIM)
        s = _dot_nt(qx[:, sl], km_ref[0, :, sl])
        m = jnp.max(s, axis=-1, keepdims=True)
        pe = jnp.exp(s - m)
        l = jnp.sum(pe, axis=-1, keepdims=True)
        heads.append((_dot(pe.astype(BF16), vm_ref[0, :, sl]) * (1.0 / l)).astype(BF16))
    x2 = x1 + _dot(jnp.concatenate(heads, axis=1), wo_ref[...])
    x2_ref[...] = x2

    h3 = _rms(x2, gf_ref[...])

    h3_hi = h3.astype(BF16)
    h3_lo = (h3 - h3_hi.astype(F32)).astype(BF16)
    hw = _dot(h3_hi, wr_ref[...])
    logits = hw[:, :LANES] + hw[:, LANES:] + _dot(h3_lo, wr_ref[:, :LANES]) + br_ref[...]
    oh0, oh1, g0, g1 = _route(logits)
    for h in range(TOKEN_TILES_PER_STEP):
        rows = slice(h * ROW_TILE, (h + 1) * ROW_TILE)
        chunks = pl.ds(h * LOCAL_CHUNKS, LOCAL_CHUNKS)
        _sort_tile(oh0[rows], oh1[rows], g0[rows], g1[rows], h3_hi[rows], ltri_ref, utri_ref,
                   hs_ref.at[chunks], route_ref.at[pl.ds(h * ROW_TILE, ROW_TILE)], cnt_ref.at[h])


def _route(logits):
    lane = lax.broadcasted_iota(I32, logits.shape, 1).astype(F32)
    big = jnp.float32(LANES)
    gl = jnp.where(lane < N_GROUPS, logits, NEG)
    gmax = jnp.max(gl, axis=-1, keepdims=True)
    gidx = jnp.min(jnp.where(gl == gmax, lane, big), axis=-1, keepdims=True)
    g_w = 1.0 / jnp.sum(jnp.exp(gl - gmax), axis=-1, keepdims=True)
    lo = N_GROUPS + EXPERTS_PER_GROUP * gidx
    el = jnp.where((lane >= lo) & (lane < lo + EXPERTS_PER_GROUP), logits, NEG)
    v1 = jnp.max(el, axis=-1, keepdims=True)
    i1 = jnp.min(jnp.where(el == v1, lane, big), axis=-1, keepdims=True)
    el2 = jnp.where(lane == i1, NEG, el)
    v2 = jnp.max(el2, axis=-1, keepdims=True)
    i2 = jnp.min(jnp.where(el2 == v2, lane, big), axis=-1, keepdims=True)
    e21 = jnp.exp(v2 - v1)
    w1 = g_w / (1.0 + e21)

    def gate_cols(w):
        hi = w.astype(BF16).astype(F32)
        return jnp.where(lane == 0, hi, jnp.where(lane == 1, w - hi, 0.0)).astype(BF16)

    oh0 = jnp.where(lane == i1 - N_GROUPS, 1.0, 0.0)
    oh1 = jnp.where(lane == i2 - N_GROUPS, 1.0, 0.0)
    return oh0, oh1, gate_cols(w1), gate_cols(w1 * e21)


def _sort_tile(oh0, oh1, g0, g1, h3_hi, ltri_ref, utri_ref, hs_ref, route_ref, cnt_ref):
    lane = lax.broadcasted_iota(I32, oh0.shape, 1)
    oh = oh0 + oh1
    nch = jnp.floor((jnp.sum(oh, axis=0, keepdims=True) + (CHUNK_ROWS - 1)) * (1.0 / CHUNK_ROWS))
    nch8 = jnp.broadcast_to(nch, (8, LANES))
    start = _dot(nch8.astype(BF16), utri_ref[...])[0:1] * CHUNK_ROWS
    rank = _dot(ltri_ref[...], oh.astype(BF16))
    row = start + rank
    pos0 = jnp.sum(row * oh0, axis=-1, keepdims=True)
    pos1 = jnp.sum(row * oh1, axis=-1, keepdims=True)
    route = jnp.where(lane == 0, pos0, jnp.where(lane == 1, pos1, 0.0))
    route_t = jnp.transpose(route)
    r = lax.broadcasted_iota(I32, (LOCAL_ROWS, ROW_TILE), 0).astype(F32)
    p0 = jnp.where(r == route_t[0:1, :], 1.0, 0.0).astype(BF16)
    p1 = jnp.where(r == route_t[1:2, :], 1.0, 0.0).astype(BF16)
    sorted_rows = jnp.concatenate([_dot(p0 + p1, h3_hi), _dot(p0, g0) + _dot(p1, g1)], axis=1)
    hs_ref[...] = sorted_rows.astype(BF16).reshape(hs_ref.shape)
    route_ref[...] = route
    cnt_ref[...] = nch8


def _token(x, oa, ob, gates, wb0, wb1, wmix, gx, wq, km, vm, wo, gf, wr, br, batch):
    t, d = x.shape
    n = TOKEN_TILES_PER_STEP
    nb = t // batch // (n * ROW_TILE)
    nt = t // ROW_TILE
    ltri = jnp.asarray(np.tril(np.ones((ROW_TILE, ROW_TILE), np.float32), -1), BF16)
    utri = jnp.asarray(np.triu(np.ones((LANES, LANES), np.float32), 1), BF16)
    cur = lambda b, i: (b * nb + i, 0)
    cur3 = lambda b, i: (b * nb + i, 0, 0)
    blk = lambda w: pl.BlockSpec((n * ROW_TILE, w), cur)
    full = lambda a: pl.BlockSpec(a.shape, lambda b, i: (0,) * a.ndim)
    mem = pl.BlockSpec((1,) + km.shape[1:], lambda b, i: (b, 0, 0))
    return pl.pallas_call(
        _token_kernel,
        grid=(batch, nb),
        in_specs=[blk(d), blk(A_WIDTH), blk(B_VAL_WIDTH), blk(2 * d), full(wb0), full(wb1), full(wmix),
                  full(gx), full(wq), mem, mem, full(wo), full(gf), full(wr), full(br), full(ltri), full(utri)],
        out_specs=[blk(d), pl.BlockSpec((n * LOCAL_CHUNKS, CHUNK_ROWS, SORT_WIDTH), cur3),
                   blk(LANES), pl.BlockSpec((n, 8, LANES), cur3)],
        out_shape=[jax.ShapeDtypeStruct((t, d), F32),
                   jax.ShapeDtypeStruct((nt * LOCAL_CHUNKS, CHUNK_ROWS, SORT_WIDTH), BF16),
                   jax.ShapeDtypeStruct((t, LANES), F32),
                   jax.ShapeDtypeStruct((nt, 8, LANES), F32)],
        compiler_params=_params("arbitrary", "arbitrary"),
        name="token",
    )(x, oa, ob, gates, wb0, wb1, wmix, gx, wq, km, vm, wo, gf, wr, br, ltri, utri)


def _expert_kernel(layer, te_ref, nu_ref, nv_ref, ch_ref, first_ref, nxt_ref,
                   hs_hbm, wg_hbm, wu_hbm, wd_hbm, ys_hbm,
                   xbuf, ybuf, wg_st, wu_st, wd_st, wgu_bf, wd_bf, gsem, ssem, wsem):
    n_used = nu_ref[0]
    ring = GATHER_AHEAD + 1
    last_tile = te_ref.shape[0] - 1

    def weights(expert, s, start):
        for src, dst in ((wg_hbm, wg_st), (wu_hbm, wu_st), (wd_hbm, wd_st)):
            cp = pltpu.make_async_copy(src.at[layer * N_EXPERTS + expert], dst.at[s], wsem.at[s])
            cp.start() if start else cp.wait()

    def for_chunks(tile, fn):
        nv = nv_ref[tile]

        @pl.when(nv == TILE_CHUNKS)
        def _():
            for c in range(TILE_CHUNKS):
                fn(c)

        @pl.when(nv != TILE_CHUNKS)
        def _():
            def body(c, carry):
                fn(c)
                return carry

            lax.fori_loop(0, nv, body, 0)

    def gather(step, start):
        tile = jnp.minimum(step, last_tile)
        s = lax.rem(step, ring)
        for c in range(TILE_CHUNKS):
            cp = pltpu.make_async_copy(hs_hbm.at[ch_ref[tile * TILE_CHUNKS + c]], xbuf.at[s, c], gsem.at[s])
            cp.start() if start else cp.wait()

    def scatter(tile, s, start):
        def one(c):
            cp = pltpu.make_async_copy(ybuf.at[s, c], ys_hbm.at[ch_ref[tile * TILE_CHUNKS + c]], ssem.at[s])
            cp.start(priority=1) if start else cp.wait()

        for_chunks(tile, one)

    weights(te_ref[0], 0, True)
    for step in range(GATHER_AHEAD):
        gather(step, True)

    def tile_body(i, run):
        slot = lax.rem(i, 2)
        gather(i, False)

        @pl.when(i >= 2)
        def _():
            scatter(i - 2, slot, False)

        @pl.when(first_ref[i] == 1)
        def _():
            s = lax.rem(run, 2)
            weights(te_ref[i], s, False)

            @pl.when(nxt_ref[i] >= 0)
            def _():
                weights(nxt_ref[i], 1 - s, True)

            wgu_bf[:, :EXPERT_FF] = wg_st[s].astype(BF16)
            wgu_bf[:, EXPERT_FF:] = wu_st[s].astype(BF16)
            wd_bf[...] = wd_st[s].astype(BF16)

        xg = xbuf[lax.rem(i, ring)].reshape(EXPERT_ROWS, SORT_WIDTH)
        hgu = _dot(xg[:, :D_MODEL], wgu_bf[...])
        gather(i + GATHER_AHEAD, True)
        hg, hu = hgu[:, :EXPERT_FF], hgu[:, EXPERT_FF:]
        hid = (hg * jax.nn.sigmoid(hg) * hu).astype(BF16)
        g = xg[:, D_MODEL:].astype(F32)
        y = ((g[:, 0:1] + g[:, 1:2]) * _dot(hid, wd_bf[...])).astype(BF16)
        y = jnp.concatenate([y, jnp.zeros((EXPERT_ROWS, LANES), BF16)], axis=1)
        ybuf[slot] = y.reshape(TILE_CHUNKS, CHUNK_ROWS, SORT_WIDTH)
        scatter(i, slot, True)
        return run + first_ref[i]

    lax.fori_loop(0, n_used, tile_body, jnp.int32(0))

    last = n_used - 1
    for ahead in range(1, GATHER_AHEAD + 1):
        gather(last + ahead, False)
    scatter(last, lax.rem(last, 2), False)

    @pl.when(last >= 1)
    def _():
        scatter(last - 1, lax.rem(last - 1, 2), False)


def _experts(hs, tile_expert, n_used, n_valid, chunks, run_first, run_next, wg, wu, wd, layer):
    anyspace = pl.BlockSpec(memory_space=pl.ANY)
    grid_spec = pltpu.PrefetchScalarGridSpec(
        num_scalar_prefetch=6,
        grid=(1,),
        in_specs=[anyspace] * 4,
        out_specs=anyspace,
        scratch_shapes=[pltpu.VMEM((GATHER_AHEAD + 1, TILE_CHUNKS, CHUNK_ROWS, SORT_WIDTH), BF16),
                        pltpu.VMEM((2, TILE_CHUNKS, CHUNK_ROWS, SORT_WIDTH), BF16),
                        pltpu.VMEM((2, D_MODEL, EXPERT_FF), F32), pltpu.VMEM((2, D_MODEL, EXPERT_FF), F32),
                        pltpu.VMEM((2, EXPERT_FF, D_MODEL), F32),
                        pltpu.VMEM((D_MODEL, 2 * EXPERT_FF), BF16), pltpu.VMEM((EXPERT_FF, D_MODEL), BF16),
                        pltpu.SemaphoreType.DMA((GATHER_AHEAD + 1,)), pltpu.SemaphoreType.DMA((2,)),
                        pltpu.SemaphoreType.DMA((2,))],
    )
    return pl.pallas_call(
        functools.partial(_expert_kernel, layer),
        grid_spec=grid_spec,
        out_shape=jax.ShapeDtypeStruct(hs.shape, BF16),
        input_output_aliases={6: 0},
        compiler_params=_params("arbitrary"),
        name="experts",
    )(tile_expert, n_used, n_valid, chunks, run_first, run_next, hs, wg, wu, wd)


def _combine_kernel(x_ref, route_ref, ys_ref, gfin_ref, o_ref):
    r = lax.broadcasted_iota(I32, (ROW_TILE, LOCAL_ROWS), 1).astype(F32)
    for t in range(COMBINE_TILES_PER_STEP):
        rows = slice(t * ROW_TILE, (t + 1) * ROW_TILE)
        pos = route_ref[rows, :]
        sel = jnp.where(r == pos[:, 0:1], 1.0, jnp.where(r == pos[:, 1:2], 1.0, 0.0)).astype(BF16)
        ys = ys_ref[t * LOCAL_CHUNKS:(t + 1) * LOCAL_CHUNKS].reshape(LOCAL_ROWS, D_MODEL)
        o_ref[rows, :] = _rms(x_ref[rows, :] + _dot(sel, ys), gfin_ref[...])


def _combine(x2, route, ys, gfin):
    t, d = x2.shape
    n = COMBINE_TILES_PER_STEP
    return pl.pallas_call(
        _combine_kernel,
        grid=(t // (n * ROW_TILE),),
        in_specs=[pl.BlockSpec((n * ROW_TILE, d), lambda i: (i, 0)),
                  pl.BlockSpec((n * ROW_TILE, LANES), lambda i: (i, 0)),
                  pl.BlockSpec((n * LOCAL_CHUNKS, CHUNK_ROWS, d), lambda i: (i, 0, 0)),
                  pl.BlockSpec((1, d), lambda i: (0, 0))],
        out_specs=pl.BlockSpec((n * ROW_TILE, d), lambda i: (i, 0)),
        out_shape=jax.ShapeDtypeStruct((t, d), F32),
        compiler_params=_params("arbitrary"),
        name="combine",
    )(x2, route, ys, gfin)


def _chunk_plan(nch, n_tiles):
    nt = nch.shape[0]
    local_start = jnp.cumsum(nch, axis=1) - nch
    cum = jnp.cumsum(nch, axis=0)
    total = cum[-1]
    tiles = (total + TILE_CHUNKS - 1) // TILE_CHUNKS
    tile_end = jnp.cumsum(tiles)
    n_used = tile_end[-1:]
    tile_ids = jnp.arange(n_tiles, dtype=I32)
    tile_expert = jnp.minimum(jnp.sum((tile_end[None, :] <= tile_ids[:, None]).astype(I32), axis=1),
                              N_EXPERTS - 1)
    sel = (tile_expert[:, None] == jnp.arange(N_EXPERTS, dtype=I32)[None, :]).astype(I32)
    pick = lambda table: jnp.sum(sel[:, :, None] * table.T[None, :, :], axis=1)
    first_tile = jnp.sum(sel * (tile_end - tiles)[None, :], axis=1)
    slot = (tile_ids - first_tile)[:, None] * TILE_CHUNKS + jnp.arange(TILE_CHUNKS, dtype=I32)[None, :]
    valid = (slot < jnp.sum(sel * total[None, :], axis=1)[:, None]) & (tile_ids < n_used)[:, None]
    src_tile = jnp.sum((pick(cum)[:, None, :] <= slot[:, :, None]).astype(I32), axis=2)
    src_tile = jnp.minimum(src_tile, nt - 1)
    at = (src_tile[:, :, None] == jnp.arange(nt, dtype=I32)[None, None, :]).astype(I32)
    before = jnp.sum(at * pick(cum - nch)[:, None, :], axis=2)
    start = jnp.sum(at * pick(local_start)[:, None, :], axis=2)
    chunk = jnp.where(valid, src_tile * LOCAL_CHUNKS + start + slot - before, LOCAL_CHUNKS - 1)
    used = tile_ids < n_used
    run_first = ((tile_ids == first_tile) & used).astype(I32)
    run_end = jnp.sum(sel * tile_end[None, :], axis=1)
    next_expert = jnp.sum((run_end[:, None] == tile_ids[None, :]).astype(I32) * tile_expert[None, :], axis=1)
    run_next = jnp.where(used & (run_end < n_used), next_expert, -1)
    return tile_expert, n_used, jnp.sum(valid.astype(I32), axis=1), chunk.reshape(-1), run_first, run_next


def kernel(x, mem, norm_mix_g, w_in, rel_bias, gla_w_alpha, gla_b_alpha, gla_norm_g, w_branch, w_mix_out, norm_x_g, mem_norm_g, w_xq, w_xkv, w_xo, norm_ffn_g, w_group_router, b_group_router, w_expert_router, b_expert_router, w_exp_gate, w_exp_up, w_exp_down, final_norm_g):
    batch, seq, d = x.shape
    depth = w_in.shape[0]
    t = batch * seq
    step_tiles = max(TOKEN_TILES_PER_STEP, GLA_TILES_PER_STEP, BAND_TILES_PER_STEP, COMBINE_TILES_PER_STEP)
    assert d == D_MODEL and seq % (step_tiles * ROW_TILE) == 0 and seq % INPROJ_ROWS == 0
    nt = t // ROW_TILE
    n_tiles = nt * LOCAL_CHUNKS // TILE_CHUNKS + N_EXPERTS

    xf = x.reshape(t, d)
    km_all, vm_all = _memkv(mem, mem_norm_g, w_xkv)
    row = lambda a: a.reshape(1, -1).astype(F32)

    moe = None
    for l in range(depth):
        res = _inproj(xf, row(norm_mix_g[l]), w_in, gla_w_alpha, row(gla_b_alpha[l]), l, moe)
        if moe is not None:
            xf, res = res[0], res[1:]
        aq, ak, av, bq, bk, bv, lga, br, gates = res

        oa = _band_attention(aq, ak, av, _band_bias_vector(rel_bias[l]), batch)
        ob = _gla(bq, bk, bv, lga, br, row(gla_norm_g[l]), batch)

        wr = jnp.pad(jnp.concatenate([w_group_router[l], w_expert_router[l]], axis=1).astype(F32),
                     ((0, 0), (0, LANES - N_GROUPS - N_EXPERTS)))
        wr_hi = wr.astype(BF16)
        wr = jnp.concatenate([wr_hi, (wr - wr_hi.astype(F32)).astype(BF16)], axis=1)
        brt = jnp.pad(jnp.concatenate([b_group_router[l], b_expert_router[l]]).astype(F32),
                      (0, LANES - N_GROUPS - N_EXPERTS)).reshape(1, LANES)
        x2, hs, route, cnt = _token(
            xf, oa, ob, gates, w_branch[l, 0].astype(BF16), w_branch[l, 1].astype(BF16),
            w_mix_out[l].astype(BF16), row(norm_x_g[l]), w_xq[l].astype(BF16), km_all[l], vm_all[l],
            w_xo[l].astype(BF16), row(norm_ffn_g[l]), wr, brt, batch)

        plan = _chunk_plan(cnt[:, 0, :N_EXPERTS].astype(I32), n_tiles)
        e3 = lambda w: w.reshape((depth * N_EXPERTS,) + w.shape[3:])
        ys = _experts(hs, *plan, e3(w_exp_gate), e3(w_exp_up), e3(w_exp_down), l)
        xf, moe = x2, (route, ys)

    return _combine(x2, route, ys, row(final_norm_g)).reshape(batch, seq, d)
```

```python
import functools

import numpy as np
import jax
import jax.numpy as jnp
from jax import lax
from jax.experimental import pallas as pl
from jax.experimental.pallas import tpu as pltpu

F32 = jnp.float32
BF16 = jnp.bfloat16
I32 = jnp.int32

D_MODEL = 1024
CHUNK = 64
EPS = 1e-6
A_HEADS = 8
A_HEAD_DIM = 64
A_WIDTH = 512
A_LEFT_CHUNKS = 8
A_MAX_REL = 256
B_HEADS = 4
B_KEY_DIM = 64
B_VAL_DIM = 128
B_KEY_WIDTH = 256
B_VAL_WIDTH = 512
B_GATE_RANK = 16
B_GATE_TAU = 16.0
X_HEADS = 4
X_HEAD_DIM = 256
N_GROUPS = 4
EXPERTS_PER_GROUP = 8
N_EXPERTS = N_GROUPS * EXPERTS_PER_GROUP
EXPERT_FF = 256

LANES = 128
ROW_TILE = 256
CHUNKS_PER_TILE = ROW_TILE // CHUNK
BAND_TILES = A_LEFT_CHUNKS // CHUNKS_PER_TILE + 1
BAND_KEYS = BAND_TILES * ROW_TILE
BIAS_PERIOD = 1024
LOG_CHUNK = 6
N_LEVELS = LOG_CHUNK
EXP_ROWS = (2 + N_LEVELS) * CHUNK
CHUNK_ROWS = 16
TOKEN_TILES_PER_STEP = 2
INPROJ_ROWS = 512
COMBINE_TILES_PER_STEP = 4
GLA_TILES_PER_STEP = 4
BAND_TILES_PER_STEP = 4
EXPERT_ROWS = 512
GATHER_AHEAD = 7
TILE_CHUNKS = EXPERT_ROWS // CHUNK_ROWS
LOCAL_CHUNKS = 2 * ROW_TILE // CHUNK_ROWS + N_EXPERTS
LOCAL_ROWS = LOCAL_CHUNKS * CHUNK_ROWS
assert (2 * ROW_TILE + N_EXPERTS * (CHUNK_ROWS - 1)) // CHUNK_ROWS < LOCAL_CHUNKS
SORT_WIDTH = D_MODEL + LANES
NEG = -1e30
LOG2E = 1.4426950408889634
VMEM_LIMIT = 56 * 1024 * 1024


def _params(*sem):
    return pltpu.CompilerParams(dimension_semantics=sem, vmem_limit_bytes=VMEM_LIMIT)


def _rms(x, g):
    return x * lax.rsqrt(jnp.mean(x * x, axis=-1, keepdims=True) + EPS) * g


def _dot(a, b):
    return jnp.dot(a, b, preferred_element_type=F32)


def _dot_nt(a, b):
    return lax.dot_general(a, b, (((1,), (1,)), ((), ())), preferred_element_type=F32)


def _dot_tn(a, b):
    return lax.dot_general(a, b, (((0,), (0,)), ((), ())), preferred_element_type=F32)


def _memkv_kernel(mem_ref, g_ref, w_ref, k_ref, v_ref):
    mn = _rms(mem_ref[0], g_ref[...]).astype(BF16)
    kv = _dot(mn, w_ref[0].astype(BF16))
    k_ref[0, 0] = kv[:, :D_MODEL].astype(BF16)
    v_ref[0, 0] = kv[:, D_MODEL:].astype(BF16)


def _memkv(mem, g, w_xkv):
    depth = w_xkv.shape[0]
    b, m, d = mem.shape
    out = jax.ShapeDtypeStruct((depth, b, m, d), BF16)
    return pl.pallas_call(
        _memkv_kernel,
        grid=(depth, b),
        in_specs=[pl.BlockSpec((1, m, d), lambda l, i: (i, 0, 0)),
                  pl.BlockSpec((1, d), lambda l, i: (0, 0)),
                  pl.BlockSpec((1, d, 2 * d), lambda l, i: (l, 0, 0))],
        out_specs=[pl.BlockSpec((1, 1, m, d), lambda l, i: (l, i, 0, 0)),
                   pl.BlockSpec((1, 1, m, d), lambda l, i: (l, i, 0, 0))],
        out_shape=[out, out],
        compiler_params=_params("arbitrary", "arbitrary"),
        name="memkv",
    )(mem, g.reshape(1, d), w_xkv)


_R_AQ, _R_AK, _R_AV = 0, 512, 1024
_R_BQ, _R_BK, _R_BV = 1536, 1792, 2048
_R_ALPHA, _R_BR, _R_GATE, _R_END = 2560, 2576, 3088, 5136
_W_PIECE = 512
assert (_R_END - _R_GATE) % _W_PIECE == 0


def _inproj_kernel(layer, fused, *refs):
    if fused:
        x_ref, route_ref, ys_ref, g_ref, wt_hbm, wal2_ref, bal_ref = refs[:7]
        refs = refs[7:]
        x3_ref, refs = refs[0], refs[1:]
    else:
        x_ref, g_ref, wt_hbm, wal2_ref, bal_ref = refs[:5]
        refs = refs[5:]
    (aq_ref, ak_ref, av_ref, bq_ref, bk_ref, bv_ref, lga_ref, br_ref, gate_ref,
     wt_bf, wal2_bf, stage, sem) = refs

    @pl.when(pl.program_id(0) == 0)
    def _():
        pieces = [(c, min(_W_PIECE, _R_END - c)) for c in range(0, _R_END, _W_PIECE)]

        def piece_copy(p):
            c, n = pieces[p]
            return pltpu.make_async_copy(wt_hbm.at[layer, pl.ds(c, n), :], stage.at[p % 2, pl.ds(0, n), :],
                                         sem.at[p % 2])

        piece_copy(0).start()
        for p, (c, n) in enumerate(pieces):
            if p + 1 < len(pieces):
                piece_copy(p + 1).start()
            piece_copy(p).wait()
            wt_bf[c:c + n, :] = stage[p % 2, 0:n, :].astype(BF16)
        wal2_bf[...] = jnp.concatenate(
            [wal2_ref[0].astype(BF16), jnp.zeros((LANES - B_GATE_RANK, B_KEY_WIDTH), BF16)], axis=0)

    if fused:
        r = lax.broadcasted_iota(I32, (ROW_TILE, LOCAL_ROWS), 1).astype(F32)
        tiles = []
        for t in range(INPROJ_ROWS // ROW_TILE):
            rows = slice(t * ROW_TILE, (t + 1) * ROW_TILE)
            pos = route_ref[rows, :]
            sel = jnp.where(r == pos[:, 0:1], 1.0, jnp.where(r == pos[:, 1:2], 1.0, 0.0)).astype(BF16)
            ys = ys_ref[t * LOCAL_CHUNKS:(t + 1) * LOCAL_CHUNKS].reshape(LOCAL_ROWS, D_MODEL)
            tiles.append(x_ref[rows, :] + _dot(sel, ys))
        x = jnp.concatenate(tiles, axis=0)
        x3_ref[...] = x
    else:
        x = x_ref[...]
    h = _rms(x, g_ref[...]).astype(BF16)

    def mm(lo, hi):
        return _dot_nt(h, wt_bf[lo:hi, :])

    aq_ref[...] = (mm(_R_AQ, _R_AK) * (A_HEAD_DIM ** -0.5 * LOG2E)).astype(BF16)
    ak_ref[...] = mm(_R_AK, _R_AV).astype(BF16)
    av_ref[...] = mm(_R_AV, _R_BQ).astype(BF16)
    bq_ref[...] = (mm(_R_BQ, _R_BK) * (B_KEY_DIM ** -0.5)).astype(BF16)
    bk_ref[...] = mm(_R_BK, _R_BV).astype(BF16)
    bv_ref[...] = mm(_R_BV, _R_ALPHA).astype(BF16)
    r = mm(_R_BR, _R_GATE)
    br_ref[...] = (r * jax.nn.sigmoid(r)).astype(BF16)
    for c in range(_R_GATE, _R_END, _W_PIECE):
        gate_ref[:, c - _R_GATE:c - _R_GATE + _W_PIECE] = jax.nn.sigmoid(mm(c, c + _W_PIECE)).astype(BF16)
    z = _dot(mm(_R_ALPHA, _R_ALPHA + LANES).astype(BF16), wal2_bf[...]) + bal_ref[...]
    lga_ref[...] = (jnp.minimum(z, 0.0) - jnp.log(1.0 + jnp.exp(-jnp.abs(z)))) * (1.0 / B_GATE_TAU)


def _inproj(x, g, w_in, w_al2, b_al, layer, moe=None):
    t, d = x.shape
    assert w_in.shape[2] == _R_END
    row = lambda w: pl.BlockSpec((INPROJ_ROWS, w), lambda i: (i, 0))
    full = lambda a: pl.BlockSpec(a.shape, lambda i: (0,) * a.ndim)
    sds = lambda w, dt: jax.ShapeDtypeStruct((t, w), dt)
    widths = [(A_WIDTH, BF16), (A_WIDTH, BF16), (A_WIDTH, BF16), (B_KEY_WIDTH, BF16), (B_KEY_WIDTH, BF16),
              (B_VAL_WIDTH, BF16), (B_KEY_WIDTH, F32), (B_VAL_WIDTH, BF16), (_R_END - _R_GATE, BF16)]
    fused = moe is not None
    moe_specs, moe_out_specs, moe_out_shape = [], [], []
    if fused:
        n_chunks = INPROJ_ROWS // ROW_TILE * LOCAL_CHUNKS
        moe_specs = [row(LANES), pl.BlockSpec((n_chunks, CHUNK_ROWS, d), lambda i: (i, 0, 0))]
        moe_out_specs, moe_out_shape = [row(d)], [sds(d, F32)]
    return pl.pallas_call(
        functools.partial(_inproj_kernel, layer, fused),
        grid=(t // INPROJ_ROWS,),
        in_specs=[row(d)] + moe_specs + [full(g), pl.BlockSpec(memory_space=pl.ANY),
                                         pl.BlockSpec((1,) + w_al2.shape[1:], lambda i: (layer, 0, 0)),
                                         full(b_al)],
        out_specs=moe_out_specs + [row(w) for w, _ in widths],
        out_shape=moe_out_shape + [sds(w, dt) for w, dt in widths],
        scratch_shapes=[pltpu.VMEM((_R_END, d), BF16), pltpu.VMEM((LANES, B_KEY_WIDTH), BF16),
                        pltpu.VMEM((2, _W_PIECE, d), F32), pltpu.SemaphoreType.DMA((2,))],
        compiler_params=_params("arbitrary"),
        name="inproj",
    )(x, *(moe or ()), g, jnp.swapaxes(w_in, 1, 2), w_al2, b_al)


def _band_kernel(q_ref, *refs):
    n_win = BAND_TILES_PER_STEP + BAND_TILES - 1
    k_refs, v_refs = refs[:n_win], refs[n_win:2 * n_win]
    u_ref, o_ref, bias_ref = refs[2 * n_win:]
    i = pl.program_id(1)
    lane = lax.broadcasted_iota(I32, (1, LANES), 1)
    low = lane < A_HEAD_DIM
    ones = jnp.ones((BAND_KEYS, LANES), BF16)

    @pl.when((pl.program_id(0) == 0) & (i == 0))
    def _():
        cq = lax.broadcasted_iota(I32, (ROW_TILE, BAND_KEYS), 0) >> LOG_CHUNK
        ck = lax.broadcasted_iota(I32, (ROW_TILE, BAND_KEYS), 1) >> LOG_CHUNK
        valid = (ck >= cq) & (ck <= cq + A_LEFT_CHUNKS)
        for h in range(A_HEADS):
            rows = jnp.broadcast_to(u_ref[h:h + 1, :], (ROW_TILE, BIAS_PERIOD))
            rows = pltpu.roll(rows, BIAS_PERIOD - (ROW_TILE - 1), 1, stride=1, stride_axis=0)
            bias_ref[h // 2, (h % 2) * ROW_TILE:(h % 2 + 1) * ROW_TILE, :] = jnp.where(
                valid, rows[:, :BAND_KEYS], NEG)

    def attend(tile, n_missing):
        rows = pl.ds(tile * ROW_TILE, ROW_TILE)
        window = range(tile, tile + BAND_TILES)
        for p in range(A_HEADS // 2):
            sl = slice(p * LANES, (p + 1) * LANES)
            qp = q_ref[rows, sl]
            zero = jnp.zeros_like(qp)
            q2 = jnp.concatenate([jnp.where(low, qp, zero), jnp.where(low, zero, qp)], axis=0)
            kp = jnp.concatenate([k_refs[j][:, sl] for j in window], axis=0)
            vp = jnp.concatenate([v_refs[j][:, sl] for j in window], axis=0)
            s = _dot_nt(q2, kp) + bias_ref[p]
            if n_missing:
                col = lax.broadcasted_iota(I32, (1, BAND_KEYS), 1)
                s = s + jnp.where(col < n_missing * ROW_TILE, NEG, 0.0).astype(F32)
            pe = jnp.exp2(s - jnp.max(s, axis=-1, keepdims=True)).astype(BF16)
            o2 = _dot(pe, jnp.concatenate([vp, ones], axis=1))
            o = o2[:, :LANES] * (1.0 / o2[:, LANES:])
            o_ref[rows, sl] = jnp.where(low, o[:ROW_TILE], o[ROW_TILE:]).astype(BF16)

    @pl.when(i > 0)
    def _():
        for tile in range(BAND_TILES_PER_STEP):
            attend(tile, 0)

    @pl.when(i == 0)
    def _():
        for tile in range(BAND_TILES_PER_STEP):
            attend(tile, max(BAND_TILES - 1 - tile, 0))


def _band_bias_vector(rel_table):
    h = rel_table.shape[0]
    tab = rel_table.astype(F32) * LOG2E
    shift = A_LEFT_CHUNKS * CHUNK + ROW_TILE - 1
    n_far = shift - A_MAX_REL + 1
    span = ROW_TILE + BAND_KEYS - 1
    assert span - 1 - shift <= A_MAX_REL and span <= BIAS_PERIOD
    u = jnp.concatenate([jnp.broadcast_to(tab[:, 2 * A_MAX_REL:], (h, n_far)),
                         tab[:, 2 * A_MAX_REL - 1:2 * A_MAX_REL - 1 - (span - n_far):-1]], axis=1)
    return jnp.pad(u, ((0, 0), (0, BIAS_PERIOD - span)))


def _band_attention(q, k, v, u, batch):
    t, w = q.shape
    n = BAND_TILES_PER_STEP
    nb = t // batch // (n * ROW_TILE)
    step = pl.BlockSpec((n * ROW_TILE, w), lambda b, i: (b * nb + i, 0))
    tile = lambda j: pl.BlockSpec(
        (ROW_TILE, w), lambda b, i: (n * b * nb + jnp.maximum(n * i + j - (BAND_TILES - 1), 0), 0))
    window = [tile(j) for j in range(n + BAND_TILES - 1)]
    return pl.pallas_call(
        _band_kernel,
        grid=(batch, nb),
        in_specs=[step] + window + window + [pl.BlockSpec(u.shape, lambda b, i: (0, 0))],
        out_specs=step,
        out_shape=jax.ShapeDtypeStruct((t, w), BF16),
        scratch_shapes=[pltpu.VMEM((A_HEADS // 2, 2 * ROW_TILE, BAND_KEYS), F32)],
        compiler_params=_params("arbitrary", "arbitrary"),
        name="band_attn",
    )(q, *([k] * len(window)), *([v] * len(window)), u)


def _gla_constants():
    c = CHUNK
    t = np.arange(c)[:, None]
    r = np.arange(c)[None, :]
    mats = [(r <= t), (r > t)]
    lvl = np.full((c, c), -1, np.int32)
    lvl[np.arange(c), np.arange(c)] = N_LEVELS
    for l in range(N_LEVELS):
        m = (c // 2) >> l
        mid = (t // (2 * m)) * (2 * m) + m
        upper = t >= mid
        mats.append(np.where(upper, (r >= mid) & (r <= t), (r > t) & (r < mid)))
        s = r
        same = (s // (2 * m)) == (t // (2 * m))
        lvl[np.asarray(same & upper & (s < mid))] = l
    eye = np.eye(CHUNKS_PER_TILE)
    mexp = np.concatenate([np.kron(eye, m) for m in mats], axis=0).astype(np.float32)
    lvl = np.tile(lvl, (1, B_HEADS))
    return jnp.asarray(mexp, BF16), jnp.asarray(lvl, I32)


def _gla_kernel(q_ref, k_ref, v_ref, g_ref, r_ref, gn_ref, mexp_ref, lvl_ref, o_ref, s_ref):
    @pl.when(pl.program_id(1) == 0)
    def _():
        s_ref[...] = jnp.zeros_like(s_ref)

    kw = B_KEY_WIDTH
    ri = lax.broadcasted_iota(I32, (kw, kw), 0) >> LOG_CHUNK
    ci = lax.broadcasted_iota(I32, (kw, kw), 1) >> LOG_CHUNK
    bd = ri == ci
    head_ind = jnp.where(bd, 1.0, 0.0).astype(BF16)
    ri2 = lax.broadcasted_iota(I32, (kw, 2 * kw), 0) >> LOG_CHUNK
    ci2 = (lax.broadcasted_iota(I32, (kw, 2 * kw), 1) & (kw - 1)) >> LOG_CHUNK
    bd2 = ri2 == ci2
    lvl = lvl_ref[...]
    split_row = lax.broadcasted_iota(I32, (CHUNK_ROWS, kw), 0)
    ones = jnp.ones((CHUNK_ROWS, LANES), BF16)
    zero_b = jnp.zeros((kw, kw), BF16)
    chunks = [slice(c * CHUNK, (c + 1) * CHUNK) for c in range(CHUNKS_PER_TILE)]

    def head_blocks(x):
        return jnp.where(bd, jnp.concatenate([x] * B_HEADS, axis=0), zero_b)

    def prepare(tile):
        trows = pl.ds(tile * ROW_TILE, ROW_TILE)
        q = q_ref[trows, :].astype(F32)
        k = k_ref[trows, :].astype(F32)
        g = g_ref[trows, :]
        gb = g.astype(BF16)
        half = EXP_ROWS * CHUNKS_PER_TILE // 2
        w = jnp.exp(jnp.concatenate([_dot(mexp_ref[:half, :], gb), _dot(mexp_ref[half:, :], gb)], axis=0))
        qt = (q * w[0:ROW_TILE]).astype(BF16)
        kb = (k * w[ROW_TILE:2 * ROW_TILE]).astype(BF16)
        qk = (q * k).astype(BF16)

        attn = [jnp.zeros((CHUNK, kw), F32) for _ in chunks]
        for l in range(N_LEVELS):
            wl = w[(2 + l) * ROW_TILE:(3 + l) * ROW_TILE]
            qh = (q * wl).astype(BF16)
            kh = (k * wl).astype(BF16)
            for c, rows in enumerate(chunks):
                attn[c] = jnp.where(lvl == l, _dot_nt(qh[rows], head_blocks(kh[rows])), attn[c])

        out = []
        for c, rows in enumerate(chunks):
            a = jnp.where(lvl == N_LEVELS, _dot(qk[rows], head_ind), attn[c])
            v = v_ref[pl.ds(tile * ROW_TILE + c * CHUNK, CHUNK), :]
            vstack = jnp.concatenate([v[:, j * LANES:(j + 1) * LANES] for j in range(B_HEADS)], axis=0)
            kv = _dot_tn(head_blocks(kb[rows]), vstack)
            d = jnp.exp(jnp.sum(g[rows], axis=0, keepdims=True))
            d1 = d.astype(BF16).astype(F32)
            dp = jnp.where(split_row == 0, d1, jnp.where(split_row == 1, d - d1, 0.0)).astype(BF16)
            dcol = _dot_tn(dp, ones)
            out.append((a.astype(BF16), qt[rows], vstack, kv, dcol))
        return out

    prepared = [p for tile in range(GLA_TILES_PER_STEP) for p in prepare(tile)]

    s = s_ref[...]
    for c, (a, qtc, vstack, kv, dcol) in enumerate(prepared):
        rows = pl.ds(c * CHUNK, CHUNK)
        lhs = jnp.concatenate([a, qtc], axis=1)
        lhs = jnp.where(bd2, jnp.concatenate([lhs] * B_HEADS, axis=0), jnp.zeros((kw, 2 * kw), BF16))
        rhs = jnp.concatenate([vstack, s.astype(BF16)], axis=0)
        o = _dot(lhs, rhs)
        s = dcol * s + kv
        for j in range(B_HEADS):
            oj = o[j * CHUNK:(j + 1) * CHUNK]
            sl = slice(j * LANES, (j + 1) * LANES)
            y = oj * lax.rsqrt(jnp.mean(oj * oj, axis=-1, keepdims=True) + EPS) * gn_ref[...]
            o_ref[rows, sl] = (y * r_ref[rows, sl].astype(F32)).astype(BF16)
    s_ref[...] = s


def _gla(q, k, v, g, r, gn, batch):
    t = q.shape[0]
    nb = t // batch // (GLA_TILES_PER_STEP * ROW_TILE)
    mexp, lvl = _gla_constants()
    cur = lambda b, i: (b * nb + i, 0)
    blk = lambda w: pl.BlockSpec((GLA_TILES_PER_STEP * ROW_TILE, w), cur)
    full = lambda a: pl.BlockSpec(a.shape, lambda b, i: (0,) * a.ndim)
    return pl.pallas_call(
        _gla_kernel,
        grid=(batch, nb),
        in_specs=[blk(B_KEY_WIDTH), blk(B_KEY_WIDTH), blk(B_VAL_WIDTH), blk(B_KEY_WIDTH), blk(B_VAL_WIDTH),
                  full(gn), full(mexp), full(lvl)],
        out_specs=blk(B_VAL_WIDTH),
        out_shape=jax.ShapeDtypeStruct((t, B_VAL_WIDTH), BF16),
        scratch_shapes=[pltpu.VMEM((B_KEY_WIDTH, B_VAL_DIM), F32)],
        compiler_params=_params("arbitrary", "arbitrary"),
        name="gla",
    )(q, k, v, g, r, gn, mexp, lvl)


def _token_kernel(x_ref, oa_ref, ob_ref, gate_ref, wb0_ref, wb1_ref, wmix_ref, gx_ref, wq_ref,
                  km_ref, vm_ref, wo_ref, gf_ref, wr_ref, br_ref, ltri_ref, utri_ref,
                  x2_ref, hs_ref, route_ref, cnt_ref):
    ma = _dot(oa_ref[...], wb0_ref[...])
    mb = _dot(ob_ref[...], wb1_ref[...])
    merged = (gate_ref[:, :D_MODEL].astype(F32) * ma + gate_ref[:, D_MODEL:].astype(F32) * mb).astype(BF16)
    x1 = x_ref[...] + _dot(merged, wmix_ref[...])

    h2 = _rms(x1, gx_ref[...]).astype(BF16)
    qx = (_dot(h2, wq_ref[...]) * (X_HEAD_DIM ** -0.5)).astype(BF16)
    heads = []
    for h in range(X_HEADS):
        sl = slice(h * X_HEAD_DIM, (h + 1) * X_HEAD_DIM)
        s = _dot_nt(qx[:, sl], km_ref[0, :, sl])
        m = jnp.max(s, axis=-1, keepdims=True)
        pe = jnp.exp(s - m)
        l = jnp.sum(pe, axis=-1, keepdims=True)
        heads.append((_dot(pe.astype(BF16), vm_ref[0, :, sl]) * (1.0 / l)).astype(BF16))
    x2 = x1 + _dot(jnp.concatenate(heads, axis=1), wo_ref[...])
    x2_ref[...] = x2

    h3 = _rms(x2, gf_ref[...])

    h3_hi = h3.astype(BF16)
    h3_lo = (h3 - h3_hi.astype(F32)).astype(BF16)
    hw = _dot(h3_hi, wr_ref[...])
    logits = hw[:, :LANES] + hw[:, LANES:] + _dot(h3_lo, wr_ref[:, :LANES]) + br_ref[...]
    oh0, oh1, g0, g1 = _route(logits)
    for h in range(TOKEN_TILES_PER_STEP):
        rows = slice(h * ROW_TILE, (h + 1) * ROW_TILE)
        chunks = pl.ds(h * LOCAL_CHUNKS, LOCAL_CHUNKS)
        _sort_tile(oh0[rows], oh1[rows], g0[rows], g1[rows], h3_hi[rows], ltri_ref, utri_ref,
                   hs_ref.at[chunks], route_ref.at[pl.ds(h * ROW_TILE, ROW_TILE)], cnt_ref.at[h])


def _route(logits):
    lane = lax.broadcasted_iota(I32, logits.shape, 1).astype(F32)
    big = jnp.float32(LANES)
    gl = jnp.where(lane < N_GROUPS, logits, NEG)
    gmax = jnp.max(gl, axis=-1, keepdims=True)
    gidx = jnp.min(jnp.where(gl == gmax, lane, big), axis=-1, keepdims=True)
    g_w = 1.0 / jnp.sum(jnp.exp(gl - gmax), axis=-1, keepdims=True)
    lo = N_GROUPS + EXPERTS_PER_GROUP * gidx
    el = jnp.where((lane >= lo) & (lane < lo + EXPERTS_PER_GROUP), logits, NEG)
    v1 = jnp.max(el, axis=-1, keepdims=True)
    i1 = jnp.min(jnp.where(el == v1, lane, big), axis=-1, keepdims=True)
    el2 = jnp.where(lane == i1, NEG, el)
    v2 = jnp.max(el2, axis=-1, keepdims=True)
    i2 = jnp.min(jnp.where(el2 == v2, lane, big), axis=-1, keepdims=True)
    e21 = jnp.exp(v2 - v1)
    w1 = g_w / (1.0 + e21)

    def gate_cols(w):
        hi = w.astype(BF16).astype(F32)
        return jnp.where(lane == 0, hi, jnp.where(lane == 1, w - hi, 0.0)).astype(BF16)

    oh0 = jnp.where(lane == i1 - N_GROUPS, 1.0, 0.0)
    oh1 = jnp.where(lane == i2 - N_GROUPS, 1.0, 0.0)
    return oh0, oh1, gate_cols(w1), gate_cols(w1 * e21)


def _sort_tile(oh0, oh1, g0, g1, h3_hi, ltri_ref, utri_ref, hs_ref, route_ref, cnt_ref):
    lane = lax.broadcasted_iota(I32, oh0.shape, 1)
    oh = oh0 + oh1
    nch = jnp.floor((jnp.sum(oh, axis=0, keepdims=True) + (CHUNK_ROWS - 1)) * (1.0 / CHUNK_ROWS))
    nch8 = jnp.broadcast_to(nch, (8, LANES))
    start = _dot(nch8.astype(BF16), utri_ref[...])[0:1] * CHUNK_ROWS
    rank = _dot(ltri_ref[...], oh.astype(BF16))
    row = start + rank
    pos0 = jnp.sum(row * oh0, axis=-1, keepdims=True)
    pos1 = jnp.sum(row * oh1, axis=-1, keepdims=True)
    route = jnp.where(lane == 0, pos0, jnp.where(lane == 1, pos1, 0.0))
    route_t = jnp.transpose(route)
    r = lax.broadcasted_iota(I32, (LOCAL_ROWS, ROW_TILE), 0).astype(F32)
    p0 = jnp.where(r == route_t[0:1, :], 1.0, 0.0).astype(BF16)
    p1 = jnp.where(r == route_t[1:2, :], 1.0, 0.0).astype(BF16)
    sorted_rows = jnp.concatenate([_dot(p0 + p1, h3_hi), _dot(p0, g0) + _dot(p1, g1)], axis=1)
    hs_ref[...] = sorted_rows.astype(BF16).reshape(hs_ref.shape)
    route_ref[...] = route
    cnt_ref[...] = nch8


def _token(x, oa, ob, gates, wb0, wb1, wmix, gx, wq, km, vm, wo, gf, wr, br, batch):
    t, d = x.shape
    n = TOKEN_TILES_PER_STEP
    nb = t // batch // (n * ROW_TILE)
    nt = t // ROW_TILE
    ltri = jnp.asarray(np.tril(np.ones((ROW_TILE, ROW_TILE), np.float32), -1), BF16)
    utri = jnp.asarray(np.triu(np.ones((LANES, LANES), np.float32), 1), BF16)
    cur = lambda b, i: (b * nb + i, 0)
    cur3 = lambda b, i: (b * nb + i, 0, 0)
    blk = lambda w: pl.BlockSpec((n * ROW_TILE, w), cur)
    full = lambda a: pl.BlockSpec(a.shape, lambda b, i: (0,) * a.ndim)
    mem = pl.BlockSpec((1,) + km.shape[1:], lambda b, i: (b, 0, 0))
    return pl.pallas_call(
        _token_kernel,
        grid=(batch, nb),
        in_specs=[blk(d), blk(A_WIDTH), blk(B_VAL_WIDTH), blk(2 * d), full(wb0), full(wb1), full(wmix),
                  full(gx), full(wq), mem, mem, full(wo), full(gf), full(wr), full(br), full(ltri), full(utri)],
        out_specs=[blk(d), pl.BlockSpec((n * LOCAL_CHUNKS, CHUNK_ROWS, SORT_WIDTH), cur3),
                   blk(LANES), pl.BlockSpec((n, 8, LANES), cur3)],
        out_shape=[jax.ShapeDtypeStruct((t, d), F32),
                   jax.ShapeDtypeStruct((nt * LOCAL_CHUNKS, CHUNK_ROWS, SORT_WIDTH), BF16),
                   jax.ShapeDtypeStruct((t, LANES), F32),
                   jax.ShapeDtypeStruct((nt, 8, LANES), F32)],
        compiler_params=_params("arbitrary", "arbitrary"),
        name="token",
    )(x, oa, ob, gates, wb0, wb1, wmix, gx, wq, km, vm, wo, gf, wr, br, ltri, utri)


def _expert_kernel(layer, te_ref, nu_ref, nv_ref, ch_ref, first_ref, nxt_ref,
                   hs_hbm, wg_hbm, wu_hbm, wd_hbm, ys_hbm,
                   xbuf, ybuf, wg_st, wu_st, wd_st, wgu_bf, wd_bf, gsem, ssem, wsem):
    n_used = nu_ref[0]
    ring = GATHER_AHEAD + 1
    last_tile = te_ref.shape[0] - 1

    def weights(expert, s, start):
        for src, dst in ((wg_hbm, wg_st), (wu_hbm, wu_st), (wd_hbm, wd_st)):
            cp = pltpu.make_async_copy(src.at[layer * N_EXPERTS + expert], dst.at[s], wsem.at[s])
            cp.start() if start else cp.wait()

    def for_chunks(tile, fn):
        nv = nv_ref[tile]

        @pl.when(nv == TILE_CHUNKS)
        def _():
            for c in range(TILE_CHUNKS):
                fn(c)

        @pl.when(nv != TILE_CHUNKS)
        def _():
            def body(c, carry):
                fn(c)
                return carry

            lax.fori_loop(0, nv, body, 0)

    def gather(step, start):
        tile = jnp.minimum(step, last_tile)
        s = lax.rem(step, ring)
        for c in range(TILE_CHUNKS):
            cp = pltpu.make_async_copy(hs_hbm.at[ch_ref[tile * TILE_CHUNKS + c]], xbuf.at[s, c], gsem.at[s])
            cp.start() if start else cp.wait()

    def scatter(tile, s, start):
        def one(c):
            cp = pltpu.make_async_copy(ybuf.at[s, c], ys_hbm.at[ch_ref[tile * TILE_CHUNKS + c]], ssem.at[s])
            cp.start(priority=1) if start else cp.wait()

        for_chunks(tile, one)

    weights(te_ref[0], 0, True)
    for step in range(GATHER_AHEAD):
        gather(step, True)

    def tile_body(i, run):
        slot = lax.rem(i, 2)
        gather(i, False)

        @pl.when(i >= 2)
        def _():
            scatter(i - 2, slot, False)

        @pl.when(first_ref[i] == 1)
        def _():
            s = lax.rem(run, 2)
            weights(te_ref[i], s, False)

            @pl.when(nxt_ref[i] >= 0)
            def _():
                weights(nxt_ref[i], 1 - s, True)

            wgu_bf[:, :EXPERT_FF] = wg_st[s].astype(BF16)
            wgu_bf[:, EXPERT_FF:] = wu_st[s].astype(BF16)
            wd_bf[...] = wd_st[s].astype(BF16)

        xg = xbuf[lax.rem(i, ring)].reshape(EXPERT_ROWS, SORT_WIDTH)
        hgu = _dot(xg[:, :D_MODEL], wgu_bf[...])
        gather(i + GATHER_AHEAD, True)
        hg, hu = hgu[:, :EXPERT_FF], hgu[:, EXPERT_FF:]
        hid = (hg * jax.nn.sigmoid(hg) * hu).astype(BF16)
        g = xg[:, D_MODEL:].astype(F32)
        y = ((g[:, 0:1] + g[:, 1:2]) * _dot(hid, wd_bf[...])).astype(BF16)
        y = jnp.concatenate([y, jnp.zeros((EXPERT_ROWS, LANES), BF16)], axis=1)
        ybuf[slot] = y.reshape(TILE_CHUNKS, CHUNK_ROWS, SORT_WIDTH)
        scatter(i, slot, True)
        return run + first_ref[i]

    lax.fori_loop(0, n_used, tile_body, jnp.int32(0))

    last = n_used - 1
    for ahead in range(1, GATHER_AHEAD + 1):
        gather(last + ahead, False)
    scatter(last, lax.rem(last, 2), False)

    @pl.when(last >= 1)
    def _():
        scatter(last - 1, lax.rem(last - 1, 2), False)


def _experts(hs, tile_expert, n_used, n_valid, chunks, run_first, run_next, wg, wu, wd, layer):
    anyspace = pl.BlockSpec(memory_space=pl.ANY)
    grid_spec = pltpu.PrefetchScalarGridSpec(
        num_scalar_prefetch=6,
        grid=(1,),
        in_specs=[anyspace] * 4,
        out_specs=anyspace,
        scratch_shapes=[pltpu.VMEM((GATHER_AHEAD + 1, TILE_CHUNKS, CHUNK_ROWS, SORT_WIDTH), BF16),
                        pltpu.VMEM((2, TILE_CHUNKS, CHUNK_ROWS, SORT_WIDTH), BF16),
                        pltpu.VMEM((2, D_MODEL, EXPERT_FF), F32), pltpu.VMEM((2, D_MODEL, EXPERT_FF), F32),
                        pltpu.VMEM((2, EXPERT_FF, D_MODEL), F32),
                        pltpu.VMEM((D_MODEL, 2 * EXPERT_FF), BF16), pltpu.VMEM((EXPERT_FF, D_MODEL), BF16),
                        pltpu.SemaphoreType.DMA((GATHER_AHEAD + 1,)), pltpu.SemaphoreType.DMA((2,)),
                        pltpu.SemaphoreType.DMA((2,))],
    )
    return pl.pallas_call(
        functools.partial(_expert_kernel, layer),
        grid_spec=grid_spec,
        out_shape=jax.ShapeDtypeStruct(hs.shape, BF16),
        input_output_aliases={6: 0},
        compiler_params=_params("arbitrary"),
        name="experts",
    )(tile_expert, n_used, n_valid, chunks, run_first, run_next, hs, wg, wu, wd)


def _combine_kernel(x_ref, route_ref, ys_ref, gfin_ref, o_ref):
    r = lax.broadcasted_iota(I32, (ROW_TILE, LOCAL_ROWS), 1).astype(F32)
    for t in range(COMBINE_TILES_PER_STEP):
        rows = slice(t * ROW_TILE, (t + 1) * ROW_TILE)
        pos = route_ref[rows, :]
        sel = jnp.where(r == pos[:, 0:1], 1.0, jnp.where(r == pos[:, 1:2], 1.0, 0.0)).astype(BF16)
        ys = ys_ref[t * LOCAL_CHUNKS:(t + 1) * LOCAL_CHUNKS].reshape(LOCAL_ROWS, D_MODEL)
        o_ref[rows, :] = _rms(x_ref[rows, :] + _dot(sel, ys), gfin_ref[...])


def _combine(x2, route, ys, gfin):
    t, d = x2.shape
    n = COMBINE_TILES_PER_STEP
    return pl.pallas_call(
        _combine_kernel,
        grid=(t // (n * ROW_TILE),),
        in_specs=[pl.BlockSpec((n * ROW_TILE, d), lambda i: (i, 0)),
                  pl.BlockSpec((n * ROW_TILE, LANES), lambda i: (i, 0)),
                  pl.BlockSpec((n * LOCAL_CHUNKS, CHUNK_ROWS, d), lambda i: (i, 0, 0)),
                  pl.BlockSpec((1, d), lambda i: (0, 0))],
        out_specs=pl.BlockSpec((n * ROW_TILE, d), lambda i: (i, 0)),
        out_shape=jax.ShapeDtypeStruct((t, d), F32),
        compiler_params=_params("arbitrary"),
        name="combine",
    )(x2, route, ys, gfin)


def _chunk_plan(nch, n_tiles):
    nt = nch.shape[0]
    local_start = jnp.cumsum(nch, axis=1) - nch
    cum = jnp.cumsum(nch, axis=0)
    total = cum[-1]
    tiles = (total + TILE_CHUNKS - 1) // TILE_CHUNKS
    tile_end = jnp.cumsum(tiles)
    n_used = tile_end[-1:]
    tile_ids = jnp.arange(n_tiles, dtype=I32)
    tile_expert = jnp.minimum(jnp.sum((tile_end[None, :] <= tile_ids[:, None]).astype(I32), axis=1),
                              N_EXPERTS - 1)
    sel = (tile_expert[:, None] == jnp.arange(N_EXPERTS, dtype=I32)[None, :]).astype(I32)
    pick = lambda table: jnp.sum(sel[:, :, None] * table.T[None, :, :], axis=1)
    first_tile = jnp.sum(sel * (tile_end - tiles)[None, :], axis=1)
    slot = (tile_ids - first_tile)[:, None] * TILE_CHUNKS + jnp.arange(TILE_CHUNKS, dtype=I32)[None, :]
    valid = (slot < jnp.sum(sel * total[None, :], axis=1)[:, None]) & (tile_ids < n_used)[:, None]
    src_tile = jnp.sum((pick(cum)[:, None, :] <= slot[:, :, None]).astype(I32), axis=2)
    src_tile = jnp.minimum(src_tile, nt - 1)
    at = (src_tile[:, :, None] == jnp.arange(nt, dtype=I32)[None, None, :]).astype(I32)
    before = jnp.sum(at * pick(cum - nch)[:, None, :], axis=2)
    start = jnp.sum(at * pick(local_start)[:, None, :], axis=2)
    chunk = jnp.where(valid, src_tile * LOCAL_CHUNKS + start + slot - before, LOCAL_CHUNKS - 1)
    used = tile_ids < n_used
    run_first = ((tile_ids == first_tile) & used).astype(I32)
    run_end = jnp.sum(sel * tile_end[None, :], axis=1)
    next_expert = jnp.sum((run_end[:, None] == tile_ids[None, :]).astype(I32) * tile_expert[None, :], axis=1)
    run_next = jnp.where(used & (run_end < n_used), next_expert, -1)
    return tile_expert, n_used, jnp.sum(valid.astype(I32), axis=1), chunk.reshape(-1), run_first, run_next


def kernel(x, mem, norm_mix_g, w_in, rel_bias, gla_w_alpha, gla_b_alpha, gla_norm_g, w_branch, w_mix_out, norm_x_g, mem_norm_g, w_xq, w_xkv, w_xo, norm_ffn_g, w_group_router, b_group_router, w_expert_router, b_expert_router, w_exp_gate, w_exp_up, w_exp_down, final_norm_g):
    batch, seq, d = x.shape
    depth = w_in.shape[0]
    t = batch * seq
    step_tiles = max(TOKEN_TILES_PER_STEP, GLA_TILES_PER_STEP, BAND_TILES_PER_STEP, COMBINE_TILES_PER_STEP)
    assert d == D_MODEL and seq % (step_tiles * ROW_TILE) == 0 and seq % INPROJ_ROWS == 0
    nt = t // ROW_TILE
    n_tiles = nt * LOCAL_CHUNKS // TILE_CHUNKS + N_EXPERTS

    xf = x.reshape(t, d)
    km_all, vm_all = _memkv(mem, mem_norm_g, w_xkv)
    row = lambda a: a.reshape(1, -1).astype(F32)

    moe = None
    for l in range(depth):
        res = _inproj(xf, row(norm_mix_g[l]), w_in, gla_w_alpha, row(gla_b_alpha[l]), l, moe)
        if moe is not None:
            xf, res = res[0], res[1:]
        aq, ak, av, bq, bk, bv, lga, br, gates = res

        oa = _band_attention(aq, ak, av, _band_bias_vector(rel_bias[l]), batch)
        ob = _gla(bq, bk, bv, lga, br, row(gla_norm_g[l]), batch)

        wr = jnp.pad(jnp.concatenate([w_group_router[l], w_expert_router[l]], axis=1).astype(F32),
                     ((0, 0), (0, LANES - N_GROUPS - N_EXPERTS)))
        wr_hi = wr.astype(BF16)
        wr = jnp.concatenate([wr_hi, (wr - wr_hi.astype(F32)).astype(BF16)], axis=1)
        brt = jnp.pad(jnp.concatenate([b_group_router[l], b_expert_router[l]]).astype(F32),
                      (0, LANES - N_GROUPS - N_EXPERTS)).reshape(1, LANES)
        x2, hs, route, cnt = _token(
            xf, oa, ob, gates, w_branch[l, 0].astype(BF16), w_branch[l, 1].astype(BF16),
            w_mix_out[l].astype(BF16), row(norm_x_g[l]), w_xq[l].astype(BF16), km_all[l], vm_all[l],
            w_xo[l].astype(BF16), row(norm_ffn_g[l]), wr, brt, batch)

        plan = _chunk_plan(cnt[:, 0, :N_EXPERTS].astype(I32), n_tiles)
        e3 = lambda w: w.reshape((depth * N_EXPERTS,) + w.shape[3:])
        ys = _experts(hs, *plan, e3(w_exp_gate), e3(w_exp_up), e3(w_exp_down), l)
        xf, moe = x2, (route, ys)

    return _combine(x2, route, ys, row(final_norm_g)).reshape(batch, seq, d)
```

```python
import functools

import numpy as np
import jax
import jax.numpy as jnp
from jax import lax
from jax.experimental import pallas as pl
from jax.experimental.pallas import tpu as pltpu

F32 = jnp.float32
BF16 = jnp.bfloat16
I32 = jnp.int32

D_MODEL = 1024
CHUNK = 64
EPS = 1e-6
A_HEADS = 8
A_HEAD_DIM = 64
A_WIDTH = 512
A_LEFT_CHUNKS = 8
A_MAX_REL = 256
B_HEADS = 4
B_KEY_DIM = 64
B_VAL_DIM = 128
B_KEY_WIDTH = 256
B_VAL_WIDTH = 512
B_GATE_RANK = 16
B_GATE_TAU = 16.0
X_HEADS = 4
X_HEAD_DIM = 256
N_GROUPS = 4
EXPERTS_PER_GROUP = 8
N_EXPERTS = N_GROUPS * EXPERTS_PER_GROUP
EXPERT_FF = 256

LANES = 128
ROW_TILE = 256
CHUNKS_PER_TILE = ROW_TILE // CHUNK
BAND_TILES = A_LEFT_CHUNKS // CHUNKS_PER_TILE + 1
BAND_KEYS = BAND_TILES * ROW_TILE
BIAS_PERIOD = 1024
LOG_CHUNK = 6
N_LEVELS = LOG_CHUNK
EXP_ROWS = (2 + N_LEVELS) * CHUNK
CHUNK_ROWS = 16
TOKEN_TILES_PER_STEP = 2
INPROJ_ROWS = 512
COMBINE_TILES_PER_STEP = 4
GLA_TILES_PER_STEP = 8
BAND_TILES_PER_STEP = 4
EXPERT_ROWS = 512
GATHER_AHEAD = 4
TILE_CHUNKS = EXPERT_ROWS // CHUNK_ROWS
LOCAL_CHUNKS = 2 * ROW_TILE // CHUNK_ROWS + N_EXPERTS
LOCAL_ROWS = LOCAL_CHUNKS * CHUNK_ROWS
assert (2 * ROW_TILE + N_EXPERTS * (CHUNK_ROWS - 1)) // CHUNK_ROWS < LOCAL_CHUNKS
SORT_WIDTH = D_MODEL + LANES
NEG = -1e30
LOG2E = 1.4426950408889634
VMEM_LIMIT = 56 * 1024 * 1024


def _params(*sem):
    return pltpu.CompilerParams(dimension_semantics=sem, vmem_limit_bytes=VMEM_LIMIT)


def _rms(x, g):
    return x * lax.rsqrt(jnp.mean(x * x, axis=-1, keepdims=True) + EPS) * g


def _dot(a, b):
    return jnp.dot(a, b, preferred_element_type=F32)


def _dot_nt(a, b):
    return lax.dot_general(a, b, (((1,), (1,)), ((), ())), preferred_element_type=F32)


def _dot_tn(a, b):
    return lax.dot_general(a, b, (((0,), (0,)), ((), ())), preferred_element_type=F32)


def _memkv_kernel(mem_ref, g_ref, w_ref, k_ref, v_ref):
    mn = _rms(mem_ref[0], g_ref[...]).astype(BF16)
    kv = _dot(mn, w_ref[0].astype(BF16))
    k_ref[0, 0] = kv[:, :D_MODEL].astype(BF16)
    v_ref[0, 0] = kv[:, D_MODEL:].astype(BF16)


def _memkv(mem, g, w_xkv):
    depth = w_xkv.shape[0]
    b, m, d = mem.shape
    out = jax.ShapeDtypeStruct((depth, b, m, d), BF16)
    return pl.pallas_call(
        _memkv_kernel,
        grid=(depth, b),
        in_specs=[pl.BlockSpec((1, m, d), lambda l, i: (i, 0, 0)),
                  pl.BlockSpec((1, d), lambda l, i: (0, 0)),
                  pl.BlockSpec((1, d, 2 * d), lambda l, i: (l, 0, 0))],
        out_specs=[pl.BlockSpec((1, 1, m, d), lambda l, i: (l, i, 0, 0)),
                   pl.BlockSpec((1, 1, m, d), lambda l, i: (l, i, 0, 0))],
        out_shape=[out, out],
        compiler_params=_params("arbitrary", "arbitrary"),
        name="memkv",
    )(mem, g.reshape(1, d), w_xkv)


_R_AQ, _R_AK, _R_AV = 0, 512, 1024
_R_BQ, _R_BK, _R_BV = 1536, 1792, 2048
_R_ALPHA, _R_BR, _R_GATE, _R_END = 2560, 2576, 3088, 5136
_W_PIECE = 512
assert (_R_END - _R_GATE) % _W_PIECE == 0


def _inproj_kernel(layer, fused, *refs):
    if fused:
        x_ref, route_ref, ys_ref, g_ref, wt_hbm, wal2_ref, bal_ref = refs[:7]
        refs = refs[7:]
        x3_ref, refs = refs[0], refs[1:]
    else:
        x_ref, g_ref, wt_hbm, wal2_ref, bal_ref = refs[:5]
        refs = refs[5:]
    (aq_ref, ak_ref, av_ref, bq_ref, bk_ref, bv_ref, lga_ref, br_ref, gate_ref,
     wt_bf, wal2_bf, stage, sem) = refs

    @pl.when(pl.program_id(0) == 0)
    def _():
        pieces = [(c, min(_W_PIECE, _R_END - c)) for c in range(0, _R_END, _W_PIECE)]

        def piece_copy(p):
            c, n = pieces[p]
            return pltpu.make_async_copy(wt_hbm.at[layer, pl.ds(c, n), :], stage.at[p % 2, pl.ds(0, n), :],
                                         sem.at[p % 2])

        piece_copy(0).start()
        for p, (c, n) in enumerate(pieces):
            if p + 1 < len(pieces):
                piece_copy(p + 1).start()
            piece_copy(p).wait()
            wt_bf[c:c + n, :] = stage[p % 2, 0:n, :].astype(BF16)
        wal2_bf[...] = jnp.concatenate(
            [wal2_ref[0].astype(BF16), jnp.zeros((LANES - B_GATE_RANK, B_KEY_WIDTH), BF16)], axis=0)

    if fused:
        r = lax.broadcasted_iota(I32, (ROW_TILE, LOCAL_ROWS), 1).astype(F32)
        tiles = []
        for t in range(INPROJ_ROWS // ROW_TILE):
            rows = slice(t * ROW_TILE, (t + 1) * ROW_TILE)
            pos = route_ref[rows, :]
            sel = jnp.where(r == pos[:, 0:1], 1.0, jnp.where(r == pos[:, 1:2], 1.0, 0.0)).astype(BF16)
            ys = ys_ref[t * LOCAL_CHUNKS:(t + 1) * LOCAL_CHUNKS].reshape(LOCAL_ROWS, D_MODEL)
            tiles.append(x_ref[rows, :] + _dot(sel, ys))
        x = jnp.concatenate(tiles, axis=0)
        x3_ref[...] = x
    else:
        x = x_ref[...]
    h = _rms(x, g_ref[...]).astype(BF16)

    def mm(lo, hi):
        return _dot_nt(h, wt_bf[lo:hi, :])

    aq_ref[...] = (mm(_R_AQ, _R_AK) * (A_HEAD_DIM ** -0.5 * LOG2E)).astype(BF16)
    ak_ref[...] = mm(_R_AK, _R_AV).astype(BF16)
    av_ref[...] = mm(_R_AV, _R_BQ).astype(BF16)
    bq_ref[...] = (mm(_R_BQ, _R_BK) * (B_KEY_DIM ** -0.5)).astype(BF16)
    bk_ref[...] = mm(_R_BK, _R_BV).astype(BF16)
    bv_ref[...] = mm(_R_BV, _R_ALPHA).astype(BF16)
    r = mm(_R_BR, _R_GATE)
    br_ref[...] = (r * jax.nn.sigmoid(r)).astype(BF16)
    for c in range(_R_GATE, _R_END, _W_PIECE):
        gate_ref[:, c - _R_GATE:c - _R_GATE + _W_PIECE] = jax.nn.sigmoid(mm(c, c + _W_PIECE)).astype(BF16)
    z = _dot(mm(_R_ALPHA, _R_ALPHA + LANES).astype(BF16), wal2_bf[...]) + bal_ref[...]
    lga_ref[...] = (jnp.minimum(z, 0.0) - jnp.log(1.0 + jnp.exp(-jnp.abs(z)))) * (1.0 / B_GATE_TAU)


def _inproj(x, g, w_in, w_al2, b_al, layer, moe=None):
    t, d = x.shape
    assert w_in.shape[2] == _R_END
    row = lambda w: pl.BlockSpec((INPROJ_ROWS, w), lambda i: (i, 0))
    full = lambda a: pl.BlockSpec(a.shape, lambda i: (0,) * a.ndim)
    sds = lambda w, dt: jax.ShapeDtypeStruct((t, w), dt)
    widths = [(A_WIDTH, BF16), (A_WIDTH, BF16), (A_WIDTH, BF16), (B_KEY_WIDTH, BF16), (B_KEY_WIDTH, BF16),
              (B_VAL_WIDTH, BF16), (B_KEY_WIDTH, F32), (B_VAL_WIDTH, BF16), (_R_END - _R_GATE, BF16)]
    fused = moe is not None
    moe_specs, moe_out_specs, moe_out_shape = [], [], []
    if fused:
        n_chunks = INPROJ_ROWS // ROW_TILE * LOCAL_CHUNKS
        moe_specs = [row(LANES), pl.BlockSpec((n_chunks, CHUNK_ROWS, d), lambda i: (i, 0, 0))]
        moe_out_specs, moe_out_shape = [row(d)], [sds(d, F32)]
    return pl.pallas_call(
        functools.partial(_inproj_kernel, layer, fused),
        grid=(t // INPROJ_ROWS,),
        in_specs=[row(d)] + moe_specs + [full(g), pl.BlockSpec(memory_space=pl.ANY),
                                         pl.BlockSpec((1,) + w_al2.shape[1:], lambda i: (layer, 0, 0)),
                                         full(b_al)],
        out_specs=moe_out_specs + [row(w) for w, _ in widths],
        out_shape=moe_out_shape + [sds(w, dt) for w, dt in widths],
        scratch_shapes=[pltpu.VMEM((_R_END, d), BF16), pltpu.VMEM((LANES, B_KEY_WIDTH), BF16),
                        pltpu.VMEM((2, _W_PIECE, d), F32), pltpu.SemaphoreType.DMA((2,))],
        compiler_params=_params("arbitrary"),
        name="inproj",
    )(x, *(moe or ()), g, jnp.swapaxes(w_in, 1, 2), w_al2, b_al)


def _band_kernel(q_ref, *refs):
    n_win = BAND_TILES_PER_STEP + BAND_TILES - 1
    k_refs, v_refs = refs[:n_win], refs[n_win:2 * n_win]
    u_ref, o_ref, bias_ref = refs[2 * n_win:]
    i = pl.program_id(1)
    lane = lax.broadcasted_iota(I32, (1, LANES), 1)
    low = lane < A_HEAD_DIM
    ones = jnp.ones((BAND_KEYS, LANES), BF16)

    @pl.when((pl.program_id(0) == 0) & (i == 0))
    def _():
        cq = lax.broadcasted_iota(I32, (ROW_TILE, BAND_KEYS), 0) >> LOG_CHUNK
        ck = lax.broadcasted_iota(I32, (ROW_TILE, BAND_KEYS), 1) >> LOG_CHUNK
        valid = (ck >= cq) & (ck <= cq + A_LEFT_CHUNKS)
        for h in range(A_HEADS):
            rows = jnp.broadcast_to(u_ref[h:h + 1, :], (ROW_TILE, BIAS_PERIOD))
            rows = pltpu.roll(rows, BIAS_PERIOD - (ROW_TILE - 1), 1, stride=1, stride_axis=0)
            bias_ref[h // 2, (h % 2) * ROW_TILE:(h % 2 + 1) * ROW_TILE, :] = jnp.where(
                valid, rows[:, :BAND_KEYS], NEG)

    def attend(tile, n_missing):
        rows = pl.ds(tile * ROW_TILE, ROW_TILE)
        window = range(tile, tile + BAND_TILES)
        for p in range(A_HEADS // 2):
            sl = slice(p * LANES, (p + 1) * LANES)
            qp = q_ref[rows, sl]
            zero = jnp.zeros_like(qp)
            q2 = jnp.concatenate([jnp.where(low, qp, zero), jnp.where(low, zero, qp)], axis=0)
            kp = jnp.concatenate([k_refs[j][:, sl] for j in window], axis=0)
            vp = jnp.concatenate([v_refs[j][:, sl] for j in window], axis=0)
            s = _dot_nt(q2, kp) + bias_ref[p]
            if n_missing:
                col = lax.broadcasted_iota(I32, (1, BAND_KEYS), 1)
                s = s + jnp.where(col < n_missing * ROW_TILE, NEG, 0.0).astype(F32)
            pe = jnp.exp2(s - jnp.max(s, axis=-1, keepdims=True)).astype(BF16)
            o2 = _dot(pe, jnp.concatenate([vp, ones], axis=1))
            o = o2[:, :LANES] * (1.0 / o2[:, LANES:])
            o_ref[rows, sl] = jnp.where(low, o[:ROW_TILE], o[ROW_TILE:]).astype(BF16)

    @pl.when(i > 0)
    def _():
        for tile in range(BAND_TILES_PER_STEP):
            attend(tile, 0)

    @pl.when(i == 0)
    def _():
        for tile in range(BAND_TILES_PER_STEP):
            attend(tile, max(BAND_TILES - 1 - tile, 0))


def _band_bias_vector(rel_table):
    h = rel_table.shape[0]
    tab = rel_table.astype(F32) * LOG2E
    shift = A_LEFT_CHUNKS * CHUNK + ROW_TILE - 1
    n_far = shift - A_MAX_REL + 1
    span = ROW_TILE + BAND_KEYS - 1
    assert span - 1 - shift <= A_MAX_REL and span <= BIAS_PERIOD
    u = jnp.concatenate([jnp.broadcast_to(tab[:, 2 * A_MAX_REL:], (h, n_far)),
                         tab[:, 2 * A_MAX_REL - 1:2 * A_MAX_REL - 1 - (span - n_far):-1]], axis=1)
    return jnp.pad(u, ((0, 0), (0, BIAS_PERIOD - span)))


def _band_attention(q, k, v, u, batch):
    t, w = q.shape
    n = BAND_TILES_PER_STEP
    nb = t // batch // (n * ROW_TILE)
    step = pl.BlockSpec((n * ROW_TILE, w), lambda b, i: (b * nb + i, 0))
    tile = lambda j: pl.BlockSpec(
        (ROW_TILE, w), lambda b, i: (n * b * nb + jnp.maximum(n * i + j - (BAND_TILES - 1), 0), 0))
    window = [tile(j) for j in range(n + BAND_TILES - 1)]
    return pl.pallas_call(
        _band_kernel,
        grid=(batch, nb),
        in_specs=[step] + window + window + [pl.BlockSpec(u.shape, lambda b, i: (0, 0))],
        out_specs=step,
        out_shape=jax.ShapeDtypeStruct((t, w), BF16),
        scratch_shapes=[pltpu.VMEM((A_HEADS // 2, 2 * ROW_TILE, BAND_KEYS), F32)],
        compiler_params=_params("arbitrary", "arbitrary"),
        name="band_attn",
    )(q, *([k] * len(window)), *([v] * len(window)), u)


def _gla_constants():
    c = CHUNK
    t = np.arange(c)[:, None]
    r = np.arange(c)[None, :]
    mats = [(r <= t), (r > t)]
    lvl = np.full((c, c), -1, np.int32)
    lvl[np.arange(c), np.arange(c)] = N_LEVELS
    for l in range(N_LEVELS):
        m = (c // 2) >> l
        mid = (t // (2 * m)) * (2 * m) + m
        upper = t >= mid
        mats.append(np.where(upper, (r >= mid) & (r <= t), (r > t) & (r < mid)))
        s = r
        same = (s // (2 * m)) == (t // (2 * m))
        lvl[np.asarray(same & upper & (s < mid))] = l
    eye = np.eye(CHUNKS_PER_TILE)
    mexp = np.concatenate([np.kron(eye, m) for m in mats], axis=0).astype(np.float32)
    lvl = np.tile(lvl, (1, B_HEADS))
    return jnp.asarray(mexp, BF16), jnp.asarray(lvl, I32)


def _gla_kernel(q_ref, k_ref, v_ref, g_ref, r_ref, gn_ref, mexp_ref, lvl_ref, o_ref, s_ref):
    @pl.when(pl.program_id(1) == 0)
    def _():
        s_ref[...] = jnp.zeros_like(s_ref)

    kw = B_KEY_WIDTH
    ri = lax.broadcasted_iota(I32, (kw, kw), 0) >> LOG_CHUNK
    ci = lax.broadcasted_iota(I32, (kw, kw), 1) >> LOG_CHUNK
    bd = ri == ci
    head_ind = jnp.where(bd, 1.0, 0.0).astype(BF16)
    ri2 = lax.broadcasted_iota(I32, (kw, 2 * kw), 0) >> LOG_CHUNK
    ci2 = (lax.broadcasted_iota(I32, (kw, 2 * kw), 1) & (kw - 1)) >> LOG_CHUNK
    bd2 = ri2 == ci2
    lvl = lvl_ref[...]
    split_row = lax.broadcasted_iota(I32, (CHUNK_ROWS, kw), 0)
    ones = jnp.ones((CHUNK_ROWS, LANES), BF16)
    zero_b = jnp.zeros((kw, kw), BF16)
    chunks = [slice(c * CHUNK, (c + 1) * CHUNK) for c in range(CHUNKS_PER_TILE)]

    def head_blocks(x):
        return jnp.where(bd, jnp.concatenate([x] * B_HEADS, axis=0), zero_b)

    def prepare(tile):
        trows = pl.ds(tile * ROW_TILE, ROW_TILE)
        q = q_ref[trows, :].astype(F32)
        k = k_ref[trows, :].astype(F32)
        g = g_ref[trows, :]
        gb = g.astype(BF16)
        half = EXP_ROWS * CHUNKS_PER_TILE // 2
        w = jnp.exp(jnp.concatenate([_dot(mexp_ref[:half, :], gb), _dot(mexp_ref[half:, :], gb)], axis=0))
        qt = (q * w[0:ROW_TILE]).astype(BF16)
        kb = (k * w[ROW_TILE:2 * ROW_TILE]).astype(BF16)
        qk = (q * k).astype(BF16)

        attn = [jnp.zeros((CHUNK, kw), F32) for _ in chunks]
        for l in range(N_LEVELS):
            wl = w[(2 + l) * ROW_TILE:(3 + l) * ROW_TILE]
            qh = (q * wl).astype(BF16)
            kh = (k * wl).astype(BF16)
            for c, rows in enumerate(chunks):
                attn[c] = jnp.where(lvl == l, _dot_nt(qh[rows], head_blocks(kh[rows])), attn[c])

        out = []
        for c, rows in enumerate(chunks):
            a = jnp.where(lvl == N_LEVELS, _dot(qk[rows], head_ind), attn[c])
            v = v_ref[pl.ds(tile * ROW_TILE + c * CHUNK, CHUNK), :]
            vstack = jnp.concatenate([v[:, j * LANES:(j + 1) * LANES] for j in range(B_HEADS)], axis=0)
            kv = _dot_tn(head_blocks(kb[rows]), vstack)
            d = jnp.exp(jnp.sum(g[rows], axis=0, keepdims=True))
            d1 = d.astype(BF16).astype(F32)
            dp = jnp.where(split_row == 0, d1, jnp.where(split_row == 1, d - d1, 0.0)).astype(BF16)
            dcol = _dot_tn(dp, ones)
            out.append((a.astype(BF16), qt[rows], vstack, kv, dcol))
        return out

    prepared = [p for tile in range(GLA_TILES_PER_STEP) for p in prepare(tile)]

    s = s_ref[...]
    for c, (a, qtc, vstack, kv, dcol) in enumerate(prepared):
        rows = pl.ds(c * CHUNK, CHUNK)
        lhs = jnp.concatenate([a, qtc], axis=1)
        lhs = jnp.where(bd2, jnp.concatenate([lhs] * B_HEADS, axis=0), jnp.zeros((kw, 2 * kw), BF16))
        rhs = jnp.concatenate([vstack, s.astype(BF16)], axis=0)
        o = _dot(lhs, rhs)
        s = dcol * s + kv
        for j in range(B_HEADS):
            oj = o[j * CHUNK:(j + 1) * CHUNK]
            sl = slice(j * LANES, (j + 1) * LANES)
            y = oj * lax.rsqrt(jnp.mean(oj * oj, axis=-1, keepdims=True) + EPS) * gn_ref[...]
            o_ref[rows, sl] = (y * r_ref[rows, sl].astype(F32)).astype(BF16)
    s_ref[...] = s


def _gla(q, k, v, g, r, gn, batch):
    t = q.shape[0]
    nb = t // batch // (GLA_TILES_PER_STEP * ROW_TILE)
    mexp, lvl = _gla_constants()
    cur = lambda b, i: (b * nb + i, 0)
    blk = lambda w: pl.BlockSpec((GLA_TILES_PER_STEP * ROW_TILE, w), cur)
    full = lambda a: pl.BlockSpec(a.shape, lambda b, i: (0,) * a.ndim)
    return pl.pallas_call(
        _gla_kernel,
        grid=(batch, nb),
        in_specs=[blk(B_KEY_WIDTH), blk(B_KEY_WIDTH), blk(B_VAL_WIDTH), blk(B_KEY_WIDTH), blk(B_VAL_WIDTH),
                  full(gn), full(mexp), full(lvl)],
        out_specs=blk(B_VAL_WIDTH),
        out_shape=jax.ShapeDtypeStruct((t, B_VAL_WIDTH), BF16),
        scratch_shapes=[pltpu.VMEM((B_KEY_WIDTH, B_VAL_DIM), F32)],
        compiler_params=_params("arbitrary", "arbitrary"),
        name="gla",
    )(q, k, v, g, r, gn, mexp, lvl)


def _token_kernel(x_ref, oa_ref, ob_ref, gate_ref, wb0_ref, wb1_ref, wmix_ref, gx_ref, wq_ref,
                  km_ref, vm_ref, wo_ref, gf_ref, wr_ref, br_ref, ltri_ref, utri_ref,
                  x2_ref, hs_ref, route_ref, cnt_ref):
    ma = _dot(oa_ref[...], wb0_ref[...])
    mb = _dot(ob_ref[...], wb1_ref[...])
    merged = (gate_ref[:, :D_MODEL].astype(F32) * ma + gate_ref[:, D_MODEL:].astype(F32) * mb).astype(BF16)
    x1 = x_ref[...] + _dot(merged, wmix_ref[...])

    h2 = _rms(x1, gx_ref[...]).astype(BF16)
    qx = (_dot(h2, wq_ref[...]) * (X_HEAD_DIM ** -0.5)).astype(BF16)
    heads = []
    for h in range(X_HEADS):
        sl = slice(h * X_HEAD_DIM, (h + 1) * X_HEAD_DIM)
        s = _dot_nt(qx[:, sl], km_ref[0, :, sl])
        m = jnp.max(s, axis=-1, keepdims=True)
        pe = jnp.exp(s - m)
        l = jnp.sum(pe, axis=-1, keepdims=True)
        heads.append((_dot(pe.astype(BF16), vm_ref[0, :, sl]) * (1.0 / l)).astype(BF16))
    x2 = x1 + _dot(jnp.concatenate(heads, axis=1), wo_ref[...])
    x2_ref[...] = x2

    h3 = _rms(x2, gf_ref[...])

    h3_hi = h3.astype(BF16)
    h3_lo = (h3 - h3_hi.astype(F32)).astype(BF16)
    hw = _dot(h3_hi, wr_ref[...])
    logits = hw[:, :LANES] + hw[:, LANES:] + _dot(h3_lo, wr_ref[:, :LANES]) + br_ref[...]
    oh0, oh1, g0, g1 = _route(logits)
    for h in range(TOKEN_TILES_PER_STEP):
        rows = slice(h * ROW_TILE, (h + 1) * ROW_TILE)
        chunks = pl.ds(h * LOCAL_CHUNKS, LOCAL_CHUNKS)
        _sort_tile(oh0[rows], oh1[rows], g0[rows], g1[rows], h3_hi[rows], ltri_ref, utri_ref,
                   hs_ref.at[chunks], route_ref.at[pl.ds(h * ROW_TILE, ROW_TILE)], cnt_ref.at[h])


def _route(logits):
    lane = lax.broadcasted_iota(I32, logits.shape, 1).astype(F32)
    big = jnp.float32(LANES)
    gl = jnp.where(lane < N_GROUPS, logits, NEG)
    gmax = jnp.max(gl, axis=-1, keepdims=True)
    gidx = jnp.min(jnp.where(gl == gmax, lane, big), axis=-1, keepdims=True)
    g_w = 1.0 / jnp.sum(jnp.exp(gl - gmax), axis=-1, keepdims=True)
    lo = N_GROUPS + EXPERTS_PER_GROUP * gidx
    el = jnp.where((lane >= lo) & (lane < lo + EXPERTS_PER_GROUP), logits, NEG)
    v1 = jnp.max(el, axis=-1, keepdims=True)
    i1 = jnp.min(jnp.where(el == v1, lane, big), axis=-1, keepdims=True)
    el2 = jnp.where(lane == i1, NEG, el)
    v2 = jnp.max(el2, axis=-1, keepdims=True)
    i2 = jnp.min(jnp.where(el2 == v2, lane, big), axis=-1, keepdims=True)
    e21 = jnp.exp(v2 - v1)
    w1 = g_w / (1.0 + e21)

    def gate_cols(w):
        hi = w.astype(BF16).astype(F32)
        return jnp.where(lane == 0, hi, jnp.where(lane == 1, w - hi, 0.0)).astype(BF16)

    oh0 = jnp.where(lane == i1 - N_GROUPS, 1.0, 0.0)
    oh1 = jnp.where(lane == i2 - N_GROUPS, 1.0, 0.0)
    return oh0, oh1, gate_cols(w1), gate_cols(w1 * e21)


def _sort_tile(oh0, oh1, g0, g1, h3_hi, ltri_ref, utri_ref, hs_ref, route_ref, cnt_ref):
    lane = lax.broadcasted_iota(I32, oh0.shape, 1)
    oh = oh0 + oh1
    nch = jnp.floor((jnp.sum(oh, axis=0, keepdims=True) + (CHUNK_ROWS - 1)) * (1.0 / CHUNK_ROWS))
    nch8 = jnp.broadcast_to(nch, (8, LANES))
    start = _dot(nch8.astype(BF16), utri_ref[...])[0:1] * CHUNK_ROWS
    rank = _dot(ltri_ref[...], oh.astype(BF16))
    row = start + rank
    pos0 = jnp.sum(row * oh0, axis=-1, keepdims=True)
    pos1 = jnp.sum(row * oh1, axis=-1, keepdims=True)
    route = jnp.where(lane == 0, pos0, jnp.where(lane == 1, pos1, 0.0))
    route_t = jnp.transpose(route)
    r = lax.broadcasted_iota(I32, (LOCAL_ROWS, ROW_TILE), 0).astype(F32)
    p0 = jnp.where(r == route_t[0:1, :], 1.0, 0.0).astype(BF16)
    p1 = jnp.where(r == route_t[1:2, :], 1.0, 0.0).astype(BF16)
    sorted_rows = jnp.concatenate([_dot(p0 + p1, h3_hi), _dot(p0, g0) + _dot(p1, g1)], axis=1)
    hs_ref[...] = sorted_rows.astype(BF16).reshape(hs_ref.shape)
    route_ref[...] = route
    cnt_ref[...] = nch8


def _token(x, oa, ob, gates, wb0, wb1, wmix, gx, wq, km, vm, wo, gf, wr, br, batch):
    t, d = x.shape
    n = TOKEN_TILES_PER_STEP
    nb = t // batch // (n * ROW_TILE)
    nt = t // ROW_TILE
    ltri = jnp.asarray(np.tril(np.ones((ROW_TILE, ROW_TILE), np.float32), -1), BF16)
    utri = jnp.asarray(np.triu(np.ones((LANES, LANES), np.float32), 1), BF16)
    cur = lambda b, i: (b * nb + i, 0)
    cur3 = lambda b, i: (b * nb + i, 0, 0)
    blk = lambda w: pl.BlockSpec((n * ROW_TILE, w), cur)
    full = lambda a: pl.BlockSpec(a.shape, lambda b, i: (0,) * a.ndim)
    mem = pl.BlockSpec((1,) + km.shape[1:], lambda b, i: (b, 0, 0))
    return pl.pallas_call(
        _token_kernel,
        grid=(batch, nb),
        in_specs=[blk(d), blk(A_WIDTH), blk(B_VAL_WIDTH), blk(2 * d), full(wb0), full(wb1), full(wmix),
                  full(gx), full(wq), mem, mem, full(wo), full(gf), full(wr), full(br), full(ltri), full(utri)],
        out_specs=[blk(d), pl.BlockSpec((n * LOCAL_CHUNKS, CHUNK_ROWS, SORT_WIDTH), cur3),
                   blk(LANES), pl.BlockSpec((n, 8, LANES), cur3)],
        out_shape=[jax.ShapeDtypeStruct((t, d), F32),
                   jax.ShapeDtypeStruct((nt * LOCAL_CHUNKS, CHUNK_ROWS, SORT_WIDTH), BF16),
                   jax.ShapeDtypeStruct((t, LANES), F32),
                   jax.ShapeDtypeStruct((nt, 8, LANES), F32)],
        compiler_params=_params("arbitrary", "arbitrary"),
        name="token",
    )(x, oa, ob, gates, wb0, wb1, wmix, gx, wq, km, vm, wo, gf, wr, br, ltri, utri)


def _expert_kernel(layer, te_ref, nu_ref, nv_ref, ch_ref, first_ref, nxt_ref,
                   hs_hbm, wg_hbm, wu_hbm, wd_hbm, ys_hbm,
                   xbuf, ybuf, wg_st, wu_st, wd_st, wgu_bf, wd_bf, gsem, ssem, wsem):
    n_used = nu_ref[0]
    ring = GATHER_AHEAD + 1
    last_tile = te_ref.shape[0] - 1

    def weights(expert, s, start):
        for src, dst in ((wg_hbm, wg_st), (wu_hbm, wu_st), (wd_hbm, wd_st)):
            cp = pltpu.make_async_copy(src.at[layer * N_EXPERTS + expert], dst.at[s], wsem.at[s])
            cp.start() if start else cp.wait()

    def for_chunks(tile, fn):
        nv = nv_ref[tile]

        @pl.when(nv == TILE_CHUNKS)
        def _():
            for c in range(TILE_CHUNKS):
                fn(c)

        @pl.when(nv != TILE_CHUNKS)
        def _():
            def body(c, carry):
                fn(c)
                return carry

            lax.fori_loop(0, nv, body, 0)

    def gather(step, start):
        tile = jnp.minimum(step, last_tile)
        s = lax.rem(step, ring)
        for c in range(TILE_CHUNKS):
            cp = pltpu.make_async_copy(hs_hbm.at[ch_ref[tile * TILE_CHUNKS + c]], xbuf.at[s, c], gsem.at[s])
            cp.start() if start else cp.wait()

    def scatter(tile, s, start):
        def one(c):
            cp = pltpu.make_async_copy(ybuf.at[s, c], ys_hbm.at[ch_ref[tile * TILE_CHUNKS + c]], ssem.at[s])
            cp.start(priority=1) if start else cp.wait()

        for_chunks(tile, one)

    weights(te_ref[0], 0, True)
    for step in range(GATHER_AHEAD):
        gather(step, True)

    def tile_body(i, run):
        slot = lax.rem(i, 2)
        gather(i, False)

        @pl.when(i >= 2)
        def _():
            scatter(i - 2, slot, False)

        @pl.when(first_ref[i] == 1)
        def _():
            s = lax.rem(run, 2)
            weights(te_ref[i], s, False)

            @pl.when(nxt_ref[i] >= 0)
            def _():
                weights(nxt_ref[i], 1 - s, True)

            wgu_bf[:, :EXPERT_FF] = wg_st[s].astype(BF16)
            wgu_bf[:, EXPERT_FF:] = wu_st[s].astype(BF16)
            wd_bf[...] = wd_st[s].astype(BF16)

        xg = xbuf[lax.rem(i, ring)].reshape(EXPERT_ROWS, SORT_WIDTH)
        hgu = _dot(xg[:, :D_MODEL], wgu_bf[...])
        gather(i + GATHER_AHEAD, True)
        hg, hu = hgu[:, :EXPERT_FF], hgu[:, EXPERT_FF:]
        hid = (hg * jax.nn.sigmoid(hg) * hu).astype(BF16)
        g = xg[:, D_MODEL:].astype(F32)
        y = ((g[:, 0:1] + g[:, 1:2]) * _dot(hid, wd_bf[...])).astype(BF16)
        y = jnp.concatenate([y, jnp.zeros((EXPERT_ROWS, LANES), BF16)], axis=1)
        ybuf[slot] = y.reshape(TILE_CHUNKS, CHUNK_ROWS, SORT_WIDTH)
        scatter(i, slot, True)
        return run + first_ref[i]

    lax.fori_loop(0, n_used, tile_body, jnp.int32(0))

    last = n_used - 1
    for ahead in range(1, GATHER_AHEAD + 1):
        gather(last + ahead, False)
    scatter(last, lax.rem(last, 2), False)

    @pl.when(last >= 1)
    def _():
        scatter(last - 1, lax.rem(last - 1, 2), False)


def _experts(hs, tile_expert, n_used, n_valid, chunks, run_first, run_next, wg, wu, wd, layer):
    anyspace = pl.BlockSpec(memory_space=pl.ANY)
    grid_spec = pltpu.PrefetchScalarGridSpec(
        num_scalar_prefetch=6,
        grid=(1,),
        in_specs=[anyspace] * 4,
        out_specs=anyspace,
        scratch_shapes=[pltpu.VMEM((GATHER_AHEAD + 1, TILE_CHUNKS, CHUNK_ROWS, SORT_WIDTH), BF16),
                        pltpu.VMEM((2, TILE_CHUNKS, CHUNK_ROWS, SORT_WIDTH), BF16),
                        pltpu.VMEM((2, D_MODEL, EXPERT_FF), F32), pltpu.VMEM((2, D_MODEL, EXPERT_FF), F32),
                        pltpu.VMEM((2, EXPERT_FF, D_MODEL), F32),
                        pltpu.VMEM((D_MODEL, 2 * EXPERT_FF), BF16), pltpu.VMEM((EXPERT_FF, D_MODEL), BF16),
                        pltpu.SemaphoreType.DMA((GATHER_AHEAD + 1,)), pltpu.SemaphoreType.DMA((2,)),
                        pltpu.SemaphoreType.DMA((2,))],
    )
    return pl.pallas_call(
        functools.partial(_expert_kernel, layer),
        grid_spec=grid_spec,
        out_shape=jax.ShapeDtypeStruct(hs.shape, BF16),
        input_output_aliases={6: 0},
        compiler_params=_params("arbitrary"),
        name="experts",
    )(tile_expert, n_used, n_valid, chunks, run_first, run_next, hs, wg, wu, wd)


def _combine_kernel(x_ref, route_ref, ys_ref, gfin_ref, o_ref):
    r = lax.broadcasted_iota(I32, (ROW_TILE, LOCAL_ROWS), 1).astype(F32)
    for t in range(COMBINE_TILES_PER_STEP):
        rows = slice(t * ROW_TILE, (t + 1) * ROW_TILE)
        pos = route_ref[rows, :]
        sel = jnp.where(r == pos[:, 0:1], 1.0, jnp.where(r == pos[:, 1:2], 1.0, 0.0)).astype(BF16)
        ys = ys_ref[t * LOCAL_CHUNKS:(t + 1) * LOCAL_CHUNKS].reshape(LOCAL_ROWS, D_MODEL)
        o_ref[rows, :] = _rms(x_ref[rows, :] + _dot(sel, ys), gfin_ref[...])


def _combine(x2, route, ys, gfin):
    t, d = x2.shape
    n = COMBINE_TILES_PER_STEP
    return pl.pallas_call(
        _combine_kernel,
        grid=(t // (n * ROW_TILE),),
        in_specs=[pl.BlockSpec((n * ROW_TILE, d), lambda i: (i, 0)),
                  pl.BlockSpec((n * ROW_TILE, LANES), lambda i: (i, 0)),
                  pl.BlockSpec((n * LOCAL_CHUNKS, CHUNK_ROWS, d), lambda i: (i, 0, 0)),
                  pl.BlockSpec((1, d), lambda i: (0, 0))],
        out_specs=pl.BlockSpec((n * ROW_TILE, d), lambda i: (i, 0)),
        out_shape=jax.ShapeDtypeStruct((t, d), F32),
        compiler_params=_params("arbitrary"),
        name="combine",
    )(x2, route, ys, gfin)


def _chunk_plan(nch, n_tiles):
    nt = nch.shape[0]
    local_start = jnp.cumsum(nch, axis=1) - nch
    cum = jnp.cumsum(nch, axis=0)
    total = cum[-1]
    tiles = (total + TILE_CHUNKS - 1) // TILE_CHUNKS
    tile_end = jnp.cumsum(tiles)
    n_used = tile_end[-1:]
    tile_ids = jnp.arange(n_tiles, dtype=I32)
    tile_expert = jnp.minimum(jnp.sum((tile_end[None, :] <= tile_ids[:, None]).astype(I32), axis=1),
                              N_EXPERTS - 1)
    sel = (tile_expert[:, None] == jnp.arange(N_EXPERTS, dtype=I32)[None, :]).astype(I32)
    pick = lambda table: jnp.sum(sel[:, :, None] * table.T[None, :, :], axis=1)
    first_tile = jnp.sum(sel * (tile_end - tiles)[None, :], axis=1)
    slot = (tile_ids - first_tile)[:, None] * TILE_CHUNKS + jnp.arange(TILE_CHUNKS, dtype=I32)[None, :]
    valid = (slot < jnp.sum(sel * total[None, :], axis=1)[:, None]) & (tile_ids < n_used)[:, None]
    src_tile = jnp.sum((pick(cum)[:, None, :] <= slot[:, :, None]).astype(I32), axis=2)
    src_tile = jnp.minimum(src_tile, nt - 1)
    at = (src_tile[:, :, None] == jnp.arange(nt, dtype=I32)[None, None, :]).astype(I32)
    before = jnp.sum(at * pick(cum - nch)[:, None, :], axis=2)
    start = jnp.sum(at * pick(local_start)[:, None, :], axis=2)
    chunk = jnp.where(valid, src_tile * LOCAL_CHUNKS + start + slot - before, LOCAL_CHUNKS - 1)
    used = tile_ids < n_used
    run_first = ((tile_ids == first_tile) & used).astype(I32)
    run_end = jnp.sum(sel * tile_end[None, :], axis=1)
    next_expert = jnp.sum((run_end[:, None] == tile_ids[None, :]).astype(I32) * tile_expert[None, :], axis=1)
    run_next = jnp.where(used & (run_end < n_used), next_expert, -1)
    return tile_expert, n_used, jnp.sum(valid.astype(I32), axis=1), chunk.reshape(-1), run_first, run_next


def kernel(x, mem, norm_mix_g, w_in, rel_bias, gla_w_alpha, gla_b_alpha, gla_norm_g, w_branch, w_mix_out, norm_x_g, mem_norm_g, w_xq, w_xkv, w_xo, norm_ffn_g, w_group_router, b_group_router, w_expert_router, b_expert_router, w_exp_gate, w_exp_up, w_exp_down, final_norm_g):
    batch, seq, d = x.shape
    depth = w_in.shape[0]
    t = batch * seq
    step_tiles = max(TOKEN_TILES_PER_STEP, GLA_TILES_PER_STEP, BAND_TILES_PER_STEP, COMBINE_TILES_PER_STEP)
    assert d == D_MODEL and seq % (step_tiles * ROW_TILE) == 0 and seq % INPROJ_ROWS == 0
    nt = t // ROW_TILE
    n_tiles = nt * LOCAL_CHUNKS // TILE_CHUNKS + N_EXPERTS

    xf = x.reshape(t, d)
    km_all, vm_all = _memkv(mem, mem_norm_g, w_xkv)
    row = lambda a: a.reshape(1, -1).astype(F32)

    moe = None
    for l in range(depth):
        res = _inproj(xf, row(norm_mix_g[l]), w_in, gla_w_alpha, row(gla_b_alpha[l]), l, moe)
        if moe is not None:
            xf, res = res[0], res[1:]
        aq, ak, av, bq, bk, bv, lga, br, gates = res

        oa = _band_attention(aq, ak, av, _band_bias_vector(rel_bias[l]), batch)
        ob = _gla(bq, bk, bv, lga, br, row(gla_norm_g[l]), batch)

        wr = jnp.pad(jnp.concatenate([w_group_router[l], w_expert_router[l]], axis=1).astype(F32),
                     ((0, 0), (0, LANES - N_GROUPS - N_EXPERTS)))
        wr_hi = wr.astype(BF16)
        wr = jnp.concatenate([wr_hi, (wr - wr_hi.astype(F32)).astype(BF16)], axis=1)
        brt = jnp.pad(jnp.concatenate([b_group_router[l], b_expert_router[l]]).astype(F32),
                      (0, LANES - N_GROUPS - N_EXPERTS)).reshape(1, LANES)
        x2, hs, route, cnt = _token(
            xf, oa, ob, gates, w_branch[l, 0].astype(BF16), w_branch[l, 1].astype(BF16),
            w_mix_out[l].astype(BF16), row(norm_x_g[l]), w_xq[l].astype(BF16), km_all[l], vm_all[l],
            w_xo[l].astype(BF16), row(norm_ffn_g[l]), wr, brt, batch)

        plan = _chunk_plan(cnt[:, 0, :N_EXPERTS].astype(I32), n_tiles)
        e3 = lambda w: w.reshape((depth * N_EXPERTS,) + w.shape[3:])
        ys = _experts(hs, *plan, e3(w_exp_gate), e3(w_exp_up), e3(w_exp_down), l)
        xf, moe = x2, (route, ys)

    return _combine(x2, route, ys, row(final_norm_g)).reshape(batch, seq, d)
```

```python
import functools

import numpy as np
import jax
import jax.numpy as jnp
from jax import lax
from jax.experimental import pallas as pl
from jax.experimental.pallas import tpu as pltpu

F32 = jnp.float32
BF16 = jnp.bfloat16
I32 = jnp.int32

D_MODEL = 1024
CHUNK = 64
EPS = 1e-6
A_HEADS = 8
A_HEAD_DIM = 64
A_WIDTH = 512
A_LEFT_CHUNKS = 8
A_MAX_REL = 256
B_HEADS = 4
B_KEY_DIM = 64
B_VAL_DIM = 128
B_KEY_WIDTH = 256
B_VAL_WIDTH = 512
B_GATE_RANK = 16
B_GATE_TAU = 16.0
X_HEADS = 4
X_HEAD_DIM = 256
N_GROUPS = 4
EXPERTS_PER_GROUP = 8
N_EXPERTS = N_GROUPS * EXPERTS_PER_GROUP
EXPERT_FF = 256

LANES = 128
ROW_TILE = 256
CHUNKS_PER_TILE = ROW_TILE // CHUNK
BAND_TILES = A_LEFT_CHUNKS // CHUNKS_PER_TILE + 1
BAND_KEYS = BAND_TILES * ROW_TILE
BIAS_PERIOD = 1024
LOG_CHUNK = 6
N_LEVELS = LOG_CHUNK
EXP_ROWS = (2 + N_LEVELS) * CHUNK
CHUNK_ROWS = 16
TOKEN_TILES_PER_STEP = 2
INPROJ_ROWS = 512
COMBINE_TILES_PER_STEP = 4
GLA_TILES_PER_STEP = 4
BAND_TILES_PER_STEP = 4
EXPERT_ROWS = 512
GATHER_AHEAD = 4
SCATTER_RING = 4
TILE_CHUNKS = EXPERT_ROWS // CHUNK_ROWS
LOCAL_CHUNKS = 2 * ROW_TILE // CHUNK_ROWS + N_EXPERTS
LOCAL_ROWS = LOCAL_CHUNKS * CHUNK_ROWS
assert (2 * ROW_TILE + N_EXPERTS * (CHUNK_ROWS - 1)) // CHUNK_ROWS < LOCAL_CHUNKS
SORT_WIDTH = D_MODEL + LANES
NEG = -1e30
LOG2E = 1.4426950408889634
VMEM_LIMIT = 56 * 1024 * 1024


def _params(*sem):
    return pltpu.CompilerParams(dimension_semantics=sem, vmem_limit_bytes=VMEM_LIMIT)


def _rms(x, g):
    return x * lax.rsqrt(jnp.mean(x * x, axis=-1, keepdims=True) + EPS) * g


def _dot(a, b):
    return jnp.dot(a, b, preferred_element_type=F32)


def _dot_nt(a, b):
    return lax.dot_general(a, b, (((1,), (1,)), ((), ())), preferred_element_type=F32)


def _dot_tn(a, b):
    return lax.dot_general(a, b, (((0,), (0,)), ((), ())), preferred_element_type=F32)


def _memkv_kernel(mem_ref, g_ref, w_ref, k_ref, v_ref):
    mn = _rms(mem_ref[0], g_ref[...]).astype(BF16)
    kv = _dot(mn, w_ref[0].astype(BF16))
    k_ref[0, 0] = kv[:, :D_MODEL].astype(BF16)
    v_ref[0, 0] = kv[:, D_MODEL:].astype(BF16)


def _memkv(mem, g, w_xkv):
    depth = w_xkv.shape[0]
    b, m, d = mem.shape
    out = jax.ShapeDtypeStruct((depth, b, m, d), BF16)
    return pl.pallas_call(
        _memkv_kernel,
        grid=(depth, b),
        in_specs=[pl.BlockSpec((1, m, d), lambda l, i: (i, 0, 0)),
                  pl.BlockSpec((1, d), lambda l, i: (0, 0)),
                  pl.BlockSpec((1, d, 2 * d), lambda l, i: (l, 0, 0))],
        out_specs=[pl.BlockSpec((1, 1, m, d), lambda l, i: (l, i, 0, 0)),
                   pl.BlockSpec((1, 1, m, d), lambda l, i: (l, i, 0, 0))],
        out_shape=[out, out],
        compiler_params=_params("arbitrary", "arbitrary"),
        name="memkv",
    )(mem, g.reshape(1, d), w_xkv)


_R_AQ, _R_AK, _R_AV = 0, 512, 1024
_R_BQ, _R_BK, _R_BV = 1536, 1792, 2048
_R_ALPHA, _R_BR, _R_GATE, _R_END = 2560, 2576, 3088, 5136
_W_PIECE = 512
assert (_R_END - _R_GATE) % _W_PIECE == 0


def _inproj_kernel(layer, fused, *refs):
    if fused:
        x_ref, route_ref, ys_ref, g_ref, wt_hbm, wal2_ref, bal_ref = refs[:7]
        refs = refs[7:]
        x3_ref, refs = refs[0], refs[1:]
    else:
        x_ref, g_ref, wt_hbm, wal2_ref, bal_ref = refs[:5]
        refs = refs[5:]
    (aq_ref, ak_ref, av_ref, bq_ref, bk_ref, bv_ref, lga_ref, br_ref, gate_ref,
     wt_bf, wal2_bf, stage, sem) = refs

    @pl.when(pl.program_id(0) == 0)
    def _():
        pieces = [(c, min(_W_PIECE, _R_END - c)) for c in range(0, _R_END, _W_PIECE)]

        def piece_copy(p):
            c, n = pieces[p]
            return pltpu.make_async_copy(wt_hbm.at[layer, pl.ds(c, n), :], stage.at[p % 2, pl.ds(0, n), :],
                                         sem.at[p % 2])

        piece_copy(0).start()
        for p, (c, n) in enumerate(pieces):
            if p + 1 < len(pieces):
                piece_copy(p + 1).start()
            piece_copy(p).wait()
            wt_bf[c:c + n, :] = stage[p % 2, 0:n, :].astype(BF16)
        wal2_bf[...] = jnp.concatenate(
            [wal2_ref[0].astype(BF16), jnp.zeros((LANES - B_GATE_RANK, B_KEY_WIDTH), BF16)], axis=0)

    if fused:
        r = lax.broadcasted_iota(I32, (ROW_TILE, LOCAL_ROWS), 1).astype(F32)
        tiles = []
        for t in range(INPROJ_ROWS // ROW_TILE):
            rows = slice(t * ROW_TILE, (t + 1) * ROW_TILE)
            pos = route_ref[rows, :]
            sel = jnp.where(r == pos[:, 0:1], 1.0, jnp.where(r == pos[:, 1:2], 1.0, 0.0)).astype(BF16)
            ys = ys_ref[t * LOCAL_CHUNKS:(t + 1) * LOCAL_CHUNKS].reshape(LOCAL_ROWS, D_MODEL)
            tiles.append(x_ref[rows, :] + _dot(sel, ys))
        x = jnp.concatenate(tiles, axis=0)
        x3_ref[...] = x
    else:
        x = x_ref[...]
    h = _rms(x, g_ref[...]).astype(BF16)

    def mm(lo, hi):
        return _dot_nt(h, wt_bf[lo:hi, :])

    aq_ref[...] = (mm(_R_AQ, _R_AK) * (A_HEAD_DIM ** -0.5 * LOG2E)).astype(BF16)
    ak_ref[...] = mm(_R_AK, _R_AV).astype(BF16)
    av_ref[...] = mm(_R_AV, _R_BQ).astype(BF16)
    bq_ref[...] = (mm(_R_BQ, _R_BK) * (B_KEY_DIM ** -0.5)).astype(BF16)
    bk_ref[...] = mm(_R_BK, _R_BV).astype(BF16)
    bv_ref[...] = mm(_R_BV, _R_ALPHA).astype(BF16)
    r = mm(_R_BR, _R_GATE)
    br_ref[...] = (r * jax.nn.sigmoid(r)).astype(BF16)
    for c in range(_R_GATE, _R_END, _W_PIECE):
        gate_ref[:, c - _R_GATE:c - _R_GATE + _W_PIECE] = jax.nn.sigmoid(mm(c, c + _W_PIECE)).astype(BF16)
    z = _dot(mm(_R_ALPHA, _R_ALPHA + LANES).astype(BF16), wal2_bf[...]) + bal_ref[...]
    lga_ref[...] = (jnp.minimum(z, 0.0) - jnp.log(1.0 + jnp.exp(-jnp.abs(z)))) * (1.0 / B_GATE_TAU)


def _inproj(x, g, w_in, w_al2, b_al, layer, moe=None):
    t, d = x.shape
    assert w_in.shape[2] == _R_END
    row = lambda w: pl.BlockSpec((INPROJ_ROWS, w), lambda i: (i, 0))
    full = lambda a: pl.BlockSpec(a.shape, lambda i: (0,) * a.ndim)
    sds = lambda w, dt: jax.ShapeDtypeStruct((t, w), dt)
    widths = [(A_WIDTH, BF16), (A_WIDTH, BF16), (A_WIDTH, BF16), (B_KEY_WIDTH, BF16), (B_KEY_WIDTH, BF16),
              (B_VAL_WIDTH, BF16), (B_KEY_WIDTH, F32), (B_VAL_WIDTH, BF16), (_R_END - _R_GATE, BF16)]
    fused = moe is not None
    moe_specs, moe_out_specs, moe_out_shape = [], [], []
    if fused:
        n_chunks = INPROJ_ROWS // ROW_TILE * LOCAL_CHUNKS
        moe_specs = [row(LANES), pl.BlockSpec((n_chunks, CHUNK_ROWS, d), lambda i: (i, 0, 0))]
        moe_out_specs, moe_out_shape = [row(d)], [sds(d, F32)]
    return pl.pallas_call(
        functools.partial(_inproj_kernel, layer, fused),
        grid=(t // INPROJ_ROWS,),
        in_specs=[row(d)] + moe_specs + [full(g), pl.BlockSpec(memory_space=pl.ANY),
                                         pl.BlockSpec((1,) + w_al2.shape[1:], lambda i: (layer, 0, 0)),
                                         full(b_al)],
        out_specs=moe_out_specs + [row(w) for w, _ in widths],
        out_shape=moe_out_shape + [sds(w, dt) for w, dt in widths],
        scratch_shapes=[pltpu.VMEM((_R_END, d), BF16), pltpu.VMEM((LANES, B_KEY_WIDTH), BF16),
                        pltpu.VMEM((2, _W_PIECE, d), F32), pltpu.SemaphoreType.DMA((2,))],
        compiler_params=_params("arbitrary"),
        name="inproj",
    )(x, *(moe or ()), g, jnp.swapaxes(w_in, 1, 2), w_al2, b_al)


def _band_kernel(q_ref, *refs):
    n_win = BAND_TILES_PER_STEP + BAND_TILES - 1
    k_refs, v_refs = refs[:n_win], refs[n_win:2 * n_win]
    u_ref, o_ref, bias_ref = refs[2 * n_win:]
    i = pl.program_id(1)
    lane = lax.broadcasted_iota(I32, (1, LANES), 1)
    low = lane < A_HEAD_DIM
    ones = jnp.ones((BAND_KEYS, LANES), BF16)

    @pl.when((pl.program_id(0) == 0) & (i == 0))
    def _():
        cq = lax.broadcasted_iota(I32, (ROW_TILE, BAND_KEYS), 0) >> LOG_CHUNK
        ck = lax.broadcasted_iota(I32, (ROW_TILE, BAND_KEYS), 1) >> LOG_CHUNK
        valid = (ck >= cq) & (ck <= cq + A_LEFT_CHUNKS)
        for h in range(A_HEADS):
            rows = jnp.broadcast_to(u_ref[h:h + 1, :], (ROW_TILE, BIAS_PERIOD))
            rows = pltpu.roll(rows, BIAS_PERIOD - (ROW_TILE - 1), 1, stride=1, stride_axis=0)
            bias_ref[h // 2, (h % 2) * ROW_TILE:(h % 2 + 1) * ROW_TILE, :] = jnp.where(
                valid, rows[:, :BAND_KEYS], NEG)

    def attend(tile, n_missing):
        rows = pl.ds(tile * ROW_TILE, ROW_TILE)
        window = range(tile, tile + BAND_TILES)
        for p in range(A_HEADS // 2):
            sl = slice(p * LANES, (p + 1) * LANES)
            qp = q_ref[rows, sl]
            zero = jnp.zeros_like(qp)
            q2 = jnp.concatenate([jnp.where(low, qp, zero), jnp.where(low, zero, qp)], axis=0)
            kp = jnp.concatenate([k_refs[j][:, sl] for j in window], axis=0)
            vp = jnp.concatenate([v_refs[j][:, sl] for j in window], axis=0)
            s = _dot_nt(q2, kp) + bias_ref[p]
            if n_missing:
                col = lax.broadcasted_iota(I32, (1, BAND_KEYS), 1)
                s = s + jnp.where(col < n_missing * ROW_TILE, NEG, 0.0).astype(F32)
            pe = jnp.exp2(s - jnp.max(s, axis=-1, keepdims=True)).astype(BF16)
            o2 = _dot(pe, jnp.concatenate([vp, ones], axis=1))
            o = o2[:, :LANES] * (1.0 / o2[:, LANES:])
            o_ref[rows, sl] = jnp.where(low, o[:ROW_TILE], o[ROW_TILE:]).astype(BF16)

    @pl.when(i > 0)
    def _():
        for tile in range(BAND_TILES_PER_STEP):
            attend(tile, 0)

    @pl.when(i == 0)
    def _():
        for tile in range(BAND_TILES_PER_STEP):
            attend(tile, max(BAND_TILES - 1 - tile, 0))


def _band_bias_vector(rel_table):
    h = rel_table.shape[0]
    tab = rel_table.astype(F32) * LOG2E
    shift = A_LEFT_CHUNKS * CHUNK + ROW_TILE - 1
    n_far = shift - A_MAX_REL + 1
    span = ROW_TILE + BAND_KEYS - 1
    assert span - 1 - shift <= A_MAX_REL and span <= BIAS_PERIOD
    u = jnp.concatenate([jnp.broadcast_to(tab[:, 2 * A_MAX_REL:], (h, n_far)),
                         tab[:, 2 * A_MAX_REL - 1:2 * A_MAX_REL - 1 - (span - n_far):-1]], axis=1)
    return jnp.pad(u, ((0, 0), (0, BIAS_PERIOD - span)))


def _band_attention(q, k, v, u, batch):
    t, w = q.shape
    n = BAND_TILES_PER_STEP
    nb = t // batch // (n * ROW_TILE)
    step = pl.BlockSpec((n * ROW_TILE, w), lambda b, i: (b * nb + i, 0))
    tile = lambda j: pl.BlockSpec(
        (ROW_TILE, w), lambda b, i: (n * b * nb + jnp.maximum(n * i + j - (BAND_TILES - 1), 0), 0))
    window = [tile(j) for j in range(n + BAND_TILES - 1)]
    return pl.pallas_call(
        _band_kernel,
        grid=(batch, nb),
        in_specs=[step] + window + window + [pl.BlockSpec(u.shape, lambda b, i: (0, 0))],
        out_specs=step,
        out_shape=jax.ShapeDtypeStruct((t, w), BF16),
        scratch_shapes=[pltpu.VMEM((A_HEADS // 2, 2 * ROW_TILE, BAND_KEYS), F32)],
        compiler_params=_params("arbitrary", "arbitrary"),
        name="band_attn",
    )(q, *([k] * len(window)), *([v] * len(window)), u)


def _gla_constants():
    c = CHUNK
    t = np.arange(c)[:, None]
    r = np.arange(c)[None, :]
    mats = [(r <= t), (r > t)]
    lvl = np.full((c, c), -1, np.int32)
    lvl[np.arange(c), np.arange(c)] = N_LEVELS
    for l in range(N_LEVELS):
        m = (c // 2) >> l
        mid = (t // (2 * m)) * (2 * m) + m
        upper = t >= mid
        mats.append(np.where(upper, (r >= mid) & (r <= t), (r > t) & (r < mid)))
        s = r
        same = (s // (2 * m)) == (t // (2 * m))
        lvl[np.asarray(same & upper & (s < mid))] = l
    eye = np.eye(CHUNKS_PER_TILE)
    mexp = np.concatenate([np.kron(eye, m) for m in mats], axis=0).astype(np.float32)
    lvl = np.tile(lvl, (1, B_HEADS))
    return jnp.asarray(mexp, BF16), jnp.asarray(lvl, I32)


def _gla_kernel(q_ref, k_ref, v_ref, g_ref, r_ref, gn_ref, mexp_ref, lvl_ref, o_ref, s_ref):
    @pl.when(pl.program_id(1) == 0)
    def _():
        s_ref[...] = jnp.zeros_like(s_ref)

    kw = B_KEY_WIDTH
    ri = lax.broadcasted_iota(I32, (kw, kw), 0) >> LOG_CHUNK
    ci = lax.broadcasted_iota(I32, (kw, kw), 1) >> LOG_CHUNK
    bd = ri == ci
    head_ind = jnp.where(bd, 1.0, 0.0).astype(BF16)
    ri2 = lax.broadcasted_iota(I32, (kw, 2 * kw), 0) >> LOG_CHUNK
    ci2 = (lax.broadcasted_iota(I32, (kw, 2 * kw), 1) & (kw - 1)) >> LOG_CHUNK
    bd2 = ri2 == ci2
    lvl = lvl_ref[...]
    split_row = lax.broadcasted_iota(I32, (CHUNK_ROWS, kw), 0)
    ones = jnp.ones((CHUNK_ROWS, LANES), BF16)
    zero_b = jnp.zeros((kw, kw), BF16)
    chunks = [slice(c * CHUNK, (c + 1) * CHUNK) for c in range(CHUNKS_PER_TILE)]

    def head_blocks(x):
        return jnp.where(bd, jnp.concatenate([x] * B_HEADS, axis=0), zero_b)

    def prepare(tile):
        trows = pl.ds(tile * ROW_TILE, ROW_TILE)
        q = q_ref[trows, :].astype(F32)
        k = k_ref[trows, :].astype(F32)
        g = g_ref[trows, :]
        gb = g.astype(BF16)
        half = EXP_ROWS * CHUNKS_PER_TILE // 2
        w = jnp.exp(jnp.concatenate([_dot(mexp_ref[:half, :], gb), _dot(mexp_ref[half:, :], gb)], axis=0))
        qt = (q * w[0:ROW_TILE]).astype(BF16)
        kb = (k * w[ROW_TILE:2 * ROW_TILE]).astype(BF16)
        qk = (q * k).astype(BF16)

        attn = [jnp.zeros((CHUNK, kw), F32) for _ in chunks]
        for l in range(N_LEVELS):
            wl = w[(2 + l) * ROW_TILE:(3 + l) * ROW_TILE]
            qh = (q * wl).astype(BF16)
            kh = (k * wl).astype(BF16)
            for c, rows in enumerate(chunks):
                attn[c] = jnp.where(lvl == l, _dot_nt(qh[rows], head_blocks(kh[rows])), attn[c])

        out = []
        for c, rows in enumerate(chunks):
            a = jnp.where(lvl == N_LEVELS, _dot(qk[rows], head_ind), attn[c])
            v = v_ref[pl.ds(tile * ROW_TILE + c * CHUNK, CHUNK), :]
            vstack = jnp.concatenate([v[:, j * LANES:(j + 1) * LANES] for j in range(B_HEADS)], axis=0)
            kv = _dot_tn(head_blocks(kb[rows]), vstack)
            d = jnp.exp(jnp.sum(g[rows], axis=0, keepdims=True))
            d1 = d.astype(BF16).astype(F32)
            dp = jnp.where(split_row == 0, d1, jnp.where(split_row == 1, d - d1, 0.0)).astype(BF16)
            dcol = _dot_tn(dp, ones)
            out.append((a.astype(BF16), qt[rows], vstack, kv, dcol))
        return out

    prepared = [p for tile in range(GLA_TILES_PER_STEP) for p in prepare(tile)]

    s = s_ref[...]
    for c, (a, qtc, vstack, kv, dcol) in enumerate(prepared):
        rows = pl.ds(c * CHUNK, CHUNK)
        lhs = jnp.concatenate([a, qtc], axis=1)
        lhs = jnp.where(bd2, jnp.concatenate([lhs] * B_HEADS, axis=0), jnp.zeros((kw, 2 * kw), BF16))
        rhs = jnp.concatenate([vstack, s.astype(BF16)], axis=0)
        o = _dot(lhs, rhs)
        s = dcol * s + kv
        for j in range(B_HEADS):
            oj = o[j * CHUNK:(j + 1) * CHUNK]
            sl = slice(j * LANES, (j + 1) * LANES)
            y = oj * lax.rsqrt(jnp.mean(oj * oj, axis=-1, keepdims=True) + EPS) * gn_ref[...]
            o_ref[rows, sl] = (y * r_ref[rows, sl].astype(F32)).astype(BF16)
    s_ref[...] = s


def _gla(q, k, v, g, r, gn, batch):
    t = q.shape[0]
    nb = t // batch // (GLA_TILES_PER_STEP * ROW_TILE)
    mexp, lvl = _gla_constants()
    cur = lambda b, i: (b * nb + i, 0)
    blk = lambda w: pl.BlockSpec((GLA_TILES_PER_STEP * ROW_TILE, w), cur)
    full = lambda a: pl.BlockSpec(a.shape, lambda b, i: (0,) * a.ndim)
    return pl.pallas_call(
        _gla_kernel,
        grid=(batch, nb),
        in_specs=[blk(B_KEY_WIDTH), blk(B_KEY_WIDTH), blk(B_VAL_WIDTH), blk(B_KEY_WIDTH), blk(B_VAL_WIDTH),
                  full(gn), full(mexp), full(lvl)],
        out_specs=blk(B_VAL_WIDTH),
        out_shape=jax.ShapeDtypeStruct((t, B_VAL_WIDTH), BF16),
        scratch_shapes=[pltpu.VMEM((B_KEY_WIDTH, B_VAL_DIM), F32)],
        compiler_params=_params("arbitrary", "arbitrary"),
        name="gla",
    )(q, k, v, g, r, gn, mexp, lvl)


def _token_kernel(x_ref, oa_ref, ob_ref, gate_ref, wb0_ref, wb1_ref, wmix_ref, gx_ref, wq_ref,
                  km_ref, vm_ref, wo_ref, gf_ref, wr_ref, br_ref, ltri_ref, utri_ref,
                  x2_ref, hs_ref, route_ref, cnt_ref):
    ma = _dot(oa_ref[...], wb0_ref[...])
    mb = _dot(ob_ref[...], wb1_ref[...])
    merged = (gate_ref[:, :D_MODEL].astype(F32) * ma + gate_ref[:, D_MODEL:].astype(F32) * mb).astype(BF16)
    x1 = x_ref[...] + _dot(merged, wmix_ref[...])

    h2 = _rms(x1, gx_ref[...]).astype(BF16)
    qx = (_dot(h2, wq_ref[...]) * (X_HEAD_DIM ** -0.5)).astype(BF16)
    heads = []
    for h in range(X_HEADS):
        sl = slice(h * X_HEAD_DIM, (h + 1) * X_HEAD_DIM)
        s = _dot_nt(qx[:, sl], km_ref[0, :, sl])
        m = jnp.max(s, axis=-1, keepdims=True)
        pe = jnp.exp(s - m)
        l = jnp.sum(pe, axis=-1, keepdims=True)
        heads.append((_dot(pe.astype(BF16), vm_ref[0, :, sl]) * (1.0 / l)).astype(BF16))
    x2 = x1 + _dot(jnp.concatenate(heads, axis=1), wo_ref[...])
    x2_ref[...] = x2

    h3 = _rms(x2, gf_ref[...])

    h3_hi = h3.astype(BF16)
    h3_lo = (h3 - h3_hi.astype(F32)).astype(BF16)
    hw = _dot(h3_hi, wr_ref[...])
    logits = hw[:, :LANES] + hw[:, LANES:] + _dot(h3_lo, wr_ref[:, :LANES]) + br_ref[...]
    oh0, oh1, g0, g1 = _route(logits)
    for h in range(TOKEN_TILES_PER_STEP):
        rows = slice(h * ROW_TILE, (h + 1) * ROW_TILE)
        chunks = pl.ds(h * LOCAL_CHUNKS, LOCAL_CHUNKS)
        _sort_tile(oh0[rows], oh1[rows], g0[rows], g1[rows], h3_hi[rows], ltri_ref, utri_ref,
                   hs_ref.at[chunks], route_ref.at[pl.ds(h * ROW_TILE, ROW_TILE)], cnt_ref.at[h])


def _route(logits):
    lane = lax.broadcasted_iota(I32, logits.shape, 1).astype(F32)
    big = jnp.float32(LANES)
    gl = jnp.where(lane < N_GROUPS, logits, NEG)
    gmax = jnp.max(gl, axis=-1, keepdims=True)
    gidx = jnp.min(jnp.where(gl == gmax, lane, big), axis=-1, keepdims=True)
    g_w = 1.0 / jnp.sum(jnp.exp(gl - gmax), axis=-1, keepdims=True)
    lo = N_GROUPS + EXPERTS_PER_GROUP * gidx
    el = jnp.where((lane >= lo) & (lane < lo + EXPERTS_PER_GROUP), logits, NEG)
    v1 = jnp.max(el, axis=-1, keepdims=True)
    i1 = jnp.min(jnp.where(el == v1, lane, big), axis=-1, keepdims=True)
    el2 = jnp.where(lane == i1, NEG, el)
    v2 = jnp.max(el2, axis=-1, keepdims=True)
    i2 = jnp.min(jnp.where(el2 == v2, lane, big), axis=-1, keepdims=True)
    e21 = jnp.exp(v2 - v1)
    w1 = g_w / (1.0 + e21)

    def gate_cols(w):
        hi = w.astype(BF16).astype(F32)
        return jnp.where(lane == 0, hi, jnp.where(lane == 1, w - hi, 0.0)).astype(BF16)

    oh0 = jnp.where(lane == i1 - N_GROUPS, 1.0, 0.0)
    oh1 = jnp.where(lane == i2 - N_GROUPS, 1.0, 0.0)
    return oh0, oh1, gate_cols(w1), gate_cols(w1 * e21)


def _sort_tile(oh0, oh1, g0, g1, h3_hi, ltri_ref, utri_ref, hs_ref, route_ref, cnt_ref):
    lane = lax.broadcasted_iota(I32, oh0.shape, 1)
    oh = oh0 + oh1
    nch = jnp.floor((jnp.sum(oh, axis=0, keepdims=True) + (CHUNK_ROWS - 1)) * (1.0 / CHUNK_ROWS))
    nch8 = jnp.broadcast_to(nch, (8, LANES))
    start = _dot(nch8.astype(BF16), utri_ref[...])[0:1] * CHUNK_ROWS
    rank = _dot(ltri_ref[...], oh.astype(BF16))
    row = start + rank
    pos0 = jnp.sum(row * oh0, axis=-1, keepdims=True)
    pos1 = jnp.sum(row * oh1, axis=-1, keepdims=True)
    route = jnp.where(lane == 0, pos0, jnp.where(lane == 1, pos1, 0.0))
    route_t = jnp.transpose(route)
    r = lax.broadcasted_iota(I32, (LOCAL_ROWS, ROW_TILE), 0).astype(F32)
    p0 = jnp.where(r == route_t[0:1, :], 1.0, 0.0).astype(BF16)
    p1 = jnp.where(r == route_t[1:2, :], 1.0, 0.0).astype(BF16)
    sorted_rows = jnp.concatenate([_dot(p0 + p1, h3_hi), _dot(p0, g0) + _dot(p1, g1)], axis=1)
    hs_ref[...] = sorted_rows.astype(BF16).reshape(hs_ref.shape)
    route_ref[...] = route
    cnt_ref[...] = nch8


def _token(x, oa, ob, gates, wb0, wb1, wmix, gx, wq, km, vm, wo, gf, wr, br, batch):
    t, d = x.shape
    n = TOKEN_TILES_PER_STEP
    nb = t // batch // (n * ROW_TILE)
    nt = t // ROW_TILE
    ltri = jnp.asarray(np.tril(np.ones((ROW_TILE, ROW_TILE), np.float32), -1), BF16)
    utri = jnp.asarray(np.triu(np.ones((LANES, LANES), np.float32), 1), BF16)
    cur = lambda b, i: (b * nb + i, 0)
    cur3 = lambda b, i: (b * nb + i, 0, 0)
    blk = lambda w: pl.BlockSpec((n * ROW_TILE, w), cur)
    full = lambda a: pl.BlockSpec(a.shape, lambda b, i: (0,) * a.ndim)
    mem = pl.BlockSpec((1,) + km.shape[1:], lambda b, i: (b, 0, 0))
    return pl.pallas_call(
        _token_kernel,
        grid=(batch, nb),
        in_specs=[blk(d), blk(A_WIDTH), blk(B_VAL_WIDTH), blk(2 * d), full(wb0), full(wb1), full(wmix),
                  full(gx), full(wq), mem, mem, full(wo), full(gf), full(wr), full(br), full(ltri), full(utri)],
        out_specs=[blk(d), pl.BlockSpec((n * LOCAL_CHUNKS, CHUNK_ROWS, SORT_WIDTH), cur3),
                   blk(LANES), pl.BlockSpec((n, 8, LANES), cur3)],
        out_shape=[jax.ShapeDtypeStruct((t, d), F32),
                   jax.ShapeDtypeStruct((nt * LOCAL_CHUNKS, CHUNK_ROWS, SORT_WIDTH), BF16),
                   jax.ShapeDtypeStruct((t, LANES), F32),
                   jax.ShapeDtypeStruct((nt, 8, LANES), F32)],
        compiler_params=_params("arbitrary", "arbitrary"),
        name="token",
    )(x, oa, ob, gates, wb0, wb1, wmix, gx, wq, km, vm, wo, gf, wr, br, ltri, utri)


def _expert_kernel(layer, te_ref, nu_ref, nv_ref, ch_ref, first_ref, nxt_ref,
                   hs_hbm, wg_hbm, wu_hbm, wd_hbm, ys_hbm,
                   xbuf, ybuf, wg_st, wu_st, wd_st, wgu_bf, wd_bf, gsem, ssem, wsem):
    n_used = nu_ref[0]
    ring = GATHER_AHEAD + 1
    last_tile = te_ref.shape[0] - 1

    def weights(expert, s, start):
        for src, dst in ((wg_hbm, wg_st), (wu_hbm, wu_st), (wd_hbm, wd_st)):
            cp = pltpu.make_async_copy(src.at[layer * N_EXPERTS + expert], dst.at[s], wsem.at[s])
            cp.start() if start else cp.wait()

    def for_chunks(tile, fn):
        nv = nv_ref[tile]

        @pl.when(nv == TILE_CHUNKS)
        def _():
            for c in range(TILE_CHUNKS):
                fn(c)

        @pl.when(nv != TILE_CHUNKS)
        def _():
            def body(c, carry):
                fn(c)
                return carry

            lax.fori_loop(0, nv, body, 0)

    def gather(step, start):
        tile = jnp.minimum(step, last_tile)
        s = lax.rem(step, ring)
        for c in range(TILE_CHUNKS):
            cp = pltpu.make_async_copy(hs_hbm.at[ch_ref[tile * TILE_CHUNKS + c]], xbuf.at[s, c], gsem.at[s])
            cp.start() if start else cp.wait()

    def scatter(tile, s, start):
        def one(c):
            cp = pltpu.make_async_copy(ybuf.at[s, c], ys_hbm.at[ch_ref[tile * TILE_CHUNKS + c]], ssem.at[s])
            cp.start(priority=1) if start else cp.wait()

        for_chunks(tile, one)

    weights(te_ref[0], 0, True)
    for step in range(GATHER_AHEAD):
        gather(step, True)

    def tile_body(i, run):
        slot = lax.rem(i, SCATTER_RING)
        gather(i, False)

        @pl.when(i >= SCATTER_RING)
        def _():
            scatter(i - SCATTER_RING, slot, False)

        @pl.when(first_ref[i] == 1)
        def _():
            s = lax.rem(run, 2)
            weights(te_ref[i], s, False)

            @pl.when(nxt_ref[i] >= 0)
            def _():
                weights(nxt_ref[i], 1 - s, True)

            wgu_bf[:, :EXPERT_FF] = wg_st[s].astype(BF16)
            wgu_bf[:, EXPERT_FF:] = wu_st[s].astype(BF16)
            wd_bf[...] = wd_st[s].astype(BF16)

        xg = xbuf[lax.rem(i, ring)].reshape(EXPERT_ROWS, SORT_WIDTH)
        hgu = _dot(xg[:, :D_MODEL], wgu_bf[...])
        gather(i + GATHER_AHEAD, True)
        hg, hu = hgu[:, :EXPERT_FF], hgu[:, EXPERT_FF:]
        hid = (hg * jax.nn.sigmoid(hg) * hu).astype(BF16)
        g = xg[:, D_MODEL:].astype(F32)
        y = ((g[:, 0:1] + g[:, 1:2]) * _dot(hid, wd_bf[...])).astype(BF16)
        y = jnp.concatenate([y, jnp.zeros((EXPERT_ROWS, LANES), BF16)], axis=1)
        ybuf[slot] = y.reshape(TILE_CHUNKS, CHUNK_ROWS, SORT_WIDTH)
        scatter(i, slot, True)
        return run + first_ref[i]

    lax.fori_loop(0, n_used, tile_body, jnp.int32(0))

    last = n_used - 1
    for ahead in range(1, GATHER_AHEAD + 1):
        gather(last + ahead, False)
    for back in range(SCATTER_RING):
        @pl.when(last - back >= 0)
        def _():
            scatter(last - back, lax.rem(last - back, SCATTER_RING), False)


def _experts(hs, tile_expert, n_used, n_valid, chunks, run_first, run_next, wg, wu, wd, layer):
    anyspace = pl.BlockSpec(memory_space=pl.ANY)
    grid_spec = pltpu.PrefetchScalarGridSpec(
        num_scalar_prefetch=6,
        grid=(1,),
        in_specs=[anyspace] * 4,
        out_specs=anyspace,
        scratch_shapes=[pltpu.VMEM((GATHER_AHEAD + 1, TILE_CHUNKS, CHUNK_ROWS, SORT_WIDTH), BF16),
                        pltpu.VMEM((SCATTER_RING, TILE_CHUNKS, CHUNK_ROWS, SORT_WIDTH), BF16),
                        pltpu.VMEM((2, D_MODEL, EXPERT_FF), F32), pltpu.VMEM((2, D_MODEL, EXPERT_FF), F32),
                        pltpu.VMEM((2, EXPERT_FF, D_MODEL), F32),
                        pltpu.VMEM((D_MODEL, 2 * EXPERT_FF), BF16), pltpu.VMEM((EXPERT_FF, D_MODEL), BF16),
                        pltpu.SemaphoreType.DMA((GATHER_AHEAD + 1,)), pltpu.SemaphoreType.DMA((SCATTER_RING,)),
                        pltpu.SemaphoreType.DMA((2,))],
    )
    return pl.pallas_call(
        functools.partial(_expert_kernel, layer),
        grid_spec=grid_spec,
        out_shape=jax.ShapeDtypeStruct(hs.shape, BF16),
        input_output_aliases={6: 0},
        compiler_params=_params("arbitrary"),
        name="experts",
    )(tile_expert, n_used, n_valid, chunks, run_first, run_next, hs, wg, wu, wd)


def _combine_kernel(x_ref, route_ref, ys_ref, gfin_ref, o_ref):
    r = lax.broadcasted_iota(I32, (ROW_TILE, LOCAL_ROWS), 1).astype(F32)
    for t in range(COMBINE_TILES_PER_STEP):
        rows = slice(t * ROW_TILE, (t + 1) * ROW_TILE)
        pos = route_ref[rows, :]
        sel = jnp.where(r == pos[:, 0:1], 1.0, jnp.where(r == pos[:, 1:2], 1.0, 0.0)).astype(BF16)
        ys = ys_ref[t * LOCAL_CHUNKS:(t + 1) * LOCAL_CHUNKS].reshape(LOCAL_ROWS, D_MODEL)
        o_ref[rows, :] = _rms(x_ref[rows, :] + _dot(sel, ys), gfin_ref[...])


def _combine(x2, route, ys, gfin):
    t, d = x2.shape
    n = COMBINE_TILES_PER_STEP
    return pl.pallas_call(
        _combine_kernel,
        grid=(t // (n * ROW_TILE),),
        in_specs=[pl.BlockSpec((n * ROW_TILE, d), lambda i: (i, 0)),
                  pl.BlockSpec((n * ROW_TILE, LANES), lambda i: (i, 0)),
                  pl.BlockSpec((n * LOCAL_CHUNKS, CHUNK_ROWS, d), lambda i: (i, 0, 0)),
                  pl.BlockSpec((1, d), lambda i: (0, 0))],
        out_specs=pl.BlockSpec((n * ROW_TILE, d), lambda i: (i, 0)),
        out_shape=jax.ShapeDtypeStruct((t, d), F32),
        compiler_params=_params("arbitrary"),
        name="combine",
    )(x2, route, ys, gfin)


def _chunk_plan(nch, n_tiles):
    nt = nch.shape[0]
    local_start = jnp.cumsum(nch, axis=1) - nch
    cum = jnp.cumsum(nch, axis=0)
    total = cum[-1]
    tiles = (total + TILE_CHUNKS - 1) // TILE_CHUNKS
    tile_end = jnp.cumsum(tiles)
    n_used = tile_end[-1:]
    tile_ids = jnp.arange(n_tiles, dtype=I32)
    tile_expert = jnp.minimum(jnp.sum((tile_end[None, :] <= tile_ids[:, None]).astype(I32), axis=1),
                              N_EXPERTS - 1)
    sel = (tile_expert[:, None] == jnp.arange(N_EXPERTS, dtype=I32)[None, :]).astype(I32)
    pick = lambda table: jnp.sum(sel[:, :, None] * table.T[None, :, :], axis=1)
    first_tile = jnp.sum(sel * (tile_end - tiles)[None, :], axis=1)
    slot = (tile_ids - first_tile)[:, None] * TILE_CHUNKS + jnp.arange(TILE_CHUNKS, dtype=I32)[None, :]
    valid = (slot < jnp.sum(sel * total[None, :], axis=1)[:, None]) & (tile_ids < n_used)[:, None]
    src_tile = jnp.sum((pick(cum)[:, None, :] <= slot[:, :, None]).astype(I32), axis=2)
    src_tile = jnp.minimum(src_tile, nt - 1)
    at = (src_tile[:, :, None] == jnp.arange(nt, dtype=I32)[None, None, :]).astype(I32)
    before = jnp.sum(at * pick(cum - nch)[:, None, :], axis=2)
    start = jnp.sum(at * pick(local_start)[:, None, :], axis=2)
    chunk = jnp.where(valid, src_tile * LOCAL_CHUNKS + start + slot - before, LOCAL_CHUNKS - 1)
    used = tile_ids < n_used
    run_first = ((tile_ids == first_tile) & used).astype(I32)
    run_end = jnp.sum(sel * tile_end[None, :], axis=1)
    next_expert = jnp.sum((run_end[:, None] == tile_ids[None, :]).astype(I32) * tile_expert[None, :], axis=1)
    run_next = jnp.where(used & (run_end < n_used), next_expert, -1)
    return tile_expert, n_used, jnp.sum(valid.astype(I32), axis=1), chunk.reshape(-1), run_first, run_next


def kernel(x, mem, norm_mix_g, w_in, rel_bias, gla_w_alpha, gla_b_alpha, gla_norm_g, w_branch, w_mix_out, norm_x_g, mem_norm_g, w_xq, w_xkv, w_xo, norm_ffn_g, w_group_router, b_group_router, w_expert_router, b_expert_router, w_exp_gate, w_exp_up, w_exp_down, final_norm_g):
    batch, seq, d = x.shape
    depth = w_in.shape[0]
    t = batch * seq
    step_tiles = max(TOKEN_TILES_PER_STEP, GLA_TILES_PER_STEP, BAND_TILES_PER_STEP, COMBINE_TILES_PER_STEP)
    assert d == D_MODEL and seq % (step_tiles * ROW_TILE) == 0 and seq % INPROJ_ROWS == 0
    nt = t // ROW_TILE
    n_tiles = nt * LOCAL_CHUNKS // TILE_CHUNKS + N_EXPERTS

    xf = x.reshape(t, d)
    km_all, vm_all = _memkv(mem, mem_norm_g, w_xkv)
    row = lambda a: a.reshape(1, -1).astype(F32)

    moe = None
    for l in range(depth):
        res = _inproj(xf, row(norm_mix_g[l]), w_in, gla_w_alpha, row(gla_b_alpha[l]), l, moe)
        if moe is not None:
            xf, res = res[0], res[1:]
        aq, ak, av, bq, bk, bv, lga, br, gates = res

        oa = _band_attention(aq, ak, av, _band_bias_vector(rel_bias[l]), batch)
        ob = _gla(bq, bk, bv, lga, br, row(gla_norm_g[l]), batch)

        wr = jnp.pad(jnp.concatenate([w_group_router[l], w_expert_router[l]], axis=1).astype(F32),
                     ((0, 0), (0, LANES - N_GROUPS - N_EXPERTS)))
        wr_hi = wr.astype(BF16)
        wr = jnp.concatenate([wr_hi, (wr - wr_hi.astype(F32)).astype(BF16)], axis=1)
        brt = jnp.pad(jnp.concatenate([b_group_router[l], b_expert_router[l]]).astype(F32),
                      (0, LANES - N_GROUPS - N_EXPERTS)).reshape(1, LANES)
        x2, hs, route, cnt = _token(
            xf, oa, ob, gates, w_branch[l, 0].astype(BF16), w_branch[l, 1].astype(BF16),
            w_mix_out[l].astype(BF16), row(norm_x_g[l]), w_xq[l].astype(BF16), km_all[l], vm_all[l],
            w_xo[l].astype(BF16), row(norm_ffn_g[l]), wr, brt, batch)

        plan = _chunk_plan(cnt[:, 0, :N_EXPERTS].astype(I32), n_tiles)
        e3 = lambda w: w.reshape((depth * N_EXPERTS,) + w.shape[3:])
        ys = _experts(hs, *plan, e3(w_exp_gate), e3(w_exp_up), e3(w_exp_down), l)
        xf, moe = x2, (route, ys)

    return _combine(x2, route, ys, row(final_norm_g)).reshape(batch, seq, d)
```
